```python
import jax, jax.numpy as jnp
from jax import lax
import numpy as np


D_MODEL = 1024
BATCH = 16
SEQ = 2048
DEPTH = 4

GRID_W = 64
HEAD_DIM = 64
N_Q_HEADS = 8
N_KV_HEADS = 2
Q_PER_KV = N_Q_HEADS // N_KV_HEADS
ATTN_WIDTH = N_Q_HEADS * HEAD_DIM
KV_WIDTH = N_KV_HEADS * HEAD_DIM
ROPE_THETA = 10000.0
Q_BLOCK = 128
CONV_WIDTH = D_MODEL // 2
POOL_WINDOWS = (2, 4, 8, 16)
N_POOL_GROUPS = 4
POOL_WIDTH = D_MODEL // 2
POOL_GROUP = POOL_WIDTH // N_POOL_GROUPS
SGU_WIDTH = D_MODEL // 2
N_SGU_GROUPS = 4
SGU_GROUP = SGU_WIDTH // N_SGU_GROUPS
SGU_CHUNK = 128
D_FF = 2816
EPS = 1e-6
EVEN_IN = 3 * CONV_WIDTH + ATTN_WIDTH + 2 * KV_WIDTH
EVEN_SPLITS = (CONV_WIDTH, 2 * CONV_WIDTH, 3 * CONV_WIDTH,
               3 * CONV_WIDTH + ATTN_WIDTH, 3 * CONV_WIDTH + ATTN_WIDTH + KV_WIDTH)
EVEN_MIX = CONV_WIDTH + ATTN_WIDTH
ODD_IN = POOL_WIDTH + 2 * SGU_WIDTH
ODD_SPLITS = (POOL_WIDTH, POOL_WIDTH + SGU_WIDTH)
ODD_MIX = POOL_WIDTH + SGU_WIDTH

kernel_name = 'hybrid_conv_gqa_pool_sgu_macaron_encoder'


def rms_norm(x, g):
    xf = x.astype(jnp.float32)
    y = xf * lax.rsqrt(jnp.mean(xf * xf, axis=-1, keepdims=True) + EPS)
    return (y * g.astype(jnp.float32)).astype(x.dtype)


def swiglu(x, w_in, w_out):
    g, u = jnp.split(x @ w_in, 2, axis=-1)
    return (jax.nn.silu(g) * u) @ w_out


def axial_rope_tables(seq):
    rows = seq // GRID_W
    r_idx, c_idx = jnp.meshgrid(jnp.arange(rows), jnp.arange(GRID_W), indexing='ij')
    r_idx = r_idx.reshape(-1).astype(jnp.float32)
    c_idx = c_idx.reshape(-1).astype(jnp.float32)
    n_freq = HEAD_DIM // 4
    inv = ROPE_THETA ** (-jnp.arange(n_freq, dtype=jnp.float32) / n_freq)
    ang = jnp.concatenate([r_idx[:, None] * inv, c_idx[:, None] * inv], axis=-1)
    return jnp.cos(ang), jnp.sin(ang)


def apply_rope(x, cos, sin):
    b, s, h, d = x.shape
    xf = x.astype(jnp.float32).reshape(b, s, h, d // 2, 2)
    x0, x1 = xf[..., 0], xf[..., 1]
    c = cos[None, :, None, :]
    sn = sin[None, :, None, :]
    out = jnp.stack([x0 * c - x1 * sn, x0 * sn + x1 * c], axis=-1)
    return out.reshape(b, s, h, d).astype(x.dtype)


def blocked_gqa(q, k, v):
    b, s, _, _ = q.shape
    nblk = s // Q_BLOCK
    qb = q.reshape(b, nblk, Q_BLOCK, N_KV_HEADS, Q_PER_KV, HEAD_DIM).transpose(1, 0, 2, 3, 4, 5)
    scale = HEAD_DIM ** -0.5

    def one_block(qi):
        sc = jnp.einsum('bqkgd,bskd->bkgqs', qi, k).astype(jnp.float32) * scale
        p = jax.nn.softmax(sc, axis=-1).astype(v.dtype)
        return jnp.einsum('bkgqs,bskd->bqkgd', p, v)

    o = lax.map(one_block, qb)
    return o.transpose(1, 0, 2, 3, 4, 5).reshape(b, s, ATTN_WIDTH)


def short_conv(h, w):
    hp = jnp.pad(h, ((0, 0), (1, 1), (0, 0)))
    return hp[:, :-2] * w[0] + hp[:, 1:-1] * w[1] + hp[:, 2:] * w[2]


def even_mixer(h, w_in, conv_w, q_g, k_g, w_out, cos, sin):
    b, s, _ = h.shape
    gate_b, gate_c, hc, q, k, v = jnp.split(h @ w_in, list(EVEN_SPLITS), axis=-1)
    a_out = gate_b * short_conv(gate_c * hc, conv_w)
    q = apply_rope(rms_norm(q.reshape(b, s, N_Q_HEADS, HEAD_DIM), q_g), cos, sin)
    k = apply_rope(rms_norm(k.reshape(b, s, N_KV_HEADS, HEAD_DIM), k_g), cos, sin)
    v = v.reshape(b, s, N_KV_HEADS, HEAD_DIM)
    b_out = blocked_gqa(q, k, v)
    return jnp.concatenate([a_out, b_out], axis=-1) @ w_out


def multiscale_pool(p):
    b, s, _ = p.shape
    pf = p.astype(jnp.float32)
    cs = jnp.concatenate([jnp.zeros((b, 1, POOL_WIDTH), jnp.float32), jnp.cumsum(pf, axis=1)], axis=1)
    t = jnp.arange(s)
    outs = []
    for gi, w in enumerate(POOL_WINDOWS):
        r = w // 2
        lo = jnp.maximum(t - r, 0)
        hi = jnp.minimum(t + r, s - 1)
        sl = slice(gi * POOL_GROUP, (gi + 1) * POOL_GROUP)
        csg = cs[:, :, sl]
        win = csg[:, hi + 1] - csg[:, lo]
        cnt = (hi - lo + 1).astype(jnp.float32)[None, :, None]
        outs.append(win / cnt - pf[:, :, sl])
    return jnp.concatenate(outs, axis=-1).astype(p.dtype)


def chunked_sgu(u, v, norm_g, w_s, b_s):
    b, s, _ = u.shape
    v = rms_norm(v, norm_g)
    n = s // SGU_CHUNK
    vc = v.reshape(b, n, SGU_CHUNK, N_SGU_GROUPS, SGU_GROUP)
    mixed = jnp.einsum('gpq,bnqgc->bnpgc', w_s, vc) + b_s.T[None, None, :, :, None]
    return u * mixed.reshape(b, s, SGU_WIDTH)


def odd_mixer(h, w_in, pool_w, pool_scale, sgu_norm, sgu_w, sgu_b, w_out):
    b, s, _ = h.shape
    p, u, v = jnp.split(h @ w_in, list(ODD_SPLITS), axis=-1)
    pooled = multiscale_pool(p).reshape(b, s, N_POOL_GROUPS, POOL_GROUP)
    c_out = jnp.einsum('bsgc,gcd->bsgd', pooled, pool_w).reshape(b, s, POOL_WIDTH) * pool_scale
    d_out = chunked_sgu(jax.nn.gelu(u), jax.nn.gelu(v), sgu_norm, sgu_w, sgu_b)
    return jnp.concatenate([c_out, d_out], axis=-1) @ w_out


def _fwd_setup_inputs(seed: int = 0) -> dict:
    key = jax.random.key(seed)
    ks = jax.random.split(key, 24)
    n_even = (DEPTH + 1) // 2
    n_odd = DEPTH // 2
    f32 = jnp.float32

    def nrm(k, shape, scale):
        return jax.random.normal(k, shape, f32) * scale

    def gain(k, shape):
        return 1.0 + 0.02 * jax.random.normal(k, shape, f32)

    return {
        'x': jax.random.normal(ks[0], (BATCH, SEQ, D_MODEL), f32),
        'ffn1_norm': gain(ks[1], (DEPTH, D_MODEL)),
        'ffn1_w_in': nrm(ks[2], (DEPTH, D_MODEL, 2 * D_FF), D_MODEL ** -0.5),
        'ffn1_w_out': nrm(ks[3], (DEPTH, D_FF, D_MODEL), D_FF ** -0.5),
        'mix_norm': gain(ks[4], (DEPTH, D_MODEL)),
        'ffn2_norm': gain(ks[5], (DEPTH, D_MODEL)),
        'ffn2_w_in': nrm(ks[6], (DEPTH, D_MODEL, 2 * D_FF), D_MODEL ** -0.5),
        'ffn2_w_out': nrm(ks[7], (DEPTH, D_FF, D_MODEL), D_FF ** -0.5),
        'ev_w_in': nrm(ks[8], (n_even, D_MODEL, EVEN_IN), D_MODEL ** -0.5),
        'ev_conv_w': nrm(ks[9], (n_even, 3, CONV_WIDTH), 3 ** -0.5),
        'ev_q_norm': gain(ks[10], (n_even, HEAD_DIM)),
        'ev_k_norm': gain(ks[11], (n_even, HEAD_DIM)),
        'ev_w_out': nrm(ks[12], (n_even, EVEN_MIX, D_MODEL), EVEN_MIX ** -0.5),
        'od_w_in': nrm(ks[13], (n_odd, D_MODEL, ODD_IN), D_MODEL ** -0.5),
        'od_pool_w': nrm(ks[14], (n_odd, N_POOL_GROUPS, POOL_GROUP, POOL_GROUP), POOL_GROUP ** -0.5),
        'od_pool_scale': 1.0 + 0.1 * jax.random.normal(ks[15], (n_odd, POOL_WIDTH), f32),
        'od_sgu_norm': gain(ks[16], (n_odd, SGU_WIDTH)),
        'od_sgu_w': nrm(ks[17], (n_odd, N_SGU_GROUPS, SGU_CHUNK, SGU_CHUNK), SGU_CHUNK ** -0.5),
        'od_sgu_b': 1.0 + 0.01 * jax.random.normal(ks[18], (n_odd, N_SGU_GROUPS, SGU_CHUNK), f32),
        'od_w_out': nrm(ks[19], (n_odd, ODD_MIX, D_MODEL), ODD_MIX ** -0.5),
        'final_norm': gain(ks[20], (D_MODEL,)),
    }


def _fwd_reference(x, ffn1_norm, ffn1_w_in, ffn1_w_out, mix_norm, ffn2_norm, ffn2_w_in, ffn2_w_out,
              ev_w_in, ev_conv_w, ev_q_norm, ev_k_norm, ev_w_out,
              od_w_in, od_pool_w, od_pool_scale, od_sgu_norm, od_sgu_w, od_sgu_b, od_w_out,
              final_norm):
    s = x.shape[1]
    cos, sin = axial_rope_tables(s)
    for layer in range(DEPTH):
        x = x + 0.5 * swiglu(rms_norm(x, ffn1_norm[layer]), ffn1_w_in[layer], ffn1_w_out[layer])
        h = rms_norm(x, mix_norm[layer])
        j = layer // 2
        if layer % 2 == 0:
            x = x + even_mixer(h, ev_w_in[j], ev_conv_w[j], ev_q_norm[j], ev_k_norm[j], ev_w_out[j], cos, sin)
        else:
            x = x + odd_mixer(h, od_w_in[j], od_pool_w[j], od_pool_scale[j], od_sgu_norm[j],
                              od_sgu_w[j], od_sgu_b[j], od_w_out[j])
        x = x + 0.5 * swiglu(rms_norm(x, ffn2_norm[layer]), ffn2_w_in[layer], ffn2_w_out[layer])
    return rms_norm(x, final_norm)


import jax as _jax
import jax.numpy as _jnp

TWIN_FORMAT = 'train_step'
FWD_PARAMS = ['x', 'ffn1_norm', 'ffn1_w_in', 'ffn1_w_out', 'mix_norm', 'ffn2_norm', 'ffn2_w_in', 'ffn2_w_out', 'ev_w_in', 'ev_conv_w', 'ev_q_norm', 'ev_k_norm', 'ev_w_out', 'od_w_in', 'od_pool_w', 'od_pool_scale', 'od_sgu_norm', 'od_sgu_w', 'od_sgu_b', 'od_w_out', 'final_norm']
TWIN_WEIGHTS = ['ffn1_norm', 'ffn1_w_in', 'ffn1_w_out', 'mix_norm', 'ffn2_norm', 'ffn2_w_in', 'ffn2_w_out', 'ev_w_in', 'ev_conv_w', 'ev_q_norm', 'ev_k_norm', 'ev_w_out', 'od_w_in', 'od_pool_w', 'od_pool_scale', 'od_sgu_norm', 'od_sgu_w', 'od_sgu_b', 'od_w_out', 'final_norm']
TWIN_DIFF_INPUT = 'x'
TWIN_INPUTS = ['x', 'ffn1_norm', 'ffn1_w_in', 'ffn1_w_out', 'mix_norm', 'ffn2_norm', 'ffn2_w_in', 'ffn2_w_out', 'ev_w_in', 'ev_conv_w', 'ev_q_norm', 'ev_k_norm', 'ev_w_out', 'od_w_in', 'od_pool_w', 'od_pool_scale', 'od_sgu_norm', 'od_sgu_w', 'od_sgu_b', 'od_w_out', 'final_norm', 'loss_target', 'm_ffn1_norm', 'm_ffn1_w_in', 'm_ffn1_w_out', 'm_mix_norm', 'm_ffn2_norm', 'm_ffn2_w_in', 'm_ffn2_w_out', 'm_ev_w_in', 'm_ev_conv_w', 'm_ev_q_norm', 'm_ev_k_norm', 'm_ev_w_out', 'm_od_w_in', 'm_od_pool_w', 'm_od_pool_scale', 'm_od_sgu_norm', 'm_od_sgu_w', 'm_od_sgu_b', 'm_od_w_out', 'm_final_norm', 'v_ffn1_norm', 'v_ffn1_w_in', 'v_ffn1_w_out', 'v_mix_norm', 'v_ffn2_norm', 'v_ffn2_w_in', 'v_ffn2_w_out', 'v_ev_w_in', 'v_ev_conv_w', 'v_ev_q_norm', 'v_ev_k_norm', 'v_ev_w_out', 'v_od_w_in', 'v_od_pool_w', 'v_od_pool_scale', 'v_od_sgu_norm', 'v_od_sgu_w', 'v_od_sgu_b', 'v_od_w_out', 'v_final_norm']
TWIN_OUTPUTS = ['loss', 'grad_x', 'grad_ffn1_norm', 'grad_ffn1_w_in', 'grad_ffn1_w_out', 'grad_mix_norm', 'grad_ffn2_norm', 'grad_ffn2_w_in', 'grad_ffn2_w_out', 'grad_ev_w_in', 'grad_ev_conv_w', 'grad_ev_q_norm', 'grad_ev_k_norm', 'grad_ev_w_out', 'grad_od_w_in', 'grad_od_pool_w', 'grad_od_pool_scale', 'grad_od_sgu_norm', 'grad_od_sgu_w', 'grad_od_sgu_b', 'grad_od_w_out', 'grad_final_norm', 'delta_ffn1_norm', 'delta_ffn1_w_in', 'delta_ffn1_w_out', 'delta_mix_norm', 'delta_ffn2_norm', 'delta_ffn2_w_in', 'delta_ffn2_w_out', 'delta_ev_w_in', 'delta_ev_conv_w', 'delta_ev_q_norm', 'delta_ev_k_norm', 'delta_ev_w_out', 'delta_od_w_in', 'delta_od_pool_w', 'delta_od_pool_scale', 'delta_od_sgu_norm', 'delta_od_sgu_w', 'delta_od_sgu_b', 'delta_od_w_out', 'delta_final_norm', 'new_m_ffn1_norm', 'new_m_ffn1_w_in', 'new_m_ffn1_w_out', 'new_m_mix_norm', 'new_m_ffn2_norm', 'new_m_ffn2_w_in', 'new_m_ffn2_w_out', 'new_m_ev_w_in', 'new_m_ev_conv_w', 'new_m_ev_q_norm', 'new_m_ev_k_norm', 'new_m_ev_w_out', 'new_m_od_w_in', 'new_m_od_pool_w', 'new_m_od_pool_scale', 'new_m_od_sgu_norm', 'new_m_od_sgu_w', 'new_m_od_sgu_b', 'new_m_od_w_out', 'new_m_final_norm', 'new_v_ffn1_norm', 'new_v_ffn1_w_in', 'new_v_ffn1_w_out', 'new_v_mix_norm', 'new_v_ffn2_norm', 'new_v_ffn2_w_in', 'new_v_ffn2_w_out', 'new_v_ev_w_in', 'new_v_ev_conv_w', 'new_v_ev_q_norm', 'new_v_ev_k_norm', 'new_v_ev_w_out', 'new_v_od_w_in', 'new_v_od_pool_w', 'new_v_od_pool_scale', 'new_v_od_sgu_norm', 'new_v_od_sgu_w', 'new_v_od_sgu_b', 'new_v_od_w_out', 'new_v_final_norm']
TWIN_LEAF_KINDS = {'loss': 'loss', 'grad_x': 'grad_x', 'grad_ffn1_norm': 'grad_w', 'grad_ffn1_w_in': 'grad_w', 'grad_ffn1_w_out': 'grad_w', 'grad_mix_norm': 'grad_w', 'grad_ffn2_norm': 'grad_w', 'grad_ffn2_w_in': 'grad_w', 'grad_ffn2_w_out': 'grad_w', 'grad_ev_w_in': 'grad_w', 'grad_ev_conv_w': 'grad_w', 'grad_ev_q_norm': 'grad_w', 'grad_ev_k_norm': 'grad_w', 'grad_ev_w_out': 'grad_w', 'grad_od_w_in': 'grad_w', 'grad_od_pool_w': 'grad_w', 'grad_od_pool_scale': 'grad_w', 'grad_od_sgu_norm': 'grad_w', 'grad_od_sgu_w': 'grad_w', 'grad_od_sgu_b': 'grad_w', 'grad_od_w_out': 'grad_w', 'grad_final_norm': 'grad_w', 'delta_ffn1_norm': 'delta_w', 'delta_ffn1_w_in': 'delta_w', 'delta_ffn1_w_out': 'delta_w', 'delta_mix_norm': 'delta_w', 'delta_ffn2_norm': 'delta_w', 'delta_ffn2_w_in': 'delta_w', 'delta_ffn2_w_out': 'delta_w', 'delta_ev_w_in': 'delta_w', 'delta_ev_conv_w': 'delta_w', 'delta_ev_q_norm': 'delta_w', 'delta_ev_k_norm': 'delta_w', 'delta_ev_w_out': 'delta_w', 'delta_od_w_in': 'delta_w', 'delta_od_pool_w': 'delta_w', 'delta_od_pool_scale': 'delta_w', 'delta_od_sgu_norm': 'delta_w', 'delta_od_sgu_w': 'delta_w', 'delta_od_sgu_b': 'delta_w', 'delta_od_w_out': 'delta_w', 'delta_final_norm': 'delta_w', 'new_m_ffn1_norm': 'new_m', 'new_m_ffn1_w_in': 'new_m', 'new_m_ffn1_w_out': 'new_m', 'new_m_mix_norm': 'new_m', 'new_m_ffn2_norm': 'new_m', 'new_m_ffn2_w_in': 'new_m', 'new_m_ffn2_w_out': 'new_m', 'new_m_ev_w_in': 'new_m', 'new_m_ev_conv_w': 'new_m', 'new_m_ev_q_norm': 'new_m', 'new_m_ev_k_norm': 'new_m', 'new_m_ev_w_out': 'new_m', 'new_m_od_w_in': 'new_m', 'new_m_od_pool_w': 'new_m', 'new_m_od_pool_scale': 'new_m', 'new_m_od_sgu_norm': 'new_m', 'new_m_od_sgu_w': 'new_m', 'new_m_od_sgu_b': 'new_m', 'new_m_od_w_out': 'new_m', 'new_m_final_norm': 'new_m', 'new_v_ffn1_norm': 'new_v', 'new_v_ffn1_w_in': 'new_v', 'new_v_ffn1_w_out': 'new_v', 'new_v_mix_norm': 'new_v', 'new_v_ffn2_norm': 'new_v', 'new_v_ffn2_w_in': 'new_v', 'new_v_ffn2_w_out': 'new_v', 'new_v_ev_w_in': 'new_v', 'new_v_ev_conv_w': 'new_v', 'new_v_ev_q_norm': 'new_v', 'new_v_ev_k_norm': 'new_v', 'new_v_ev_w_out': 'new_v', 'new_v_od_w_in': 'new_v', 'new_v_od_pool_w': 'new_v', 'new_v_od_pool_scale': 'new_v', 'new_v_od_sgu_norm': 'new_v', 'new_v_od_sgu_w': 'new_v', 'new_v_od_sgu_b': 'new_v', 'new_v_od_w_out': 'new_v', 'new_v_final_norm': 'new_v'}


def _forward(args):
    return _fwd_reference(*[args[k] for k in FWD_PARAMS])


def _output_shape():
    out = _jax.eval_shape(lambda: _forward(_fwd_setup_inputs(0)))
    return out.shape, out.dtype

N_MICROBATCH = 1
ADAM_LR = 0.001
ADAM_B1 = 0.9
ADAM_B2 = 0.999
ADAM_EPS = 1e-08
ADAM_WD = 0.01
ADAM_STEP = 10
PER_EXAMPLE_BATCH_AXIS = {'x': 0, 'loss_target': 0}
SHARED_INPUTS = []
_WEIGHT_DTYPES = {'ffn1_norm': _jnp.float32, 'ffn1_w_in': _jnp.float32, 'ffn1_w_out': _jnp.float32, 'mix_norm': _jnp.float32, 'ffn2_norm': _jnp.float32, 'ffn2_w_in': _jnp.float32, 'ffn2_w_out': _jnp.float32, 'ev_w_in': _jnp.float32, 'ev_conv_w': _jnp.float32, 'ev_q_norm': _jnp.float32, 'ev_k_norm': _jnp.float32, 'ev_w_out': _jnp.float32, 'od_w_in': _jnp.float32, 'od_pool_w': _jnp.float32, 'od_pool_scale': _jnp.float32, 'od_sgu_norm': _jnp.float32, 'od_sgu_w': _jnp.float32, 'od_sgu_b': _jnp.float32, 'od_w_out': _jnp.float32, 'final_norm': _jnp.float32}
MOMENT_SCALE = {'ffn1_norm': 8.256088e-02, 'ffn1_w_in': 3.557366e-02, 'ffn1_w_out': 5.798185e-02, 'mix_norm': 1.607372e-01, 'ffn2_norm': 5.647253e-02, 'ffn2_w_in': 2.416271e-02, 'ffn2_w_out': 3.938405e-02, 'ev_w_in': 1.303088e-01, 'ev_conv_w': 1.560698e-01, 'ev_q_norm': 4.086447e-02, 'ev_k_norm': 4.225186e-02, 'ev_w_out': 1.119124e-01, 'od_w_in': 9.868089e-02, 'od_pool_w': 1.058943e-01, 'od_pool_scale': 1.123917e-01, 'od_sgu_norm': 8.228657e-02, 'od_sgu_w': 7.508626e-02, 'od_sgu_b': 7.521638e-02, 'od_w_out': 1.060333e-01, 'final_norm': 3.208057e+01}


def _to_microbatches(a, axis):
    t = _jnp.moveaxis(a, axis, 0)
    t = t.reshape((N_MICROBATCH, t.shape[0] // N_MICROBATCH) + t.shape[1:])
    return _jnp.moveaxis(t, 1, axis + 1)


def setup_inputs(seed: int = 0) -> dict:
    inp = _fwd_setup_inputs(seed)
    key = _jax.random.fold_in(_jax.random.key(seed), 7919)
    shape, _ = _output_shape()
    out = dict(inp)
    out["loss_target"] = _jax.random.normal(_jax.random.fold_in(key, 0), shape, _jnp.float32)
    for i, name in enumerate(TWIN_WEIGHTS):
        w = inp[name].astype(_jnp.float32)
        if MOMENT_SCALE is None:
            s = _jnp.sqrt(_jnp.mean(_jnp.square(w)) + 1e-30)
        else:
            s = MOMENT_SCALE[name]
        km, kv = _jax.random.split(_jax.random.fold_in(key, i + 1))
        out[name] = w
        out["m_" + name] = s * _jax.random.normal(km, w.shape, _jnp.float32)
        out["v_" + name] = (s * s) * _jax.random.uniform(kv, w.shape, _jnp.float32, 0.5, 1.5)
    if N_MICROBATCH > 1:
        for name, axis in PER_EXAMPLE_BATCH_AXIS.items():
            out[name] = _to_microbatches(out[name], axis)
    return {'x': out['x'], 'ffn1_norm': out['ffn1_norm'], 'ffn1_w_in': out['ffn1_w_in'], 'ffn1_w_out': out['ffn1_w_out'], 'mix_norm': out['mix_norm'], 'ffn2_norm': out['ffn2_norm'], 'ffn2_w_in': out['ffn2_w_in'], 'ffn2_w_out': out['ffn2_w_out'], 'ev_w_in': out['ev_w_in'], 'ev_conv_w': out['ev_conv_w'], 'ev_q_norm': out['ev_q_norm'], 'ev_k_norm': out['ev_k_norm'], 'ev_w_out': out['ev_w_out'], 'od_w_in': out['od_w_in'], 'od_pool_w': out['od_pool_w'], 'od_pool_scale': out['od_pool_scale'], 'od_sgu_norm': out['od_sgu_norm'], 'od_sgu_w': out['od_sgu_w'], 'od_sgu_b': out['od_sgu_b'], 'od_w_out': out['od_w_out'], 'final_norm': out['final_norm'], 'loss_target': out['loss_target'], 'm_ffn1_norm': out['m_ffn1_norm'], 'm_ffn1_w_in': out['m_ffn1_w_in'], 'm_ffn1_w_out': out['m_ffn1_w_out'], 'm_mix_norm': out['m_mix_norm'], 'm_ffn2_norm': out['m_ffn2_norm'], 'm_ffn2_w_in': out['m_ffn2_w_in'], 'm_ffn2_w_out': out['m_ffn2_w_out'], 'm_ev_w_in': out['m_ev_w_in'], 'm_ev_conv_w': out['m_ev_conv_w'], 'm_ev_q_norm': out['m_ev_q_norm'], 'm_ev_k_norm': out['m_ev_k_norm'], 'm_ev_w_out': out['m_ev_w_out'], 'm_od_w_in': out['m_od_w_in'], 'm_od_pool_w': out['m_od_pool_w'], 'm_od_pool_scale': out['m_od_pool_scale'], 'm_od_sgu_norm': out['m_od_sgu_norm'], 'm_od_sgu_w': out['m_od_sgu_w'], 'm_od_sgu_b': out['m_od_sgu_b'], 'm_od_w_out': out['m_od_w_out'], 'm_final_norm': out['m_final_norm'], 'v_ffn1_norm': out['v_ffn1_norm'], 'v_ffn1_w_in': out['v_ffn1_w_in'], 'v_ffn1_w_out': out['v_ffn1_w_out'], 'v_mix_norm': out['v_mix_norm'], 'v_ffn2_norm': out['v_ffn2_norm'], 'v_ffn2_w_in': out['v_ffn2_w_in'], 'v_ffn2_w_out': out['v_ffn2_w_out'], 'v_ev_w_in': out['v_ev_w_in'], 'v_ev_conv_w': out['v_ev_conv_w'], 'v_ev_q_norm': out['v_ev_q_norm'], 'v_ev_k_norm': out['v_ev_k_norm'], 'v_ev_w_out': out['v_ev_w_out'], 'v_od_w_in': out['v_od_w_in'], 'v_od_pool_w': out['v_od_pool_w'], 'v_od_pool_scale': out['v_od_pool_scale'], 'v_od_sgu_norm': out['v_od_sgu_norm'], 'v_od_sgu_w': out['v_od_sgu_w'], 'v_od_sgu_b': out['v_od_sgu_b'], 'v_od_w_out': out['v_od_w_out'], 'v_final_norm': out['v_final_norm']}


def _loss(weights, diff, rest, loss_target):
    with _jax.named_scope("forward"):
        args = {**rest, TWIN_DIFF_INPUT: diff, **{k: w.astype(_WEIGHT_DTYPES[k]) for k, w in weights.items()}}
        y = _forward(args)
    with _jax.named_scope("loss_head"):
        err = _jnp.square(y.astype(_jnp.float32) - loss_target)
        return 0.5 * _jnp.sum(_jnp.mean(err, axis=-1)) if err.ndim else 0.5 * err


def _adamw(w, g, m, v):
    m = ADAM_B1 * m + (1.0 - ADAM_B1) * g
    v = ADAM_B2 * v + (1.0 - ADAM_B2) * _jnp.square(g)
    m_hat = m / (1.0 - ADAM_B1 ** ADAM_STEP)
    v_hat = v / (1.0 - ADAM_B2 ** ADAM_STEP)
    delta = -ADAM_LR * (m_hat / (_jnp.sqrt(v_hat) + ADAM_EPS) + ADAM_WD * w)
    return delta, m, v


def reference(x, ffn1_norm, ffn1_w_in, ffn1_w_out, mix_norm, ffn2_norm, ffn2_w_in, ffn2_w_out, ev_w_in, ev_conv_w, ev_q_norm, ev_k_norm, ev_w_out, od_w_in, od_pool_w, od_pool_scale, od_sgu_norm, od_sgu_w, od_sgu_b, od_w_out, final_norm, loss_target, m_ffn1_norm, m_ffn1_w_in, m_ffn1_w_out, m_mix_norm, m_ffn2_norm, m_ffn2_w_in, m_ffn2_w_out, m_ev_w_in, m_ev_conv_w, m_ev_q_norm, m_ev_k_norm, m_ev_w_out, m_od_w_in, m_od_pool_w, m_od_pool_scale, m_od_sgu_norm, m_od_sgu_w, m_od_sgu_b, m_od_w_out, m_final_norm, v_ffn1_norm, v_ffn1_w_in, v_ffn1_w_out, v_mix_norm, v_ffn2_norm, v_ffn2_w_in, v_ffn2_w_out, v_ev_w_in, v_ev_conv_w, v_ev_q_norm, v_ev_k_norm, v_ev_w_out, v_od_w_in, v_od_pool_w, v_od_pool_scale, v_od_sgu_norm, v_od_sgu_w, v_od_sgu_b, v_od_w_out, v_final_norm):
    given = dict(x=x, ffn1_norm=ffn1_norm, ffn1_w_in=ffn1_w_in, ffn1_w_out=ffn1_w_out, mix_norm=mix_norm, ffn2_norm=ffn2_norm, ffn2_w_in=ffn2_w_in, ffn2_w_out=ffn2_w_out, ev_w_in=ev_w_in, ev_conv_w=ev_conv_w, ev_q_norm=ev_q_norm, ev_k_norm=ev_k_norm, ev_w_out=ev_w_out, od_w_in=od_w_in, od_pool_w=od_pool_w, od_pool_scale=od_pool_scale, od_sgu_norm=od_sgu_norm, od_sgu_w=od_sgu_w, od_sgu_b=od_sgu_b, od_w_out=od_w_out, final_norm=final_norm, loss_target=loss_target, m_ffn1_norm=m_ffn1_norm, m_ffn1_w_in=m_ffn1_w_in, m_ffn1_w_out=m_ffn1_w_out, m_mix_norm=m_mix_norm, m_ffn2_norm=m_ffn2_norm, m_ffn2_w_in=m_ffn2_w_in, m_ffn2_w_out=m_ffn2_w_out, m_ev_w_in=m_ev_w_in, m_ev_conv_w=m_ev_conv_w, m_ev_q_norm=m_ev_q_norm, m_ev_k_norm=m_ev_k_norm, m_ev_w_out=m_ev_w_out, m_od_w_in=m_od_w_in, m_od_pool_w=m_od_pool_w, m_od_pool_scale=m_od_pool_scale, m_od_sgu_norm=m_od_sgu_norm, m_od_sgu_w=m_od_sgu_w, m_od_sgu_b=m_od_sgu_b, m_od_w_out=m_od_w_out, m_final_norm=m_final_norm, v_ffn1_norm=v_ffn1_norm, v_ffn1_w_in=v_ffn1_w_in, v_ffn1_w_out=v_ffn1_w_out, v_mix_norm=v_mix_norm, v_ffn2_norm=v_ffn2_norm, v_ffn2_w_in=v_ffn2_w_in, v_ffn2_w_out=v_ffn2_w_out, v_ev_w_in=v_ev_w_in, v_ev_conv_w=v_ev_conv_w, v_ev_q_norm=v_ev_q_norm, v_ev_k_norm=v_ev_k_norm, v_ev_w_out=v_ev_w_out, v_od_w_in=v_od_w_in, v_od_pool_w=v_od_pool_w, v_od_pool_scale=v_od_pool_scale, v_od_sgu_norm=v_od_sgu_norm, v_od_sgu_w=v_od_sgu_w, v_od_sgu_b=v_od_sgu_b, v_od_w_out=v_od_w_out, v_final_norm=v_final_norm)
    weights = {n: given[n] for n in TWIN_WEIGHTS}
    shared = {n: given[n] for n in SHARED_INPUTS}
    per_example = {n: given[n] for n in ['x']}
    grad_fn = _jax.value_and_grad(_loss, argnums=(0, 1))

    def one_microbatch(ex, loss_target):
        ex = dict(ex)
        diff = ex.pop(TWIN_DIFF_INPUT)
        return grad_fn(weights, diff, {**shared, **ex}, loss_target)

    if N_MICROBATCH == 1:
        loss, (grad_w, grad_x) = one_microbatch(per_example, given["loss_target"])
    else:
        def body(carry, xs):
            loss_sum, grad_sum = carry
            l_k, (gw_k, gx_k) = one_microbatch(xs[0], xs[1])
            with _jax.named_scope("update"):
                return (loss_sum + l_k, _jax.tree.map(_jnp.add, grad_sum, gw_k)), gx_k

        init = (_jnp.zeros((), _jnp.float32), _jax.tree.map(_jnp.zeros_like, weights))
        (loss, grad_w), grad_x = _jax.lax.scan(body, init, (per_example, given["loss_target"]))
    with _jax.named_scope("update"):
        delta_w, new_m, new_v = {}, {}, {}
        for n in TWIN_WEIGHTS:
            delta_w[n], new_m[n], new_v[n] = _adamw(weights[n], grad_w[n], given["m_" + n], given["v_" + n])
    return (loss, grad_x, *[grad_w[n] for n in TWIN_WEIGHTS], *[delta_w[n] for n in TWIN_WEIGHTS],
            *[new_m[n] for n in TWIN_WEIGHTS], *[new_v[n] for n in TWIN_WEIGHTS])
```

```python
import functools
import math

import jax
import jax.numpy as jnp
from jax import lax
from jax.experimental import pallas as pl
from jax.experimental.pallas import tpu as pltpu

F32, BF16 = jnp.float32, jnp.bfloat16
EPS = 1e-6
N_DEV = 8
V7X_VMEM_BYTES = 64 * 1024 * 1024
VMEM_LIMIT = V7X_VMEM_BYTES - 8 * 1024 * 1024
LANES = 128
HEAD_DIM = 64
N_Q_HEADS = 8
N_KV_HEADS = 2
Q_PER_KV = N_Q_HEADS // N_KV_HEADS
GRID_W = 64
ROPE_THETA = 10000.0
POOL_RADII = (1, 2, 4, 8)
SGU_CHUNK = 128
GROUP = 128
ADAM_LR, ADAM_B1, ADAM_B2, ADAM_EPS, ADAM_WD, ADAM_STEP = 0.001, 0.9, 0.999, 1e-08, 0.01, 10
MESH_ID = pl.DeviceIdType.MESH


def _cp(*sem):
    return pltpu.CompilerParams(dimension_semantics=sem, vmem_limit_bytes=VMEM_LIMIT)


def _dot(a, b, ca, cb):
    return lax.dot_general(a, b, (((ca,), (cb,)), ((), ())), preferred_element_type=F32)


def _nn(a, b):
    return _dot(a, b, 1, 0)


def _nt(a, b):
    return _dot(a, b, 1, 1)


def _tn(a, b):
    return _dot(a, b, 0, 0)


def _split_mm(x, m):
    hi = x.astype(BF16)
    lo = (x - hi.astype(F32)).astype(BF16)
    return _nn(hi, m) + _nn(lo, m)


def _tile(n, pref):
    t = min(n, pref)
    assert n % t == 0, (n, pref)
    return t


@jax.custom_vjp
def bmm(x, w):
    return _nn(x.astype(BF16), w.astype(BF16))


def _bmm_fwd(x, w):
    return bmm(x, w), (x, w)


def _bmm_bwd(res, g):
    x, w = res
    gb = g.astype(BF16)
    return _nt(gb, w.astype(BF16)), _tn(x.astype(BF16), gb)


bmm.defvjp(_bmm_fwd, _bmm_bwd)


def _shift_raw(x, d):
    n = x.shape[0]
    r = pltpu.roll(x, d % n, axis=0)
    row = lax.broadcasted_iota(jnp.int32, x.shape, 0)
    keep = (row >= d) if d > 0 else (row < n + d)
    return jnp.where(keep, r, 0.0)


def shift_rows(x, d):
    @jax.custom_vjp
    def f(v):
        return _shift_raw(v, d)

    f.defvjp(lambda v: (_shift_raw(v, d), None), lambda _, g: (_shift_raw(g, -d),))
    return f(x)


def _swap_raw(x):
    n = x.shape[1]
    nxt = pltpu.roll(x, n - 1, axis=1)
    prv = pltpu.roll(x, 1, axis=1)
    lane = lax.broadcasted_iota(jnp.int32, x.shape, 1)
    return jnp.where(lane % 2 == 0, nxt, prv)


@jax.custom_vjp
def swap_pairs(x):
    return _swap_raw(x)


swap_pairs.defvjp(lambda x: (_swap_raw(x), None), lambda _, g: (_swap_raw(g),))


@jax.custom_vjp
def group_mean(x, bd):
    return _split_mm(x, bd)


group_mean.defvjp(lambda x, bd: (_split_mm(x, bd), bd), lambda bd, g: (_split_mm(g, bd), jnp.zeros_like(bd)))


def _rope_norm(x, gain, cos, sgn, bd, scale):
    xn = x * lax.rsqrt(group_mean(x * x, bd) + EPS) * gain
    return (xn * cos + swap_pairs(xn) * sgn) * scale


def _conv_gate(gb, gc, hc, w):
    z = gc * hc
    c = shift_rows(z, 1) * w[0:1] + z * w[1:2] + shift_rows(z, -1) * w[2:3]
    return gb * c


def _window_sum(p, r):
    b = f = p
    k = 1
    while k < r:
        b = b + shift_rows(b, k)
        f = f + shift_rows(f, -k)
        k *= 2
    return b + f - p + shift_rows(p, r) + shift_rows(p, -r)


def _pool_mix(p, pool_w, scale):
    n = p.shape[0]
    t = lax.broadcasted_iota(jnp.int32, (n, 1), 0)
    outs = []
    for gi, r in enumerate(POOL_RADII):
        pg = p[:, gi * GROUP:(gi + 1) * GROUP]
        cnt = (jnp.minimum(t + r, n - 1) - jnp.maximum(t - r, 0) + 1).astype(F32)
        pooled = _window_sum(pg, r) / cnt - pg
        outs.append(bmm(pooled, pool_w[gi]))
    return jnp.concatenate(outs, axis=1) * scale


def _sgu(u, v, norm_g, w_s, b_full):
    ug = jax.nn.gelu(u)
    vg = jax.nn.gelu(v)
    vn = vg * lax.rsqrt(jnp.mean(vg * vg, axis=-1, keepdims=True) + EPS) * norm_g
    cols = []
    for g in range(w_s.shape[0]):
        rows = []
        for n in range(u.shape[0] // SGU_CHUNK):
            blk = vn[n * SGU_CHUNK:(n + 1) * SGU_CHUNK, g * GROUP:(g + 1) * GROUP]
            rows.append(bmm(w_s[g], blk) + b_full[g])
        cols.append(jnp.concatenate(rows, axis=0))
    return ug * jnp.concatenate(cols, axis=1)


def rms_fwd(x, gain):
    T, D = x.shape
    tm = _tile(T, 512)

    def body(x_ref, g_ref, o_ref):
        xv = x_ref[...]
        r = lax.rsqrt(jnp.mean(xv * xv, axis=-1, keepdims=True) + EPS)
        o_ref[...] = (xv * r * g_ref[...]).astype(BF16)

    return pl.pallas_call(
        body, name="rms_fwd", grid=(T // tm,),
        in_specs=[pl.BlockSpec((tm, D), lambda i: (i, 0)), pl.BlockSpec((1, D), lambda i: (0, 0))],
        out_specs=pl.BlockSpec((tm, D), lambda i: (i, 0)),
        out_shape=jax.ShapeDtypeStruct((T, D), BF16), compiler_params=_cp("parallel"),
    )(x, gain)


def _rms_bwd_math(xv, gain, dy, dres):
    r = lax.rsqrt(jnp.mean(xv * xv, axis=-1, keepdims=True) + EPS)
    xh = xv * r
    dxh = dy * gain
    dx = dres + r * (dxh - xh * jnp.mean(dxh * xh, axis=-1, keepdims=True))
    return dx, jnp.sum(dy * xh, axis=0, keepdims=True)


def rms_bwd(x, gain, dy, dres):
    T, D = x.shape
    tm = _tile(T, 512)

    def body(x_ref, g_ref, dy_ref, dr_ref, dx_ref, dg_ref):
        dx, dg = _rms_bwd_math(x_ref[...], g_ref[...], dy_ref[...], dr_ref[...])
        dx_ref[...] = dx

        @pl.when(pl.program_id(0) == 0)
        def _():
            dg_ref[...] = jnp.zeros_like(dg_ref)

        dg_ref[...] += dg

    row = pl.BlockSpec((tm, D), lambda i: (i, 0))
    return pl.pallas_call(
        body, name="rms_bwd", grid=(T // tm,),
        in_specs=[row, pl.BlockSpec((1, D), lambda i: (0, 0)), row, row],
        out_specs=[row, pl.BlockSpec((1, D), lambda i: (0, 0))],
        out_shape=[jax.ShapeDtypeStruct((T, D), F32), jax.ShapeDtypeStruct((1, D), F32)],
        compiler_params=_cp("arbitrary"),
    )(x, gain, dy, dres)


FFN_TN = 256


def ffn_fwd(x, gain, wt_in, w_out):
    T, D = x.shape
    F = w_out.shape[0]
    tm, tn = _tile(T, 1024), FFN_TN
    nc = F // tn

    def body(x_ref, gn_ref, wg_ref, wu_ref, wo_ref, y_ref, gu_ref, xn_s, acc_s):
        c = pl.program_id(1)

        @pl.when(c == 0)
        def _():
            xv = x_ref[...]
            r = lax.rsqrt(jnp.mean(xv * xv, axis=-1, keepdims=True) + EPS)
            xn_s[...] = (xv * r * gn_ref[...]).astype(BF16)
            acc_s[...] = jnp.zeros_like(acc_s)

        xn = xn_s[...]
        g = _nt(xn, wg_ref[...])
        u = _nt(xn, wu_ref[...])
        gu_ref[:, :tn] = g.astype(BF16)
        gu_ref[:, tn:] = u.astype(BF16)
        a = (g * jax.nn.sigmoid(g) * u).astype(BF16)
        acc_s[...] += _nn(a, wo_ref[...])

        @pl.when(c == nc - 1)
        def _():
            y_ref[...] = x_ref[...] + 0.5 * acc_s[...]

    row = pl.BlockSpec((tm, D), lambda i, c: (i, 0))
    return pl.pallas_call(
        body, name="ffn_fwd", grid=(T // tm, nc),
        in_specs=[row, pl.BlockSpec((1, D), lambda i, c: (0, 0)),
                  pl.BlockSpec((tn, D), lambda i, c: (c, 0)),
                  pl.BlockSpec((tn, D), lambda i, c: (c + nc, 0)),
                  pl.BlockSpec((tn, D), lambda i, c: (c, 0))],
        out_specs=[row, pl.BlockSpec((tm, 2 * tn), lambda i, c: (i, c))],
        out_shape=[jax.ShapeDtypeStruct((T, D), F32), jax.ShapeDtypeStruct((T, 2 * F), BF16)],
        scratch_shapes=[pltpu.VMEM((tm, D), BF16), pltpu.VMEM((tm, D), F32)],
        compiler_params=_cp("parallel", "arbitrary"),
    )(x, gain, wt_in, wt_in, w_out)


def ffn_bwd_x(dout, x, gain, gu, wt_in, w_out):
    T, D = x.shape
    F = w_out.shape[0]
    tm, tn = _tile(T, 1024), FFN_TN
    nc = F // tn

    def body(do_ref, x_ref, gn_ref, gu_ref, wg_ref, wu_ref, wo_ref,
             dx_ref, dgn_ref, a_ref, dgu_ref, xn_ref, dob_ref, acc_s):
        i, c = pl.program_id(0), pl.program_id(1)

        @pl.when(c == 0)
        def _():
            xv = x_ref[...]
            r = lax.rsqrt(jnp.mean(xv * xv, axis=-1, keepdims=True) + EPS)
            xn_ref[...] = (xv * r * gn_ref[...]).astype(BF16)
            dob_ref[...] = (0.5 * do_ref[...]).astype(BF16)
            acc_s[...] = jnp.zeros_like(acc_s)

        da = _nt(dob_ref[...], wo_ref[...])
        g = gu_ref[:, :tn].astype(F32)
        u = gu_ref[:, tn:].astype(F32)
        sig = jax.nn.sigmoid(g)
        sl = g * sig
        a_ref[...] = (sl * u).astype(BF16)
        dg = (da * u * (sig * (1.0 + g * (1.0 - sig)))).astype(BF16)
        du = (da * sl).astype(BF16)
        dgu_ref[:, :tn] = dg
        dgu_ref[:, tn:] = du
        acc_s[...] += _nn(dg, wg_ref[...]) + _nn(du, wu_ref[...])

        @pl.when(c == nc - 1)
        def _():
            dx, dgn = _rms_bwd_math(x_ref[...], gn_ref[...], acc_s[...], do_ref[...])
            dx_ref[...] = dx

            @pl.when(i == 0)
            def _():
                dgn_ref[...] = jnp.zeros_like(dgn_ref)

            dgn_ref[...] += dgn

    row = pl.BlockSpec((tm, D), lambda i, c: (i, 0))
    return pl.pallas_call(
        body, name="ffn_bwd_x", grid=(T // tm, nc),
        in_specs=[row, row, pl.BlockSpec((1, D), lambda i, c: (0, 0)),
                  pl.BlockSpec((tm, 2 * tn), lambda i, c: (i, c)),
                  pl.BlockSpec((tn, D), lambda i, c: (c, 0)),
                  pl.BlockSpec((tn, D), lambda i, c: (c + nc, 0)),
                  pl.BlockSpec((tn, D), lambda i, c: (c, 0))],
        out_specs=[row, pl.BlockSpec((1, D), lambda i, c: (0, 0)),
                   pl.BlockSpec((tm, tn), lambda i, c: (i, c)),
                   pl.BlockSpec((tm, 2 * tn), lambda i, c: (i, c)), row, row],
        out_shape=[jax.ShapeDtypeStruct((T, D), F32), jax.ShapeDtypeStruct((1, D), F32),
                   jax.ShapeDtypeStruct((T, F), BF16), jax.ShapeDtypeStruct((T, 2 * F), BF16),
                   jax.ShapeDtypeStruct((T, D), BF16), jax.ShapeDtypeStruct((T, D), BF16)],
        scratch_shapes=[pltpu.VMEM((tm, D), F32)],
        compiler_params=_cp("arbitrary", "arbitrary"),
    )(dout, x, gain, gu, wt_in, wt_in, w_out)


MM_TM = 512
MM_TC = 256


def mm_nt(a, wt, pieces, out_dtype, emit_a_bf16=False):
    T, K = a.shape
    tm = _tile(T, MM_TM)
    npc = len(pieces)

    def body(*refs):
        a_ref, w_refs, o_refs = refs[0], refs[1:1 + npc], refs[1 + npc:]
        ab = a_ref[...].astype(BF16)
        for w_ref, o_ref in zip(w_refs, o_refs[:npc]):
            o_ref[...] = _nt(ab, w_ref[...]).astype(o_ref.dtype)
        if emit_a_bf16:
            o_refs[npc][...] = ab

    in_specs = [pl.BlockSpec((tm, K), lambda i: (i, 0))]
    out_specs, out_shape = [], []
    for r0, n in pieces:
        assert r0 % n == 0
        in_specs.append(pl.BlockSpec((n, K), functools.partial(lambda i, b: (b, 0), b=r0 // n)))
        out_specs.append(pl.BlockSpec((tm, n), lambda i: (i, 0)))
        out_shape.append(jax.ShapeDtypeStruct((T, n), out_dtype))
    if emit_a_bf16:
        out_specs.append(pl.BlockSpec((tm, K), lambda i: (i, 0)))
        out_shape.append(jax.ShapeDtypeStruct((T, K), BF16))
    return pl.pallas_call(
        body, name="mm_nt", grid=(T // tm,), in_specs=in_specs, out_specs=out_specs, out_shape=out_shape,
        compiler_params=_cp("parallel"),
    )(a, *([wt] * npc))


def mm_nn(a_list, w, residual=None):
    T = a_list[0].shape[0]
    N = w.shape[1]
    tm = _tile(T, MM_TM)
    na = len(a_list)

    def body(*refs):
        a_refs, w_refs = refs[:na], refs[na:2 * na]
        o_ref = refs[-1]
        acc = refs[2 * na][...] if residual is not None else None
        for a_ref, w_ref in zip(a_refs, w_refs):
            t = _nn(a_ref[...].astype(BF16), w_ref[...])
            acc = t if acc is None else acc + t
        o_ref[...] = acc

    in_specs, r0 = [], 0
    w_specs = []
    for a in a_list:
        k = a.shape[1]
        assert r0 % k == 0
        in_specs.append(pl.BlockSpec((tm, k), lambda i: (i, 0)))
        w_specs.append(pl.BlockSpec((k, N), functools.partial(lambda i, b: (b, 0), b=r0 // k)))
        r0 += k
    assert r0 == w.shape[0]
    args = list(a_list) + [w] * na
    in_specs = in_specs + w_specs
    if residual is not None:
        in_specs.append(pl.BlockSpec((tm, N), lambda i: (i, 0)))
        args.append(residual)
    return pl.pallas_call(
        body, name="mm_nn", grid=(T // tm,), in_specs=in_specs,
        out_specs=pl.BlockSpec((tm, N), lambda i: (i, 0)),
        out_shape=jax.ShapeDtypeStruct((T, N), F32), compiler_params=_cp("parallel"),
    )(*args)


def mm_tn(a, b, n_rows, row_block, prev=None, grid=None, col_block=None):
    T, M = a.shape
    N = b.shape[1]
    tc = MM_TC
    assert M % tc == 0 and n_rows % tc == 0
    if grid is None:
        grid, col_block = (M // tc,), (lambda j: j)

    def body(*refs):
        a_ref, b_ref, o_ref = refs[0], refs[1], refs[-1]
        o_ref[...] = _tn(a_ref[...], b_ref[...]).astype(BF16)

    in_specs = [pl.BlockSpec((T, tc), lambda *g: (0, col_block(*g))), pl.BlockSpec((T, N), lambda *g: (0, 0))]
    args = [a, b]
    aliases = {}
    if prev is not None:
        in_specs.append(pl.BlockSpec(memory_space=pl.ANY))
        args.append(prev)
        aliases = {2: 0}
    return pl.pallas_call(
        body, name="mm_tn", grid=grid, in_specs=in_specs,
        out_specs=pl.BlockSpec((tc, N), lambda *g: (row_block(*g), 0)),
        out_shape=jax.ShapeDtypeStruct((n_rows, N), BF16), input_output_aliases=aliases,
        compiler_params=_cp(*(["parallel"] * len(grid))),
    )(*args)


def conv_fwd(proj_a, conv_w, n_ex):
    T, C3 = proj_a.shape
    C = C3 // 3
    S = T // n_ex

    def body(gb_ref, gc_ref, hc_ref, w_ref, o_ref):
        o_ref[...] = _conv_gate(gb_ref[...], gc_ref[...], hc_ref[...], w_ref[...]).astype(BF16)

    col = lambda k: pl.BlockSpec((S, C), functools.partial(lambda b, kk: (b, kk), kk=k))
    return pl.pallas_call(
        body, name="conv_fwd", grid=(n_ex,),
        in_specs=[col(0), col(1), col(2), pl.BlockSpec((3, C), lambda b: (0, 0))],
        out_specs=pl.BlockSpec((S, C), lambda b: (b, 0)),
        out_shape=jax.ShapeDtypeStruct((T, C), BF16), compiler_params=_cp("parallel"),
    )(proj_a, proj_a, proj_a, conv_w)


def conv_bwd(proj_a, conv_w, n_ex, dy):
    T, C3 = proj_a.shape
    C = C3 // 3
    S = T // n_ex

    def body(gb_ref, gc_ref, hc_ref, w_ref, dy_ref, dp_ref, dw_ref):
        _, vjp = jax.vjp(_conv_gate, gb_ref[...], gc_ref[...], hc_ref[...], w_ref[...])
        dgb, dgc, dhc, dw = vjp(dy_ref[...].astype(F32))
        dp_ref[:, 0:C] = dgb.astype(BF16)
        dp_ref[:, C:2 * C] = dgc.astype(BF16)
        dp_ref[:, 2 * C:] = dhc.astype(BF16)

        @pl.when(pl.program_id(0) == 0)
        def _():
            dw_ref[...] = jnp.zeros_like(dw_ref)

        dw_ref[...] += dw

    col = lambda k: pl.BlockSpec((S, C), functools.partial(lambda b, kk: (b, kk), kk=k))
    return pl.pallas_call(
        body, name="conv_bwd", grid=(n_ex,),
        in_specs=[col(0), col(1), col(2), pl.BlockSpec((3, C), lambda b: (0, 0)),
                  pl.BlockSpec((S, C), lambda b: (b, 0))],
        out_specs=[pl.BlockSpec((S, C3), lambda b: (b, 0)), pl.BlockSpec((3, C), lambda b: (0, 0))],
        out_shape=[jax.ShapeDtypeStruct((T, C3), BF16), jax.ShapeDtypeStruct((3, C), F32)],
        compiler_params=_cp("arbitrary"),
    )(proj_a, proj_a, proj_a, conv_w, dy)


QW = N_Q_HEADS * HEAD_DIM
KW = N_KV_HEADS * HEAD_DIM
QP = N_Q_HEADS * LANES
KP = N_KV_HEADS * LANES


def _attn_consts(seq):
    rows = seq // GRID_W
    r_idx, c_idx = jnp.meshgrid(jnp.arange(rows), jnp.arange(GRID_W), indexing='ij')
    r_idx = r_idx.reshape(-1).astype(F32)
    c_idx = c_idx.reshape(-1).astype(F32)
    n_freq = HEAD_DIM // 4
    inv = ROPE_THETA ** (-jnp.arange(n_freq, dtype=F32) / n_freq)
    ang = jnp.concatenate([r_idx[:, None] * inv, c_idx[:, None] * inv], axis=-1)
    cos = jnp.repeat(jnp.cos(ang), 2, axis=1)
    sin = jnp.repeat(jnp.sin(ang), 2, axis=1)
    sgn = sin * jnp.tile(jnp.array([-1.0, 1.0], F32), HEAD_DIM // 2)
    cos = jnp.tile(cos, (1, N_Q_HEADS))
    sgn = jnp.tile(sgn, (1, N_Q_HEADS))
    lane = jnp.arange(QW)
    bd = jnp.where(lane[:, None] // HEAD_DIM == lane[None, :] // HEAD_DIM, 1.0 / HEAD_DIM, 0.0).astype(BF16)
    dst = (lane // HEAD_DIM) * LANES + lane % HEAD_DIM
    spread = (dst[:, None] == jnp.arange(QP)[None, :]).astype(BF16)
    return dict(cos=cos, sgn=sgn, bd=bd, spread=spread, gather=spread.T)


def qkv_prep_fwd(proj_b, qg, kg, cst, n_ex):
    T = proj_b.shape[0]
    S = T // n_ex
    tm = _tile(S, 512)
    nb = S // tm

    def body(p_ref, qg_ref, kg_ref, cos_ref, sgn_ref, bd_ref, sp_ref, q_ref, k_ref, v_ref):
        pv = p_ref[...]
        cos, sgn, bd, sp = cos_ref[...], sgn_ref[...], bd_ref[...], sp_ref[...]
        qr = _rope_norm(pv[:, :QW], qg_ref[...], cos, sgn, bd, HEAD_DIM ** -0.5)
        kr = _rope_norm(pv[:, QW:QW + KW], kg_ref[...], cos[:, :KW], sgn[:, :KW], bd[:KW, :KW], 1.0)
        q_ref[...] = _nn(qr.astype(BF16), sp).astype(BF16)
        k_ref[...] = _nn(kr.astype(BF16), sp[:KW, :KP]).astype(BF16)
        v_ref[...] = _nn(pv[:, QW + KW:].astype(BF16), sp[:KW, :KP]).astype(BF16)

    full = lambda a: pl.BlockSpec(a.shape, lambda i: (0,) * a.ndim)
    tab = pl.BlockSpec((tm, QW), lambda i: (i % nb, 0))
    return pl.pallas_call(
        body, name="qkv_prep_fwd", grid=(T // tm,),
        in_specs=[pl.BlockSpec((tm, QW + 2 * KW), lambda i: (i, 0)), full(qg), full(kg), tab, tab,
                  full(cst["bd"]), full(cst["spread"])],
        out_specs=[pl.BlockSpec((tm, QP), lambda i: (i, 0)), pl.BlockSpec((tm, KP), lambda i: (i, 0)),
                   pl.BlockSpec((tm, KP), lambda i: (i, 0))],
        out_shape=[jax.ShapeDtypeStruct((T, QP), BF16), jax.ShapeDtypeStruct((T, KP), BF16),
                   jax.ShapeDtypeStruct((T, KP), BF16)],
        compiler_params=_cp("parallel"),
    )(proj_b, qg, kg, cst["cos"], cst["sgn"], cst["bd"], cst["spread"])


def qkv_prep_bwd(proj_b, qg, kg, cst, n_ex, dq, dk_pad, dv_pad):
    T = proj_b.shape[0]
    S = T // n_ex
    tm = _tile(S, 512)
    nb = S // tm

    def body(p_ref, qg_ref, kg_ref, cos_ref, sgn_ref, bd_ref, ga_ref, dq_ref, dk_ref, dv_ref,
             dp_ref, dqg_ref, dkg_ref):
        pv = p_ref[...]
        cos, sgn, bd, ga = cos_ref[...], sgn_ref[...], bd_ref[...], ga_ref[...]
        fq = lambda q, g: _rope_norm(q, g, cos, sgn, bd, HEAD_DIM ** -0.5)
        fk = lambda k, g: _rope_norm(k, g, cos[:, :KW], sgn[:, :KW], bd[:KW, :KW], 1.0)
        _, vq = jax.vjp(fq, pv[:, :QW], qg_ref[...])
        _, vk = jax.vjp(fk, pv[:, QW:QW + KW], kg_ref[...])
        dqp, dqg = vq(dq_ref[...])
        dkp, dkg = vk(_split_mm(dk_ref[...], ga[:KP, :KW]))
        dp_ref[:, :QW] = dqp.astype(BF16)
        dp_ref[:, QW:QW + KW] = dkp.astype(BF16)
        dp_ref[:, QW + KW:] = _split_mm(dv_ref[...], ga[:KP, :KW]).astype(BF16)

        @pl.when(pl.program_id(0) == 0)
        def _():
            dqg_ref[...] = jnp.zeros_like(dqg_ref)
            dkg_ref[...] = jnp.zeros_like(dkg_ref)

        dqg_ref[...] += dqg
        dkg_ref[...] += dkg

    full = lambda a: pl.BlockSpec(a.shape, lambda i: (0,) * a.ndim)
    tab = pl.BlockSpec((tm, QW), lambda i: (i % nb, 0))
    row = lambda n: pl.BlockSpec((tm, n), lambda i: (i, 0))
    return pl.pallas_call(
        body, name="qkv_prep_bwd", grid=(T // tm,),
        in_specs=[row(QW + 2 * KW), full(qg), full(kg), tab, tab, full(cst["bd"]), full(cst["gather"]),
                  row(QW), row(KP), row(KP)],
        out_specs=[row(QW + 2 * KW), pl.BlockSpec((1, QW), lambda i: (0, 0)), pl.BlockSpec((1, KW), lambda i: (0, 0))],
        out_shape=[jax.ShapeDtypeStruct((T, QW + 2 * KW), BF16), jax.ShapeDtypeStruct((1, QW), F32),
                   jax.ShapeDtypeStruct((1, KW), F32)],
        compiler_params=_cp("arbitrary"),
    )(proj_b, qg, kg, cst["cos"], cst["sgn"], cst["bd"], cst["gather"], dq, dk_pad, dv_pad)


ATT_TQ = 256


def attn_fwd(qp, kp, vp, gather, n_ex):
    T = qp.shape[0]
    S = T // n_ex
    tq = _tile(S, ATT_TQ)
    nq = S // tq

    def body(q_ref, k_ref, v_ref, ga_ref, o_ref, op_ref, lse_ref):
        lane = lax.broadcasted_iota(jnp.int32, (tq, LANES), 1)
        lse_all = jnp.zeros((tq, LANES), F32)
        for h in range(N_Q_HEADS):
            kv = h // Q_PER_KV
            qh = q_ref[:, h * LANES:(h + 1) * LANES]
            s = _nt(qh, k_ref[:, kv * LANES:(kv + 1) * LANES])
            m = jnp.max(s, axis=-1, keepdims=True)
            p = jnp.exp(s - m)
            lsum = jnp.sum(p, axis=-1, keepdims=True)
            o = _nn(p.astype(BF16), v_ref[:, kv * LANES:(kv + 1) * LANES]) / lsum
            op_ref[:, h * LANES:(h + 1) * LANES] = o.astype(BF16)
            lse_all = jnp.where(lane == h, m + jnp.log(lsum), lse_all)
        lse_ref[...] = lse_all
        o_ref[...] = _nn(op_ref[...], ga_ref[...]).astype(BF16)

    blk = lambda n: pl.BlockSpec((tq, n), lambda b, i: (b * nq + i, 0))
    kvs = pl.BlockSpec((S, KP), lambda b, i: (b, 0))
    return pl.pallas_call(
        body, name="attn_fwd", grid=(n_ex, nq),
        in_specs=[blk(QP), kvs, kvs, pl.BlockSpec(gather.shape, lambda b, i: (0, 0))],
        out_specs=[blk(QW), blk(QP), blk(LANES)],
        out_shape=[jax.ShapeDtypeStruct((T, QW), BF16), jax.ShapeDtypeStruct((T, QP), BF16),
                   jax.ShapeDtypeStruct((T, LANES), F32)],
        compiler_params=_cp("parallel", "parallel"),
    )(qp, kp, vp, gather)


def attn_bwd(qp, kp, vp, op, lse, do, cst, n_ex):
    T = qp.shape[0]
    S = T // n_ex
    tq = _tile(S, ATT_TQ)
    nq = S // tq

    def body(q_ref, k_ref, v_ref, op_ref, lse_ref, do_ref, sp_ref, ga_ref, dq_ref, dk_ref, dv_ref, dqp_s):
        @pl.when(pl.program_id(1) == 0)
        def _():
            dk_ref[...] = jnp.zeros_like(dk_ref)
            dv_ref[...] = jnp.zeros_like(dv_ref)

        lane = lax.broadcasted_iota(jnp.int32, (tq, LANES), 1)
        dop = _nn(do_ref[...], sp_ref[...]).astype(BF16)
        lse_all = lse_ref[...]
        for h in range(N_Q_HEADS):
            kv = h // Q_PER_KV
            hs = slice(h * LANES, (h + 1) * LANES)
            ks = slice(kv * LANES, (kv + 1) * LANES)
            qh, kk, vv = q_ref[:, hs], k_ref[:, ks], v_ref[:, ks]
            doh = dop[:, hs]
            lse_h = jnp.sum(jnp.where(lane == h, lse_all, 0.0), axis=-1, keepdims=True)
            p = jnp.exp(_nt(qh, kk) - lse_h)
            dp = _nt(doh, vv)
            delta = jnp.sum(doh.astype(F32) * op_ref[:, hs].astype(F32), axis=-1, keepdims=True)
            ds = (p * (dp - delta)).astype(BF16)
            dqp_s[:, hs] = _nn(ds, kk)
            dk_ref[:, ks] += _tn(ds, qh)
            dv_ref[:, ks] += _tn(p.astype(BF16), doh)
        dq_ref[...] = _split_mm(dqp_s[...], ga_ref[...])

    blk = lambda n: pl.BlockSpec((tq, n), lambda b, i: (b * nq + i, 0))
    kvs = pl.BlockSpec((S, KP), lambda b, i: (b, 0))
    full = lambda a: pl.BlockSpec(a.shape, lambda b, i: (0, 0))
    return pl.pallas_call(
        body, name="attn_bwd", grid=(n_ex, nq),
        in_specs=[blk(QP), kvs, kvs, blk(QP), blk(LANES), blk(QW), full(cst["spread"]), full(cst["gather"])],
        out_specs=[blk(QW), kvs, kvs],
        out_shape=[jax.ShapeDtypeStruct((T, QW), F32), jax.ShapeDtypeStruct((T, KP), F32),
                   jax.ShapeDtypeStruct((T, KP), F32)],
        scratch_shapes=[pltpu.VMEM((tq, QP), F32)],
        compiler_params=_cp("arbitrary", "arbitrary"),
    )(qp, kp, vp, op, lse, do, cst["spread"], cst["gather"])


def pool_fwd(p, pool_w, scale, n_ex):
    T, W = p.shape
    S = T // n_ex

    def body(p_ref, w_ref, s_ref, o_ref):
        o_ref[...] = _pool_mix(p_ref[...], w_ref[...], s_ref[...]).astype(BF16)

    return pl.pallas_call(
        body, name="pool_fwd", grid=(n_ex,),
        in_specs=[pl.BlockSpec((S, W), lambda b: (b, 0)),
                  pl.BlockSpec(pool_w.shape, lambda b: (0, 0, 0)),
                  pl.BlockSpec((1, W), lambda b: (0, 0))],
        out_specs=pl.BlockSpec((S, W), lambda b: (b, 0)),
        out_shape=jax.ShapeDtypeStruct((T, W), BF16), compiler_params=_cp("parallel"),
    )(p, pool_w, scale)


def pool_bwd(p, pool_w, scale, n_ex, dy):
    T, W = p.shape
    S = T // n_ex

    def body(p_ref, w_ref, s_ref, dy_ref, dp_ref, dw_ref, ds_ref):
        _, vjp = jax.vjp(_pool_mix, p_ref[...], w_ref[...], s_ref[...])
        dp, dw, ds = vjp(dy_ref[...].astype(F32))
        dp_ref[...] = dp.astype(BF16)

        @pl.when(pl.program_id(0) == 0)
        def _():
            dw_ref[...] = jnp.zeros_like(dw_ref)
            ds_ref[...] = jnp.zeros_like(ds_ref)

        dw_ref[...] += dw
        ds_ref[...] += ds

    wshape = pool_w.shape
    return pl.pallas_call(
        body, name="pool_bwd", grid=(n_ex,),
        in_specs=[pl.BlockSpec((S, W), lambda b: (b, 0)),
                  pl.BlockSpec(wshape, lambda b: (0, 0, 0)),
                  pl.BlockSpec((1, W), lambda b: (0, 0)), pl.BlockSpec((S, W), lambda b: (b, 0))],
        out_specs=[pl.BlockSpec((S, W), lambda b: (b, 0)), pl.BlockSpec(wshape, lambda b: (0, 0, 0)),
                   pl.BlockSpec((1, W), lambda b: (0, 0))],
        out_shape=[jax.ShapeDtypeStruct((T, W), BF16), jax.ShapeDtypeStruct(wshape, F32),
                   jax.ShapeDtypeStruct((1, W), F32)],
        compiler_params=_cp("arbitrary"),
    )(p, pool_w, scale, dy)


SGU_TS = 512


def sgu_fwd(u, v, norm_g, w_s, b_full):
    T, W = u.shape
    ts = _tile(T, SGU_TS)

    def body(u_ref, v_ref, g_ref, w_ref, b_ref, o_ref):
        o_ref[...] = _sgu(u_ref[...], v_ref[...], g_ref[...], w_ref[...], b_ref[...]).astype(BF16)

    row = pl.BlockSpec((ts, W), lambda i: (i, 0))
    wsp = pl.BlockSpec(w_s.shape, lambda i: (0, 0, 0))
    return pl.pallas_call(
        body, name="sgu_fwd", grid=(T // ts,),
        in_specs=[row, row, pl.BlockSpec((1, W), lambda i: (0, 0)), wsp, wsp],
        out_specs=row, out_shape=jax.ShapeDtypeStruct((T, W), BF16), compiler_params=_cp("parallel"),
    )(u, v, norm_g, w_s, b_full)


def sgu_bwd(u, v, norm_g, w_s, b_full, dy):
    T, W = u.shape
    ts = _tile(T, SGU_TS)
    wshape = w_s.shape

    def body(u_ref, v_ref, g_ref, w_ref, b_ref, dy_ref, du_ref, dv_ref, dg_ref, dw_ref, db_ref):
        _, vjp = jax.vjp(_sgu, u_ref[...], v_ref[...], g_ref[...], w_ref[...], b_ref[...])
        du, dv, dg, dw, db = vjp(dy_ref[...].astype(F32))
        du_ref[...] = du.astype(BF16)
        dv_ref[...] = dv.astype(BF16)

        @pl.when(pl.program_id(0) == 0)
        def _():
            dg_ref[...] = jnp.zeros_like(dg_ref)
            dw_ref[...] = jnp.zeros_like(dw_ref)
            db_ref[...] = jnp.zeros_like(db_ref)

        dg_ref[...] += dg
        dw_ref[...] += dw
        db_ref[...] += db

    row = pl.BlockSpec((ts, W), lambda i: (i, 0))
    wsp = pl.BlockSpec(wshape, lambda i: (0, 0, 0))
    wout = pl.BlockSpec(wshape, lambda i: (0, 0, 0))
    vec = pl.BlockSpec((1, W), lambda i: (0, 0))
    return pl.pallas_call(
        body, name="sgu_bwd", grid=(T // ts,),
        in_specs=[row, row, vec, wsp, wsp, row],
        out_specs=[row, row, vec, wout, wout],
        out_shape=[jax.ShapeDtypeStruct((T, W), BF16), jax.ShapeDtypeStruct((T, W), BF16),
                   jax.ShapeDtypeStruct((1, W), F32), jax.ShapeDtypeStruct(wshape, F32),
                   jax.ShapeDtypeStruct(wshape, F32)],
        compiler_params=_cp("arbitrary"),
    )(u, v, norm_g, w_s, b_full, dy)


def loss_head(x, gain, target):
    T, D = x.shape
    tm = _tile(T, 512)

    def body(x_ref, g_ref, t_ref, loss_ref, dx_ref, dg_ref):
        xv, g = x_ref[...], g_ref[...]
        r = lax.rsqrt(jnp.mean(xv * xv, axis=-1, keepdims=True) + EPS)
        err = xv * r * g - t_ref[...]
        part = 0.5 * jnp.sum(jnp.mean(err * err, axis=-1, keepdims=True), axis=0, keepdims=True)
        dx, dg = _rms_bwd_math(xv, g, err * (1.0 / D), jnp.zeros_like(xv))
        dx_ref[...] = dx

        @pl.when(pl.program_id(0) == 0)
        def _():
            loss_ref[...] = jnp.zeros_like(loss_ref)
            dg_ref[...] = jnp.zeros_like(dg_ref)

        loss_ref[...] += part
        dg_ref[...] += dg

    row = pl.BlockSpec((tm, D), lambda i: (i, 0))
    vec = pl.BlockSpec((1, D), lambda i: (0, 0))
    return pl.pallas_call(
        body, name="loss_head", grid=(T // tm,),
        in_specs=[row, vec, row], out_specs=[pl.BlockSpec((1, 1), lambda i: (0, 0)), row, vec],
        out_shape=[jax.ShapeDtypeStruct((1, 1), F32), jax.ShapeDtypeStruct((T, D), F32),
                   jax.ShapeDtypeStruct((1, D), F32)],
        compiler_params=_cp("arbitrary"),
    )(x, gain, target)


def _local_step(x, target, layers, final_norm, n_ex):
    T, D = x.shape
    L = len(layers)
    F = layers[0]["f1_out"].shape[0]
    nc = F // FFN_TN
    cst = _attn_consts(T // n_ex)
    ident = lambda j: j
    EV_A, EV_B = 3 * (D // 2), QW + 2 * KW
    OD_W = D // 2

    saved = []
    for l, W in enumerate(layers):
        s = dict(x0=x)
        x1, s["gu1"] = ffn_fwd(x, W["n1"], W["f1_in_t"], W["f1_out"])
        h = rms_fwd(x1, W["nm"])
        if l % 2 == 0:
            pa, pb = mm_nt(h, W["mx_in_t"], [(0, EV_A), (EV_A, EV_B)], F32)
            qg = jnp.tile(W["q_norm"], N_Q_HEADS)[None]
            kg = jnp.tile(W["k_norm"], N_KV_HEADS)[None]
            mix_a = conv_fwd(pa, W["conv_w"], n_ex)
            qp, kp, vp = qkv_prep_fwd(pb, qg, kg, cst, n_ex)
            mix_b, op, lse = attn_fwd(qp, kp, vp, cst["gather"], n_ex)
            s.update(pa=pa, pb=pb, qg=qg, kg=kg, qp=qp, kp=kp, vp=vp, op=op, lse=lse)
        else:
            p, u, v = mm_nt(h, W["mx_in_t"], [(0, OD_W), (OD_W, OD_W), (2 * OD_W, OD_W)], F32)
            scale = W["pool_scale"][None]
            sn = W["sgu_norm"][None]
            b_full = jnp.broadcast_to(W["sgu_b"][..., None], W["sgu_w"].shape)
            mix_a = pool_fwd(p, W["pool_w"], scale, n_ex)
            mix_b = sgu_fwd(u, v, sn, W["sgu_w"], b_full)
            s.update(p=p, u=u, v=v, scale=scale, sn=sn, b_full=b_full)
        x2 = mm_nn([mix_a, mix_b], W["mx_out"], residual=x1)
        x3, s["gu2"] = ffn_fwd(x2, W["n2"], W["f2_in_t"], W["f2_out"])
        s.update(x1=x1, x2=x2, h=h, mix_a=mix_a, mix_b=mix_b)
        saved.append(s)
        x = x3

    loss, dx, d_final = loss_head(x, final_norm, target)

    big, small = [None] * L, [None] * L

    def ffn_back(dout, xin, gain, gu, wt_in, w_out):
        dxi, dgn, a, dgu, xn, dob = ffn_bwd_x(dout, xin, gain, gu, wt_in, w_out)
        g_out = mm_tn(a, dob, F, ident)
        g_in_t = mm_tn(dgu, xn, 2 * F, lambda k, c: k * nc + c, grid=(2, nc), col_block=lambda k, c: 2 * c + k)
        return dxi, dgn, g_in_t, g_out

    for l in reversed(range(L)):
        s, W = saved[l], layers[l]
        g, sm = {}, {}
        dx, sm["n2"], g["f2_in_t"], g["f2_out"] = ffn_back(dx, s["x2"], W["n2"], s["gu2"], W["f2_in_t"], W["f2_out"])
        dm_a, dm_b, dxb = mm_nt(dx, W["mx_out"], [(0, D // 2), (D // 2, D // 2)], BF16, emit_a_bf16=True)
        half_blocks = (D // 2) // MM_TC
        g_out = mm_tn(s["mix_a"], dxb, D, ident)
        g["mx_out"] = mm_tn(s["mix_b"], dxb, D, lambda jj: jj + half_blocks, prev=g_out)
        if l % 2 == 0:
            d_a, sm["conv_w"] = conv_bwd(s["pa"], W["conv_w"], n_ex, dm_a)
            dq, dkp, dvp = attn_bwd(s["qp"], s["kp"], s["vp"], s["op"], s["lse"], dm_b, cst, n_ex)
            d_b, dqg, dkg = qkv_prep_bwd(s["pb"], s["qg"], s["kg"], cst, n_ex, dq, dkp, dvp)
            dh = mm_nn([d_a, d_b], W["mx_in_t"])
            g_in = mm_tn(d_a, s["h"], EV_A + EV_B, ident)
            g["mx_in_t"] = mm_tn(d_b, s["h"], EV_A + EV_B, lambda jj: jj + EV_A // MM_TC, prev=g_in)
            sm["q_norm"] = dqg.reshape(N_Q_HEADS, HEAD_DIM).sum(0)
            sm["k_norm"] = dkg.reshape(N_KV_HEADS, HEAD_DIM).sum(0)
        else:
            d_p, sm["pool_w"], d_ps = pool_bwd(s["p"], W["pool_w"], s["scale"], n_ex, dm_a)
            d_u, d_v, d_sn, sm["sgu_w"], d_sb = sgu_bwd(s["u"], s["v"], s["sn"], W["sgu_w"], s["b_full"], dm_b)
            dh = mm_nn([d_p, d_u, d_v], W["mx_in_t"])
            nb = OD_W // MM_TC
            g_in = mm_tn(d_p, s["h"], 3 * OD_W, ident)
            g_in = mm_tn(d_u, s["h"], 3 * OD_W, lambda jj: jj + nb, prev=g_in)
            g["mx_in_t"] = mm_tn(d_v, s["h"], 3 * OD_W, lambda jj: jj + 2 * nb, prev=g_in)
            sm["pool_scale"], sm["sgu_norm"], sm["sgu_b"] = d_ps[0], d_sn[0], d_sb.sum(-1)
        dx, sm["nm"] = rms_bwd(s["x1"], W["nm"], dh, dx)
        dx, sm["n1"], g["f1_in_t"], g["f1_out"] = ffn_back(dx, s["x0"], W["n1"], s["gu1"], W["f1_in_t"], W["f1_out"])
        big[l], small[l] = g, sm
    return loss, dx, big, small, d_final


def _my_place():
    x, y, c = lax.axis_index("x"), lax.axis_index("y"), lax.axis_index("c")
    return x, y, c, 4 * x + 2 * y + c


def _peer(x, y, c, k):
    px = 1 - x if k & 4 else x
    py = 1 - y if k & 2 else y
    pc = 1 - c if k & 1 else c
    return (px, py, pc), 4 * px + 2 * py + pc


ANY = pl.BlockSpec(memory_space=pl.ANY)


def all_gather(srcs):
    n = len(srcs)
    dims = [a.shape[-2:] for a, _ in srcs]

    def body(*refs):
        ins, outs = refs[:n], refs[n:2 * n]
        send_sems, recv_sems, loc_sems = refs[2 * n:]
        x, y, c, me = _my_place()

        def rows(t, idx):
            r = dims[t][0]
            return outs[t].at[pl.ds(pl.multiple_of(idx * r, 8), r), :]

        started = []
        for t in range(n):
            src = ins[t] if srcs[t][1] is None else ins[t].at[srcs[t][1]]
            loc = pltpu.make_async_copy(src, rows(t, me), loc_sems.at[t])
            loc.start()
            started.append(loc)
            for k in range(1, N_DEV):
                peer, _ = _peer(x, y, c, k)
                cp = pltpu.make_async_remote_copy(src_ref=src, dst_ref=rows(t, me), send_sem=send_sems.at[t * 7 + k - 1],
                                                  recv_sem=recv_sems.at[t * 7 + k - 1], device_id=peer, device_id_type=MESH_ID)
                cp.start()
        for t in range(n):
            for k in range(1, N_DEV):
                peer, pidx = _peer(x, y, c, k)
                cp = pltpu.make_async_remote_copy(src_ref=rows(t, pidx), dst_ref=rows(t, pidx), send_sem=send_sems.at[t * 7 + k - 1],
                                                  recv_sem=recv_sems.at[t * 7 + k - 1], device_id=peer, device_id_type=MESH_ID)
                cp.wait_recv()
                cp.wait_send()
        for loc in started:
            loc.wait()

    return pl.pallas_call(
        body, name="all_gather", in_specs=[ANY] * n, out_specs=[ANY] * n,
        out_shape=[jax.ShapeDtypeStruct((N_DEV * r, cc), a.dtype) for (a, _), (r, cc) in zip(srcs, dims)],
        scratch_shapes=[pltpu.SemaphoreType.DMA((7 * n,)), pltpu.SemaphoreType.DMA((7 * n,)), pltpu.SemaphoreType.DMA((n,))],
    )(*[a for a, _ in srcs])


def reduce_scatter(grads):
    n = len(grads)
    dims = [(g.shape[0] // N_DEV, g.shape[1]) for g in grads]

    def body(*refs):
        ins, outs = refs[:n], refs[n:2 * n]
        send_sems, recv_sems, loc_sems = refs[2 * n:]
        x, y, c, me = _my_place()

        def rows(t, idx):
            r = dims[t][0]
            return ins[t].at[pl.ds(pl.multiple_of(idx * r, 8), r), :]

        started = []
        for t in range(n):
            loc = pltpu.make_async_copy(rows(t, me), outs[t].at[me], loc_sems.at[t])
            loc.start()
            started.append(loc)
            for k in range(1, N_DEV):
                peer, pidx = _peer(x, y, c, k)
                cp = pltpu.make_async_remote_copy(src_ref=rows(t, pidx), dst_ref=outs[t].at[me], send_sem=send_sems.at[t * 7 + k - 1],
                                                  recv_sem=recv_sems.at[t * 7 + k - 1], device_id=peer, device_id_type=MESH_ID)
                cp.start()
        for t in range(n):
            for k in range(1, N_DEV):
                peer, pidx = _peer(x, y, c, k)
                cp = pltpu.make_async_remote_copy(src_ref=rows(t, pidx), dst_ref=outs[t].at[pidx], send_sem=send_sems.at[t * 7 + k - 1],
                                                  recv_sem=recv_sems.at[t * 7 + k - 1], device_id=peer, device_id_type=MESH_ID)
                cp.wait_recv()
                cp.wait_send()
        for loc in started:
            loc.wait()

    return pl.pallas_call(
        body, name="reduce_scatter", in_specs=[ANY] * n, out_specs=[ANY] * n,
        out_shape=[jax.ShapeDtypeStruct((N_DEV, r, cc), g.dtype) for g, (r, cc) in zip(grads, dims)],
        scratch_shapes=[pltpu.SemaphoreType.DMA((7 * n,)), pltpu.SemaphoreType.DMA((7 * n,)), pltpu.SemaphoreType.DMA((n,))],
    )(*grads)


def all_reduce_small(v):
    R, C = v.shape

    def body(v_ref, o_ref, buf, send_sems, recv_sems):
        x, y, c, me = _my_place()
        buf[me] = v_ref[...]
        for k in range(1, N_DEV):
            peer, _ = _peer(x, y, c, k)
            pltpu.make_async_remote_copy(src_ref=v_ref, dst_ref=buf.at[me], send_sem=send_sems.at[k - 1],
                                         recv_sem=recv_sems.at[k - 1], device_id=peer, device_id_type=MESH_ID).start()
        for k in range(1, N_DEV):
            peer, pidx = _peer(x, y, c, k)
            cp = pltpu.make_async_remote_copy(src_ref=v_ref, dst_ref=buf.at[pidx], send_sem=send_sems.at[k - 1],
                                              recv_sem=recv_sems.at[k - 1], device_id=peer, device_id_type=MESH_ID)
            cp.wait_recv()
            cp.wait_send()
        acc = buf[0]
        for s in range(1, N_DEV):
            acc = acc + buf[s]
        o_ref[...] = acc

    vm = pl.BlockSpec(memory_space=pltpu.VMEM)
    return pl.pallas_call(
        body, name="all_reduce_small", in_specs=[vm], out_specs=vm, out_shape=jax.ShapeDtypeStruct((R, C), F32),
        scratch_shapes=[pltpu.VMEM((N_DEV, R, C), F32), pltpu.SemaphoreType.DMA((7,)), pltpu.SemaphoreType.DMA((7,))],
        compiler_params=pltpu.CompilerParams(vmem_limit_bytes=VMEM_LIMIT),
    )(v)


def cast_shards(w, transpose):
    L, A, B = w.shape
    oshape = (B, A) if transpose else (A, B)

    def body(w_ref, o_ref):
        wv = w_ref[...]
        o_ref[...] = (wv.T if transpose else wv).astype(BF16)

    return pl.pallas_call(
        body, name="cast_shards", grid=(L,),
        in_specs=[pl.BlockSpec((None, A, B), lambda l: (l, 0, 0))],
        out_specs=pl.BlockSpec((None,) + oshape, lambda l: (l, 0, 0)),
        out_shape=jax.ShapeDtypeStruct((L,) + oshape, BF16), compiler_params=_cp("parallel"),
    )(w)


ADAM_TC = 256


def adamw(parts, w, m, v, l, transpose):
    P, R, C = parts.shape
    tc = _tile(C, ADAM_TC)
    c1, c2 = 1.0 - ADAM_B1 ** ADAM_STEP, 1.0 - ADAM_B2 ** ADAM_STEP

    def body(p_ref, w_ref, m_ref, v_ref, g_ref, d_ref, mo_ref, vo_ref):
        g = p_ref[0].astype(F32)
        for s in range(1, P):
            g = g + p_ref[s].astype(F32)
        if transpose:
            g = g.T
        m1 = ADAM_B1 * m_ref[...] + (1.0 - ADAM_B1) * g
        v1 = ADAM_B2 * v_ref[...] + (1.0 - ADAM_B2) * (g * g)
        g_ref[...] = g
        mo_ref[...] = m1
        vo_ref[...] = v1
        d_ref[...] = -ADAM_LR * ((m1 / c1) / (jnp.sqrt(v1 / c2) + ADAM_EPS) + ADAM_WD * w_ref[...])

    if transpose:
        wspec = pl.BlockSpec((None, tc, R), lambda i: (l, i, 0))
        ospec, oshape = pl.BlockSpec((tc, R), lambda i: (i, 0)), (C, R)
    else:
        wspec = pl.BlockSpec((None, R, tc), lambda i: (l, 0, i))
        ospec, oshape = pl.BlockSpec((R, tc), lambda i: (0, i)), (R, C)
    return pl.pallas_call(
        body, name="adamw", grid=(C // tc,),
        in_specs=[pl.BlockSpec((P, R, tc), lambda i: (0, 0, i)), wspec, wspec, wspec],
        out_specs=[ospec] * 4, out_shape=[jax.ShapeDtypeStruct(oshape, F32)] * 4,
        compiler_params=_cp("parallel"),
    )(parts, w, m, v)


_WEIGHTS = ['ffn1_norm', 'ffn1_w_in', 'ffn1_w_out', 'mix_norm', 'ffn2_norm', 'ffn2_w_in', 'ffn2_w_out', 'ev_w_in',
            'ev_conv_w', 'ev_q_norm', 'ev_k_norm', 'ev_w_out', 'od_w_in', 'od_pool_w', 'od_pool_scale', 'od_sgu_norm',
            'od_sgu_w', 'od_sgu_b', 'od_w_out', 'final_norm']
_BIG = dict(ffn1_w_in=True, ffn1_w_out=False, ffn2_w_in=True, ffn2_w_out=False,
            ev_w_in=True, ev_w_out=False, od_w_in=True, od_w_out=False)
_SMALL_SHARDED = ['ev_conv_w', 'od_pool_scale', 'od_sgu_norm']
_SMALL = [n for n in _WEIGHTS if n not in _BIG]
_PACK_ROWS = 8 * LANES


def _pack(arrs):
    flat = jnp.concatenate([a.reshape(-1) for a in arrs])
    pad = (-flat.shape[0]) % _PACK_ROWS
    return jnp.pad(flat, (0, pad)).reshape(-1, LANES)


def _unpack(buf, shapes):
    flat, out, off = buf.reshape(-1), [], 0
    for s in shapes:
        n = math.prod(s)
        out.append(flat[off:off + n].reshape(s))
        off += n
    return out


def _unshard_last(g, lead):
    nd = len(lead)
    return jnp.moveaxis(g, 0, nd).reshape(*lead, -1)


def kernel(x, ffn1_norm, ffn1_w_in, ffn1_w_out, mix_norm, ffn2_norm, ffn2_w_in, ffn2_w_out, ev_w_in, ev_conv_w, ev_q_norm, ev_k_norm, ev_w_out, od_w_in, od_pool_w, od_pool_scale, od_sgu_norm, od_sgu_w, od_sgu_b, od_w_out, final_norm, loss_target, m_ffn1_norm, m_ffn1_w_in, m_ffn1_w_out, m_mix_norm, m_ffn2_norm, m_ffn2_w_in, m_ffn2_w_out, m_ev_w_in, m_ev_conv_w, m_ev_q_norm, m_ev_k_norm, m_ev_w_out, m_od_w_in, m_od_pool_w, m_od_pool_scale, m_od_sgu_norm, m_od_sgu_w, m_od_sgu_b, m_od_w_out, m_final_norm, v_ffn1_norm, v_ffn1_w_in, v_ffn1_w_out, v_mix_norm, v_ffn2_norm, v_ffn2_w_in, v_ffn2_w_out, v_ev_w_in, v_ev_conv_w, v_ev_q_norm, v_ev_k_norm, v_ev_w_out, v_od_w_in, v_od_pool_w, v_od_pool_scale, v_od_sgu_norm, v_od_sgu_w, v_od_sgu_b, v_od_w_out, v_final_norm):
    w = dict(zip(_WEIGHTS, (ffn1_norm, ffn1_w_in, ffn1_w_out, mix_norm, ffn2_norm, ffn2_w_in, ffn2_w_out, ev_w_in, ev_conv_w, ev_q_norm, ev_k_norm, ev_w_out, od_w_in, od_pool_w, od_pool_scale, od_sgu_norm, od_sgu_w, od_sgu_b, od_w_out, final_norm)))
    m = dict(zip(_WEIGHTS, (m_ffn1_norm, m_ffn1_w_in, m_ffn1_w_out, m_mix_norm, m_ffn2_norm, m_ffn2_w_in, m_ffn2_w_out, m_ev_w_in, m_ev_conv_w, m_ev_q_norm, m_ev_k_norm, m_ev_w_out, m_od_w_in, m_od_pool_w, m_od_pool_scale, m_od_sgu_norm, m_od_sgu_w, m_od_sgu_b, m_od_w_out, m_final_norm)))
    v = dict(zip(_WEIGHTS, (v_ffn1_norm, v_ffn1_w_in, v_ffn1_w_out, v_mix_norm, v_ffn2_norm, v_ffn2_w_in, v_ffn2_w_out, v_ev_w_in, v_ev_conv_w, v_ev_q_norm, v_ev_k_norm, v_ev_w_out, v_od_w_in, v_od_pool_w, v_od_pool_scale, v_od_sgu_norm, v_od_sgu_w, v_od_sgu_b, v_od_w_out, v_final_norm)))
    n_ex, seq, D = x.shape
    T = n_ex * seq
    L = ffn1_norm.shape[0]
    me = 4 * lax.axis_index("x") + 2 * lax.axis_index("y") + lax.axis_index("c")

    sh_small = [w[n] for n in _SMALL_SHARDED]
    packed = all_gather([(_pack(sh_small), None)])[0].reshape(N_DEV, -1)
    full_small = {}
    off = 0
    for n, a in zip(_SMALL_SHARDED, sh_small):
        cnt = math.prod(a.shape)
        full_small[n] = _unshard_last(packed[:, off:off + cnt].reshape((N_DEV,) + a.shape), a.shape[:-1])
        off += cnt

    shards = {n: cast_shards(w[n], t) for n, t in _BIG.items()}
    layers = []
    for l in range(L):
        j = l // 2
        mx = "ev" if l % 2 == 0 else "od"
        f1i, f1o, f2i, f2o, mxi, mxo = all_gather([
            (shards["ffn1_w_in"], l), (shards["ffn1_w_out"], l), (shards["ffn2_w_in"], l), (shards["ffn2_w_out"], l),
            (shards[mx + "_w_in"], j), (shards[mx + "_w_out"], j)])
        W = dict(n1=ffn1_norm[l][None], nm=mix_norm[l][None], n2=ffn2_norm[l][None],
                 f1_in_t=f1i, f1_out=f1o, f2_in_t=f2i, f2_out=f2o, mx_in_t=mxi, mx_out=mxo)
        if l % 2 == 0:
            W.update(conv_w=full_small["ev_conv_w"][j], q_norm=ev_q_norm[j], k_norm=ev_k_norm[j])
        else:
            W.update(pool_w=od_pool_w[j], pool_scale=full_small["od_pool_scale"][j], sgu_norm=full_small["od_sgu_norm"][j],
                     sgu_w=od_sgu_w[j], sgu_b=od_sgu_b[j])
        layers.append(W)

    loss, dx, big, small, d_final = _local_step(x.reshape(T, D), loss_target.reshape(T, D), layers, final_norm[None], n_ex)

    res = {n: [] for n in _BIG}
    for l in range(L):
        j = l // 2
        mx = "ev" if l % 2 == 0 else "od"
        g = big[l]
        parts = reduce_scatter([g["f1_in_t"], g["f1_out"], g["f2_in_t"], g["f2_out"], g["mx_in_t"], g["mx_out"]])
        for name, p, idx in zip(("ffn1_w_in", "ffn1_w_out", "ffn2_w_in", "ffn2_w_out", mx + "_w_in", mx + "_w_out"),
                                parts, (l, l, l, l, j, j)):
            res[name].append(adamw(p, w[name], m[name], v[name], idx, _BIG[name]))
    out = {n: [jnp.stack([r[i] for r in rs]) for i in range(4)] for n, rs in res.items()}

    ev = [sm for l, sm in enumerate(small) if l % 2 == 0]
    od = [sm for l, sm in enumerate(small) if l % 2 == 1]
    st = lambda sms, k: jnp.stack([sm[k] for sm in sms])
    g_full = dict(ffn1_norm=st(small, "n1")[:, 0], mix_norm=st(small, "nm")[:, 0], ffn2_norm=st(small, "n2")[:, 0],
                  ev_conv_w=st(ev, "conv_w"), ev_q_norm=st(ev, "q_norm"), ev_k_norm=st(ev, "k_norm"),
                  od_pool_w=st(od, "pool_w"), od_pool_scale=st(od, "pool_scale"), od_sgu_norm=st(od, "sgu_norm"),
                  od_sgu_w=st(od, "sgu_w"), od_sgu_b=st(od, "sgu_b"), final_norm=d_final[0])
    summed = _unpack(all_reduce_small(_pack([g_full[n] for n in _SMALL])), [g_full[n].shape for n in _SMALL])
    g_small = {}
    for n, g in zip(_SMALL, summed):
        if n in _SMALL_SHARDED:
            width = w[n].shape[-1]
            g = lax.dynamic_slice_in_dim(g, me * width, width, axis=g.ndim - 1)
        g_small[n] = g
    pk = lambda d: _pack([d[n] for n in _SMALL])[None]
    small_out = adamw(pk(g_small), pk(w), pk(m), pk(v), 0, False)
    shapes = [w[n].shape for n in _SMALL]
    for i in range(4):
        for n, a in zip(_SMALL, _unpack(small_out[i], shapes)):
            out.setdefault(n, [None] * 4)[i] = a

    total = lax.psum(loss[0, 0], ("x", "y", "c"))
    return (total, dx.reshape(n_ex, seq, D), *[out[n][0] for n in _WEIGHTS], *[out[n][1] for n in _WEIGHTS],
            *[out[n][2] for n in _WEIGHTS], *[out[n][3] for n in _WEIGHTS])
```

```python
import functools
import math

import jax
import jax.numpy as jnp
from jax import lax
from jax.experimental import pallas as pl
from jax.experimental.pallas import tpu as pltpu

F32, BF16 = jnp.float32, jnp.bfloat16
EPS = 1e-6
N_DEV = 8
V7X_VMEM_BYTES = 64 * 1024 * 1024
VMEM_LIMIT = V7X_VMEM_BYTES - 8 * 1024 * 1024
LANES = 128
HEAD_DIM = 64
N_Q_HEADS = 8
N_KV_HEADS = 2
Q_PER_KV = N_Q_HEADS // N_KV_HEADS
GRID_W = 64
ROPE_THETA = 10000.0
POOL_RADII = (1, 2, 4, 8)
SGU_CHUNK = 128
GROUP = 128
ADAM_LR, ADAM_B1, ADAM_B2, ADAM_EPS, ADAM_WD, ADAM_STEP = 0.001, 0.9, 0.999, 1e-08, 0.01, 10
MESH_ID = pl.DeviceIdType.MESH


def _cp(*sem):
    return pltpu.CompilerParams(dimension_semantics=sem, vmem_limit_bytes=VMEM_LIMIT)


def _dot(a, b, ca, cb):
    return lax.dot_general(a, b, (((ca,), (cb,)), ((), ())), preferred_element_type=F32)


def _nn(a, b):
    return _dot(a, b, 1, 0)


def _nt(a, b):
    return _dot(a, b, 1, 1)


def _tn(a, b):
    return _dot(a, b, 0, 0)


def _split_mm(x, m):
    hi = x.astype(BF16)
    lo = (x - hi.astype(F32)).astype(BF16)
    return _nn(hi, m) + _nn(lo, m)


def _tile(n, pref):
    t = min(n, pref)
    assert n % t == 0, (n, pref)
    return t


ANY = pl.BlockSpec(memory_space=pl.ANY)
OTHER_CHIPS = (2, 4, 6)


def _my_place():
    x, y, c = lax.axis_index("x"), lax.axis_index("y"), lax.axis_index("c")
    return x, y, c, 4 * x + 2 * y + c


def _peer(x, y, c, k):
    px = 1 - x if k & 4 else x
    py = 1 - y if k & 2 else y
    pc = 1 - c if k & 1 else c
    return (px, py, pc), 4 * px + 2 * py + pc


def _remote(src, dst, send_sems, recv_sems, i, peer):
    return pltpu.make_async_remote_copy(src_ref=src, dst_ref=dst, send_sem=send_sems.at[i], recv_sem=recv_sems.at[i],
                                        device_id=peer, device_id_type=MESH_ID)


class GatherJob:
    def __init__(self, srcs):
        self.srcs = srcs
        self.args = [a for a, _ in srcs]
        self.dims = [a.shape[-2:] for a, _ in srcs]
        n = self.n_in = self.n_out = len(srcs)
        self.out_shape = [jax.ShapeDtypeStruct((N_DEV * r, cc), a.dtype) for (a, _), (r, cc) in zip(srcs, self.dims)]
        self.scratch = [pltpu.SemaphoreType.DMA((N_DEV * n,)), pltpu.SemaphoreType.DMA((N_DEV * n,)),
                        pltpu.SemaphoreType.DMA((n,))]

    def _rows(self, outs, t, idx):
        r = self.dims[t][0]
        return outs[t].at[pl.ds(pl.multiple_of(idx * r, 8), r), :]

    def _local(self, ins, outs, loc, t, me):
        src = ins[t] if self.srcs[t][1] is None else ins[t].at[self.srcs[t][1]]
        return src, pltpu.make_async_copy(src, self._rows(outs, t, me), loc.at[t])

    def start(self, ins, outs, sems):
        send, recv, loc = sems
        x, y, c, me = _my_place()
        for t in range(self.n_in):
            src, local = self._local(ins, outs, loc, t, me)
            local.start()
            for k in OTHER_CHIPS + (1,):
                _remote(src, self._rows(outs, t, me), send, recv, N_DEV * t + k, _peer(x, y, c, k)[0]).start()

    def finish(self, ins, outs, sems):
        send, recv, loc = sems
        x, y, c, me = _my_place()
        sibling = _peer(x, y, c, 1)[0]
        for t in range(self.n_in):
            for k in OTHER_CHIPS:
                peer, pidx = _peer(x, y, c, k)
                blk = self._rows(outs, t, pidx)
                _remote(blk, blk, send, recv, N_DEV * t + k, peer).wait_recv()
                _remote(blk, blk, send, recv, N_DEV * t + k + 1, sibling).start()
        for t in range(self.n_in):
            for k in range(1, N_DEV):
                peer, pidx = _peer(x, y, c, k)
                blk = self._rows(outs, t, pidx)
                if k % 2 == 1:
                    _remote(blk, blk, send, recv, N_DEV * t + k, peer).wait_recv()
                _remote(blk, blk, send, recv, N_DEV * t + k, peer).wait_send()
            self._local(ins, outs, loc, t, me)[1].wait()


class ScatterJob:
    def __init__(self, grads):
        self.args = list(grads)
        self.dims = [(g.shape[0] // N_DEV, g.shape[1]) for g in grads]
        n = self.n_in = self.n_out = len(grads)
        self.out_shape = [jax.ShapeDtypeStruct((N_DEV, r, cc), g.dtype) for g, (r, cc) in zip(grads, self.dims)]
        self.scratch = [pltpu.SemaphoreType.DMA((N_DEV * n,)), pltpu.SemaphoreType.DMA((N_DEV * n,)),
                        pltpu.SemaphoreType.DMA((n,))]

    def _rows(self, ins, t, idx):
        r = self.dims[t][0]
        return ins[t].at[pl.ds(pl.multiple_of(idx * r, 8), r), :]

    def start(self, ins, outs, sems):
        send, recv, loc = sems
        x, y, c, me = _my_place()
        for t in range(self.n_in):
            pltpu.make_async_copy(self._rows(ins, t, me), outs[t].at[me], loc.at[t]).start()
            for k in OTHER_CHIPS + (1, 3, 5, 7):
                peer, pidx = _peer(x, y, c, k)
                _remote(self._rows(ins, t, pidx), outs[t].at[me], send, recv, N_DEV * t + k, peer).start()

    def finish(self, ins, outs, sems):
        send, recv, loc = sems
        x, y, c, me = _my_place()
        for t in range(self.n_in):
            for k in range(1, N_DEV):
                peer, pidx = _peer(x, y, c, k)
                cp = _remote(self._rows(ins, t, pidx), outs[t].at[pidx], send, recv, N_DEV * t + k, peer)
                cp.wait_recv()
                cp.wait_send()
            pltpu.make_async_copy(self._rows(ins, t, me), outs[t].at[me], loc.at[t]).wait()


def _call(body, name, grid, in_specs, args, out_specs, out_shape, scratch=(), sem=(), job=None, aliases=None):
    in_specs, out_specs, out_shape, scratch = list(in_specs), list(out_specs), list(out_shape), list(scratch)
    n_in, n_out, n_scr = len(args), len(out_shape), len(scratch)
    if job is None:
        res = pl.pallas_call(body, name=name, grid=grid, in_specs=in_specs, out_specs=out_specs, out_shape=out_shape,
                             scratch_shapes=scratch, input_output_aliases=aliases or {}, compiler_params=_cp(*sem))(*args)
        return res, None
    o0 = n_in + job.n_in
    s0 = o0 + n_out + job.n_out

    def carrier(*refs):
        jin, jout, jsem = refs[n_in:o0], refs[o0 + n_out:s0], refs[s0 + n_scr:]
        ids = [pl.program_id(a) for a in range(len(grid))]
        if grid:
            first = functools.reduce(jnp.logical_and, [i == 0 for i in ids])
            last = functools.reduce(jnp.logical_and, [i == g - 1 for i, g in zip(ids, grid)])
            pl.when(first)(lambda: job.start(jin, jout, jsem))
        else:
            job.start(jin, jout, jsem)
        body(*refs[:n_in], *refs[o0:o0 + n_out], *refs[s0:s0 + n_scr])
        if grid:
            pl.when(last)(lambda: job.finish(jin, jout, jsem))
        else:
            job.finish(jin, jout, jsem)

    res = pl.pallas_call(
        carrier, name=name + "_comm", grid=grid, in_specs=in_specs + [ANY] * job.n_in,
        out_specs=out_specs + [ANY] * job.n_out, out_shape=out_shape + job.out_shape,
        scratch_shapes=scratch + job.scratch, input_output_aliases=aliases or {},
        compiler_params=_cp(*(["arbitrary"] * len(grid))))(*args, *job.args)
    return res[:n_out], res[n_out:]


def run_job(job, name):
    return _call(lambda: None, name, (), [], [], [], [], job=job)[1]


@jax.custom_vjp
def bmm(x, w):
    return _nn(x.astype(BF16), w.astype(BF16))


def _bmm_fwd(x, w):
    return bmm(x, w), (x, w)


def _bmm_bwd(res, g):
    x, w = res
    gb = g.astype(BF16)
    return _nt(gb, w.astype(BF16)), _tn(x.astype(BF16), gb)


bmm.defvjp(_bmm_fwd, _bmm_bwd)


def _shift_raw(x, d):
    n = x.shape[0]
    r = pltpu.roll(x, d % n, axis=0)
    row = lax.broadcasted_iota(jnp.int32, x.shape, 0)
    keep = (row >= d) if d > 0 else (row < n + d)
    return jnp.where(keep, r, 0.0)


def shift_rows(x, d):
    @jax.custom_vjp
    def f(v):
        return _shift_raw(v, d)

    f.defvjp(lambda v: (_shift_raw(v, d), None), lambda _, g: (_shift_raw(g, -d),))
    return f(x)


def _swap_raw(x):
    n = x.shape[1]
    nxt = pltpu.roll(x, n - 1, axis=1)
    prv = pltpu.roll(x, 1, axis=1)
    lane = lax.broadcasted_iota(jnp.int32, x.shape, 1)
    return jnp.where(lane % 2 == 0, nxt, prv)


@jax.custom_vjp
def swap_pairs(x):
    return _swap_raw(x)


swap_pairs.defvjp(lambda x: (_swap_raw(x), None), lambda _, g: (_swap_raw(g),))


@jax.custom_vjp
def group_mean(x, bd):
    return _split_mm(x, bd)


group_mean.defvjp(lambda x, bd: (_split_mm(x, bd), bd), lambda bd, g: (_split_mm(g, bd), jnp.zeros_like(bd)))


def _rope_norm(x, gain, cos, sgn, bd, scale):
    xn = x * lax.rsqrt(group_mean(x * x, bd) + EPS) * gain
    return (xn * cos + swap_pairs(xn) * sgn) * scale


def _conv_gate(gb, gc, hc, w):
    z = gc * hc
    c = shift_rows(z, 1) * w[0:1] + z * w[1:2] + shift_rows(z, -1) * w[2:3]
    return gb * c


def _window_sum(p, r):
    b = f = p
    k = 1
    while k < r:
        b = b + shift_rows(b, k)
        f = f + shift_rows(f, -k)
        k *= 2
    return b + f - p + shift_rows(p, r) + shift_rows(p, -r)


def _pool_mix(p, pool_w, scale):
    n = p.shape[0]
    t = lax.broadcasted_iota(jnp.int32, (n, 1), 0)
    outs = []
    for gi, r in enumerate(POOL_RADII):
        pg = p[:, gi * GROUP:(gi + 1) * GROUP]
        cnt = (jnp.minimum(t + r, n - 1) - jnp.maximum(t - r, 0) + 1).astype(F32)
        pooled = _window_sum(pg, r) / cnt - pg
        outs.append(bmm(pooled, pool_w[gi]))
    return jnp.concatenate(outs, axis=1) * scale


def _sgu(u, v, norm_g, w_s, b_full):
    ug = jax.nn.gelu(u)
    vg = jax.nn.gelu(v)
    vn = vg * lax.rsqrt(jnp.mean(vg * vg, axis=-1, keepdims=True) + EPS) * norm_g
    cols = []
    for g in range(w_s.shape[0]):
        rows = []
        for n in range(u.shape[0] // SGU_CHUNK):
            blk = vn[n * SGU_CHUNK:(n + 1) * SGU_CHUNK, g * GROUP:(g + 1) * GROUP]
            rows.append(bmm(w_s[g], blk) + b_full[g])
        cols.append(jnp.concatenate(rows, axis=0))
    return ug * jnp.concatenate(cols, axis=1)


def rms_fwd(x, gain):
    T, D = x.shape
    tm = _tile(T, 512)

    def body(x_ref, g_ref, o_ref):
        xv = x_ref[...]
        r = lax.rsqrt(jnp.mean(xv * xv, axis=-1, keepdims=True) + EPS)
        o_ref[...] = (xv * r * g_ref[...]).astype(BF16)

    return pl.pallas_call(
        body, name="rms_fwd", grid=(T // tm,),
        in_specs=[pl.BlockSpec((tm, D), lambda i: (i, 0)), pl.BlockSpec((1, D), lambda i: (0, 0))],
        out_specs=pl.BlockSpec((tm, D), lambda i: (i, 0)),
        out_shape=jax.ShapeDtypeStruct((T, D), BF16), compiler_params=_cp("parallel"),
    )(x, gain)


def _rms_bwd_math(xv, gain, dy, dres):
    r = lax.rsqrt(jnp.mean(xv * xv, axis=-1, keepdims=True) + EPS)
    xh = xv * r
    dxh = dy * gain
    dx = dres + r * (dxh - xh * jnp.mean(dxh * xh, axis=-1, keepdims=True))
    return dx, jnp.sum(dy * xh, axis=0, keepdims=True)


def rms_bwd(x, gain, dy, dres):
    T, D = x.shape
    tm = _tile(T, 512)

    def body(x_ref, g_ref, dy_ref, dr_ref, dx_ref, dg_ref):
        dx, dg = _rms_bwd_math(x_ref[...], g_ref[...], dy_ref[...], dr_ref[...])
        dx_ref[...] = dx

        @pl.when(pl.program_id(0) == 0)
        def _():
            dg_ref[...] = jnp.zeros_like(dg_ref)

        dg_ref[...] += dg

    row = pl.BlockSpec((tm, D), lambda i: (i, 0))
    return pl.pallas_call(
        body, name="rms_bwd", grid=(T // tm,),
        in_specs=[row, pl.BlockSpec((1, D), lambda i: (0, 0)), row, row],
        out_specs=[row, pl.BlockSpec((1, D), lambda i: (0, 0))],
        out_shape=[jax.ShapeDtypeStruct((T, D), F32), jax.ShapeDtypeStruct((1, D), F32)],
        compiler_params=_cp("arbitrary"),
    )(x, gain, dy, dres)


FFN_TN = 256


def ffn_fwd(x, gain, wt_in, w_out, job=None):
    T, D = x.shape
    F = w_out.shape[0]
    tm, tn = _tile(T, 1024), FFN_TN
    nc = F // tn

    def body(x_ref, gn_ref, wg_ref, wu_ref, wo_ref, y_ref, gu_ref, xn_s, acc_s):
        c = pl.program_id(1)

        @pl.when(c == 0)
        def _():
            xv = x_ref[...]
            r = lax.rsqrt(jnp.mean(xv * xv, axis=-1, keepdims=True) + EPS)
            xn_s[...] = (xv * r * gn_ref[...]).astype(BF16)
            acc_s[...] = jnp.zeros_like(acc_s)

        xn = xn_s[...]
        g = _nt(xn, wg_ref[...])
        u = _nt(xn, wu_ref[...])
        gu_ref[:, :tn] = g.astype(BF16)
        gu_ref[:, tn:] = u.astype(BF16)
        a = (g * jax.nn.sigmoid(g) * u).astype(BF16)
        acc_s[...] += _nn(a, wo_ref[...])

        @pl.when(c == nc - 1)
        def _():
            y_ref[...] = x_ref[...] + 0.5 * acc_s[...]

    row = pl.BlockSpec((tm, D), lambda i, c: (i, 0))
    return _call(
        body, "ffn_fwd", (T // tm, nc),
        [row, pl.BlockSpec((1, D), lambda i, c: (0, 0)),
         pl.BlockSpec((tn, D), lambda i, c: (c, 0)),
         pl.BlockSpec((tn, D), lambda i, c: (c + nc, 0)),
         pl.BlockSpec((tn, D), lambda i, c: (c, 0))],
        [x, gain, wt_in, wt_in, w_out],
        [row, pl.BlockSpec((tm, 2 * tn), lambda i, c: (i, c))],
        [jax.ShapeDtypeStruct((T, D), F32), jax.ShapeDtypeStruct((T, 2 * F), BF16)],
        [pltpu.VMEM((tm, D), BF16), pltpu.VMEM((tm, D), F32)], ("parallel", "arbitrary"), job)


def ffn_bwd_x(dout, x, gain, gu, wt_in, w_out, job=None):
    T, D = x.shape
    F = w_out.shape[0]
    tm, tn = _tile(T, 1024), FFN_TN
    nc = F // tn

    def body(do_ref, x_ref, gn_ref, gu_ref, wg_ref, wu_ref, wo_ref,
             dx_ref, dgn_ref, a_ref, dgu_ref, xn_ref, dob_ref, acc_s):
        i, c = pl.program_id(0), pl.program_id(1)

        @pl.when(c == 0)
        def _():
            xv = x_ref[...]
            r = lax.rsqrt(jnp.mean(xv * xv, axis=-1, keepdims=True) + EPS)
            xn_ref[...] = (xv * r * gn_ref[...]).astype(BF16)
            dob_ref[...] = (0.5 * do_ref[...]).astype(BF16)
            acc_s[...] = jnp.zeros_like(acc_s)

        da = _nt(dob_ref[...], wo_ref[...])
        g = gu_ref[:, :tn].astype(F32)
        u = gu_ref[:, tn:].astype(F32)
        sig = jax.nn.sigmoid(g)
        sl = g * sig
        a_ref[...] = (sl * u).astype(BF16)
        dg = (da * u * (sig * (1.0 + g * (1.0 - sig)))).astype(BF16)
        du = (da * sl).astype(BF16)
        dgu_ref[:, :tn] = dg
        dgu_ref[:, tn:] = du
        acc_s[...] += _nn(dg, wg_ref[...]) + _nn(du, wu_ref[...])

        @pl.when(c == nc - 1)
        def _():
            dx, dgn = _rms_bwd_math(x_ref[...], gn_ref[...], acc_s[...], do_ref[...])
            dx_ref[...] = dx

            @pl.when(i == 0)
            def _():
                dgn_ref[...] = jnp.zeros_like(dgn_ref)

            dgn_ref[...] += dgn

    row = pl.BlockSpec((tm, D), lambda i, c: (i, 0))
    return _call(
        body, "ffn_bwd_x", (T // tm, nc),
        [row, row, pl.BlockSpec((1, D), lambda i, c: (0, 0)),
         pl.BlockSpec((tm, 2 * tn), lambda i, c: (i, c)),
         pl.BlockSpec((tn, D), lambda i, c: (c, 0)),
         pl.BlockSpec((tn, D), lambda i, c: (c + nc, 0)),
         pl.BlockSpec((tn, D), lambda i, c: (c, 0))],
        [dout, x, gain, gu, wt_in, wt_in, w_out],
        [row, pl.BlockSpec((1, D), lambda i, c: (0, 0)),
         pl.BlockSpec((tm, tn), lambda i, c: (i, c)),
         pl.BlockSpec((tm, 2 * tn), lambda i, c: (i, c)), row, row],
        [jax.ShapeDtypeStruct((T, D), F32), jax.ShapeDtypeStruct((1, D), F32),
         jax.ShapeDtypeStruct((T, F), BF16), jax.ShapeDtypeStruct((T, 2 * F), BF16),
         jax.ShapeDtypeStruct((T, D), BF16), jax.ShapeDtypeStruct((T, D), BF16)],
        [pltpu.VMEM((tm, D), F32)], ("arbitrary", "arbitrary"), job)


MM_TM = 512
MM_TC = 256


def mm_nt(a, wt, pieces, out_dtype, emit_a_bf16=False):
    T, K = a.shape
    tm = _tile(T, MM_TM)
    npc = len(pieces)

    def body(*refs):
        a_ref, w_refs, o_refs = refs[0], refs[1:1 + npc], refs[1 + npc:]
        ab = a_ref[...].astype(BF16)
        for w_ref, o_ref in zip(w_refs, o_refs[:npc]):
            o_ref[...] = _nt(ab, w_ref[...]).astype(o_ref.dtype)
        if emit_a_bf16:
            o_refs[npc][...] = ab

    in_specs = [pl.BlockSpec((tm, K), lambda i: (i, 0))]
    out_specs, out_shape = [], []
    for r0, n in pieces:
        assert r0 % n == 0
        in_specs.append(pl.BlockSpec((n, K), functools.partial(lambda i, b: (b, 0), b=r0 // n)))
        out_specs.append(pl.BlockSpec((tm, n), lambda i: (i, 0)))
        out_shape.append(jax.ShapeDtypeStruct((T, n), out_dtype))
    if emit_a_bf16:
        out_specs.append(pl.BlockSpec((tm, K), lambda i: (i, 0)))
        out_shape.append(jax.ShapeDtypeStruct((T, K), BF16))
    return pl.pallas_call(
        body, name="mm_nt", grid=(T // tm,), in_specs=in_specs, out_specs=out_specs, out_shape=out_shape,
        compiler_params=_cp("parallel"),
    )(a, *([wt] * npc))


def mm_nn(a_list, w, residual=None):
    T = a_list[0].shape[0]
    N = w.shape[1]
    tm = _tile(T, MM_TM)
    na = len(a_list)

    def body(*refs):
        a_refs, w_refs = refs[:na], refs[na:2 * na]
        o_ref = refs[-1]
        acc = refs[2 * na][...] if residual is not None else None
        for a_ref, w_ref in zip(a_refs, w_refs):
            t = _nn(a_ref[...].astype(BF16), w_ref[...])
            acc = t if acc is None else acc + t
        o_ref[...] = acc

    in_specs, r0 = [], 0
    w_specs = []
    for a in a_list:
        k = a.shape[1]
        assert r0 % k == 0
        in_specs.append(pl.BlockSpec((tm, k), lambda i: (i, 0)))
        w_specs.append(pl.BlockSpec((k, N), functools.partial(lambda i, b: (b, 0), b=r0 // k)))
        r0 += k
    assert r0 == w.shape[0]
    args = list(a_list) + [w] * na
    in_specs = in_specs + w_specs
    if residual is not None:
        in_specs.append(pl.BlockSpec((tm, N), lambda i: (i, 0)))
        args.append(residual)
    return pl.pallas_call(
        body, name="mm_nn", grid=(T // tm,), in_specs=in_specs,
        out_specs=pl.BlockSpec((tm, N), lambda i: (i, 0)),
        out_shape=jax.ShapeDtypeStruct((T, N), F32), compiler_params=_cp("parallel"),
    )(*args)


def mm_tn(a, b, n_rows, row_block, prev=None, grid=None, col_block=None, job=None):
    T, M = a.shape
    N = b.shape[1]
    tc = MM_TC
    assert M % tc == 0 and n_rows % tc == 0
    if grid is None:
        grid, col_block = (M // tc,), (lambda j: j)

    def body(*refs):
        a_ref, b_ref, o_ref = refs[0], refs[1], refs[-1]
        o_ref[...] = _tn(a_ref[...], b_ref[...]).astype(BF16)

    in_specs = [pl.BlockSpec((T, tc), lambda *g: (0, col_block(*g))), pl.BlockSpec((T, N), lambda *g: (0, 0))]
    args = [a, b]
    aliases = {}
    if prev is not None:
        in_specs.append(pl.BlockSpec(memory_space=pl.ANY))
        args.append(prev)
        aliases = {2: 0}
    res, jres = _call(body, "mm_tn", grid, in_specs, args, [pl.BlockSpec((tc, N), lambda *g: (row_block(*g), 0))],
                      [jax.ShapeDtypeStruct((n_rows, N), BF16)], (), ["parallel"] * len(grid), job, aliases)
    return res[0] if job is None else (res[0], jres)


def conv_fwd(proj_a, conv_w, n_ex):
    T, C3 = proj_a.shape
    C = C3 // 3
    S = T // n_ex

    def body(gb_ref, gc_ref, hc_ref, w_ref, o_ref):
        o_ref[...] = _conv_gate(gb_ref[...], gc_ref[...], hc_ref[...], w_ref[...]).astype(BF16)

    col = lambda k: pl.BlockSpec((S, C), functools.partial(lambda b, kk: (b, kk), kk=k))
    return pl.pallas_call(
        body, name="conv_fwd", grid=(n_ex,),
        in_specs=[col(0), col(1), col(2), pl.BlockSpec((3, C), lambda b: (0, 0))],
        out_specs=pl.BlockSpec((S, C), lambda b: (b, 0)),
        out_shape=jax.ShapeDtypeStruct((T, C), BF16), compiler_params=_cp("parallel"),
    )(proj_a, proj_a, proj_a, conv_w)


def conv_bwd(proj_a, conv_w, n_ex, dy):
    T, C3 = proj_a.shape
    C = C3 // 3
    S = T // n_ex

    def body(gb_ref, gc_ref, hc_ref, w_ref, dy_ref, dp_ref, dw_ref):
        _, vjp = jax.vjp(_conv_gate, gb_ref[...], gc_ref[...], hc_ref[...], w_ref[...])
        dgb, dgc, dhc, dw = vjp(dy_ref[...].astype(F32))
        dp_ref[:, 0:C] = dgb.astype(BF16)
        dp_ref[:, C:2 * C] = dgc.astype(BF16)
        dp_ref[:, 2 * C:] = dhc.astype(BF16)

        @pl.when(pl.program_id(0) == 0)
        def _():
            dw_ref[...] = jnp.zeros_like(dw_ref)

        dw_ref[...] += dw

    col = lambda k: pl.BlockSpec((S, C), functools.partial(lambda b, kk: (b, kk), kk=k))
    return pl.pallas_call(
        body, name="conv_bwd", grid=(n_ex,),
        in_specs=[col(0), col(1), col(2), pl.BlockSpec((3, C), lambda b: (0, 0)),
                  pl.BlockSpec((S, C), lambda b: (b, 0))],
        out_specs=[pl.BlockSpec((S, C3), lambda b: (b, 0)), pl.BlockSpec((3, C), lambda b: (0, 0))],
        out_shape=[jax.ShapeDtypeStruct((T, C3), BF16), jax.ShapeDtypeStruct((3, C), F32)],
        compiler_params=_cp("arbitrary"),
    )(proj_a, proj_a, proj_a, conv_w, dy)


QW = N_Q_HEADS * HEAD_DIM
KW = N_KV_HEADS * HEAD_DIM
QP = N_Q_HEADS * LANES
KP = N_KV_HEADS * LANES


def _attn_consts(seq):
    rows = seq // GRID_W
    r_idx, c_idx = jnp.meshgrid(jnp.arange(rows), jnp.arange(GRID_W), indexing='ij')
    r_idx = r_idx.reshape(-1).astype(F32)
    c_idx = c_idx.reshape(-1).astype(F32)
    n_freq = HEAD_DIM // 4
    inv = ROPE_THETA ** (-jnp.arange(n_freq, dtype=F32) / n_freq)
    ang = jnp.concatenate([r_idx[:, None] * inv, c_idx[:, None] * inv], axis=-1)
    cos = jnp.repeat(jnp.cos(ang), 2, axis=1)
    sin = jnp.repeat(jnp.sin(ang), 2, axis=1)
    sgn = sin * jnp.tile(jnp.array([-1.0, 1.0], F32), HEAD_DIM // 2)
    cos = jnp.tile(cos, (1, N_Q_HEADS))
    sgn = jnp.tile(sgn, (1, N_Q_HEADS))
    lane = jnp.arange(QW)
    bd = jnp.where(lane[:, None] // HEAD_DIM == lane[None, :] // HEAD_DIM, 1.0 / HEAD_DIM, 0.0).astype(BF16)
    dst = (lane // HEAD_DIM) * LANES + lane % HEAD_DIM
    spread = (dst[:, None] == jnp.arange(QP)[None, :]).astype(BF16)
    return dict(cos=cos, sgn=sgn, bd=bd, spread=spread, gather=spread.T)


def qkv_prep_fwd(proj_b, qg, kg, cst, n_ex):
    T = proj_b.shape[0]
    S = T // n_ex
    tm = _tile(S, 512)
    nb = S // tm

    def body(p_ref, qg_ref, kg_ref, cos_ref, sgn_ref, bd_ref, sp_ref, q_ref, k_ref, v_ref):
        pv = p_ref[...]
        cos, sgn, bd, sp = cos_ref[...], sgn_ref[...], bd_ref[...], sp_ref[...]
        qr = _rope_norm(pv[:, :QW], qg_ref[...], cos, sgn, bd, HEAD_DIM ** -0.5)
        kr = _rope_norm(pv[:, QW:QW + KW], kg_ref[...], cos[:, :KW], sgn[:, :KW], bd[:KW, :KW], 1.0)
        q_ref[...] = _nn(qr.astype(BF16), sp).astype(BF16)
        k_ref[...] = _nn(kr.astype(BF16), sp[:KW, :KP]).astype(BF16)
        v_ref[...] = _nn(pv[:, QW + KW:].astype(BF16), sp[:KW, :KP]).astype(BF16)

    full = lambda a: pl.BlockSpec(a.shape, lambda i: (0,) * a.ndim)
    tab = pl.BlockSpec((tm, QW), lambda i: (i % nb, 0))
    return pl.pallas_call(
        body, name="qkv_prep_fwd", grid=(T // tm,),
        in_specs=[pl.BlockSpec((tm, QW + 2 * KW), lambda i: (i, 0)), full(qg), full(kg), tab, tab,
                  full(cst["bd"]), full(cst["spread"])],
        out_specs=[pl.BlockSpec((tm, QP), lambda i: (i, 0)), pl.BlockSpec((tm, KP), lambda i: (i, 0)),
                   pl.BlockSpec((tm, KP), lambda i: (i, 0))],
        out_shape=[jax.ShapeDtypeStruct((T, QP), BF16), jax.ShapeDtypeStruct((T, KP), BF16),
                   jax.ShapeDtypeStruct((T, KP), BF16)],
        compiler_params=_cp("parallel"),
    )(proj_b, qg, kg, cst["cos"], cst["sgn"], cst["bd"], cst["spread"])


def qkv_prep_bwd(proj_b, qg, kg, cst, n_ex, dq, dk_pad, dv_pad):
    T = proj_b.shape[0]
    S = T // n_ex
    tm = _tile(S, 512)
    nb = S // tm

    def body(p_ref, qg_ref, kg_ref, cos_ref, sgn_ref, bd_ref, ga_ref, dq_ref, dk_ref, dv_ref,
             dp_ref, dqg_ref, dkg_ref):
        pv = p_ref[...]
        cos, sgn, bd, ga = cos_ref[...], sgn_ref[...], bd_ref[...], ga_ref[...]
        fq = lambda q, g: _rope_norm(q, g, cos, sgn, bd, HEAD_DIM ** -0.5)
        fk = lambda k, g: _rope_norm(k, g, cos[:, :KW], sgn[:, :KW], bd[:KW, :KW], 1.0)
        _, vq = jax.vjp(fq, pv[:, :QW], qg_ref[...])
        _, vk = jax.vjp(fk, pv[:, QW:QW + KW], kg_ref[...])
        dqp, dqg = vq(dq_ref[...])
        dkp, dkg = vk(_split_mm(dk_ref[...], ga[:KP, :KW]))
        dp_ref[:, :QW] = dqp.astype(BF16)
        dp_ref[:, QW:QW + KW] = dkp.astype(BF16)
        dp_ref[:, QW + KW:] = _split_mm(dv_ref[...], ga[:KP, :KW]).astype(BF16)

        @pl.when(pl.program_id(0) == 0)
        def _():
            dqg_ref[...] = jnp.zeros_like(dqg_ref)
            dkg_ref[...] = jnp.zeros_like(dkg_ref)

        dqg_ref[...] += dqg
        dkg_ref[...] += dkg

    full = lambda a: pl.BlockSpec(a.shape, lambda i: (0,) * a.ndim)
    tab = pl.BlockSpec((tm, QW), lambda i: (i % nb, 0))
    row = lambda n: pl.BlockSpec((tm, n), lambda i: (i, 0))
    return pl.pallas_call(
        body, name="qkv_prep_bwd", grid=(T // tm,),
        in_specs=[row(QW + 2 * KW), full(qg), full(kg), tab, tab, full(cst["bd"]), full(cst["gather"]),
                  row(QW), row(KP), row(KP)],
        out_specs=[row(QW + 2 * KW), pl.BlockSpec((1, QW), lambda i: (0, 0)), pl.BlockSpec((1, KW), lambda i: (0, 0))],
        out_shape=[jax.ShapeDtypeStruct((T, QW + 2 * KW), BF16), jax.ShapeDtypeStruct((1, QW), F32),
                   jax.ShapeDtypeStruct((1, KW), F32)],
        compiler_params=_cp("arbitrary"),
    )(proj_b, qg, kg, cst["cos"], cst["sgn"], cst["bd"], cst["gather"], dq, dk_pad, dv_pad)


ATT_TQ = 256


def attn_fwd(qp, kp, vp, gather, n_ex, job=None):
    T = qp.shape[0]
    S = T // n_ex
    tq = _tile(S, ATT_TQ)
    nq = S // tq

    def body(q_ref, k_ref, v_ref, ga_ref, o_ref, op_ref, lse_ref):
        lane = lax.broadcasted_iota(jnp.int32, (tq, LANES), 1)
        lse_all = jnp.zeros((tq, LANES), F32)
        for h in range(N_Q_HEADS):
            kv = h // Q_PER_KV
            qh = q_ref[:, h * LANES:(h + 1) * LANES]
            s = _nt(qh, k_ref[:, kv * LANES:(kv + 1) * LANES])
            m = jnp.max(s, axis=-1, keepdims=True)
            p = jnp.exp(s - m)
            lsum = jnp.sum(p, axis=-1, keepdims=True)
            o = _nn(p.astype(BF16), v_ref[:, kv * LANES:(kv + 1) * LANES]) / lsum
            op_ref[:, h * LANES:(h + 1) * LANES] = o.astype(BF16)
            lse_all = jnp.where(lane == h, m + jnp.log(lsum), lse_all)
        lse_ref[...] = lse_all
        o_ref[...] = _nn(op_ref[...], ga_ref[...]).astype(BF16)

    blk = lambda n: pl.BlockSpec((tq, n), lambda b, i: (b * nq + i, 0))
    kvs = pl.BlockSpec((S, KP), lambda b, i: (b, 0))
    return _call(
        body, "attn_fwd", (n_ex, nq),
        [blk(QP), kvs, kvs, pl.BlockSpec(gather.shape, lambda b, i: (0, 0))], [qp, kp, vp, gather],
        [blk(QW), blk(QP), blk(LANES)],
        [jax.ShapeDtypeStruct((T, QW), BF16), jax.ShapeDtypeStruct((T, QP), BF16),
         jax.ShapeDtypeStruct((T, LANES), F32)], (), ("parallel", "parallel"), job)


def attn_bwd(qp, kp, vp, op, lse, do, cst, n_ex, job=None):
    T = qp.shape[0]
    S = T // n_ex
    tq = _tile(S, ATT_TQ)
    nq = S // tq

    def body(q_ref, k_ref, v_ref, op_ref, lse_ref, do_ref, sp_ref, ga_ref, dq_ref, dk_ref, dv_ref, dqp_s):
        @pl.when(pl.program_id(1) == 0)
        def _():
            dk_ref[...] = jnp.zeros_like(dk_ref)
            dv_ref[...] = jnp.zeros_like(dv_ref)

        lane = lax.broadcasted_iota(jnp.int32, (tq, LANES), 1)
        dop = _nn(do_ref[...], sp_ref[...]).astype(BF16)
        lse_all = lse_ref[...]
        for h in range(N_Q_HEADS):
            kv = h // Q_PER_KV
            hs = slice(h * LANES, (h + 1) * LANES)
            ks = slice(kv * LANES, (kv + 1) * LANES)
            qh, kk, vv = q_ref[:, hs], k_ref[:, ks], v_ref[:, ks]
            doh = dop[:, hs]
            lse_h = jnp.sum(jnp.where(lane == h, lse_all, 0.0), axis=-1, keepdims=True)
            p = jnp.exp(_nt(qh, kk) - lse_h)
            dp = _nt(doh, vv)
            delta = jnp.sum(doh.astype(F32) * op_ref[:, hs].astype(F32), axis=-1, keepdims=True)
            ds = (p * (dp - delta)).astype(BF16)
            dqp_s[:, hs] = _nn(ds, kk)
            dk_ref[:, ks] += _tn(ds, qh)
            dv_ref[:, ks] += _tn(p.astype(BF16), doh)
        dq_ref[...] = _split_mm(dqp_s[...], ga_ref[...])

    blk = lambda n: pl.BlockSpec((tq, n), lambda b, i: (b * nq + i, 0))
    kvs = pl.BlockSpec((S, KP), lambda b, i: (b, 0))
    full = lambda a: pl.BlockSpec(a.shape, lambda b, i: (0, 0))
    return _call(
        body, "attn_bwd", (n_ex, nq),
        [blk(QP), kvs, kvs, blk(QP), blk(LANES), blk(QW), full(cst["spread"]), full(cst["gather"])],
        [qp, kp, vp, op, lse, do, cst["spread"], cst["gather"]],
        [blk(QW), kvs, kvs],
        [jax.ShapeDtypeStruct((T, QW), F32), jax.ShapeDtypeStruct((T, KP), F32), jax.ShapeDtypeStruct((T, KP), F32)],
        [pltpu.VMEM((tq, QP), F32)], ("arbitrary", "arbitrary"), job)


def pool_fwd(p, pool_w, scale, n_ex):
    T, W = p.shape
    S = T // n_ex

    def body(p_ref, w_ref, s_ref, o_ref):
        o_ref[...] = _pool_mix(p_ref[...], w_ref[...], s_ref[...]).astype(BF16)

    return pl.pallas_call(
        body, name="pool_fwd", grid=(n_ex,),
        in_specs=[pl.BlockSpec((S, W), lambda b: (b, 0)),
                  pl.BlockSpec(pool_w.shape, lambda b: (0, 0, 0)),
                  pl.BlockSpec((1, W), lambda b: (0, 0))],
        out_specs=pl.BlockSpec((S, W), lambda b: (b, 0)),
        out_shape=jax.ShapeDtypeStruct((T, W), BF16), compiler_params=_cp("parallel"),
    )(p, pool_w, scale)


def pool_bwd(p, pool_w, scale, n_ex, dy):
    T, W = p.shape
    S = T // n_ex

    def body(p_ref, w_ref, s_ref, dy_ref, dp_ref, dw_ref, ds_ref):
        _, vjp = jax.vjp(_pool_mix, p_ref[...], w_ref[...], s_ref[...])
        dp, dw, ds = vjp(dy_ref[...].astype(F32))
        dp_ref[...] = dp.astype(BF16)

        @pl.when(pl.program_id(0) == 0)
        def _():
            dw_ref[...] = jnp.zeros_like(dw_ref)
            ds_ref[...] = jnp.zeros_like(ds_ref)

        dw_ref[...] += dw
        ds_ref[...] += ds

    wshape = pool_w.shape
    return pl.pallas_call(
        body, name="pool_bwd", grid=(n_ex,),
        in_specs=[pl.BlockSpec((S, W), lambda b: (b, 0)),
                  pl.BlockSpec(wshape, lambda b: (0, 0, 0)),
                  pl.BlockSpec((1, W), lambda b: (0, 0)), pl.BlockSpec((S, W), lambda b: (b, 0))],
        out_specs=[pl.BlockSpec((S, W), lambda b: (b, 0)), pl.BlockSpec(wshape, lambda b: (0, 0, 0)),
                   pl.BlockSpec((1, W), lambda b: (0, 0))],
        out_shape=[jax.ShapeDtypeStruct((T, W), BF16), jax.ShapeDtypeStruct(wshape, F32),
                   jax.ShapeDtypeStruct((1, W), F32)],
        compiler_params=_cp("arbitrary"),
    )(p, pool_w, scale, dy)


SGU_TS = 512


def sgu_fwd(u, v, norm_g, w_s, b_full):
    T, W = u.shape
    ts = _tile(T, SGU_TS)

    def body(u_ref, v_ref, g_ref, w_ref, b_ref, o_ref):
        o_ref[...] = _sgu(u_ref[...], v_ref[...], g_ref[...], w_ref[...], b_ref[...]).astype(BF16)

    row = pl.BlockSpec((ts, W), lambda i: (i, 0))
    wsp = pl.BlockSpec(w_s.shape, lambda i: (0, 0, 0))
    return pl.pallas_call(
        body, name="sgu_fwd", grid=(T // ts,),
        in_specs=[row, row, pl.BlockSpec((1, W), lambda i: (0, 0)), wsp, wsp],
        out_specs=row, out_shape=jax.ShapeDtypeStruct((T, W), BF16), compiler_params=_cp("parallel"),
    )(u, v, norm_g, w_s, b_full)


def sgu_bwd(u, v, norm_g, w_s, b_full, dy):
    T, W = u.shape
    ts = _tile(T, SGU_TS)
    wshape = w_s.shape

    def body(u_ref, v_ref, g_ref, w_ref, b_ref, dy_ref, du_ref, dv_ref, dg_ref, dw_ref, db_ref):
        _, vjp = jax.vjp(_sgu, u_ref[...], v_ref[...], g_ref[...], w_ref[...], b_ref[...])
        du, dv, dg, dw, db = vjp(dy_ref[...].astype(F32))
        du_ref[...] = du.astype(BF16)
        dv_ref[...] = dv.astype(BF16)

        @pl.when(pl.program_id(0) == 0)
        def _():
            dg_ref[...] = jnp.zeros_like(dg_ref)
            dw_ref[...] = jnp.zeros_like(dw_ref)
            db_ref[...] = jnp.zeros_like(db_ref)

        dg_ref[...] += dg
        dw_ref[...] += dw
        db_ref[...] += db

    row = pl.BlockSpec((ts, W), lambda i: (i, 0))
    wsp = pl.BlockSpec(wshape, lambda i: (0, 0, 0))
    wout = pl.BlockSpec(wshape, lambda i: (0, 0, 0))
    vec = pl.BlockSpec((1, W), lambda i: (0, 0))
    return pl.pallas_call(
        body, name="sgu_bwd", grid=(T // ts,),
        in_specs=[row, row, vec, wsp, wsp, row],
        out_specs=[row, row, vec, wout, wout],
        out_shape=[jax.ShapeDtypeStruct((T, W), BF16), jax.ShapeDtypeStruct((T, W), BF16),
                   jax.ShapeDtypeStruct((1, W), F32), jax.ShapeDtypeStruct(wshape, F32),
                   jax.ShapeDtypeStruct(wshape, F32)],
        compiler_params=_cp("arbitrary"),
    )(u, v, norm_g, w_s, b_full, dy)


def loss_head(x, gain, target):
    T, D = x.shape
    tm = _tile(T, 512)

    def body(x_ref, g_ref, t_ref, loss_ref, dx_ref, dg_ref):
        xv, g = x_ref[...], g_ref[...]
        r = lax.rsqrt(jnp.mean(xv * xv, axis=-1, keepdims=True) + EPS)
        err = xv * r * g - t_ref[...]
        part = 0.5 * jnp.sum(jnp.mean(err * err, axis=-1, keepdims=True), axis=0, keepdims=True)
        dx, dg = _rms_bwd_math(xv, g, err * (1.0 / D), jnp.zeros_like(xv))
        dx_ref[...] = dx

        @pl.when(pl.program_id(0) == 0)
        def _():
            loss_ref[...] = jnp.zeros_like(loss_ref)
            dg_ref[...] = jnp.zeros_like(dg_ref)

        loss_ref[...] += part
        dg_ref[...] += dg

    row = pl.BlockSpec((tm, D), lambda i: (i, 0))
    vec = pl.BlockSpec((1, D), lambda i: (0, 0))
    return pl.pallas_call(
        body, name="loss_head", grid=(T // tm,),
        in_specs=[row, vec, row], out_specs=[pl.BlockSpec((1, 1), lambda i: (0, 0)), row, vec],
        out_shape=[jax.ShapeDtypeStruct((1, 1), F32), jax.ShapeDtypeStruct((T, D), F32),
                   jax.ShapeDtypeStruct((1, D), F32)],
        compiler_params=_cp("arbitrary"),
    )(x, gain, target)


class MultiJob:
    def __init__(self, jobs):
        self.jobs = jobs
        self.args = [a for j in jobs for a in j.args]
        self.out_shape = [s for j in jobs for s in j.out_shape]
        self.scratch = [s for j in jobs for s in j.scratch]
        self.n_in, self.n_out = len(self.args), len(self.out_shape)

    def _each(self, ins, outs, sems):
        i = o = s = 0
        for j in self.jobs:
            yield j, ins[i:i + j.n_in], outs[o:o + j.n_out], sems[s:s + len(j.scratch)]
            i, o, s = i + j.n_in, o + j.n_out, s + len(j.scratch)

    def start(self, ins, outs, sems):
        for j, a, b, c in self._each(ins, outs, sems):
            j.start(a, b, c)

    def finish(self, ins, outs, sems):
        for j, a, b, c in self._each(ins, outs, sems):
            j.finish(a, b, c)

    def split(self, results):
        o = 0
        for j in self.jobs:
            yield results[o:o + j.n_out]
            o += j.n_out


class Plan:
    def __init__(self, shard, gathers, scatters, small_carrier=None, pack_small=None):
        self.shard, self.gathers, self.scatters = shard, gathers, scatters
        self.small_carrier, self.pack_small = small_carrier, pack_small
        self.weights, self.grads, self.parts = {}, {}, {}
        self.small_src = self.small_parts = None

    def weight(self, kind, l):
        return self.weights[(kind, l)]

    def grad(self, kind, l, g):
        self.grads[(kind, l)] = g

    def small_ready(self, small, d_final):
        if self.pack_small is not None:
            self.small_src = self.pack_small(small, d_final)

    def _jobs(self, key):
        jobs = []
        if key in self.gathers:
            ks = self.gathers[key]
            jobs.append((GatherJob([self.shard(*k) for k in ks]), self.weights, ks))
        if key in self.scatters:
            ks = self.scatters[key]
            jobs.append((ScatterJob([self.grads[k] for k in ks]), self.parts, ks))
        if key == self.small_carrier and self.small_src is not None:
            jobs.append((GatherJob([(self.small_src, None)]), None, None))
        return jobs

    def _deliver(self, jobs, results):
        multi = MultiJob([j for j, _, _ in jobs])
        for (_, store, ks), res in zip(jobs, multi.split(results)):
            if store is None:
                self.small_parts = res[0]
            else:
                store.update(zip(ks, res))

    def run(self, key, fn, *args, **kw):
        jobs = self._jobs(key)
        if not jobs:
            out = fn(*args, **kw)
            return out if fn is mm_tn else out[0]
        res, jres = fn(*args, job=MultiJob([j for j, _, _ in jobs]), **kw)
        self._deliver(jobs, jres)
        return res

    def alone(self, key, name):
        jobs = self._jobs(key)
        if jobs:
            self._deliver(jobs, run_job(MultiJob([j for j, _, _ in jobs]), name))


def _local_step(x, target, layers, final_norm, n_ex, plan):
    T, D = x.shape
    L = len(layers)
    cst = _attn_consts(T // n_ex)
    ident = lambda j: j
    EV_A, EV_B = 3 * (D // 2), QW + 2 * KW
    OD_W = D // 2
    wt = plan.weight

    saved = []
    for l, W in enumerate(layers):
        s = dict(x0=x)
        x1, s["gu1"] = plan.run(("ffn1_fwd", l), ffn_fwd, x, W["n1"], wt("f1_in_t", l), wt("f1_out", l))
        h = rms_fwd(x1, W["nm"])
        if l % 2 == 0:
            pa, pb = mm_nt(h, wt("mx_in_t", l), [(0, EV_A), (EV_A, EV_B)], F32)
            qg = jnp.tile(W["q_norm"], N_Q_HEADS)[None]
            kg = jnp.tile(W["k_norm"], N_KV_HEADS)[None]
            mix_a = conv_fwd(pa, W["conv_w"], n_ex)
            qp, kp, vp = qkv_prep_fwd(pb, qg, kg, cst, n_ex)
            mix_b, op, lse = plan.run(("attn_fwd", l), attn_fwd, qp, kp, vp, cst["gather"], n_ex)
            s.update(pa=pa, pb=pb, qg=qg, kg=kg, qp=qp, kp=kp, vp=vp, op=op, lse=lse)
        else:
            p, u, v = mm_nt(h, wt("mx_in_t", l), [(0, OD_W), (OD_W, OD_W), (2 * OD_W, OD_W)], F32)
            scale = W["pool_scale"][None]
            sn = W["sgu_norm"][None]
            b_full = jnp.broadcast_to(W["sgu_b"][..., None], W["sgu_w"].shape)
            mix_a = pool_fwd(p, W["pool_w"], scale, n_ex)
            mix_b = sgu_fwd(u, v, sn, W["sgu_w"], b_full)
            s.update(p=p, u=u, v=v, scale=scale, sn=sn, b_full=b_full)
        x2 = mm_nn([mix_a, mix_b], wt("mx_out", l), residual=x1)
        x3, s["gu2"] = plan.run(("ffn2_fwd", l), ffn_fwd, x2, W["n2"], wt("f2_in_t", l), wt("f2_out", l))
        s.update(x1=x1, x2=x2, h=h, mix_a=mix_a, mix_b=mix_b)
        saved.append(s)
        x = x3

    loss, dx, d_final = loss_head(x, final_norm, target)

    small = [None] * L

    def ffn_back(which, l, dout, xin, gain, gu, sm, sm_key):
        w_in, w_out = wt(which + "_in_t", l), wt(which + "_out", l)
        F = w_out.shape[0]
        nc = F // FFN_TN
        dxi, sm[sm_key], a, dgu, xn, dob = plan.run((which + "_bwd", l), ffn_bwd_x, dout, xin, gain, gu, w_in, w_out)
        plan.grad(which + "_out", l, plan.run((which + "_out_grad", l), mm_tn, a, dob, F, ident))
        if which == "f1" and l == 0:
            plan.small_ready(small, d_final)
        plan.grad(which + "_in_t", l, plan.run(
            (which + "_in_grad", l), mm_tn, dgu, xn, 2 * F, lambda k, c: k * nc + c, grid=(2, nc),
            col_block=lambda k, c: 2 * c + k))
        return dxi

    for l in reversed(range(L)):
        s, W = saved[l], layers[l]
        sm = small[l] = {}
        dx = ffn_back("f2", l, dx, s["x2"], W["n2"], s["gu2"], sm, "n2")
        dm_a, dm_b, dxb = mm_nt(dx, wt("mx_out", l), [(0, D // 2), (D // 2, D // 2)], BF16, emit_a_bf16=True)
        half_blocks = (D // 2) // MM_TC
        g_out = mm_tn(s["mix_a"], dxb, D, ident)
        plan.grad("mx_out", l, mm_tn(s["mix_b"], dxb, D, lambda jj: jj + half_blocks, prev=g_out))
        if l % 2 == 0:
            d_a, sm["conv_w"] = conv_bwd(s["pa"], W["conv_w"], n_ex, dm_a)
            dq, dkp, dvp = plan.run(("attn_bwd", l), attn_bwd, s["qp"], s["kp"], s["vp"], s["op"], s["lse"], dm_b, cst, n_ex)
            d_b, dqg, dkg = qkv_prep_bwd(s["pb"], s["qg"], s["kg"], cst, n_ex, dq, dkp, dvp)
            dh = mm_nn([d_a, d_b], wt("mx_in_t", l))
            g_in = mm_tn(d_a, s["h"], EV_A + EV_B, ident)
            plan.grad("mx_in_t", l, mm_tn(d_b, s["h"], EV_A + EV_B, lambda jj: jj + EV_A // MM_TC, prev=g_in))
            sm["q_norm"] = dqg.reshape(N_Q_HEADS, HEAD_DIM).sum(0)
            sm["k_norm"] = dkg.reshape(N_KV_HEADS, HEAD_DIM).sum(0)
        else:
            d_p, sm["pool_w"], d_ps = pool_bwd(s["p"], W["pool_w"], s["scale"], n_ex, dm_a)
            d_u, d_v, d_sn, sm["sgu_w"], d_sb = sgu_bwd(s["u"], s["v"], s["sn"], W["sgu_w"], s["b_full"], dm_b)
            dh = mm_nn([d_p, d_u, d_v], wt("mx_in_t", l))
            nb = OD_W // MM_TC
            g_in = mm_tn(d_p, s["h"], 3 * OD_W, ident)
            g_in = mm_tn(d_u, s["h"], 3 * OD_W, lambda jj: jj + nb, prev=g_in)
            plan.grad("mx_in_t", l, mm_tn(d_v, s["h"], 3 * OD_W, lambda jj: jj + 2 * nb, prev=g_in))
            sm["pool_scale"], sm["sgu_norm"], sm["sgu_b"] = d_ps[0], d_sn[0], d_sb.sum(-1)
        dx, sm["nm"] = rms_bwd(s["x1"], W["nm"], dh, dx)
        dx = ffn_back("f1", l, dx, s["x0"], W["n1"], s["gu1"], sm, "n1")
    return loss, dx


def all_gather(srcs):
    return run_job(GatherJob(srcs), "all_gather")


def all_reduce_small(v):
    R, C = v.shape

    def body(v_ref, o_ref, buf, send_sems, recv_sems):
        x, y, c, me = _my_place()
        buf[me] = v_ref[...]
        for k in range(1, N_DEV):
            peer, _ = _peer(x, y, c, k)
            pltpu.make_async_remote_copy(src_ref=v_ref, dst_ref=buf.at[me], send_sem=send_sems.at[k - 1],
                                         recv_sem=recv_sems.at[k - 1], device_id=peer, device_id_type=MESH_ID).start()
        for k in range(1, N_DEV):
            peer, pidx = _peer(x, y, c, k)
            cp = pltpu.make_async_remote_copy(src_ref=v_ref, dst_ref=buf.at[pidx], send_sem=send_sems.at[k - 1],
                                              recv_sem=recv_sems.at[k - 1], device_id=peer, device_id_type=MESH_ID)
            cp.wait_recv()
            cp.wait_send()
        acc = buf[0]
        for s in range(1, N_DEV):
            acc = acc + buf[s]
        o_ref[...] = acc

    vm = pl.BlockSpec(memory_space=pltpu.VMEM)
    return pl.pallas_call(
        body, name="all_reduce_small", in_specs=[vm], out_specs=vm, out_shape=jax.ShapeDtypeStruct((R, C), F32),
        scratch_shapes=[pltpu.VMEM((N_DEV, R, C), F32), pltpu.SemaphoreType.DMA((7,)), pltpu.SemaphoreType.DMA((7,))],
        compiler_params=pltpu.CompilerParams(vmem_limit_bytes=VMEM_LIMIT),
    )(v)


def cast_shards(w, transpose):
    L, A, B = w.shape
    oshape = (B, A) if transpose else (A, B)

    def body(w_ref, o_ref):
        wv = w_ref[...]
        o_ref[...] = (wv.T if transpose else wv).astype(BF16)

    return pl.pallas_call(
        body, name="cast_shards", grid=(L,),
        in_specs=[pl.BlockSpec((None, A, B), lambda l: (l, 0, 0))],
        out_specs=pl.BlockSpec((None,) + oshape, lambda l: (l, 0, 0)),
        out_shape=jax.ShapeDtypeStruct((L,) + oshape, BF16), compiler_params=_cp("parallel"),
    )(w)


ADAM_TC = 256


def adamw(parts, w, m, v, l, transpose):
    P, R, C = parts.shape
    tc = _tile(C, ADAM_TC)
    c1, c2 = 1.0 - ADAM_B1 ** ADAM_STEP, 1.0 - ADAM_B2 ** ADAM_STEP

    def body(p_ref, w_ref, m_ref, v_ref, g_ref, d_ref, mo_ref, vo_ref):
        g = p_ref[0].astype(F32)
        for s in range(1, P):
            g = g + p_ref[s].astype(F32)
        if transpose:
            g = g.T
        m1 = ADAM_B1 * m_ref[...] + (1.0 - ADAM_B1) * g
        v1 = ADAM_B2 * v_ref[...] + (1.0 - ADAM_B2) * (g * g)
        g_ref[...] = g
        mo_ref[...] = m1
        vo_ref[...] = v1
        d_ref[...] = -ADAM_LR * ((m1 / c1) / (jnp.sqrt(v1 / c2) + ADAM_EPS) + ADAM_WD * w_ref[...])

    if transpose:
        wspec = pl.BlockSpec((None, tc, R), lambda i: (l, i, 0))
        ospec, oshape = pl.BlockSpec((tc, R), lambda i: (i, 0)), (C, R)
    else:
        wspec = pl.BlockSpec((None, R, tc), lambda i: (l, 0, i))
        ospec, oshape = pl.BlockSpec((R, tc), lambda i: (0, i)), (R, C)
    return pl.pallas_call(
        body, name="adamw", grid=(C // tc,),
        in_specs=[pl.BlockSpec((P, R, tc), lambda i: (0, 0, i)), wspec, wspec, wspec],
        out_specs=[ospec] * 4, out_shape=[jax.ShapeDtypeStruct(oshape, F32)] * 4,
        compiler_params=_cp("parallel"),
    )(parts, w, m, v)


_WEIGHTS = ['ffn1_norm', 'ffn1_w_in', 'ffn1_w_out', 'mix_norm', 'ffn2_norm', 'ffn2_w_in', 'ffn2_w_out', 'ev_w_in',
            'ev_conv_w', 'ev_q_norm', 'ev_k_norm', 'ev_w_out', 'od_w_in', 'od_pool_w', 'od_pool_scale', 'od_sgu_norm',
            'od_sgu_w', 'od_sgu_b', 'od_w_out', 'final_norm']
_BIG = dict(ffn1_w_in=True, ffn1_w_out=False, ffn2_w_in=True, ffn2_w_out=False,
            ev_w_in=True, ev_w_out=False, od_w_in=True, od_w_out=False)
_SMALL_SHARDED = ['ev_conv_w', 'od_pool_scale', 'od_sgu_norm']
_SMALL = [n for n in _WEIGHTS if n not in _BIG]
_PACK_ROWS = 8 * LANES


_KINDS = ("f1_in_t", "f1_out", "mx_in_t", "mx_out", "f2_in_t", "f2_out")
_CARRIER_US = dict(ffn1_fwd=105, ffn2_fwd=105, attn_fwd=115, f2_bwd=165, f1_bwd=165, attn_bwd=205,
                   f2_out_grad=33, f1_out_grad=33, f2_in_grad=61, f1_in_grad=61)
_GATHER_US_PER_ROW, _SCATTER_US_PER_ROW, _SMALL_GATHER_US, _SLACK_US = 0.088, 0.176, 47, 10


def _schedule(L, rows):
    events = []
    for l in range(L):
        events += [("ffn1_fwd", l), ("mixer", l)] + ([("attn_fwd", l)] if l % 2 == 0 else []) + [("ffn2_fwd", l)]
    consumer = {"f1": "ffn1_fwd", "mx": "mixer", "f2": "ffn2_fwd"}
    queue = [(k, l) for l in range(L) for k in _KINDS]
    pos = {t: events.index((consumer[t[0][:2]], t[1])) for t in queue}
    gathers = {"first": [t for t in queue if pos[t] == 0]}
    queue = [t for t in queue if pos[t] > 0]
    carriers = [i for i, e in enumerate(events) if e[0] in _CARRIER_US]
    for i in carriers:
        budget, take = _CARRIER_US[events[i][0]], []
        later = [j for j in carriers if j > i]
        while queue:
            t = queue[0]
            cost = rows(*t) * _GATHER_US_PER_ROW
            forced = not any(j < pos[t] for j in later)
            if not forced and cost > budget:
                break
            take.append(queue.pop(0))
            budget -= cost
        if take:
            gathers[events[i]] = take
    assert not queue
    events = []
    for l in reversed(range(L)):
        events += [("f2_bwd", l), ("f2_out_grad", l), ("f2_in_grad", l), ("mx_out_ready", l)]
        events += [("attn_bwd", l)] if l % 2 == 0 else []
        events += [("mx_in_ready", l), ("f1_bwd", l), ("f1_out_grad", l), ("f1_in_grad", l)]
    made_by = {"f2_out": "f2_out_grad", "f2_in_t": "f2_in_grad", "mx_out": "mx_out_ready", "mx_in_t": "mx_in_ready",
               "f1_out": "f1_out_grad", "f1_in_t": "f1_in_grad"}
    small_carrier = ("f1_in_grad", 0)
    scatters, ready = {}, []
    for e in events:
        if e[0] in _CARRIER_US:
            budget, take = _CARRIER_US[e[0]] - (_SMALL_GATHER_US if e == small_carrier else 0), []
            while True:
                fits = [t for t in ready if rows(*t) * _SCATTER_US_PER_ROW <= budget + _SLACK_US]
                if not fits:
                    break
                t = max(fits, key=lambda u: rows(*u))
                budget -= rows(*t) * _SCATTER_US_PER_ROW
                ready.remove(t)
                take.append(t)
            if take:
                scatters[e] = take
        ready += [(k, e[1]) for k in _KINDS if made_by[k] == e[0]]
    scatters["last"] = ready
    return gathers, scatters, small_carrier


def _pack(arrs):
    flat = jnp.concatenate([a.reshape(-1) for a in arrs])
    pad = (-flat.shape[0]) % _PACK_ROWS
    return jnp.pad(flat, (0, pad)).reshape(-1, LANES)


def _unpack(buf, shapes):
    flat, out, off = buf.reshape(-1), [], 0
    for s in shapes:
        n = math.prod(s)
        out.append(flat[off:off + n].reshape(s))
        off += n
    return out


def _unshard_last(g, lead):
    nd = len(lead)
    return jnp.moveaxis(g, 0, nd).reshape(*lead, -1)


def kernel(x, ffn1_norm, ffn1_w_in, ffn1_w_out, mix_norm, ffn2_norm, ffn2_w_in, ffn2_w_out, ev_w_in, ev_conv_w, ev_q_norm, ev_k_norm, ev_w_out, od_w_in, od_pool_w, od_pool_scale, od_sgu_norm, od_sgu_w, od_sgu_b, od_w_out, final_norm, loss_target, m_ffn1_norm, m_ffn1_w_in, m_ffn1_w_out, m_mix_norm, m_ffn2_norm, m_ffn2_w_in, m_ffn2_w_out, m_ev_w_in, m_ev_conv_w, m_ev_q_norm, m_ev_k_norm, m_ev_w_out, m_od_w_in, m_od_pool_w, m_od_pool_scale, m_od_sgu_norm, m_od_sgu_w, m_od_sgu_b, m_od_w_out, m_final_norm, v_ffn1_norm, v_ffn1_w_in, v_ffn1_w_out, v_mix_norm, v_ffn2_norm, v_ffn2_w_in, v_ffn2_w_out, v_ev_w_in, v_ev_conv_w, v_ev_q_norm, v_ev_k_norm, v_ev_w_out, v_od_w_in, v_od_pool_w, v_od_pool_scale, v_od_sgu_norm, v_od_sgu_w, v_od_sgu_b, v_od_w_out, v_final_norm):
    w = dict(zip(_WEIGHTS, (ffn1_norm, ffn1_w_in, ffn1_w_out, mix_norm, ffn2_norm, ffn2_w_in, ffn2_w_out, ev_w_in, ev_conv_w, ev_q_norm, ev_k_norm, ev_w_out, od_w_in, od_pool_w, od_pool_scale, od_sgu_norm, od_sgu_w, od_sgu_b, od_w_out, final_norm)))
    m = dict(zip(_WEIGHTS, (m_ffn1_norm, m_ffn1_w_in, m_ffn1_w_out, m_mix_norm, m_ffn2_norm, m_ffn2_w_in, m_ffn2_w_out, m_ev_w_in, m_ev_conv_w, m_ev_q_norm, m_ev_k_norm, m_ev_w_out, m_od_w_in, m_od_pool_w, m_od_pool_scale, m_od_sgu_norm, m_od_sgu_w, m_od_sgu_b, m_od_w_out, m_final_norm)))
    v = dict(zip(_WEIGHTS, (v_ffn1_norm, v_ffn1_w_in, v_ffn1_w_out, v_mix_norm, v_ffn2_norm, v_ffn2_w_in, v_ffn2_w_out, v_ev_w_in, v_ev_conv_w, v_ev_q_norm, v_ev_k_norm, v_ev_w_out, v_od_w_in, v_od_pool_w, v_od_pool_scale, v_od_sgu_norm, v_od_sgu_w, v_od_sgu_b, v_od_w_out, v_final_norm)))
    n_ex, seq, D = x.shape
    T = n_ex * seq
    L = ffn1_norm.shape[0]
    me = 4 * lax.axis_index("x") + 2 * lax.axis_index("y") + lax.axis_index("c")

    sh_small = [w[n] for n in _SMALL_SHARDED]
    packed = all_gather([(_pack(sh_small), None)])[0].reshape(N_DEV, -1)
    full_small = {}
    off = 0
    for n, a in zip(_SMALL_SHARDED, sh_small):
        cnt = math.prod(a.shape)
        full_small[n] = _unshard_last(packed[:, off:off + cnt].reshape((N_DEV,) + a.shape), a.shape[:-1])
        off += cnt

    shards = {n: cast_shards(w[n], t) for n, t in _BIG.items()}

    def name_of(kind, l):
        mx = "ev" if l % 2 == 0 else "od"
        return {"f1_in_t": "ffn1_w_in", "f1_out": "ffn1_w_out", "f2_in_t": "ffn2_w_in", "f2_out": "ffn2_w_out",
                "mx_in_t": mx + "_w_in", "mx_out": mx + "_w_out"}[kind], (l // 2 if kind.startswith("mx") else l)

    def shard(kind, l):
        name, idx = name_of(kind, l)
        return shards[name], idx

    g_shapes = {}

    def pack_small(small, d_final):
        ev = [sm for l, sm in enumerate(small) if l % 2 == 0]
        od = [sm for l, sm in enumerate(small) if l % 2 == 1]
        st = lambda sms, k: jnp.stack([sm[k] for sm in sms])
        g_full = dict(ffn1_norm=st(small, "n1")[:, 0], mix_norm=st(small, "nm")[:, 0], ffn2_norm=st(small, "n2")[:, 0],
                      ev_conv_w=st(ev, "conv_w"), ev_q_norm=st(ev, "q_norm"), ev_k_norm=st(ev, "k_norm"),
                      od_pool_w=st(od, "pool_w"), od_pool_scale=st(od, "pool_scale"), od_sgu_norm=st(od, "sgu_norm"),
                      od_sgu_w=st(od, "sgu_w"), od_sgu_b=st(od, "sgu_b"), final_norm=d_final[0])
        g_shapes.update({n: g_full[n].shape for n in _SMALL})
        return _pack([g_full[n] for n in _SMALL])

    gathers, scatters, small_carrier = _schedule(L, lambda kind, l: shards[name_of(kind, l)[0]].shape[1])
    plan = Plan(shard, gathers, scatters, small_carrier, pack_small)
    layers = []
    for l in range(L):
        j = l // 2
        W = dict(n1=ffn1_norm[l][None], nm=mix_norm[l][None], n2=ffn2_norm[l][None])
        if l % 2 == 0:
            W.update(conv_w=full_small["ev_conv_w"][j], q_norm=ev_q_norm[j], k_norm=ev_k_norm[j])
        else:
            W.update(pool_w=od_pool_w[j], pool_scale=full_small["od_pool_scale"][j], sgu_norm=full_small["od_sgu_norm"][j],
                     sgu_w=od_sgu_w[j], sgu_b=od_sgu_b[j])
        layers.append(W)

    plan.alone("first", "gather_first")
    loss, dx = _local_step(x.reshape(T, D), loss_target.reshape(T, D), layers, final_norm[None], n_ex, plan)
    plan.alone("last", "scatter_last")

    res = {n: [None] * w[n].shape[0] for n in _BIG}
    for (kind, l), parts in plan.parts.items():
        name, idx = name_of(kind, l)
        res[name][idx] = adamw(parts, w[name], m[name], v[name], idx, _BIG[name])
    out = {n: [jnp.stack([r[i] for r in rs]) for i in range(4)] for n, rs in res.items()}

    g8 = plan.small_parts.reshape(N_DEV, -1)
    cols, off = [], 0
    for n in _SMALL:
        cnt = math.prod(g_shapes[n])
        g = g8[:, off:off + cnt].reshape((N_DEV,) + g_shapes[n])
        off += cnt
        if n in _SMALL_SHARDED:
            width = w[n].shape[-1]
            g = lax.dynamic_slice_in_dim(g, me * width, width, axis=g.ndim - 1)
        cols.append(g.reshape(N_DEV, -1))
    g8 = jnp.concatenate(cols, axis=1)
    g8 = jnp.pad(g8, ((0, 0), (0, (-g8.shape[1]) % _PACK_ROWS))).reshape(N_DEV, -1, LANES)
    pk = lambda d: _pack([d[n] for n in _SMALL])[None]
    small_out = adamw(g8, pk(w), pk(m), pk(v), 0, False)
    shapes = [w[n].shape for n in _SMALL]
    for i in range(4):
        for n, a in zip(_SMALL, _unpack(small_out[i], shapes)):
            out.setdefault(n, [None] * 4)[i] = a

    total = lax.psum(loss[0, 0], ("x", "y", "c"))
    return (total, dx.reshape(n_ex, seq, D), *[out[n][0] for n in _WEIGHTS], *[out[n][1] for n in _WEIGHTS],
            *[out[n][2] for n in _WEIGHTS], *[out[n][3] for n in _WEIGHTS])
```

```python
import functools
import math

import jax
import jax.numpy as jnp
from jax import lax
from jax.experimental import pallas as pl
from jax.experimental.pallas import tpu as pltpu

F32, BF16 = jnp.float32, jnp.bfloat16
EPS = 1e-6
N_DEV = 8
V7X_VMEM_BYTES = 64 * 1024 * 1024
VMEM_LIMIT = V7X_VMEM_BYTES - 8 * 1024 * 1024
LANES = 128
HEAD_DIM = 64
N_Q_HEADS = 8
N_KV_HEADS = 2
Q_PER_KV = N_Q_HEADS // N_KV_HEADS
GRID_W = 64
ROPE_THETA = 10000.0
POOL_RADII = (1, 2, 4, 8)
SGU_CHUNK = 128
GROUP = 128
ADAM_LR, ADAM_B1, ADAM_B2, ADAM_EPS, ADAM_WD, ADAM_STEP = 0.001, 0.9, 0.999, 1e-08, 0.01, 10
MESH_ID = pl.DeviceIdType.MESH


def _cp(*sem):
    return pltpu.CompilerParams(dimension_semantics=sem, vmem_limit_bytes=VMEM_LIMIT)


def _dot(a, b, ca, cb):
    return lax.dot_general(a, b, (((ca,), (cb,)), ((), ())), preferred_element_type=F32)


def _nn(a, b):
    return _dot(a, b, 1, 0)


def _nt(a, b):
    return _dot(a, b, 1, 1)


def _tn(a, b):
    return _dot(a, b, 0, 0)


def _split_mm(x, m):
    hi = x.astype(BF16)
    lo = (x - hi.astype(F32)).astype(BF16)
    return _nn(hi, m) + _nn(lo, m)


def _tile(n, pref):
    t = min(n, pref)
    assert n % t == 0, (n, pref)
    return t


ANY = pl.BlockSpec(memory_space=pl.ANY)
OTHER_CHIPS = (2, 4, 6)


def _my_place():
    x, y, c = lax.axis_index("x"), lax.axis_index("y"), lax.axis_index("c")
    return x, y, c, 4 * x + 2 * y + c


def _peer(x, y, c, k):
    px = 1 - x if k & 4 else x
    py = 1 - y if k & 2 else y
    pc = 1 - c if k & 1 else c
    return (px, py, pc), 4 * px + 2 * py + pc


def _remote(src, dst, send_sems, recv_sems, i, peer):
    return pltpu.make_async_remote_copy(src_ref=src, dst_ref=dst, send_sem=send_sems.at[i], recv_sem=recv_sems.at[i],
                                        device_id=peer, device_id_type=MESH_ID)


class GatherJob:
    def __init__(self, srcs):
        self.srcs = srcs
        self.args = [a for a, _ in srcs]
        self.dims = [a.shape[-2:] for a, _ in srcs]
        n = self.n_in = self.n_out = len(srcs)
        self.out_shape = [jax.ShapeDtypeStruct((N_DEV * r, cc), a.dtype) for (a, _), (r, cc) in zip(srcs, self.dims)]
        self.scratch = [pltpu.SemaphoreType.DMA((N_DEV * n,)), pltpu.SemaphoreType.DMA((N_DEV * n,)),
                        pltpu.SemaphoreType.DMA((n,))]

    def _rows(self, outs, t, idx):
        r = self.dims[t][0]
        return outs[t].at[pl.ds(pl.multiple_of(idx * r, 8), r), :]

    def _local(self, ins, outs, loc, t, me):
        src = ins[t] if self.srcs[t][1] is None else ins[t].at[self.srcs[t][1]]
        return src, pltpu.make_async_copy(src, self._rows(outs, t, me), loc.at[t])

    def start(self, ins, outs, sems):
        send, recv, loc = sems
        x, y, c, me = _my_place()
        for t in range(self.n_in):
            src, local = self._local(ins, outs, loc, t, me)
            local.start()
            for k in OTHER_CHIPS + (1,):
                _remote(src, self._rows(outs, t, me), send, recv, N_DEV * t + k, _peer(x, y, c, k)[0]).start()

    def finish(self, ins, outs, sems):
        send, recv, loc = sems
        x, y, c, me = _my_place()
        sibling = _peer(x, y, c, 1)[0]
        for t in range(self.n_in):
            for k in OTHER_CHIPS:
                peer, pidx = _peer(x, y, c, k)
                blk = self._rows(outs, t, pidx)
                _remote(blk, blk, send, recv, N_DEV * t + k, peer).wait_recv()
                _remote(blk, blk, send, recv, N_DEV * t + k + 1, sibling).start()
        for t in range(self.n_in):
            for k in range(1, N_DEV):
                peer, pidx = _peer(x, y, c, k)
                blk = self._rows(outs, t, pidx)
                if k % 2 == 1:
                    _remote(blk, blk, send, recv, N_DEV * t + k, peer).wait_recv()
                _remote(blk, blk, send, recv, N_DEV * t + k, peer).wait_send()
            self._local(ins, outs, loc, t, me)[1].wait()


class ScatterJob:
    def __init__(self, grads):
        self.args = list(grads)
        self.dims = [(g.shape[0] // N_DEV, g.shape[1]) for g in grads]
        n = self.n_in = self.n_out = len(grads)
        self.out_shape = [jax.ShapeDtypeStruct((N_DEV, r, cc), g.dtype) for g, (r, cc) in zip(grads, self.dims)]
        self.scratch = [pltpu.SemaphoreType.DMA((N_DEV * n,)), pltpu.SemaphoreType.DMA((N_DEV * n,)),
                        pltpu.SemaphoreType.DMA((n,))]

    def _rows(self, ins, t, idx):
        r = self.dims[t][0]
        return ins[t].at[pl.ds(pl.multiple_of(idx * r, 8), r), :]

    def start(self, ins, outs, sems):
        send, recv, loc = sems
        x, y, c, me = _my_place()
        for t in range(self.n_in):
            pltpu.make_async_copy(self._rows(ins, t, me), outs[t].at[me], loc.at[t]).start()
            for k in OTHER_CHIPS + (1, 3, 5, 7):
                peer, pidx = _peer(x, y, c, k)
                _remote(self._rows(ins, t, pidx), outs[t].at[me], send, recv, N_DEV * t + k, peer).start()

    def finish(self, ins, outs, sems):
        send, recv, loc = sems
        x, y, c, me = _my_place()
        for t in range(self.n_in):
            for k in range(1, N_DEV):
                peer, pidx = _peer(x, y, c, k)
                cp = _remote(self._rows(ins, t, pidx), outs[t].at[pidx], send, recv, N_DEV * t + k, peer)
                cp.wait_recv()
                cp.wait_send()
            pltpu.make_async_copy(self._rows(ins, t, me), outs[t].at[me], loc.at[t]).wait()


def _call(body, name, grid, in_specs, args, out_specs, out_shape, scratch=(), sem=(), job=None, aliases=None):
    in_specs, out_specs, out_shape, scratch = list(in_specs), list(out_specs), list(out_shape), list(scratch)
    n_in, n_out, n_scr = len(args), len(out_shape), len(scratch)
    if job is None:
        res = pl.pallas_call(body, name=name, grid=grid, in_specs=in_specs, out_specs=out_specs, out_shape=out_shape,
                             scratch_shapes=scratch, input_output_aliases=aliases or {}, compiler_params=_cp(*sem))(*args)
        return res, None
    o0 = n_in + job.n_in
    s0 = o0 + n_out + job.n_out

    def carrier(*refs):
        jin, jout, jsem = refs[n_in:o0], refs[o0 + n_out:s0], refs[s0 + n_scr:]
        ids = [pl.program_id(a) for a in range(len(grid))]
        if grid:
            first = functools.reduce(jnp.logical_and, [i == 0 for i in ids])
            last = functools.reduce(jnp.logical_and, [i == g - 1 for i, g in zip(ids, grid)])
            pl.when(first)(lambda: job.start(jin, jout, jsem))
        else:
            job.start(jin, jout, jsem)
        body(*refs[:n_in], *refs[o0:o0 + n_out], *refs[s0:s0 + n_scr])
        if grid:
            pl.when(last)(lambda: job.finish(jin, jout, jsem))
        else:
            job.finish(jin, jout, jsem)

    res = pl.pallas_call(
        carrier, name=name + "_comm", grid=grid, in_specs=in_specs + [ANY] * job.n_in,
        out_specs=out_specs + [ANY] * job.n_out, out_shape=out_shape + job.out_shape,
        scratch_shapes=scratch + job.scratch, input_output_aliases=aliases or {},
        compiler_params=_cp(*(["arbitrary"] * len(grid))))(*args, *job.args)
    return res[:n_out], res[n_out:]


def run_job(job, name):
    return _call(lambda: None, name, (), [], [], [], [], job=job)[1]


@jax.custom_vjp
def bmm(x, w):
    return _nn(x.astype(BF16), w.astype(BF16))


def _bmm_fwd(x, w):
    return bmm(x, w), (x, w)


def _bmm_bwd(res, g):
    x, w = res
    gb = g.astype(BF16)
    return _nt(gb, w.astype(BF16)), _tn(x.astype(BF16), gb)


bmm.defvjp(_bmm_fwd, _bmm_bwd)


def _shift_raw(x, d):
    n = x.shape[0]
    r = pltpu.roll(x, d % n, axis=0)
    row = lax.broadcasted_iota(jnp.int32, x.shape, 0)
    keep = (row >= d) if d > 0 else (row < n + d)
    return jnp.where(keep, r, 0.0)


def shift_rows(x, d):
    @jax.custom_vjp
    def f(v):
        return _shift_raw(v, d)

    f.defvjp(lambda v: (_shift_raw(v, d), None), lambda _, g: (_shift_raw(g, -d),))
    return f(x)


def _swap_raw(x):
    n = x.shape[1]
    nxt = pltpu.roll(x, n - 1, axis=1)
    prv = pltpu.roll(x, 1, axis=1)
    lane = lax.broadcasted_iota(jnp.int32, x.shape, 1)
    return jnp.where(lane % 2 == 0, nxt, prv)


@jax.custom_vjp
def swap_pairs(x):
    return _swap_raw(x)


swap_pairs.defvjp(lambda x: (_swap_raw(x), None), lambda _, g: (_swap_raw(g),))


@jax.custom_vjp
def group_mean(x, bd):
    return _split_mm(x, bd)


group_mean.defvjp(lambda x, bd: (_split_mm(x, bd), bd), lambda bd, g: (_split_mm(g, bd), jnp.zeros_like(bd)))


def _rope_norm(x, gain, cos, sgn, bd, scale):
    xn = x * lax.rsqrt(group_mean(x * x, bd) + EPS) * gain
    return (xn * cos + swap_pairs(xn) * sgn) * scale


def _conv_gate(gb, gc, hc, w):
    z = gc * hc
    c = shift_rows(z, 1) * w[0:1] + z * w[1:2] + shift_rows(z, -1) * w[2:3]
    return gb * c


def _window_sum(p, r):
    b = f = p
    k = 1
    while k < r:
        b = b + shift_rows(b, k)
        f = f + shift_rows(f, -k)
        k *= 2
    return b + f - p + shift_rows(p, r) + shift_rows(p, -r)


def _pool_mix(p, pool_w, scale):
    n = p.shape[0]
    t = lax.broadcasted_iota(jnp.int32, (n, 1), 0)
    outs = []
    for gi, r in enumerate(POOL_RADII):
        pg = p[:, gi * GROUP:(gi + 1) * GROUP]
        cnt = (jnp.minimum(t + r, n - 1) - jnp.maximum(t - r, 0) + 1).astype(F32)
        pooled = _window_sum(pg, r) / cnt - pg
        outs.append(bmm(pooled, pool_w[gi]))
    return jnp.concatenate(outs, axis=1) * scale


def _sgu(u, v, norm_g, w_s, b_full):
    ug = jax.nn.gelu(u)
    vg = jax.nn.gelu(v)
    vn = vg * lax.rsqrt(jnp.mean(vg * vg, axis=-1, keepdims=True) + EPS) * norm_g
    cols = []
    for g in range(w_s.shape[0]):
        rows = []
        for n in range(u.shape[0] // SGU_CHUNK):
            blk = vn[n * SGU_CHUNK:(n + 1) * SGU_CHUNK, g * GROUP:(g + 1) * GROUP]
            rows.append(bmm(w_s[g], blk) + b_full[g])
        cols.append(jnp.concatenate(rows, axis=0))
    return ug * jnp.concatenate(cols, axis=1)


def rms_fwd(x, gain):
    T, D = x.shape
    tm = _tile(T, 512)

    def body(x_ref, g_ref, o_ref):
        xv = x_ref[...]
        r = lax.rsqrt(jnp.mean(xv * xv, axis=-1, keepdims=True) + EPS)
        o_ref[...] = (xv * r * g_ref[...]).astype(BF16)

    return pl.pallas_call(
        body, name="rms_fwd", grid=(T // tm,),
        in_specs=[pl.BlockSpec((tm, D), lambda i: (i, 0)), pl.BlockSpec((1, D), lambda i: (0, 0))],
        out_specs=pl.BlockSpec((tm, D), lambda i: (i, 0)),
        out_shape=jax.ShapeDtypeStruct((T, D), BF16), compiler_params=_cp("parallel"),
    )(x, gain)


def _rms_bwd_math(xv, gain, dy, dres):
    r = lax.rsqrt(jnp.mean(xv * xv, axis=-1, keepdims=True) + EPS)
    xh = xv * r
    dxh = dy * gain
    dx = dres + r * (dxh - xh * jnp.mean(dxh * xh, axis=-1, keepdims=True))
    return dx, jnp.sum(dy * xh, axis=0, keepdims=True)


def rms_bwd(x, gain, dy, dres):
    T, D = x.shape
    tm = _tile(T, 512)

    def body(x_ref, g_ref, dy_ref, dr_ref, dx_ref, dg_ref):
        dx, dg = _rms_bwd_math(x_ref[...], g_ref[...], dy_ref[...], dr_ref[...])
        dx_ref[...] = dx

        @pl.when(pl.program_id(0) == 0)
        def _():
            dg_ref[...] = jnp.zeros_like(dg_ref)

        dg_ref[...] += dg

    row = pl.BlockSpec((tm, D), lambda i: (i, 0))
    return pl.pallas_call(
        body, name="rms_bwd", grid=(T // tm,),
        in_specs=[row, pl.BlockSpec((1, D), lambda i: (0, 0)), row, row],
        out_specs=[row, pl.BlockSpec((1, D), lambda i: (0, 0))],
        out_shape=[jax.ShapeDtypeStruct((T, D), F32), jax.ShapeDtypeStruct((1, D), F32)],
        compiler_params=_cp("arbitrary"),
    )(x, gain, dy, dres)


FFN_TN = 256


def ffn_fwd(x, gain, wt_in, w_out, job=None):
    T, D = x.shape
    F = w_out.shape[0]
    tm, tn = _tile(T, 1024), FFN_TN
    nc = F // tn

    def body(x_ref, gn_ref, wg_ref, wu_ref, wo_ref, y_ref, gu_ref, xn_s, acc_s):
        c = pl.program_id(1)

        @pl.when(c == 0)
        def _():
            xv = x_ref[...]
            r = lax.rsqrt(jnp.mean(xv * xv, axis=-1, keepdims=True) + EPS)
            xn_s[...] = (xv * r * gn_ref[...]).astype(BF16)
            acc_s[...] = jnp.zeros_like(acc_s)

        xn = xn_s[...]
        g = _nt(xn, wg_ref[...])
        u = _nt(xn, wu_ref[...])
        gu_ref[:, :tn] = g.astype(BF16)
        gu_ref[:, tn:] = u.astype(BF16)
        a = (g * jax.nn.sigmoid(g) * u).astype(BF16)
        acc_s[...] += _nn(a, wo_ref[...])

        @pl.when(c == nc - 1)
        def _():
            y_ref[...] = x_ref[...] + 0.5 * acc_s[...]

    row = pl.BlockSpec((tm, D), lambda i, c: (i, 0))
    return _call(
        body, "ffn_fwd", (T // tm, nc),
        [row, pl.BlockSpec((1, D), lambda i, c: (0, 0)),
         pl.BlockSpec((tn, D), lambda i, c: (c, 0)),
         pl.BlockSpec((tn, D), lambda i, c: (c + nc, 0)),
         pl.BlockSpec((tn, D), lambda i, c: (c, 0))],
        [x, gain, wt_in, wt_in, w_out],
        [row, pl.BlockSpec((tm, 2 * tn), lambda i, c: (i, c))],
        [jax.ShapeDtypeStruct((T, D), F32), jax.ShapeDtypeStruct((T, 2 * F), BF16)],
        [pltpu.VMEM((tm, D), BF16), pltpu.VMEM((tm, D), F32)], ("parallel", "arbitrary"), job)


def ffn_bwd_x(dout, x, gain, gu, wt_in, w_out, job=None):
    T, D = x.shape
    F = w_out.shape[0]
    tm, tn = _tile(T, 1024), FFN_TN
    nc = F // tn

    def body(do_ref, x_ref, gn_ref, gu_ref, wg_ref, wu_ref, wo_ref,
             dx_ref, dgn_ref, a_ref, dgu_ref, xn_ref, dob_ref, acc_s):
        i, c = pl.program_id(0), pl.program_id(1)

        @pl.when(c == 0)
        def _():
            xv = x_ref[...]
            r = lax.rsqrt(jnp.mean(xv * xv, axis=-1, keepdims=True) + EPS)
            xn_ref[...] = (xv * r * gn_ref[...]).astype(BF16)
            dob_ref[...] = (0.5 * do_ref[...]).astype(BF16)
            acc_s[...] = jnp.zeros_like(acc_s)

        da = jnp.concatenate([_nt(dob_ref[:tm // 2, :], wo_ref[...]), _nt(dob_ref[tm // 2:, :], wo_ref[...])], axis=0)
        g = gu_ref[:, :tn].astype(F32)
        u = gu_ref[:, tn:].astype(F32)
        sig = jax.nn.sigmoid(g)
        sl = g * sig
        a_ref[...] = (sl * u).astype(BF16)
        dg = (da * u * (sig * (1.0 + g * (1.0 - sig)))).astype(BF16)
        du = (da * sl).astype(BF16)
        dgu_ref[:, :tn] = dg
        dgu_ref[:, tn:] = du
        acc_s[...] += _nn(dg, wg_ref[...]) + _nn(du, wu_ref[...])

        @pl.when(c == nc - 1)
        def _():
            dx, dgn = _rms_bwd_math(x_ref[...], gn_ref[...], acc_s[...], do_ref[...])
            dx_ref[...] = dx

            @pl.when(i == 0)
            def _():
                dgn_ref[...] = jnp.zeros_like(dgn_ref)

            dgn_ref[...] += dgn

    row = pl.BlockSpec((tm, D), lambda i, c: (i, 0))
    return _call(
        body, "ffn_bwd_x", (T // tm, nc),
        [row, row, pl.BlockSpec((1, D), lambda i, c: (0, 0)),
         pl.BlockSpec((tm, 2 * tn), lambda i, c: (i, c)),
         pl.BlockSpec((tn, D), lambda i, c: (c, 0)),
         pl.BlockSpec((tn, D), lambda i, c: (c + nc, 0)),
         pl.BlockSpec((tn, D), lambda i, c: (c, 0))],
        [dout, x, gain, gu, wt_in, wt_in, w_out],
        [row, pl.BlockSpec((1, D), lambda i, c: (0, 0)),
         pl.BlockSpec((tm, tn), lambda i, c: (i, c)),
         pl.BlockSpec((tm, 2 * tn), lambda i, c: (i, c)), row, row],
        [jax.ShapeDtypeStruct((T, D), F32), jax.ShapeDtypeStruct((1, D), F32),
         jax.ShapeDtypeStruct((T, F), BF16), jax.ShapeDtypeStruct((T, 2 * F), BF16),
         jax.ShapeDtypeStruct((T, D), BF16), jax.ShapeDtypeStruct((T, D), BF16)],
        [pltpu.VMEM((tm, D), F32)], ("arbitrary", "arbitrary"), job)


MM_TM = 512
MM_TC = 256


def mm_nt(a, wt, pieces, out_dtype, emit_a_bf16=False):
    T, K = a.shape
    tm = _tile(T, MM_TM)
    npc = len(pieces)

    def body(*refs):
        a_ref, w_refs, o_refs = refs[0], refs[1:1 + npc], refs[1 + npc:]
        ab = a_ref[...].astype(BF16)
        for w_ref, o_ref in zip(w_refs, o_refs[:npc]):
            o_ref[...] = _nt(ab, w_ref[...]).astype(o_ref.dtype)
        if emit_a_bf16:
            o_refs[npc][...] = ab

    in_specs = [pl.BlockSpec((tm, K), lambda i: (i, 0))]
    out_specs, out_shape = [], []
    for r0, n in pieces:
        assert r0 % n == 0
        in_specs.append(pl.BlockSpec((n, K), functools.partial(lambda i, b: (b, 0), b=r0 // n)))
        out_specs.append(pl.BlockSpec((tm, n), lambda i: (i, 0)))
        out_shape.append(jax.ShapeDtypeStruct((T, n), out_dtype))
    if emit_a_bf16:
        out_specs.append(pl.BlockSpec((tm, K), lambda i: (i, 0)))
        out_shape.append(jax.ShapeDtypeStruct((T, K), BF16))
    return pl.pallas_call(
        body, name="mm_nt", grid=(T // tm,), in_specs=in_specs, out_specs=out_specs, out_shape=out_shape,
        compiler_params=_cp("parallel"),
    )(a, *([wt] * npc))


def mm_nn(a_list, w, residual=None):
    T = a_list[0].shape[0]
    N = w.shape[1]
    tm = _tile(T, MM_TM)
    na = len(a_list)

    def body(*refs):
        a_refs, w_refs = refs[:na], refs[na:2 * na]
        o_ref = refs[-1]
        acc = refs[2 * na][...] if residual is not None else None
        for a_ref, w_ref in zip(a_refs, w_refs):
            t = _nn(a_ref[...].astype(BF16), w_ref[...])
            acc = t if acc is None else acc + t
        o_ref[...] = acc

    in_specs, r0 = [], 0
    w_specs = []
    for a in a_list:
        k = a.shape[1]
        assert r0 % k == 0
        in_specs.append(pl.BlockSpec((tm, k), lambda i: (i, 0)))
        w_specs.append(pl.BlockSpec((k, N), functools.partial(lambda i, b: (b, 0), b=r0 // k)))
        r0 += k
    assert r0 == w.shape[0]
    args = list(a_list) + [w] * na
    in_specs = in_specs + w_specs
    if residual is not None:
        in_specs.append(pl.BlockSpec((tm, N), lambda i: (i, 0)))
        args.append(residual)
    return pl.pallas_call(
        body, name="mm_nn", grid=(T // tm,), in_specs=in_specs,
        out_specs=pl.BlockSpec((tm, N), lambda i: (i, 0)),
        out_shape=jax.ShapeDtypeStruct((T, N), F32), compiler_params=_cp("parallel"),
    )(*args)


def mm_tn(a, b, n_rows, row_block, prev=None, grid=None, col_block=None, job=None):
    T, M = a.shape
    N = b.shape[1]
    tc = MM_TC
    assert M % tc == 0 and n_rows % tc == 0
    if grid is None:
        grid, col_block = (M // tc,), (lambda j: j)

    def body(*refs):
        a_ref, b_ref, o_ref = refs[0], refs[1], refs[-1]
        o_ref[...] = _tn(a_ref[...], b_ref[...]).astype(BF16)

    in_specs = [pl.BlockSpec((T, tc), lambda *g: (0, col_block(*g))), pl.BlockSpec((T, N), lambda *g: (0, 0))]
    args = [a, b]
    aliases = {}
    if prev is not None:
        in_specs.append(pl.BlockSpec(memory_space=pl.ANY))
        args.append(prev)
        aliases = {2: 0}
    res, jres = _call(body, "mm_tn", grid, in_specs, args, [pl.BlockSpec((tc, N), lambda *g: (row_block(*g), 0))],
                      [jax.ShapeDtypeStruct((n_rows, N), BF16)], (), ["parallel"] * len(grid), job, aliases)
    return res[0] if job is None else (res[0], jres)


def conv_fwd(proj_a, conv_w, n_ex):
    T, C3 = proj_a.shape
    C = C3 // 3
    S = T // n_ex

    def body(gb_ref, gc_ref, hc_ref, w_ref, o_ref):
        o_ref[...] = _conv_gate(gb_ref[...], gc_ref[...], hc_ref[...], w_ref[...]).astype(BF16)

    col = lambda k: pl.BlockSpec((S, C), functools.partial(lambda b, kk: (b, kk), kk=k))
    return pl.pallas_call(
        body, name="conv_fwd", grid=(n_ex,),
        in_specs=[col(0), col(1), col(2), pl.BlockSpec((3, C), lambda b: (0, 0))],
        out_specs=pl.BlockSpec((S, C), lambda b: (b, 0)),
        out_shape=jax.ShapeDtypeStruct((T, C), BF16), compiler_params=_cp("parallel"),
    )(proj_a, proj_a, proj_a, conv_w)


def conv_bwd(proj_a, conv_w, n_ex, dy):
    T, C3 = proj_a.shape
    C = C3 // 3
    S = T // n_ex

    def body(gb_ref, gc_ref, hc_ref, w_ref, dy_ref, dp_ref, dw_ref):
        _, vjp = jax.vjp(_conv_gate, gb_ref[...], gc_ref[...], hc_ref[...], w_ref[...])
        dgb, dgc, dhc, dw = vjp(dy_ref[...].astype(F32))
        dp_ref[:, 0:C] = dgb.astype(BF16)
        dp_ref[:, C:2 * C] = dgc.astype(BF16)
        dp_ref[:, 2 * C:] = dhc.astype(BF16)

        @pl.when(pl.program_id(0) == 0)
        def _():
            dw_ref[...] = jnp.zeros_like(dw_ref)

        dw_ref[...] += dw

    col = lambda k: pl.BlockSpec((S, C), functools.partial(lambda b, kk: (b, kk), kk=k))
    return pl.pallas_call(
        body, name="conv_bwd", grid=(n_ex,),
        in_specs=[col(0), col(1), col(2), pl.BlockSpec((3, C), lambda b: (0, 0)),
                  pl.BlockSpec((S, C), lambda b: (b, 0))],
        out_specs=[pl.BlockSpec((S, C3), lambda b: (b, 0)), pl.BlockSpec((3, C), lambda b: (0, 0))],
        out_shape=[jax.ShapeDtypeStruct((T, C3), BF16), jax.ShapeDtypeStruct((3, C), F32)],
        compiler_params=_cp("arbitrary"),
    )(proj_a, proj_a, proj_a, conv_w, dy)


QW = N_Q_HEADS * HEAD_DIM
KW = N_KV_HEADS * HEAD_DIM
QP = N_Q_HEADS * LANES
KP = N_KV_HEADS * LANES


def _attn_consts(seq):
    rows = seq // GRID_W
    r_idx, c_idx = jnp.meshgrid(jnp.arange(rows), jnp.arange(GRID_W), indexing='ij')
    r_idx = r_idx.reshape(-1).astype(F32)
    c_idx = c_idx.reshape(-1).astype(F32)
    n_freq = HEAD_DIM // 4
    inv = ROPE_THETA ** (-jnp.arange(n_freq, dtype=F32) / n_freq)
    ang = jnp.concatenate([r_idx[:, None] * inv, c_idx[:, None] * inv], axis=-1)
    cos = jnp.repeat(jnp.cos(ang), 2, axis=1)
    sin = jnp.repeat(jnp.sin(ang), 2, axis=1)
    sgn = sin * jnp.tile(jnp.array([-1.0, 1.0], F32), HEAD_DIM // 2)
    cos = jnp.tile(cos, (1, N_Q_HEADS))
    sgn = jnp.tile(sgn, (1, N_Q_HEADS))
    lane = jnp.arange(QW)
    bd = jnp.where(lane[:, None] // HEAD_DIM == lane[None, :] // HEAD_DIM, 1.0 / HEAD_DIM, 0.0).astype(BF16)
    dst = (lane // HEAD_DIM) * LANES + lane % HEAD_DIM
    spread = (dst[:, None] == jnp.arange(QP)[None, :]).astype(BF16)
    return dict(cos=cos, sgn=sgn, bd=bd, spread=spread, gather=spread.T)


def qkv_prep_fwd(proj_b, qg, kg, cst, n_ex):
    T = proj_b.shape[0]
    S = T // n_ex
    tm = _tile(S, 512)
    nb = S // tm

    def body(p_ref, qg_ref, kg_ref, cos_ref, sgn_ref, bd_ref, sp_ref, q_ref, k_ref, v_ref):
        pv = p_ref[...]
        cos, sgn, bd, sp = cos_ref[...], sgn_ref[...], bd_ref[...], sp_ref[...]
        qr = _rope_norm(pv[:, :QW], qg_ref[...], cos, sgn, bd, HEAD_DIM ** -0.5)
        kr = _rope_norm(pv[:, QW:QW + KW], kg_ref[...], cos[:, :KW], sgn[:, :KW], bd[:KW, :KW], 1.0)
        q_ref[...] = _nn(qr.astype(BF16), sp).astype(BF16)
        k_ref[...] = _nn(kr.astype(BF16), sp[:KW, :KP]).astype(BF16)
        v_ref[...] = _nn(pv[:, QW + KW:].astype(BF16), sp[:KW, :KP]).astype(BF16)

    full = lambda a: pl.BlockSpec(a.shape, lambda i: (0,) * a.ndim)
    tab = pl.BlockSpec((tm, QW), lambda i: (i % nb, 0))
    return pl.pallas_call(
        body, name="qkv_prep_fwd", grid=(T // tm,),
        in_specs=[pl.BlockSpec((tm, QW + 2 * KW), lambda i: (i, 0)), full(qg), full(kg), tab, tab,
                  full(cst["bd"]), full(cst["spread"])],
        out_specs=[pl.BlockSpec((tm, QP), lambda i: (i, 0)), pl.BlockSpec((tm, KP), lambda i: (i, 0)),
                   pl.BlockSpec((tm, KP), lambda i: (i, 0))],
        out_shape=[jax.ShapeDtypeStruct((T, QP), BF16), jax.ShapeDtypeStruct((T, KP), BF16),
                   jax.ShapeDtypeStruct((T, KP), BF16)],
        compiler_params=_cp("parallel"),
    )(proj_b, qg, kg, cst["cos"], cst["sgn"], cst["bd"], cst["spread"])


def qkv_prep_bwd(proj_b, qg, kg, cst, n_ex, dq, dk_pad, dv_pad):
    T = proj_b.shape[0]
    S = T // n_ex
    tm = _tile(S, 512)
    nb = S // tm

    def body(p_ref, qg_ref, kg_ref, cos_ref, sgn_ref, bd_ref, ga_ref, dq_ref, dk_ref, dv_ref,
             dp_ref, dqg_ref, dkg_ref):
        pv = p_ref[...]
        cos, sgn, bd, ga = cos_ref[...], sgn_ref[...], bd_ref[...], ga_ref[...]
        fq = lambda q, g: _rope_norm(q, g, cos, sgn, bd, HEAD_DIM ** -0.5)
        fk = lambda k, g: _rope_norm(k, g, cos[:, :KW], sgn[:, :KW], bd[:KW, :KW], 1.0)
        _, vq = jax.vjp(fq, pv[:, :QW], qg_ref[...])
        _, vk = jax.vjp(fk, pv[:, QW:QW + KW], kg_ref[...])
        dqp, dqg = vq(dq_ref[...])
        dkp, dkg = vk(_split_mm(dk_ref[...], ga[:KP, :KW]))
        dp_ref[:, :QW] = dqp.astype(BF16)
        dp_ref[:, QW:QW + KW] = dkp.astype(BF16)
        dp_ref[:, QW + KW:] = _split_mm(dv_ref[...], ga[:KP, :KW]).astype(BF16)

        @pl.when(pl.program_id(0) == 0)
        def _():
            dqg_ref[...] = jnp.zeros_like(dqg_ref)
            dkg_ref[...] = jnp.zeros_like(dkg_ref)

        dqg_ref[...] += dqg
        dkg_ref[...] += dkg

    full = lambda a: pl.BlockSpec(a.shape, lambda i: (0,) * a.ndim)
    tab = pl.BlockSpec((tm, QW), lambda i: (i % nb, 0))
    row = lambda n: pl.BlockSpec((tm, n), lambda i: (i, 0))
    return pl.pallas_call(
        body, name="qkv_prep_bwd", grid=(T // tm,),
        in_specs=[row(QW + 2 * KW), full(qg), full(kg), tab, tab, full(cst["bd"]), full(cst["gather"]),
                  row(QW), row(KP), row(KP)],
        out_specs=[row(QW + 2 * KW), pl.BlockSpec((1, QW), lambda i: (0, 0)), pl.BlockSpec((1, KW), lambda i: (0, 0))],
        out_shape=[jax.ShapeDtypeStruct((T, QW + 2 * KW), BF16), jax.ShapeDtypeStruct((1, QW), F32),
                   jax.ShapeDtypeStruct((1, KW), F32)],
        compiler_params=_cp("arbitrary"),
    )(proj_b, qg, kg, cst["cos"], cst["sgn"], cst["bd"], cst["gather"], dq, dk_pad, dv_pad)


ATT_TQ = 256


def attn_fwd(qp, kp, vp, gather, n_ex, job=None):
    T = qp.shape[0]
    S = T // n_ex
    tq = _tile(S, ATT_TQ)
    nq = S // tq

    def body(q_ref, k_ref, v_ref, ga_ref, o_ref, op_ref, lse_ref):
        lane = lax.broadcasted_iota(jnp.int32, (tq, LANES), 1)
        lse_all = jnp.zeros((tq, LANES), F32)
        for h in range(N_Q_HEADS):
            kv = h // Q_PER_KV
            qh = q_ref[:, h * LANES:(h + 1) * LANES]
            s = _nt(qh, k_ref[:, kv * LANES:(kv + 1) * LANES])
            m = jnp.max(s, axis=-1, keepdims=True)
            p = jnp.exp(s - m)
            lsum = jnp.sum(p, axis=-1, keepdims=True)
            o = _nn(p.astype(BF16), v_ref[:, kv * LANES:(kv + 1) * LANES]) / lsum
            op_ref[:, h * LANES:(h + 1) * LANES] = o.astype(BF16)
            lse_all = jnp.where(lane == h, m + jnp.log(lsum), lse_all)
        lse_ref[...] = lse_all
        o_ref[...] = _nn(op_ref[...], ga_ref[...]).astype(BF16)

    blk = lambda n: pl.BlockSpec((tq, n), lambda b, i: (b * nq + i, 0))
    kvs = pl.BlockSpec((S, KP), lambda b, i: (b, 0))
    return _call(
        body, "attn_fwd", (n_ex, nq),
        [blk(QP), kvs, kvs, pl.BlockSpec(gather.shape, lambda b, i: (0, 0))], [qp, kp, vp, gather],
        [blk(QW), blk(QP), blk(LANES)],
        [jax.ShapeDtypeStruct((T, QW), BF16), jax.ShapeDtypeStruct((T, QP), BF16),
         jax.ShapeDtypeStruct((T, LANES), F32)], (), ("parallel", "parallel"), job)


def attn_bwd(qp, kp, vp, op, lse, do, cst, n_ex, job=None):
    T = qp.shape[0]
    S = T // n_ex
    tq = _tile(S, ATT_TQ)
    nq = S // tq

    def body(q_ref, k_ref, v_ref, op_ref, lse_ref, do_ref, sp_ref, ga_ref, dq_ref, dk_ref, dv_ref, dqp_s):
        @pl.when(pl.program_id(1) == 0)
        def _():
            dk_ref[...] = jnp.zeros_like(dk_ref)
            dv_ref[...] = jnp.zeros_like(dv_ref)

        lane = lax.broadcasted_iota(jnp.int32, (tq, LANES), 1)
        dop = _nn(do_ref[...], sp_ref[...]).astype(BF16)
        lse_all = lse_ref[...]
        for h in range(N_Q_HEADS):
            kv = h // Q_PER_KV
            hs = slice(h * LANES, (h + 1) * LANES)
            ks = slice(kv * LANES, (kv + 1) * LANES)
            qh, kk, vv = q_ref[:, hs], k_ref[:, ks], v_ref[:, ks]
            doh = dop[:, hs]
            lse_h = jnp.sum(jnp.where(lane == h, lse_all, 0.0), axis=-1, keepdims=True)
            p = jnp.exp(_nt(qh, kk) - lse_h)
            dp = _nt(doh, vv)
            delta = jnp.sum(doh.astype(F32) * op_ref[:, hs].astype(F32), axis=-1, keepdims=True)
            ds = (p * (dp - delta)).astype(BF16)
            dqp_s[:, hs] = _nn(ds, kk)
            dk_ref[:, ks] += _tn(ds, qh)
            dv_ref[:, ks] += _tn(p.astype(BF16), doh)
        dq_ref[...] = _split_mm(dqp_s[...], ga_ref[...])

    blk = lambda n: pl.BlockSpec((tq, n), lambda b, i: (b * nq + i, 0))
    kvs = pl.BlockSpec((S, KP), lambda b, i: (b, 0))
    full = lambda a: pl.BlockSpec(a.shape, lambda b, i: (0, 0))
    return _call(
        body, "attn_bwd", (n_ex, nq),
        [blk(QP), kvs, kvs, blk(QP), blk(LANES), blk(QW), full(cst["spread"]), full(cst["gather"])],
        [qp, kp, vp, op, lse, do, cst["spread"], cst["gather"]],
        [blk(QW), kvs, kvs],
        [jax.ShapeDtypeStruct((T, QW), F32), jax.ShapeDtypeStruct((T, KP), F32), jax.ShapeDtypeStruct((T, KP), F32)],
        [pltpu.VMEM((tq, QP), F32)], ("arbitrary", "arbitrary"), job)


def pool_fwd(p, pool_w, scale, n_ex):
    T, W = p.shape
    S = T // n_ex

    def body(p_ref, w_ref, s_ref, o_ref):
        o_ref[...] = _pool_mix(p_ref[...], w_ref[...], s_ref[...]).astype(BF16)

    return pl.pallas_call(
        body, name="pool_fwd", grid=(n_ex,),
        in_specs=[pl.BlockSpec((S, W), lambda b: (b, 0)),
                  pl.BlockSpec(pool_w.shape, lambda b: (0, 0, 0)),
                  pl.BlockSpec((1, W), lambda b: (0, 0))],
        out_specs=pl.BlockSpec((S, W), lambda b: (b, 0)),
        out_shape=jax.ShapeDtypeStruct((T, W), BF16), compiler_params=_cp("parallel"),
    )(p, pool_w, scale)


def pool_bwd(p, pool_w, scale, n_ex, dy):
    T, W = p.shape
    S = T // n_ex

    def body(p_ref, w_ref, s_ref, dy_ref, dp_ref, dw_ref, ds_ref):
        _, vjp = jax.vjp(_pool_mix, p_ref[...], w_ref[...], s_ref[...])
        dp, dw, ds = vjp(dy_ref[...].astype(F32))
        dp_ref[...] = dp.astype(BF16)

        @pl.when(pl.program_id(0) == 0)
        def _():
            dw_ref[...] = jnp.zeros_like(dw_ref)
            ds_ref[...] = jnp.zeros_like(ds_ref)

        dw_ref[...] += dw
        ds_ref[...] += ds

    wshape = pool_w.shape
    return pl.pallas_call(
        body, name="pool_bwd", grid=(n_ex,),
        in_specs=[pl.BlockSpec((S, W), lambda b: (b, 0)),
                  pl.BlockSpec(wshape, lambda b: (0, 0, 0)),
                  pl.BlockSpec((1, W), lambda b: (0, 0)), pl.BlockSpec((S, W), lambda b: (b, 0))],
        out_specs=[pl.BlockSpec((S, W), lambda b: (b, 0)), pl.BlockSpec(wshape, lambda b: (0, 0, 0)),
                   pl.BlockSpec((1, W), lambda b: (0, 0))],
        out_shape=[jax.ShapeDtypeStruct((T, W), BF16), jax.ShapeDtypeStruct(wshape, F32),
                   jax.ShapeDtypeStruct((1, W), F32)],
        compiler_params=_cp("arbitrary"),
    )(p, pool_w, scale, dy)


SGU_TS = 512


def sgu_fwd(u, v, norm_g, w_s, b_full):
    T, W = u.shape
    ts = _tile(T, SGU_TS)

    def body(u_ref, v_ref, g_ref, w_ref, b_ref, o_ref):
        o_ref[...] = _sgu(u_ref[...], v_ref[...], g_ref[...], w_ref[...], b_ref[...]).astype(BF16)

    row = pl.BlockSpec((ts, W), lambda i: (i, 0))
    wsp = pl.BlockSpec(w_s.shape, lambda i: (0, 0, 0))
    return pl.pallas_call(
        body, name="sgu_fwd", grid=(T // ts,),
        in_specs=[row, row, pl.BlockSpec((1, W), lambda i: (0, 0)), wsp, wsp],
        out_specs=row, out_shape=jax.ShapeDtypeStruct((T, W), BF16), compiler_params=_cp("parallel"),
    )(u, v, norm_g, w_s, b_full)


def sgu_bwd(u, v, norm_g, w_s, b_full, dy):
    T, W = u.shape
    ts = _tile(T, SGU_TS)
    wshape = w_s.shape

    def body(u_ref, v_ref, g_ref, w_ref, b_ref, dy_ref, du_ref, dv_ref, dg_ref, dw_ref, db_ref):
        _, vjp = jax.vjp(_sgu, u_ref[...], v_ref[...], g_ref[...], w_ref[...], b_ref[...])
        du, dv, dg, dw, db = vjp(dy_ref[...].astype(F32))
        du_ref[...] = du.astype(BF16)
        dv_ref[...] = dv.astype(BF16)

        @pl.when(pl.program_id(0) == 0)
        def _():
            dg_ref[...] = jnp.zeros_like(dg_ref)
            dw_ref[...] = jnp.zeros_like(dw_ref)
            db_ref[...] = jnp.zeros_like(db_ref)

        dg_ref[...] += dg
        dw_ref[...] += dw
        db_ref[...] += db

    row = pl.BlockSpec((ts, W), lambda i: (i, 0))
    wsp = pl.BlockSpec(wshape, lambda i: (0, 0, 0))
    wout = pl.BlockSpec(wshape, lambda i: (0, 0, 0))
    vec = pl.BlockSpec((1, W), lambda i: (0, 0))
    return pl.pallas_call(
        body, name="sgu_bwd", grid=(T // ts,),
        in_specs=[row, row, vec, wsp, wsp, row],
        out_specs=[row, row, vec, wout, wout],
        out_shape=[jax.ShapeDtypeStruct((T, W), BF16), jax.ShapeDtypeStruct((T, W), BF16),
                   jax.ShapeDtypeStruct((1, W), F32), jax.ShapeDtypeStruct(wshape, F32),
                   jax.ShapeDtypeStruct(wshape, F32)],
        compiler_params=_cp("arbitrary"),
    )(u, v, norm_g, w_s, b_full, dy)


def loss_head(x, gain, target):
    T, D = x.shape
    tm = _tile(T, 512)

    def body(x_ref, g_ref, t_ref, loss_ref, dx_ref, dg_ref):
        xv, g = x_ref[...], g_ref[...]
        r = lax.rsqrt(jnp.mean(xv * xv, axis=-1, keepdims=True) + EPS)
        err = xv * r * g - t_ref[...]
        part = 0.5 * jnp.sum(jnp.mean(err * err, axis=-1, keepdims=True), axis=0, keepdims=True)
        dx, dg = _rms_bwd_math(xv, g, err * (1.0 / D), jnp.zeros_like(xv))
        dx_ref[...] = dx

        @pl.when(pl.program_id(0) == 0)
        def _():
            loss_ref[...] = jnp.zeros_like(loss_ref)
            dg_ref[...] = jnp.zeros_like(dg_ref)

        loss_ref[...] += part
        dg_ref[...] += dg

    row = pl.BlockSpec((tm, D), lambda i: (i, 0))
    vec = pl.BlockSpec((1, D), lambda i: (0, 0))
    return pl.pallas_call(
        body, name="loss_head", grid=(T // tm,),
        in_specs=[row, vec, row], out_specs=[pl.BlockSpec((1, 1), lambda i: (0, 0)), row, vec],
        out_shape=[jax.ShapeDtypeStruct((1, 1), F32), jax.ShapeDtypeStruct((T, D), F32),
                   jax.ShapeDtypeStruct((1, D), F32)],
        compiler_params=_cp("arbitrary"),
    )(x, gain, target)


class MultiJob:
    def __init__(self, jobs):
        self.jobs = jobs
        self.args = [a for j in jobs for a in j.args]
        self.out_shape = [s for j in jobs for s in j.out_shape]
        self.scratch = [s for j in jobs for s in j.scratch]
        self.n_in, self.n_out = len(self.args), len(self.out_shape)

    def _each(self, ins, outs, sems):
        i = o = s = 0
        for j in self.jobs:
            yield j, ins[i:i + j.n_in], outs[o:o + j.n_out], sems[s:s + len(j.scratch)]
            i, o, s = i + j.n_in, o + j.n_out, s + len(j.scratch)

    def start(self, ins, outs, sems):
        for j, a, b, c in self._each(ins, outs, sems):
            j.start(a, b, c)

    def finish(self, ins, outs, sems):
        for j, a, b, c in self._each(ins, outs, sems):
            j.finish(a, b, c)

    def split(self, results):
        o = 0
        for j in self.jobs:
            yield results[o:o + j.n_out]
            o += j.n_out


class Plan:
    def __init__(self, shard, gathers, scatters, small_carrier=None, pack_small=None):
        self.shard, self.gathers, self.scatters = shard, gathers, scatters
        self.small_carrier, self.pack_small = small_carrier, pack_small
        self.weights, self.grads, self.parts = {}, {}, {}
        self.small_src = self.small_parts = None

    def weight(self, kind, l):
        return self.weights[(kind, l)]

    def grad(self, kind, l, g):
        self.grads[(kind, l)] = g

    def small_ready(self, small, d_final):
        if self.pack_small is not None:
            self.small_src = self.pack_small(small, d_final)

    def _jobs(self, key):
        jobs = []
        if key in self.gathers:
            ks = self.gathers[key]
            jobs.append((GatherJob([self.shard(*k) for k in ks]), self.weights, ks))
        if key in self.scatters:
            ks = self.scatters[key]
            jobs.append((ScatterJob([self.grads[k] for k in ks]), self.parts, ks))
        if key == self.small_carrier and self.small_src is not None:
            jobs.append((GatherJob([(self.small_src, None)]), None, None))
        return jobs

    def _deliver(self, jobs, results):
        multi = MultiJob([j for j, _, _ in jobs])
        for (_, store, ks), res in zip(jobs, multi.split(results)):
            if store is None:
                self.small_parts = res[0]
            else:
                store.update(zip(ks, res))

    def run(self, key, fn, *args, **kw):
        jobs = self._jobs(key)
        if not jobs:
            out = fn(*args, **kw)
            return out if fn is mm_tn else out[0]
        res, jres = fn(*args, job=MultiJob([j for j, _, _ in jobs]), **kw)
        self._deliver(jobs, jres)
        return res

    def alone(self, key, name):
        jobs = self._jobs(key)
        if jobs:
            self._deliver(jobs, run_job(MultiJob([j for j, _, _ in jobs]), name))


def _local_step(x, target, layers, final_norm, n_ex, plan):
    T, D = x.shape
    L = len(layers)
    cst = _attn_consts(T // n_ex)
    ident = lambda j: j
    EV_A, EV_B = 3 * (D // 2), QW + 2 * KW
    OD_W = D // 2
    wt = plan.weight

    saved = []
    for l, W in enumerate(layers):
        s = dict(x0=x)
        x1, s["gu1"] = plan.run(("ffn1_fwd", l), ffn_fwd, x, W["n1"], wt("f1_in_t", l), wt("f1_out", l))
        h = rms_fwd(x1, W["nm"])
        if l % 2 == 0:
            pa, pb = mm_nt(h, wt("mx_in_t", l), [(0, EV_A), (EV_A, EV_B)], F32)
            qg = jnp.tile(W["q_norm"], N_Q_HEADS)[None]
            kg = jnp.tile(W["k_norm"], N_KV_HEADS)[None]
            mix_a = conv_fwd(pa, W["conv_w"], n_ex)
            qp, kp, vp = qkv_prep_fwd(pb, qg, kg, cst, n_ex)
            mix_b, op, lse = plan.run(("attn_fwd", l), attn_fwd, qp, kp, vp, cst["gather"], n_ex)
            s.update(pa=pa, pb=pb, qg=qg, kg=kg, qp=qp, kp=kp, vp=vp, op=op, lse=lse)
        else:
            p, u, v = mm_nt(h, wt("mx_in_t", l), [(0, OD_W), (OD_W, OD_W), (2 * OD_W, OD_W)], F32)
            scale = W["pool_scale"][None]
            sn = W["sgu_norm"][None]
            b_full = jnp.broadcast_to(W["sgu_b"][..., None], W["sgu_w"].shape)
            mix_a = pool_fwd(p, W["pool_w"], scale, n_ex)
            mix_b = sgu_fwd(u, v, sn, W["sgu_w"], b_full)
            s.update(p=p, u=u, v=v, scale=scale, sn=sn, b_full=b_full)
        x2 = mm_nn([mix_a, mix_b], wt("mx_out", l), residual=x1)
        x3, s["gu2"] = plan.run(("ffn2_fwd", l), ffn_fwd, x2, W["n2"], wt("f2_in_t", l), wt("f2_out", l))
        s.update(x1=x1, x2=x2, h=h, mix_a=mix_a, mix_b=mix_b)
        saved.append(s)
        x = x3

    loss, dx, d_final = loss_head(x, final_norm, target)

    small = [None] * L

    def ffn_back(which, l, dout, xin, gain, gu, sm, sm_key):
        w_in, w_out = wt(which + "_in_t", l), wt(which + "_out", l)
        F = w_out.shape[0]
        nc = F // FFN_TN
        dxi, sm[sm_key], a, dgu, xn, dob = plan.run((which + "_bwd", l), ffn_bwd_x, dout, xin, gain, gu, w_in, w_out)
        plan.grad(which + "_out", l, plan.run((which + "_out_grad", l), mm_tn, a, dob, F, ident))
        if which == "f1" and l == 0:
            plan.small_ready(small, d_final)
        plan.grad(which + "_in_t", l, plan.run(
            (which + "_in_grad", l), mm_tn, dgu, xn, 2 * F, lambda k, c: k * nc + c, grid=(2, nc),
            col_block=lambda k, c: 2 * c + k))
        return dxi

    for l in reversed(range(L)):
        s, W = saved[l], layers[l]
        sm = small[l] = {}
        dx = ffn_back("f2", l, dx, s["x2"], W["n2"], s["gu2"], sm, "n2")
        dm_a, dm_b, dxb = mm_nt(dx, wt("mx_out", l), [(0, D // 2), (D // 2, D // 2)], BF16, emit_a_bf16=True)
        half_blocks = (D // 2) // MM_TC
        g_out = mm_tn(s["mix_a"], dxb, D, ident)
        plan.grad("mx_out", l, mm_tn(s["mix_b"], dxb, D, lambda jj: jj + half_blocks, prev=g_out))
        if l % 2 == 0:
            d_a, sm["conv_w"] = conv_bwd(s["pa"], W["conv_w"], n_ex, dm_a)
            dq, dkp, dvp = plan.run(("attn_bwd", l), attn_bwd, s["qp"], s["kp"], s["vp"], s["op"], s["lse"], dm_b, cst, n_ex)
            d_b, dqg, dkg = qkv_prep_bwd(s["pb"], s["qg"], s["kg"], cst, n_ex, dq, dkp, dvp)
            dh = mm_nn([d_a, d_b], wt("mx_in_t", l))
            g_in = mm_tn(d_a, s["h"], EV_A + EV_B, ident)
            plan.grad("mx_in_t", l, mm_tn(d_b, s["h"], EV_A + EV_B, lambda jj: jj + EV_A // MM_TC, prev=g_in))
            sm["q_norm"] = dqg.reshape(N_Q_HEADS, HEAD_DIM).sum(0)
            sm["k_norm"] = dkg.reshape(N_KV_HEADS, HEAD_DIM).sum(0)
        else:
            d_p, sm["pool_w"], d_ps = pool_bwd(s["p"], W["pool_w"], s["scale"], n_ex, dm_a)
            d_u, d_v, d_sn, sm["sgu_w"], d_sb = sgu_bwd(s["u"], s["v"], s["sn"], W["sgu_w"], s["b_full"], dm_b)
            dh = mm_nn([d_p, d_u, d_v], wt("mx_in_t", l))
            nb = OD_W // MM_TC
            g_in = mm_tn(d_p, s["h"], 3 * OD_W, ident)
            g_in = mm_tn(d_u, s["h"], 3 * OD_W, lambda jj: jj + nb, prev=g_in)
            plan.grad("mx_in_t", l, mm_tn(d_v, s["h"], 3 * OD_W, lambda jj: jj + 2 * nb, prev=g_in))
            sm["pool_scale"], sm["sgu_norm"], sm["sgu_b"] = d_ps[0], d_sn[0], d_sb.sum(-1)
        dx, sm["nm"] = rms_bwd(s["x1"], W["nm"], dh, dx)
        dx = ffn_back("f1", l, dx, s["x0"], W["n1"], s["gu1"], sm, "n1")
    return loss, dx


def all_gather(srcs):
    return run_job(GatherJob(srcs), "all_gather")


def all_reduce_small(v):
    R, C = v.shape

    def body(v_ref, o_ref, buf, send_sems, recv_sems):
        x, y, c, me = _my_place()
        buf[me] = v_ref[...]
        for k in range(1, N_DEV):
            peer, _ = _peer(x, y, c, k)
            pltpu.make_async_remote_copy(src_ref=v_ref, dst_ref=buf.at[me], send_sem=send_sems.at[k - 1],
                                         recv_sem=recv_sems.at[k - 1], device_id=peer, device_id_type=MESH_ID).start()
        for k in range(1, N_DEV):
            peer, pidx = _peer(x, y, c, k)
            cp = pltpu.make_async_remote_copy(src_ref=v_ref, dst_ref=buf.at[pidx], send_sem=send_sems.at[k - 1],
                                              recv_sem=recv_sems.at[k - 1], device_id=peer, device_id_type=MESH_ID)
            cp.wait_recv()
            cp.wait_send()
        acc = buf[0]
        for s in range(1, N_DEV):
            acc = acc + buf[s]
        o_ref[...] = acc

    vm = pl.BlockSpec(memory_space=pltpu.VMEM)
    return pl.pallas_call(
        body, name="all_reduce_small", in_specs=[vm], out_specs=vm, out_shape=jax.ShapeDtypeStruct((R, C), F32),
        scratch_shapes=[pltpu.VMEM((N_DEV, R, C), F32), pltpu.SemaphoreType.DMA((7,)), pltpu.SemaphoreType.DMA((7,))],
        compiler_params=pltpu.CompilerParams(vmem_limit_bytes=VMEM_LIMIT),
    )(v)


def cast_shards(w, transpose):
    L, A, B = w.shape
    oshape = (B, A) if transpose else (A, B)

    def body(w_ref, o_ref):
        wv = w_ref[...]
        o_ref[...] = (wv.T if transpose else wv).astype(BF16)

    return pl.pallas_call(
        body, name="cast_shards", grid=(L,),
        in_specs=[pl.BlockSpec((None, A, B), lambda l: (l, 0, 0))],
        out_specs=pl.BlockSpec((None,) + oshape, lambda l: (l, 0, 0)),
        out_shape=jax.ShapeDtypeStruct((L,) + oshape, BF16), compiler_params=_cp("parallel"),
    )(w)


ADAM_TC = 256


def adamw(parts, w, m, v, l, transpose, prev=None):
    P, R, C = parts.shape
    tc = _tile(C, ADAM_TC)
    c1, c2 = 1.0 - ADAM_B1 ** ADAM_STEP, 1.0 - ADAM_B2 ** ADAM_STEP

    def body(p_ref, w_ref, m_ref, v_ref, g_ref, d_ref, mo_ref, vo_ref):
        g = p_ref[0].astype(F32)
        for s in range(1, P):
            g = g + p_ref[s].astype(F32)
        if transpose:
            g = g.T
        m1 = ADAM_B1 * m_ref[...] + (1.0 - ADAM_B1) * g
        v1 = ADAM_B2 * v_ref[...] + (1.0 - ADAM_B2) * (g * g)
        g_ref[...] = g
        mo_ref[...] = m1
        vo_ref[...] = v1
        d_ref[...] = -ADAM_LR * ((m1 / c1) / (jnp.sqrt(v1 / c2) + ADAM_EPS) + ADAM_WD * w_ref[...])

    if transpose:
        wspec = pl.BlockSpec((None, tc, R), lambda i: (l, i, 0))
    else:
        wspec = pl.BlockSpec((None, R, tc), lambda i: (l, 0, i))
    prev = list(prev) if prev is not None else []
    return pl.pallas_call(
        lambda *refs: body(*refs[:4], *refs[4 + len(prev):]), name="adamw", grid=(C // tc,),
        in_specs=[pl.BlockSpec((P, R, tc), lambda i: (0, 0, i)), wspec, wspec, wspec] + [ANY] * len(prev),
        out_specs=[wspec] * 4, out_shape=[jax.ShapeDtypeStruct(w.shape, F32)] * 4,
        input_output_aliases={4 + i: i for i in range(len(prev))},
        compiler_params=_cp("parallel"),
    )(parts, w, m, v, *prev)


_WEIGHTS = ['ffn1_norm', 'ffn1_w_in', 'ffn1_w_out', 'mix_norm', 'ffn2_norm', 'ffn2_w_in', 'ffn2_w_out', 'ev_w_in',
            'ev_conv_w', 'ev_q_norm', 'ev_k_norm', 'ev_w_out', 'od_w_in', 'od_pool_w', 'od_pool_scale', 'od_sgu_norm',
            'od_sgu_w', 'od_sgu_b', 'od_w_out', 'final_norm']
_BIG = dict(ffn1_w_in=True, ffn1_w_out=False, ffn2_w_in=True, ffn2_w_out=False,
            ev_w_in=True, ev_w_out=False, od_w_in=True, od_w_out=False)
_SMALL_SHARDED = ['ev_conv_w', 'od_pool_scale', 'od_sgu_norm']
_SMALL = [n for n in _WEIGHTS if n not in _BIG]
_PACK_ROWS = 8 * LANES


_KINDS = ("f1_in_t", "f1_out", "mx_in_t", "mx_out", "f2_in_t", "f2_out")
_CARRIER_US = dict(ffn1_fwd=105, ffn2_fwd=105, attn_fwd=115, f2_bwd=165, f1_bwd=165, attn_bwd=205,
                   f2_out_grad=33, f1_out_grad=33, f2_in_grad=61, f1_in_grad=61)
_GATHER_US_PER_ROW, _SCATTER_US_PER_ROW, _SMALL_GATHER_US, _SLACK_US = 0.088, 0.176, 47, 10


def _schedule(L, rows):
    events = []
    for l in range(L):
        events += [("ffn1_fwd", l), ("mixer", l)] + ([("attn_fwd", l)] if l % 2 == 0 else []) + [("ffn2_fwd", l)]
    consumer = {"f1": "ffn1_fwd", "mx": "mixer", "f2": "ffn2_fwd"}
    queue = [(k, l) for l in range(L) for k in _KINDS]
    pos = {t: events.index((consumer[t[0][:2]], t[1])) for t in queue}
    gathers = {"first": [t for t in queue if pos[t] == 0]}
    queue = [t for t in queue if pos[t] > 0]
    carriers = [i for i, e in enumerate(events) if e[0] in _CARRIER_US]
    for i in carriers:
        budget, take = _CARRIER_US[events[i][0]], []
        later = [j for j in carriers if j > i]
        while queue:
            t = queue[0]
            cost = rows(*t) * _GATHER_US_PER_ROW
            forced = not any(j < pos[t] for j in later)
            if not forced and cost > budget:
                break
            take.append(queue.pop(0))
            budget -= cost
        if take:
            gathers[events[i]] = take
    assert not queue
    events = []
    for l in reversed(range(L)):
        events += [("f2_bwd", l), ("f2_out_grad", l), ("f2_in_grad", l), ("mx_out_ready", l)]
        events += [("attn_bwd", l)] if l % 2 == 0 else []
        events += [("mx_in_ready", l), ("f1_bwd", l), ("f1_out_grad", l), ("f1_in_grad", l)]
    made_by = {"f2_out": "f2_out_grad", "f2_in_t": "f2_in_grad", "mx_out": "mx_out_ready", "mx_in_t": "mx_in_ready",
               "f1_out": "f1_out_grad", "f1_in_t": "f1_in_grad"}
    small_carrier = ("f1_in_grad", 0)
    scatters, ready = {}, []
    for e in events:
        if e[0] in _CARRIER_US:
            budget, take = _CARRIER_US[e[0]] - (_SMALL_GATHER_US if e == small_carrier else 0), []
            while True:
                fits = [t for t in ready if rows(*t) * _SCATTER_US_PER_ROW <= budget + _SLACK_US]
                if not fits:
                    break
                t = max(fits, key=lambda u: rows(*u))
                budget -= rows(*t) * _SCATTER_US_PER_ROW
                ready.remove(t)
                take.append(t)
            if take:
                scatters[e] = take
        ready += [(k, e[1]) for k in _KINDS if made_by[k] == e[0]]
    scatters["last"] = ready
    return gathers, scatters, small_carrier


def _pack(arrs):
    flat = jnp.concatenate([a.reshape(-1) for a in arrs])
    pad = (-flat.shape[0]) % _PACK_ROWS
    return jnp.pad(flat, (0, pad)).reshape(-1, LANES)


def _unpack(buf, shapes):
    flat, out, off = buf.reshape(-1), [], 0
    for s in shapes:
        n = math.prod(s)
        out.append(flat[off:off + n].reshape(s))
        off += n
    return out


def _unshard_last(g, lead):
    nd = len(lead)
    return jnp.moveaxis(g, 0, nd).reshape(*lead, -1)


def kernel(x, ffn1_norm, ffn1_w_in, ffn1_w_out, mix_norm, ffn2_norm, ffn2_w_in, ffn2_w_out, ev_w_in, ev_conv_w, ev_q_norm, ev_k_norm, ev_w_out, od_w_in, od_pool_w, od_pool_scale, od_sgu_norm, od_sgu_w, od_sgu_b, od_w_out, final_norm, loss_target, m_ffn1_norm, m_ffn1_w_in, m_ffn1_w_out, m_mix_norm, m_ffn2_norm, m_ffn2_w_in, m_ffn2_w_out, m_ev_w_in, m_ev_conv_w, m_ev_q_norm, m_ev_k_norm, m_ev_w_out, m_od_w_in, m_od_pool_w, m_od_pool_scale, m_od_sgu_norm, m_od_sgu_w, m_od_sgu_b, m_od_w_out, m_final_norm, v_ffn1_norm, v_ffn1_w_in, v_ffn1_w_out, v_mix_norm, v_ffn2_norm, v_ffn2_w_in, v_ffn2_w_out, v_ev_w_in, v_ev_conv_w, v_ev_q_norm, v_ev_k_norm, v_ev_w_out, v_od_w_in, v_od_pool_w, v_od_pool_scale, v_od_sgu_norm, v_od_sgu_w, v_od_sgu_b, v_od_w_out, v_final_norm):
    w = dict(zip(_WEIGHTS, (ffn1_norm, ffn1_w_in, ffn1_w_out, mix_norm, ffn2_norm, ffn2_w_in, ffn2_w_out, ev_w_in, ev_conv_w, ev_q_norm, ev_k_norm, ev_w_out, od_w_in, od_pool_w, od_pool_scale, od_sgu_norm, od_sgu_w, od_sgu_b, od_w_out, final_norm)))
    m = dict(zip(_WEIGHTS, (m_ffn1_norm, m_ffn1_w_in, m_ffn1_w_out, m_mix_norm, m_ffn2_norm, m_ffn2_w_in, m_ffn2_w_out, m_ev_w_in, m_ev_conv_w, m_ev_q_norm, m_ev_k_norm, m_ev_w_out, m_od_w_in, m_od_pool_w, m_od_pool_scale, m_od_sgu_norm, m_od_sgu_w, m_od_sgu_b, m_od_w_out, m_final_norm)))
    v = dict(zip(_WEIGHTS, (v_ffn1_norm, v_ffn1_w_in, v_ffn1_w_out, v_mix_norm, v_ffn2_norm, v_ffn2_w_in, v_ffn2_w_out, v_ev_w_in, v_ev_conv_w, v_ev_q_norm, v_ev_k_norm, v_ev_w_out, v_od_w_in, v_od_pool_w, v_od_pool_scale, v_od_sgu_norm, v_od_sgu_w, v_od_sgu_b, v_od_w_out, v_final_norm)))
    n_ex, seq, D = x.shape
    T = n_ex * seq
    L = ffn1_norm.shape[0]
    me = 4 * lax.axis_index("x") + 2 * lax.axis_index("y") + lax.axis_index("c")

    sh_small = [w[n] for n in _SMALL_SHARDED]
    packed = all_gather([(_pack(sh_small), None)])[0].reshape(N_DEV, -1)
    full_small = {}
    off = 0
    for n, a in zip(_SMALL_SHARDED, sh_small):
        cnt = math.prod(a.shape)
        full_small[n] = _unshard_last(packed[:, off:off + cnt].reshape((N_DEV,) + a.shape), a.shape[:-1])
        off += cnt

    shards = {n: cast_shards(w[n], t) for n, t in _BIG.items()}

    def name_of(kind, l):
        mx = "ev" if l % 2 == 0 else "od"
        return {"f1_in_t": "ffn1_w_in", "f1_out": "ffn1_w_out", "f2_in_t": "ffn2_w_in", "f2_out": "ffn2_w_out",
                "mx_in_t": mx + "_w_in", "mx_out": mx + "_w_out"}[kind], (l // 2 if kind.startswith("mx") else l)

    def shard(kind, l):
        name, idx = name_of(kind, l)
        return shards[name], idx

    g_shapes = {}

    def pack_small(small, d_final):
        ev = [sm for l, sm in enumerate(small) if l % 2 == 0]
        od = [sm for l, sm in enumerate(small) if l % 2 == 1]
        st = lambda sms, k: jnp.stack([sm[k] for sm in sms])
        g_full = dict(ffn1_norm=st(small, "n1")[:, 0], mix_norm=st(small, "nm")[:, 0], ffn2_norm=st(small, "n2")[:, 0],
                      ev_conv_w=st(ev, "conv_w"), ev_q_norm=st(ev, "q_norm"), ev_k_norm=st(ev, "k_norm"),
                      od_pool_w=st(od, "pool_w"), od_pool_scale=st(od, "pool_scale"), od_sgu_norm=st(od, "sgu_norm"),
                      od_sgu_w=st(od, "sgu_w"), od_sgu_b=st(od, "sgu_b"), final_norm=d_final[0])
        g_shapes.update({n: g_full[n].shape for n in _SMALL})
        return _pack([g_full[n] for n in _SMALL])

    gathers, scatters, small_carrier = _schedule(L, lambda kind, l: shards[name_of(kind, l)[0]].shape[1])
    plan = Plan(shard, gathers, scatters, small_carrier, pack_small)
    layers = []
    for l in range(L):
        j = l // 2
        W = dict(n1=ffn1_norm[l][None], nm=mix_norm[l][None], n2=ffn2_norm[l][None])
        if l % 2 == 0:
            W.update(conv_w=full_small["ev_conv_w"][j], q_norm=ev_q_norm[j], k_norm=ev_k_norm[j])
        else:
            W.update(pool_w=od_pool_w[j], pool_scale=full_small["od_pool_scale"][j], sgu_norm=full_small["od_sgu_norm"][j],
                     sgu_w=od_sgu_w[j], sgu_b=od_sgu_b[j])
        layers.append(W)

    plan.alone("first", "gather_first")
    loss, dx = _local_step(x.reshape(T, D), loss_target.reshape(T, D), layers, final_norm[None], n_ex, plan)
    plan.alone("last", "scatter_last")

    out = {n: None for n in _BIG}
    for (kind, l), parts in plan.parts.items():
        name, idx = name_of(kind, l)
        out[name] = adamw(parts, w[name], m[name], v[name], idx, _BIG[name], prev=out[name])

    g8 = plan.small_parts.reshape(N_DEV, -1)
    cols, off = [], 0
    for n in _SMALL:
        cnt = math.prod(g_shapes[n])
        g = g8[:, off:off + cnt].reshape((N_DEV,) + g_shapes[n])
        off += cnt
        if n in _SMALL_SHARDED:
            width = w[n].shape[-1]
            g = lax.dynamic_slice_in_dim(g, me * width, width, axis=g.ndim - 1)
        cols.append(g.reshape(N_DEV, -1))
    g8 = jnp.concatenate(cols, axis=1)
    g8 = jnp.pad(g8, ((0, 0), (0, (-g8.shape[1]) % _PACK_ROWS))).reshape(N_DEV, -1, LANES)
    pk = lambda d: _pack([d[n] for n in _SMALL])[None]
    small_out = adamw(g8, pk(w), pk(m), pk(v), 0, False)
    shapes = [w[n].shape for n in _SMALL]
    for i in range(4):
        for n, a in zip(_SMALL, _unpack(small_out[i], shapes)):
            out.setdefault(n, [None] * 4)[i] = a

    total = lax.psum(loss[0, 0], ("x", "y", "c"))
    return (total, dx.reshape(n_ex, seq, D), *[out[n][0] for n in _WEIGHTS], *[out[n][1] for n in _WEIGHTS],
            *[out[n][2] for n in _WEIGHTS], *[out[n][3] for n in _WEIGHTS])
```

```python
import functools
import math

import jax
import jax.numpy as jnp
from jax import lax
from jax.experimental import pallas as pl
from jax.experimental.pallas import tpu as pltpu

F32, BF16 = jnp.float32, jnp.bfloat16
EPS = 1e-6
N_DEV = 8
V7X_VMEM_BYTES = 64 * 1024 * 1024
VMEM_LIMIT = V7X_VMEM_BYTES - 8 * 1024 * 1024
LANES = 128
HEAD_DIM = 64
N_Q_HEADS = 8
N_KV_HEADS = 2
Q_PER_KV = N_Q_HEADS // N_KV_HEADS
GRID_W = 64
ROPE_THETA = 10000.0
POOL_RADII = (1, 2, 4, 8)
SGU_CHUNK = 128
GROUP = 128
ADAM_LR, ADAM_B1, ADAM_B2, ADAM_EPS, ADAM_WD, ADAM_STEP = 0.001, 0.9, 0.999, 1e-08, 0.01, 10
MESH_ID = pl.DeviceIdType.MESH


def _cp(*sem):
    return pltpu.CompilerParams(dimension_semantics=sem, vmem_limit_bytes=VMEM_LIMIT)


def _dot(a, b, ca, cb):
    return lax.dot_general(a, b, (((ca,), (cb,)), ((), ())), preferred_element_type=F32)


def _nn(a, b):
    return _dot(a, b, 1, 0)


def _nt(a, b):
    return _dot(a, b, 1, 1)


def _tn(a, b):
    return _dot(a, b, 0, 0)


def _split_mm(x, m):
    hi = x.astype(BF16)
    lo = (x - hi.astype(F32)).astype(BF16)
    return _nn(hi, m) + _nn(lo, m)


def _tile(n, pref):
    t = min(n, pref)
    assert n % t == 0, (n, pref)
    return t


ANY = pl.BlockSpec(memory_space=pl.ANY)
OTHER_CHIPS = (2, 4, 6)


def _my_place():
    x, y, c = lax.axis_index("x"), lax.axis_index("y"), lax.axis_index("c")
    return x, y, c, 4 * x + 2 * y + c


def _peer(x, y, c, k):
    px = 1 - x if k & 4 else x
    py = 1 - y if k & 2 else y
    pc = 1 - c if k & 1 else c
    return (px, py, pc), 4 * px + 2 * py + pc


def _remote(src, dst, send_sems, recv_sems, i, peer):
    return pltpu.make_async_remote_copy(src_ref=src, dst_ref=dst, send_sem=send_sems.at[i], recv_sem=recv_sems.at[i],
                                        device_id=peer, device_id_type=MESH_ID)


class GatherJob:
    def __init__(self, srcs):
        self.srcs = srcs
        self.args = [a for a, _ in srcs]
        self.dims = [a.shape[-2:] for a, _ in srcs]
        n = self.n_in = self.n_out = len(srcs)
        self.out_shape = [jax.ShapeDtypeStruct((N_DEV * r, cc), a.dtype) for (a, _), (r, cc) in zip(srcs, self.dims)]
        self.scratch = [pltpu.SemaphoreType.DMA((N_DEV * n,)), pltpu.SemaphoreType.DMA((N_DEV * n,)),
                        pltpu.SemaphoreType.DMA((n,))]

    def _rows(self, outs, t, idx):
        r = self.dims[t][0]
        return outs[t].at[pl.ds(pl.multiple_of(idx * r, 8), r), :]

    def _local(self, ins, outs, loc, t, me):
        src = ins[t] if self.srcs[t][1] is None else ins[t].at[self.srcs[t][1]]
        return src, pltpu.make_async_copy(src, self._rows(outs, t, me), loc.at[t])

    def start(self, ins, outs, sems):
        send, recv, loc = sems
        x, y, c, me = _my_place()
        for t in range(self.n_in):
            src, local = self._local(ins, outs, loc, t, me)
            local.start()
            for k in (2, 4, 1):
                _remote(src, self._rows(outs, t, me), send, recv, N_DEV * t + k, _peer(x, y, c, k)[0]).start()

    def finish(self, ins, outs, sems):
        send, recv, loc = sems
        x, y, c, me = _my_place()
        sibling = _peer(x, y, c, 1)[0]

        def block(t, k):
            return self._rows(outs, t, _peer(x, y, c, k)[1])

        def arrived(t, k):
            _remote(block(t, k), block(t, k), send, recv, N_DEV * t + k, _peer(x, y, c, k)[0]).wait_recv()

        def relay(t, got, to):
            arrived(t, got)
            _remote(block(t, got), block(t, got), send, recv, N_DEV * t + 6, _peer(x, y, c, to)[0]).start()
            arrived(t, to)

        for t in range(self.n_in):
            pl.when(c == 1)(functools.partial(relay, t, 2, 4))
            pl.when(c == 0)(functools.partial(relay, t, 4, 2))
            for k in (2, 4):
                _remote(block(t, k), block(t, k), send, recv, N_DEV * t + k + 1, sibling).start()
        for t in range(self.n_in):
            arrived(t, 6)
            _remote(block(t, 6), block(t, 6), send, recv, N_DEV * t + 7, sibling).start()
        for t in range(self.n_in):
            for k in range(1, N_DEV):
                peer, pidx = _peer(x, y, c, k)
                blk = self._rows(outs, t, pidx)
                if k % 2 == 1:
                    _remote(blk, blk, send, recv, N_DEV * t + k, peer).wait_recv()
                _remote(blk, blk, send, recv, N_DEV * t + k, peer).wait_send()
            self._local(ins, outs, loc, t, me)[1].wait()


class ScatterJob:
    def __init__(self, grads):
        self.args = list(grads)
        self.dims = [(g.shape[0] // N_DEV, g.shape[1]) for g in grads]
        n = self.n_in = self.n_out = len(grads)
        self.out_shape = [jax.ShapeDtypeStruct((N_DEV, r, cc), g.dtype) for g, (r, cc) in zip(grads, self.dims)]
        self.scratch = [pltpu.SemaphoreType.DMA((N_DEV * n,)), pltpu.SemaphoreType.DMA((N_DEV * n,)),
                        pltpu.SemaphoreType.DMA((n,))]

    def _rows(self, ins, t, idx):
        r = self.dims[t][0]
        return ins[t].at[pl.ds(pl.multiple_of(idx * r, 8), r), :]

    def start(self, ins, outs, sems):
        send, recv, loc = sems
        x, y, c, me = _my_place()
        for t in range(self.n_in):
            pltpu.make_async_copy(self._rows(ins, t, me), outs[t].at[me], loc.at[t]).start()
            for k in OTHER_CHIPS + (1, 3, 5, 7):
                peer, pidx = _peer(x, y, c, k)
                _remote(self._rows(ins, t, pidx), outs[t].at[me], send, recv, N_DEV * t + k, peer).start()

    def finish(self, ins, outs, sems):
        send, recv, loc = sems
        x, y, c, me = _my_place()
        for t in range(self.n_in):
            for k in range(1, N_DEV):
                peer, pidx = _peer(x, y, c, k)
                cp = _remote(self._rows(ins, t, pidx), outs[t].at[pidx], send, recv, N_DEV * t + k, peer)
                cp.wait_recv()
                cp.wait_send()
            pltpu.make_async_copy(self._rows(ins, t, me), outs[t].at[me], loc.at[t]).wait()


def _call(body, name, grid, in_specs, args, out_specs, out_shape, scratch=(), sem=(), job=None, aliases=None):
    in_specs, out_specs, out_shape, scratch = list(in_specs), list(out_specs), list(out_shape), list(scratch)
    n_in, n_out, n_scr = len(args), len(out_shape), len(scratch)
    if job is None:
        res = pl.pallas_call(body, name=name, grid=grid, in_specs=in_specs, out_specs=out_specs, out_shape=out_shape,
                             scratch_shapes=scratch, input_output_aliases=aliases or {}, compiler_params=_cp(*sem))(*args)
        return res, None
    o0 = n_in + job.n_in
    s0 = o0 + n_out + job.n_out

    def carrier(*refs):
        jin, jout, jsem = refs[n_in:o0], refs[o0 + n_out:s0], refs[s0 + n_scr:]
        ids = [pl.program_id(a) for a in range(len(grid))]
        if grid:
            first = functools.reduce(jnp.logical_and, [i == 0 for i in ids])
            last = functools.reduce(jnp.logical_and, [i == g - 1 for i, g in zip(ids, grid)])
            pl.when(first)(lambda: job.start(jin, jout, jsem))
        else:
            job.start(jin, jout, jsem)
        body(*refs[:n_in], *refs[o0:o0 + n_out], *refs[s0:s0 + n_scr])
        if grid:
            pl.when(last)(lambda: job.finish(jin, jout, jsem))
        else:
            job.finish(jin, jout, jsem)

    res = pl.pallas_call(
        carrier, name=name + "_comm", grid=grid, in_specs=in_specs + [ANY] * job.n_in,
        out_specs=out_specs + [ANY] * job.n_out, out_shape=out_shape + job.out_shape,
        scratch_shapes=scratch + job.scratch, input_output_aliases=aliases or {},
        compiler_params=_cp(*(["arbitrary"] * len(grid))))(*args, *job.args)
    return res[:n_out], res[n_out:]


def run_job(job, name):
    return _call(lambda: None, name, (), [], [], [], [], job=job)[1]


@jax.custom_vjp
def bmm(x, w):
    return _nn(x.astype(BF16), w.astype(BF16))


def _bmm_fwd(x, w):
    return bmm(x, w), (x, w)


def _bmm_bwd(res, g):
    x, w = res
    gb = g.astype(BF16)
    return _nt(gb, w.astype(BF16)), _tn(x.astype(BF16), gb)


bmm.defvjp(_bmm_fwd, _bmm_bwd)


def _shift_raw(x, d):
    n = x.shape[0]
    r = pltpu.roll(x, d % n, axis=0)
    row = lax.broadcasted_iota(jnp.int32, x.shape, 0)
    keep = (row >= d) if d > 0 else (row < n + d)
    return jnp.where(keep, r, 0.0)


def shift_rows(x, d):
    @jax.custom_vjp
    def f(v):
        return _shift_raw(v, d)

    f.defvjp(lambda v: (_shift_raw(v, d), None), lambda _, g: (_shift_raw(g, -d),))
    return f(x)


def _swap_raw(x):
    n = x.shape[1]
    nxt = pltpu.roll(x, n - 1, axis=1)
    prv = pltpu.roll(x, 1, axis=1)
    lane = lax.broadcasted_iota(jnp.int32, x.shape, 1)
    return jnp.where(lane % 2 == 0, nxt, prv)


@jax.custom_vjp
def swap_pairs(x):
    return _swap_raw(x)


swap_pairs.defvjp(lambda x: (_swap_raw(x), None), lambda _, g: (_swap_raw(g),))


@jax.custom_vjp
def group_mean(x, bd):
    return _split_mm(x, bd)


group_mean.defvjp(lambda x, bd: (_split_mm(x, bd), bd), lambda bd, g: (_split_mm(g, bd), jnp.zeros_like(bd)))


def _rope_norm(x, gain, cos, sgn, bd, scale):
    xn = x * lax.rsqrt(group_mean(x * x, bd) + EPS) * gain
    return (xn * cos + swap_pairs(xn) * sgn) * scale


def _conv_gate(gb, gc, hc, w):
    z = gc * hc
    c = shift_rows(z, 1) * w[0:1] + z * w[1:2] + shift_rows(z, -1) * w[2:3]
    return gb * c


def _window_sum(p, r):
    b = f = p
    k = 1
    while k < r:
        b = b + shift_rows(b, k)
        f = f + shift_rows(f, -k)
        k *= 2
    return b + f - p + shift_rows(p, r) + shift_rows(p, -r)


def _pool_mix(p, pool_w, scale):
    n = p.shape[0]
    t = lax.broadcasted_iota(jnp.int32, (n, 1), 0)
    outs = []
    for gi, r in enumerate(POOL_RADII):
        pg = p[:, gi * GROUP:(gi + 1) * GROUP]
        cnt = (jnp.minimum(t + r, n - 1) - jnp.maximum(t - r, 0) + 1).astype(F32)
        pooled = _window_sum(pg, r) / cnt - pg
        outs.append(bmm(pooled, pool_w[gi]))
    return jnp.concatenate(outs, axis=1) * scale


def _sgu(u, v, norm_g, w_s, b_full):
    ug = jax.nn.gelu(u)
    vg = jax.nn.gelu(v)
    vn = vg * lax.rsqrt(jnp.mean(vg * vg, axis=-1, keepdims=True) + EPS) * norm_g
    cols = []
    for g in range(w_s.shape[0]):
        rows = []
        for n in range(u.shape[0] // SGU_CHUNK):
            blk = vn[n * SGU_CHUNK:(n + 1) * SGU_CHUNK, g * GROUP:(g + 1) * GROUP]
            rows.append(bmm(w_s[g], blk) + b_full[g])
        cols.append(jnp.concatenate(rows, axis=0))
    return ug * jnp.concatenate(cols, axis=1)


def rms_fwd(x, gain):
    T, D = x.shape
    tm = _tile(T, 512)

    def body(x_ref, g_ref, o_ref):
        xv = x_ref[...]
        r = lax.rsqrt(jnp.mean(xv * xv, axis=-1, keepdims=True) + EPS)
        o_ref[...] = (xv * r * g_ref[...]).astype(BF16)

    return pl.pallas_call(
        body, name="rms_fwd", grid=(T // tm,),
        in_specs=[pl.BlockSpec((tm, D), lambda i: (i, 0)), pl.BlockSpec((1, D), lambda i: (0, 0))],
        out_specs=pl.BlockSpec((tm, D), lambda i: (i, 0)),
        out_shape=jax.ShapeDtypeStruct((T, D), BF16), compiler_params=_cp("parallel"),
    )(x, gain)


def _rms_bwd_math(xv, gain, dy, dres):
    r = lax.rsqrt(jnp.mean(xv * xv, axis=-1, keepdims=True) + EPS)
    xh = xv * r
    dxh = dy * gain
    dx = dres + r * (dxh - xh * jnp.mean(dxh * xh, axis=-1, keepdims=True))
    return dx, jnp.sum(dy * xh, axis=0, keepdims=True)


def rms_bwd(x, gain, dy, dres):
    T, D = x.shape
    tm = _tile(T, 512)

    def body(x_ref, g_ref, dy_ref, dr_ref, dx_ref, dg_ref):
        dx, dg = _rms_bwd_math(x_ref[...], g_ref[...], dy_ref[...], dr_ref[...])
        dx_ref[...] = dx

        @pl.when(pl.program_id(0) == 0)
        def _():
            dg_ref[...] = jnp.zeros_like(dg_ref)

        dg_ref[...] += dg

    row = pl.BlockSpec((tm, D), lambda i: (i, 0))
    return pl.pallas_call(
        body, name="rms_bwd", grid=(T // tm,),
        in_specs=[row, pl.BlockSpec((1, D), lambda i: (0, 0)), row, row],
        out_specs=[row, pl.BlockSpec((1, D), lambda i: (0, 0))],
        out_shape=[jax.ShapeDtypeStruct((T, D), F32), jax.ShapeDtypeStruct((1, D), F32)],
        compiler_params=_cp("arbitrary"),
    )(x, gain, dy, dres)


FFN_TN = 256


def ffn_fwd(x, gain, wt_in, w_out, job=None):
    T, D = x.shape
    F = w_out.shape[0]
    tm, tn = _tile(T, 1024), FFN_TN
    nc = F // tn

    def body(x_ref, gn_ref, wg_ref, wu_ref, wo_ref, y_ref, gu_ref, xn_s, acc_s):
        c = pl.program_id(1)

        @pl.when(c == 0)
        def _():
            xv = x_ref[...]
            r = lax.rsqrt(jnp.mean(xv * xv, axis=-1, keepdims=True) + EPS)
            xn_s[...] = (xv * r * gn_ref[...]).astype(BF16)
            acc_s[...] = jnp.zeros_like(acc_s)

        xn = xn_s[...]
        g = _nt(xn, wg_ref[...])
        u = _nt(xn, wu_ref[...])
        gu_ref[:, :tn] = g.astype(BF16)
        gu_ref[:, tn:] = u.astype(BF16)
        a = (g * jax.nn.sigmoid(g) * u).astype(BF16)
        acc_s[...] += _nn(a, wo_ref[...])

        @pl.when(c == nc - 1)
        def _():
            y_ref[...] = x_ref[...] + 0.5 * acc_s[...]

    row = pl.BlockSpec((tm, D), lambda i, c: (i, 0))
    return _call(
        body, "ffn_fwd", (T // tm, nc),
        [row, pl.BlockSpec((1, D), lambda i, c: (0, 0)),
         pl.BlockSpec((tn, D), lambda i, c: (c, 0)),
         pl.BlockSpec((tn, D), lambda i, c: (c + nc, 0)),
         pl.BlockSpec((tn, D), lambda i, c: (c, 0))],
        [x, gain, wt_in, wt_in, w_out],
        [row, pl.BlockSpec((tm, 2 * tn), lambda i, c: (i, c))],
        [jax.ShapeDtypeStruct((T, D), F32), jax.ShapeDtypeStruct((T, 2 * F), BF16)],
        [pltpu.VMEM((tm, D), BF16), pltpu.VMEM((tm, D), F32)], ("parallel", "arbitrary"), job)


def ffn_bwd_x(dout, x, gain, gu, wt_in, w_out, job=None):
    T, D = x.shape
    F = w_out.shape[0]
    tm, tn = _tile(T, 1024), FFN_TN
    nc = F // tn

    def body(do_ref, x_ref, gn_ref, gu_ref, wg_ref, wu_ref, wo_ref,
             dx_ref, dgn_ref, a_ref, dgu_ref, xn_ref, dob_ref, acc_s):
        i, c = pl.program_id(0), pl.program_id(1)

        @pl.when(c == 0)
        def _():
            xv = x_ref[...]
            r = lax.rsqrt(jnp.mean(xv * xv, axis=-1, keepdims=True) + EPS)
            xn_ref[...] = (xv * r * gn_ref[...]).astype(BF16)
            dob_ref[...] = (0.5 * do_ref[...]).astype(BF16)
            acc_s[...] = jnp.zeros_like(acc_s)

        da = jnp.concatenate([_nt(dob_ref[:tm // 2, :], wo_ref[...]), _nt(dob_ref[tm // 2:, :], wo_ref[...])], axis=0)
        g = gu_ref[:, :tn].astype(F32)
        u = gu_ref[:, tn:].astype(F32)
        sig = jax.nn.sigmoid(g)
        sl = g * sig
        a_ref[...] = (sl * u).astype(BF16)
        dg = (da * u * (sig * (1.0 + g * (1.0 - sig)))).astype(BF16)
        du = (da * sl).astype(BF16)
        dgu_ref[:, :tn] = dg
        dgu_ref[:, tn:] = du
        acc_s[...] += _nn(dg, wg_ref[...]) + _nn(du, wu_ref[...])

        @pl.when(c == nc - 1)
        def _():
            dx, dgn = _rms_bwd_math(x_ref[...], gn_ref[...], acc_s[...], do_ref[...])
            dx_ref[...] = dx

            @pl.when(i == 0)
            def _():
                dgn_ref[...] = jnp.zeros_like(dgn_ref)

            dgn_ref[...] += dgn

    row = pl.BlockSpec((tm, D), lambda i, c: (i, 0))
    return _call(
        body, "ffn_bwd_x", (T // tm, nc),
        [row, row, pl.BlockSpec((1, D), lambda i, c: (0, 0)),
         pl.BlockSpec((tm, 2 * tn), lambda i, c: (i, c)),
         pl.BlockSpec((tn, D), lambda i, c: (c, 0)),
         pl.BlockSpec((tn, D), lambda i, c: (c + nc, 0)),
         pl.BlockSpec((tn, D), lambda i, c: (c, 0))],
        [dout, x, gain, gu, wt_in, wt_in, w_out],
        [row, pl.BlockSpec((1, D), lambda i, c: (0, 0)),
         pl.BlockSpec((tm, tn), lambda i, c: (i, c)),
         pl.BlockSpec((tm, 2 * tn), lambda i, c: (i, c)), row, row],
        [jax.ShapeDtypeStruct((T, D), F32), jax.ShapeDtypeStruct((1, D), F32),
         jax.ShapeDtypeStruct((T, F), BF16), jax.ShapeDtypeStruct((T, 2 * F), BF16),
         jax.ShapeDtypeStruct((T, D), BF16), jax.ShapeDtypeStruct((T, D), BF16)],
        [pltpu.VMEM((tm, D), F32)], ("arbitrary", "arbitrary"), job)


MM_TM = 512
MM_TC = 256


def mm_nt(a, wt, pieces, out_dtype, emit_a_bf16=False):
    T, K = a.shape
    tm = _tile(T, MM_TM)
    npc = len(pieces)

    def body(*refs):
        a_ref, w_refs, o_refs = refs[0], refs[1:1 + npc], refs[1 + npc:]
        ab = a_ref[...].astype(BF16)
        for w_ref, o_ref in zip(w_refs, o_refs[:npc]):
            o_ref[...] = _nt(ab, w_ref[...]).astype(o_ref.dtype)
        if emit_a_bf16:
            o_refs[npc][...] = ab

    in_specs = [pl.BlockSpec((tm, K), lambda i: (i, 0))]
    out_specs, out_shape = [], []
    for r0, n in pieces:
        assert r0 % n == 0
        in_specs.append(pl.BlockSpec((n, K), functools.partial(lambda i, b: (b, 0), b=r0 // n)))
        out_specs.append(pl.BlockSpec((tm, n), lambda i: (i, 0)))
        out_shape.append(jax.ShapeDtypeStruct((T, n), out_dtype))
    if emit_a_bf16:
        out_specs.append(pl.BlockSpec((tm, K), lambda i: (i, 0)))
        out_shape.append(jax.ShapeDtypeStruct((T, K), BF16))
    return pl.pallas_call(
        body, name="mm_nt", grid=(T // tm,), in_specs=in_specs, out_specs=out_specs, out_shape=out_shape,
        compiler_params=_cp("parallel"),
    )(a, *([wt] * npc))


def mm_nn(a_list, w, residual=None):
    T = a_list[0].shape[0]
    N = w.shape[1]
    tm = _tile(T, MM_TM)
    na = len(a_list)

    def body(*refs):
        a_refs, w_refs = refs[:na], refs[na:2 * na]
        o_ref = refs[-1]
        acc = refs[2 * na][...] if residual is not None else None
        for a_ref, w_ref in zip(a_refs, w_refs):
            t = _nn(a_ref[...].astype(BF16), w_ref[...])
            acc = t if acc is None else acc + t
        o_ref[...] = acc

    in_specs, r0 = [], 0
    w_specs = []
    for a in a_list:
        k = a.shape[1]
        assert r0 % k == 0
        in_specs.append(pl.BlockSpec((tm, k), lambda i: (i, 0)))
        w_specs.append(pl.BlockSpec((k, N), functools.partial(lambda i, b: (b, 0), b=r0 // k)))
        r0 += k
    assert r0 == w.shape[0]
    args = list(a_list) + [w] * na
    in_specs = in_specs + w_specs
    if residual is not None:
        in_specs.append(pl.BlockSpec((tm, N), lambda i: (i, 0)))
        args.append(residual)
    return pl.pallas_call(
        body, name="mm_nn", grid=(T // tm,), in_specs=in_specs,
        out_specs=pl.BlockSpec((tm, N), lambda i: (i, 0)),
        out_shape=jax.ShapeDtypeStruct((T, N), F32), compiler_params=_cp("parallel"),
    )(*args)


def mm_tn(a, b, n_rows, row_block, prev=None, grid=None, col_block=None, job=None):
    T, M = a.shape
    N = b.shape[1]
    tc = MM_TC
    assert M % tc == 0 and n_rows % tc == 0
    if grid is None:
        grid, col_block = (M // tc,), (lambda j: j)

    def body(*refs):
        a_ref, b_ref, o_ref = refs[0], refs[1], refs[-1]
        o_ref[...] = _tn(a_ref[...], b_ref[...]).astype(BF16)

    in_specs = [pl.BlockSpec((T, tc), lambda *g: (0, col_block(*g))), pl.BlockSpec((T, N), lambda *g: (0, 0))]
    args = [a, b]
    aliases = {}
    if prev is not None:
        in_specs.append(pl.BlockSpec(memory_space=pl.ANY))
        args.append(prev)
        aliases = {2: 0}
    res, jres = _call(body, "mm_tn", grid, in_specs, args, [pl.BlockSpec((tc, N), lambda *g: (row_block(*g), 0))],
                      [jax.ShapeDtypeStruct((n_rows, N), BF16)], (), ["parallel"] * len(grid), job, aliases)
    return res[0] if job is None else (res[0], jres)


def conv_fwd(proj_a, conv_w, n_ex):
    T, C3 = proj_a.shape
    C = C3 // 3
    S = T // n_ex

    def body(gb_ref, gc_ref, hc_ref, w_ref, o_ref):
        o_ref[...] = _conv_gate(gb_ref[...], gc_ref[...], hc_ref[...], w_ref[...]).astype(BF16)

    col = lambda k: pl.BlockSpec((S, C), functools.partial(lambda b, kk: (b, kk), kk=k))
    return pl.pallas_call(
        body, name="conv_fwd", grid=(n_ex,),
        in_specs=[col(0), col(1), col(2), pl.BlockSpec((3, C), lambda b: (0, 0))],
        out_specs=pl.BlockSpec((S, C), lambda b: (b, 0)),
        out_shape=jax.ShapeDtypeStruct((T, C), BF16), compiler_params=_cp("parallel"),
    )(proj_a, proj_a, proj_a, conv_w)


def conv_bwd(proj_a, conv_w, n_ex, dy):
    T, C3 = proj_a.shape
    C = C3 // 3
    S = T // n_ex

    def body(gb_ref, gc_ref, hc_ref, w_ref, dy_ref, dp_ref, dw_ref):
        _, vjp = jax.vjp(_conv_gate, gb_ref[...], gc_ref[...], hc_ref[...], w_ref[...])
        dgb, dgc, dhc, dw = vjp(dy_ref[...].astype(F32))
        dp_ref[:, 0:C] = dgb.astype(BF16)
        dp_ref[:, C:2 * C] = dgc.astype(BF16)
        dp_ref[:, 2 * C:] = dhc.astype(BF16)

        @pl.when(pl.program_id(0) == 0)
        def _():
            dw_ref[...] = jnp.zeros_like(dw_ref)

        dw_ref[...] += dw

    col = lambda k: pl.BlockSpec((S, C), functools.partial(lambda b, kk: (b, kk), kk=k))
    return pl.pallas_call(
        body, name="conv_bwd", grid=(n_ex,),
        in_specs=[col(0), col(1), col(2), pl.BlockSpec((3, C), lambda b: (0, 0)),
                  pl.BlockSpec((S, C), lambda b: (b, 0))],
        out_specs=[pl.BlockSpec((S, C3), lambda b: (b, 0)), pl.BlockSpec((3, C), lambda b: (0, 0))],
        out_shape=[jax.ShapeDtypeStruct((T, C3), BF16), jax.ShapeDtypeStruct((3, C), F32)],
        compiler_params=_cp("arbitrary"),
    )(proj_a, proj_a, proj_a, conv_w, dy)


QW = N_Q_HEADS * HEAD_DIM
KW = N_KV_HEADS * HEAD_DIM
QP = N_Q_HEADS * LANES
KP = N_KV_HEADS * LANES


def _attn_consts(seq):
    rows = seq // GRID_W
    r_idx, c_idx = jnp.meshgrid(jnp.arange(rows), jnp.arange(GRID_W), indexing='ij')
    r_idx = r_idx.reshape(-1).astype(F32)
    c_idx = c_idx.reshape(-1).astype(F32)
    n_freq = HEAD_DIM // 4
    inv = ROPE_THETA ** (-jnp.arange(n_freq, dtype=F32) / n_freq)
    ang = jnp.concatenate([r_idx[:, None] * inv, c_idx[:, None] * inv], axis=-1)
    cos = jnp.repeat(jnp.cos(ang), 2, axis=1)
    sin = jnp.repeat(jnp.sin(ang), 2, axis=1)
    sgn = sin * jnp.tile(jnp.array([-1.0, 1.0], F32), HEAD_DIM // 2)
    cos = jnp.tile(cos, (1, N_Q_HEADS))
    sgn = jnp.tile(sgn, (1, N_Q_HEADS))
    lane = jnp.arange(QW)
    bd = jnp.where(lane[:, None] // HEAD_DIM == lane[None, :] // HEAD_DIM, 1.0 / HEAD_DIM, 0.0).astype(BF16)
    dst = (lane // HEAD_DIM) * LANES + lane % HEAD_DIM
    spread = (dst[:, None] == jnp.arange(QP)[None, :]).astype(BF16)
    return dict(cos=cos, sgn=sgn, bd=bd, spread=spread, gather=spread.T)


def qkv_prep_fwd(proj_b, qg, kg, cst, n_ex):
    T = proj_b.shape[0]
    S = T // n_ex
    tm = _tile(S, 512)
    nb = S // tm

    def body(p_ref, qg_ref, kg_ref, cos_ref, sgn_ref, bd_ref, sp_ref, q_ref, k_ref, v_ref):
        pv = p_ref[...]
        cos, sgn, bd, sp = cos_ref[...], sgn_ref[...], bd_ref[...], sp_ref[...]
        qr = _rope_norm(pv[:, :QW], qg_ref[...], cos, sgn, bd, HEAD_DIM ** -0.5)
        kr = _rope_norm(pv[:, QW:QW + KW], kg_ref[...], cos[:, :KW], sgn[:, :KW], bd[:KW, :KW], 1.0)
        q_ref[...] = _nn(qr.astype(BF16), sp).astype(BF16)
        k_ref[...] = _nn(kr.astype(BF16), sp[:KW, :KP]).astype(BF16)
        v_ref[...] = _nn(pv[:, QW + KW:].astype(BF16), sp[:KW, :KP]).astype(BF16)

    full = lambda a: pl.BlockSpec(a.shape, lambda i: (0,) * a.ndim)
    tab = pl.BlockSpec((tm, QW), lambda i: (i % nb, 0))
    return pl.pallas_call(
        body, name="qkv_prep_fwd", grid=(T // tm,),
        in_specs=[pl.BlockSpec((tm, QW + 2 * KW), lambda i: (i, 0)), full(qg), full(kg), tab, tab,
                  full(cst["bd"]), full(cst["spread"])],
        out_specs=[pl.BlockSpec((tm, QP), lambda i: (i, 0)), pl.BlockSpec((tm, KP), lambda i: (i, 0)),
                   pl.BlockSpec((tm, KP), lambda i: (i, 0))],
        out_shape=[jax.ShapeDtypeStruct((T, QP), BF16), jax.ShapeDtypeStruct((T, KP), BF16),
                   jax.ShapeDtypeStruct((T, KP), BF16)],
        compiler_params=_cp("parallel"),
    )(proj_b, qg, kg, cst["cos"], cst["sgn"], cst["bd"], cst["spread"])


def qkv_prep_bwd(proj_b, qg, kg, cst, n_ex, dq, dk_pad, dv_pad):
    T = proj_b.shape[0]
    S = T // n_ex
    tm = _tile(S, 512)
    nb = S // tm

    def body(p_ref, qg_ref, kg_ref, cos_ref, sgn_ref, bd_ref, ga_ref, dq_ref, dk_ref, dv_ref,
             dp_ref, dqg_ref, dkg_ref):
        pv = p_ref[...]
        cos, sgn, bd, ga = cos_ref[...], sgn_ref[...], bd_ref[...], ga_ref[...]
        fq = lambda q, g: _rope_norm(q, g, cos, sgn, bd, HEAD_DIM ** -0.5)
        fk = lambda k, g: _rope_norm(k, g, cos[:, :KW], sgn[:, :KW], bd[:KW, :KW], 1.0)
        _, vq = jax.vjp(fq, pv[:, :QW], qg_ref[...])
        _, vk = jax.vjp(fk, pv[:, QW:QW + KW], kg_ref[...])
        dqp, dqg = vq(dq_ref[...])
        dkp, dkg = vk(_split_mm(dk_ref[...], ga[:KP, :KW]))
        dp_ref[:, :QW] = dqp.astype(BF16)
        dp_ref[:, QW:QW + KW] = dkp.astype(BF16)
        dp_ref[:, QW + KW:] = _split_mm(dv_ref[...], ga[:KP, :KW]).astype(BF16)

        @pl.when(pl.program_id(0) == 0)
        def _():
            dqg_ref[...] = jnp.zeros_like(dqg_ref)
            dkg_ref[...] = jnp.zeros_like(dkg_ref)

        dqg_ref[...] += dqg
        dkg_ref[...] += dkg

    full = lambda a: pl.BlockSpec(a.shape, lambda i: (0,) * a.ndim)
    tab = pl.BlockSpec((tm, QW), lambda i: (i % nb, 0))
    row = lambda n: pl.BlockSpec((tm, n), lambda i: (i, 0))
    return pl.pallas_call(
        body, name="qkv_prep_bwd", grid=(T // tm,),
        in_specs=[row(QW + 2 * KW), full(qg), full(kg), tab, tab, full(cst["bd"]), full(cst["gather"]),
                  row(QW), row(KP), row(KP)],
        out_specs=[row(QW + 2 * KW), pl.BlockSpec((1, QW), lambda i: (0, 0)), pl.BlockSpec((1, KW), lambda i: (0, 0))],
        out_shape=[jax.ShapeDtypeStruct((T, QW + 2 * KW), BF16), jax.ShapeDtypeStruct((1, QW), F32),
                   jax.ShapeDtypeStruct((1, KW), F32)],
        compiler_params=_cp("arbitrary"),
    )(proj_b, qg, kg, cst["cos"], cst["sgn"], cst["bd"], cst["gather"], dq, dk_pad, dv_pad)


ATT_TQ = 256


def attn_fwd(qp, kp, vp, gather, n_ex, job=None):
    T = qp.shape[0]
    S = T // n_ex
    tq = _tile(S, ATT_TQ)
    nq = S // tq

    def body(q_ref, k_ref, v_ref, ga_ref, o_ref, op_ref, lse_ref):
        lane = lax.broadcasted_iota(jnp.int32, (tq, LANES), 1)
        lse_all = jnp.zeros((tq, LANES), F32)
        for h in range(N_Q_HEADS):
            kv = h // Q_PER_KV
            qh = q_ref[:, h * LANES:(h + 1) * LANES]
            s = _nt(qh, k_ref[:, kv * LANES:(kv + 1) * LANES])
            m = jnp.max(s, axis=-1, keepdims=True)
            p = jnp.exp(s - m)
            lsum = jnp.sum(p, axis=-1, keepdims=True)
            o = _nn(p.astype(BF16), v_ref[:, kv * LANES:(kv + 1) * LANES]) / lsum
            op_ref[:, h * LANES:(h + 1) * LANES] = o.astype(BF16)
            lse_all = jnp.where(lane == h, m + jnp.log(lsum), lse_all)
        lse_ref[...] = lse_all
        o_ref[...] = _nn(op_ref[...], ga_ref[...]).astype(BF16)

    blk = lambda n: pl.BlockSpec((tq, n), lambda b, i: (b * nq + i, 0))
    kvs = pl.BlockSpec((S, KP), lambda b, i: (b, 0))
    return _call(
        body, "attn_fwd", (n_ex, nq),
        [blk(QP), kvs, kvs, pl.BlockSpec(gather.shape, lambda b, i: (0, 0))], [qp, kp, vp, gather],
        [blk(QW), blk(QP), blk(LANES)],
        [jax.ShapeDtypeStruct((T, QW), BF16), jax.ShapeDtypeStruct((T, QP), BF16),
         jax.ShapeDtypeStruct((T, LANES), F32)], (), ("parallel", "parallel"), job)


def attn_bwd(qp, kp, vp, op, lse, do, cst, n_ex, job=None):
    T = qp.shape[0]
    S = T // n_ex
    tq = _tile(S, ATT_TQ)
    nq = S // tq

    def body(q_ref, k_ref, v_ref, op_ref, lse_ref, do_ref, sp_ref, ga_ref, dq_ref, dk_ref, dv_ref, dqp_s):
        @pl.when(pl.program_id(1) == 0)
        def _():
            dk_ref[...] = jnp.zeros_like(dk_ref)
            dv_ref[...] = jnp.zeros_like(dv_ref)

        lane = lax.broadcasted_iota(jnp.int32, (tq, LANES), 1)
        dop = _nn(do_ref[...], sp_ref[...]).astype(BF16)
        lse_all = lse_ref[...]
        for h in range(N_Q_HEADS):
            kv = h // Q_PER_KV
            hs = slice(h * LANES, (h + 1) * LANES)
            ks = slice(kv * LANES, (kv + 1) * LANES)
            qh, kk, vv = q_ref[:, hs], k_ref[:, ks], v_ref[:, ks]
            doh = dop[:, hs]
            lse_h = jnp.sum(jnp.where(lane == h, lse_all, 0.0), axis=-1, keepdims=True)
            p = jnp.exp(_nt(qh, kk) - lse_h)
            dp = _nt(doh, vv)
            delta = jnp.sum(doh.astype(F32) * op_ref[:, hs].astype(F32), axis=-1, keepdims=True)
            ds = (p * (dp - delta)).astype(BF16)
            dqp_s[:, hs] = _nn(ds, kk)
            dk_ref[:, ks] += _tn(ds, qh)
            dv_ref[:, ks] += _tn(p.astype(BF16), doh)
        dq_ref[...] = _split_mm(dqp_s[...], ga_ref[...])

    blk = lambda n: pl.BlockSpec((tq, n), lambda b, i: (b * nq + i, 0))
    kvs = pl.BlockSpec((S, KP), lambda b, i: (b, 0))
    full = lambda a: pl.BlockSpec(a.shape, lambda b, i: (0, 0))
    return _call(
        body, "attn_bwd", (n_ex, nq),
        [blk(QP), kvs, kvs, blk(QP), blk(LANES), blk(QW), full(cst["spread"]), full(cst["gather"])],
        [qp, kp, vp, op, lse, do, cst["spread"], cst["gather"]],
        [blk(QW), kvs, kvs],
        [jax.ShapeDtypeStruct((T, QW), F32), jax.ShapeDtypeStruct((T, KP), F32), jax.ShapeDtypeStruct((T, KP), F32)],
        [pltpu.VMEM((tq, QP), F32)], ("arbitrary", "arbitrary"), job)


def pool_fwd(p, pool_w, scale, n_ex):
    T, W = p.shape
    S = T // n_ex

    def body(p_ref, w_ref, s_ref, o_ref):
        o_ref[...] = _pool_mix(p_ref[...], w_ref[...], s_ref[...]).astype(BF16)

    return pl.pallas_call(
        body, name="pool_fwd", grid=(n_ex,),
        in_specs=[pl.BlockSpec((S, W), lambda b: (b, 0)),
                  pl.BlockSpec(pool_w.shape, lambda b: (0, 0, 0)),
                  pl.BlockSpec((1, W), lambda b: (0, 0))],
        out_specs=pl.BlockSpec((S, W), lambda b: (b, 0)),
        out_shape=jax.ShapeDtypeStruct((T, W), BF16), compiler_params=_cp("parallel"),
    )(p, pool_w, scale)


def pool_bwd(p, pool_w, scale, n_ex, dy):
    T, W = p.shape
    S = T // n_ex

    def body(p_ref, w_ref, s_ref, dy_ref, dp_ref, dw_ref, ds_ref):
        _, vjp = jax.vjp(_pool_mix, p_ref[...], w_ref[...], s_ref[...])
        dp, dw, ds = vjp(dy_ref[...].astype(F32))
        dp_ref[...] = dp.astype(BF16)

        @pl.when(pl.program_id(0) == 0)
        def _():
            dw_ref[...] = jnp.zeros_like(dw_ref)
            ds_ref[...] = jnp.zeros_like(ds_ref)

        dw_ref[...] += dw
        ds_ref[...] += ds

    wshape = pool_w.shape
    return pl.pallas_call(
        body, name="pool_bwd", grid=(n_ex,),
        in_specs=[pl.BlockSpec((S, W), lambda b: (b, 0)),
                  pl.BlockSpec(wshape, lambda b: (0, 0, 0)),
                  pl.BlockSpec((1, W), lambda b: (0, 0)), pl.BlockSpec((S, W), lambda b: (b, 0))],
        out_specs=[pl.BlockSpec((S, W), lambda b: (b, 0)), pl.BlockSpec(wshape, lambda b: (0, 0, 0)),
                   pl.BlockSpec((1, W), lambda b: (0, 0))],
        out_shape=[jax.ShapeDtypeStruct((T, W), BF16), jax.ShapeDtypeStruct(wshape, F32),
                   jax.ShapeDtypeStruct((1, W), F32)],
        compiler_params=_cp("arbitrary"),
    )(p, pool_w, scale, dy)


SGU_TS = 512


def sgu_fwd(u, v, norm_g, w_s, b_full):
    T, W = u.shape
    ts = _tile(T, SGU_TS)

    def body(u_ref, v_ref, g_ref, w_ref, b_ref, o_ref):
        o_ref[...] = _sgu(u_ref[...], v_ref[...], g_ref[...], w_ref[...], b_ref[...]).astype(BF16)

    row = pl.BlockSpec((ts, W), lambda i: (i, 0))
    wsp = pl.BlockSpec(w_s.shape, lambda i: (0, 0, 0))
    return pl.pallas_call(
        body, name="sgu_fwd", grid=(T // ts,),
        in_specs=[row, row, pl.BlockSpec((1, W), lambda i: (0, 0)), wsp, wsp],
        out_specs=row, out_shape=jax.ShapeDtypeStruct((T, W), BF16), compiler_params=_cp("parallel"),
    )(u, v, norm_g, w_s, b_full)


def sgu_bwd(u, v, norm_g, w_s, b_full, dy):
    T, W = u.shape
    ts = _tile(T, SGU_TS)
    wshape = w_s.shape

    def body(u_ref, v_ref, g_ref, w_ref, b_ref, dy_ref, du_ref, dv_ref, dg_ref, dw_ref, db_ref):
        _, vjp = jax.vjp(_sgu, u_ref[...], v_ref[...], g_ref[...], w_ref[...], b_ref[...])
        du, dv, dg, dw, db = vjp(dy_ref[...].astype(F32))
        du_ref[...] = du.astype(BF16)
        dv_ref[...] = dv.astype(BF16)

        @pl.when(pl.program_id(0) == 0)
        def _():
            dg_ref[...] = jnp.zeros_like(dg_ref)
            dw_ref[...] = jnp.zeros_like(dw_ref)
            db_ref[...] = jnp.zeros_like(db_ref)

        dg_ref[...] += dg
        dw_ref[...] += dw
        db_ref[...] += db

    row = pl.BlockSpec((ts, W), lambda i: (i, 0))
    wsp = pl.BlockSpec(wshape, lambda i: (0, 0, 0))
    wout = pl.BlockSpec(wshape, lambda i: (0, 0, 0))
    vec = pl.BlockSpec((1, W), lambda i: (0, 0))
    return pl.pallas_call(
        body, name="sgu_bwd", grid=(T // ts,),
        in_specs=[row, row, vec, wsp, wsp, row],
        out_specs=[row, row, vec, wout, wout],
        out_shape=[jax.ShapeDtypeStruct((T, W), BF16), jax.ShapeDtypeStruct((T, W), BF16),
                   jax.ShapeDtypeStruct((1, W), F32), jax.ShapeDtypeStruct(wshape, F32),
                   jax.ShapeDtypeStruct(wshape, F32)],
        compiler_params=_cp("arbitrary"),
    )(u, v, norm_g, w_s, b_full, dy)


def loss_head(x, gain, target):
    T, D = x.shape
    tm = _tile(T, 512)

    def body(x_ref, g_ref, t_ref, loss_ref, dx_ref, dg_ref):
        xv, g = x_ref[...], g_ref[...]
        r = lax.rsqrt(jnp.mean(xv * xv, axis=-1, keepdims=True) + EPS)
        err = xv * r * g - t_ref[...]
        part = 0.5 * jnp.sum(jnp.mean(err * err, axis=-1, keepdims=True), axis=0, keepdims=True)
        dx, dg = _rms_bwd_math(xv, g, err * (1.0 / D), jnp.zeros_like(xv))
        dx_ref[...] = dx

        @pl.when(pl.program_id(0) == 0)
        def _():
            loss_ref[...] = jnp.zeros_like(loss_ref)
            dg_ref[...] = jnp.zeros_like(dg_ref)

        loss_ref[...] += part
        dg_ref[...] += dg

    row = pl.BlockSpec((tm, D), lambda i: (i, 0))
    vec = pl.BlockSpec((1, D), lambda i: (0, 0))
    return pl.pallas_call(
        body, name="loss_head", grid=(T // tm,),
        in_specs=[row, vec, row], out_specs=[pl.BlockSpec((1, 1), lambda i: (0, 0)), row, vec],
        out_shape=[jax.ShapeDtypeStruct((1, 1), F32), jax.ShapeDtypeStruct((T, D), F32),
                   jax.ShapeDtypeStruct((1, D), F32)],
        compiler_params=_cp("arbitrary"),
    )(x, gain, target)


class MultiJob:
    def __init__(self, jobs):
        self.jobs = jobs
        self.args = [a for j in jobs for a in j.args]
        self.out_shape = [s for j in jobs for s in j.out_shape]
        self.scratch = [s for j in jobs for s in j.scratch]
        self.n_in, self.n_out = len(self.args), len(self.out_shape)

    def _each(self, ins, outs, sems):
        i = o = s = 0
        for j in self.jobs:
            yield j, ins[i:i + j.n_in], outs[o:o + j.n_out], sems[s:s + len(j.scratch)]
            i, o, s = i + j.n_in, o + j.n_out, s + len(j.scratch)

    def start(self, ins, outs, sems):
        for j, a, b, c in self._each(ins, outs, sems):
            j.start(a, b, c)

    def finish(self, ins, outs, sems):
        for j, a, b, c in self._each(ins, outs, sems):
            j.finish(a, b, c)

    def split(self, results):
        o = 0
        for j in self.jobs:
            yield results[o:o + j.n_out]
            o += j.n_out


class Plan:
    def __init__(self, shard, gathers, scatters, small_carrier=None, pack_small=None):
        self.shard, self.gathers, self.scatters = shard, gathers, scatters
        self.small_carrier, self.pack_small = small_carrier, pack_small
        self.weights, self.grads, self.parts = {}, {}, {}
        self.small_src = self.small_parts = None

    def weight(self, kind, l):
        return self.weights[(kind, l)]

    def grad(self, kind, l, g):
        self.grads[(kind, l)] = g

    def small_ready(self, small, d_final):
        if self.pack_small is not None:
            self.small_src = self.pack_small(small, d_final)

    def _jobs(self, key):
        jobs = []
        if key in self.gathers:
            ks = self.gathers[key]
            jobs.append((GatherJob([self.shard(*k) for k in ks]), self.weights, ks))
        if key in self.scatters:
            ks = self.scatters[key]
            jobs.append((ScatterJob([self.grads[k] for k in ks]), self.parts, ks))
        if key == self.small_carrier and self.small_src is not None:
            jobs.append((GatherJob([(self.small_src, None)]), None, None))
        return jobs

    def _deliver(self, jobs, results):
        multi = MultiJob([j for j, _, _ in jobs])
        for (_, store, ks), res in zip(jobs, multi.split(results)):
            if store is None:
                self.small_parts = res[0]
            else:
                store.update(zip(ks, res))

    def run(self, key, fn, *args, **kw):
        jobs = self._jobs(key)
        if not jobs:
            out = fn(*args, **kw)
            return out if fn is mm_tn else out[0]
        res, jres = fn(*args, job=MultiJob([j for j, _, _ in jobs]), **kw)
        self._deliver(jobs, jres)
        return res

    def alone(self, key, name):
        jobs = self._jobs(key)
        if jobs:
            self._deliver(jobs, run_job(MultiJob([j for j, _, _ in jobs]), name))


def _local_step(x, target, layers, final_norm, n_ex, plan):
    T, D = x.shape
    L = len(layers)
    cst = _attn_consts(T // n_ex)
    ident = lambda j: j
    EV_A, EV_B = 3 * (D // 2), QW + 2 * KW
    OD_W = D // 2
    wt = plan.weight

    saved = []
    for l, W in enumerate(layers):
        s = dict(x0=x)
        x1, s["gu1"] = plan.run(("ffn1_fwd", l), ffn_fwd, x, W["n1"], wt("f1_in_t", l), wt("f1_out", l))
        h = rms_fwd(x1, W["nm"])
        if l % 2 == 0:
            pa, pb = mm_nt(h, wt("mx_in_t", l), [(0, EV_A), (EV_A, EV_B)], F32)
            qg = jnp.tile(W["q_norm"], N_Q_HEADS)[None]
            kg = jnp.tile(W["k_norm"], N_KV_HEADS)[None]
            mix_a = conv_fwd(pa, W["conv_w"], n_ex)
            qp, kp, vp = qkv_prep_fwd(pb, qg, kg, cst, n_ex)
            mix_b, op, lse = plan.run(("attn_fwd", l), attn_fwd, qp, kp, vp, cst["gather"], n_ex)
            s.update(pa=pa, pb=pb, qg=qg, kg=kg, qp=qp, kp=kp, vp=vp, op=op, lse=lse)
        else:
            p, u, v = mm_nt(h, wt("mx_in_t", l), [(0, OD_W), (OD_W, OD_W), (2 * OD_W, OD_W)], F32)
            scale = W["pool_scale"][None]
            sn = W["sgu_norm"][None]
            b_full = jnp.broadcast_to(W["sgu_b"][..., None], W["sgu_w"].shape)
            mix_a = pool_fwd(p, W["pool_w"], scale, n_ex)
            mix_b = sgu_fwd(u, v, sn, W["sgu_w"], b_full)
            s.update(p=p, u=u, v=v, scale=scale, sn=sn, b_full=b_full)
        x2 = mm_nn([mix_a, mix_b], wt("mx_out", l), residual=x1)
        x3, s["gu2"] = plan.run(("ffn2_fwd", l), ffn_fwd, x2, W["n2"], wt("f2_in_t", l), wt("f2_out", l))
        s.update(x1=x1, x2=x2, h=h, mix_a=mix_a, mix_b=mix_b)
        saved.append(s)
        x = x3

    loss, dx, d_final = loss_head(x, final_norm, target)

    small = [None] * L

    def ffn_back(which, l, dout, xin, gain, gu, sm, sm_key):
        w_in, w_out = wt(which + "_in_t", l), wt(which + "_out", l)
        F = w_out.shape[0]
        nc = F // FFN_TN
        dxi, sm[sm_key], a, dgu, xn, dob = plan.run((which + "_bwd", l), ffn_bwd_x, dout, xin, gain, gu, w_in, w_out)
        plan.grad(which + "_out", l, plan.run((which + "_out_grad", l), mm_tn, a, dob, F, ident))
        if which == "f1" and l == 0:
            plan.small_ready(small, d_final)
        plan.grad(which + "_in_t", l, plan.run(
            (which + "_in_grad", l), mm_tn, dgu, xn, 2 * F, lambda k, c: k * nc + c, grid=(2, nc),
            col_block=lambda k, c: 2 * c + k))
        return dxi

    for l in reversed(range(L)):
        s, W = saved[l], layers[l]
        sm = small[l] = {}
        dx = ffn_back("f2", l, dx, s["x2"], W["n2"], s["gu2"], sm, "n2")
        dm_a, dm_b, dxb = mm_nt(dx, wt("mx_out", l), [(0, D // 2), (D // 2, D // 2)], BF16, emit_a_bf16=True)
        half_blocks = (D // 2) // MM_TC
        g_out = mm_tn(s["mix_a"], dxb, D, ident)
        plan.grad("mx_out", l, mm_tn(s["mix_b"], dxb, D, lambda jj: jj + half_blocks, prev=g_out))
        if l % 2 == 0:
            d_a, sm["conv_w"] = conv_bwd(s["pa"], W["conv_w"], n_ex, dm_a)
            dq, dkp, dvp = plan.run(("attn_bwd", l), attn_bwd, s["qp"], s["kp"], s["vp"], s["op"], s["lse"], dm_b, cst, n_ex)
            d_b, dqg, dkg = qkv_prep_bwd(s["pb"], s["qg"], s["kg"], cst, n_ex, dq, dkp, dvp)
            dh = mm_nn([d_a, d_b], wt("mx_in_t", l))
            g_in = mm_tn(d_a, s["h"], EV_A + EV_B, ident)
            plan.grad("mx_in_t", l, mm_tn(d_b, s["h"], EV_A + EV_B, lambda jj: jj + EV_A // MM_TC, prev=g_in))
            sm["q_norm"] = dqg.reshape(N_Q_HEADS, HEAD_DIM).sum(0)
            sm["k_norm"] = dkg.reshape(N_KV_HEADS, HEAD_DIM).sum(0)
        else:
            d_p, sm["pool_w"], d_ps = pool_bwd(s["p"], W["pool_w"], s["scale"], n_ex, dm_a)
            d_u, d_v, d_sn, sm["sgu_w"], d_sb = sgu_bwd(s["u"], s["v"], s["sn"], W["sgu_w"], s["b_full"], dm_b)
            dh = mm_nn([d_p, d_u, d_v], wt("mx_in_t", l))
            nb = OD_W // MM_TC
            g_in = mm_tn(d_p, s["h"], 3 * OD_W, ident)
            g_in = mm_tn(d_u, s["h"], 3 * OD_W, lambda jj: jj + nb, prev=g_in)
            plan.grad("mx_in_t", l, mm_tn(d_v, s["h"], 3 * OD_W, lambda jj: jj + 2 * nb, prev=g_in))
            sm["pool_scale"], sm["sgu_norm"], sm["sgu_b"] = d_ps[0], d_sn[0], d_sb.sum(-1)
        dx, sm["nm"] = rms_bwd(s["x1"], W["nm"], dh, dx)
        dx = ffn_back("f1", l, dx, s["x0"], W["n1"], s["gu1"], sm, "n1")
    return loss, dx


def all_gather(srcs):
    return run_job(GatherJob(srcs), "all_gather")


def all_reduce_small(v):
    R, C = v.shape

    def body(v_ref, o_ref, buf, send_sems, recv_sems):
        x, y, c, me = _my_place()
        buf[me] = v_ref[...]
        for k in range(1, N_DEV):
            peer, _ = _peer(x, y, c, k)
            pltpu.make_async_remote_copy(src_ref=v_ref, dst_ref=buf.at[me], send_sem=send_sems.at[k - 1],
                                         recv_sem=recv_sems.at[k - 1], device_id=peer, device_id_type=MESH_ID).start()
        for k in range(1, N_DEV):
            peer, pidx = _peer(x, y, c, k)
            cp = pltpu.make_async_remote_copy(src_ref=v_ref, dst_ref=buf.at[pidx], send_sem=send_sems.at[k - 1],
                                              recv_sem=recv_sems.at[k - 1], device_id=peer, device_id_type=MESH_ID)
            cp.wait_recv()
            cp.wait_send()
        acc = buf[0]
        for s in range(1, N_DEV):
            acc = acc + buf[s]
        o_ref[...] = acc

    vm = pl.BlockSpec(memory_space=pltpu.VMEM)
    return pl.pallas_call(
        body, name="all_reduce_small", in_specs=[vm], out_specs=vm, out_shape=jax.ShapeDtypeStruct((R, C), F32),
        scratch_shapes=[pltpu.VMEM((N_DEV, R, C), F32), pltpu.SemaphoreType.DMA((7,)), pltpu.SemaphoreType.DMA((7,))],
        compiler_params=pltpu.CompilerParams(vmem_limit_bytes=VMEM_LIMIT),
    )(v)


def cast_shards(w):
    L, A, B = w.shape

    def body(w_ref, o_ref):
        o_ref[...] = w_ref[...].astype(BF16)

    return pl.pallas_call(
        body, name="cast_shards", grid=(L,),
        in_specs=[pl.BlockSpec((None, A, B), lambda l: (l, 0, 0))],
        out_specs=pl.BlockSpec((None, A, B), lambda l: (l, 0, 0)),
        out_shape=jax.ShapeDtypeStruct((L, A, B), BF16), compiler_params=_cp("parallel"),
    )(w)


ADAM_TC = 256


def adamw(parts, w, m, v, l, prev=None):
    P, R, C = parts.shape
    tc = _tile(C, ADAM_TC)
    c1, c2 = 1.0 - ADAM_B1 ** ADAM_STEP, 1.0 - ADAM_B2 ** ADAM_STEP

    def body(p_ref, w_ref, m_ref, v_ref, g_ref, d_ref, mo_ref, vo_ref):
        g = p_ref[0].astype(F32)
        for s in range(1, P):
            g = g + p_ref[s].astype(F32)
        m1 = ADAM_B1 * m_ref[...] + (1.0 - ADAM_B1) * g
        v1 = ADAM_B2 * v_ref[...] + (1.0 - ADAM_B2) * (g * g)
        g_ref[...] = g
        mo_ref[...] = m1
        vo_ref[...] = v1
        d_ref[...] = -ADAM_LR * ((m1 / c1) / (jnp.sqrt(v1 / c2) + ADAM_EPS) + ADAM_WD * w_ref[...])

    wspec = pl.BlockSpec((None, R, tc), lambda i: (l, 0, i))
    prev = list(prev) if prev is not None else []
    return pl.pallas_call(
        lambda *refs: body(*refs[:4], *refs[4 + len(prev):]), name="adamw", grid=(C // tc,),
        in_specs=[pl.BlockSpec((P, R, tc), lambda i: (0, 0, i)), wspec, wspec, wspec] + [ANY] * len(prev),
        out_specs=[wspec] * 4, out_shape=[jax.ShapeDtypeStruct(w.shape, F32)] * 4,
        input_output_aliases={4 + i: i for i in range(len(prev))},
        compiler_params=_cp("parallel"),
    )(parts, w, m, v, *prev)


_WEIGHTS = ['ffn1_norm', 'ffn1_w_in', 'ffn1_w_out', 'mix_norm', 'ffn2_norm', 'ffn2_w_in', 'ffn2_w_out', 'ev_w_in',
            'ev_conv_w', 'ev_q_norm', 'ev_k_norm', 'ev_w_out', 'od_w_in', 'od_pool_w', 'od_pool_scale', 'od_sgu_norm',
            'od_sgu_w', 'od_sgu_b', 'od_w_out', 'final_norm']
_BIG = dict(ffn1_w_in=True, ffn1_w_out=False, ffn2_w_in=True, ffn2_w_out=False,
            ev_w_in=True, ev_w_out=False, od_w_in=True, od_w_out=False)
_SMALL_SHARDED = ['ev_conv_w', 'od_pool_scale', 'od_sgu_norm']
_SMALL = [n for n in _WEIGHTS if n not in _BIG]
_PACK_ROWS = 8 * LANES


_KINDS = ("f1_in_t", "f1_out", "mx_in_t", "mx_out", "f2_in_t", "f2_out")
_CARRIER_US = dict(ffn1_fwd=105, ffn2_fwd=105, attn_fwd=115, f2_bwd=135, f1_bwd=135, attn_bwd=205,
                   f2_out_grad=33, f1_out_grad=33, f2_in_grad=61, f1_in_grad=61)
_GATHER_US_PER_ROW, _SCATTER_US_PER_ROW, _SMALL_GATHER_US, _SLACK_US = 0.066, 0.176, 36, 10


def _schedule(L, rows):
    events = []
    for l in range(L):
        events += [("ffn1_fwd", l), ("mixer", l)] + ([("attn_fwd", l)] if l % 2 == 0 else []) + [("ffn2_fwd", l)]
    consumer = {"f1": "ffn1_fwd", "mx": "mixer", "f2": "ffn2_fwd"}
    queue = [(k, l) for l in range(L) for k in _KINDS]
    pos = {t: events.index((consumer[t[0][:2]], t[1])) for t in queue}
    gathers = {"first": [t for t in queue if pos[t] == 0]}
    queue = [t for t in queue if pos[t] > 0]
    carriers = [i for i, e in enumerate(events) if e[0] in _CARRIER_US]
    for i in carriers:
        budget, take = _CARRIER_US[events[i][0]], []
        later = [j for j in carriers if j > i]
        while queue:
            t = queue[0]
            cost = rows(*t) * _GATHER_US_PER_ROW
            forced = not any(j < pos[t] for j in later)
            if not forced and cost > budget:
                break
            take.append(queue.pop(0))
            budget -= cost
        if take:
            gathers[events[i]] = take
    assert not queue
    events = []
    for l in reversed(range(L)):
        events += [("f2_bwd", l), ("f2_out_grad", l), ("f2_in_grad", l), ("mx_out_ready", l)]
        events += [("attn_bwd", l)] if l % 2 == 0 else []
        events += [("mx_in_ready", l), ("f1_bwd", l), ("f1_out_grad", l), ("f1_in_grad", l)]
    made_by = {"f2_out": "f2_out_grad", "f2_in_t": "f2_in_grad", "mx_out": "mx_out_ready", "mx_in_t": "mx_in_ready",
               "f1_out": "f1_out_grad", "f1_in_t": "f1_in_grad"}
    small_carrier = ("f1_in_grad", 0)
    scatters, ready = {}, []
    for e in events:
        if e[0] in _CARRIER_US:
            budget, take = _CARRIER_US[e[0]] - (_SMALL_GATHER_US if e == small_carrier else 0), []
            while True:
                fits = [t for t in ready if rows(*t) * _SCATTER_US_PER_ROW <= budget + _SLACK_US]
                if not fits:
                    break
                t = max(fits, key=lambda u: rows(*u))
                budget -= rows(*t) * _SCATTER_US_PER_ROW
                ready.remove(t)
                take.append(t)
            if take:
                scatters[e] = take
        ready += [(k, e[1]) for k in _KINDS if made_by[k] == e[0]]
    scatters["last"] = ready
    return gathers, scatters, small_carrier


def _pack(arrs):
    flat = jnp.concatenate([a.reshape(-1) for a in arrs])
    pad = (-flat.shape[0]) % _PACK_ROWS
    return jnp.pad(flat, (0, pad)).reshape(-1, LANES)


def _unpack(buf, shapes):
    flat, out, off = buf.reshape(-1), [], 0
    for s in shapes:
        n = math.prod(s)
        out.append(flat[off:off + n].reshape(s))
        off += n
    return out


def _unshard_last(g, lead):
    nd = len(lead)
    return jnp.moveaxis(g, 0, nd).reshape(*lead, -1)


def kernel(x, ffn1_norm, ffn1_w_in, ffn1_w_out, mix_norm, ffn2_norm, ffn2_w_in, ffn2_w_out, ev_w_in, ev_conv_w, ev_q_norm, ev_k_norm, ev_w_out, od_w_in, od_pool_w, od_pool_scale, od_sgu_norm, od_sgu_w, od_sgu_b, od_w_out, final_norm, loss_target, m_ffn1_norm, m_ffn1_w_in, m_ffn1_w_out, m_mix_norm, m_ffn2_norm, m_ffn2_w_in, m_ffn2_w_out, m_ev_w_in, m_ev_conv_w, m_ev_q_norm, m_ev_k_norm, m_ev_w_out, m_od_w_in, m_od_pool_w, m_od_pool_scale, m_od_sgu_norm, m_od_sgu_w, m_od_sgu_b, m_od_w_out, m_final_norm, v_ffn1_norm, v_ffn1_w_in, v_ffn1_w_out, v_mix_norm, v_ffn2_norm, v_ffn2_w_in, v_ffn2_w_out, v_ev_w_in, v_ev_conv_w, v_ev_q_norm, v_ev_k_norm, v_ev_w_out, v_od_w_in, v_od_pool_w, v_od_pool_scale, v_od_sgu_norm, v_od_sgu_w, v_od_sgu_b, v_od_w_out, v_final_norm):
    w = dict(zip(_WEIGHTS, (ffn1_norm, ffn1_w_in, ffn1_w_out, mix_norm, ffn2_norm, ffn2_w_in, ffn2_w_out, ev_w_in, ev_conv_w, ev_q_norm, ev_k_norm, ev_w_out, od_w_in, od_pool_w, od_pool_scale, od_sgu_norm, od_sgu_w, od_sgu_b, od_w_out, final_norm)))
    m = dict(zip(_WEIGHTS, (m_ffn1_norm, m_ffn1_w_in, m_ffn1_w_out, m_mix_norm, m_ffn2_norm, m_ffn2_w_in, m_ffn2_w_out, m_ev_w_in, m_ev_conv_w, m_ev_q_norm, m_ev_k_norm, m_ev_w_out, m_od_w_in, m_od_pool_w, m_od_pool_scale, m_od_sgu_norm, m_od_sgu_w, m_od_sgu_b, m_od_w_out, m_final_norm)))
    v = dict(zip(_WEIGHTS, (v_ffn1_norm, v_ffn1_w_in, v_ffn1_w_out, v_mix_norm, v_ffn2_norm, v_ffn2_w_in, v_ffn2_w_out, v_ev_w_in, v_ev_conv_w, v_ev_q_norm, v_ev_k_norm, v_ev_w_out, v_od_w_in, v_od_pool_w, v_od_pool_scale, v_od_sgu_norm, v_od_sgu_w, v_od_sgu_b, v_od_w_out, v_final_norm)))
    n_ex, seq, D = x.shape
    T = n_ex * seq
    L = ffn1_norm.shape[0]
    me = 4 * lax.axis_index("x") + 2 * lax.axis_index("y") + lax.axis_index("c")

    sh_small = [w[n] for n in _SMALL_SHARDED]
    packed = all_gather([(_pack(sh_small), None)])[0].reshape(N_DEV, -1)
    full_small = {}
    off = 0
    for n, a in zip(_SMALL_SHARDED, sh_small):
        cnt = math.prod(a.shape)
        full_small[n] = _unshard_last(packed[:, off:off + cnt].reshape((N_DEV,) + a.shape), a.shape[:-1])
        off += cnt

    tr = lambda a: jnp.swapaxes(a, 1, 2)
    wmv = {n: tuple(tr(d[n]) if t else d[n] for d in (w, m, v)) for n, t in _BIG.items()}
    shards = {n: cast_shards(wmv[n][0]) for n in _BIG}

    def name_of(kind, l):
        mx = "ev" if l % 2 == 0 else "od"
        return {"f1_in_t": "ffn1_w_in", "f1_out": "ffn1_w_out", "f2_in_t": "ffn2_w_in", "f2_out": "ffn2_w_out",
                "mx_in_t": mx + "_w_in", "mx_out": mx + "_w_out"}[kind], (l // 2 if kind.startswith("mx") else l)

    def shard(kind, l):
        name, idx = name_of(kind, l)
        return shards[name], idx

    g_shapes = {}

    def pack_small(small, d_final):
        ev = [sm for l, sm in enumerate(small) if l % 2 == 0]
        od = [sm for l, sm in enumerate(small) if l % 2 == 1]
        st = lambda sms, k: jnp.stack([sm[k] for sm in sms])
        g_full = dict(ffn1_norm=st(small, "n1")[:, 0], mix_norm=st(small, "nm")[:, 0], ffn2_norm=st(small, "n2")[:, 0],
                      ev_conv_w=st(ev, "conv_w"), ev_q_norm=st(ev, "q_norm"), ev_k_norm=st(ev, "k_norm"),
                      od_pool_w=st(od, "pool_w"), od_pool_scale=st(od, "pool_scale"), od_sgu_norm=st(od, "sgu_norm"),
                      od_sgu_w=st(od, "sgu_w"), od_sgu_b=st(od, "sgu_b"), final_norm=d_final[0])
        g_shapes.update({n: g_full[n].shape for n in _SMALL})
        return _pack([g_full[n] for n in _SMALL])

    gathers, scatters, small_carrier = _schedule(L, lambda kind, l: shards[name_of(kind, l)[0]].shape[1])
    plan = Plan(shard, gathers, scatters, small_carrier, pack_small)
    layers = []
    for l in range(L):
        j = l // 2
        W = dict(n1=ffn1_norm[l][None], nm=mix_norm[l][None], n2=ffn2_norm[l][None])
        if l % 2 == 0:
            W.update(conv_w=full_small["ev_conv_w"][j], q_norm=ev_q_norm[j], k_norm=ev_k_norm[j])
        else:
            W.update(pool_w=od_pool_w[j], pool_scale=full_small["od_pool_scale"][j], sgu_norm=full_small["od_sgu_norm"][j],
                     sgu_w=od_sgu_w[j], sgu_b=od_sgu_b[j])
        layers.append(W)

    plan.alone("first", "gather_first")
    loss, dx = _local_step(x.reshape(T, D), loss_target.reshape(T, D), layers, final_norm[None], n_ex, plan)
    plan.alone("last", "scatter_last")

    out = {n: None for n in _BIG}
    for (kind, l), parts in plan.parts.items():
        name, idx = name_of(kind, l)
        out[name] = adamw(parts, *wmv[name], idx, prev=out[name])
    out = {n: [tr(a) if _BIG[n] else a for a in res] for n, res in out.items()}

    g8 = plan.small_parts.reshape(N_DEV, -1)
    cols, off = [], 0
    for n in _SMALL:
        cnt = math.prod(g_shapes[n])
        g = g8[:, off:off + cnt].reshape((N_DEV,) + g_shapes[n])
        off += cnt
        if n in _SMALL_SHARDED:
            width = w[n].shape[-1]
            g = lax.dynamic_slice_in_dim(g, me * width, width, axis=g.ndim - 1)
        cols.append(g.reshape(N_DEV, -1))
    g8 = jnp.concatenate(cols, axis=1)
    g8 = jnp.pad(g8, ((0, 0), (0, (-g8.shape[1]) % _PACK_ROWS))).reshape(N_DEV, -1, LANES)
    pk = lambda d: _pack([d[n] for n in _SMALL])[None]
    small_out = adamw(g8, pk(w), pk(m), pk(v), 0)
    shapes = [w[n].shape for n in _SMALL]
    for i in range(4):
        for n, a in zip(_SMALL, _unpack(small_out[i], shapes)):
            out.setdefault(n, [None] * 4)[i] = a

    total = lax.psum(loss[0, 0], ("x", "y", "c"))
    return (total, dx.reshape(n_ex, seq, D), *[out[n][0] for n in _WEIGHTS], *[out[n][1] for n in _WEIGHTS],
            *[out[n][2] for n in _WEIGHTS], *[out[n][3] for n in _WEIGHTS])
```

```python
import functools
import math

import jax
import jax.numpy as jnp
from jax import lax
from jax.experimental import pallas as pl
from jax.experimental.pallas import tpu as pltpu

F32, BF16 = jnp.float32, jnp.bfloat16
EPS = 1e-6
N_DEV = 8
V7X_VMEM_BYTES = 64 * 1024 * 1024
VMEM_LIMIT = V7X_VMEM_BYTES - 8 * 1024 * 1024
LANES = 128
HEAD_DIM = 64
N_Q_HEADS = 8
N_KV_HEADS = 2
Q_PER_KV = N_Q_HEADS // N_KV_HEADS
GRID_W = 64
ROPE_THETA = 10000.0
POOL_RADII = (1, 2, 4, 8)
SGU_CHUNK = 128
GROUP = 128
ADAM_LR, ADAM_B1, ADAM_B2, ADAM_EPS, ADAM_WD, ADAM_STEP = 0.001, 0.9, 0.999, 1e-08, 0.01, 10
MESH_ID = pl.DeviceIdType.MESH


def _cp(*sem):
    return pltpu.CompilerParams(dimension_semantics=sem, vmem_limit_bytes=VMEM_LIMIT)


def _dot(a, b, ca, cb):
    return lax.dot_general(a, b, (((ca,), (cb,)), ((), ())), preferred_element_type=F32)


def _nn(a, b):
    return _dot(a, b, 1, 0)


def _nt(a, b):
    return _dot(a, b, 1, 1)


def _tn(a, b):
    return _dot(a, b, 0, 0)


def _split_mm(x, m):
    hi = x.astype(BF16)
    lo = (x - hi.astype(F32)).astype(BF16)
    return _nn(hi, m) + _nn(lo, m)


def _tile(n, pref):
    t = min(n, pref)
    assert n % t == 0, (n, pref)
    return t


ANY = pl.BlockSpec(memory_space=pl.ANY)
OTHER_CHIPS = (2, 4, 6)
JOB_MIDDLE = 0.55


def _my_place():
    x, y, c = lax.axis_index("x"), lax.axis_index("y"), lax.axis_index("c")
    return x, y, c, 4 * x + 2 * y + c


def _peer(x, y, c, k):
    px = 1 - x if k & 4 else x
    py = 1 - y if k & 2 else y
    pc = 1 - c if k & 1 else c
    return (px, py, pc), 4 * px + 2 * py + pc


def _remote(src, dst, send_sems, recv_sems, i, peer):
    return pltpu.make_async_remote_copy(src_ref=src, dst_ref=dst, send_sem=send_sems.at[i], recv_sem=recv_sems.at[i],
                                        device_id=peer, device_id_type=MESH_ID)


class GatherJob:
    def __init__(self, srcs):
        self.srcs = srcs
        self.args = [a for a, _ in srcs]
        self.dims = [a.shape[-2:] for a, _ in srcs]
        n = self.n_in = self.n_out = len(srcs)
        self.out_shape = [jax.ShapeDtypeStruct((N_DEV * r, cc), a.dtype) for (a, _), (r, cc) in zip(srcs, self.dims)]
        self.scratch = [pltpu.SemaphoreType.DMA((N_DEV * n,)), pltpu.SemaphoreType.DMA((N_DEV * n,)),
                        pltpu.SemaphoreType.DMA((n,))]

    def _rows(self, outs, t, idx):
        r = self.dims[t][0]
        return outs[t].at[pl.ds(pl.multiple_of(idx * r, 8), r), :]

    def _local(self, ins, outs, loc, t, me):
        src = ins[t] if self.srcs[t][1] is None else ins[t].at[self.srcs[t][1]]
        return src, pltpu.make_async_copy(src, self._rows(outs, t, me), loc.at[t])

    def start(self, ins, outs, sems):
        send, recv, loc = sems
        x, y, c, me = _my_place()
        for t in range(self.n_in):
            src, local = self._local(ins, outs, loc, t, me)
            local.start()
            for k in (2, 4, 1):
                _remote(src, self._rows(outs, t, me), send, recv, N_DEV * t + k, _peer(x, y, c, k)[0]).start()

    def _copy(self, outs, sems, t, origin, i, to):
        x, y, c, _ = _my_place()
        blk = self._rows(outs, t, _peer(x, y, c, origin)[1])
        return _remote(blk, blk, sems[0], sems[1], N_DEV * t + i, _peer(x, y, c, to)[0])

    def middle(self, ins, outs, sems):
        c = _my_place()[2]

        def relay(t, got, to):
            self._copy(outs, sems, t, got, got, got).wait_recv()
            self._copy(outs, sems, t, got, 6, to).start()
            self._copy(outs, sems, t, to, to, to).wait_recv()

        for t in range(self.n_in):
            pl.when(c == 1)(functools.partial(relay, t, 2, 4))
            pl.when(c == 0)(functools.partial(relay, t, 4, 2))
            for k in (2, 4):
                self._copy(outs, sems, t, k, k + 1, 1).start()

    def finish(self, ins, outs, sems):
        send, recv, loc = sems
        x, y, c, me = _my_place()
        for t in range(self.n_in):
            self._copy(outs, sems, t, 6, 6, 6).wait_recv()
            self._copy(outs, sems, t, 6, 7, 1).start()
        for t in range(self.n_in):
            for k in range(1, N_DEV):
                peer, pidx = _peer(x, y, c, k)
                blk = self._rows(outs, t, pidx)
                if k % 2 == 1:
                    _remote(blk, blk, send, recv, N_DEV * t + k, peer).wait_recv()
                _remote(blk, blk, send, recv, N_DEV * t + k, peer).wait_send()
            self._local(ins, outs, loc, t, me)[1].wait()


class ScatterJob:
    def __init__(self, grads):
        self.args = list(grads)
        self.dims = [(g.shape[0] // N_DEV, g.shape[1]) for g in grads]
        n = self.n_in = self.n_out = len(grads)
        self.out_shape = [jax.ShapeDtypeStruct((N_DEV, r, cc), g.dtype) for g, (r, cc) in zip(grads, self.dims)]
        self.scratch = [pltpu.SemaphoreType.DMA((N_DEV * n,)), pltpu.SemaphoreType.DMA((N_DEV * n,)),
                        pltpu.SemaphoreType.DMA((n,))]

    def _rows(self, ins, t, idx):
        r = self.dims[t][0]
        return ins[t].at[pl.ds(pl.multiple_of(idx * r, 8), r), :]

    def start(self, ins, outs, sems):
        send, recv, loc = sems
        x, y, c, me = _my_place()
        for t in range(self.n_in):
            pltpu.make_async_copy(self._rows(ins, t, me), outs[t].at[me], loc.at[t]).start()
            for k in OTHER_CHIPS + (1, 3, 5, 7):
                peer, pidx = _peer(x, y, c, k)
                _remote(self._rows(ins, t, pidx), outs[t].at[me], send, recv, N_DEV * t + k, peer).start()

    def middle(self, ins, outs, sems):
        pass

    def finish(self, ins, outs, sems):
        send, recv, loc = sems
        x, y, c, me = _my_place()
        for t in range(self.n_in):
            for k in range(1, N_DEV):
                peer, pidx = _peer(x, y, c, k)
                cp = _remote(self._rows(ins, t, pidx), outs[t].at[pidx], send, recv, N_DEV * t + k, peer)
                cp.wait_recv()
                cp.wait_send()
            pltpu.make_async_copy(self._rows(ins, t, me), outs[t].at[me], loc.at[t]).wait()


def _call(body, name, grid, in_specs, args, out_specs, out_shape, scratch=(), sem=(), job=None, aliases=None):
    in_specs, out_specs, out_shape, scratch = list(in_specs), list(out_specs), list(out_shape), list(scratch)
    n_in, n_out, n_scr = len(args), len(out_shape), len(scratch)
    if job is None:
        res = pl.pallas_call(body, name=name, grid=grid, in_specs=in_specs, out_specs=out_specs, out_shape=out_shape,
                             scratch_shapes=scratch, input_output_aliases=aliases or {}, compiler_params=_cp(*sem))(*args)
        return res, None
    o0 = n_in + job.n_in
    s0 = o0 + n_out + job.n_out

    def carrier(*refs):
        jin, jout, jsem = refs[n_in:o0], refs[o0 + n_out:s0], refs[s0 + n_scr:]
        ids = [pl.program_id(a) for a in range(len(grid))]
        def at(step):
            idx = []
            for g in reversed(grid):
                idx.append(step % g)
                step //= g
            return functools.reduce(jnp.logical_and, [i == j for i, j in zip(ids, reversed(idx))])

        steps = math.prod(grid)
        if grid:
            pl.when(at(0))(lambda: job.start(jin, jout, jsem))
            pl.when(at(int(steps * JOB_MIDDLE)))(lambda: job.middle(jin, jout, jsem))
        else:
            job.start(jin, jout, jsem)
            job.middle(jin, jout, jsem)
        body(*refs[:n_in], *refs[o0:o0 + n_out], *refs[s0:s0 + n_scr])
        if grid:
            pl.when(at(steps - 1))(lambda: job.finish(jin, jout, jsem))
        else:
            job.finish(jin, jout, jsem)

    res = pl.pallas_call(
        carrier, name=name + "_comm", grid=grid, in_specs=in_specs + [ANY] * job.n_in,
        out_specs=out_specs + [ANY] * job.n_out, out_shape=out_shape + job.out_shape,
        scratch_shapes=scratch + job.scratch, input_output_aliases=aliases or {},
        compiler_params=_cp(*(["arbitrary"] * len(grid))))(*args, *job.args)
    return res[:n_out], res[n_out:]


def run_job(job, name):
    return _call(lambda: None, name, (), [], [], [], [], job=job)[1]


@jax.custom_vjp
def bmm(x, w):
    return _nn(x.astype(BF16), w.astype(BF16))


def _bmm_fwd(x, w):
    return bmm(x, w), (x, w)


def _bmm_bwd(res, g):
    x, w = res
    gb = g.astype(BF16)
    return _nt(gb, w.astype(BF16)), _tn(x.astype(BF16), gb)


bmm.defvjp(_bmm_fwd, _bmm_bwd)


def _shift_raw(x, d):
    n = x.shape[0]
    r = pltpu.roll(x, d % n, axis=0)
    row = lax.broadcasted_iota(jnp.int32, x.shape, 0)
    keep = (row >= d) if d > 0 else (row < n + d)
    return jnp.where(keep, r, 0.0)


def shift_rows(x, d):
    @jax.custom_vjp
    def f(v):
        return _shift_raw(v, d)

    f.defvjp(lambda v: (_shift_raw(v, d), None), lambda _, g: (_shift_raw(g, -d),))
    return f(x)


def _swap_raw(x):
    n = x.shape[1]
    nxt = pltpu.roll(x, n - 1, axis=1)
    prv = pltpu.roll(x, 1, axis=1)
    lane = lax.broadcasted_iota(jnp.int32, x.shape, 1)
    return jnp.where(lane % 2 == 0, nxt, prv)


@jax.custom_vjp
def swap_pairs(x):
    return _swap_raw(x)


swap_pairs.defvjp(lambda x: (_swap_raw(x), None), lambda _, g: (_swap_raw(g),))


@jax.custom_vjp
def group_mean(x, bd):
    return _split_mm(x, bd)


group_mean.defvjp(lambda x, bd: (_split_mm(x, bd), bd), lambda bd, g: (_split_mm(g, bd), jnp.zeros_like(bd)))


def _rope_norm(x, gain, cos, sgn, bd, scale):
    xn = x * lax.rsqrt(group_mean(x * x, bd) + EPS) * gain
    return (xn * cos + swap_pairs(xn) * sgn) * scale


def _conv_gate(gb, gc, hc, w):
    z = gc * hc
    c = shift_rows(z, 1) * w[0:1] + z * w[1:2] + shift_rows(z, -1) * w[2:3]
    return gb * c


def _window_sum(p, r):
    b = f = p
    k = 1
    while k < r:
        b = b + shift_rows(b, k)
        f = f + shift_rows(f, -k)
        k *= 2
    return b + f - p + shift_rows(p, r) + shift_rows(p, -r)


def _pool_mix(p, pool_w, scale):
    n = p.shape[0]
    t = lax.broadcasted_iota(jnp.int32, (n, 1), 0)
    outs = []
    for gi, r in enumerate(POOL_RADII):
        pg = p[:, gi * GROUP:(gi + 1) * GROUP]
        cnt = (jnp.minimum(t + r, n - 1) - jnp.maximum(t - r, 0) + 1).astype(F32)
        pooled = _window_sum(pg, r) / cnt - pg
        outs.append(bmm(pooled, pool_w[gi]))
    return jnp.concatenate(outs, axis=1) * scale


def _sgu(u, v, norm_g, w_s, b_full):
    ug = jax.nn.gelu(u)
    vg = jax.nn.gelu(v)
    vn = vg * lax.rsqrt(jnp.mean(vg * vg, axis=-1, keepdims=True) + EPS) * norm_g
    cols = []
    for g in range(w_s.shape[0]):
        rows = []
        for n in range(u.shape[0] // SGU_CHUNK):
            blk = vn[n * SGU_CHUNK:(n + 1) * SGU_CHUNK, g * GROUP:(g + 1) * GROUP]
            rows.append(bmm(w_s[g], blk) + b_full[g])
        cols.append(jnp.concatenate(rows, axis=0))
    return ug * jnp.concatenate(cols, axis=1)


def rms_fwd(x, gain):
    T, D = x.shape
    tm = _tile(T, 512)

    def body(x_ref, g_ref, o_ref):
        xv = x_ref[...]
        r = lax.rsqrt(jnp.mean(xv * xv, axis=-1, keepdims=True) + EPS)
        o_ref[...] = (xv * r * g_ref[...]).astype(BF16)

    return pl.pallas_call(
        body, name="rms_fwd", grid=(T // tm,),
        in_specs=[pl.BlockSpec((tm, D), lambda i: (i, 0)), pl.BlockSpec((1, D), lambda i: (0, 0))],
        out_specs=pl.BlockSpec((tm, D), lambda i: (i, 0)),
        out_shape=jax.ShapeDtypeStruct((T, D), BF16), compiler_params=_cp("parallel"),
    )(x, gain)


def _rms_bwd_math(xv, gain, dy, dres):
    r = lax.rsqrt(jnp.mean(xv * xv, axis=-1, keepdims=True) + EPS)
    xh = xv * r
    dxh = dy * gain
    dx = dres + r * (dxh - xh * jnp.mean(dxh * xh, axis=-1, keepdims=True))
    return dx, jnp.sum(dy * xh, axis=0, keepdims=True)


def rms_bwd(x, gain, dy, dres):
    T, D = x.shape
    tm = _tile(T, 512)

    def body(x_ref, g_ref, dy_ref, dr_ref, dx_ref, dg_ref):
        dx, dg = _rms_bwd_math(x_ref[...], g_ref[...], dy_ref[...], dr_ref[...])
        dx_ref[...] = dx

        @pl.when(pl.program_id(0) == 0)
        def _():
            dg_ref[...] = jnp.zeros_like(dg_ref)

        dg_ref[...] += dg

    row = pl.BlockSpec((tm, D), lambda i: (i, 0))
    return pl.pallas_call(
        body, name="rms_bwd", grid=(T // tm,),
        in_specs=[row, pl.BlockSpec((1, D), lambda i: (0, 0)), row, row],
        out_specs=[row, pl.BlockSpec((1, D), lambda i: (0, 0))],
        out_shape=[jax.ShapeDtypeStruct((T, D), F32), jax.ShapeDtypeStruct((1, D), F32)],
        compiler_params=_cp("arbitrary"),
    )(x, gain, dy, dres)


FFN_TN = 256


def ffn_fwd(x, gain, wt_in, w_out, job=None):
    T, D = x.shape
    F = w_out.shape[0]
    tm, tn = _tile(T, 1024), FFN_TN
    nc = F // tn

    def body(x_ref, gn_ref, wg_ref, wu_ref, wo_ref, y_ref, gu_ref, xn_s, acc_s):
        c = pl.program_id(1)

        @pl.when(c == 0)
        def _():
            xv = x_ref[...]
            r = lax.rsqrt(jnp.mean(xv * xv, axis=-1, keepdims=True) + EPS)
            xn_s[...] = (xv * r * gn_ref[...]).astype(BF16)
            acc_s[...] = jnp.zeros_like(acc_s)

        xn = xn_s[...]
        g = _nt(xn, wg_ref[...])
        u = _nt(xn, wu_ref[...])
        gu_ref[:, :tn] = g.astype(BF16)
        gu_ref[:, tn:] = u.astype(BF16)
        a = (g * jax.nn.sigmoid(g) * u).astype(BF16)
        acc_s[...] += _nn(a, wo_ref[...])

        @pl.when(c == nc - 1)
        def _():
            y_ref[...] = x_ref[...] + 0.5 * acc_s[...]

    row = pl.BlockSpec((tm, D), lambda i, c: (i, 0))
    return _call(
        body, "ffn_fwd", (T // tm, nc),
        [row, pl.BlockSpec((1, D), lambda i, c: (0, 0)),
         pl.BlockSpec((tn, D), lambda i, c: (c, 0)),
         pl.BlockSpec((tn, D), lambda i, c: (c + nc, 0)),
         pl.BlockSpec((tn, D), lambda i, c: (c, 0))],
        [x, gain, wt_in, wt_in, w_out],
        [row, pl.BlockSpec((tm, 2 * tn), lambda i, c: (i, c))],
        [jax.ShapeDtypeStruct((T, D), F32), jax.ShapeDtypeStruct((T, 2 * F), BF16)],
        [pltpu.VMEM((tm, D), BF16), pltpu.VMEM((tm, D), F32)], ("parallel", "arbitrary"), job)


def ffn_bwd_x(dout, x, gain, gu, wt_in, w_out, job=None):
    T, D = x.shape
    F = w_out.shape[0]
    tm, tn = _tile(T, 1024), FFN_TN
    nc = F // tn

    def body(do_ref, x_ref, gn_ref, gu_ref, wg_ref, wu_ref, wo_ref,
             dx_ref, dgn_ref, a_ref, dgu_ref, xn_ref, dob_ref, acc_s):
        i, c = pl.program_id(0), pl.program_id(1)

        @pl.when(c == 0)
        def _():
            xv = x_ref[...]
            r = lax.rsqrt(jnp.mean(xv * xv, axis=-1, keepdims=True) + EPS)
            xn_ref[...] = (xv * r * gn_ref[...]).astype(BF16)
            dob_ref[...] = (0.5 * do_ref[...]).astype(BF16)
            acc_s[...] = jnp.zeros_like(acc_s)

        da = jnp.concatenate([_nt(dob_ref[:tm // 2, :], wo_ref[...]), _nt(dob_ref[tm // 2:, :], wo_ref[...])], axis=0)
        g = gu_ref[:, :tn].astype(F32)
        u = gu_ref[:, tn:].astype(F32)
        sig = jax.nn.sigmoid(g)
        sl = g * sig
        a_ref[...] = (sl * u).astype(BF16)
        dg = (da * u * (sig * (1.0 + g * (1.0 - sig)))).astype(BF16)
        du = (da * sl).astype(BF16)
        dgu_ref[:, :tn] = dg
        dgu_ref[:, tn:] = du
        acc_s[...] += _nn(dg, wg_ref[...]) + _nn(du, wu_ref[...])

        @pl.when(c == nc - 1)
        def _():
            dx, dgn = _rms_bwd_math(x_ref[...], gn_ref[...], acc_s[...], do_ref[...])
            dx_ref[...] = dx

            @pl.when(i == 0)
            def _():
                dgn_ref[...] = jnp.zeros_like(dgn_ref)

            dgn_ref[...] += dgn

    row = pl.BlockSpec((tm, D), lambda i, c: (i, 0))
    return _call(
        body, "ffn_bwd_x", (T // tm, nc),
        [row, row, pl.BlockSpec((1, D), lambda i, c: (0, 0)),
         pl.BlockSpec((tm, 2 * tn), lambda i, c: (i, c)),
         pl.BlockSpec((tn, D), lambda i, c: (c, 0)),
         pl.BlockSpec((tn, D), lambda i, c: (c + nc, 0)),
         pl.BlockSpec((tn, D), lambda i, c: (c, 0))],
        [dout, x, gain, gu, wt_in, wt_in, w_out],
        [row, pl.BlockSpec((1, D), lambda i, c: (0, 0)),
         pl.BlockSpec((tm, tn), lambda i, c: (i, c)),
         pl.BlockSpec((tm, 2 * tn), lambda i, c: (i, c)), row, row],
        [jax.ShapeDtypeStruct((T, D), F32), jax.ShapeDtypeStruct((1, D), F32),
         jax.ShapeDtypeStruct((T, F), BF16), jax.ShapeDtypeStruct((T, 2 * F), BF16),
         jax.ShapeDtypeStruct((T, D), BF16), jax.ShapeDtypeStruct((T, D), BF16)],
        [pltpu.VMEM((tm, D), F32)], ("arbitrary", "arbitrary"), job)


MM_TM = 512
MM_TC = 256


def mm_nt(a, wt, pieces, out_dtype, emit_a_bf16=False):
    T, K = a.shape
    tm = _tile(T, MM_TM)
    npc = len(pieces)

    def body(*refs):
        a_ref, w_refs, o_refs = refs[0], refs[1:1 + npc], refs[1 + npc:]
        ab = a_ref[...].astype(BF16)
        for w_ref, o_ref in zip(w_refs, o_refs[:npc]):
            o_ref[...] = _nt(ab, w_ref[...]).astype(o_ref.dtype)
        if emit_a_bf16:
            o_refs[npc][...] = ab

    in_specs = [pl.BlockSpec((tm, K), lambda i: (i, 0))]
    out_specs, out_shape = [], []
    for r0, n in pieces:
        assert r0 % n == 0
        in_specs.append(pl.BlockSpec((n, K), functools.partial(lambda i, b: (b, 0), b=r0 // n)))
        out_specs.append(pl.BlockSpec((tm, n), lambda i: (i, 0)))
        out_shape.append(jax.ShapeDtypeStruct((T, n), out_dtype))
    if emit_a_bf16:
        out_specs.append(pl.BlockSpec((tm, K), lambda i: (i, 0)))
        out_shape.append(jax.ShapeDtypeStruct((T, K), BF16))
    return pl.pallas_call(
        body, name="mm_nt", grid=(T // tm,), in_specs=in_specs, out_specs=out_specs, out_shape=out_shape,
        compiler_params=_cp("parallel"),
    )(a, *([wt] * npc))


def mm_nn(pieces, w, residual=None):
    T = pieces[0][0].shape[0]
    N = w.shape[1]
    tm = _tile(T, MM_TM)
    na = len(pieces)

    def body(*refs):
        a_refs, w_refs = refs[:na], refs[na:2 * na]
        o_ref = refs[-1]
        acc = refs[2 * na][...] if residual is not None else None
        for a_ref, w_ref in zip(a_refs, w_refs):
            t = _nn(a_ref[...].astype(BF16), w_ref[...])
            acc = t if acc is None else acc + t
        o_ref[...] = acc

    in_specs, w_specs = [], []
    for a, cb, k, rb in pieces:
        in_specs.append(pl.BlockSpec((tm, k), functools.partial(lambda i, b: (i, b), b=cb)))
        w_specs.append(pl.BlockSpec((k, N), functools.partial(lambda i, b: (b, 0), b=rb)))
    assert sum(k for _, _, k, _ in pieces) == w.shape[0]
    args = [a for a, _, _, _ in pieces] + [w] * na
    in_specs = in_specs + w_specs
    if residual is not None:
        in_specs.append(pl.BlockSpec((tm, N), lambda i: (i, 0)))
        args.append(residual)
    return pl.pallas_call(
        body, name="mm_nn", grid=(T // tm,), in_specs=in_specs,
        out_specs=pl.BlockSpec((tm, N), lambda i: (i, 0)),
        out_shape=jax.ShapeDtypeStruct((T, N), F32), compiler_params=_cp("parallel"),
    )(*args)


def mm_tn(a, b, n_rows, row_block, prev=None, grid=None, col_block=None, job=None):
    T, M = a.shape
    N = b.shape[1]
    tc = MM_TC
    assert M % tc == 0 and n_rows % tc == 0
    if grid is None:
        grid, col_block = (M // tc,), (lambda j: j)

    def body(*refs):
        a_ref, b_ref, o_ref = refs[0], refs[1], refs[-1]
        o_ref[...] = _tn(a_ref[...], b_ref[...]).astype(BF16)

    in_specs = [pl.BlockSpec((T, tc), lambda *g: (0, col_block(*g))), pl.BlockSpec((T, N), lambda *g: (0, 0))]
    args = [a, b]
    aliases = {}
    if prev is not None:
        in_specs.append(pl.BlockSpec(memory_space=pl.ANY))
        args.append(prev)
        aliases = {2: 0}
    res, jres = _call(body, "mm_tn", grid, in_specs, args, [pl.BlockSpec((tc, N), lambda *g: (row_block(*g), 0))],
                      [jax.ShapeDtypeStruct((n_rows, N), BF16)], (), ["parallel"] * len(grid), job, aliases)
    return res[0] if job is None else (res[0], jres)


def conv_fwd(proj_a, conv_w, n_ex):
    T, C3 = proj_a.shape
    C = C3 // 3
    S = T // n_ex

    def body(gb_ref, gc_ref, hc_ref, w_ref, o_ref):
        o_ref[...] = _conv_gate(gb_ref[...], gc_ref[...], hc_ref[...], w_ref[...]).astype(BF16)

    col = lambda k: pl.BlockSpec((S, C), functools.partial(lambda b, kk: (b, kk), kk=k))
    return pl.pallas_call(
        body, name="conv_fwd", grid=(n_ex,),
        in_specs=[col(0), col(1), col(2), pl.BlockSpec((3, C), lambda b: (0, 0))],
        out_specs=pl.BlockSpec((S, C), lambda b: (b, 0)),
        out_shape=jax.ShapeDtypeStruct((T, 2 * C), BF16), compiler_params=_cp("parallel"),
    )(proj_a, proj_a, proj_a, conv_w)


def conv_bwd(proj_a, conv_w, n_ex, dy, total_cols):
    T, C3 = proj_a.shape
    C = C3 // 3
    S = T // n_ex

    def body(gb_ref, gc_ref, hc_ref, w_ref, dy_ref, dp_ref, dw_ref):
        _, vjp = jax.vjp(_conv_gate, gb_ref[...], gc_ref[...], hc_ref[...], w_ref[...])
        dgb, dgc, dhc, dw = vjp(dy_ref[...].astype(F32))
        dp_ref[:, 0:C] = dgb.astype(BF16)
        dp_ref[:, C:2 * C] = dgc.astype(BF16)
        dp_ref[:, 2 * C:] = dhc.astype(BF16)

        @pl.when(pl.program_id(0) == 0)
        def _():
            dw_ref[...] = jnp.zeros_like(dw_ref)

        dw_ref[...] += dw

    col = lambda k: pl.BlockSpec((S, C), functools.partial(lambda b, kk: (b, kk), kk=k))
    return pl.pallas_call(
        body, name="conv_bwd", grid=(n_ex,),
        in_specs=[col(0), col(1), col(2), pl.BlockSpec((3, C), lambda b: (0, 0)),
                  pl.BlockSpec((S, C), lambda b: (b, 0))],
        out_specs=[pl.BlockSpec((S, C3), lambda b: (b, 0)), pl.BlockSpec((3, C), lambda b: (0, 0))],
        out_shape=[jax.ShapeDtypeStruct((T, total_cols), BF16), jax.ShapeDtypeStruct((3, C), F32)],
        compiler_params=_cp("arbitrary"),
    )(proj_a, proj_a, proj_a, conv_w, dy)


QW = N_Q_HEADS * HEAD_DIM
KW = N_KV_HEADS * HEAD_DIM
QP = N_Q_HEADS * LANES
KP = N_KV_HEADS * LANES


def _attn_consts(seq):
    rows = seq // GRID_W
    r_idx, c_idx = jnp.meshgrid(jnp.arange(rows), jnp.arange(GRID_W), indexing='ij')
    r_idx = r_idx.reshape(-1).astype(F32)
    c_idx = c_idx.reshape(-1).astype(F32)
    n_freq = HEAD_DIM // 4
    inv = ROPE_THETA ** (-jnp.arange(n_freq, dtype=F32) / n_freq)
    ang = jnp.concatenate([r_idx[:, None] * inv, c_idx[:, None] * inv], axis=-1)
    cos = jnp.repeat(jnp.cos(ang), 2, axis=1)
    sin = jnp.repeat(jnp.sin(ang), 2, axis=1)
    sgn = sin * jnp.tile(jnp.array([-1.0, 1.0], F32), HEAD_DIM // 2)
    cos = jnp.tile(cos, (1, N_Q_HEADS))
    sgn = jnp.tile(sgn, (1, N_Q_HEADS))
    lane = jnp.arange(QW)
    bd = jnp.where(lane[:, None] // HEAD_DIM == lane[None, :] // HEAD_DIM, 1.0 / HEAD_DIM, 0.0).astype(BF16)
    dst = (lane // HEAD_DIM) * LANES + lane % HEAD_DIM
    spread = (dst[:, None] == jnp.arange(QP)[None, :]).astype(BF16)
    return dict(cos=cos, sgn=sgn, bd=bd, spread=spread, gather=spread.T)


def qkv_prep_fwd(proj_b, qg, kg, cst, n_ex):
    T = proj_b.shape[0]
    S = T // n_ex
    tm = _tile(S, 512)
    nb = S // tm

    def body(p_ref, qg_ref, kg_ref, cos_ref, sgn_ref, bd_ref, sp_ref, q_ref, k_ref, v_ref):
        pv = p_ref[...]
        cos, sgn, bd, sp = cos_ref[...], sgn_ref[...], bd_ref[...], sp_ref[...]
        qr = _rope_norm(pv[:, :QW], qg_ref[...], cos, sgn, bd, HEAD_DIM ** -0.5)
        kr = _rope_norm(pv[:, QW:QW + KW], kg_ref[...], cos[:, :KW], sgn[:, :KW], bd[:KW, :KW], 1.0)
        q_ref[...] = _nn(qr.astype(BF16), sp).astype(BF16)
        k_ref[...] = _nn(kr.astype(BF16), sp[:KW, :KP]).astype(BF16)
        v_ref[...] = _nn(pv[:, QW + KW:].astype(BF16), sp[:KW, :KP]).astype(BF16)

    full = lambda a: pl.BlockSpec(a.shape, lambda i: (0,) * a.ndim)
    tab = pl.BlockSpec((tm, QW), lambda i: (i % nb, 0))
    return pl.pallas_call(
        body, name="qkv_prep_fwd", grid=(T // tm,),
        in_specs=[pl.BlockSpec((tm, QW + 2 * KW), lambda i: (i, 0)), full(qg), full(kg), tab, tab,
                  full(cst["bd"]), full(cst["spread"])],
        out_specs=[pl.BlockSpec((tm, QP), lambda i: (i, 0)), pl.BlockSpec((tm, KP), lambda i: (i, 0)),
                   pl.BlockSpec((tm, KP), lambda i: (i, 0))],
        out_shape=[jax.ShapeDtypeStruct((T, QP), BF16), jax.ShapeDtypeStruct((T, KP), BF16),
                   jax.ShapeDtypeStruct((T, KP), BF16)],
        compiler_params=_cp("parallel"),
    )(proj_b, qg, kg, cst["cos"], cst["sgn"], cst["bd"], cst["spread"])


def qkv_prep_bwd(proj_b, qg, kg, cst, n_ex, dq, dk_pad, dv_pad, d_proj):
    T = proj_b.shape[0]
    S = T // n_ex
    tm = _tile(S, 512)
    nb = S // tm

    def body(p_ref, qg_ref, kg_ref, cos_ref, sgn_ref, bd_ref, ga_ref, dq_ref, dk_ref, dv_ref, _kept,
             dp_ref, dqg_ref, dkg_ref):
        pv = p_ref[...]
        cos, sgn, bd, ga = cos_ref[...], sgn_ref[...], bd_ref[...], ga_ref[...]
        fq = lambda q, g: _rope_norm(q, g, cos, sgn, bd, HEAD_DIM ** -0.5)
        fk = lambda k, g: _rope_norm(k, g, cos[:, :KW], sgn[:, :KW], bd[:KW, :KW], 1.0)
        _, vq = jax.vjp(fq, pv[:, :QW], qg_ref[...])
        _, vk = jax.vjp(fk, pv[:, QW:QW + KW], kg_ref[...])
        dqp, dqg = vq(dq_ref[...])
        dkp, dkg = vk(_split_mm(dk_ref[...], ga[:KP, :KW]))
        dp_ref[:, :QW] = dqp.astype(BF16)
        dp_ref[:, QW:QW + KW] = dkp.astype(BF16)
        dp_ref[:, QW + KW:] = _split_mm(dv_ref[...], ga[:KP, :KW]).astype(BF16)

        @pl.when(pl.program_id(0) == 0)
        def _():
            dqg_ref[...] = jnp.zeros_like(dqg_ref)
            dkg_ref[...] = jnp.zeros_like(dkg_ref)

        dqg_ref[...] += dqg
        dkg_ref[...] += dkg

    full = lambda a: pl.BlockSpec(a.shape, lambda i: (0,) * a.ndim)
    tab = pl.BlockSpec((tm, QW), lambda i: (i % nb, 0))
    row = lambda n: pl.BlockSpec((tm, n), lambda i: (i, 0))
    wb = QW + 2 * KW
    assert d_proj.shape[1] % wb == 0
    last = d_proj.shape[1] // wb - 1
    return pl.pallas_call(
        body, name="qkv_prep_bwd", grid=(T // tm,),
        in_specs=[row(wb), full(qg), full(kg), tab, tab, full(cst["bd"]), full(cst["gather"]),
                  row(QW), row(KP), row(KP), ANY],
        out_specs=[pl.BlockSpec((tm, wb), lambda i: (i, last)), pl.BlockSpec((1, QW), lambda i: (0, 0)),
                   pl.BlockSpec((1, KW), lambda i: (0, 0))],
        out_shape=[jax.ShapeDtypeStruct(d_proj.shape, BF16), jax.ShapeDtypeStruct((1, QW), F32),
                   jax.ShapeDtypeStruct((1, KW), F32)],
        input_output_aliases={10: 0}, compiler_params=_cp("arbitrary"),
    )(proj_b, qg, kg, cst["cos"], cst["sgn"], cst["bd"], cst["gather"], dq, dk_pad, dv_pad, d_proj)


ATT_TQ = 256


def attn_fwd(qp, kp, vp, gather, n_ex, mix, job=None):
    T = qp.shape[0]
    S = T // n_ex
    tq = _tile(S, ATT_TQ)
    nq = S // tq

    def body(q_ref, k_ref, v_ref, ga_ref, _kept, o_ref, op_ref, lse_ref):
        lane = lax.broadcasted_iota(jnp.int32, (tq, LANES), 1)
        lse_all = jnp.zeros((tq, LANES), F32)
        for h in range(N_Q_HEADS):
            kv = h // Q_PER_KV
            qh = q_ref[:, h * LANES:(h + 1) * LANES]
            s = _nt(qh, k_ref[:, kv * LANES:(kv + 1) * LANES])
            m = jnp.max(s, axis=-1, keepdims=True)
            p = jnp.exp(s - m)
            lsum = jnp.sum(p, axis=-1, keepdims=True)
            o = _nn(p.astype(BF16), v_ref[:, kv * LANES:(kv + 1) * LANES]) / lsum
            op_ref[:, h * LANES:(h + 1) * LANES] = o.astype(BF16)
            lse_all = jnp.where(lane == h, m + jnp.log(lsum), lse_all)
        lse_ref[...] = lse_all
        o_ref[...] = _nn(op_ref[...], ga_ref[...]).astype(BF16)

    blk = lambda n: pl.BlockSpec((tq, n), lambda b, i: (b * nq + i, 0))
    kvs = pl.BlockSpec((S, KP), lambda b, i: (b, 0))
    return _call(
        body, "attn_fwd", (n_ex, nq),
        [blk(QP), kvs, kvs, pl.BlockSpec(gather.shape, lambda b, i: (0, 0)), ANY], [qp, kp, vp, gather, mix],
        [pl.BlockSpec((tq, QW), lambda b, i: (b * nq + i, 1)), blk(QP), blk(LANES)],
        [jax.ShapeDtypeStruct(mix.shape, BF16), jax.ShapeDtypeStruct((T, QP), BF16),
         jax.ShapeDtypeStruct((T, LANES), F32)], (), ("parallel", "parallel"), job, {4: 0})


def attn_bwd(qp, kp, vp, op, lse, do, cst, n_ex, job=None):
    T = qp.shape[0]
    S = T // n_ex
    tq = _tile(S, ATT_TQ)
    nq = S // tq

    def body(q_ref, k_ref, v_ref, op_ref, lse_ref, do_ref, sp_ref, ga_ref, dq_ref, dk_ref, dv_ref, dqp_s):
        @pl.when(pl.program_id(1) == 0)
        def _():
            dk_ref[...] = jnp.zeros_like(dk_ref)
            dv_ref[...] = jnp.zeros_like(dv_ref)

        lane = lax.broadcasted_iota(jnp.int32, (tq, LANES), 1)
        dop = _nn(do_ref[...], sp_ref[...]).astype(BF16)
        lse_all = lse_ref[...]
        for h in range(N_Q_HEADS):
            kv = h // Q_PER_KV
            hs = slice(h * LANES, (h + 1) * LANES)
            ks = slice(kv * LANES, (kv + 1) * LANES)
            qh, kk, vv = q_ref[:, hs], k_ref[:, ks], v_ref[:, ks]
            doh = dop[:, hs]
            lse_h = jnp.sum(jnp.where(lane == h, lse_all, 0.0), axis=-1, keepdims=True)
            p = jnp.exp(_nt(qh, kk) - lse_h)
            dp = _nt(doh, vv)
            delta = jnp.sum(doh.astype(F32) * op_ref[:, hs].astype(F32), axis=-1, keepdims=True)
            ds = (p * (dp - delta)).astype(BF16)
            dqp_s[:, hs] = _nn(ds, kk)
            dk_ref[:, ks] += _tn(ds, qh)
            dv_ref[:, ks] += _tn(p.astype(BF16), doh)
        dq_ref[...] = _split_mm(dqp_s[...], ga_ref[...])

    blk = lambda n: pl.BlockSpec((tq, n), lambda b, i: (b * nq + i, 0))
    kvs = pl.BlockSpec((S, KP), lambda b, i: (b, 0))
    full = lambda a: pl.BlockSpec(a.shape, lambda b, i: (0, 0))
    return _call(
        body, "attn_bwd", (n_ex, nq),
        [blk(QP), kvs, kvs, blk(QP), blk(LANES), pl.BlockSpec((tq, QW), lambda b, i: (b * nq + i, 1)),
         full(cst["spread"]), full(cst["gather"])],
        [qp, kp, vp, op, lse, do, cst["spread"], cst["gather"]],
        [blk(QW), kvs, kvs],
        [jax.ShapeDtypeStruct((T, QW), F32), jax.ShapeDtypeStruct((T, KP), F32), jax.ShapeDtypeStruct((T, KP), F32)],
        [pltpu.VMEM((tq, QP), F32)], ("arbitrary", "arbitrary"), job)


def pool_fwd(p, pool_w, scale, n_ex):
    T, W = p.shape
    S = T // n_ex

    def body(p_ref, w_ref, s_ref, o_ref):
        o_ref[...] = _pool_mix(p_ref[...], w_ref[...], s_ref[...]).astype(BF16)

    return pl.pallas_call(
        body, name="pool_fwd", grid=(n_ex,),
        in_specs=[pl.BlockSpec((S, W), lambda b: (b, 0)),
                  pl.BlockSpec(pool_w.shape, lambda b: (0, 0, 0)),
                  pl.BlockSpec((1, W), lambda b: (0, 0))],
        out_specs=pl.BlockSpec((S, W), lambda b: (b, 0)),
        out_shape=jax.ShapeDtypeStruct((T, 2 * W), BF16), compiler_params=_cp("parallel"),
    )(p, pool_w, scale)


def pool_bwd(p, pool_w, scale, n_ex, dy, d_proj):
    T, W = p.shape
    S = T // n_ex
    last = d_proj.shape[1] // W - 1

    def body(p_ref, w_ref, s_ref, dy_ref, _kept, dp_ref, dw_ref, ds_ref):
        _, vjp = jax.vjp(_pool_mix, p_ref[...], w_ref[...], s_ref[...])
        dp, dw, ds = vjp(dy_ref[...].astype(F32))
        dp_ref[...] = dp.astype(BF16)

        @pl.when(pl.program_id(0) == 0)
        def _():
            dw_ref[...] = jnp.zeros_like(dw_ref)
            ds_ref[...] = jnp.zeros_like(ds_ref)

        dw_ref[...] += dw
        ds_ref[...] += ds

    wshape = pool_w.shape
    return pl.pallas_call(
        body, name="pool_bwd", grid=(n_ex,),
        in_specs=[pl.BlockSpec((S, W), lambda b: (b, 0)),
                  pl.BlockSpec(wshape, lambda b: (0, 0, 0)),
                  pl.BlockSpec((1, W), lambda b: (0, 0)), pl.BlockSpec((S, W), lambda b: (b, 0)), ANY],
        out_specs=[pl.BlockSpec((S, W), lambda b: (b, last)), pl.BlockSpec(wshape, lambda b: (0, 0, 0)),
                   pl.BlockSpec((1, W), lambda b: (0, 0))],
        out_shape=[jax.ShapeDtypeStruct(d_proj.shape, BF16), jax.ShapeDtypeStruct(wshape, F32),
                   jax.ShapeDtypeStruct((1, W), F32)],
        input_output_aliases={4: 0}, compiler_params=_cp("arbitrary"),
    )(p, pool_w, scale, dy, d_proj)


SGU_TS = 512


def sgu_fwd(u, v, norm_g, w_s, b_full, mix):
    T, W = u.shape
    ts = _tile(T, SGU_TS)

    def body(u_ref, v_ref, g_ref, w_ref, b_ref, _kept, o_ref):
        o_ref[...] = _sgu(u_ref[...], v_ref[...], g_ref[...], w_ref[...], b_ref[...]).astype(BF16)

    row = pl.BlockSpec((ts, W), lambda i: (i, 0))
    wsp = pl.BlockSpec(w_s.shape, lambda i: (0, 0, 0))
    return pl.pallas_call(
        body, name="sgu_fwd", grid=(T // ts,),
        in_specs=[row, row, pl.BlockSpec((1, W), lambda i: (0, 0)), wsp, wsp, ANY],
        out_specs=pl.BlockSpec((ts, W), lambda i: (i, 1)), out_shape=jax.ShapeDtypeStruct(mix.shape, BF16),
        input_output_aliases={5: 0}, compiler_params=_cp("parallel"),
    )(u, v, norm_g, w_s, b_full, mix)


def sgu_bwd(u, v, norm_g, w_s, b_full, dy):
    T, W = u.shape
    ts = _tile(T, SGU_TS)
    wshape = w_s.shape

    def body(u_ref, v_ref, g_ref, w_ref, b_ref, dy_ref, duv_ref, dg_ref, dw_ref, db_ref):
        _, vjp = jax.vjp(_sgu, u_ref[...], v_ref[...], g_ref[...], w_ref[...], b_ref[...])
        du, dv, dg, dw, db = vjp(dy_ref[...].astype(F32))
        duv_ref[:, :W] = du.astype(BF16)
        duv_ref[:, W:] = dv.astype(BF16)

        @pl.when(pl.program_id(0) == 0)
        def _():
            dg_ref[...] = jnp.zeros_like(dg_ref)
            dw_ref[...] = jnp.zeros_like(dw_ref)
            db_ref[...] = jnp.zeros_like(db_ref)

        dg_ref[...] += dg
        dw_ref[...] += dw
        db_ref[...] += db

    row = pl.BlockSpec((ts, W), lambda i: (i, 0))
    wsp = pl.BlockSpec(wshape, lambda i: (0, 0, 0))
    wout = pl.BlockSpec(wshape, lambda i: (0, 0, 0))
    vec = pl.BlockSpec((1, W), lambda i: (0, 0))
    return pl.pallas_call(
        body, name="sgu_bwd", grid=(T // ts,),
        in_specs=[row, row, vec, wsp, wsp, pl.BlockSpec((ts, W), lambda i: (i, 1))],
        out_specs=[pl.BlockSpec((ts, 2 * W), lambda i: (i, 0)), vec, wout, wout],
        out_shape=[jax.ShapeDtypeStruct((T, 3 * W), BF16),
                   jax.ShapeDtypeStruct((1, W), F32), jax.ShapeDtypeStruct(wshape, F32),
                   jax.ShapeDtypeStruct(wshape, F32)],
        compiler_params=_cp("arbitrary"),
    )(u, v, norm_g, w_s, b_full, dy)


def loss_head(x, gain, target):
    T, D = x.shape
    tm = _tile(T, 512)

    def body(x_ref, g_ref, t_ref, loss_ref, dx_ref, dg_ref):
        xv, g = x_ref[...], g_ref[...]
        r = lax.rsqrt(jnp.mean(xv * xv, axis=-1, keepdims=True) + EPS)
        err = xv * r * g - t_ref[...]
        part = 0.5 * jnp.sum(jnp.mean(err * err, axis=-1, keepdims=True), axis=0, keepdims=True)
        dx, dg = _rms_bwd_math(xv, g, err * (1.0 / D), jnp.zeros_like(xv))
        dx_ref[...] = dx

        @pl.when(pl.program_id(0) == 0)
        def _():
            loss_ref[...] = jnp.zeros_like(loss_ref)
            dg_ref[...] = jnp.zeros_like(dg_ref)

        loss_ref[...] += part
        dg_ref[...] += dg

    row = pl.BlockSpec((tm, D), lambda i: (i, 0))
    vec = pl.BlockSpec((1, D), lambda i: (0, 0))
    return pl.pallas_call(
        body, name="loss_head", grid=(T // tm,),
        in_specs=[row, vec, row], out_specs=[pl.BlockSpec((1, 1), lambda i: (0, 0)), row, vec],
        out_shape=[jax.ShapeDtypeStruct((1, 1), F32), jax.ShapeDtypeStruct((T, D), F32),
                   jax.ShapeDtypeStruct((1, D), F32)],
        compiler_params=_cp("arbitrary"),
    )(x, gain, target)


class MultiJob:
    def __init__(self, jobs):
        self.jobs = jobs
        self.args = [a for j in jobs for a in j.args]
        self.out_shape = [s for j in jobs for s in j.out_shape]
        self.scratch = [s for j in jobs for s in j.scratch]
        self.n_in, self.n_out = len(self.args), len(self.out_shape)

    def _each(self, ins, outs, sems):
        i = o = s = 0
        for j in self.jobs:
            yield j, ins[i:i + j.n_in], outs[o:o + j.n_out], sems[s:s + len(j.scratch)]
            i, o, s = i + j.n_in, o + j.n_out, s + len(j.scratch)

    def start(self, ins, outs, sems):
        for j, a, b, c in self._each(ins, outs, sems):
            j.start(a, b, c)

    def middle(self, ins, outs, sems):
        for j, a, b, c in self._each(ins, outs, sems):
            j.middle(a, b, c)

    def finish(self, ins, outs, sems):
        for j, a, b, c in self._each(ins, outs, sems):
            j.finish(a, b, c)

    def split(self, results):
        o = 0
        for j in self.jobs:
            yield results[o:o + j.n_out]
            o += j.n_out


class Plan:
    def __init__(self, shard, gathers, scatters, small_carrier=None, pack_small=None):
        self.shard, self.gathers, self.scatters = shard, gathers, scatters
        self.small_carrier, self.pack_small = small_carrier, pack_small
        self.weights, self.grads, self.parts = {}, {}, {}
        self.small_src = self.small_parts = None

    def weight(self, kind, l):
        return self.weights[(kind, l)]

    def grad(self, kind, l, g):
        self.grads[(kind, l)] = g

    def small_ready(self, small, d_final):
        if self.pack_small is not None:
            self.small_src = self.pack_small(small, d_final)

    def _jobs(self, key):
        jobs = []
        if key in self.gathers:
            ks = self.gathers[key]
            jobs.append((GatherJob([self.shard(*k) for k in ks]), self.weights, ks))
        if key in self.scatters:
            ks = self.scatters[key]
            jobs.append((ScatterJob([self.grads[k] for k in ks]), self.parts, ks))
        if key == self.small_carrier and self.small_src is not None:
            jobs.append((GatherJob([(self.small_src, None)]), None, None))
        return jobs

    def _deliver(self, jobs, results):
        multi = MultiJob([j for j, _, _ in jobs])
        for (_, store, ks), res in zip(jobs, multi.split(results)):
            if store is None:
                self.small_parts = res[0]
            else:
                store.update(zip(ks, res))

    def run(self, key, fn, *args, **kw):
        jobs = self._jobs(key)
        if not jobs:
            out = fn(*args, **kw)
            return out if fn is mm_tn else out[0]
        res, jres = fn(*args, job=MultiJob([j for j, _, _ in jobs]), **kw)
        self._deliver(jobs, jres)
        return res

    def alone(self, key, name):
        jobs = self._jobs(key)
        if jobs:
            self._deliver(jobs, run_job(MultiJob([j for j, _, _ in jobs]), name))


def _local_step(x, target, layers, final_norm, n_ex, plan):
    T, D = x.shape
    L = len(layers)
    cst = _attn_consts(T // n_ex)
    ident = lambda j: j
    EV_A, EV_B = 3 * (D // 2), QW + 2 * KW
    OD_W = D // 2
    wt = plan.weight

    saved = []
    for l, W in enumerate(layers):
        s = dict(x0=x)
        x1, s["gu1"] = plan.run(("ffn1_fwd", l), ffn_fwd, x, W["n1"], wt("f1_in_t", l), wt("f1_out", l))
        h = rms_fwd(x1, W["nm"])
        if l % 2 == 0:
            pa, pb = mm_nt(h, wt("mx_in_t", l), [(0, EV_A), (EV_A, EV_B)], F32)
            qg = jnp.tile(W["q_norm"], N_Q_HEADS)[None]
            kg = jnp.tile(W["k_norm"], N_KV_HEADS)[None]
            mix = conv_fwd(pa, W["conv_w"], n_ex)
            qp, kp, vp = qkv_prep_fwd(pb, qg, kg, cst, n_ex)
            mix, op, lse = plan.run(("attn_fwd", l), attn_fwd, qp, kp, vp, cst["gather"], n_ex, mix)
            s.update(pa=pa, pb=pb, qg=qg, kg=kg, qp=qp, kp=kp, vp=vp, op=op, lse=lse)
        else:
            p, u, v = mm_nt(h, wt("mx_in_t", l), [(0, OD_W), (OD_W, OD_W), (2 * OD_W, OD_W)], F32)
            scale = W["pool_scale"][None]
            sn = W["sgu_norm"][None]
            b_full = jnp.broadcast_to(W["sgu_b"][..., None], W["sgu_w"].shape)
            mix = sgu_fwd(u, v, sn, W["sgu_w"], b_full, pool_fwd(p, W["pool_w"], scale, n_ex))
            s.update(p=p, u=u, v=v, scale=scale, sn=sn, b_full=b_full)
        x2 = mm_nn([(mix, 0, D, 0)], wt("mx_out", l), residual=x1)
        x3, s["gu2"] = plan.run(("ffn2_fwd", l), ffn_fwd, x2, W["n2"], wt("f2_in_t", l), wt("f2_out", l))
        s.update(x1=x1, x2=x2, h=h, mix=mix)
        saved.append(s)
        x = x3

    loss, dx, d_final = loss_head(x, final_norm, target)

    small = [None] * L

    def ffn_back(which, l, dout, xin, gain, gu, sm, sm_key):
        w_in, w_out = wt(which + "_in_t", l), wt(which + "_out", l)
        F = w_out.shape[0]
        nc = F // FFN_TN
        dxi, sm[sm_key], a, dgu, xn, dob = plan.run((which + "_bwd", l), ffn_bwd_x, dout, xin, gain, gu, w_in, w_out)
        plan.grad(which + "_out", l, plan.run((which + "_out_grad", l), mm_tn, a, dob, F, ident))
        if which == "f1" and l == 0:
            plan.small_ready(small, d_final)
        plan.grad(which + "_in_t", l, plan.run(
            (which + "_in_grad", l), mm_tn, dgu, xn, 2 * F, lambda k, c: k * nc + c, grid=(2, nc),
            col_block=lambda k, c: 2 * c + k))
        return dxi

    for l in reversed(range(L)):
        s, W = saved[l], layers[l]
        sm = small[l] = {}
        dx = ffn_back("f2", l, dx, s["x2"], W["n2"], s["gu2"], sm, "n2")
        dmix, dxb = mm_nt(dx, wt("mx_out", l), [(0, D)], BF16, emit_a_bf16=True)
        plan.grad("mx_out", l, mm_tn(s["mix"], dxb, D, ident))
        if l % 2 == 0:
            d_proj, sm["conv_w"] = conv_bwd(s["pa"], W["conv_w"], n_ex, dmix, EV_A + EV_B)
            dq, dkp, dvp = plan.run(("attn_bwd", l), attn_bwd, s["qp"], s["kp"], s["vp"], s["op"], s["lse"], dmix, cst, n_ex)
            d_proj, dqg, dkg = qkv_prep_bwd(s["pb"], s["qg"], s["kg"], cst, n_ex, dq, dkp, dvp, d_proj)
            dh = mm_nn([(d_proj, 0, EV_A + EV_B, 0)], wt("mx_in_t", l))
            plan.grad("mx_in_t", l, mm_tn(d_proj, s["h"], EV_A + EV_B, ident))
            sm["q_norm"] = dqg.reshape(N_Q_HEADS, HEAD_DIM).sum(0)
            sm["k_norm"] = dkg.reshape(N_KV_HEADS, HEAD_DIM).sum(0)
        else:
            d_proj, d_sn, sm["sgu_w"], d_sb = sgu_bwd(s["u"], s["v"], s["sn"], W["sgu_w"], s["b_full"], dmix)
            d_proj, sm["pool_w"], d_ps = pool_bwd(s["p"], W["pool_w"], s["scale"], n_ex, dmix, d_proj)
            dh = mm_nn([(d_proj, 0, OD_W, 1), (d_proj, 1, OD_W, 2), (d_proj, 2, OD_W, 0)], wt("mx_in_t", l))
            nb = OD_W // MM_TC
            plan.grad("mx_in_t", l, mm_tn(d_proj, s["h"], 3 * OD_W,
                                          lambda jj: jnp.where(jj < 2 * nb, jj + nb, jj - 2 * nb)))
            sm["pool_scale"], sm["sgu_norm"], sm["sgu_b"] = d_ps[0], d_sn[0], d_sb.sum(-1)
        dx, sm["nm"] = rms_bwd(s["x1"], W["nm"], dh, dx)
        dx = ffn_back("f1", l, dx, s["x0"], W["n1"], s["gu1"], sm, "n1")
    return loss, dx


def all_gather(srcs):
    return run_job(GatherJob(srcs), "all_gather")


def all_reduce_small(v):
    R, C = v.shape

    def body(v_ref, o_ref, buf, send_sems, recv_sems):
        x, y, c, me = _my_place()
        buf[me] = v_ref[...]
        for k in range(1, N_DEV):
            peer, _ = _peer(x, y, c, k)
            pltpu.make_async_remote_copy(src_ref=v_ref, dst_ref=buf.at[me], send_sem=send_sems.at[k - 1],
                                         recv_sem=recv_sems.at[k - 1], device_id=peer, device_id_type=MESH_ID).start()
        for k in range(1, N_DEV):
            peer, pidx = _peer(x, y, c, k)
            cp = pltpu.make_async_remote_copy(src_ref=v_ref, dst_ref=buf.at[pidx], send_sem=send_sems.at[k - 1],
                                              recv_sem=recv_sems.at[k - 1], device_id=peer, device_id_type=MESH_ID)
            cp.wait_recv()
            cp.wait_send()
        acc = buf[0]
        for s in range(1, N_DEV):
            acc = acc + buf[s]
        o_ref[...] = acc

    vm = pl.BlockSpec(memory_space=pltpu.VMEM)
    return pl.pallas_call(
        body, name="all_reduce_small", in_specs=[vm], out_specs=vm, out_shape=jax.ShapeDtypeStruct((R, C), F32),
        scratch_shapes=[pltpu.VMEM((N_DEV, R, C), F32), pltpu.SemaphoreType.DMA((7,)), pltpu.SemaphoreType.DMA((7,))],
        compiler_params=pltpu.CompilerParams(vmem_limit_bytes=VMEM_LIMIT),
    )(v)


def cast_shards(w):
    L, A, B = w.shape

    def body(w_ref, o_ref):
        o_ref[...] = w_ref[...].astype(BF16)

    return pl.pallas_call(
        body, name="cast_shards", grid=(L,),
        in_specs=[pl.BlockSpec((None, A, B), lambda l: (l, 0, 0))],
        out_specs=pl.BlockSpec((None, A, B), lambda l: (l, 0, 0)),
        out_shape=jax.ShapeDtypeStruct((L, A, B), BF16), compiler_params=_cp("parallel"),
    )(w)


ADAM_TC = 256


def adamw(parts, w, m, v, l, prev=None):
    P, R, C = parts.shape
    tc = _tile(C, ADAM_TC)
    c1, c2 = 1.0 - ADAM_B1 ** ADAM_STEP, 1.0 - ADAM_B2 ** ADAM_STEP

    def body(p_ref, w_ref, m_ref, v_ref, g_ref, d_ref, mo_ref, vo_ref):
        g = p_ref[0].astype(F32)
        for s in range(1, P):
            g = g + p_ref[s].astype(F32)
        m1 = ADAM_B1 * m_ref[...] + (1.0 - ADAM_B1) * g
        v1 = ADAM_B2 * v_ref[...] + (1.0 - ADAM_B2) * (g * g)
        g_ref[...] = g
        mo_ref[...] = m1
        vo_ref[...] = v1
        d_ref[...] = -ADAM_LR * ((m1 / c1) / (jnp.sqrt(v1 / c2) + ADAM_EPS) + ADAM_WD * w_ref[...])

    wspec = pl.BlockSpec((None, R, tc), lambda i: (l, 0, i))
    prev = list(prev) if prev is not None else []
    return pl.pallas_call(
        lambda *refs: body(*refs[:4], *refs[4 + len(prev):]), name="adamw", grid=(C // tc,),
        in_specs=[pl.BlockSpec((P, R, tc), lambda i: (0, 0, i)), wspec, wspec, wspec] + [ANY] * len(prev),
        out_specs=[wspec] * 4, out_shape=[jax.ShapeDtypeStruct(w.shape, F32)] * 4,
        input_output_aliases={4 + i: i for i in range(len(prev))},
        compiler_params=_cp("parallel"),
    )(parts, w, m, v, *prev)


_WEIGHTS = ['ffn1_norm', 'ffn1_w_in', 'ffn1_w_out', 'mix_norm', 'ffn2_norm', 'ffn2_w_in', 'ffn2_w_out', 'ev_w_in',
            'ev_conv_w', 'ev_q_norm', 'ev_k_norm', 'ev_w_out', 'od_w_in', 'od_pool_w', 'od_pool_scale', 'od_sgu_norm',
            'od_sgu_w', 'od_sgu_b', 'od_w_out', 'final_norm']
_BIG = dict(ffn1_w_in=True, ffn1_w_out=False, ffn2_w_in=True, ffn2_w_out=False,
            ev_w_in=True, ev_w_out=False, od_w_in=True, od_w_out=False)
_SMALL_SHARDED = ['ev_conv_w', 'od_pool_scale', 'od_sgu_norm']
_SMALL = [n for n in _WEIGHTS if n not in _BIG]
_PACK_ROWS = 8 * LANES


_KINDS = ("f1_in_t", "f1_out", "mx_in_t", "mx_out", "f2_in_t", "f2_out")
_CARRIER_US = dict(ffn1_fwd=105, ffn2_fwd=105, attn_fwd=115, f2_bwd=135, f1_bwd=135, attn_bwd=205,
                   f2_out_grad=33, f1_out_grad=33, f2_in_grad=61, f1_in_grad=61)
_GATHER_US_PER_ROW, _SCATTER_US_PER_ROW, _SMALL_GATHER_US, _SLACK_US = 0.08, 0.176, 42, 10


def _schedule(L, rows):
    events = []
    for l in range(L):
        events += [("ffn1_fwd", l), ("mixer", l)] + ([("attn_fwd", l)] if l % 2 == 0 else []) + [("ffn2_fwd", l)]
    consumer = {"f1": "ffn1_fwd", "mx": "mixer", "f2": "ffn2_fwd"}
    queue = [(k, l) for l in range(L) for k in _KINDS]
    pos = {t: events.index((consumer[t[0][:2]], t[1])) for t in queue}
    gathers = {"first": [t for t in queue if pos[t] == 0]}
    queue = [t for t in queue if pos[t] > 0]
    carriers = [i for i, e in enumerate(events) if e[0] in _CARRIER_US]
    for i in carriers:
        budget, take = _CARRIER_US[events[i][0]], []
        later = [j for j in carriers if j > i]
        while queue:
            t = queue[0]
            cost = rows(*t) * _GATHER_US_PER_ROW
            forced = not any(j < pos[t] for j in later)
            if not forced and cost > budget:
                break
            take.append(queue.pop(0))
            budget -= cost
        if take:
            gathers[events[i]] = take
    assert not queue
    events = []
    for l in reversed(range(L)):
        events += [("f2_bwd", l), ("f2_out_grad", l), ("f2_in_grad", l), ("mx_out_ready", l)]
        events += [("attn_bwd", l)] if l % 2 == 0 else []
        events += [("mx_in_ready", l), ("f1_bwd", l), ("f1_out_grad", l), ("f1_in_grad", l)]
    made_by = {"f2_out": "f2_out_grad", "f2_in_t": "f2_in_grad", "mx_out": "mx_out_ready", "mx_in_t": "mx_in_ready",
               "f1_out": "f1_out_grad", "f1_in_t": "f1_in_grad"}
    small_carrier = ("f1_in_grad", 0)
    scatters, ready = {}, []
    for e in events:
        if e[0] in _CARRIER_US:
            budget, take = _CARRIER_US[e[0]] - (_SMALL_GATHER_US if e == small_carrier else 0), []
            while True:
                fits = [t for t in ready if rows(*t) * _SCATTER_US_PER_ROW <= budget + _SLACK_US]
                if not fits:
                    break
                t = max(fits, key=lambda u: rows(*u))
                budget -= rows(*t) * _SCATTER_US_PER_ROW
                ready.remove(t)
                take.append(t)
            if take:
                scatters[e] = take
        ready += [(k, e[1]) for k in _KINDS if made_by[k] == e[0]]
    scatters["last"] = ready
    return gathers, scatters, small_carrier


def _pack(arrs):
    flat = jnp.concatenate([a.reshape(-1) for a in arrs])
    pad = (-flat.shape[0]) % _PACK_ROWS
    return jnp.pad(flat, (0, pad)).reshape(-1, LANES)


def _unpack(buf, shapes):
    flat, out, off = buf.reshape(-1), [], 0
    for s in shapes:
        n = math.prod(s)
        out.append(flat[off:off + n].reshape(s))
        off += n
    return out


def _unshard_last(g, lead):
    nd = len(lead)
    return jnp.moveaxis(g, 0, nd).reshape(*lead, -1)


def kernel(x, ffn1_norm, ffn1_w_in, ffn1_w_out, mix_norm, ffn2_norm, ffn2_w_in, ffn2_w_out, ev_w_in, ev_conv_w, ev_q_norm, ev_k_norm, ev_w_out, od_w_in, od_pool_w, od_pool_scale, od_sgu_norm, od_sgu_w, od_sgu_b, od_w_out, final_norm, loss_target, m_ffn1_norm, m_ffn1_w_in, m_ffn1_w_out, m_mix_norm, m_ffn2_norm, m_ffn2_w_in, m_ffn2_w_out, m_ev_w_in, m_ev_conv_w, m_ev_q_norm, m_ev_k_norm, m_ev_w_out, m_od_w_in, m_od_pool_w, m_od_pool_scale, m_od_sgu_norm, m_od_sgu_w, m_od_sgu_b, m_od_w_out, m_final_norm, v_ffn1_norm, v_ffn1_w_in, v_ffn1_w_out, v_mix_norm, v_ffn2_norm, v_ffn2_w_in, v_ffn2_w_out, v_ev_w_in, v_ev_conv_w, v_ev_q_norm, v_ev_k_norm, v_ev_w_out, v_od_w_in, v_od_pool_w, v_od_pool_scale, v_od_sgu_norm, v_od_sgu_w, v_od_sgu_b, v_od_w_out, v_final_norm):
    w = dict(zip(_WEIGHTS, (ffn1_norm, ffn1_w_in, ffn1_w_out, mix_norm, ffn2_norm, ffn2_w_in, ffn2_w_out, ev_w_in, ev_conv_w, ev_q_norm, ev_k_norm, ev_w_out, od_w_in, od_pool_w, od_pool_scale, od_sgu_norm, od_sgu_w, od_sgu_b, od_w_out, final_norm)))
    m = dict(zip(_WEIGHTS, (m_ffn1_norm, m_ffn1_w_in, m_ffn1_w_out, m_mix_norm, m_ffn2_norm, m_ffn2_w_in, m_ffn2_w_out, m_ev_w_in, m_ev_conv_w, m_ev_q_norm, m_ev_k_norm, m_ev_w_out, m_od_w_in, m_od_pool_w, m_od_pool_scale, m_od_sgu_norm, m_od_sgu_w, m_od_sgu_b, m_od_w_out, m_final_norm)))
    v = dict(zip(_WEIGHTS, (v_ffn1_norm, v_ffn1_w_in, v_ffn1_w_out, v_mix_norm, v_ffn2_norm, v_ffn2_w_in, v_ffn2_w_out, v_ev_w_in, v_ev_conv_w, v_ev_q_norm, v_ev_k_norm, v_ev_w_out, v_od_w_in, v_od_pool_w, v_od_pool_scale, v_od_sgu_norm, v_od_sgu_w, v_od_sgu_b, v_od_w_out, v_final_norm)))
    n_ex, seq, D = x.shape
    T = n_ex * seq
    L = ffn1_norm.shape[0]
    me = 4 * lax.axis_index("x") + 2 * lax.axis_index("y") + lax.axis_index("c")

    sh_small = [w[n] for n in _SMALL_SHARDED]
    packed = all_gather([(_pack(sh_small), None)])[0].reshape(N_DEV, -1)
    full_small = {}
    off = 0
    for n, a in zip(_SMALL_SHARDED, sh_small):
        cnt = math.prod(a.shape)
        full_small[n] = _unshard_last(packed[:, off:off + cnt].reshape((N_DEV,) + a.shape), a.shape[:-1])
        off += cnt

    tr = lambda a: jnp.swapaxes(a, 1, 2)
    wmv = {n: tuple(tr(d[n]) if t else d[n] for d in (w, m, v)) for n, t in _BIG.items()}
    shards = {n: cast_shards(wmv[n][0]) for n in _BIG}

    def name_of(kind, l):
        mx = "ev" if l % 2 == 0 else "od"
        return {"f1_in_t": "ffn1_w_in", "f1_out": "ffn1_w_out", "f2_in_t": "ffn2_w_in", "f2_out": "ffn2_w_out",
                "mx_in_t": mx + "_w_in", "mx_out": mx + "_w_out"}[kind], (l // 2 if kind.startswith("mx") else l)

    def shard(kind, l):
        name, idx = name_of(kind, l)
        return shards[name], idx

    g_shapes = {}

    def pack_small(small, d_final):
        ev = [sm for l, sm in enumerate(small) if l % 2 == 0]
        od = [sm for l, sm in enumerate(small) if l % 2 == 1]
        st = lambda sms, k: jnp.stack([sm[k] for sm in sms])
        g_full = dict(ffn1_norm=st(small, "n1")[:, 0], mix_norm=st(small, "nm")[:, 0], ffn2_norm=st(small, "n2")[:, 0],
                      ev_conv_w=st(ev, "conv_w"), ev_q_norm=st(ev, "q_norm"), ev_k_norm=st(ev, "k_norm"),
                      od_pool_w=st(od, "pool_w"), od_pool_scale=st(od, "pool_scale"), od_sgu_norm=st(od, "sgu_norm"),
                      od_sgu_w=st(od, "sgu_w"), od_sgu_b=st(od, "sgu_b"), final_norm=d_final[0])
        g_shapes.update({n: g_full[n].shape for n in _SMALL})
        return _pack([g_full[n] for n in _SMALL])

    gathers, scatters, small_carrier = _schedule(L, lambda kind, l: shards[name_of(kind, l)[0]].shape[1])
    plan = Plan(shard, gathers, scatters, small_carrier, pack_small)
    layers = []
    for l in range(L):
        j = l // 2
        W = dict(n1=ffn1_norm[l][None], nm=mix_norm[l][None], n2=ffn2_norm[l][None])
        if l % 2 == 0:
            W.update(conv_w=full_small["ev_conv_w"][j], q_norm=ev_q_norm[j], k_norm=ev_k_norm[j])
        else:
            W.update(pool_w=od_pool_w[j], pool_scale=full_small["od_pool_scale"][j], sgu_norm=full_small["od_sgu_norm"][j],
                     sgu_w=od_sgu_w[j], sgu_b=od_sgu_b[j])
        layers.append(W)

    plan.alone("first", "gather_first")
    loss, dx = _local_step(x.reshape(T, D), loss_target.reshape(T, D), layers, final_norm[None], n_ex, plan)
    plan.alone("last", "scatter_last")

    out = {n: None for n in _BIG}
    for (kind, l), parts in plan.parts.items():
        name, idx = name_of(kind, l)
        out[name] = adamw(parts, *wmv[name], idx, prev=out[name])
    out = {n: [tr(a) if _BIG[n] else a for a in res] for n, res in out.items()}

    g8 = plan.small_parts.reshape(N_DEV, -1)
    cols, off = [], 0
    for n in _SMALL:
        cnt = math.prod(g_shapes[n])
        g = g8[:, off:off + cnt].reshape((N_DEV,) + g_shapes[n])
        off += cnt
        if n in _SMALL_SHARDED:
            width = w[n].shape[-1]
            g = lax.dynamic_slice_in_dim(g, me * width, width, axis=g.ndim - 1)
        cols.append(g.reshape(N_DEV, -1))
    g8 = jnp.concatenate(cols, axis=1)
    g8 = jnp.pad(g8, ((0, 0), (0, (-g8.shape[1]) % _PACK_ROWS))).reshape(N_DEV, -1, LANES)
    pk = lambda d: _pack([d[n] for n in _SMALL])[None]
    small_out = adamw(g8, pk(w), pk(m), pk(v), 0)
    shapes = [w[n].shape for n in _SMALL]
    for i in range(4):
        for n, a in zip(_SMALL, _unpack(small_out[i], shapes)):
            out.setdefault(n, [None] * 4)[i] = a

    total = lax.psum(loss[0, 0], ("x", "y", "c"))
    return (total, dx.reshape(n_ex, seq, D), *[out[n][0] for n in _WEIGHTS], *[out[n][1] for n in _WEIGHTS],
            *[out[n][2] for n in _WEIGHTS], *[out[n][3] for n in _WEIGHTS])
```

```python
import functools
import math

import jax
import jax.numpy as jnp
from jax import lax
from jax.experimental import pallas as pl
from jax.experimental.pallas import tpu as pltpu

F32, BF16 = jnp.float32, jnp.bfloat16
EPS = 1e-6
N_DEV = 8
V7X_VMEM_BYTES = 64 * 1024 * 1024
VMEM_LIMIT = V7X_VMEM_BYTES - 8 * 1024 * 1024
LANES = 128
HEAD_DIM = 64
N_Q_HEADS = 8
N_KV_HEADS = 2
Q_PER_KV = N_Q_HEADS // N_KV_HEADS
GRID_W = 64
ROPE_THETA = 10000.0
POOL_RADII = (1, 2, 4, 8)
SGU_CHUNK = 128
GROUP = 128
ADAM_LR, ADAM_B1, ADAM_B2, ADAM_EPS, ADAM_WD, ADAM_STEP = 0.001, 0.9, 0.999, 1e-08, 0.01, 10
MESH_ID = pl.DeviceIdType.MESH


def _cp(*sem):
    return pltpu.CompilerParams(dimension_semantics=sem, vmem_limit_bytes=VMEM_LIMIT)


def _dot(a, b, ca, cb):
    return lax.dot_general(a, b, (((ca,), (cb,)), ((), ())), preferred_element_type=F32)


def _nn(a, b):
    return _dot(a, b, 1, 0)


def _nt(a, b):
    return _dot(a, b, 1, 1)


def _tn(a, b):
    return _dot(a, b, 0, 0)


def _split_mm(x, m):
    hi = x.astype(BF16)
    lo = (x - hi.astype(F32)).astype(BF16)
    return _nn(hi, m) + _nn(lo, m)


def _tile(n, pref):
    t = min(n, pref)
    assert n % t == 0, (n, pref)
    return t


ANY = pl.BlockSpec(memory_space=pl.ANY)
OTHER_CHIPS = (2, 4, 6)
JOB_MIDDLE = 0.55


def _my_place():
    x, y, c = lax.axis_index("x"), lax.axis_index("y"), lax.axis_index("c")
    return x, y, c, 4 * x + 2 * y + c


def _peer(x, y, c, k):
    px = 1 - x if k & 4 else x
    py = 1 - y if k & 2 else y
    pc = 1 - c if k & 1 else c
    return (px, py, pc), 4 * px + 2 * py + pc


def _remote(src, dst, send_sems, recv_sems, i, peer):
    return pltpu.make_async_remote_copy(src_ref=src, dst_ref=dst, send_sem=send_sems.at[i], recv_sem=recv_sems.at[i],
                                        device_id=peer, device_id_type=MESH_ID)


class GatherJob:
    def __init__(self, srcs):
        self.srcs = srcs
        self.args = [a for a, _ in srcs]
        self.dims = [a.shape[-2:] for a, _ in srcs]
        n = self.n_in = self.n_out = len(srcs)
        self.out_shape = [jax.ShapeDtypeStruct((N_DEV * r, cc), a.dtype) for (a, _), (r, cc) in zip(srcs, self.dims)]
        self.scratch = [pltpu.SemaphoreType.DMA((N_DEV * n,)), pltpu.SemaphoreType.DMA((N_DEV * n,)),
                        pltpu.SemaphoreType.DMA((n,))]

    def _rows(self, outs, t, idx):
        r = self.dims[t][0]
        return outs[t].at[pl.ds(pl.multiple_of(idx * r, 8), r), :]

    def _local(self, ins, outs, loc, t, me):
        src = ins[t] if self.srcs[t][1] is None else ins[t].at[self.srcs[t][1]]
        return src, pltpu.make_async_copy(src, self._rows(outs, t, me), loc.at[t])

    def start(self, ins, outs, sems):
        send, recv, loc = sems
        x, y, c, me = _my_place()
        for t in range(self.n_in):
            src, local = self._local(ins, outs, loc, t, me)
            local.start()
            for k in (2, 4, 1):
                _remote(src, self._rows(outs, t, me), send, recv, N_DEV * t + k, _peer(x, y, c, k)[0]).start()

    def _copy(self, outs, sems, t, origin, i, to):
        x, y, c, _ = _my_place()
        blk = self._rows(outs, t, _peer(x, y, c, origin)[1])
        return _remote(blk, blk, sems[0], sems[1], N_DEV * t + i, _peer(x, y, c, to)[0])

    def middle(self, ins, outs, sems):
        c = _my_place()[2]

        def relay(t, got, to):
            self._copy(outs, sems, t, got, got, got).wait_recv()
            self._copy(outs, sems, t, got, 6, to).start()
            self._copy(outs, sems, t, to, to, to).wait_recv()

        for t in range(self.n_in):
            pl.when(c == 1)(functools.partial(relay, t, 2, 4))
            pl.when(c == 0)(functools.partial(relay, t, 4, 2))
            for k in (2, 4):
                self._copy(outs, sems, t, k, k + 1, 1).start()

    def finish(self, ins, outs, sems):
        send, recv, loc = sems
        x, y, c, me = _my_place()
        for t in range(self.n_in):
            self._copy(outs, sems, t, 6, 6, 6).wait_recv()
            self._copy(outs, sems, t, 6, 7, 1).start()
        for t in range(self.n_in):
            for k in range(1, N_DEV):
                peer, pidx = _peer(x, y, c, k)
                blk = self._rows(outs, t, pidx)
                if k % 2 == 1:
                    _remote(blk, blk, send, recv, N_DEV * t + k, peer).wait_recv()
                _remote(blk, blk, send, recv, N_DEV * t + k, peer).wait_send()
            self._local(ins, outs, loc, t, me)[1].wait()


class ScatterJob:
    def __init__(self, grads):
        self.args = list(grads)
        self.dims = [(g.shape[0] // N_DEV, g.shape[1]) for g in grads]
        n = self.n_in = self.n_out = len(grads)
        self.out_shape = [jax.ShapeDtypeStruct((N_DEV, r, cc), g.dtype) for g, (r, cc) in zip(grads, self.dims)]
        self.scratch = [pltpu.SemaphoreType.DMA((N_DEV * n,)), pltpu.SemaphoreType.DMA((N_DEV * n,)),
                        pltpu.SemaphoreType.DMA((n,))]

    def _rows(self, ins, t, idx):
        r = self.dims[t][0]
        return ins[t].at[pl.ds(pl.multiple_of(idx * r, 8), r), :]

    def start(self, ins, outs, sems):
        send, recv, loc = sems
        x, y, c, me = _my_place()
        for t in range(self.n_in):
            pltpu.make_async_copy(self._rows(ins, t, me), outs[t].at[me], loc.at[t]).start()
            for k in OTHER_CHIPS + (1, 3, 5, 7):
                peer, pidx = _peer(x, y, c, k)
                _remote(self._rows(ins, t, pidx), outs[t].at[me], send, recv, N_DEV * t + k, peer).start()

    def middle(self, ins, outs, sems):
        pass

    def finish(self, ins, outs, sems):
        send, recv, loc = sems
        x, y, c, me = _my_place()
        for t in range(self.n_in):
            for k in range(1, N_DEV):
                peer, pidx = _peer(x, y, c, k)
                cp = _remote(self._rows(ins, t, pidx), outs[t].at[pidx], send, recv, N_DEV * t + k, peer)
                cp.wait_recv()
                cp.wait_send()
            pltpu.make_async_copy(self._rows(ins, t, me), outs[t].at[me], loc.at[t]).wait()


def _call(body, name, grid, in_specs, args, out_specs, out_shape, scratch=(), sem=(), job=None, aliases=None):
    in_specs, out_specs, out_shape, scratch = list(in_specs), list(out_specs), list(out_shape), list(scratch)
    n_in, n_out, n_scr = len(args), len(out_shape), len(scratch)
    if job is None:
        res = pl.pallas_call(body, name=name, grid=grid, in_specs=in_specs, out_specs=out_specs, out_shape=out_shape,
                             scratch_shapes=scratch, input_output_aliases=aliases or {}, compiler_params=_cp(*sem))(*args)
        return res, None
    o0 = n_in + job.n_in
    s0 = o0 + n_out + job.n_out

    def carrier(*refs):
        jin, jout, jsem = refs[n_in:o0], refs[o0 + n_out:s0], refs[s0 + n_scr:]
        ids = [pl.program_id(a) for a in range(len(grid))]
        def at(step):
            idx = []
            for g in reversed(grid):
                idx.append(step % g)
                step //= g
            return functools.reduce(jnp.logical_and, [i == j for i, j in zip(ids, reversed(idx))])

        steps = math.prod(grid)
        if grid:
            pl.when(at(0))(lambda: job.start(jin, jout, jsem))
            pl.when(at(int(steps * JOB_MIDDLE)))(lambda: job.middle(jin, jout, jsem))
        else:
            job.start(jin, jout, jsem)
            job.middle(jin, jout, jsem)
        body(*refs[:n_in], *refs[o0:o0 + n_out], *refs[s0:s0 + n_scr])
        if grid:
            pl.when(at(steps - 1))(lambda: job.finish(jin, jout, jsem))
        else:
            job.finish(jin, jout, jsem)

    res = pl.pallas_call(
        carrier, name=name + "_comm", grid=grid, in_specs=in_specs + [ANY] * job.n_in,
        out_specs=out_specs + [ANY] * job.n_out, out_shape=out_shape + job.out_shape,
        scratch_shapes=scratch + job.scratch, input_output_aliases=aliases or {},
        compiler_params=_cp(*(["arbitrary"] * len(grid))))(*args, *job.args)
    return res[:n_out], res[n_out:]


def run_job(job, name):
    return _call(lambda: None, name, (), [], [], [], [], job=job)[1]


@jax.custom_vjp
def bmm(x, w):
    return _nn(x.astype(BF16), w.astype(BF16))


def _bmm_fwd(x, w):
    return bmm(x, w), (x, w)


def _bmm_bwd(res, g):
    x, w = res
    gb = g.astype(BF16)
    return _nt(gb, w.astype(BF16)), _tn(x.astype(BF16), gb)


bmm.defvjp(_bmm_fwd, _bmm_bwd)


def _shift_raw(x, d):
    n = x.shape[0]
    r = pltpu.roll(x, d % n, axis=0)
    row = lax.broadcasted_iota(jnp.int32, x.shape, 0)
    keep = (row >= d) if d > 0 else (row < n + d)
    return jnp.where(keep, r, 0.0)


def shift_rows(x, d):
    @jax.custom_vjp
    def f(v):
        return _shift_raw(v, d)

    f.defvjp(lambda v: (_shift_raw(v, d), None), lambda _, g: (_shift_raw(g, -d),))
    return f(x)


def _swap_raw(x):
    n = x.shape[1]
    nxt = pltpu.roll(x, n - 1, axis=1)
    prv = pltpu.roll(x, 1, axis=1)
    lane = lax.broadcasted_iota(jnp.int32, x.shape, 1)
    return jnp.where(lane % 2 == 0, nxt, prv)


@jax.custom_vjp
def swap_pairs(x):
    return _swap_raw(x)


swap_pairs.defvjp(lambda x: (_swap_raw(x), None), lambda _, g: (_swap_raw(g),))


@jax.custom_vjp
def group_mean(x, bd):
    return _split_mm(x, bd)


group_mean.defvjp(lambda x, bd: (_split_mm(x, bd), bd), lambda bd, g: (_split_mm(g, bd), jnp.zeros_like(bd)))


def _rope_norm(x, gain, cos, sgn, bd, scale):
    xn = x * lax.rsqrt(group_mean(x * x, bd) + EPS) * gain
    return (xn * cos + swap_pairs(xn) * sgn) * scale


def _conv_gate(gb, gc, hc, w):
    z = gc * hc
    c = shift_rows(z, 1) * w[0:1] + z * w[1:2] + shift_rows(z, -1) * w[2:3]
    return gb * c


def _window_sum(p, r):
    b = f = p
    k = 1
    while k < r:
        b = b + shift_rows(b, k)
        f = f + shift_rows(f, -k)
        k *= 2
    return b + f - p + shift_rows(p, r) + shift_rows(p, -r)


def _pool_mix(p, pool_w, scale):
    n = p.shape[0]
    t = lax.broadcasted_iota(jnp.int32, (n, 1), 0)
    outs = []
    for gi, r in enumerate(POOL_RADII):
        pg = p[:, gi * GROUP:(gi + 1) * GROUP]
        cnt = (jnp.minimum(t + r, n - 1) - jnp.maximum(t - r, 0) + 1).astype(F32)
        pooled = _window_sum(pg, r) / cnt - pg
        outs.append(bmm(pooled, pool_w[gi]))
    return jnp.concatenate(outs, axis=1) * scale


def _sgu(u, v, norm_g, w_s, b_full):
    ug = jax.nn.gelu(u)
    vg = jax.nn.gelu(v)
    vn = vg * lax.rsqrt(jnp.mean(vg * vg, axis=-1, keepdims=True) + EPS) * norm_g
    cols = []
    for g in range(w_s.shape[0]):
        rows = []
        for n in range(u.shape[0] // SGU_CHUNK):
            blk = vn[n * SGU_CHUNK:(n + 1) * SGU_CHUNK, g * GROUP:(g + 1) * GROUP]
            rows.append(bmm(w_s[g], blk) + b_full[g])
        cols.append(jnp.concatenate(rows, axis=0))
    return ug * jnp.concatenate(cols, axis=1)


def _rms_bwd_math(xv, gain, dy, dres):
    r = lax.rsqrt(jnp.mean(xv * xv, axis=-1, keepdims=True) + EPS)
    xh = xv * r
    dxh = dy * gain
    dx = dres + r * (dxh - xh * jnp.mean(dxh * xh, axis=-1, keepdims=True))
    return dx, jnp.sum(dy * xh, axis=0, keepdims=True)


FFN_TN = 256


def ffn_fwd(x, gain, wt_in, w_out, job=None):
    T, D = x.shape
    F = w_out.shape[0]
    tm, tn = _tile(T, 1024), FFN_TN
    nc = F // tn

    def body(x_ref, gn_ref, wg_ref, wu_ref, wo_ref, y_ref, gu_ref, xn_s, acc_s):
        c = pl.program_id(1)

        @pl.when(c == 0)
        def _():
            xv = x_ref[...]
            r = lax.rsqrt(jnp.mean(xv * xv, axis=-1, keepdims=True) + EPS)
            xn_s[...] = (xv * r * gn_ref[...]).astype(BF16)
            acc_s[...] = jnp.zeros_like(acc_s)

        xn = xn_s[...]
        g = _nt(xn, wg_ref[...])
        u = _nt(xn, wu_ref[...])
        gu_ref[:, :tn] = g.astype(BF16)
        gu_ref[:, tn:] = u.astype(BF16)
        a = (g * jax.nn.sigmoid(g) * u).astype(BF16)
        acc_s[...] += _nn(a, wo_ref[...])

        @pl.when(c == nc - 1)
        def _():
            y_ref[...] = x_ref[...] + 0.5 * acc_s[...]

    row = pl.BlockSpec((tm, D), lambda i, c: (i, 0))
    return _call(
        body, "ffn_fwd", (T // tm, nc),
        [row, pl.BlockSpec((1, D), lambda i, c: (0, 0)),
         pl.BlockSpec((tn, D), lambda i, c: (c, 0)),
         pl.BlockSpec((tn, D), lambda i, c: (c + nc, 0)),
         pl.BlockSpec((tn, D), lambda i, c: (c, 0))],
        [x, gain, wt_in, wt_in, w_out],
        [row, pl.BlockSpec((tm, 2 * tn), lambda i, c: (i, c))],
        [jax.ShapeDtypeStruct((T, D), F32), jax.ShapeDtypeStruct((T, 2 * F), BF16)],
        [pltpu.VMEM((tm, D), BF16), pltpu.VMEM((tm, D), F32)], ("parallel", "arbitrary"), job)


def ffn_bwd_x(dout, x, gain, gu, wt_in, w_out, job=None):
    T, D = x.shape
    F = w_out.shape[0]
    tm, tn = _tile(T, 1024), FFN_TN
    nc = F // tn

    def body(do_ref, x_ref, gn_ref, gu_ref, wg_ref, wu_ref, wo_ref,
             dx_ref, dgn_ref, a_ref, dgu_ref, xn_ref, dob_ref, acc_s):
        i, c = pl.program_id(0), pl.program_id(1)

        @pl.when(c == 0)
        def _():
            xv = x_ref[...]
            r = lax.rsqrt(jnp.mean(xv * xv, axis=-1, keepdims=True) + EPS)
            xn_ref[...] = (xv * r * gn_ref[...]).astype(BF16)
            dob_ref[...] = (0.5 * do_ref[...]).astype(BF16)
            acc_s[...] = jnp.zeros_like(acc_s)

        da = jnp.concatenate([_nt(dob_ref[:tm // 2, :], wo_ref[...]), _nt(dob_ref[tm // 2:, :], wo_ref[...])], axis=0)
        g = gu_ref[:, :tn].astype(F32)
        u = gu_ref[:, tn:].astype(F32)
        sig = jax.nn.sigmoid(g)
        sl = g * sig
        a_ref[...] = (sl * u).astype(BF16)
        dg = (da * u * (sig * (1.0 + g * (1.0 - sig)))).astype(BF16)
        du = (da * sl).astype(BF16)
        dgu_ref[:, :tn] = dg
        dgu_ref[:, tn:] = du
        acc_s[...] += _nn(dg, wg_ref[...]) + _nn(du, wu_ref[...])

        @pl.when(c == nc - 1)
        def _():
            dx, dgn = _rms_bwd_math(x_ref[...], gn_ref[...], acc_s[...], do_ref[...])
            dx_ref[...] = dx

            @pl.when(i == 0)
            def _():
                dgn_ref[...] = jnp.zeros_like(dgn_ref)

            dgn_ref[...] += dgn

    row = pl.BlockSpec((tm, D), lambda i, c: (i, 0))
    return _call(
        body, "ffn_bwd_x", (T // tm, nc),
        [row, row, pl.BlockSpec((1, D), lambda i, c: (0, 0)),
         pl.BlockSpec((tm, 2 * tn), lambda i, c: (i, c)),
         pl.BlockSpec((tn, D), lambda i, c: (c, 0)),
         pl.BlockSpec((tn, D), lambda i, c: (c + nc, 0)),
         pl.BlockSpec((tn, D), lambda i, c: (c, 0))],
        [dout, x, gain, gu, wt_in, wt_in, w_out],
        [row, pl.BlockSpec((1, D), lambda i, c: (0, 0)),
         pl.BlockSpec((tm, tn), lambda i, c: (i, c)),
         pl.BlockSpec((tm, 2 * tn), lambda i, c: (i, c)), row, row],
        [jax.ShapeDtypeStruct((T, D), F32), jax.ShapeDtypeStruct((1, D), F32),
         jax.ShapeDtypeStruct((T, F), BF16), jax.ShapeDtypeStruct((T, 2 * F), BF16),
         jax.ShapeDtypeStruct((T, D), BF16), jax.ShapeDtypeStruct((T, D), BF16)],
        [pltpu.VMEM((tm, D), F32)], ("arbitrary", "arbitrary"), job)


MM_TM = 512
MM_TC = 256


def mm_nt(a, wt, pieces, out_dtype, emit_a_bf16=False, norm_gain=None):
    T, K = a.shape
    tm = _tile(T, MM_TM)
    npc = len(pieces)
    n_lead = 1 if norm_gain is None else 2

    def body(*refs):
        a_ref, w_refs, o_refs = refs[0], refs[n_lead:n_lead + npc], refs[n_lead + npc:]
        av = a_ref[...]
        if norm_gain is not None:
            av = av * lax.rsqrt(jnp.mean(av * av, axis=-1, keepdims=True) + EPS) * refs[1][...]
        ab = av.astype(BF16)
        for w_ref, o_ref in zip(w_refs, o_refs[:npc]):
            o_ref[...] = _nt(ab, w_ref[...]).astype(o_ref.dtype)
        if emit_a_bf16:
            o_refs[npc][...] = ab

    in_specs = [pl.BlockSpec((tm, K), lambda i: (i, 0))]
    if norm_gain is not None:
        in_specs.append(pl.BlockSpec((1, K), lambda i: (0, 0)))
    out_specs, out_shape = [], []
    for r0, n in pieces:
        assert r0 % n == 0
        in_specs.append(pl.BlockSpec((n, K), functools.partial(lambda i, b: (b, 0), b=r0 // n)))
        out_specs.append(pl.BlockSpec((tm, n), lambda i: (i, 0)))
        out_shape.append(jax.ShapeDtypeStruct((T, n), out_dtype))
    if emit_a_bf16:
        out_specs.append(pl.BlockSpec((tm, K), lambda i: (i, 0)))
        out_shape.append(jax.ShapeDtypeStruct((T, K), BF16))
    return pl.pallas_call(
        body, name="mm_nt", grid=(T // tm,), in_specs=in_specs, out_specs=out_specs, out_shape=out_shape,
        compiler_params=_cp("parallel"),
    )(a, *([] if norm_gain is None else [norm_gain]), *([wt] * npc))


def mm_nn(pieces, w, residual=None, norm_bwd=None):
    T = pieces[0][0].shape[0]
    N = w.shape[1]
    tm = _tile(T, MM_TM)
    na = len(pieces)

    def body(*refs):
        a_refs, w_refs = refs[:na], refs[na:2 * na]
        acc = refs[2 * na][...] if residual is not None else None
        for a_ref, w_ref in zip(a_refs, w_refs):
            t = _nn(a_ref[...].astype(BF16), w_ref[...])
            acc = t if acc is None else acc + t
        if norm_bwd is None:
            refs[-1][...] = acc
            return
        x_ref, g_ref, dr_ref, dx_ref, dg_ref = refs[-5:]
        dx, dg = _rms_bwd_math(x_ref[...], g_ref[...], acc, dr_ref[...])
        dx_ref[...] = dx

        @pl.when(pl.program_id(0) == 0)
        def _():
            dg_ref[...] = jnp.zeros_like(dg_ref)

        dg_ref[...] += dg

    in_specs, w_specs = [], []
    for a, cb, k, rb in pieces:
        in_specs.append(pl.BlockSpec((tm, k), functools.partial(lambda i, b: (i, b), b=cb)))
        w_specs.append(pl.BlockSpec((k, N), functools.partial(lambda i, b: (b, 0), b=rb)))
    assert sum(k for _, _, k, _ in pieces) == w.shape[0]
    args = [a for a, _, _, _ in pieces] + [w] * na
    in_specs = in_specs + w_specs
    row = pl.BlockSpec((tm, N), lambda i: (i, 0))
    if residual is not None:
        in_specs.append(row)
        args.append(residual)
    if norm_bwd is None:
        return pl.pallas_call(
            body, name="mm_nn", grid=(T // tm,), in_specs=in_specs, out_specs=row,
            out_shape=jax.ShapeDtypeStruct((T, N), F32), compiler_params=_cp("parallel"),
        )(*args)
    vec = pl.BlockSpec((1, N), lambda i: (0, 0))
    return pl.pallas_call(
        body, name="mm_nn_norm_bwd", grid=(T // tm,), in_specs=in_specs + [row, vec, row], out_specs=[row, vec],
        out_shape=[jax.ShapeDtypeStruct((T, N), F32), jax.ShapeDtypeStruct((1, N), F32)],
        compiler_params=_cp("arbitrary"),
    )(*args, *norm_bwd)


def mm_tn(a, b, n_rows, row_block, prev=None, grid=None, col_block=None, job=None):
    T, M = a.shape
    N = b.shape[1]
    tc = MM_TC
    assert M % tc == 0 and n_rows % tc == 0
    if grid is None:
        grid, col_block = (M // tc,), (lambda j: j)

    def body(*refs):
        a_ref, b_ref, o_ref = refs[0], refs[1], refs[-1]
        o_ref[...] = _tn(a_ref[...], b_ref[...]).astype(BF16)

    in_specs = [pl.BlockSpec((T, tc), lambda *g: (0, col_block(*g))), pl.BlockSpec((T, N), lambda *g: (0, 0))]
    args = [a, b]
    aliases = {}
    if prev is not None:
        in_specs.append(pl.BlockSpec(memory_space=pl.ANY))
        args.append(prev)
        aliases = {2: 0}
    res, jres = _call(body, "mm_tn", grid, in_specs, args, [pl.BlockSpec((tc, N), lambda *g: (row_block(*g), 0))],
                      [jax.ShapeDtypeStruct((n_rows, N), BF16)], (), ["parallel"] * len(grid), job, aliases)
    return res[0] if job is None else (res[0], jres)


def conv_fwd(proj_a, conv_w, n_ex):
    T, C3 = proj_a.shape
    C = C3 // 3
    S = T // n_ex

    def body(gb_ref, gc_ref, hc_ref, w_ref, o_ref):
        o_ref[...] = _conv_gate(gb_ref[...], gc_ref[...], hc_ref[...], w_ref[...]).astype(BF16)

    col = lambda k: pl.BlockSpec((S, C), functools.partial(lambda b, kk: (b, kk), kk=k))
    return pl.pallas_call(
        body, name="conv_fwd", grid=(n_ex,),
        in_specs=[col(0), col(1), col(2), pl.BlockSpec((3, C), lambda b: (0, 0))],
        out_specs=pl.BlockSpec((S, C), lambda b: (b, 0)),
        out_shape=jax.ShapeDtypeStruct((T, 2 * C), BF16), compiler_params=_cp("parallel"),
    )(proj_a, proj_a, proj_a, conv_w)


def conv_bwd(proj_a, conv_w, n_ex, dy, total_cols):
    T, C3 = proj_a.shape
    C = C3 // 3
    S = T // n_ex

    def body(gb_ref, gc_ref, hc_ref, w_ref, dy_ref, dp_ref, dw_ref):
        _, vjp = jax.vjp(_conv_gate, gb_ref[...], gc_ref[...], hc_ref[...], w_ref[...])
        dgb, dgc, dhc, dw = vjp(dy_ref[...].astype(F32))
        dp_ref[:, 0:C] = dgb.astype(BF16)
        dp_ref[:, C:2 * C] = dgc.astype(BF16)
        dp_ref[:, 2 * C:] = dhc.astype(BF16)

        @pl.when(pl.program_id(0) == 0)
        def _():
            dw_ref[...] = jnp.zeros_like(dw_ref)

        dw_ref[...] += dw

    col = lambda k: pl.BlockSpec((S, C), functools.partial(lambda b, kk: (b, kk), kk=k))
    return pl.pallas_call(
        body, name="conv_bwd", grid=(n_ex,),
        in_specs=[col(0), col(1), col(2), pl.BlockSpec((3, C), lambda b: (0, 0)),
                  pl.BlockSpec((S, C), lambda b: (b, 0))],
        out_specs=[pl.BlockSpec((S, C3), lambda b: (b, 0)), pl.BlockSpec((3, C), lambda b: (0, 0))],
        out_shape=[jax.ShapeDtypeStruct((T, total_cols), BF16), jax.ShapeDtypeStruct((3, C), F32)],
        compiler_params=_cp("arbitrary"),
    )(proj_a, proj_a, proj_a, conv_w, dy)


QW = N_Q_HEADS * HEAD_DIM
KW = N_KV_HEADS * HEAD_DIM
QP = N_Q_HEADS * LANES
KP = N_KV_HEADS * LANES


def _attn_consts(seq):
    rows = seq // GRID_W
    r_idx, c_idx = jnp.meshgrid(jnp.arange(rows), jnp.arange(GRID_W), indexing='ij')
    r_idx = r_idx.reshape(-1).astype(F32)
    c_idx = c_idx.reshape(-1).astype(F32)
    n_freq = HEAD_DIM // 4
    inv = ROPE_THETA ** (-jnp.arange(n_freq, dtype=F32) / n_freq)
    ang = jnp.concatenate([r_idx[:, None] * inv, c_idx[:, None] * inv], axis=-1)
    cos = jnp.repeat(jnp.cos(ang), 2, axis=1)
    sin = jnp.repeat(jnp.sin(ang), 2, axis=1)
    sgn = sin * jnp.tile(jnp.array([-1.0, 1.0], F32), HEAD_DIM // 2)
    cos = jnp.tile(cos, (1, N_Q_HEADS))
    sgn = jnp.tile(sgn, (1, N_Q_HEADS))
    lane = jnp.arange(QW)
    bd = jnp.where(lane[:, None] // HEAD_DIM == lane[None, :] // HEAD_DIM, 1.0 / HEAD_DIM, 0.0).astype(BF16)
    dst = (lane // HEAD_DIM) * LANES + lane % HEAD_DIM
    spread = (dst[:, None] == jnp.arange(QP)[None, :]).astype(BF16)
    return dict(cos=cos, sgn=sgn, bd=bd, spread=spread, gather=spread.T)


def qkv_prep_fwd(proj_b, qg, kg, cst, n_ex):
    T = proj_b.shape[0]
    S = T // n_ex
    tm = _tile(S, 512)
    nb = S // tm

    def body(p_ref, qg_ref, kg_ref, cos_ref, sgn_ref, bd_ref, sp_ref, q_ref, k_ref, v_ref):
        pv = p_ref[...]
        cos, sgn, bd, sp = cos_ref[...], sgn_ref[...], bd_ref[...], sp_ref[...]
        qr = _rope_norm(pv[:, :QW], qg_ref[...], cos, sgn, bd, HEAD_DIM ** -0.5)
        kr = _rope_norm(pv[:, QW:QW + KW], kg_ref[...], cos[:, :KW], sgn[:, :KW], bd[:KW, :KW], 1.0)
        q_ref[...] = _nn(qr.astype(BF16), sp).astype(BF16)
        k_ref[...] = _nn(kr.astype(BF16), sp[:KW, :KP]).astype(BF16)
        v_ref[...] = _nn(pv[:, QW + KW:].astype(BF16), sp[:KW, :KP]).astype(BF16)

    full = lambda a: pl.BlockSpec(a.shape, lambda i: (0,) * a.ndim)
    tab = pl.BlockSpec((tm, QW), lambda i: (i % nb, 0))
    return pl.pallas_call(
        body, name="qkv_prep_fwd", grid=(T // tm,),
        in_specs=[pl.BlockSpec((tm, QW + 2 * KW), lambda i: (i, 0)), full(qg), full(kg), tab, tab,
                  full(cst["bd"]), full(cst["spread"])],
        out_specs=[pl.BlockSpec((tm, QP), lambda i: (i, 0)), pl.BlockSpec((tm, KP), lambda i: (i, 0)),
                   pl.BlockSpec((tm, KP), lambda i: (i, 0))],
        out_shape=[jax.ShapeDtypeStruct((T, QP), BF16), jax.ShapeDtypeStruct((T, KP), BF16),
                   jax.ShapeDtypeStruct((T, KP), BF16)],
        compiler_params=_cp("parallel"),
    )(proj_b, qg, kg, cst["cos"], cst["sgn"], cst["bd"], cst["spread"])


def qkv_prep_bwd(proj_b, qg, kg, cst, n_ex, dq, dk_pad, dv_pad, d_proj):
    T = proj_b.shape[0]
    S = T // n_ex
    tm = _tile(S, 512)
    nb = S // tm

    def body(p_ref, qg_ref, kg_ref, cos_ref, sgn_ref, bd_ref, ga_ref, dq_ref, dk_ref, dv_ref, _kept,
             dp_ref, dqg_ref, dkg_ref):
        pv = p_ref[...]
        cos, sgn, bd, ga = cos_ref[...], sgn_ref[...], bd_ref[...], ga_ref[...]
        fq = lambda q, g: _rope_norm(q, g, cos, sgn, bd, HEAD_DIM ** -0.5)
        fk = lambda k, g: _rope_norm(k, g, cos[:, :KW], sgn[:, :KW], bd[:KW, :KW], 1.0)
        _, vq = jax.vjp(fq, pv[:, :QW], qg_ref[...])
        _, vk = jax.vjp(fk, pv[:, QW:QW + KW], kg_ref[...])
        dqp, dqg = vq(dq_ref[...])
        dkp, dkg = vk(_split_mm(dk_ref[...], ga[:KP, :KW]))
        dp_ref[:, :QW] = dqp.astype(BF16)
        dp_ref[:, QW:QW + KW] = dkp.astype(BF16)
        dp_ref[:, QW + KW:] = _split_mm(dv_ref[...], ga[:KP, :KW]).astype(BF16)

        @pl.when(pl.program_id(0) == 0)
        def _():
            dqg_ref[...] = jnp.zeros_like(dqg_ref)
            dkg_ref[...] = jnp.zeros_like(dkg_ref)

        dqg_ref[...] += dqg
        dkg_ref[...] += dkg

    full = lambda a: pl.BlockSpec(a.shape, lambda i: (0,) * a.ndim)
    tab = pl.BlockSpec((tm, QW), lambda i: (i % nb, 0))
    row = lambda n: pl.BlockSpec((tm, n), lambda i: (i, 0))
    wb = QW + 2 * KW
    assert d_proj.shape[1] % wb == 0
    last = d_proj.shape[1] // wb - 1
    return pl.pallas_call(
        body, name="qkv_prep_bwd", grid=(T // tm,),
        in_specs=[row(wb), full(qg), full(kg), tab, tab, full(cst["bd"]), full(cst["gather"]),
                  row(QW), row(KP), row(KP), ANY],
        out_specs=[pl.BlockSpec((tm, wb), lambda i: (i, last)), pl.BlockSpec((1, QW), lambda i: (0, 0)),
                   pl.BlockSpec((1, KW), lambda i: (0, 0))],
        out_shape=[jax.ShapeDtypeStruct(d_proj.shape, BF16), jax.ShapeDtypeStruct((1, QW), F32),
                   jax.ShapeDtypeStruct((1, KW), F32)],
        input_output_aliases={10: 0}, compiler_params=_cp("arbitrary"),
    )(proj_b, qg, kg, cst["cos"], cst["sgn"], cst["bd"], cst["gather"], dq, dk_pad, dv_pad, d_proj)


ATT_TQ = 256


def attn_fwd(qp, kp, vp, gather, n_ex, mix, job=None):
    T = qp.shape[0]
    S = T // n_ex
    tq = _tile(S, ATT_TQ)
    nq = S // tq

    def body(q_ref, k_ref, v_ref, ga_ref, _kept, o_ref, op_ref, lse_ref):
        lane = lax.broadcasted_iota(jnp.int32, (tq, LANES), 1)
        lse_all = jnp.zeros((tq, LANES), F32)
        for h in range(N_Q_HEADS):
            kv = h // Q_PER_KV
            qh = q_ref[:, h * LANES:(h + 1) * LANES]
            s = _nt(qh, k_ref[:, kv * LANES:(kv + 1) * LANES])
            m = jnp.max(s, axis=-1, keepdims=True)
            p = jnp.exp(s - m)
            lsum = jnp.sum(p, axis=-1, keepdims=True)
            o = _nn(p.astype(BF16), v_ref[:, kv * LANES:(kv + 1) * LANES]) / lsum
            op_ref[:, h * LANES:(h + 1) * LANES] = o.astype(BF16)
            lse_all = jnp.where(lane == h, m + jnp.log(lsum), lse_all)
        lse_ref[...] = lse_all
        o_ref[...] = _nn(op_ref[...], ga_ref[...]).astype(BF16)

    blk = lambda n: pl.BlockSpec((tq, n), lambda b, i: (b * nq + i, 0))
    kvs = pl.BlockSpec((S, KP), lambda b, i: (b, 0))
    return _call(
        body, "attn_fwd", (n_ex, nq),
        [blk(QP), kvs, kvs, pl.BlockSpec(gather.shape, lambda b, i: (0, 0)), ANY], [qp, kp, vp, gather, mix],
        [pl.BlockSpec((tq, QW), lambda b, i: (b * nq + i, 1)), blk(QP), blk(LANES)],
        [jax.ShapeDtypeStruct(mix.shape, BF16), jax.ShapeDtypeStruct((T, QP), BF16),
         jax.ShapeDtypeStruct((T, LANES), F32)], (), ("parallel", "parallel"), job, {4: 0})


def attn_bwd(qp, kp, vp, op, lse, do, cst, n_ex, job=None):
    T = qp.shape[0]
    S = T // n_ex
    tq = _tile(S, ATT_TQ)
    nq = S // tq

    def body(q_ref, k_ref, v_ref, op_ref, lse_ref, do_ref, sp_ref, ga_ref, dq_ref, dk_ref, dv_ref, dqp_s):
        @pl.when(pl.program_id(1) == 0)
        def _():
            dk_ref[...] = jnp.zeros_like(dk_ref)
            dv_ref[...] = jnp.zeros_like(dv_ref)

        lane = lax.broadcasted_iota(jnp.int32, (tq, LANES), 1)
        dop = _nn(do_ref[...], sp_ref[...]).astype(BF16)
        lse_all = lse_ref[...]
        for h in range(N_Q_HEADS):
            kv = h // Q_PER_KV
            hs = slice(h * LANES, (h + 1) * LANES)
            ks = slice(kv * LANES, (kv + 1) * LANES)
            qh, kk, vv = q_ref[:, hs], k_ref[:, ks], v_ref[:, ks]
            doh = dop[:, hs]
            lse_h = jnp.sum(jnp.where(lane == h, lse_all, 0.0), axis=-1, keepdims=True)
            p = jnp.exp(_nt(qh, kk) - lse_h)
            dp = _nt(doh, vv)
            delta = jnp.sum(doh.astype(F32) * op_ref[:, hs].astype(F32), axis=-1, keepdims=True)
            ds = (p * (dp - delta)).astype(BF16)
            dqp_s[:, hs] = _nn(ds, kk)
            dk_ref[:, ks] += _tn(ds, qh)
            dv_ref[:, ks] += _tn(p.astype(BF16), doh)
        dq_ref[...] = _split_mm(dqp_s[...], ga_ref[...])

    blk = lambda n: pl.BlockSpec((tq, n), lambda b, i: (b * nq + i, 0))
    kvs = pl.BlockSpec((S, KP), lambda b, i: (b, 0))
    full = lambda a: pl.BlockSpec(a.shape, lambda b, i: (0, 0))
    return _call(
        body, "attn_bwd", (n_ex, nq),
        [blk(QP), kvs, kvs, blk(QP), blk(LANES), pl.BlockSpec((tq, QW), lambda b, i: (b * nq + i, 1)),
         full(cst["spread"]), full(cst["gather"])],
        [qp, kp, vp, op, lse, do, cst["spread"], cst["gather"]],
        [blk(QW), kvs, kvs],
        [jax.ShapeDtypeStruct((T, QW), F32), jax.ShapeDtypeStruct((T, KP), F32), jax.ShapeDtypeStruct((T, KP), F32)],
        [pltpu.VMEM((tq, QP), F32)], ("arbitrary", "arbitrary"), job)


def pool_fwd(p, pool_w, scale, n_ex):
    T, W = p.shape
    S = T // n_ex

    def body(p_ref, w_ref, s_ref, o_ref):
        o_ref[...] = _pool_mix(p_ref[...], w_ref[...], s_ref[...]).astype(BF16)

    return pl.pallas_call(
        body, name="pool_fwd", grid=(n_ex,),
        in_specs=[pl.BlockSpec((S, W), lambda b: (b, 0)),
                  pl.BlockSpec(pool_w.shape, lambda b: (0, 0, 0)),
                  pl.BlockSpec((1, W), lambda b: (0, 0))],
        out_specs=pl.BlockSpec((S, W), lambda b: (b, 0)),
        out_shape=jax.ShapeDtypeStruct((T, 2 * W), BF16), compiler_params=_cp("parallel"),
    )(p, pool_w, scale)


def pool_bwd(p, pool_w, scale, n_ex, dy, d_proj):
    T, W = p.shape
    S = T // n_ex
    last = d_proj.shape[1] // W - 1

    def body(p_ref, w_ref, s_ref, dy_ref, _kept, dp_ref, dw_ref, ds_ref):
        _, vjp = jax.vjp(_pool_mix, p_ref[...], w_ref[...], s_ref[...])
        dp, dw, ds = vjp(dy_ref[...].astype(F32))
        dp_ref[...] = dp.astype(BF16)

        @pl.when(pl.program_id(0) == 0)
        def _():
            dw_ref[...] = jnp.zeros_like(dw_ref)
            ds_ref[...] = jnp.zeros_like(ds_ref)

        dw_ref[...] += dw
        ds_ref[...] += ds

    wshape = pool_w.shape
    return pl.pallas_call(
        body, name="pool_bwd", grid=(n_ex,),
        in_specs=[pl.BlockSpec((S, W), lambda b: (b, 0)),
                  pl.BlockSpec(wshape, lambda b: (0, 0, 0)),
                  pl.BlockSpec((1, W), lambda b: (0, 0)), pl.BlockSpec((S, W), lambda b: (b, 0)), ANY],
        out_specs=[pl.BlockSpec((S, W), lambda b: (b, last)), pl.BlockSpec(wshape, lambda b: (0, 0, 0)),
                   pl.BlockSpec((1, W), lambda b: (0, 0))],
        out_shape=[jax.ShapeDtypeStruct(d_proj.shape, BF16), jax.ShapeDtypeStruct(wshape, F32),
                   jax.ShapeDtypeStruct((1, W), F32)],
        input_output_aliases={4: 0}, compiler_params=_cp("arbitrary"),
    )(p, pool_w, scale, dy, d_proj)


SGU_TS = 512


def sgu_fwd(u, v, norm_g, w_s, b_full, mix):
    T, W = u.shape
    ts = _tile(T, SGU_TS)

    def body(u_ref, v_ref, g_ref, w_ref, b_ref, _kept, o_ref):
        o_ref[...] = _sgu(u_ref[...], v_ref[...], g_ref[...], w_ref[...], b_ref[...]).astype(BF16)

    row = pl.BlockSpec((ts, W), lambda i: (i, 0))
    wsp = pl.BlockSpec(w_s.shape, lambda i: (0, 0, 0))
    return pl.pallas_call(
        body, name="sgu_fwd", grid=(T // ts,),
        in_specs=[row, row, pl.BlockSpec((1, W), lambda i: (0, 0)), wsp, wsp, ANY],
        out_specs=pl.BlockSpec((ts, W), lambda i: (i, 1)), out_shape=jax.ShapeDtypeStruct(mix.shape, BF16),
        input_output_aliases={5: 0}, compiler_params=_cp("parallel"),
    )(u, v, norm_g, w_s, b_full, mix)


def sgu_bwd(u, v, norm_g, w_s, b_full, dy):
    T, W = u.shape
    ts = _tile(T, SGU_TS)
    wshape = w_s.shape

    def body(u_ref, v_ref, g_ref, w_ref, b_ref, dy_ref, duv_ref, dg_ref, dw_ref, db_ref):
        _, vjp = jax.vjp(_sgu, u_ref[...], v_ref[...], g_ref[...], w_ref[...], b_ref[...])
        du, dv, dg, dw, db = vjp(dy_ref[...].astype(F32))
        duv_ref[:, :W] = du.astype(BF16)
        duv_ref[:, W:] = dv.astype(BF16)

        @pl.when(pl.program_id(0) == 0)
        def _():
            dg_ref[...] = jnp.zeros_like(dg_ref)
            dw_ref[...] = jnp.zeros_like(dw_ref)
            db_ref[...] = jnp.zeros_like(db_ref)

        dg_ref[...] += dg
        dw_ref[...] += dw
        db_ref[...] += db

    row = pl.BlockSpec((ts, W), lambda i: (i, 0))
    wsp = pl.BlockSpec(wshape, lambda i: (0, 0, 0))
    wout = pl.BlockSpec(wshape, lambda i: (0, 0, 0))
    vec = pl.BlockSpec((1, W), lambda i: (0, 0))
    return pl.pallas_call(
        body, name="sgu_bwd", grid=(T // ts,),
        in_specs=[row, row, vec, wsp, wsp, pl.BlockSpec((ts, W), lambda i: (i, 1))],
        out_specs=[pl.BlockSpec((ts, 2 * W), lambda i: (i, 0)), vec, wout, wout],
        out_shape=[jax.ShapeDtypeStruct((T, 3 * W), BF16),
                   jax.ShapeDtypeStruct((1, W), F32), jax.ShapeDtypeStruct(wshape, F32),
                   jax.ShapeDtypeStruct(wshape, F32)],
        compiler_params=_cp("arbitrary"),
    )(u, v, norm_g, w_s, b_full, dy)


def loss_head(x, gain, target):
    T, D = x.shape
    tm = _tile(T, 512)

    def body(x_ref, g_ref, t_ref, loss_ref, dx_ref, dg_ref):
        xv, g = x_ref[...], g_ref[...]
        r = lax.rsqrt(jnp.mean(xv * xv, axis=-1, keepdims=True) + EPS)
        err = xv * r * g - t_ref[...]
        part = 0.5 * jnp.sum(jnp.mean(err * err, axis=-1, keepdims=True), axis=0, keepdims=True)
        dx, dg = _rms_bwd_math(xv, g, err * (1.0 / D), jnp.zeros_like(xv))
        dx_ref[...] = dx

        @pl.when(pl.program_id(0) == 0)
        def _():
            loss_ref[...] = jnp.zeros_like(loss_ref)
            dg_ref[...] = jnp.zeros_like(dg_ref)

        loss_ref[...] += part
        dg_ref[...] += dg

    row = pl.BlockSpec((tm, D), lambda i: (i, 0))
    vec = pl.BlockSpec((1, D), lambda i: (0, 0))
    return pl.pallas_call(
        body, name="loss_head", grid=(T // tm,),
        in_specs=[row, vec, row], out_specs=[pl.BlockSpec((1, 1), lambda i: (0, 0)), row, vec],
        out_shape=[jax.ShapeDtypeStruct((1, 1), F32), jax.ShapeDtypeStruct((T, D), F32),
                   jax.ShapeDtypeStruct((1, D), F32)],
        compiler_params=_cp("arbitrary"),
    )(x, gain, target)


class MultiJob:
    def __init__(self, jobs):
        self.jobs = jobs
        self.args = [a for j in jobs for a in j.args]
        self.out_shape = [s for j in jobs for s in j.out_shape]
        self.scratch = [s for j in jobs for s in j.scratch]
        self.n_in, self.n_out = len(self.args), len(self.out_shape)

    def _each(self, ins, outs, sems):
        i = o = s = 0
        for j in self.jobs:
            yield j, ins[i:i + j.n_in], outs[o:o + j.n_out], sems[s:s + len(j.scratch)]
            i, o, s = i + j.n_in, o + j.n_out, s + len(j.scratch)

    def start(self, ins, outs, sems):
        for j, a, b, c in self._each(ins, outs, sems):
            j.start(a, b, c)

    def middle(self, ins, outs, sems):
        for j, a, b, c in self._each(ins, outs, sems):
            j.middle(a, b, c)

    def finish(self, ins, outs, sems):
        for j, a, b, c in self._each(ins, outs, sems):
            j.finish(a, b, c)

    def split(self, results):
        o = 0
        for j in self.jobs:
            yield results[o:o + j.n_out]
            o += j.n_out


class Plan:
    def __init__(self, shard, gathers, scatters, small_carrier=None, pack_small=None):
        self.shard, self.gathers, self.scatters = shard, gathers, scatters
        self.small_carrier, self.pack_small = small_carrier, pack_small
        self.weights, self.grads, self.parts = {}, {}, {}
        self.small_src = self.small_parts = None

    def weight(self, kind, l):
        return self.weights[(kind, l)]

    def grad(self, kind, l, g):
        self.grads[(kind, l)] = g

    def small_ready(self, small, d_final):
        if self.pack_small is not None:
            self.small_src = self.pack_small(small, d_final)

    def _jobs(self, key):
        jobs = []
        if key in self.gathers:
            ks = self.gathers[key]
            jobs.append((GatherJob([self.shard(*k) for k in ks]), self.weights, ks))
        if key in self.scatters:
            ks = self.scatters[key]
            jobs.append((ScatterJob([self.grads[k] for k in ks]), self.parts, ks))
        if key == self.small_carrier and self.small_src is not None:
            jobs.append((GatherJob([(self.small_src, None)]), None, None))
        return jobs

    def _deliver(self, jobs, results):
        multi = MultiJob([j for j, _, _ in jobs])
        for (_, store, ks), res in zip(jobs, multi.split(results)):
            if store is None:
                self.small_parts = res[0]
            else:
                store.update(zip(ks, res))

    def run(self, key, fn, *args, **kw):
        jobs = self._jobs(key)
        if not jobs:
            out = fn(*args, **kw)
            return out if fn is mm_tn else out[0]
        res, jres = fn(*args, job=MultiJob([j for j, _, _ in jobs]), **kw)
        self._deliver(jobs, jres)
        return res

    def alone(self, key, name):
        jobs = self._jobs(key)
        if jobs:
            self._deliver(jobs, run_job(MultiJob([j for j, _, _ in jobs]), name))


def _local_step(x, target, layers, final_norm, n_ex, plan):
    T, D = x.shape
    L = len(layers)
    cst = _attn_consts(T // n_ex)
    ident = lambda j: j
    EV_A, EV_B = 3 * (D // 2), QW + 2 * KW
    OD_W = D // 2
    wt = plan.weight

    saved = []
    for l, W in enumerate(layers):
        s = dict(x0=x)
        x1, s["gu1"] = plan.run(("ffn1_fwd", l), ffn_fwd, x, W["n1"], wt("f1_in_t", l), wt("f1_out", l))
        if l % 2 == 0:
            pa, pb, h = mm_nt(x1, wt("mx_in_t", l), [(0, EV_A), (EV_A, EV_B)], F32, True, W["nm"])
            qg = jnp.tile(W["q_norm"], N_Q_HEADS)[None]
            kg = jnp.tile(W["k_norm"], N_KV_HEADS)[None]
            mix = conv_fwd(pa, W["conv_w"], n_ex)
            qp, kp, vp = qkv_prep_fwd(pb, qg, kg, cst, n_ex)
            mix, op, lse = plan.run(("attn_fwd", l), attn_fwd, qp, kp, vp, cst["gather"], n_ex, mix)
            s.update(pa=pa, pb=pb, qg=qg, kg=kg, qp=qp, kp=kp, vp=vp, op=op, lse=lse)
        else:
            p, u, v, h = mm_nt(x1, wt("mx_in_t", l), [(0, OD_W), (OD_W, OD_W), (2 * OD_W, OD_W)], F32, True, W["nm"])
            scale = W["pool_scale"][None]
            sn = W["sgu_norm"][None]
            b_full = jnp.broadcast_to(W["sgu_b"][..., None], W["sgu_w"].shape)
            mix = sgu_fwd(u, v, sn, W["sgu_w"], b_full, pool_fwd(p, W["pool_w"], scale, n_ex))
            s.update(p=p, u=u, v=v, scale=scale, sn=sn, b_full=b_full)
        x2 = mm_nn([(mix, 0, D, 0)], wt("mx_out", l), residual=x1)
        x3, s["gu2"] = plan.run(("ffn2_fwd", l), ffn_fwd, x2, W["n2"], wt("f2_in_t", l), wt("f2_out", l))
        s.update(x1=x1, x2=x2, h=h, mix=mix)
        saved.append(s)
        x = x3

    loss, dx, d_final = loss_head(x, final_norm, target)

    small = [None] * L

    def ffn_back(which, l, dout, xin, gain, gu, sm, sm_key):
        w_in, w_out = wt(which + "_in_t", l), wt(which + "_out", l)
        F = w_out.shape[0]
        nc = F // FFN_TN
        dxi, sm[sm_key], a, dgu, xn, dob = plan.run((which + "_bwd", l), ffn_bwd_x, dout, xin, gain, gu, w_in, w_out)
        plan.grad(which + "_out", l, plan.run((which + "_out_grad", l), mm_tn, a, dob, F, ident))
        if which == "f1" and l == 0:
            plan.small_ready(small, d_final)
        plan.grad(which + "_in_t", l, plan.run(
            (which + "_in_grad", l), mm_tn, dgu, xn, 2 * F, lambda k, c: k * nc + c, grid=(2, nc),
            col_block=lambda k, c: 2 * c + k))
        return dxi

    for l in reversed(range(L)):
        s, W = saved[l], layers[l]
        sm = small[l] = {}
        dx = ffn_back("f2", l, dx, s["x2"], W["n2"], s["gu2"], sm, "n2")
        dmix, dxb = mm_nt(dx, wt("mx_out", l), [(0, D)], BF16, emit_a_bf16=True)
        plan.grad("mx_out", l, mm_tn(s["mix"], dxb, D, ident))
        if l % 2 == 0:
            d_proj, sm["conv_w"] = conv_bwd(s["pa"], W["conv_w"], n_ex, dmix, EV_A + EV_B)
            dq, dkp, dvp = plan.run(("attn_bwd", l), attn_bwd, s["qp"], s["kp"], s["vp"], s["op"], s["lse"], dmix, cst, n_ex)
            d_proj, dqg, dkg = qkv_prep_bwd(s["pb"], s["qg"], s["kg"], cst, n_ex, dq, dkp, dvp, d_proj)
            d_pieces = [(d_proj, 0, EV_A + EV_B, 0)]
            plan.grad("mx_in_t", l, mm_tn(d_proj, s["h"], EV_A + EV_B, ident))
            sm["q_norm"] = dqg.reshape(N_Q_HEADS, HEAD_DIM).sum(0)
            sm["k_norm"] = dkg.reshape(N_KV_HEADS, HEAD_DIM).sum(0)
        else:
            d_proj, d_sn, sm["sgu_w"], d_sb = sgu_bwd(s["u"], s["v"], s["sn"], W["sgu_w"], s["b_full"], dmix)
            d_proj, sm["pool_w"], d_ps = pool_bwd(s["p"], W["pool_w"], s["scale"], n_ex, dmix, d_proj)
            d_pieces = [(d_proj, 0, OD_W, 1), (d_proj, 1, OD_W, 2), (d_proj, 2, OD_W, 0)]
            nb = OD_W // MM_TC
            plan.grad("mx_in_t", l, mm_tn(d_proj, s["h"], 3 * OD_W,
                                          lambda jj: jnp.where(jj < 2 * nb, jj + nb, jj - 2 * nb)))
            sm["pool_scale"], sm["sgu_norm"], sm["sgu_b"] = d_ps[0], d_sn[0], d_sb.sum(-1)
        dx, sm["nm"] = mm_nn(d_pieces, wt("mx_in_t", l), norm_bwd=(s["x1"], W["nm"], dx))
        dx = ffn_back("f1", l, dx, s["x0"], W["n1"], s["gu1"], sm, "n1")
    return loss, dx


def all_gather(srcs):
    return run_job(GatherJob(srcs), "all_gather")


def all_reduce_small(v):
    R, C = v.shape

    def body(v_ref, o_ref, buf, send_sems, recv_sems):
        x, y, c, me = _my_place()
        buf[me] = v_ref[...]
        for k in range(1, N_DEV):
            peer, _ = _peer(x, y, c, k)
            pltpu.make_async_remote_copy(src_ref=v_ref, dst_ref=buf.at[me], send_sem=send_sems.at[k - 1],
                                         recv_sem=recv_sems.at[k - 1], device_id=peer, device_id_type=MESH_ID).start()
        for k in range(1, N_DEV):
            peer, pidx = _peer(x, y, c, k)
            cp = pltpu.make_async_remote_copy(src_ref=v_ref, dst_ref=buf.at[pidx], send_sem=send_sems.at[k - 1],
                                              recv_sem=recv_sems.at[k - 1], device_id=peer, device_id_type=MESH_ID)
            cp.wait_recv()
            cp.wait_send()
        acc = buf[0]
        for s in range(1, N_DEV):
            acc = acc + buf[s]
        o_ref[...] = acc

    vm = pl.BlockSpec(memory_space=pltpu.VMEM)
    return pl.pallas_call(
        body, name="all_reduce_small", in_specs=[vm], out_specs=vm, out_shape=jax.ShapeDtypeStruct((R, C), F32),
        scratch_shapes=[pltpu.VMEM((N_DEV, R, C), F32), pltpu.SemaphoreType.DMA((7,)), pltpu.SemaphoreType.DMA((7,))],
        compiler_params=pltpu.CompilerParams(vmem_limit_bytes=VMEM_LIMIT),
    )(v)


def cast_shards(w):
    L, A, B = w.shape

    def body(w_ref, o_ref):
        o_ref[...] = w_ref[...].astype(BF16)

    return pl.pallas_call(
        body, name="cast_shards", grid=(L,),
        in_specs=[pl.BlockSpec((None, A, B), lambda l: (l, 0, 0))],
        out_specs=pl.BlockSpec((None, A, B), lambda l: (l, 0, 0)),
        out_shape=jax.ShapeDtypeStruct((L, A, B), BF16), compiler_params=_cp("parallel"),
    )(w)


ADAM_TC = 256


def adamw(parts, w, m, v, l, prev=None):
    P, R, C = parts.shape
    tc = _tile(C, ADAM_TC)
    c1, c2 = 1.0 - ADAM_B1 ** ADAM_STEP, 1.0 - ADAM_B2 ** ADAM_STEP

    def body(p_ref, w_ref, m_ref, v_ref, g_ref, d_ref, mo_ref, vo_ref):
        g = p_ref[0].astype(F32)
        for s in range(1, P):
            g = g + p_ref[s].astype(F32)
        m1 = ADAM_B1 * m_ref[...] + (1.0 - ADAM_B1) * g
        v1 = ADAM_B2 * v_ref[...] + (1.0 - ADAM_B2) * (g * g)
        g_ref[...] = g
        mo_ref[...] = m1
        vo_ref[...] = v1
        d_ref[...] = -ADAM_LR * ((m1 / c1) / (jnp.sqrt(v1 / c2) + ADAM_EPS) + ADAM_WD * w_ref[...])

    wspec = pl.BlockSpec((None, R, tc), lambda i: (l, 0, i))
    prev = list(prev) if prev is not None else []
    return pl.pallas_call(
        lambda *refs: body(*refs[:4], *refs[4 + len(prev):]), name="adamw", grid=(C // tc,),
        in_specs=[pl.BlockSpec((P, R, tc), lambda i: (0, 0, i)), wspec, wspec, wspec] + [ANY] * len(prev),
        out_specs=[wspec] * 4, out_shape=[jax.ShapeDtypeStruct(w.shape, F32)] * 4,
        input_output_aliases={4 + i: i for i in range(len(prev))},
        compiler_params=_cp("parallel"),
    )(parts, w, m, v, *prev)


_WEIGHTS = ['ffn1_norm', 'ffn1_w_in', 'ffn1_w_out', 'mix_norm', 'ffn2_norm', 'ffn2_w_in', 'ffn2_w_out', 'ev_w_in',
            'ev_conv_w', 'ev_q_norm', 'ev_k_norm', 'ev_w_out', 'od_w_in', 'od_pool_w', 'od_pool_scale', 'od_sgu_norm',
            'od_sgu_w', 'od_sgu_b', 'od_w_out', 'final_norm']
_BIG = dict(ffn1_w_in=True, ffn1_w_out=False, ffn2_w_in=True, ffn2_w_out=False,
            ev_w_in=True, ev_w_out=False, od_w_in=True, od_w_out=False)
_SMALL_SHARDED = ['ev_conv_w', 'od_pool_scale', 'od_sgu_norm']
_SMALL = [n for n in _WEIGHTS if n not in _BIG]
_PACK_ROWS = 8 * LANES


_KINDS = ("f1_in_t", "f1_out", "mx_in_t", "mx_out", "f2_in_t", "f2_out")
_CARRIER_US = dict(ffn1_fwd=105, ffn2_fwd=105, attn_fwd=115, f2_bwd=135, f1_bwd=135, attn_bwd=205,
                   f2_out_grad=33, f1_out_grad=33, f2_in_grad=61, f1_in_grad=61)
_GATHER_US_PER_ROW, _SCATTER_US_PER_ROW, _SMALL_GATHER_US, _SLACK_US = 0.08, 0.176, 42, 10


def _schedule(L, rows):
    events = []
    for l in range(L):
        events += [("ffn1_fwd", l), ("mixer", l)] + ([("attn_fwd", l)] if l % 2 == 0 else []) + [("ffn2_fwd", l)]
    consumer = {"f1": "ffn1_fwd", "mx": "mixer", "f2": "ffn2_fwd"}
    queue = [(k, l) for l in range(L) for k in _KINDS]
    pos = {t: events.index((consumer[t[0][:2]], t[1])) for t in queue}
    gathers = {"first": [t for t in queue if pos[t] == 0]}
    queue = [t for t in queue if pos[t] > 0]
    carriers = [i for i, e in enumerate(events) if e[0] in _CARRIER_US]
    for i in carriers:
        budget, take = _CARRIER_US[events[i][0]], []
        later = [j for j in carriers if j > i]
        while queue:
            t = queue[0]
            cost = rows(*t) * _GATHER_US_PER_ROW
            forced = not any(j < pos[t] for j in later)
            if not forced and cost > budget:
                break
            take.append(queue.pop(0))
            budget -= cost
        if take:
            gathers[events[i]] = take
    assert not queue
    events = []
    for l in reversed(range(L)):
        events += [("f2_bwd", l), ("f2_out_grad", l), ("f2_in_grad", l), ("mx_out_ready", l)]
        events += [("attn_bwd", l)] if l % 2 == 0 else []
        events += [("mx_in_ready", l), ("f1_bwd", l), ("f1_out_grad", l), ("f1_in_grad", l)]
    made_by = {"f2_out": "f2_out_grad", "f2_in_t": "f2_in_grad", "mx_out": "mx_out_ready", "mx_in_t": "mx_in_ready",
               "f1_out": "f1_out_grad", "f1_in_t": "f1_in_grad"}
    small_carrier = ("f1_in_grad", 0)
    scatters, ready = {}, []
    for e in events:
        if e[0] in _CARRIER_US:
            budget, take = _CARRIER_US[e[0]] - (_SMALL_GATHER_US if e == small_carrier else 0), []
            while True:
                fits = [t for t in ready if rows(*t) * _SCATTER_US_PER_ROW <= budget + _SLACK_US]
                if not fits:
                    break
                t = max(fits, key=lambda u: rows(*u))
                budget -= rows(*t) * _SCATTER_US_PER_ROW
                ready.remove(t)
                take.append(t)
            if take:
                scatters[e] = take
        ready += [(k, e[1]) for k in _KINDS if made_by[k] == e[0]]
    scatters["last"] = ready
    return gathers, scatters, small_carrier


def _pack(arrs):
    flat = jnp.concatenate([a.reshape(-1) for a in arrs])
    pad = (-flat.shape[0]) % _PACK_ROWS
    return jnp.pad(flat, (0, pad)).reshape(-1, LANES)


def _unpack(buf, shapes):
    flat, out, off = buf.reshape(-1), [], 0
    for s in shapes:
        n = math.prod(s)
        out.append(flat[off:off + n].reshape(s))
        off += n
    return out


def _unshard_last(g, lead):
    nd = len(lead)
    return jnp.moveaxis(g, 0, nd).reshape(*lead, -1)


def kernel(x, ffn1_norm, ffn1_w_in, ffn1_w_out, mix_norm, ffn2_norm, ffn2_w_in, ffn2_w_out, ev_w_in, ev_conv_w, ev_q_norm, ev_k_norm, ev_w_out, od_w_in, od_pool_w, od_pool_scale, od_sgu_norm, od_sgu_w, od_sgu_b, od_w_out, final_norm, loss_target, m_ffn1_norm, m_ffn1_w_in, m_ffn1_w_out, m_mix_norm, m_ffn2_norm, m_ffn2_w_in, m_ffn2_w_out, m_ev_w_in, m_ev_conv_w, m_ev_q_norm, m_ev_k_norm, m_ev_w_out, m_od_w_in, m_od_pool_w, m_od_pool_scale, m_od_sgu_norm, m_od_sgu_w, m_od_sgu_b, m_od_w_out, m_final_norm, v_ffn1_norm, v_ffn1_w_in, v_ffn1_w_out, v_mix_norm, v_ffn2_norm, v_ffn2_w_in, v_ffn2_w_out, v_ev_w_in, v_ev_conv_w, v_ev_q_norm, v_ev_k_norm, v_ev_w_out, v_od_w_in, v_od_pool_w, v_od_pool_scale, v_od_sgu_norm, v_od_sgu_w, v_od_sgu_b, v_od_w_out, v_final_norm):
    w = dict(zip(_WEIGHTS, (ffn1_norm, ffn1_w_in, ffn1_w_out, mix_norm, ffn2_norm, ffn2_w_in, ffn2_w_out, ev_w_in, ev_conv_w, ev_q_norm, ev_k_norm, ev_w_out, od_w_in, od_pool_w, od_pool_scale, od_sgu_norm, od_sgu_w, od_sgu_b, od_w_out, final_norm)))
    m = dict(zip(_WEIGHTS, (m_ffn1_norm, m_ffn1_w_in, m_ffn1_w_out, m_mix_norm, m_ffn2_norm, m_ffn2_w_in, m_ffn2_w_out, m_ev_w_in, m_ev_conv_w, m_ev_q_norm, m_ev_k_norm, m_ev_w_out, m_od_w_in, m_od_pool_w, m_od_pool_scale, m_od_sgu_norm, m_od_sgu_w, m_od_sgu_b, m_od_w_out, m_final_norm)))
    v = dict(zip(_WEIGHTS, (v_ffn1_norm, v_ffn1_w_in, v_ffn1_w_out, v_mix_norm, v_ffn2_norm, v_ffn2_w_in, v_ffn2_w_out, v_ev_w_in, v_ev_conv_w, v_ev_q_norm, v_ev_k_norm, v_ev_w_out, v_od_w_in, v_od_pool_w, v_od_pool_scale, v_od_sgu_norm, v_od_sgu_w, v_od_sgu_b, v_od_w_out, v_final_norm)))
    n_ex, seq, D = x.shape
    T = n_ex * seq
    L = ffn1_norm.shape[0]
    me = 4 * lax.axis_index("x") + 2 * lax.axis_index("y") + lax.axis_index("c")

    sh_small = [w[n] for n in _SMALL_SHARDED]
    packed = all_gather([(_pack(sh_small), None)])[0].reshape(N_DEV, -1)
    full_small = {}
    off = 0
    for n, a in zip(_SMALL_SHARDED, sh_small):
        cnt = math.prod(a.shape)
        full_small[n] = _unshard_last(packed[:, off:off + cnt].reshape((N_DEV,) + a.shape), a.shape[:-1])
        off += cnt

    tr = lambda a: jnp.swapaxes(a, 1, 2)
    wmv = {n: tuple(tr(d[n]) if t else d[n] for d in (w, m, v)) for n, t in _BIG.items()}
    shards = {n: cast_shards(wmv[n][0]) for n in _BIG}

    def name_of(kind, l):
        mx = "ev" if l % 2 == 0 else "od"
        return {"f1_in_t": "ffn1_w_in", "f1_out": "ffn1_w_out", "f2_in_t": "ffn2_w_in", "f2_out": "ffn2_w_out",
                "mx_in_t": mx + "_w_in", "mx_out": mx + "_w_out"}[kind], (l // 2 if kind.startswith("mx") else l)

    def shard(kind, l):
        name, idx = name_of(kind, l)
        return shards[name], idx

    g_shapes = {}

    def pack_small(small, d_final):
        ev = [sm for l, sm in enumerate(small) if l % 2 == 0]
        od = [sm for l, sm in enumerate(small) if l % 2 == 1]
        st = lambda sms, k: jnp.stack([sm[k] for sm in sms])
        g_full = dict(ffn1_norm=st(small, "n1")[:, 0], mix_norm=st(small, "nm")[:, 0], ffn2_norm=st(small, "n2")[:, 0],
                      ev_conv_w=st(ev, "conv_w"), ev_q_norm=st(ev, "q_norm"), ev_k_norm=st(ev, "k_norm"),
                      od_pool_w=st(od, "pool_w"), od_pool_scale=st(od, "pool_scale"), od_sgu_norm=st(od, "sgu_norm"),
                      od_sgu_w=st(od, "sgu_w"), od_sgu_b=st(od, "sgu_b"), final_norm=d_final[0])
        g_shapes.update({n: g_full[n].shape for n in _SMALL})
        return _pack([g_full[n] for n in _SMALL])

    gathers, scatters, small_carrier = _schedule(L, lambda kind, l: shards[name_of(kind, l)[0]].shape[1])
    plan = Plan(shard, gathers, scatters, small_carrier, pack_small)
    layers = []
    for l in range(L):
        j = l // 2
        W = dict(n1=ffn1_norm[l][None], nm=mix_norm[l][None], n2=ffn2_norm[l][None])
        if l % 2 == 0:
            W.update(conv_w=full_small["ev_conv_w"][j], q_norm=ev_q_norm[j], k_norm=ev_k_norm[j])
        else:
            W.update(pool_w=od_pool_w[j], pool_scale=full_small["od_pool_scale"][j], sgu_norm=full_small["od_sgu_norm"][j],
                     sgu_w=od_sgu_w[j], sgu_b=od_sgu_b[j])
        layers.append(W)

    plan.alone("first", "gather_first")
    loss, dx = _local_step(x.reshape(T, D), loss_target.reshape(T, D), layers, final_norm[None], n_ex, plan)
    plan.alone("last", "scatter_last")

    out = {n: None for n in _BIG}
    for (kind, l), parts in plan.parts.items():
        name, idx = name_of(kind, l)
        out[name] = adamw(parts, *wmv[name], idx, prev=out[name])
    out = {n: [tr(a) if _BIG[n] else a for a in res] for n, res in out.items()}

    g8 = plan.small_parts.reshape(N_DEV, -1)
    cols, off = [], 0
    for n in _SMALL:
        cnt = math.prod(g_shapes[n])
        g = g8[:, off:off + cnt].reshape((N_DEV,) + g_shapes[n])
        off += cnt
        if n in _SMALL_SHARDED:
            width = w[n].shape[-1]
            g = lax.dynamic_slice_in_dim(g, me * width, width, axis=g.ndim - 1)
        cols.append(g.reshape(N_DEV, -1))
    g8 = jnp.concatenate(cols, axis=1)
    g8 = jnp.pad(g8, ((0, 0), (0, (-g8.shape[1]) % _PACK_ROWS))).reshape(N_DEV, -1, LANES)
    pk = lambda d: _pack([d[n] for n in _SMALL])[None]
    small_out = adamw(g8, pk(w), pk(m), pk(v), 0)
    shapes = [w[n].shape for n in _SMALL]
    for i in range(4):
        for n, a in zip(_SMALL, _unpack(small_out[i], shapes)):
            out.setdefault(n, [None] * 4)[i] = a

    total = lax.psum(loss[0, 0], ("x", "y", "c"))
    return (total, dx.reshape(n_ex, seq, D), *[out[n][0] for n in _WEIGHTS], *[out[n][1] for n in _WEIGHTS],
            *[out[n][2] for n in _WEIGHTS], *[out[n][3] for n in _WEIGHTS])
```

```python
import functools
import math

import jax
import jax.numpy as jnp
from jax import lax
from jax.experimental import pallas as pl
from jax.experimental.pallas import tpu as pltpu

F32, BF16 = jnp.float32, jnp.bfloat16
EPS = 1e-6
N_DEV = 8
V7X_VMEM_BYTES = 64 * 1024 * 1024
VMEM_LIMIT = V7X_VMEM_BYTES - 8 * 1024 * 1024
LANES = 128
HEAD_DIM = 64
N_Q_HEADS = 8
N_KV_HEADS = 2
Q_PER_KV = N_Q_HEADS // N_KV_HEADS
GRID_W = 64
ROPE_THETA = 10000.0
POOL_RADII = (1, 2, 4, 8)
SGU_CHUNK = 128
GROUP = 128
ADAM_LR, ADAM_B1, ADAM_B2, ADAM_EPS, ADAM_WD, ADAM_STEP = 0.001, 0.9, 0.999, 1e-08, 0.01, 10
MESH_ID = pl.DeviceIdType.MESH


def _cp(*sem):
    return pltpu.CompilerParams(dimension_semantics=sem, vmem_limit_bytes=VMEM_LIMIT)


def _dot(a, b, ca, cb):
    return lax.dot_general(a, b, (((ca,), (cb,)), ((), ())), preferred_element_type=F32)


def _nn(a, b):
    return _dot(a, b, 1, 0)


def _nt(a, b):
    return _dot(a, b, 1, 1)


def _tn(a, b):
    return _dot(a, b, 0, 0)


def _split_mm(x, m):
    hi = x.astype(BF16)
    lo = (x - hi.astype(F32)).astype(BF16)
    return _nn(hi, m) + _nn(lo, m)


def _tile(n, pref):
    t = min(n, pref)
    assert n % t == 0, (n, pref)
    return t


ANY = pl.BlockSpec(memory_space=pl.ANY)
OTHER_CHIPS = (2, 4, 6)
JOB_MIDDLE = 0.55


def _my_place():
    x, y, c = lax.axis_index("x"), lax.axis_index("y"), lax.axis_index("c")
    return x, y, c, 4 * x + 2 * y + c


def _peer(x, y, c, k):
    px = 1 - x if k & 4 else x
    py = 1 - y if k & 2 else y
    pc = 1 - c if k & 1 else c
    return (px, py, pc), 4 * px + 2 * py + pc


def _remote(src, dst, send_sems, recv_sems, i, peer):
    return pltpu.make_async_remote_copy(src_ref=src, dst_ref=dst, send_sem=send_sems.at[i], recv_sem=recv_sems.at[i],
                                        device_id=peer, device_id_type=MESH_ID)


class GatherJob:
    def __init__(self, srcs):
        self.srcs = srcs
        self.args = [a for a, _ in srcs]
        self.dims = [a.shape[-2:] for a, _ in srcs]
        n = self.n_in = self.n_out = len(srcs)
        self.out_shape = [jax.ShapeDtypeStruct((N_DEV * r, cc), a.dtype) for (a, _), (r, cc) in zip(srcs, self.dims)]
        self.scratch = [pltpu.SemaphoreType.DMA((N_DEV * n,)), pltpu.SemaphoreType.DMA((N_DEV * n,)),
                        pltpu.SemaphoreType.DMA((n,))]

    def _rows(self, outs, t, idx):
        r = self.dims[t][0]
        return outs[t].at[pl.ds(pl.multiple_of(idx * r, 8), r), :]

    def _local(self, ins, outs, loc, t, me):
        src = ins[t] if self.srcs[t][1] is None else ins[t].at[self.srcs[t][1]]
        return src, pltpu.make_async_copy(src, self._rows(outs, t, me), loc.at[t])

    def start(self, ins, outs, sems):
        send, recv, loc = sems
        x, y, c, me = _my_place()
        for t in range(self.n_in):
            src, local = self._local(ins, outs, loc, t, me)
            local.start()
            for k in (2, 4, 1):
                _remote(src, self._rows(outs, t, me), send, recv, N_DEV * t + k, _peer(x, y, c, k)[0]).start()

    def _copy(self, outs, sems, t, origin, i, to):
        x, y, c, _ = _my_place()
        blk = self._rows(outs, t, _peer(x, y, c, origin)[1])
        return _remote(blk, blk, sems[0], sems[1], N_DEV * t + i, _peer(x, y, c, to)[0])

    def middle(self, ins, outs, sems):
        c = _my_place()[2]

        def relay(t, got, to):
            self._copy(outs, sems, t, got, got, got).wait_recv()
            self._copy(outs, sems, t, got, 6, to).start()
            self._copy(outs, sems, t, to, to, to).wait_recv()

        for t in range(self.n_in):
            pl.when(c == 1)(functools.partial(relay, t, 2, 4))
            pl.when(c == 0)(functools.partial(relay, t, 4, 2))
            for k in (2, 4):
                self._copy(outs, sems, t, k, k + 1, 1).start()

    def finish(self, ins, outs, sems):
        send, recv, loc = sems
        x, y, c, me = _my_place()
        for t in range(self.n_in):
            self._copy(outs, sems, t, 6, 6, 6).wait_recv()
            self._copy(outs, sems, t, 6, 7, 1).start()
        for t in range(self.n_in):
            for k in range(1, N_DEV):
                peer, pidx = _peer(x, y, c, k)
                blk = self._rows(outs, t, pidx)
                if k % 2 == 1:
                    _remote(blk, blk, send, recv, N_DEV * t + k, peer).wait_recv()
                _remote(blk, blk, send, recv, N_DEV * t + k, peer).wait_send()
            self._local(ins, outs, loc, t, me)[1].wait()


class ScatterJob:
    def __init__(self, grads):
        self.args = list(grads)
        self.dims = [(g.shape[0] // N_DEV, g.shape[1]) for g in grads]
        n = self.n_in = self.n_out = len(grads)
        self.out_shape = [jax.ShapeDtypeStruct((N_DEV, r, cc), g.dtype) for g, (r, cc) in zip(grads, self.dims)]
        self.scratch = [pltpu.SemaphoreType.DMA((N_DEV * n,)), pltpu.SemaphoreType.DMA((N_DEV * n,)),
                        pltpu.SemaphoreType.DMA((n,))]

    def _rows(self, ins, t, idx):
        r = self.dims[t][0]
        return ins[t].at[pl.ds(pl.multiple_of(idx * r, 8), r), :]

    def start(self, ins, outs, sems):
        send, recv, loc = sems
        x, y, c, me = _my_place()
        for t in range(self.n_in):
            pltpu.make_async_copy(self._rows(ins, t, me), outs[t].at[me], loc.at[t]).start()
            for k in OTHER_CHIPS + (1, 3, 5, 7):
                peer, pidx = _peer(x, y, c, k)
                _remote(self._rows(ins, t, pidx), outs[t].at[me], send, recv, N_DEV * t + k, peer).start()

    def middle(self, ins, outs, sems):
        pass

    def finish(self, ins, outs, sems):
        send, recv, loc = sems
        x, y, c, me = _my_place()
        for t in range(self.n_in):
            for k in range(1, N_DEV):
                peer, pidx = _peer(x, y, c, k)
                cp = _remote(self._rows(ins, t, pidx), outs[t].at[pidx], send, recv, N_DEV * t + k, peer)
                cp.wait_recv()
                cp.wait_send()
            pltpu.make_async_copy(self._rows(ins, t, me), outs[t].at[me], loc.at[t]).wait()


def _call(body, name, grid, in_specs, args, out_specs, out_shape, scratch=(), sem=(), job=None, aliases=None):
    in_specs, out_specs, out_shape, scratch = list(in_specs), list(out_specs), list(out_shape), list(scratch)
    n_in, n_out, n_scr = len(args), len(out_shape), len(scratch)
    if job is None:
        res = pl.pallas_call(body, name=name, grid=grid, in_specs=in_specs, out_specs=out_specs, out_shape=out_shape,
                             scratch_shapes=scratch, input_output_aliases=aliases or {}, compiler_params=_cp(*sem))(*args)
        return res, None
    o0 = n_in + job.n_in
    s0 = o0 + n_out + job.n_out

    def carrier(*refs):
        jin, jout, jsem = refs[n_in:o0], refs[o0 + n_out:s0], refs[s0 + n_scr:]
        ids = [pl.program_id(a) for a in range(len(grid))]
        def at(step):
            idx = []
            for g in reversed(grid):
                idx.append(step % g)
                step //= g
            return functools.reduce(jnp.logical_and, [i == j for i, j in zip(ids, reversed(idx))])

        steps = math.prod(grid)
        if grid:
            pl.when(at(0))(lambda: job.start(jin, jout, jsem))
            pl.when(at(int(steps * JOB_MIDDLE)))(lambda: job.middle(jin, jout, jsem))
        else:
            job.start(jin, jout, jsem)
            job.middle(jin, jout, jsem)
        body(*refs[:n_in], *refs[o0:o0 + n_out], *refs[s0:s0 + n_scr])
        if grid:
            pl.when(at(steps - 1))(lambda: job.finish(jin, jout, jsem))
        else:
            job.finish(jin, jout, jsem)

    res = pl.pallas_call(
        carrier, name=name + "_comm", grid=grid, in_specs=in_specs + [ANY] * job.n_in,
        out_specs=out_specs + [ANY] * job.n_out, out_shape=out_shape + job.out_shape,
        scratch_shapes=scratch + job.scratch, input_output_aliases=aliases or {},
        compiler_params=_cp(*(["arbitrary"] * len(grid))))(*args, *job.args)
    return res[:n_out], res[n_out:]


def run_job(job, name):
    return _call(lambda: None, name, (), [], [], [], [], job=job)[1]


@jax.custom_vjp
def bmm(x, w):
    return _nn(x.astype(BF16), w.astype(BF16))


def _bmm_fwd(x, w):
    return bmm(x, w), (x, w)


def _bmm_bwd(res, g):
    x, w = res
    gb = g.astype(BF16)
    return _nt(gb, w.astype(BF16)), _tn(x.astype(BF16), gb)


bmm.defvjp(_bmm_fwd, _bmm_bwd)


def _shift_raw(x, d):
    n = x.shape[0]
    r = pltpu.roll(x, d % n, axis=0)
    row = lax.broadcasted_iota(jnp.int32, x.shape, 0)
    keep = (row >= d) if d > 0 else (row < n + d)
    return jnp.where(keep, r, 0.0)


def shift_rows(x, d):
    @jax.custom_vjp
    def f(v):
        return _shift_raw(v, d)

    f.defvjp(lambda v: (_shift_raw(v, d), None), lambda _, g: (_shift_raw(g, -d),))
    return f(x)


def _swap_raw(x):
    n = x.shape[1]
    nxt = pltpu.roll(x, n - 1, axis=1)
    prv = pltpu.roll(x, 1, axis=1)
    lane = lax.broadcasted_iota(jnp.int32, x.shape, 1)
    return jnp.where(lane % 2 == 0, nxt, prv)


@jax.custom_vjp
def swap_pairs(x):
    return _swap_raw(x)


swap_pairs.defvjp(lambda x: (_swap_raw(x), None), lambda _, g: (_swap_raw(g),))


@jax.custom_vjp
def group_mean(x, bd):
    return _split_mm(x, bd)


group_mean.defvjp(lambda x, bd: (_split_mm(x, bd), bd), lambda bd, g: (_split_mm(g, bd), jnp.zeros_like(bd)))


def _rope_norm(x, gain, cos, sgn, bd, scale):
    xn = x * lax.rsqrt(group_mean(x * x, bd) + EPS) * gain
    return (xn * cos + swap_pairs(xn) * sgn) * scale


def _conv_gate(gb, gc, hc, w):
    z = gc * hc
    c = shift_rows(z, 1) * w[0:1] + z * w[1:2] + shift_rows(z, -1) * w[2:3]
    return gb * c


def _window_sum(p, r):
    b = f = p
    k = 1
    while k < r:
        b = b + shift_rows(b, k)
        f = f + shift_rows(f, -k)
        k *= 2
    return b + f - p + shift_rows(p, r) + shift_rows(p, -r)


def _pool_mix(p, pool_w, scale):
    n = p.shape[0]
    t = lax.broadcasted_iota(jnp.int32, (n, 1), 0)
    outs = []
    for gi, r in enumerate(POOL_RADII):
        pg = p[:, gi * GROUP:(gi + 1) * GROUP]
        cnt = (jnp.minimum(t + r, n - 1) - jnp.maximum(t - r, 0) + 1).astype(F32)
        pooled = _window_sum(pg, r) / cnt - pg
        outs.append(bmm(pooled, pool_w[gi]))
    return jnp.concatenate(outs, axis=1) * scale


def _sgu(u, v, norm_g, w_s, b_full):
    ug = jax.nn.gelu(u)
    vg = jax.nn.gelu(v)
    vn = vg * lax.rsqrt(jnp.mean(vg * vg, axis=-1, keepdims=True) + EPS) * norm_g
    cols = []
    for g in range(w_s.shape[0]):
        rows = []
        for n in range(u.shape[0] // SGU_CHUNK):
            blk = vn[n * SGU_CHUNK:(n + 1) * SGU_CHUNK, g * GROUP:(g + 1) * GROUP]
            rows.append(bmm(w_s[g], blk) + b_full[g])
        cols.append(jnp.concatenate(rows, axis=0))
    return ug * jnp.concatenate(cols, axis=1)


def _rms_bwd_math(xv, gain, dy, dres):
    r = lax.rsqrt(jnp.mean(xv * xv, axis=-1, keepdims=True) + EPS)
    xh = xv * r
    dxh = dy * gain
    dx = dres + r * (dxh - xh * jnp.mean(dxh * xh, axis=-1, keepdims=True))
    return dx, jnp.sum(dy * xh, axis=0, keepdims=True)


FFN_TN = 256


def ffn_fwd(x, gain, wt_in, w_out, job=None):
    T, D = x.shape
    F = w_out.shape[0]
    tm, tn = _tile(T, 1024), FFN_TN
    nc = F // tn

    def body(x_ref, gn_ref, wg_ref, wu_ref, wo_ref, y_ref, gu_ref, xn_s, acc_s):
        c = pl.program_id(1)

        @pl.when(c == 0)
        def _():
            xv = x_ref[...]
            r = lax.rsqrt(jnp.mean(xv * xv, axis=-1, keepdims=True) + EPS)
            xn_s[...] = (xv * r * gn_ref[...]).astype(BF16)
            acc_s[...] = jnp.zeros_like(acc_s)

        xn = xn_s[...]
        g = _nt(xn, wg_ref[...])
        u = _nt(xn, wu_ref[...])
        gu_ref[:, :tn] = g.astype(BF16)
        gu_ref[:, tn:] = u.astype(BF16)
        a = (g * jax.nn.sigmoid(g) * u).astype(BF16)
        acc_s[...] += _nn(a, wo_ref[...])

        @pl.when(c == nc - 1)
        def _():
            y_ref[...] = x_ref[...] + 0.5 * acc_s[...]

    row = pl.BlockSpec((tm, D), lambda i, c: (i, 0))
    return _call(
        body, "ffn_fwd", (T // tm, nc),
        [row, pl.BlockSpec((1, D), lambda i, c: (0, 0)),
         pl.BlockSpec((tn, D), lambda i, c: (c, 0)),
         pl.BlockSpec((tn, D), lambda i, c: (c + nc, 0)),
         pl.BlockSpec((tn, D), lambda i, c: (c, 0))],
        [x, gain, wt_in, wt_in, w_out],
        [row, pl.BlockSpec((tm, 2 * tn), lambda i, c: (i, c)), row],
        [jax.ShapeDtypeStruct((T, D), F32), jax.ShapeDtypeStruct((T, 2 * F), BF16), jax.ShapeDtypeStruct((T, D), BF16)],
        [pltpu.VMEM((tm, D), F32)], ("parallel", "arbitrary"), job)


def ffn_bwd_x(dout, x, gain, gu, wt_in, w_out, job=None):
    T, D = x.shape
    F = w_out.shape[0]
    tm, tn = _tile(T, 1024), FFN_TN
    nc = F // tn

    def body(do_ref, x_ref, gn_ref, gu_ref, wg_ref, wu_ref, wo_ref,
             dx_ref, dgn_ref, a_ref, dgu_ref, dob_ref, acc_s):
        i, c = pl.program_id(0), pl.program_id(1)

        @pl.when(c == 0)
        def _():
            dob_ref[...] = (0.5 * do_ref[...]).astype(BF16)
            acc_s[...] = jnp.zeros_like(acc_s)

        da = jnp.concatenate([_nt(dob_ref[:tm // 2, :], wo_ref[...]), _nt(dob_ref[tm // 2:, :], wo_ref[...])], axis=0)
        g = gu_ref[:, :tn].astype(F32)
        u = gu_ref[:, tn:].astype(F32)
        sig = jax.nn.sigmoid(g)
        sl = g * sig
        a_ref[...] = (sl * u).astype(BF16)
        dg = (da * u * (sig * (1.0 + g * (1.0 - sig)))).astype(BF16)
        du = (da * sl).astype(BF16)
        dgu_ref[:, :tn] = dg
        dgu_ref[:, tn:] = du
        acc_s[...] += _nn(dg, wg_ref[...]) + _nn(du, wu_ref[...])

        @pl.when(c == nc - 1)
        def _():
            dx, dgn = _rms_bwd_math(x_ref[...], gn_ref[...], acc_s[...], do_ref[...])
            dx_ref[...] = dx

            @pl.when(i == 0)
            def _():
                dgn_ref[...] = jnp.zeros_like(dgn_ref)

            dgn_ref[...] += dgn

    row = pl.BlockSpec((tm, D), lambda i, c: (i, 0))
    return _call(
        body, "ffn_bwd_x", (T // tm, nc),
        [row, row, pl.BlockSpec((1, D), lambda i, c: (0, 0)),
         pl.BlockSpec((tm, 2 * tn), lambda i, c: (i, c)),
         pl.BlockSpec((tn, D), lambda i, c: (c, 0)),
         pl.BlockSpec((tn, D), lambda i, c: (c + nc, 0)),
         pl.BlockSpec((tn, D), lambda i, c: (c, 0))],
        [dout, x, gain, gu, wt_in, wt_in, w_out],
        [row, pl.BlockSpec((1, D), lambda i, c: (0, 0)),
         pl.BlockSpec((tm, tn), lambda i, c: (i, c)),
         pl.BlockSpec((tm, 2 * tn), lambda i, c: (i, c)), row],
        [jax.ShapeDtypeStruct((T, D), F32), jax.ShapeDtypeStruct((1, D), F32),
         jax.ShapeDtypeStruct((T, F), BF16), jax.ShapeDtypeStruct((T, 2 * F), BF16),
         jax.ShapeDtypeStruct((T, D), BF16)],
        [pltpu.VMEM((tm, D), F32)], ("arbitrary", "arbitrary"), job)


MM_TM = 512
MM_TC = 256


def mm_nt(a, wt, pieces, out_dtype, emit_a_bf16=False, norm_gain=None):
    T, K = a.shape
    tm = _tile(T, MM_TM)
    npc = len(pieces)
    n_lead = 1 if norm_gain is None else 2

    def body(*refs):
        a_ref, w_refs, o_refs = refs[0], refs[n_lead:n_lead + npc], refs[n_lead + npc:]
        av = a_ref[...]
        if norm_gain is not None:
            av = av * lax.rsqrt(jnp.mean(av * av, axis=-1, keepdims=True) + EPS) * refs[1][...]
        ab = av.astype(BF16)
        for w_ref, o_ref in zip(w_refs, o_refs[:npc]):
            o_ref[...] = _nt(ab, w_ref[...]).astype(o_ref.dtype)
        if emit_a_bf16:
            o_refs[npc][...] = ab

    in_specs = [pl.BlockSpec((tm, K), lambda i: (i, 0))]
    if norm_gain is not None:
        in_specs.append(pl.BlockSpec((1, K), lambda i: (0, 0)))
    out_specs, out_shape = [], []
    for r0, n in pieces:
        assert r0 % n == 0
        in_specs.append(pl.BlockSpec((n, K), functools.partial(lambda i, b: (b, 0), b=r0 // n)))
        out_specs.append(pl.BlockSpec((tm, n), lambda i: (i, 0)))
        out_shape.append(jax.ShapeDtypeStruct((T, n), out_dtype))
    if emit_a_bf16:
        out_specs.append(pl.BlockSpec((tm, K), lambda i: (i, 0)))
        out_shape.append(jax.ShapeDtypeStruct((T, K), BF16))
    return pl.pallas_call(
        body, name="mm_nt", grid=(T // tm,), in_specs=in_specs, out_specs=out_specs, out_shape=out_shape,
        compiler_params=_cp("parallel"),
    )(a, *([] if norm_gain is None else [norm_gain]), *([wt] * npc))


def mm_nn(pieces, w, residual=None, norm_bwd=None):
    T = pieces[0][0].shape[0]
    N = w.shape[1]
    tm = _tile(T, MM_TM)
    na = len(pieces)

    def body(*refs):
        a_refs, w_refs = refs[:na], refs[na:2 * na]
        acc = refs[2 * na][...] if residual is not None else None
        for a_ref, w_ref in zip(a_refs, w_refs):
            t = _nn(a_ref[...].astype(BF16), w_ref[...])
            acc = t if acc is None else acc + t
        if norm_bwd is None:
            refs[-1][...] = acc
            return
        x_ref, g_ref, dr_ref, dx_ref, dg_ref = refs[-5:]
        dx, dg = _rms_bwd_math(x_ref[...], g_ref[...], acc, dr_ref[...])
        dx_ref[...] = dx

        @pl.when(pl.program_id(0) == 0)
        def _():
            dg_ref[...] = jnp.zeros_like(dg_ref)

        dg_ref[...] += dg

    in_specs, w_specs = [], []
    for a, cb, k, rb in pieces:
        in_specs.append(pl.BlockSpec((tm, k), functools.partial(lambda i, b: (i, b), b=cb)))
        w_specs.append(pl.BlockSpec((k, N), functools.partial(lambda i, b: (b, 0), b=rb)))
    assert sum(k for _, _, k, _ in pieces) == w.shape[0]
    args = [a for a, _, _, _ in pieces] + [w] * na
    in_specs = in_specs + w_specs
    row = pl.BlockSpec((tm, N), lambda i: (i, 0))
    if residual is not None:
        in_specs.append(row)
        args.append(residual)
    if norm_bwd is None:
        return pl.pallas_call(
            body, name="mm_nn", grid=(T // tm,), in_specs=in_specs, out_specs=row,
            out_shape=jax.ShapeDtypeStruct((T, N), F32), compiler_params=_cp("parallel"),
        )(*args)
    vec = pl.BlockSpec((1, N), lambda i: (0, 0))
    return pl.pallas_call(
        body, name="mm_nn_norm_bwd", grid=(T // tm,), in_specs=in_specs + [row, vec, row], out_specs=[row, vec],
        out_shape=[jax.ShapeDtypeStruct((T, N), F32), jax.ShapeDtypeStruct((1, N), F32)],
        compiler_params=_cp("arbitrary"),
    )(*args, *norm_bwd)


def mm_tn(a, b, n_rows, row_block, prev=None, grid=None, col_block=None, job=None):
    T, M = a.shape
    N = b.shape[1]
    tc = MM_TC
    assert M % tc == 0 and n_rows % tc == 0
    if grid is None:
        grid, col_block = (M // tc,), (lambda j: j)

    def body(*refs):
        a_ref, b_ref, o_ref = refs[0], refs[1], refs[-1]
        o_ref[...] = _tn(a_ref[...], b_ref[...]).astype(BF16)

    in_specs = [pl.BlockSpec((T, tc), lambda *g: (0, col_block(*g))), pl.BlockSpec((T, N), lambda *g: (0, 0))]
    args = [a, b]
    aliases = {}
    if prev is not None:
        in_specs.append(pl.BlockSpec(memory_space=pl.ANY))
        args.append(prev)
        aliases = {2: 0}
    res, jres = _call(body, "mm_tn", grid, in_specs, args, [pl.BlockSpec((tc, N), lambda *g: (row_block(*g), 0))],
                      [jax.ShapeDtypeStruct((n_rows, N), BF16)], (), ["parallel"] * len(grid), job, aliases)
    return res[0] if job is None else (res[0], jres)


def conv_fwd(proj_a, conv_w, n_ex):
    T, C3 = proj_a.shape
    C = C3 // 3
    S = T // n_ex

    def body(gb_ref, gc_ref, hc_ref, w_ref, o_ref):
        o_ref[...] = _conv_gate(gb_ref[...], gc_ref[...], hc_ref[...], w_ref[...]).astype(BF16)

    col = lambda k: pl.BlockSpec((S, C), functools.partial(lambda b, kk: (b, kk), kk=k))
    return pl.pallas_call(
        body, name="conv_fwd", grid=(n_ex,),
        in_specs=[col(0), col(1), col(2), pl.BlockSpec((3, C), lambda b: (0, 0))],
        out_specs=pl.BlockSpec((S, C), lambda b: (b, 0)),
        out_shape=jax.ShapeDtypeStruct((T, 2 * C), BF16), compiler_params=_cp("parallel"),
    )(proj_a, proj_a, proj_a, conv_w)


def conv_bwd(proj_a, conv_w, n_ex, dy, total_cols):
    T, C3 = proj_a.shape
    C = C3 // 3
    S = T // n_ex

    def body(gb_ref, gc_ref, hc_ref, w_ref, dy_ref, dp_ref, dw_ref):
        _, vjp = jax.vjp(_conv_gate, gb_ref[...], gc_ref[...], hc_ref[...], w_ref[...])
        dgb, dgc, dhc, dw = vjp(dy_ref[...].astype(F32))
        dp_ref[:, 0:C] = dgb.astype(BF16)
        dp_ref[:, C:2 * C] = dgc.astype(BF16)
        dp_ref[:, 2 * C:] = dhc.astype(BF16)

        @pl.when(pl.program_id(0) == 0)
        def _():
            dw_ref[...] = jnp.zeros_like(dw_ref)

        dw_ref[...] += dw

    col = lambda k: pl.BlockSpec((S, C), functools.partial(lambda b, kk: (b, kk), kk=k))
    return pl.pallas_call(
        body, name="conv_bwd", grid=(n_ex,),
        in_specs=[col(0), col(1), col(2), pl.BlockSpec((3, C), lambda b: (0, 0)),
                  pl.BlockSpec((S, C), lambda b: (b, 0))],
        out_specs=[pl.BlockSpec((S, C3), lambda b: (b, 0)), pl.BlockSpec((3, C), lambda b: (0, 0))],
        out_shape=[jax.ShapeDtypeStruct((T, total_cols), BF16), jax.ShapeDtypeStruct((3, C), F32)],
        compiler_params=_cp("arbitrary"),
    )(proj_a, proj_a, proj_a, conv_w, dy)


QW = N_Q_HEADS * HEAD_DIM
KW = N_KV_HEADS * HEAD_DIM
QP = N_Q_HEADS * LANES
KP = N_KV_HEADS * LANES


def _attn_consts(seq):
    rows = seq // GRID_W
    r_idx, c_idx = jnp.meshgrid(jnp.arange(rows), jnp.arange(GRID_W), indexing='ij')
    r_idx = r_idx.reshape(-1).astype(F32)
    c_idx = c_idx.reshape(-1).astype(F32)
    n_freq = HEAD_DIM // 4
    inv = ROPE_THETA ** (-jnp.arange(n_freq, dtype=F32) / n_freq)
    ang = jnp.concatenate([r_idx[:, None] * inv, c_idx[:, None] * inv], axis=-1)
    cos = jnp.repeat(jnp.cos(ang), 2, axis=1)
    sin = jnp.repeat(jnp.sin(ang), 2, axis=1)
    sgn = sin * jnp.tile(jnp.array([-1.0, 1.0], F32), HEAD_DIM // 2)
    cos = jnp.tile(cos, (1, N_Q_HEADS))
    sgn = jnp.tile(sgn, (1, N_Q_HEADS))
    lane = jnp.arange(QW)
    bd = jnp.where(lane[:, None] // HEAD_DIM == lane[None, :] // HEAD_DIM, 1.0 / HEAD_DIM, 0.0).astype(BF16)
    dst = (lane // HEAD_DIM) * LANES + lane % HEAD_DIM
    spread = (dst[:, None] == jnp.arange(QP)[None, :]).astype(BF16)
    return dict(cos=cos, sgn=sgn, bd=bd, spread=spread, gather=spread.T)


def qkv_prep_fwd(proj_b, qg, kg, cst, n_ex):
    T = proj_b.shape[0]
    S = T // n_ex
    tm = _tile(S, 512)
    nb = S // tm

    def body(p_ref, qg_ref, kg_ref, cos_ref, sgn_ref, bd_ref, sp_ref, q_ref, k_ref, v_ref):
        pv = p_ref[...]
        cos, sgn, bd, sp = cos_ref[...], sgn_ref[...], bd_ref[...], sp_ref[...]
        qr = _rope_norm(pv[:, :QW], qg_ref[...], cos, sgn, bd, HEAD_DIM ** -0.5)
        kr = _rope_norm(pv[:, QW:QW + KW], kg_ref[...], cos[:, :KW], sgn[:, :KW], bd[:KW, :KW], 1.0)
        q_ref[...] = _nn(qr.astype(BF16), sp).astype(BF16)
        k_ref[...] = _nn(kr.astype(BF16), sp[:KW, :KP]).astype(BF16)
        v_ref[...] = _nn(pv[:, QW + KW:].astype(BF16), sp[:KW, :KP]).astype(BF16)

    full = lambda a: pl.BlockSpec(a.shape, lambda i: (0,) * a.ndim)
    tab = pl.BlockSpec((tm, QW), lambda i: (i % nb, 0))
    return pl.pallas_call(
        body, name="qkv_prep_fwd", grid=(T // tm,),
        in_specs=[pl.BlockSpec((tm, QW + 2 * KW), lambda i: (i, 0)), full(qg), full(kg), tab, tab,
                  full(cst["bd"]), full(cst["spread"])],
        out_specs=[pl.BlockSpec((tm, QP), lambda i: (i, 0)), pl.BlockSpec((tm, KP), lambda i: (i, 0)),
                   pl.BlockSpec((tm, KP), lambda i: (i, 0))],
        out_shape=[jax.ShapeDtypeStruct((T, QP), BF16), jax.ShapeDtypeStruct((T, KP), BF16),
                   jax.ShapeDtypeStruct((T, KP), BF16)],
        compiler_params=_cp("parallel"),
    )(proj_b, qg, kg, cst["cos"], cst["sgn"], cst["bd"], cst["spread"])


def qkv_prep_bwd(proj_b, qg, kg, cst, n_ex, dq, dk_pad, dv_pad, d_proj):
    T = proj_b.shape[0]
    S = T // n_ex
    tm = _tile(S, 512)
    nb = S // tm

    def body(p_ref, qg_ref, kg_ref, cos_ref, sgn_ref, bd_ref, ga_ref, dq_ref, dk_ref, dv_ref, _kept,
             dp_ref, dqg_ref, dkg_ref):
        pv = p_ref[...]
        cos, sgn, bd, ga = cos_ref[...], sgn_ref[...], bd_ref[...], ga_ref[...]
        fq = lambda q, g: _rope_norm(q, g, cos, sgn, bd, HEAD_DIM ** -0.5)
        fk = lambda k, g: _rope_norm(k, g, cos[:, :KW], sgn[:, :KW], bd[:KW, :KW], 1.0)
        _, vq = jax.vjp(fq, pv[:, :QW], qg_ref[...])
        _, vk = jax.vjp(fk, pv[:, QW:QW + KW], kg_ref[...])
        dqp, dqg = vq(dq_ref[...])
        dkp, dkg = vk(_split_mm(dk_ref[...], ga[:KP, :KW]))
        dp_ref[:, :QW] = dqp.astype(BF16)
        dp_ref[:, QW:QW + KW] = dkp.astype(BF16)
        dp_ref[:, QW + KW:] = _split_mm(dv_ref[...], ga[:KP, :KW]).astype(BF16)

        @pl.when(pl.program_id(0) == 0)
        def _():
            dqg_ref[...] = jnp.zeros_like(dqg_ref)
            dkg_ref[...] = jnp.zeros_like(dkg_ref)

        dqg_ref[...] += dqg
        dkg_ref[...] += dkg

    full = lambda a: pl.BlockSpec(a.shape, lambda i: (0,) * a.ndim)
    tab = pl.BlockSpec((tm, QW), lambda i: (i % nb, 0))
    row = lambda n: pl.BlockSpec((tm, n), lambda i: (i, 0))
    wb = QW + 2 * KW
    assert d_proj.shape[1] % wb == 0
    last = d_proj.shape[1] // wb - 1
    return pl.pallas_call(
        body, name="qkv_prep_bwd", grid=(T // tm,),
        in_specs=[row(wb), full(qg), full(kg), tab, tab, full(cst["bd"]), full(cst["gather"]),
                  row(QW), row(KP), row(KP), ANY],
        out_specs=[pl.BlockSpec((tm, wb), lambda i: (i, last)), pl.BlockSpec((1, QW), lambda i: (0, 0)),
                   pl.BlockSpec((1, KW), lambda i: (0, 0))],
        out_shape=[jax.ShapeDtypeStruct(d_proj.shape, BF16), jax.ShapeDtypeStruct((1, QW), F32),
                   jax.ShapeDtypeStruct((1, KW), F32)],
        input_output_aliases={10: 0}, compiler_params=_cp("arbitrary"),
    )(proj_b, qg, kg, cst["cos"], cst["sgn"], cst["bd"], cst["gather"], dq, dk_pad, dv_pad, d_proj)


ATT_TQ = 256


def attn_fwd(qp, kp, vp, gather, n_ex, mix, job=None):
    T = qp.shape[0]
    S = T // n_ex
    tq = _tile(S, ATT_TQ)
    nq = S // tq

    def body(q_ref, k_ref, v_ref, ga_ref, _kept, o_ref, op_ref, lse_ref):
        lane = lax.broadcasted_iota(jnp.int32, (tq, LANES), 1)
        lse_all = jnp.zeros((tq, LANES), F32)
        for h in range(N_Q_HEADS):
            kv = h // Q_PER_KV
            qh = q_ref[:, h * LANES:(h + 1) * LANES]
            s = _nt(qh, k_ref[:, kv * LANES:(kv + 1) * LANES])
            m = jnp.max(s, axis=-1, keepdims=True)
            p = jnp.exp(s - m)
            lsum = jnp.sum(p, axis=-1, keepdims=True)
            o = _nn(p.astype(BF16), v_ref[:, kv * LANES:(kv + 1) * LANES]) / lsum
            op_ref[:, h * LANES:(h + 1) * LANES] = o.astype(BF16)
            lse_all = jnp.where(lane == h, m + jnp.log(lsum), lse_all)
        lse_ref[...] = lse_all
        o_ref[...] = _nn(op_ref[...], ga_ref[...]).astype(BF16)

    blk = lambda n: pl.BlockSpec((tq, n), lambda b, i: (b * nq + i, 0))
    kvs = pl.BlockSpec((S, KP), lambda b, i: (b, 0))
    return _call(
        body, "attn_fwd", (n_ex, nq),
        [blk(QP), kvs, kvs, pl.BlockSpec(gather.shape, lambda b, i: (0, 0)), ANY], [qp, kp, vp, gather, mix],
        [pl.BlockSpec((tq, QW), lambda b, i: (b * nq + i, 1)), blk(QP), blk(LANES)],
        [jax.ShapeDtypeStruct(mix.shape, BF16), jax.ShapeDtypeStruct((T, QP), BF16),
         jax.ShapeDtypeStruct((T, LANES), F32)], (), ("parallel", "parallel"), job, {4: 0})


def attn_bwd(qp, kp, vp, op, lse, do, cst, n_ex, job=None):
    T = qp.shape[0]
    S = T // n_ex
    tq = _tile(S, ATT_TQ)
    nq = S // tq

    def body(q_ref, k_ref, v_ref, op_ref, lse_ref, do_ref, sp_ref, ga_ref, dq_ref, dk_ref, dv_ref, dqp_s):
        @pl.when(pl.program_id(1) == 0)
        def _():
            dk_ref[...] = jnp.zeros_like(dk_ref)
            dv_ref[...] = jnp.zeros_like(dv_ref)

        lane = lax.broadcasted_iota(jnp.int32, (tq, LANES), 1)
        dop = _nn(do_ref[...], sp_ref[...]).astype(BF16)
        lse_all = lse_ref[...]
        for h in range(N_Q_HEADS):
            kv = h // Q_PER_KV
            hs = slice(h * LANES, (h + 1) * LANES)
            ks = slice(kv * LANES, (kv + 1) * LANES)
            qh, kk, vv = q_ref[:, hs], k_ref[:, ks], v_ref[:, ks]
            doh = dop[:, hs]
            lse_h = jnp.sum(jnp.where(lane == h, lse_all, 0.0), axis=-1, keepdims=True)
            p = jnp.exp(_nt(qh, kk) - lse_h)
            dp = _nt(doh, vv)
            delta = jnp.sum(doh.astype(F32) * op_ref[:, hs].astype(F32), axis=-1, keepdims=True)
            ds = (p * (dp - delta)).astype(BF16)
            dqp_s[:, hs] = _nn(ds, kk)
            dk_ref[:, ks] += _tn(ds, qh)
            dv_ref[:, ks] += _tn(p.astype(BF16), doh)
        dq_ref[...] = _split_mm(dqp_s[...], ga_ref[...])

    blk = lambda n: pl.BlockSpec((tq, n), lambda b, i: (b * nq + i, 0))
    kvs = pl.BlockSpec((S, KP), lambda b, i: (b, 0))
    full = lambda a: pl.BlockSpec(a.shape, lambda b, i: (0, 0))
    return _call(
        body, "attn_bwd", (n_ex, nq),
        [blk(QP), kvs, kvs, blk(QP), blk(LANES), pl.BlockSpec((tq, QW), lambda b, i: (b * nq + i, 1)),
         full(cst["spread"]), full(cst["gather"])],
        [qp, kp, vp, op, lse, do, cst["spread"], cst["gather"]],
        [blk(QW), kvs, kvs],
        [jax.ShapeDtypeStruct((T, QW), F32), jax.ShapeDtypeStruct((T, KP), F32), jax.ShapeDtypeStruct((T, KP), F32)],
        [pltpu.VMEM((tq, QP), F32)], ("arbitrary", "arbitrary"), job)


def pool_fwd(p, pool_w, scale, n_ex):
    T, W = p.shape
    S = T // n_ex

    def body(p_ref, w_ref, s_ref, o_ref):
        o_ref[...] = _pool_mix(p_ref[...], w_ref[...], s_ref[...]).astype(BF16)

    return pl.pallas_call(
        body, name="pool_fwd", grid=(n_ex,),
        in_specs=[pl.BlockSpec((S, W), lambda b: (b, 0)),
                  pl.BlockSpec(pool_w.shape, lambda b: (0, 0, 0)),
                  pl.BlockSpec((1, W), lambda b: (0, 0))],
        out_specs=pl.BlockSpec((S, W), lambda b: (b, 0)),
        out_shape=jax.ShapeDtypeStruct((T, 2 * W), BF16), compiler_params=_cp("parallel"),
    )(p, pool_w, scale)


def pool_bwd(p, pool_w, scale, n_ex, dy, d_proj):
    T, W = p.shape
    S = T // n_ex
    last = d_proj.shape[1] // W - 1

    def body(p_ref, w_ref, s_ref, dy_ref, _kept, dp_ref, dw_ref, ds_ref):
        _, vjp = jax.vjp(_pool_mix, p_ref[...], w_ref[...], s_ref[...])
        dp, dw, ds = vjp(dy_ref[...].astype(F32))
        dp_ref[...] = dp.astype(BF16)

        @pl.when(pl.program_id(0) == 0)
        def _():
            dw_ref[...] = jnp.zeros_like(dw_ref)
            ds_ref[...] = jnp.zeros_like(ds_ref)

        dw_ref[...] += dw
        ds_ref[...] += ds

    wshape = pool_w.shape
    return pl.pallas_call(
        body, name="pool_bwd", grid=(n_ex,),
        in_specs=[pl.BlockSpec((S, W), lambda b: (b, 0)),
                  pl.BlockSpec(wshape, lambda b: (0, 0, 0)),
                  pl.BlockSpec((1, W), lambda b: (0, 0)), pl.BlockSpec((S, W), lambda b: (b, 0)), ANY],
        out_specs=[pl.BlockSpec((S, W), lambda b: (b, last)), pl.BlockSpec(wshape, lambda b: (0, 0, 0)),
                   pl.BlockSpec((1, W), lambda b: (0, 0))],
        out_shape=[jax.ShapeDtypeStruct(d_proj.shape, BF16), jax.ShapeDtypeStruct(wshape, F32),
                   jax.ShapeDtypeStruct((1, W), F32)],
        input_output_aliases={4: 0}, compiler_params=_cp("arbitrary"),
    )(p, pool_w, scale, dy, d_proj)


SGU_TS = 512


def sgu_fwd(u, v, norm_g, w_s, b_full, mix):
    T, W = u.shape
    ts = _tile(T, SGU_TS)

    def body(u_ref, v_ref, g_ref, w_ref, b_ref, _kept, o_ref):
        o_ref[...] = _sgu(u_ref[...], v_ref[...], g_ref[...], w_ref[...], b_ref[...]).astype(BF16)

    row = pl.BlockSpec((ts, W), lambda i: (i, 0))
    wsp = pl.BlockSpec(w_s.shape, lambda i: (0, 0, 0))
    return pl.pallas_call(
        body, name="sgu_fwd", grid=(T // ts,),
        in_specs=[row, row, pl.BlockSpec((1, W), lambda i: (0, 0)), wsp, wsp, ANY],
        out_specs=pl.BlockSpec((ts, W), lambda i: (i, 1)), out_shape=jax.ShapeDtypeStruct(mix.shape, BF16),
        input_output_aliases={5: 0}, compiler_params=_cp("parallel"),
    )(u, v, norm_g, w_s, b_full, mix)


def sgu_bwd(u, v, norm_g, w_s, b_full, dy):
    T, W = u.shape
    ts = _tile(T, SGU_TS)
    wshape = w_s.shape

    def body(u_ref, v_ref, g_ref, w_ref, b_ref, dy_ref, duv_ref, dg_ref, dw_ref, db_ref):
        _, vjp = jax.vjp(_sgu, u_ref[...], v_ref[...], g_ref[...], w_ref[...], b_ref[...])
        du, dv, dg, dw, db = vjp(dy_ref[...].astype(F32))
        duv_ref[:, :W] = du.astype(BF16)
        duv_ref[:, W:] = dv.astype(BF16)

        @pl.when(pl.program_id(0) == 0)
        def _():
            dg_ref[...] = jnp.zeros_like(dg_ref)
            dw_ref[...] = jnp.zeros_like(dw_ref)
            db_ref[...] = jnp.zeros_like(db_ref)

        dg_ref[...] += dg
        dw_ref[...] += dw
        db_ref[...] += db

    row = pl.BlockSpec((ts, W), lambda i: (i, 0))
    wsp = pl.BlockSpec(wshape, lambda i: (0, 0, 0))
    wout = pl.BlockSpec(wshape, lambda i: (0, 0, 0))
    vec = pl.BlockSpec((1, W), lambda i: (0, 0))
    return pl.pallas_call(
        body, name="sgu_bwd", grid=(T // ts,),
        in_specs=[row, row, vec, wsp, wsp, pl.BlockSpec((ts, W), lambda i: (i, 1))],
        out_specs=[pl.BlockSpec((ts, 2 * W), lambda i: (i, 0)), vec, wout, wout],
        out_shape=[jax.ShapeDtypeStruct((T, 3 * W), BF16),
                   jax.ShapeDtypeStruct((1, W), F32), jax.ShapeDtypeStruct(wshape, F32),
                   jax.ShapeDtypeStruct(wshape, F32)],
        compiler_params=_cp("arbitrary"),
    )(u, v, norm_g, w_s, b_full, dy)


def loss_head(x, gain, target):
    T, D = x.shape
    tm = _tile(T, 512)

    def body(x_ref, g_ref, t_ref, loss_ref, dx_ref, dg_ref):
        xv, g = x_ref[...], g_ref[...]
        r = lax.rsqrt(jnp.mean(xv * xv, axis=-1, keepdims=True) + EPS)
        err = xv * r * g - t_ref[...]
        part = 0.5 * jnp.sum(jnp.mean(err * err, axis=-1, keepdims=True), axis=0, keepdims=True)
        dx, dg = _rms_bwd_math(xv, g, err * (1.0 / D), jnp.zeros_like(xv))
        dx_ref[...] = dx

        @pl.when(pl.program_id(0) == 0)
        def _():
            loss_ref[...] = jnp.zeros_like(loss_ref)
            dg_ref[...] = jnp.zeros_like(dg_ref)

        loss_ref[...] += part
        dg_ref[...] += dg

    row = pl.BlockSpec((tm, D), lambda i: (i, 0))
    vec = pl.BlockSpec((1, D), lambda i: (0, 0))
    return pl.pallas_call(
        body, name="loss_head", grid=(T // tm,),
        in_specs=[row, vec, row], out_specs=[pl.BlockSpec((1, 1), lambda i: (0, 0)), row, vec],
        out_shape=[jax.ShapeDtypeStruct((1, 1), F32), jax.ShapeDtypeStruct((T, D), F32),
                   jax.ShapeDtypeStruct((1, D), F32)],
        compiler_params=_cp("arbitrary"),
    )(x, gain, target)


class MultiJob:
    def __init__(self, jobs):
        self.jobs = jobs
        self.args = [a for j in jobs for a in j.args]
        self.out_shape = [s for j in jobs for s in j.out_shape]
        self.scratch = [s for j in jobs for s in j.scratch]
        self.n_in, self.n_out = len(self.args), len(self.out_shape)

    def _each(self, ins, outs, sems):
        i = o = s = 0
        for j in self.jobs:
            yield j, ins[i:i + j.n_in], outs[o:o + j.n_out], sems[s:s + len(j.scratch)]
            i, o, s = i + j.n_in, o + j.n_out, s + len(j.scratch)

    def start(self, ins, outs, sems):
        for j, a, b, c in self._each(ins, outs, sems):
            j.start(a, b, c)

    def middle(self, ins, outs, sems):
        for j, a, b, c in self._each(ins, outs, sems):
            j.middle(a, b, c)

    def finish(self, ins, outs, sems):
        for j, a, b, c in self._each(ins, outs, sems):
            j.finish(a, b, c)

    def split(self, results):
        o = 0
        for j in self.jobs:
            yield results[o:o + j.n_out]
            o += j.n_out


class Plan:
    def __init__(self, shard, gathers, scatters, small_carrier=None, pack_small=None):
        self.shard, self.gathers, self.scatters = shard, gathers, scatters
        self.small_carrier, self.pack_small = small_carrier, pack_small
        self.weights, self.grads, self.parts = {}, {}, {}
        self.small_src = self.small_parts = None

    def weight(self, kind, l):
        return self.weights[(kind, l)]

    def grad(self, kind, l, g):
        self.grads[(kind, l)] = g

    def small_ready(self, small, d_final):
        if self.pack_small is not None:
            self.small_src = self.pack_small(small, d_final)

    def _jobs(self, key):
        jobs = []
        if key in self.gathers:
            ks = self.gathers[key]
            jobs.append((GatherJob([self.shard(*k) for k in ks]), self.weights, ks))
        if key in self.scatters:
            ks = self.scatters[key]
            jobs.append((ScatterJob([self.grads[k] for k in ks]), self.parts, ks))
        if key == self.small_carrier and self.small_src is not None:
            jobs.append((GatherJob([(self.small_src, None)]), None, None))
        return jobs

    def _deliver(self, jobs, results):
        multi = MultiJob([j for j, _, _ in jobs])
        for (_, store, ks), res in zip(jobs, multi.split(results)):
            if store is None:
                self.small_parts = res[0]
            else:
                store.update(zip(ks, res))

    def run(self, key, fn, *args, **kw):
        jobs = self._jobs(key)
        if not jobs:
            out = fn(*args, **kw)
            return out if fn is mm_tn else out[0]
        res, jres = fn(*args, job=MultiJob([j for j, _, _ in jobs]), **kw)
        self._deliver(jobs, jres)
        return res

    def alone(self, key, name):
        jobs = self._jobs(key)
        if jobs:
            self._deliver(jobs, run_job(MultiJob([j for j, _, _ in jobs]), name))


def _local_step(x, target, layers, final_norm, n_ex, plan):
    T, D = x.shape
    L = len(layers)
    cst = _attn_consts(T // n_ex)
    ident = lambda j: j
    EV_A, EV_B = 3 * (D // 2), QW + 2 * KW
    OD_W = D // 2
    wt = plan.weight

    saved = []
    for l, W in enumerate(layers):
        s = dict(x0=x)
        x1, *s["gu1"] = plan.run(("ffn1_fwd", l), ffn_fwd, x, W["n1"], wt("f1_in_t", l), wt("f1_out", l))
        if l % 2 == 0:
            pa, pb, h = mm_nt(x1, wt("mx_in_t", l), [(0, EV_A), (EV_A, EV_B)], F32, True, W["nm"])
            qg = jnp.tile(W["q_norm"], N_Q_HEADS)[None]
            kg = jnp.tile(W["k_norm"], N_KV_HEADS)[None]
            mix = conv_fwd(pa, W["conv_w"], n_ex)
            qp, kp, vp = qkv_prep_fwd(pb, qg, kg, cst, n_ex)
            mix, op, lse = plan.run(("attn_fwd", l), attn_fwd, qp, kp, vp, cst["gather"], n_ex, mix)
            s.update(pa=pa, pb=pb, qg=qg, kg=kg, qp=qp, kp=kp, vp=vp, op=op, lse=lse)
        else:
            p, u, v, h = mm_nt(x1, wt("mx_in_t", l), [(0, OD_W), (OD_W, OD_W), (2 * OD_W, OD_W)], F32, True, W["nm"])
            scale = W["pool_scale"][None]
            sn = W["sgu_norm"][None]
            b_full = jnp.broadcast_to(W["sgu_b"][..., None], W["sgu_w"].shape)
            mix = sgu_fwd(u, v, sn, W["sgu_w"], b_full, pool_fwd(p, W["pool_w"], scale, n_ex))
            s.update(p=p, u=u, v=v, scale=scale, sn=sn, b_full=b_full)
        x2 = mm_nn([(mix, 0, D, 0)], wt("mx_out", l), residual=x1)
        x3, *s["gu2"] = plan.run(("ffn2_fwd", l), ffn_fwd, x2, W["n2"], wt("f2_in_t", l), wt("f2_out", l))
        s.update(x1=x1, x2=x2, h=h, mix=mix)
        saved.append(s)
        x = x3

    loss, dx, d_final = loss_head(x, final_norm, target)

    small = [None] * L

    def ffn_back(which, l, dout, xin, gain, gu_xn, sm, sm_key):
        w_in, w_out = wt(which + "_in_t", l), wt(which + "_out", l)
        F = w_out.shape[0]
        nc = F // FFN_TN
        gu, xn = gu_xn
        dxi, sm[sm_key], a, dgu, dob = plan.run((which + "_bwd", l), ffn_bwd_x, dout, xin, gain, gu, w_in, w_out)
        plan.grad(which + "_out", l, plan.run((which + "_out_grad", l), mm_tn, a, dob, F, ident))
        if which == "f1" and l == 0:
            plan.small_ready(small, d_final)
        plan.grad(which + "_in_t", l, plan.run(
            (which + "_in_grad", l), mm_tn, dgu, xn, 2 * F, lambda k, c: k * nc + c, grid=(2, nc),
            col_block=lambda k, c: 2 * c + k))
        return dxi

    for l in reversed(range(L)):
        s, W = saved[l], layers[l]
        sm = small[l] = {}
        dx = ffn_back("f2", l, dx, s["x2"], W["n2"], s["gu2"], sm, "n2")
        dmix, dxb = mm_nt(dx, wt("mx_out", l), [(0, D)], BF16, emit_a_bf16=True)
        plan.grad("mx_out", l, mm_tn(s["mix"], dxb, D, ident))
        if l % 2 == 0:
            d_proj, sm["conv_w"] = conv_bwd(s["pa"], W["conv_w"], n_ex, dmix, EV_A + EV_B)
            dq, dkp, dvp = plan.run(("attn_bwd", l), attn_bwd, s["qp"], s["kp"], s["vp"], s["op"], s["lse"], dmix, cst, n_ex)
            d_proj, dqg, dkg = qkv_prep_bwd(s["pb"], s["qg"], s["kg"], cst, n_ex, dq, dkp, dvp, d_proj)
            d_pieces = [(d_proj, 0, EV_A + EV_B, 0)]
            plan.grad("mx_in_t", l, mm_tn(d_proj, s["h"], EV_A + EV_B, ident))
            sm["q_norm"] = dqg.reshape(N_Q_HEADS, HEAD_DIM).sum(0)
            sm["k_norm"] = dkg.reshape(N_KV_HEADS, HEAD_DIM).sum(0)
        else:
            d_proj, d_sn, sm["sgu_w"], d_sb = sgu_bwd(s["u"], s["v"], s["sn"], W["sgu_w"], s["b_full"], dmix)
            d_proj, sm["pool_w"], d_ps = pool_bwd(s["p"], W["pool_w"], s["scale"], n_ex, dmix, d_proj)
            d_pieces = [(d_proj, 0, OD_W, 1), (d_proj, 1, OD_W, 2), (d_proj, 2, OD_W, 0)]
            nb = OD_W // MM_TC
            plan.grad("mx_in_t", l, mm_tn(d_proj, s["h"], 3 * OD_W,
                                          lambda jj: jnp.where(jj < 2 * nb, jj + nb, jj - 2 * nb)))
            sm["pool_scale"], sm["sgu_norm"], sm["sgu_b"] = d_ps[0], d_sn[0], d_sb.sum(-1)
        dx, sm["nm"] = mm_nn(d_pieces, wt("mx_in_t", l), norm_bwd=(s["x1"], W["nm"], dx))
        dx = ffn_back("f1", l, dx, s["x0"], W["n1"], s["gu1"], sm, "n1")
    return loss, dx


def all_gather(srcs):
    return run_job(GatherJob(srcs), "all_gather")


def cast_shards(w):
    L, A, B = w.shape

    def body(w_ref, o_ref):
        o_ref[...] = w_ref[...].astype(BF16)

    return pl.pallas_call(
        body, name="cast_shards", grid=(L,),
        in_specs=[pl.BlockSpec((None, A, B), lambda l: (l, 0, 0))],
        out_specs=pl.BlockSpec((None, A, B), lambda l: (l, 0, 0)),
        out_shape=jax.ShapeDtypeStruct((L, A, B), BF16), compiler_params=_cp("parallel"),
    )(w)


ADAM_TC = 256


def adamw(parts, w, m, v, l, prev=None):
    P, R, C = parts.shape
    tc = _tile(C, ADAM_TC)
    c1, c2 = 1.0 - ADAM_B1 ** ADAM_STEP, 1.0 - ADAM_B2 ** ADAM_STEP

    def body(p_ref, w_ref, m_ref, v_ref, g_ref, d_ref, mo_ref, vo_ref):
        g = p_ref[0].astype(F32)
        for s in range(1, P):
            g = g + p_ref[s].astype(F32)
        m1 = ADAM_B1 * m_ref[...] + (1.0 - ADAM_B1) * g
        v1 = ADAM_B2 * v_ref[...] + (1.0 - ADAM_B2) * (g * g)
        g_ref[...] = g
        mo_ref[...] = m1
        vo_ref[...] = v1
        d_ref[...] = -ADAM_LR * ((m1 / c1) / (jnp.sqrt(v1 / c2) + ADAM_EPS) + ADAM_WD * w_ref[...])

    wspec = pl.BlockSpec((None, R, tc), lambda i: (l, 0, i))
    prev = list(prev) if prev is not None else []
    return pl.pallas_call(
        lambda *refs: body(*refs[:4], *refs[4 + len(prev):]), name="adamw", grid=(C // tc,),
        in_specs=[pl.BlockSpec((P, R, tc), lambda i: (0, 0, i)), wspec, wspec, wspec] + [ANY] * len(prev),
        out_specs=[wspec] * 4, out_shape=[jax.ShapeDtypeStruct(w.shape, F32)] * 4,
        input_output_aliases={4 + i: i for i in range(len(prev))},
        compiler_params=_cp("parallel"),
    )(parts, w, m, v, *prev)


_WEIGHTS = ['ffn1_norm', 'ffn1_w_in', 'ffn1_w_out', 'mix_norm', 'ffn2_norm', 'ffn2_w_in', 'ffn2_w_out', 'ev_w_in',
            'ev_conv_w', 'ev_q_norm', 'ev_k_norm', 'ev_w_out', 'od_w_in', 'od_pool_w', 'od_pool_scale', 'od_sgu_norm',
            'od_sgu_w', 'od_sgu_b', 'od_w_out', 'final_norm']
_BIG = dict(ffn1_w_in=True, ffn1_w_out=False, ffn2_w_in=True, ffn2_w_out=False,
            ev_w_in=True, ev_w_out=False, od_w_in=True, od_w_out=False)
_SMALL_SHARDED = ['ev_conv_w', 'od_pool_scale', 'od_sgu_norm']
_SMALL = [n for n in _WEIGHTS if n not in _BIG]
_PACK_ROWS = 8 * LANES


_KINDS = ("f1_in_t", "f1_out", "mx_in_t", "mx_out", "f2_in_t", "f2_out")
_CARRIER_US = dict(ffn1_fwd=105, ffn2_fwd=105, attn_fwd=115, f2_bwd=135, f1_bwd=135, attn_bwd=205,
                   f2_out_grad=33, f1_out_grad=33, f2_in_grad=61, f1_in_grad=61)
_GATHER_US_PER_ROW, _SCATTER_US_PER_ROW, _SMALL_GATHER_US, _SLACK_US = 0.08, 0.176, 42, 10


def _schedule(L, rows):
    events = []
    for l in range(L):
        events += [("ffn1_fwd", l), ("mixer", l)] + ([("attn_fwd", l)] if l % 2 == 0 else []) + [("ffn2_fwd", l)]
    consumer = {"f1": "ffn1_fwd", "mx": "mixer", "f2": "ffn2_fwd"}
    queue = [(k, l) for l in range(L) for k in _KINDS]
    pos = {t: events.index((consumer[t[0][:2]], t[1])) for t in queue}
    gathers = {"first": [t for t in queue if pos[t] == 0]}
    queue = [t for t in queue if pos[t] > 0]
    carriers = [i for i, e in enumerate(events) if e[0] in _CARRIER_US]
    for i in carriers:
        budget, take = _CARRIER_US[events[i][0]], []
        later = [j for j in carriers if j > i]
        while queue:
            t = queue[0]
            cost = rows(*t) * _GATHER_US_PER_ROW
            forced = not any(j < pos[t] for j in later)
            if not forced and cost > budget:
                break
            take.append(queue.pop(0))
            budget -= cost
        if take:
            gathers[events[i]] = take
    assert not queue
    events = []
    for l in reversed(range(L)):
        events += [("f2_bwd", l), ("f2_out_grad", l), ("f2_in_grad", l), ("mx_out_ready", l)]
        events += [("attn_bwd", l)] if l % 2 == 0 else []
        events += [("mx_in_ready", l), ("f1_bwd", l), ("f1_out_grad", l), ("f1_in_grad", l)]
    made_by = {"f2_out": "f2_out_grad", "f2_in_t": "f2_in_grad", "mx_out": "mx_out_ready", "mx_in_t": "mx_in_ready",
               "f1_out": "f1_out_grad", "f1_in_t": "f1_in_grad"}
    small_carrier = ("f1_in_grad", 0)
    scatters, ready = {}, []
    for e in events:
        if e[0] in _CARRIER_US:
            budget, take = _CARRIER_US[e[0]] - (_SMALL_GATHER_US if e == small_carrier else 0), []
            while True:
                fits = [t for t in ready if rows(*t) * _SCATTER_US_PER_ROW <= budget + _SLACK_US]
                if not fits:
                    break
                t = max(fits, key=lambda u: rows(*u))
                budget -= rows(*t) * _SCATTER_US_PER_ROW
                ready.remove(t)
                take.append(t)
            if take:
                scatters[e] = take
        ready += [(k, e[1]) for k in _KINDS if made_by[k] == e[0]]
    scatters["last"] = ready
    return gathers, scatters, small_carrier


def _pack(arrs):
    flat = jnp.concatenate([a.reshape(-1) for a in arrs])
    pad = (-flat.shape[0]) % _PACK_ROWS
    return jnp.pad(flat, (0, pad)).reshape(-1, LANES)


def _unpack(buf, shapes):
    flat, out, off = buf.reshape(-1), [], 0
    for s in shapes:
        n = math.prod(s)
        out.append(flat[off:off + n].reshape(s))
        off += n
    return out


def _unshard_last(g, lead):
    nd = len(lead)
    return jnp.moveaxis(g, 0, nd).reshape(*lead, -1)


def kernel(x, ffn1_norm, ffn1_w_in, ffn1_w_out, mix_norm, ffn2_norm, ffn2_w_in, ffn2_w_out, ev_w_in, ev_conv_w, ev_q_norm, ev_k_norm, ev_w_out, od_w_in, od_pool_w, od_pool_scale, od_sgu_norm, od_sgu_w, od_sgu_b, od_w_out, final_norm, loss_target, m_ffn1_norm, m_ffn1_w_in, m_ffn1_w_out, m_mix_norm, m_ffn2_norm, m_ffn2_w_in, m_ffn2_w_out, m_ev_w_in, m_ev_conv_w, m_ev_q_norm, m_ev_k_norm, m_ev_w_out, m_od_w_in, m_od_pool_w, m_od_pool_scale, m_od_sgu_norm, m_od_sgu_w, m_od_sgu_b, m_od_w_out, m_final_norm, v_ffn1_norm, v_ffn1_w_in, v_ffn1_w_out, v_mix_norm, v_ffn2_norm, v_ffn2_w_in, v_ffn2_w_out, v_ev_w_in, v_ev_conv_w, v_ev_q_norm, v_ev_k_norm, v_ev_w_out, v_od_w_in, v_od_pool_w, v_od_pool_scale, v_od_sgu_norm, v_od_sgu_w, v_od_sgu_b, v_od_w_out, v_final_norm):
    w = dict(zip(_WEIGHTS, (ffn1_norm, ffn1_w_in, ffn1_w_out, mix_norm, ffn2_norm, ffn2_w_in, ffn2_w_out, ev_w_in, ev_conv_w, ev_q_norm, ev_k_norm, ev_w_out, od_w_in, od_pool_w, od_pool_scale, od_sgu_norm, od_sgu_w, od_sgu_b, od_w_out, final_norm)))
    m = dict(zip(_WEIGHTS, (m_ffn1_norm, m_ffn1_w_in, m_ffn1_w_out, m_mix_norm, m_ffn2_norm, m_ffn2_w_in, m_ffn2_w_out, m_ev_w_in, m_ev_conv_w, m_ev_q_norm, m_ev_k_norm, m_ev_w_out, m_od_w_in, m_od_pool_w, m_od_pool_scale, m_od_sgu_norm, m_od_sgu_w, m_od_sgu_b, m_od_w_out, m_final_norm)))
    v = dict(zip(_WEIGHTS, (v_ffn1_norm, v_ffn1_w_in, v_ffn1_w_out, v_mix_norm, v_ffn2_norm, v_ffn2_w_in, v_ffn2_w_out, v_ev_w_in, v_ev_conv_w, v_ev_q_norm, v_ev_k_norm, v_ev_w_out, v_od_w_in, v_od_pool_w, v_od_pool_scale, v_od_sgu_norm, v_od_sgu_w, v_od_sgu_b, v_od_w_out, v_final_norm)))
    n_ex, seq, D = x.shape
    T = n_ex * seq
    L = ffn1_norm.shape[0]
    me = 4 * lax.axis_index("x") + 2 * lax.axis_index("y") + lax.axis_index("c")

    sh_small = [w[n] for n in _SMALL_SHARDED]
    packed = all_gather([(_pack(sh_small), None)])[0].reshape(N_DEV, -1)
    full_small = {}
    off = 0
    for n, a in zip(_SMALL_SHARDED, sh_small):
        cnt = math.prod(a.shape)
        full_small[n] = _unshard_last(packed[:, off:off + cnt].reshape((N_DEV,) + a.shape), a.shape[:-1])
        off += cnt

    tr = lambda a: jnp.swapaxes(a, 1, 2)
    wmv = {n: tuple(tr(d[n]) if t else d[n] for d in (w, m, v)) for n, t in _BIG.items()}
    shards = {n: cast_shards(wmv[n][0]) for n in _BIG}

    def name_of(kind, l):
        mx = "ev" if l % 2 == 0 else "od"
        return {"f1_in_t": "ffn1_w_in", "f1_out": "ffn1_w_out", "f2_in_t": "ffn2_w_in", "f2_out": "ffn2_w_out",
                "mx_in_t": mx + "_w_in", "mx_out": mx + "_w_out"}[kind], (l // 2 if kind.startswith("mx") else l)

    def shard(kind, l):
        name, idx = name_of(kind, l)
        return shards[name], idx

    g_shapes = {}

    def pack_small(small, d_final):
        ev = [sm for l, sm in enumerate(small) if l % 2 == 0]
        od = [sm for l, sm in enumerate(small) if l % 2 == 1]
        st = lambda sms, k: jnp.stack([sm[k] for sm in sms])
        g_full = dict(ffn1_norm=st(small, "n1")[:, 0], mix_norm=st(small, "nm")[:, 0], ffn2_norm=st(small, "n2")[:, 0],
                      ev_conv_w=st(ev, "conv_w"), ev_q_norm=st(ev, "q_norm"), ev_k_norm=st(ev, "k_norm"),
                      od_pool_w=st(od, "pool_w"), od_pool_scale=st(od, "pool_scale"), od_sgu_norm=st(od, "sgu_norm"),
                      od_sgu_w=st(od, "sgu_w"), od_sgu_b=st(od, "sgu_b"), final_norm=d_final[0])
        g_shapes.update({n: g_full[n].shape for n in _SMALL})
        return _pack([g_full[n] for n in _SMALL])

    gathers, scatters, small_carrier = _schedule(L, lambda kind, l: shards[name_of(kind, l)[0]].shape[1])
    plan = Plan(shard, gathers, scatters, small_carrier, pack_small)
    layers = []
    for l in range(L):
        j = l // 2
        W = dict(n1=ffn1_norm[l][None], nm=mix_norm[l][None], n2=ffn2_norm[l][None])
        if l % 2 == 0:
            W.update(conv_w=full_small["ev_conv_w"][j], q_norm=ev_q_norm[j], k_norm=ev_k_norm[j])
        else:
            W.update(pool_w=od_pool_w[j], pool_scale=full_small["od_pool_scale"][j], sgu_norm=full_small["od_sgu_norm"][j],
                     sgu_w=od_sgu_w[j], sgu_b=od_sgu_b[j])
        layers.append(W)

    plan.alone("first", "gather_first")
    loss, dx = _local_step(x.reshape(T, D), loss_target.reshape(T, D), layers, final_norm[None], n_ex, plan)
    plan.alone("last", "scatter_last")

    out = {n: None for n in _BIG}
    for (kind, l), parts in plan.parts.items():
        name, idx = name_of(kind, l)
        out[name] = adamw(parts, *wmv[name], idx, prev=out[name])
    out = {n: [tr(a) if _BIG[n] else a for a in res] for n, res in out.items()}

    g8 = plan.small_parts.reshape(N_DEV, -1)
    cols, off = [], 0
    for n in _SMALL:
        cnt = math.prod(g_shapes[n])
        g = g8[:, off:off + cnt].reshape((N_DEV,) + g_shapes[n])
        off += cnt
        if n in _SMALL_SHARDED:
            width = w[n].shape[-1]
            g = lax.dynamic_slice_in_dim(g, me * width, width, axis=g.ndim - 1)
        cols.append(g.reshape(N_DEV, -1))
    g8 = jnp.concatenate(cols, axis=1)
    g8 = jnp.pad(g8, ((0, 0), (0, (-g8.shape[1]) % _PACK_ROWS))).reshape(N_DEV, -1, LANES)
    pk = lambda d: _pack([d[n] for n in _SMALL])[None]
    small_out = adamw(g8, pk(w), pk(m), pk(v), 0)
    shapes = [w[n].shape for n in _SMALL]
    for i in range(4):
        for n, a in zip(_SMALL, _unpack(small_out[i], shapes)):
            out.setdefault(n, [None] * 4)[i] = a

    total = lax.psum(loss[0, 0], ("x", "y", "c"))
    return (total, dx.reshape(n_ex, seq, D), *[out[n][0] for n in _WEIGHTS], *[out[n][1] for n in _WEIGHTS],
            *[out[n][2] for n in _WEIGHTS], *[out[n][3] for n in _WEIGHTS])
```

```python
import functools
import math

import jax
import jax.numpy as jnp
from jax import lax
from jax.experimental import pallas as pl
from jax.experimental.pallas import tpu as pltpu

F32, BF16 = jnp.float32, jnp.bfloat16
EPS = 1e-6
N_DEV = 8
V7X_VMEM_BYTES = 64 * 1024 * 1024
VMEM_LIMIT = V7X_VMEM_BYTES - 8 * 1024 * 1024
LANES = 128
HEAD_DIM = 64
N_Q_HEADS = 8
N_KV_HEADS = 2
Q_PER_KV = N_Q_HEADS // N_KV_HEADS
GRID_W = 64
ROPE_THETA = 10000.0
POOL_RADII = (1, 2, 4, 8)
SGU_CHUNK = 128
GROUP = 128
ADAM_LR, ADAM_B1, ADAM_B2, ADAM_EPS, ADAM_WD, ADAM_STEP = 0.001, 0.9, 0.999, 1e-08, 0.01, 10
MESH_ID = pl.DeviceIdType.MESH


def _cp(*sem):
    return pltpu.CompilerParams(dimension_semantics=sem, vmem_limit_bytes=VMEM_LIMIT)


def _dot(a, b, ca, cb):
    return lax.dot_general(a, b, (((ca,), (cb,)), ((), ())), preferred_element_type=F32)


def _nn(a, b):
    return _dot(a, b, 1, 0)


def _nt(a, b):
    return _dot(a, b, 1, 1)


def _tn(a, b):
    return _dot(a, b, 0, 0)


def _split_mm(x, m):
    hi = x.astype(BF16)
    lo = (x - hi.astype(F32)).astype(BF16)
    return _nn(hi, m) + _nn(lo, m)


def _tile(n, pref):
    t = min(n, pref)
    assert n % t == 0, (n, pref)
    return t


ANY = pl.BlockSpec(memory_space=pl.ANY)
OTHER_CHIPS = (2, 4, 6)
JOB_MIDDLE = 0.55


def _my_place():
    x, y, c = lax.axis_index("x"), lax.axis_index("y"), lax.axis_index("c")
    return x, y, c, 4 * x + 2 * y + c


def _peer(x, y, c, k):
    px = 1 - x if k & 4 else x
    py = 1 - y if k & 2 else y
    pc = 1 - c if k & 1 else c
    return (px, py, pc), 4 * px + 2 * py + pc


def _remote(src, dst, send_sems, recv_sems, i, peer):
    return pltpu.make_async_remote_copy(src_ref=src, dst_ref=dst, send_sem=send_sems.at[i], recv_sem=recv_sems.at[i],
                                        device_id=peer, device_id_type=MESH_ID)


class GatherJob:
    def __init__(self, srcs):
        self.srcs = srcs
        self.args = [a for a, _ in srcs]
        self.dims = [a.shape[-2:] for a, _ in srcs]
        n = self.n_in = self.n_out = len(srcs)
        self.out_shape = [jax.ShapeDtypeStruct((N_DEV * r, cc), a.dtype) for (a, _), (r, cc) in zip(srcs, self.dims)]
        self.scratch = [pltpu.SemaphoreType.DMA((N_DEV * n,)), pltpu.SemaphoreType.DMA((N_DEV * n,)),
                        pltpu.SemaphoreType.DMA((n,))]

    def _rows(self, outs, t, idx):
        r = self.dims[t][0]
        return outs[t].at[pl.ds(pl.multiple_of(idx * r, 8), r), :]

    def _local(self, ins, outs, loc, t, me):
        src = ins[t] if self.srcs[t][1] is None else ins[t].at[self.srcs[t][1]]
        return src, pltpu.make_async_copy(src, self._rows(outs, t, me), loc.at[t])

    def start(self, ins, outs, sems):
        send, recv, loc = sems
        x, y, c, me = _my_place()
        for t in range(self.n_in):
            src, local = self._local(ins, outs, loc, t, me)
            local.start()
            for k in (2, 4, 1):
                _remote(src, self._rows(outs, t, me), send, recv, N_DEV * t + k, _peer(x, y, c, k)[0]).start()

    def _copy(self, outs, sems, t, origin, i, to):
        x, y, c, _ = _my_place()
        blk = self._rows(outs, t, _peer(x, y, c, origin)[1])
        return _remote(blk, blk, sems[0], sems[1], N_DEV * t + i, _peer(x, y, c, to)[0])

    def middle(self, ins, outs, sems):
        c = _my_place()[2]

        def relay(t, got, to):
            self._copy(outs, sems, t, got, got, got).wait_recv()
            self._copy(outs, sems, t, got, 6, to).start()
            self._copy(outs, sems, t, to, to, to).wait_recv()

        for t in range(self.n_in):
            pl.when(c == 1)(functools.partial(relay, t, 2, 4))
            pl.when(c == 0)(functools.partial(relay, t, 4, 2))
            for k in (2, 4):
                self._copy(outs, sems, t, k, k + 1, 1).start()

    def finish(self, ins, outs, sems):
        send, recv, loc = sems
        x, y, c, me = _my_place()
        for t in range(self.n_in):
            self._copy(outs, sems, t, 6, 6, 6).wait_recv()
            self._copy(outs, sems, t, 6, 7, 1).start()
        for t in range(self.n_in):
            for k in range(1, N_DEV):
                peer, pidx = _peer(x, y, c, k)
                blk = self._rows(outs, t, pidx)
                if k % 2 == 1:
                    _remote(blk, blk, send, recv, N_DEV * t + k, peer).wait_recv()
                _remote(blk, blk, send, recv, N_DEV * t + k, peer).wait_send()
            self._local(ins, outs, loc, t, me)[1].wait()


class ScatterJob:
    def __init__(self, grads):
        self.args = list(grads)
        self.dims = [(g.shape[0] // N_DEV, g.shape[1]) for g in grads]
        n = self.n_in = self.n_out = len(grads)
        self.out_shape = [jax.ShapeDtypeStruct((N_DEV, r, cc), g.dtype) for g, (r, cc) in zip(grads, self.dims)]
        self.scratch = [pltpu.SemaphoreType.DMA((N_DEV * n,)), pltpu.SemaphoreType.DMA((N_DEV * n,)),
                        pltpu.SemaphoreType.DMA((n,))]

    def _rows(self, ins, t, idx):
        r = self.dims[t][0]
        return ins[t].at[pl.ds(pl.multiple_of(idx * r, 8), r), :]

    def start(self, ins, outs, sems):
        send, recv, loc = sems
        x, y, c, me = _my_place()
        for t in range(self.n_in):
            pltpu.make_async_copy(self._rows(ins, t, me), outs[t].at[me], loc.at[t]).start()
            for k in OTHER_CHIPS + (1, 3, 5, 7):
                peer, pidx = _peer(x, y, c, k)
                _remote(self._rows(ins, t, pidx), outs[t].at[me], send, recv, N_DEV * t + k, peer).start()

    def middle(self, ins, outs, sems):
        pass

    def finish(self, ins, outs, sems):
        send, recv, loc = sems
        x, y, c, me = _my_place()
        for t in range(self.n_in):
            for k in range(1, N_DEV):
                peer, pidx = _peer(x, y, c, k)
                cp = _remote(self._rows(ins, t, pidx), outs[t].at[pidx], send, recv, N_DEV * t + k, peer)
                cp.wait_recv()
                cp.wait_send()
            pltpu.make_async_copy(self._rows(ins, t, me), outs[t].at[me], loc.at[t]).wait()


def _call(body, name, grid, in_specs, args, out_specs, out_shape, scratch=(), sem=(), job=None, aliases=None):
    in_specs, out_specs, out_shape, scratch = list(in_specs), list(out_specs), list(out_shape), list(scratch)
    n_in, n_out, n_scr = len(args), len(out_shape), len(scratch)
    if job is None:
        res = pl.pallas_call(body, name=name, grid=grid, in_specs=in_specs, out_specs=out_specs, out_shape=out_shape,
                             scratch_shapes=scratch, input_output_aliases=aliases or {}, compiler_params=_cp(*sem))(*args)
        return res, None
    o0 = n_in + job.n_in
    s0 = o0 + n_out + job.n_out

    def carrier(*refs):
        jin, jout, jsem = refs[n_in:o0], refs[o0 + n_out:s0], refs[s0 + n_scr:]
        ids = [pl.program_id(a) for a in range(len(grid))]
        def at(step):
            idx = []
            for g in reversed(grid):
                idx.append(step % g)
                step //= g
            return functools.reduce(jnp.logical_and, [i == j for i, j in zip(ids, reversed(idx))])

        steps = math.prod(grid)
        if grid:
            pl.when(at(0))(lambda: job.start(jin, jout, jsem))
            pl.when(at(int(steps * JOB_MIDDLE)))(lambda: job.middle(jin, jout, jsem))
        else:
            job.start(jin, jout, jsem)
            job.middle(jin, jout, jsem)
        body(*refs[:n_in], *refs[o0:o0 + n_out], *refs[s0:s0 + n_scr])
        if grid:
            pl.when(at(steps - 1))(lambda: job.finish(jin, jout, jsem))
        else:
            job.finish(jin, jout, jsem)

    res = pl.pallas_call(
        carrier, name=name + "_comm", grid=grid, in_specs=in_specs + [ANY] * job.n_in,
        out_specs=out_specs + [ANY] * job.n_out, out_shape=out_shape + job.out_shape,
        scratch_shapes=scratch + job.scratch, input_output_aliases=aliases or {},
        compiler_params=_cp(*(["arbitrary"] * len(grid))))(*args, *job.args)
    return res[:n_out], res[n_out:]


def run_job(job, name):
    return _call(lambda: None, name, (), [], [], [], [], job=job)[1]


@jax.custom_vjp
def bmm(x, w):
    return _nn(x.astype(BF16), w.astype(BF16))


def _bmm_fwd(x, w):
    return bmm(x, w), (x, w)


def _bmm_bwd(res, g):
    x, w = res
    gb = g.astype(BF16)
    return _nt(gb, w.astype(BF16)), _tn(x.astype(BF16), gb)


bmm.defvjp(_bmm_fwd, _bmm_bwd)


def _shift_raw(x, d):
    n = x.shape[0]
    r = pltpu.roll(x, d % n, axis=0)
    row = lax.broadcasted_iota(jnp.int32, x.shape, 0)
    keep = (row >= d) if d > 0 else (row < n + d)
    return jnp.where(keep, r, 0.0)


def shift_rows(x, d):
    @jax.custom_vjp
    def f(v):
        return _shift_raw(v, d)

    f.defvjp(lambda v: (_shift_raw(v, d), None), lambda _, g: (_shift_raw(g, -d),))
    return f(x)


def _swap_raw(x):
    n = x.shape[1]
    nxt = pltpu.roll(x, n - 1, axis=1)
    prv = pltpu.roll(x, 1, axis=1)
    lane = lax.broadcasted_iota(jnp.int32, x.shape, 1)
    return jnp.where(lane % 2 == 0, nxt, prv)


@jax.custom_vjp
def swap_pairs(x):
    return _swap_raw(x)


swap_pairs.defvjp(lambda x: (_swap_raw(x), None), lambda _, g: (_swap_raw(g),))


@jax.custom_vjp
def group_mean(x, bd):
    return _split_mm(x, bd)


group_mean.defvjp(lambda x, bd: (_split_mm(x, bd), bd), lambda bd, g: (_split_mm(g, bd), jnp.zeros_like(bd)))


def _rope_norm(x, gain, cos, sgn, bd, scale):
    xn = x * lax.rsqrt(group_mean(x * x, bd) + EPS) * gain
    return (xn * cos + swap_pairs(xn) * sgn) * scale


def _conv_gate(gb, gc, hc, w):
    z = gc * hc
    c = shift_rows(z, 1) * w[0:1] + z * w[1:2] + shift_rows(z, -1) * w[2:3]
    return gb * c


def _window_sum(p, r):
    b = f = p
    k = 1
    while k < r:
        b = b + shift_rows(b, k)
        f = f + shift_rows(f, -k)
        k *= 2
    return b + f - p + shift_rows(p, r) + shift_rows(p, -r)


def _pool_mix(p, pool_w, scale):
    n = p.shape[0]
    t = lax.broadcasted_iota(jnp.int32, (n, 1), 0)
    outs = []
    for gi, r in enumerate(POOL_RADII):
        pg = p[:, gi * GROUP:(gi + 1) * GROUP]
        cnt = (jnp.minimum(t + r, n - 1) - jnp.maximum(t - r, 0) + 1).astype(F32)
        pooled = _window_sum(pg, r) / cnt - pg
        outs.append(bmm(pooled, pool_w[gi]))
    return jnp.concatenate(outs, axis=1) * scale


def _sgu(u, v, norm_g, w_s, b_full):
    ug = jax.nn.gelu(u)
    vg = jax.nn.gelu(v)
    vn = vg * lax.rsqrt(jnp.mean(vg * vg, axis=-1, keepdims=True) + EPS) * norm_g
    cols = []
    for g in range(w_s.shape[0]):
        rows = []
        for n in range(u.shape[0] // SGU_CHUNK):
            blk = vn[n * SGU_CHUNK:(n + 1) * SGU_CHUNK, g * GROUP:(g + 1) * GROUP]
            rows.append(bmm(w_s[g], blk) + b_full[g])
        cols.append(jnp.concatenate(rows, axis=0))
    return ug * jnp.concatenate(cols, axis=1)


def _rms_bwd_math(xv, gain, dy, dres):
    r = lax.rsqrt(jnp.mean(xv * xv, axis=-1, keepdims=True) + EPS)
    xh = xv * r
    dxh = dy * gain
    dx = dres + r * (dxh - xh * jnp.mean(dxh * xh, axis=-1, keepdims=True))
    return dx, jnp.sum(dy * xh, axis=0, keepdims=True)


FFN_TN = 256


def ffn_fwd(x, gain, wt_in, w_out, job=None):
    T, D = x.shape
    F = w_out.shape[0]
    tm, tn = _tile(T, 1024), FFN_TN
    nc = F // tn

    def body(x_ref, gn_ref, wg_ref, wu_ref, wo_ref, y_ref, gu_ref, xn_s, acc_s):
        c = pl.program_id(1)

        @pl.when(c == 0)
        def _():
            xv = x_ref[...]
            r = lax.rsqrt(jnp.mean(xv * xv, axis=-1, keepdims=True) + EPS)
            xn_s[...] = (xv * r * gn_ref[...]).astype(BF16)
            acc_s[...] = jnp.zeros_like(acc_s)

        xn = xn_s[...]
        g = _nt(xn, wg_ref[...])
        u = _nt(xn, wu_ref[...])
        gu_ref[:, :tn] = g.astype(BF16)
        gu_ref[:, tn:] = u.astype(BF16)
        a = (g * jax.nn.sigmoid(g) * u).astype(BF16)
        acc_s[...] += _nn(a, wo_ref[...])

        @pl.when(c == nc - 1)
        def _():
            y_ref[...] = x_ref[...] + 0.5 * acc_s[...]

    row = pl.BlockSpec((tm, D), lambda i, c: (i, 0))
    return _call(
        body, "ffn_fwd", (T // tm, nc),
        [row, pl.BlockSpec((1, D), lambda i, c: (0, 0)),
         pl.BlockSpec((tn, D), lambda i, c: (c, 0)),
         pl.BlockSpec((tn, D), lambda i, c: (c + nc, 0)),
         pl.BlockSpec((tn, D), lambda i, c: (c, 0))],
        [x, gain, wt_in, wt_in, w_out],
        [row, pl.BlockSpec((tm, 2 * tn), lambda i, c: (i, c)), row],
        [jax.ShapeDtypeStruct((T, D), F32), jax.ShapeDtypeStruct((T, 2 * F), BF16), jax.ShapeDtypeStruct((T, D), BF16)],
        [pltpu.VMEM((tm, D), F32)], ("parallel", "arbitrary"), job)


def ffn_bwd_x(dout, x, gain, gu, wt_in, w_out, job=None):
    T, D = x.shape
    F = w_out.shape[0]
    tm, tn = _tile(T, 1024), FFN_TN
    nc = F // tn

    def body(do_ref, x_ref, gn_ref, gu_ref, wg_ref, wu_ref, wo_ref,
             dx_ref, dgn_ref, a_ref, dgu_ref, dob_ref, acc_s):
        i, c = pl.program_id(0), pl.program_id(1)

        @pl.when(c == 0)
        def _():
            dob_ref[...] = (0.5 * do_ref[...]).astype(BF16)
            acc_s[...] = jnp.zeros_like(acc_s)

        da = jnp.concatenate([_nt(dob_ref[:tm // 2, :], wo_ref[...]), _nt(dob_ref[tm // 2:, :], wo_ref[...])], axis=0)
        g = gu_ref[:, :tn].astype(F32)
        u = gu_ref[:, tn:].astype(F32)
        sig = jax.nn.sigmoid(g)
        sl = g * sig
        a_ref[...] = (sl * u).astype(BF16)
        dg = (da * u * (sig * (1.0 + g * (1.0 - sig)))).astype(BF16)
        du = (da * sl).astype(BF16)
        dgu_ref[:, :tn] = dg
        dgu_ref[:, tn:] = du
        acc_s[...] += _nn(dg, wg_ref[...]) + _nn(du, wu_ref[...])

        @pl.when(c == nc - 1)
        def _():
            dx, dgn = _rms_bwd_math(x_ref[...], gn_ref[...], acc_s[...], do_ref[...])
            dx_ref[...] = dx

            @pl.when(i == 0)
            def _():
                dgn_ref[...] = jnp.zeros_like(dgn_ref)

            dgn_ref[...] += dgn

    row = pl.BlockSpec((tm, D), lambda i, c: (i, 0))
    return _call(
        body, "ffn_bwd_x", (T // tm, nc),
        [row, row, pl.BlockSpec((1, D), lambda i, c: (0, 0)),
         pl.BlockSpec((tm, 2 * tn), lambda i, c: (i, c)),
         pl.BlockSpec((tn, D), lambda i, c: (c, 0)),
         pl.BlockSpec((tn, D), lambda i, c: (c + nc, 0)),
         pl.BlockSpec((tn, D), lambda i, c: (c, 0))],
        [dout, x, gain, gu, wt_in, wt_in, w_out],
        [row, pl.BlockSpec((1, D), lambda i, c: (0, 0)),
         pl.BlockSpec((tm, tn), lambda i, c: (i, c)),
         pl.BlockSpec((tm, 2 * tn), lambda i, c: (i, c)), row],
        [jax.ShapeDtypeStruct((T, D), F32), jax.ShapeDtypeStruct((1, D), F32),
         jax.ShapeDtypeStruct((T, F), BF16), jax.ShapeDtypeStruct((T, 2 * F), BF16),
         jax.ShapeDtypeStruct((T, D), BF16)],
        [pltpu.VMEM((tm, D), F32)], ("arbitrary", "arbitrary"), job)


MM_TM = 512
MM_TC = 256


def mm_nt(a, wt, pieces, out_dtype, emit_a_bf16=False, norm_gain=None):
    T, K = a.shape
    tm = _tile(T, MM_TM)
    npc = len(pieces)
    n_lead = 1 if norm_gain is None else 2

    def body(*refs):
        a_ref, w_refs, o_refs = refs[0], refs[n_lead:n_lead + npc], refs[n_lead + npc:]
        av = a_ref[...]
        if norm_gain is not None:
            av = av * lax.rsqrt(jnp.mean(av * av, axis=-1, keepdims=True) + EPS) * refs[1][...]
        ab = av.astype(BF16)
        for w_ref, o_ref in zip(w_refs, o_refs[:npc]):
            o_ref[...] = _nt(ab, w_ref[...]).astype(o_ref.dtype)
        if emit_a_bf16:
            o_refs[npc][...] = ab

    in_specs = [pl.BlockSpec((tm, K), lambda i: (i, 0))]
    if norm_gain is not None:
        in_specs.append(pl.BlockSpec((1, K), lambda i: (0, 0)))
    out_specs, out_shape = [], []
    for r0, n in pieces:
        assert r0 % n == 0
        in_specs.append(pl.BlockSpec((n, K), functools.partial(lambda i, b: (b, 0), b=r0 // n)))
        out_specs.append(pl.BlockSpec((tm, n), lambda i: (i, 0)))
        out_shape.append(jax.ShapeDtypeStruct((T, n), out_dtype))
    if emit_a_bf16:
        out_specs.append(pl.BlockSpec((tm, K), lambda i: (i, 0)))
        out_shape.append(jax.ShapeDtypeStruct((T, K), BF16))
    return pl.pallas_call(
        body, name="mm_nt", grid=(T // tm,), in_specs=in_specs, out_specs=out_specs, out_shape=out_shape,
        compiler_params=_cp("parallel"),
    )(a, *([] if norm_gain is None else [norm_gain]), *([wt] * npc))


def mm_nn(pieces, w, residual=None, norm_bwd=None):
    T = pieces[0][0].shape[0]
    N = w.shape[1]
    tm = _tile(T, MM_TM)
    na = len(pieces)

    def body(*refs):
        a_refs, w_refs = refs[:na], refs[na:2 * na]
        acc = refs[2 * na][...] if residual is not None else None
        for a_ref, w_ref in zip(a_refs, w_refs):
            t = _nn(a_ref[...].astype(BF16), w_ref[...])
            acc = t if acc is None else acc + t
        if norm_bwd is None:
            refs[-1][...] = acc
            return
        x_ref, g_ref, dr_ref, dx_ref, dg_ref = refs[-5:]
        dx, dg = _rms_bwd_math(x_ref[...], g_ref[...], acc, dr_ref[...])
        dx_ref[...] = dx

        @pl.when(pl.program_id(0) == 0)
        def _():
            dg_ref[...] = jnp.zeros_like(dg_ref)

        dg_ref[...] += dg

    in_specs, w_specs = [], []
    for a, cb, k, rb in pieces:
        in_specs.append(pl.BlockSpec((tm, k), functools.partial(lambda i, b: (i, b), b=cb)))
        w_specs.append(pl.BlockSpec((k, N), functools.partial(lambda i, b: (b, 0), b=rb)))
    assert sum(k for _, _, k, _ in pieces) == w.shape[0]
    args = [a for a, _, _, _ in pieces] + [w] * na
    in_specs = in_specs + w_specs
    row = pl.BlockSpec((tm, N), lambda i: (i, 0))
    if residual is not None:
        in_specs.append(row)
        args.append(residual)
    if norm_bwd is None:
        return pl.pallas_call(
            body, name="mm_nn", grid=(T // tm,), in_specs=in_specs, out_specs=row,
            out_shape=jax.ShapeDtypeStruct((T, N), F32), compiler_params=_cp("parallel"),
        )(*args)
    vec = pl.BlockSpec((1, N), lambda i: (0, 0))
    return pl.pallas_call(
        body, name="mm_nn_norm_bwd", grid=(T // tm,), in_specs=in_specs + [row, vec, row], out_specs=[row, vec],
        out_shape=[jax.ShapeDtypeStruct((T, N), F32), jax.ShapeDtypeStruct((1, N), F32)],
        compiler_params=_cp("arbitrary"),
    )(*args, *norm_bwd)


def mm_tn(a, b, n_rows, row_block, prev=None, grid=None, col_block=None, job=None):
    T, M = a.shape
    N = b.shape[1]
    tc = MM_TC
    assert M % tc == 0 and n_rows % tc == 0
    if grid is None:
        grid, col_block = (M // tc,), (lambda j: j)

    def body(*refs):
        a_ref, b_ref, o_ref = refs[0], refs[1], refs[-1]
        o_ref[...] = _tn(a_ref[...], b_ref[...]).astype(BF16)

    in_specs = [pl.BlockSpec((T, tc), lambda *g: (0, col_block(*g))), pl.BlockSpec((T, N), lambda *g: (0, 0))]
    args = [a, b]
    aliases = {}
    if prev is not None:
        in_specs.append(pl.BlockSpec(memory_space=pl.ANY))
        args.append(prev)
        aliases = {2: 0}
    res, jres = _call(body, "mm_tn", grid, in_specs, args, [pl.BlockSpec((tc, N), lambda *g: (row_block(*g), 0))],
                      [jax.ShapeDtypeStruct((n_rows, N), BF16)], (), ["parallel"] * len(grid), job, aliases)
    return res[0] if job is None else (res[0], jres)


def conv_fwd(proj_a, conv_w, n_ex):
    T, C3 = proj_a.shape
    C = C3 // 3
    S = T // n_ex

    def body(gb_ref, gc_ref, hc_ref, w_ref, o_ref):
        o_ref[...] = _conv_gate(gb_ref[...], gc_ref[...], hc_ref[...], w_ref[...]).astype(BF16)

    col = lambda k: pl.BlockSpec((S, C), functools.partial(lambda b, kk: (b, kk), kk=k))
    return pl.pallas_call(
        body, name="conv_fwd", grid=(n_ex,),
        in_specs=[col(0), col(1), col(2), pl.BlockSpec((3, C), lambda b: (0, 0))],
        out_specs=pl.BlockSpec((S, C), lambda b: (b, 0)),
        out_shape=jax.ShapeDtypeStruct((T, 2 * C), BF16), compiler_params=_cp("parallel"),
    )(proj_a, proj_a, proj_a, conv_w)


def conv_bwd(proj_a, conv_w, n_ex, dy, total_cols):
    T, C3 = proj_a.shape
    C = C3 // 3
    S = T // n_ex

    def body(gb_ref, gc_ref, hc_ref, w_ref, dy_ref, dp_ref, dw_ref):
        _, vjp = jax.vjp(_conv_gate, gb_ref[...], gc_ref[...], hc_ref[...], w_ref[...])
        dgb, dgc, dhc, dw = vjp(dy_ref[...].astype(F32))
        dp_ref[:, 0:C] = dgb.astype(BF16)
        dp_ref[:, C:2 * C] = dgc.astype(BF16)
        dp_ref[:, 2 * C:] = dhc.astype(BF16)

        @pl.when(pl.program_id(0) == 0)
        def _():
            dw_ref[...] = jnp.zeros_like(dw_ref)

        dw_ref[...] += dw

    col = lambda k: pl.BlockSpec((S, C), functools.partial(lambda b, kk: (b, kk), kk=k))
    return pl.pallas_call(
        body, name="conv_bwd", grid=(n_ex,),
        in_specs=[col(0), col(1), col(2), pl.BlockSpec((3, C), lambda b: (0, 0)),
                  pl.BlockSpec((S, C), lambda b: (b, 0))],
        out_specs=[pl.BlockSpec((S, C3), lambda b: (b, 0)), pl.BlockSpec((3, C), lambda b: (0, 0))],
        out_shape=[jax.ShapeDtypeStruct((T, total_cols), BF16), jax.ShapeDtypeStruct((3, C), F32)],
        compiler_params=_cp("arbitrary"),
    )(proj_a, proj_a, proj_a, conv_w, dy)


QW = N_Q_HEADS * HEAD_DIM
KW = N_KV_HEADS * HEAD_DIM
QP = N_Q_HEADS * LANES
KP = N_KV_HEADS * LANES


def _attn_consts(seq):
    rows = seq // GRID_W
    r_idx, c_idx = jnp.meshgrid(jnp.arange(rows), jnp.arange(GRID_W), indexing='ij')
    r_idx = r_idx.reshape(-1).astype(F32)
    c_idx = c_idx.reshape(-1).astype(F32)
    n_freq = HEAD_DIM // 4
    inv = ROPE_THETA ** (-jnp.arange(n_freq, dtype=F32) / n_freq)
    ang = jnp.concatenate([r_idx[:, None] * inv, c_idx[:, None] * inv], axis=-1)
    cos = jnp.repeat(jnp.cos(ang), 2, axis=1)
    sin = jnp.repeat(jnp.sin(ang), 2, axis=1)
    sgn = sin * jnp.tile(jnp.array([-1.0, 1.0], F32), HEAD_DIM // 2)
    cos = jnp.tile(cos, (1, N_Q_HEADS))
    sgn = jnp.tile(sgn, (1, N_Q_HEADS))
    lane = jnp.arange(QW)
    bd = jnp.where(lane[:, None] // HEAD_DIM == lane[None, :] // HEAD_DIM, 1.0 / HEAD_DIM, 0.0).astype(BF16)
    dst = (lane // HEAD_DIM) * LANES + lane % HEAD_DIM
    spread = (dst[:, None] == jnp.arange(QP)[None, :]).astype(BF16)
    return dict(cos=cos, sgn=sgn, bd=bd, spread=spread, gather=spread.T)


def qkv_prep_fwd(proj_b, qg, kg, cst, n_ex):
    T = proj_b.shape[0]
    S = T // n_ex
    tm = _tile(S, 512)
    nb = S // tm

    def body(p_ref, qg_ref, kg_ref, cos_ref, sgn_ref, bd_ref, sp_ref, q_ref, k_ref, v_ref):
        pv = p_ref[...]
        cos, sgn, bd, sp = cos_ref[...], sgn_ref[...], bd_ref[...], sp_ref[...]
        qr = _rope_norm(pv[:, :QW], qg_ref[...], cos, sgn, bd, HEAD_DIM ** -0.5)
        kr = _rope_norm(pv[:, QW:QW + KW], kg_ref[...], cos[:, :KW], sgn[:, :KW], bd[:KW, :KW], 1.0)
        q_ref[...] = _nn(qr.astype(BF16), sp).astype(BF16)
        k_ref[...] = _nn(kr.astype(BF16), sp[:KW, :KP]).astype(BF16)
        v_ref[...] = _nn(pv[:, QW + KW:].astype(BF16), sp[:KW, :KP]).astype(BF16)

    full = lambda a: pl.BlockSpec(a.shape, lambda i: (0,) * a.ndim)
    tab = pl.BlockSpec((tm, QW), lambda i: (i % nb, 0))
    return pl.pallas_call(
        body, name="qkv_prep_fwd", grid=(T // tm,),
        in_specs=[pl.BlockSpec((tm, QW + 2 * KW), lambda i: (i, 0)), full(qg), full(kg), tab, tab,
                  full(cst["bd"]), full(cst["spread"])],
        out_specs=[pl.BlockSpec((tm, QP), lambda i: (i, 0)), pl.BlockSpec((tm, KP), lambda i: (i, 0)),
                   pl.BlockSpec((tm, KP), lambda i: (i, 0))],
        out_shape=[jax.ShapeDtypeStruct((T, QP), BF16), jax.ShapeDtypeStruct((T, KP), BF16),
                   jax.ShapeDtypeStruct((T, KP), BF16)],
        compiler_params=_cp("parallel"),
    )(proj_b, qg, kg, cst["cos"], cst["sgn"], cst["bd"], cst["spread"])


def qkv_prep_bwd(proj_b, qg, kg, cst, n_ex, dq, dk_pad, dv_pad, d_proj):
    T = proj_b.shape[0]
    S = T // n_ex
    tm = _tile(S, 512)
    nb = S // tm

    def body(p_ref, qg_ref, kg_ref, cos_ref, sgn_ref, bd_ref, ga_ref, dq_ref, dk_ref, dv_ref, _kept,
             dp_ref, dqg_ref, dkg_ref):
        pv = p_ref[...]
        cos, sgn, bd, ga = cos_ref[...], sgn_ref[...], bd_ref[...], ga_ref[...]
        fq = lambda q, g: _rope_norm(q, g, cos, sgn, bd, HEAD_DIM ** -0.5)
        fk = lambda k, g: _rope_norm(k, g, cos[:, :KW], sgn[:, :KW], bd[:KW, :KW], 1.0)
        _, vq = jax.vjp(fq, pv[:, :QW], qg_ref[...])
        _, vk = jax.vjp(fk, pv[:, QW:QW + KW], kg_ref[...])
        dqp, dqg = vq(dq_ref[...])
        dkp, dkg = vk(_split_mm(dk_ref[...], ga[:KP, :KW]))
        dp_ref[:, :QW] = dqp.astype(BF16)
        dp_ref[:, QW:QW + KW] = dkp.astype(BF16)
        dp_ref[:, QW + KW:] = _split_mm(dv_ref[...], ga[:KP, :KW]).astype(BF16)

        @pl.when(pl.program_id(0) == 0)
        def _():
            dqg_ref[...] = jnp.zeros_like(dqg_ref)
            dkg_ref[...] = jnp.zeros_like(dkg_ref)

        dqg_ref[...] += dqg
        dkg_ref[...] += dkg

    full = lambda a: pl.BlockSpec(a.shape, lambda i: (0,) * a.ndim)
    tab = pl.BlockSpec((tm, QW), lambda i: (i % nb, 0))
    row = lambda n: pl.BlockSpec((tm, n), lambda i: (i, 0))
    wb = QW + 2 * KW
    assert d_proj.shape[1] % wb == 0
    last = d_proj.shape[1] // wb - 1
    return pl.pallas_call(
        body, name="qkv_prep_bwd", grid=(T // tm,),
        in_specs=[row(wb), full(qg), full(kg), tab, tab, full(cst["bd"]), full(cst["gather"]),
                  row(QW), row(KP), row(KP), ANY],
        out_specs=[pl.BlockSpec((tm, wb), lambda i: (i, last)), pl.BlockSpec((1, QW), lambda i: (0, 0)),
                   pl.BlockSpec((1, KW), lambda i: (0, 0))],
        out_shape=[jax.ShapeDtypeStruct(d_proj.shape, BF16), jax.ShapeDtypeStruct((1, QW), F32),
                   jax.ShapeDtypeStruct((1, KW), F32)],
        input_output_aliases={10: 0}, compiler_params=_cp("arbitrary"),
    )(proj_b, qg, kg, cst["cos"], cst["sgn"], cst["bd"], cst["gather"], dq, dk_pad, dv_pad, d_proj)


ATT_TQ = 256


def attn_fwd(qp, kp, vp, gather, n_ex, mix, job=None):
    T = qp.shape[0]
    S = T // n_ex
    tq = _tile(S, ATT_TQ)
    nq = S // tq

    def body(q_ref, k_ref, v_ref, ga_ref, _kept, o_ref, op_ref, lse_ref):
        lane = lax.broadcasted_iota(jnp.int32, (tq, LANES), 1)
        lse_all = jnp.zeros((tq, LANES), F32)
        for h in range(N_Q_HEADS):
            kv = h // Q_PER_KV
            qh = q_ref[:, h * LANES:(h + 1) * LANES]
            s = _nt(qh, k_ref[:, kv * LANES:(kv + 1) * LANES])
            m = jnp.max(s, axis=-1, keepdims=True)
            p = jnp.exp(s - m)
            lsum = jnp.sum(p, axis=-1, keepdims=True)
            o = _nn(p.astype(BF16), v_ref[:, kv * LANES:(kv + 1) * LANES]) / lsum
            op_ref[:, h * LANES:(h + 1) * LANES] = o.astype(BF16)
            lse_all = jnp.where(lane == h, m + jnp.log(lsum), lse_all)
        lse_ref[...] = lse_all
        o_ref[...] = _nn(op_ref[...], ga_ref[...]).astype(BF16)

    blk = lambda n: pl.BlockSpec((tq, n), lambda b, i: (b * nq + i, 0))
    kvs = pl.BlockSpec((S, KP), lambda b, i: (b, 0))
    return _call(
        body, "attn_fwd", (n_ex, nq),
        [blk(QP), kvs, kvs, pl.BlockSpec(gather.shape, lambda b, i: (0, 0)), ANY], [qp, kp, vp, gather, mix],
        [pl.BlockSpec((tq, QW), lambda b, i: (b * nq + i, 1)), blk(QP), blk(LANES)],
        [jax.ShapeDtypeStruct(mix.shape, BF16), jax.ShapeDtypeStruct((T, QP), BF16),
         jax.ShapeDtypeStruct((T, LANES), F32)], (), ("parallel", "parallel"), job, {4: 0})


def attn_bwd(qp, kp, vp, op, lse, do, cst, n_ex, job=None):
    T = qp.shape[0]
    S = T // n_ex
    tq = _tile(S, ATT_TQ)
    nq = S // tq

    def body(q_ref, k_ref, v_ref, op_ref, lse_ref, do_ref, sp_ref, ga_ref, dq_ref, dk_ref, dv_ref, dqp_s):
        @pl.when(pl.program_id(1) == 0)
        def _():
            dk_ref[...] = jnp.zeros_like(dk_ref)
            dv_ref[...] = jnp.zeros_like(dv_ref)

        lane = lax.broadcasted_iota(jnp.int32, (tq, LANES), 1)
        dop = _nn(do_ref[...], sp_ref[...]).astype(BF16)
        lse_all = lse_ref[...]
        for h in range(N_Q_HEADS):
            kv = h // Q_PER_KV
            hs = slice(h * LANES, (h + 1) * LANES)
            ks = slice(kv * LANES, (kv + 1) * LANES)
            qh, kk, vv = q_ref[:, hs], k_ref[:, ks], v_ref[:, ks]
            doh = dop[:, hs]
            lse_h = jnp.sum(jnp.where(lane == h, lse_all, 0.0), axis=-1, keepdims=True)
            p = jnp.exp(_nt(qh, kk) - lse_h)
            dp = _nt(doh, vv)
            delta = jnp.sum(doh.astype(F32) * op_ref[:, hs].astype(F32), axis=-1, keepdims=True)
            ds = (p * (dp - delta)).astype(BF16)
            dqp_s[:, hs] = _nn(ds, kk)
            dk_ref[:, ks] += _tn(ds, qh)
            dv_ref[:, ks] += _tn(p.astype(BF16), doh)
        dq_ref[...] = _split_mm(dqp_s[...], ga_ref[...])

    blk = lambda n: pl.BlockSpec((tq, n), lambda b, i: (b * nq + i, 0))
    kvs = pl.BlockSpec((S, KP), lambda b, i: (b, 0))
    full = lambda a: pl.BlockSpec(a.shape, lambda b, i: (0, 0))
    return _call(
        body, "attn_bwd", (n_ex, nq),
        [blk(QP), kvs, kvs, blk(QP), blk(LANES), pl.BlockSpec((tq, QW), lambda b, i: (b * nq + i, 1)),
         full(cst["spread"]), full(cst["gather"])],
        [qp, kp, vp, op, lse, do, cst["spread"], cst["gather"]],
        [blk(QW), kvs, kvs],
        [jax.ShapeDtypeStruct((T, QW), F32), jax.ShapeDtypeStruct((T, KP), F32), jax.ShapeDtypeStruct((T, KP), F32)],
        [pltpu.VMEM((tq, QP), F32)], ("arbitrary", "arbitrary"), job)


def pool_fwd(p, pool_w, scale, n_ex):
    T, W = p.shape
    S = T // n_ex

    def body(p_ref, w_ref, s_ref, o_ref):
        o_ref[...] = _pool_mix(p_ref[...], w_ref[...], s_ref[...]).astype(BF16)

    return pl.pallas_call(
        body, name="pool_fwd", grid=(n_ex,),
        in_specs=[pl.BlockSpec((S, W), lambda b: (b, 0)),
                  pl.BlockSpec(pool_w.shape, lambda b: (0, 0, 0)),
                  pl.BlockSpec((1, W), lambda b: (0, 0))],
        out_specs=pl.BlockSpec((S, W), lambda b: (b, 0)),
        out_shape=jax.ShapeDtypeStruct((T, 2 * W), BF16), compiler_params=_cp("parallel"),
    )(p, pool_w, scale)


def pool_bwd(p, pool_w, scale, n_ex, dy, d_proj):
    T, W = p.shape
    S = T // n_ex
    last = d_proj.shape[1] // W - 1

    def body(p_ref, w_ref, s_ref, dy_ref, _kept, dp_ref, dw_ref, ds_ref):
        _, vjp = jax.vjp(_pool_mix, p_ref[...], w_ref[...], s_ref[...])
        dp, dw, ds = vjp(dy_ref[...].astype(F32))
        dp_ref[...] = dp.astype(BF16)

        @pl.when(pl.program_id(0) == 0)
        def _():
            dw_ref[...] = jnp.zeros_like(dw_ref)
            ds_ref[...] = jnp.zeros_like(ds_ref)

        dw_ref[...] += dw
        ds_ref[...] += ds

    wshape = pool_w.shape
    return pl.pallas_call(
        body, name="pool_bwd", grid=(n_ex,),
        in_specs=[pl.BlockSpec((S, W), lambda b: (b, 0)),
                  pl.BlockSpec(wshape, lambda b: (0, 0, 0)),
                  pl.BlockSpec((1, W), lambda b: (0, 0)), pl.BlockSpec((S, W), lambda b: (b, 0)), ANY],
        out_specs=[pl.BlockSpec((S, W), lambda b: (b, last)), pl.BlockSpec(wshape, lambda b: (0, 0, 0)),
                   pl.BlockSpec((1, W), lambda b: (0, 0))],
        out_shape=[jax.ShapeDtypeStruct(d_proj.shape, BF16), jax.ShapeDtypeStruct(wshape, F32),
                   jax.ShapeDtypeStruct((1, W), F32)],
        input_output_aliases={4: 0}, compiler_params=_cp("arbitrary"),
    )(p, pool_w, scale, dy, d_proj)


SGU_TS = 512


def sgu_fwd(u, v, norm_g, w_s, b_full, mix):
    T, W = u.shape
    ts = _tile(T, SGU_TS)

    def body(u_ref, v_ref, g_ref, w_ref, b_ref, _kept, o_ref):
        o_ref[...] = _sgu(u_ref[...], v_ref[...], g_ref[...], w_ref[...], b_ref[...]).astype(BF16)

    row = pl.BlockSpec((ts, W), lambda i: (i, 0))
    wsp = pl.BlockSpec(w_s.shape, lambda i: (0, 0, 0))
    return pl.pallas_call(
        body, name="sgu_fwd", grid=(T // ts,),
        in_specs=[row, row, pl.BlockSpec((1, W), lambda i: (0, 0)), wsp, wsp, ANY],
        out_specs=pl.BlockSpec((ts, W), lambda i: (i, 1)), out_shape=jax.ShapeDtypeStruct(mix.shape, BF16),
        input_output_aliases={5: 0}, compiler_params=_cp("parallel"),
    )(u, v, norm_g, w_s, b_full, mix)


def sgu_bwd(u, v, norm_g, w_s, b_full, dy):
    T, W = u.shape
    ts = _tile(T, SGU_TS)
    wshape = w_s.shape

    def body(u_ref, v_ref, g_ref, w_ref, b_ref, dy_ref, duv_ref, dg_ref, dw_ref, db_ref):
        _, vjp = jax.vjp(_sgu, u_ref[...], v_ref[...], g_ref[...], w_ref[...], b_ref[...])
        du, dv, dg, dw, db = vjp(dy_ref[...].astype(F32))
        duv_ref[:, :W] = du.astype(BF16)
        duv_ref[:, W:] = dv.astype(BF16)

        @pl.when(pl.program_id(0) == 0)
        def _():
            dg_ref[...] = jnp.zeros_like(dg_ref)
            dw_ref[...] = jnp.zeros_like(dw_ref)
            db_ref[...] = jnp.zeros_like(db_ref)

        dg_ref[...] += dg
        dw_ref[...] += dw
        db_ref[...] += db

    row = pl.BlockSpec((ts, W), lambda i: (i, 0))
    wsp = pl.BlockSpec(wshape, lambda i: (0, 0, 0))
    wout = pl.BlockSpec(wshape, lambda i: (0, 0, 0))
    vec = pl.BlockSpec((1, W), lambda i: (0, 0))
    return pl.pallas_call(
        body, name="sgu_bwd", grid=(T // ts,),
        in_specs=[row, row, vec, wsp, wsp, pl.BlockSpec((ts, W), lambda i: (i, 1))],
        out_specs=[pl.BlockSpec((ts, 2 * W), lambda i: (i, 0)), vec, wout, wout],
        out_shape=[jax.ShapeDtypeStruct((T, 3 * W), BF16),
                   jax.ShapeDtypeStruct((1, W), F32), jax.ShapeDtypeStruct(wshape, F32),
                   jax.ShapeDtypeStruct(wshape, F32)],
        compiler_params=_cp("arbitrary"),
    )(u, v, norm_g, w_s, b_full, dy)


def loss_head(x, gain, target):
    T, D = x.shape
    tm = _tile(T, 512)

    def body(x_ref, g_ref, t_ref, loss_ref, dx_ref, dg_ref):
        xv, g = x_ref[...], g_ref[...]
        r = lax.rsqrt(jnp.mean(xv * xv, axis=-1, keepdims=True) + EPS)
        err = xv * r * g - t_ref[...]
        part = 0.5 * jnp.sum(jnp.mean(err * err, axis=-1, keepdims=True), axis=0, keepdims=True)
        dx, dg = _rms_bwd_math(xv, g, err * (1.0 / D), jnp.zeros_like(xv))
        dx_ref[...] = dx

        @pl.when(pl.program_id(0) == 0)
        def _():
            loss_ref[...] = jnp.zeros_like(loss_ref)
            dg_ref[...] = jnp.zeros_like(dg_ref)

        loss_ref[...] += part
        dg_ref[...] += dg

    row = pl.BlockSpec((tm, D), lambda i: (i, 0))
    vec = pl.BlockSpec((1, D), lambda i: (0, 0))
    return pl.pallas_call(
        body, name="loss_head", grid=(T // tm,),
        in_specs=[row, vec, row], out_specs=[pl.BlockSpec((1, 1), lambda i: (0, 0)), row, vec],
        out_shape=[jax.ShapeDtypeStruct((1, 1), F32), jax.ShapeDtypeStruct((T, D), F32),
                   jax.ShapeDtypeStruct((1, D), F32)],
        compiler_params=_cp("arbitrary"),
    )(x, gain, target)


class MultiJob:
    def __init__(self, jobs):
        self.jobs = jobs
        self.args = [a for j in jobs for a in j.args]
        self.out_shape = [s for j in jobs for s in j.out_shape]
        self.scratch = [s for j in jobs for s in j.scratch]
        self.n_in, self.n_out = len(self.args), len(self.out_shape)

    def _each(self, ins, outs, sems):
        i = o = s = 0
        for j in self.jobs:
            yield j, ins[i:i + j.n_in], outs[o:o + j.n_out], sems[s:s + len(j.scratch)]
            i, o, s = i + j.n_in, o + j.n_out, s + len(j.scratch)

    def start(self, ins, outs, sems):
        for j, a, b, c in self._each(ins, outs, sems):
            j.start(a, b, c)

    def middle(self, ins, outs, sems):
        for j, a, b, c in self._each(ins, outs, sems):
            j.middle(a, b, c)

    def finish(self, ins, outs, sems):
        for j, a, b, c in self._each(ins, outs, sems):
            j.finish(a, b, c)

    def split(self, results):
        o = 0
        for j in self.jobs:
            yield results[o:o + j.n_out]
            o += j.n_out


class Plan:
    def __init__(self, shard, gathers, scatters, small_carrier=None, pack_small=None):
        self.shard, self.gathers, self.scatters = shard, gathers, scatters
        self.small_carrier, self.pack_small = small_carrier, pack_small
        self.weights, self.grads, self.parts = {}, {}, {}
        self.small_src = self.small_parts = None

    def weight(self, kind, l):
        return self.weights[(kind, l)]

    def grad(self, kind, l, g):
        self.grads[(kind, l)] = g

    def small_ready(self, small, d_final):
        if self.pack_small is not None:
            self.small_src = self.pack_small(small, d_final)

    def _jobs(self, key):
        jobs = []
        if key in self.gathers:
            ks = self.gathers[key]
            jobs.append((GatherJob([self.shard(*k) for k in ks]), self.weights, ks))
        if key in self.scatters:
            ks = self.scatters[key]
            jobs.append((ScatterJob([self.grads[k] for k in ks]), self.parts, ks))
        if key == self.small_carrier and self.small_src is not None:
            jobs.append((GatherJob([(self.small_src, None)]), None, None))
        return jobs

    def _deliver(self, jobs, results):
        multi = MultiJob([j for j, _, _ in jobs])
        for (_, store, ks), res in zip(jobs, multi.split(results)):
            if store is None:
                self.small_parts = res[0]
            else:
                store.update(zip(ks, res))

    def run(self, key, fn, *args, **kw):
        jobs = self._jobs(key)
        if not jobs:
            out = fn(*args, **kw)
            return out if fn is mm_tn else out[0]
        res, jres = fn(*args, job=MultiJob([j for j, _, _ in jobs]), **kw)
        self._deliver(jobs, jres)
        return res

    def alone(self, key, name):
        jobs = self._jobs(key)
        if jobs:
            self._deliver(jobs, run_job(MultiJob([j for j, _, _ in jobs]), name))


def _local_step(x, target, layers, final_norm, n_ex, plan):
    T, D = x.shape
    L = len(layers)
    cst = _attn_consts(T // n_ex)
    ident = lambda j: j
    EV_A, EV_B = 3 * (D // 2), QW + 2 * KW
    OD_W = D // 2
    wt = plan.weight

    saved = []
    for l, W in enumerate(layers):
        s = dict(x0=x)
        x1, *s["gu1"] = plan.run(("ffn1_fwd", l), ffn_fwd, x, W["n1"], wt("f1_in_t", l), wt("f1_out", l))
        if l % 2 == 0:
            pa, pb, h = mm_nt(x1, wt("mx_in_t", l), [(0, EV_A), (EV_A, EV_B)], F32, True, W["nm"])
            qg = jnp.tile(W["q_norm"], N_Q_HEADS)[None]
            kg = jnp.tile(W["k_norm"], N_KV_HEADS)[None]
            mix = conv_fwd(pa, W["conv_w"], n_ex)
            qp, kp, vp = qkv_prep_fwd(pb, qg, kg, cst, n_ex)
            mix, op, lse = plan.run(("attn_fwd", l), attn_fwd, qp, kp, vp, cst["gather"], n_ex, mix)
            s.update(pa=pa, pb=pb, qg=qg, kg=kg, qp=qp, kp=kp, vp=vp, op=op, lse=lse)
        else:
            p, u, v, h = mm_nt(x1, wt("mx_in_t", l), [(0, OD_W), (OD_W, OD_W), (2 * OD_W, OD_W)], F32, True, W["nm"])
            scale = W["pool_scale"][None]
            sn = W["sgu_norm"][None]
            b_full = jnp.broadcast_to(W["sgu_b"][..., None], W["sgu_w"].shape)
            mix = sgu_fwd(u, v, sn, W["sgu_w"], b_full, pool_fwd(p, W["pool_w"], scale, n_ex))
            s.update(p=p, u=u, v=v, scale=scale, sn=sn, b_full=b_full)
        x2 = mm_nn([(mix, 0, D, 0)], wt("mx_out", l), residual=x1)
        x3, *s["gu2"] = plan.run(("ffn2_fwd", l), ffn_fwd, x2, W["n2"], wt("f2_in_t", l), wt("f2_out", l))
        s.update(x1=x1, x2=x2, h=h, mix=mix)
        saved.append(s)
        x = x3

    loss, dx, d_final = loss_head(x, final_norm, target)

    small = [None] * L

    def ffn_back(which, l, dout, xin, gain, gu_xn, sm, sm_key):
        w_in, w_out = wt(which + "_in_t", l), wt(which + "_out", l)
        F = w_out.shape[0]
        nc = F // FFN_TN
        gu, xn = gu_xn
        dxi, sm[sm_key], a, dgu, dob = plan.run((which + "_bwd", l), ffn_bwd_x, dout, xin, gain, gu, w_in, w_out)
        if which == "f1" and l == 0:
            plan.small_ready(small, d_final)
        plan.grad(which + "_in_t", l, plan.run(
            (which + "_in_grad", l), mm_tn, dgu, xn, 2 * F, lambda k, c: k * nc + c, grid=(2, nc),
            col_block=lambda k, c: 2 * c + k))
        plan.grad(which + "_out", l, plan.run((which + "_out_grad", l), mm_tn, a, dob, F, ident))
        return dxi

    for l in reversed(range(L)):
        s, W = saved[l], layers[l]
        sm = small[l] = {}
        dx = ffn_back("f2", l, dx, s["x2"], W["n2"], s["gu2"], sm, "n2")
        dmix, dxb = mm_nt(dx, wt("mx_out", l), [(0, D)], BF16, emit_a_bf16=True)
        plan.grad("mx_out", l, mm_tn(s["mix"], dxb, D, ident))
        if l % 2 == 0:
            d_proj, sm["conv_w"] = conv_bwd(s["pa"], W["conv_w"], n_ex, dmix, EV_A + EV_B)
            dq, dkp, dvp = plan.run(("attn_bwd", l), attn_bwd, s["qp"], s["kp"], s["vp"], s["op"], s["lse"], dmix, cst, n_ex)
            d_proj, dqg, dkg = qkv_prep_bwd(s["pb"], s["qg"], s["kg"], cst, n_ex, dq, dkp, dvp, d_proj)
            d_pieces = [(d_proj, 0, EV_A + EV_B, 0)]
            plan.grad("mx_in_t", l, mm_tn(d_proj, s["h"], EV_A + EV_B, ident))
            sm["q_norm"] = dqg.reshape(N_Q_HEADS, HEAD_DIM).sum(0)
            sm["k_norm"] = dkg.reshape(N_KV_HEADS, HEAD_DIM).sum(0)
        else:
            d_proj, d_sn, sm["sgu_w"], d_sb = sgu_bwd(s["u"], s["v"], s["sn"], W["sgu_w"], s["b_full"], dmix)
            d_proj, sm["pool_w"], d_ps = pool_bwd(s["p"], W["pool_w"], s["scale"], n_ex, dmix, d_proj)
            d_pieces = [(d_proj, 0, OD_W, 1), (d_proj, 1, OD_W, 2), (d_proj, 2, OD_W, 0)]
            nb = OD_W // MM_TC
            plan.grad("mx_in_t", l, mm_tn(d_proj, s["h"], 3 * OD_W,
                                          lambda jj: jnp.where(jj < 2 * nb, jj + nb, jj - 2 * nb)))
            sm["pool_scale"], sm["sgu_norm"], sm["sgu_b"] = d_ps[0], d_sn[0], d_sb.sum(-1)
        dx, sm["nm"] = mm_nn(d_pieces, wt("mx_in_t", l), norm_bwd=(s["x1"], W["nm"], dx))
        dx = ffn_back("f1", l, dx, s["x0"], W["n1"], s["gu1"], sm, "n1")
    return loss, dx


def all_gather(srcs):
    return run_job(GatherJob(srcs), "all_gather")


def cast_shards(w):
    L, A, B = w.shape

    def body(w_ref, o_ref):
        o_ref[...] = w_ref[...].astype(BF16)

    return pl.pallas_call(
        body, name="cast_shards", grid=(L,),
        in_specs=[pl.BlockSpec((None, A, B), lambda l: (l, 0, 0))],
        out_specs=pl.BlockSpec((None, A, B), lambda l: (l, 0, 0)),
        out_shape=jax.ShapeDtypeStruct((L, A, B), BF16), compiler_params=_cp("parallel"),
    )(w)


ADAM_TC = 256


def adamw(parts, w, m, v, l, prev=None):
    P, R, C = parts.shape
    tc = _tile(C, ADAM_TC)
    c1, c2 = 1.0 - ADAM_B1 ** ADAM_STEP, 1.0 - ADAM_B2 ** ADAM_STEP

    def body(p_ref, w_ref, m_ref, v_ref, g_ref, d_ref, mo_ref, vo_ref):
        g = p_ref[0].astype(F32)
        for s in range(1, P):
            g = g + p_ref[s].astype(F32)
        m1 = ADAM_B1 * m_ref[...] + (1.0 - ADAM_B1) * g
        v1 = ADAM_B2 * v_ref[...] + (1.0 - ADAM_B2) * (g * g)
        g_ref[...] = g
        mo_ref[...] = m1
        vo_ref[...] = v1
        d_ref[...] = -ADAM_LR * ((m1 / c1) / (jnp.sqrt(v1 / c2) + ADAM_EPS) + ADAM_WD * w_ref[...])

    wspec = pl.BlockSpec((None, R, tc), lambda i: (l, 0, i))
    prev = list(prev) if prev is not None else []
    return pl.pallas_call(
        lambda *refs: body(*refs[:4], *refs[4 + len(prev):]), name="adamw", grid=(C // tc,),
        in_specs=[pl.BlockSpec((P, R, tc), lambda i: (0, 0, i)), wspec, wspec, wspec] + [ANY] * len(prev),
        out_specs=[wspec] * 4, out_shape=[jax.ShapeDtypeStruct(w.shape, F32)] * 4,
        input_output_aliases={4 + i: i for i in range(len(prev))},
        compiler_params=_cp("parallel"),
    )(parts, w, m, v, *prev)


_WEIGHTS = ['ffn1_norm', 'ffn1_w_in', 'ffn1_w_out', 'mix_norm', 'ffn2_norm', 'ffn2_w_in', 'ffn2_w_out', 'ev_w_in',
            'ev_conv_w', 'ev_q_norm', 'ev_k_norm', 'ev_w_out', 'od_w_in', 'od_pool_w', 'od_pool_scale', 'od_sgu_norm',
            'od_sgu_w', 'od_sgu_b', 'od_w_out', 'final_norm']
_BIG = dict(ffn1_w_in=True, ffn1_w_out=False, ffn2_w_in=True, ffn2_w_out=False,
            ev_w_in=True, ev_w_out=False, od_w_in=True, od_w_out=False)
_SMALL_SHARDED = ['ev_conv_w', 'od_pool_scale', 'od_sgu_norm']
_SMALL = [n for n in _WEIGHTS if n not in _BIG]
_PACK_ROWS = 8 * LANES


_KINDS = ("f1_in_t", "f1_out", "mx_in_t", "mx_out", "f2_in_t", "f2_out")
_CARRIER_US = dict(ffn1_fwd=105, ffn2_fwd=105, attn_fwd=115, f2_bwd=135, f1_bwd=135, attn_bwd=205,
                   f2_out_grad=33, f1_out_grad=33, f2_in_grad=61, f1_in_grad=61)
_GATHER_US_PER_ROW, _SCATTER_US_PER_ROW, _SMALL_GATHER_US, _SLACK_US = 0.08, 0.176, 42, 10


def _schedule(L, rows):
    events = []
    for l in range(L):
        events += [("ffn1_fwd", l), ("mixer", l)] + ([("attn_fwd", l)] if l % 2 == 0 else []) + [("ffn2_fwd", l)]
    consumer = {"f1": "ffn1_fwd", "mx": "mixer", "f2": "ffn2_fwd"}
    queue = [(k, l) for l in range(L) for k in _KINDS]
    pos = {t: events.index((consumer[t[0][:2]], t[1])) for t in queue}
    gathers = {"first": [t for t in queue if pos[t] == 0]}
    queue = [t for t in queue if pos[t] > 0]
    carriers = [i for i, e in enumerate(events) if e[0] in _CARRIER_US]
    for i in carriers:
        budget, take = _CARRIER_US[events[i][0]], []
        later = [j for j in carriers if j > i]
        while queue:
            t = queue[0]
            cost = rows(*t) * _GATHER_US_PER_ROW
            forced = not any(j < pos[t] for j in later)
            if not forced and cost > budget:
                break
            take.append(queue.pop(0))
            budget -= cost
        if take:
            gathers[events[i]] = take
    assert not queue
    events = []
    for l in reversed(range(L)):
        events += [("f2_bwd", l), ("f2_in_grad", l), ("f2_out_grad", l), ("mx_out_ready", l)]
        events += [("attn_bwd", l)] if l % 2 == 0 else []
        events += [("mx_in_ready", l), ("f1_bwd", l), ("f1_in_grad", l), ("f1_out_grad", l)]
    made_by = {"f2_out": "f2_out_grad", "f2_in_t": "f2_in_grad", "mx_out": "mx_out_ready", "mx_in_t": "mx_in_ready",
               "f1_out": "f1_out_grad", "f1_in_t": "f1_in_grad"}
    small_carrier = ("f1_in_grad", 0)
    scatters, ready = {}, []
    for e in events:
        if e[0] in _CARRIER_US:
            budget, take = _CARRIER_US[e[0]] - (_SMALL_GATHER_US if e == small_carrier else 0), []
            if e == events[-1]:
                budget = float("inf")
            while True:
                fits = [t for t in ready if rows(*t) * _SCATTER_US_PER_ROW <= budget + _SLACK_US]
                if not fits:
                    break
                t = max(fits, key=lambda u: rows(*u))
                budget -= rows(*t) * _SCATTER_US_PER_ROW
                ready.remove(t)
                take.append(t)
            if take:
                scatters[e] = take
        ready += [(k, e[1]) for k in _KINDS if made_by[k] == e[0]]
    scatters["last"] = ready
    return gathers, scatters, small_carrier


def _pack(arrs):
    flat = jnp.concatenate([a.reshape(-1) for a in arrs])
    pad = (-flat.shape[0]) % _PACK_ROWS
    return jnp.pad(flat, (0, pad)).reshape(-1, LANES)


def _unpack(buf, shapes):
    flat, out, off = buf.reshape(-1), [], 0
    for s in shapes:
        n = math.prod(s)
        out.append(flat[off:off + n].reshape(s))
        off += n
    return out


def _unshard_last(g, lead):
    nd = len(lead)
    return jnp.moveaxis(g, 0, nd).reshape(*lead, -1)


def kernel(x, ffn1_norm, ffn1_w_in, ffn1_w_out, mix_norm, ffn2_norm, ffn2_w_in, ffn2_w_out, ev_w_in, ev_conv_w, ev_q_norm, ev_k_norm, ev_w_out, od_w_in, od_pool_w, od_pool_scale, od_sgu_norm, od_sgu_w, od_sgu_b, od_w_out, final_norm, loss_target, m_ffn1_norm, m_ffn1_w_in, m_ffn1_w_out, m_mix_norm, m_ffn2_norm, m_ffn2_w_in, m_ffn2_w_out, m_ev_w_in, m_ev_conv_w, m_ev_q_norm, m_ev_k_norm, m_ev_w_out, m_od_w_in, m_od_pool_w, m_od_pool_scale, m_od_sgu_norm, m_od_sgu_w, m_od_sgu_b, m_od_w_out, m_final_norm, v_ffn1_norm, v_ffn1_w_in, v_ffn1_w_out, v_mix_norm, v_ffn2_norm, v_ffn2_w_in, v_ffn2_w_out, v_ev_w_in, v_ev_conv_w, v_ev_q_norm, v_ev_k_norm, v_ev_w_out, v_od_w_in, v_od_pool_w, v_od_pool_scale, v_od_sgu_norm, v_od_sgu_w, v_od_sgu_b, v_od_w_out, v_final_norm):
    w = dict(zip(_WEIGHTS, (ffn1_norm, ffn1_w_in, ffn1_w_out, mix_norm, ffn2_norm, ffn2_w_in, ffn2_w_out, ev_w_in, ev_conv_w, ev_q_norm, ev_k_norm, ev_w_out, od_w_in, od_pool_w, od_pool_scale, od_sgu_norm, od_sgu_w, od_sgu_b, od_w_out, final_norm)))
    m = dict(zip(_WEIGHTS, (m_ffn1_norm, m_ffn1_w_in, m_ffn1_w_out, m_mix_norm, m_ffn2_norm, m_ffn2_w_in, m_ffn2_w_out, m_ev_w_in, m_ev_conv_w, m_ev_q_norm, m_ev_k_norm, m_ev_w_out, m_od_w_in, m_od_pool_w, m_od_pool_scale, m_od_sgu_norm, m_od_sgu_w, m_od_sgu_b, m_od_w_out, m_final_norm)))
    v = dict(zip(_WEIGHTS, (v_ffn1_norm, v_ffn1_w_in, v_ffn1_w_out, v_mix_norm, v_ffn2_norm, v_ffn2_w_in, v_ffn2_w_out, v_ev_w_in, v_ev_conv_w, v_ev_q_norm, v_ev_k_norm, v_ev_w_out, v_od_w_in, v_od_pool_w, v_od_pool_scale, v_od_sgu_norm, v_od_sgu_w, v_od_sgu_b, v_od_w_out, v_final_norm)))
    n_ex, seq, D = x.shape
    T = n_ex * seq
    L = ffn1_norm.shape[0]
    me = 4 * lax.axis_index("x") + 2 * lax.axis_index("y") + lax.axis_index("c")

    sh_small = [w[n] for n in _SMALL_SHARDED]
    packed = all_gather([(_pack(sh_small), None)])[0].reshape(N_DEV, -1)
    full_small = {}
    off = 0
    for n, a in zip(_SMALL_SHARDED, sh_small):
        cnt = math.prod(a.shape)
        full_small[n] = _unshard_last(packed[:, off:off + cnt].reshape((N_DEV,) + a.shape), a.shape[:-1])
        off += cnt

    tr = lambda a: jnp.swapaxes(a, 1, 2)
    wmv = {n: tuple(tr(d[n]) if t else d[n] for d in (w, m, v)) for n, t in _BIG.items()}
    shards = {n: cast_shards(wmv[n][0]) for n in _BIG}

    def name_of(kind, l):
        mx = "ev" if l % 2 == 0 else "od"
        return {"f1_in_t": "ffn1_w_in", "f1_out": "ffn1_w_out", "f2_in_t": "ffn2_w_in", "f2_out": "ffn2_w_out",
                "mx_in_t": mx + "_w_in", "mx_out": mx + "_w_out"}[kind], (l // 2 if kind.startswith("mx") else l)

    def shard(kind, l):
        name, idx = name_of(kind, l)
        return shards[name], idx

    g_shapes = {}

    def pack_small(small, d_final):
        ev = [sm for l, sm in enumerate(small) if l % 2 == 0]
        od = [sm for l, sm in enumerate(small) if l % 2 == 1]
        st = lambda sms, k: jnp.stack([sm[k] for sm in sms])
        g_full = dict(ffn1_norm=st(small, "n1")[:, 0], mix_norm=st(small, "nm")[:, 0], ffn2_norm=st(small, "n2")[:, 0],
                      ev_conv_w=st(ev, "conv_w"), ev_q_norm=st(ev, "q_norm"), ev_k_norm=st(ev, "k_norm"),
                      od_pool_w=st(od, "pool_w"), od_pool_scale=st(od, "pool_scale"), od_sgu_norm=st(od, "sgu_norm"),
                      od_sgu_w=st(od, "sgu_w"), od_sgu_b=st(od, "sgu_b"), final_norm=d_final[0])
        g_shapes.update({n: g_full[n].shape for n in _SMALL})
        return _pack([g_full[n] for n in _SMALL])

    gathers, scatters, small_carrier = _schedule(L, lambda kind, l: shards[name_of(kind, l)[0]].shape[1])
    plan = Plan(shard, gathers, scatters, small_carrier, pack_small)
    layers = []
    for l in range(L):
        j = l // 2
        W = dict(n1=ffn1_norm[l][None], nm=mix_norm[l][None], n2=ffn2_norm[l][None])
        if l % 2 == 0:
            W.update(conv_w=full_small["ev_conv_w"][j], q_norm=ev_q_norm[j], k_norm=ev_k_norm[j])
        else:
            W.update(pool_w=od_pool_w[j], pool_scale=full_small["od_pool_scale"][j], sgu_norm=full_small["od_sgu_norm"][j],
                     sgu_w=od_sgu_w[j], sgu_b=od_sgu_b[j])
        layers.append(W)

    plan.alone("first", "gather_first")
    loss, dx = _local_step(x.reshape(T, D), loss_target.reshape(T, D), layers, final_norm[None], n_ex, plan)
    plan.alone("last", "scatter_last")

    out = {n: None for n in _BIG}
    for (kind, l), parts in plan.parts.items():
        name, idx = name_of(kind, l)
        out[name] = adamw(parts, *wmv[name], idx, prev=out[name])
    out = {n: [tr(a) if _BIG[n] else a for a in res] for n, res in out.items()}

    g8 = plan.small_parts.reshape(N_DEV, -1)
    cols, off = [], 0
    for n in _SMALL:
        cnt = math.prod(g_shapes[n])
        g = g8[:, off:off + cnt].reshape((N_DEV,) + g_shapes[n])
        off += cnt
        if n in _SMALL_SHARDED:
            width = w[n].shape[-1]
            g = lax.dynamic_slice_in_dim(g, me * width, width, axis=g.ndim - 1)
        cols.append(g.reshape(N_DEV, -1))
    g8 = jnp.concatenate(cols, axis=1)
    g8 = jnp.pad(g8, ((0, 0), (0, (-g8.shape[1]) % _PACK_ROWS))).reshape(N_DEV, -1, LANES)
    pk = lambda d: _pack([d[n] for n in _SMALL])[None]
    small_out = adamw(g8, pk(w), pk(m), pk(v), 0)
    shapes = [w[n].shape for n in _SMALL]
    for i in range(4):
        for n, a in zip(_SMALL, _unpack(small_out[i], shapes)):
            out.setdefault(n, [None] * 4)[i] = a

    total = lax.psum(loss[0, 0], ("x", "y", "c"))
    return (total, dx.reshape(n_ex, seq, D), *[out[n][0] for n in _WEIGHTS], *[out[n][1] for n in _WEIGHTS],
            *[out[n][2] for n in _WEIGHTS], *[out[n][3] for n in _WEIGHTS])
```

```python
import functools
import math

import jax
import jax.numpy as jnp
from jax import lax
from jax.experimental import pallas as pl
from jax.experimental.pallas import tpu as pltpu

F32, BF16 = jnp.float32, jnp.bfloat16
EPS = 1e-6
N_DEV = 8
V7X_VMEM_BYTES = 64 * 1024 * 1024
VMEM_LIMIT = V7X_VMEM_BYTES - 8 * 1024 * 1024
LANES = 128
HEAD_DIM = 64
N_Q_HEADS = 8
N_KV_HEADS = 2
Q_PER_KV = N_Q_HEADS // N_KV_HEADS
GRID_W = 64
ROPE_THETA = 10000.0
POOL_RADII = (1, 2, 4, 8)
SGU_CHUNK = 128
GROUP = 128
ADAM_LR, ADAM_B1, ADAM_B2, ADAM_EPS, ADAM_WD, ADAM_STEP = 0.001, 0.9, 0.999, 1e-08, 0.01, 10
MESH_ID = pl.DeviceIdType.MESH


def _cp(*sem):
    return pltpu.CompilerParams(dimension_semantics=sem, vmem_limit_bytes=VMEM_LIMIT)


def _dot(a, b, ca, cb):
    return lax.dot_general(a, b, (((ca,), (cb,)), ((), ())), preferred_element_type=F32)


def _nn(a, b):
    return _dot(a, b, 1, 0)


def _nt(a, b):
    return _dot(a, b, 1, 1)


def _tn(a, b):
    return _dot(a, b, 0, 0)


def _split_mm(x, m):
    hi = x.astype(BF16)
    lo = (x - hi.astype(F32)).astype(BF16)
    return _nn(hi, m) + _nn(lo, m)


def _tile(n, pref):
    t = min(n, pref)
    assert n % t == 0, (n, pref)
    return t


ANY = pl.BlockSpec(memory_space=pl.ANY)
OTHER_CHIPS = (2, 4, 6)
JOB_MIDDLE = 0.55


def _my_place():
    x, y, c = lax.axis_index("x"), lax.axis_index("y"), lax.axis_index("c")
    return x, y, c, 4 * x + 2 * y + c


def _peer(x, y, c, k):
    px = 1 - x if k & 4 else x
    py = 1 - y if k & 2 else y
    pc = 1 - c if k & 1 else c
    return (px, py, pc), 4 * px + 2 * py + pc


def _remote(src, dst, send_sems, recv_sems, i, peer):
    return pltpu.make_async_remote_copy(src_ref=src, dst_ref=dst, send_sem=send_sems.at[i], recv_sem=recv_sems.at[i],
                                        device_id=peer, device_id_type=MESH_ID)


class GatherJob:
    def __init__(self, srcs):
        self.srcs = srcs
        self.args = [a for a, _ in srcs]
        self.dims = [a.shape[-2:] for a, _ in srcs]
        n = self.n_in = self.n_out = len(srcs)
        self.out_shape = [jax.ShapeDtypeStruct((N_DEV * r, cc), a.dtype) for (a, _), (r, cc) in zip(srcs, self.dims)]
        self.scratch = [pltpu.SemaphoreType.DMA((N_DEV * n,)), pltpu.SemaphoreType.DMA((N_DEV * n,)),
                        pltpu.SemaphoreType.DMA((n,))]

    def _rows(self, outs, t, idx):
        r = self.dims[t][0]
        return outs[t].at[pl.ds(pl.multiple_of(idx * r, 8), r), :]

    def _local(self, ins, outs, loc, t, me):
        src = ins[t] if self.srcs[t][1] is None else ins[t].at[self.srcs[t][1]]
        return src, pltpu.make_async_copy(src, self._rows(outs, t, me), loc.at[t])

    def start(self, ins, outs, sems):
        send, recv, loc = sems
        x, y, c, me = _my_place()
        for t in range(self.n_in):
            src, local = self._local(ins, outs, loc, t, me)
            local.start()
            for k in (2, 4, 1):
                _remote(src, self._rows(outs, t, me), send, recv, N_DEV * t + k, _peer(x, y, c, k)[0]).start()

    def _copy(self, outs, sems, t, origin, i, to):
        x, y, c, _ = _my_place()
        blk = self._rows(outs, t, _peer(x, y, c, origin)[1])
        return _remote(blk, blk, sems[0], sems[1], N_DEV * t + i, _peer(x, y, c, to)[0])

    def middle(self, ins, outs, sems):
        c = _my_place()[2]

        def relay(t, got, to):
            self._copy(outs, sems, t, got, got, got).wait_recv()
            self._copy(outs, sems, t, got, 6, to).start()
            self._copy(outs, sems, t, to, to, to).wait_recv()

        for t in range(self.n_in):
            pl.when(c == 1)(functools.partial(relay, t, 2, 4))
            pl.when(c == 0)(functools.partial(relay, t, 4, 2))
            for k in (2, 4):
                self._copy(outs, sems, t, k, k + 1, 1).start()

    def finish(self, ins, outs, sems):
        send, recv, loc = sems
        x, y, c, me = _my_place()
        for t in range(self.n_in):
            self._copy(outs, sems, t, 6, 6, 6).wait_recv()
            self._copy(outs, sems, t, 6, 7, 1).start()
        for t in range(self.n_in):
            for k in range(1, N_DEV):
                peer, pidx = _peer(x, y, c, k)
                blk = self._rows(outs, t, pidx)
                if k % 2 == 1:
                    _remote(blk, blk, send, recv, N_DEV * t + k, peer).wait_recv()
                _remote(blk, blk, send, recv, N_DEV * t + k, peer).wait_send()
            self._local(ins, outs, loc, t, me)[1].wait()


class ScatterJob:
    def __init__(self, grads):
        self.args = list(grads)
        self.dims = [(g.shape[0] // N_DEV, g.shape[1]) for g in grads]
        n = self.n_in = self.n_out = len(grads)
        self.out_shape = [jax.ShapeDtypeStruct((N_DEV, r, cc), g.dtype) for g, (r, cc) in zip(grads, self.dims)]
        self.scratch = [pltpu.SemaphoreType.DMA((N_DEV * n,)), pltpu.SemaphoreType.DMA((N_DEV * n,)),
                        pltpu.SemaphoreType.DMA((n,))]

    def _rows(self, ins, t, idx):
        r = self.dims[t][0]
        return ins[t].at[pl.ds(pl.multiple_of(idx * r, 8), r), :]

    def start(self, ins, outs, sems):
        send, recv, loc = sems
        x, y, c, me = _my_place()
        for t in range(self.n_in):
            pltpu.make_async_copy(self._rows(ins, t, me), outs[t].at[me], loc.at[t]).start()
            for k in OTHER_CHIPS + (1, 3, 5, 7):
                peer, pidx = _peer(x, y, c, k)
                _remote(self._rows(ins, t, pidx), outs[t].at[me], send, recv, N_DEV * t + k, peer).start()

    def middle(self, ins, outs, sems):
        pass

    def finish(self, ins, outs, sems):
        send, recv, loc = sems
        x, y, c, me = _my_place()
        for t in range(self.n_in):
            for k in range(1, N_DEV):
                peer, pidx = _peer(x, y, c, k)
                cp = _remote(self._rows(ins, t, pidx), outs[t].at[pidx], send, recv, N_DEV * t + k, peer)
                cp.wait_recv()
                cp.wait_send()
            pltpu.make_async_copy(self._rows(ins, t, me), outs[t].at[me], loc.at[t]).wait()


def _call(body, name, grid, in_specs, args, out_specs, out_shape, scratch=(), sem=(), job=None, aliases=None):
    in_specs, out_specs, out_shape, scratch = list(in_specs), list(out_specs), list(out_shape), list(scratch)
    n_in, n_out, n_scr = len(args), len(out_shape), len(scratch)
    if job is None:
        res = pl.pallas_call(body, name=name, grid=grid, in_specs=in_specs, out_specs=out_specs, out_shape=out_shape,
                             scratch_shapes=scratch, input_output_aliases=aliases or {}, compiler_params=_cp(*sem))(*args)
        return res, None
    o0 = n_in + job.n_in
    s0 = o0 + n_out + job.n_out

    def carrier(*refs):
        jin, jout, jsem = refs[n_in:o0], refs[o0 + n_out:s0], refs[s0 + n_scr:]
        ids = [pl.program_id(a) for a in range(len(grid))]
        def at(step):
            idx = []
            for g in reversed(grid):
                idx.append(step % g)
                step //= g
            return functools.reduce(jnp.logical_and, [i == j for i, j in zip(ids, reversed(idx))])

        steps = math.prod(grid)
        if grid:
            pl.when(at(0))(lambda: job.start(jin, jout, jsem))
            pl.when(at(int(steps * JOB_MIDDLE)))(lambda: job.middle(jin, jout, jsem))
        else:
            job.start(jin, jout, jsem)
            job.middle(jin, jout, jsem)
        body(*refs[:n_in], *refs[o0:o0 + n_out], *refs[s0:s0 + n_scr])
        if grid:
            pl.when(at(steps - 1))(lambda: job.finish(jin, jout, jsem))
        else:
            job.finish(jin, jout, jsem)

    res = pl.pallas_call(
        carrier, name=name + "_comm", grid=grid, in_specs=in_specs + [ANY] * job.n_in,
        out_specs=out_specs + [ANY] * job.n_out, out_shape=out_shape + job.out_shape,
        scratch_shapes=scratch + job.scratch, input_output_aliases=aliases or {},
        compiler_params=_cp(*(["arbitrary"] * len(grid))))(*args, *job.args)
    return res[:n_out], res[n_out:]


def run_job(job, name):
    return _call(lambda: None, name, (), [], [], [], [], job=job)[1]


@jax.custom_vjp
def bmm(x, w):
    return _nn(x.astype(BF16), w.astype(BF16))


def _bmm_fwd(x, w):
    return bmm(x, w), (x, w)


def _bmm_bwd(res, g):
    x, w = res
    gb = g.astype(BF16)
    return _nt(gb, w.astype(BF16)), _tn(x.astype(BF16), gb)


bmm.defvjp(_bmm_fwd, _bmm_bwd)


def _shift_raw(x, d):
    n = x.shape[0]
    r = pltpu.roll(x, d % n, axis=0)
    row = lax.broadcasted_iota(jnp.int32, x.shape, 0)
    keep = (row >= d) if d > 0 else (row < n + d)
    return jnp.where(keep, r, 0.0)


def shift_rows(x, d):
    @jax.custom_vjp
    def f(v):
        return _shift_raw(v, d)

    f.defvjp(lambda v: (_shift_raw(v, d), None), lambda _, g: (_shift_raw(g, -d),))
    return f(x)


def _swap_raw(x):
    n = x.shape[1]
    nxt = pltpu.roll(x, n - 1, axis=1)
    prv = pltpu.roll(x, 1, axis=1)
    lane = lax.broadcasted_iota(jnp.int32, x.shape, 1)
    return jnp.where(lane % 2 == 0, nxt, prv)


@jax.custom_vjp
def swap_pairs(x):
    return _swap_raw(x)


swap_pairs.defvjp(lambda x: (_swap_raw(x), None), lambda _, g: (_swap_raw(g),))


@jax.custom_vjp
def group_mean(x, bd):
    return _split_mm(x, bd)


group_mean.defvjp(lambda x, bd: (_split_mm(x, bd), bd), lambda bd, g: (_split_mm(g, bd), jnp.zeros_like(bd)))


def _rope_norm(x, gain, cos, sgn, bd, scale):
    xn = x * lax.rsqrt(group_mean(x * x, bd) + EPS) * gain
    return (xn * cos + swap_pairs(xn) * sgn) * scale


def _conv_gate(gb, gc, hc, w):
    z = gc * hc
    c = shift_rows(z, 1) * w[0:1] + z * w[1:2] + shift_rows(z, -1) * w[2:3]
    return gb * c


def _window_sum(p, r):
    b = f = p
    k = 1
    while k < r:
        b = b + shift_rows(b, k)
        f = f + shift_rows(f, -k)
        k *= 2
    return b + f - p + shift_rows(p, r) + shift_rows(p, -r)


def _pool_mix(p, pool_w, scale):
    n = p.shape[0]
    t = lax.broadcasted_iota(jnp.int32, (n, 1), 0)
    outs = []
    for gi, r in enumerate(POOL_RADII):
        pg = p[:, gi * GROUP:(gi + 1) * GROUP]
        cnt = (jnp.minimum(t + r, n - 1) - jnp.maximum(t - r, 0) + 1).astype(F32)
        pooled = _window_sum(pg, r) / cnt - pg
        outs.append(bmm(pooled, pool_w[gi]))
    return jnp.concatenate(outs, axis=1) * scale


def _sgu(u, v, norm_g, w_s, b_full):
    ug = jax.nn.gelu(u)
    vg = jax.nn.gelu(v)
    vn = vg * lax.rsqrt(jnp.mean(vg * vg, axis=-1, keepdims=True) + EPS) * norm_g
    cols = []
    for g in range(w_s.shape[0]):
        rows = []
        for n in range(u.shape[0] // SGU_CHUNK):
            blk = vn[n * SGU_CHUNK:(n + 1) * SGU_CHUNK, g * GROUP:(g + 1) * GROUP]
            rows.append(bmm(w_s[g], blk) + b_full[g])
        cols.append(jnp.concatenate(rows, axis=0))
    return ug * jnp.concatenate(cols, axis=1)


def _rms_bwd_math(xv, gain, dy, dres):
    r = lax.rsqrt(jnp.mean(xv * xv, axis=-1, keepdims=True) + EPS)
    xh = xv * r
    dxh = dy * gain
    dx = dres + r * (dxh - xh * jnp.mean(dxh * xh, axis=-1, keepdims=True))
    return dx, jnp.sum(dy * xh, axis=0, keepdims=True)


FFN_TN = 256


def ffn_fwd(x, gain, wt_in, w_out, job=None):
    T, D = x.shape
    F = w_out.shape[0]
    tm, tn = _tile(T, 1024), FFN_TN
    nc = F // tn

    def body(x_ref, gn_ref, wg_ref, wu_ref, wo_ref, y_ref, gu_ref, xn_s, acc_s):
        c = pl.program_id(1)

        @pl.when(c == 0)
        def _():
            xv = x_ref[...]
            r = lax.rsqrt(jnp.mean(xv * xv, axis=-1, keepdims=True) + EPS)
            xn_s[...] = (xv * r * gn_ref[...]).astype(BF16)
            acc_s[...] = jnp.zeros_like(acc_s)

        xn = xn_s[...]
        g = _nt(xn, wg_ref[...])
        u = _nt(xn, wu_ref[...])
        gu_ref[:, :tn] = g.astype(BF16)
        gu_ref[:, tn:] = u.astype(BF16)
        a = (g * jax.nn.sigmoid(g) * u).astype(BF16)
        acc_s[...] += _nn(a, wo_ref[...])

        @pl.when(c == nc - 1)
        def _():
            y_ref[...] = x_ref[...] + 0.5 * acc_s[...]

    row = pl.BlockSpec((tm, D), lambda i, c: (i, 0))
    return _call(
        body, "ffn_fwd", (T // tm, nc),
        [row, pl.BlockSpec((1, D), lambda i, c: (0, 0)),
         pl.BlockSpec((tn, D), lambda i, c: (c, 0)),
         pl.BlockSpec((tn, D), lambda i, c: (c + nc, 0)),
         pl.BlockSpec((tn, D), lambda i, c: (c, 0))],
        [x, gain, wt_in, wt_in, w_out],
        [row, pl.BlockSpec((tm, 2 * tn), lambda i, c: (i, c)), row],
        [jax.ShapeDtypeStruct((T, D), F32), jax.ShapeDtypeStruct((T, 2 * F), BF16), jax.ShapeDtypeStruct((T, D), BF16)],
        [pltpu.VMEM((tm, D), F32)], ("parallel", "arbitrary"), job)


def ffn_bwd_x(dout, x, gain, gu, wt_in, w_out, job=None):
    T, D = x.shape
    F = w_out.shape[0]
    tm, tn = _tile(T, 1024), FFN_TN
    nc = F // tn

    def body(do_ref, x_ref, gn_ref, gu_ref, wg_ref, wu_ref, wo_ref,
             dx_ref, dgn_ref, a_ref, dgu_ref, dob_ref, acc_s):
        i, c = pl.program_id(0), pl.program_id(1)

        @pl.when(c == 0)
        def _():
            dob_ref[...] = (0.5 * do_ref[...]).astype(BF16)
            acc_s[...] = jnp.zeros_like(acc_s)

        da = jnp.concatenate([_nt(dob_ref[:tm // 2, :], wo_ref[...]), _nt(dob_ref[tm // 2:, :], wo_ref[...])], axis=0)
        g = gu_ref[:, :tn].astype(F32)
        u = gu_ref[:, tn:].astype(F32)
        sig = jax.nn.sigmoid(g)
        sl = g * sig
        a_ref[...] = (sl * u).astype(BF16)
        dg = (da * u * (sig * (1.0 + g * (1.0 - sig)))).astype(BF16)
        du = (da * sl).astype(BF16)
        dgu_ref[:, :tn] = dg
        dgu_ref[:, tn:] = du
        acc_s[...] += _nn(dg, wg_ref[...]) + _nn(du, wu_ref[...])

        @pl.when(c == nc - 1)
        def _():
            dx, dgn = _rms_bwd_math(x_ref[...], gn_ref[...], acc_s[...], do_ref[...])
            dx_ref[...] = dx

            @pl.when(i == 0)
            def _():
                dgn_ref[...] = jnp.zeros_like(dgn_ref)

            dgn_ref[...] += dgn

    row = pl.BlockSpec((tm, D), lambda i, c: (i, 0))
    return _call(
        body, "ffn_bwd_x", (T // tm, nc),
        [row, row, pl.BlockSpec((1, D), lambda i, c: (0, 0)),
         pl.BlockSpec((tm, 2 * tn), lambda i, c: (i, c)),
         pl.BlockSpec((tn, D), lambda i, c: (c, 0)),
         pl.BlockSpec((tn, D), lambda i, c: (c + nc, 0)),
         pl.BlockSpec((tn, D), lambda i, c: (c, 0))],
        [dout, x, gain, gu, wt_in, wt_in, w_out],
        [row, pl.BlockSpec((1, D), lambda i, c: (0, 0)),
         pl.BlockSpec((tm, tn), lambda i, c: (i, c)),
         pl.BlockSpec((tm, 2 * tn), lambda i, c: (i, c)), row],
        [jax.ShapeDtypeStruct((T, D), F32), jax.ShapeDtypeStruct((1, D), F32),
         jax.ShapeDtypeStruct((T, F), BF16), jax.ShapeDtypeStruct((T, 2 * F), BF16),
         jax.ShapeDtypeStruct((T, D), BF16)],
        [pltpu.VMEM((tm, D), F32)], ("arbitrary", "arbitrary"), job)


MM_TM = 512
MM_TC = 256


def mm_nt(a, wt, pieces, out_dtype, emit_a_bf16=False, norm_gain=None):
    T, K = a.shape
    tm = _tile(T, MM_TM)
    npc = len(pieces)
    n_lead = 1 if norm_gain is None else 2

    def body(*refs):
        a_ref, w_refs, o_refs = refs[0], refs[n_lead:n_lead + npc], refs[n_lead + npc:]
        av = a_ref[...]
        if norm_gain is not None:
            av = av * lax.rsqrt(jnp.mean(av * av, axis=-1, keepdims=True) + EPS) * refs[1][...]
        ab = av.astype(BF16)
        for w_ref, o_ref in zip(w_refs, o_refs[:npc]):
            o_ref[...] = _nt(ab, w_ref[...]).astype(o_ref.dtype)
        if emit_a_bf16:
            o_refs[npc][...] = ab

    in_specs = [pl.BlockSpec((tm, K), lambda i: (i, 0))]
    if norm_gain is not None:
        in_specs.append(pl.BlockSpec((1, K), lambda i: (0, 0)))
    out_specs, out_shape = [], []
    for r0, n in pieces:
        assert r0 % n == 0
        in_specs.append(pl.BlockSpec((n, K), functools.partial(lambda i, b: (b, 0), b=r0 // n)))
        out_specs.append(pl.BlockSpec((tm, n), lambda i: (i, 0)))
        out_shape.append(jax.ShapeDtypeStruct((T, n), out_dtype))
    if emit_a_bf16:
        out_specs.append(pl.BlockSpec((tm, K), lambda i: (i, 0)))
        out_shape.append(jax.ShapeDtypeStruct((T, K), BF16))
    return pl.pallas_call(
        body, name="mm_nt", grid=(T // tm,), in_specs=in_specs, out_specs=out_specs, out_shape=out_shape,
        compiler_params=_cp("parallel"),
    )(a, *([] if norm_gain is None else [norm_gain]), *([wt] * npc))


def mm_nn(pieces, w, residual=None, norm_bwd=None):
    T = pieces[0][0].shape[0]
    N = w.shape[1]
    tm = _tile(T, MM_TM)
    na = len(pieces)

    def body(*refs):
        a_refs, w_refs = refs[:na], refs[na:2 * na]
        acc = refs[2 * na][...] if residual is not None else None
        for a_ref, w_ref in zip(a_refs, w_refs):
            t = _nn(a_ref[...].astype(BF16), w_ref[...])
            acc = t if acc is None else acc + t
        if norm_bwd is None:
            refs[-1][...] = acc
            return
        x_ref, g_ref, dr_ref, dx_ref, dg_ref = refs[-5:]
        dx, dg = _rms_bwd_math(x_ref[...], g_ref[...], acc, dr_ref[...])
        dx_ref[...] = dx

        @pl.when(pl.program_id(0) == 0)
        def _():
            dg_ref[...] = jnp.zeros_like(dg_ref)

        dg_ref[...] += dg

    in_specs, w_specs = [], []
    for a, cb, k, rb in pieces:
        in_specs.append(pl.BlockSpec((tm, k), functools.partial(lambda i, b: (i, b), b=cb)))
        w_specs.append(pl.BlockSpec((k, N), functools.partial(lambda i, b: (b, 0), b=rb)))
    assert sum(k for _, _, k, _ in pieces) == w.shape[0]
    args = [a for a, _, _, _ in pieces] + [w] * na
    in_specs = in_specs + w_specs
    row = pl.BlockSpec((tm, N), lambda i: (i, 0))
    if residual is not None:
        in_specs.append(row)
        args.append(residual)
    if norm_bwd is None:
        return pl.pallas_call(
            body, name="mm_nn", grid=(T // tm,), in_specs=in_specs, out_specs=row,
            out_shape=jax.ShapeDtypeStruct((T, N), F32), compiler_params=_cp("parallel"),
        )(*args)
    vec = pl.BlockSpec((1, N), lambda i: (0, 0))
    return pl.pallas_call(
        body, name="mm_nn_norm_bwd", grid=(T // tm,), in_specs=in_specs + [row, vec, row], out_specs=[row, vec],
        out_shape=[jax.ShapeDtypeStruct((T, N), F32), jax.ShapeDtypeStruct((1, N), F32)],
        compiler_params=_cp("arbitrary"),
    )(*args, *norm_bwd)


def mm_tn(a, b, n_rows, row_block, prev=None, grid=None, col_block=None, job=None):
    T, M = a.shape
    N = b.shape[1]
    tc = MM_TC
    assert M % tc == 0 and n_rows % tc == 0
    if grid is None:
        grid, col_block = (M // tc,), (lambda j: j)

    def body(*refs):
        a_ref, b_ref, o_ref = refs[0], refs[1], refs[-1]
        o_ref[...] = _tn(a_ref[...], b_ref[...]).astype(BF16)

    in_specs = [pl.BlockSpec((T, tc), lambda *g: (0, col_block(*g))), pl.BlockSpec((T, N), lambda *g: (0, 0))]
    args = [a, b]
    aliases = {}
    if prev is not None:
        in_specs.append(pl.BlockSpec(memory_space=pl.ANY))
        args.append(prev)
        aliases = {2: 0}
    res, jres = _call(body, "mm_tn", grid, in_specs, args, [pl.BlockSpec((tc, N), lambda *g: (row_block(*g), 0))],
                      [jax.ShapeDtypeStruct((n_rows, N), BF16)], (), ["parallel"] * len(grid), job, aliases)
    return res[0] if job is None else (res[0], jres)


def conv_fwd(proj_a, conv_w, n_ex):
    T, C3 = proj_a.shape
    C = C3 // 3
    S = T // n_ex

    def body(gb_ref, gc_ref, hc_ref, w_ref, o_ref):
        o_ref[...] = _conv_gate(gb_ref[...], gc_ref[...], hc_ref[...], w_ref[...]).astype(BF16)

    col = lambda k: pl.BlockSpec((S, C), functools.partial(lambda b, kk: (b, kk), kk=k))
    return pl.pallas_call(
        body, name="conv_fwd", grid=(n_ex,),
        in_specs=[col(0), col(1), col(2), pl.BlockSpec((3, C), lambda b: (0, 0))],
        out_specs=pl.BlockSpec((S, C), lambda b: (b, 0)),
        out_shape=jax.ShapeDtypeStruct((T, 2 * C), BF16), compiler_params=_cp("parallel"),
    )(proj_a, proj_a, proj_a, conv_w)


def conv_bwd(proj_a, conv_w, n_ex, dy, total_cols):
    T, C3 = proj_a.shape
    C = C3 // 3
    S = T // n_ex

    def body(gb_ref, gc_ref, hc_ref, w_ref, dy_ref, dp_ref, dw_ref):
        _, vjp = jax.vjp(_conv_gate, gb_ref[...], gc_ref[...], hc_ref[...], w_ref[...])
        dgb, dgc, dhc, dw = vjp(dy_ref[...].astype(F32))
        dp_ref[:, 0:C] = dgb.astype(BF16)
        dp_ref[:, C:2 * C] = dgc.astype(BF16)
        dp_ref[:, 2 * C:] = dhc.astype(BF16)

        @pl.when(pl.program_id(0) == 0)
        def _():
            dw_ref[...] = jnp.zeros_like(dw_ref)

        dw_ref[...] += dw

    col = lambda k: pl.BlockSpec((S, C), functools.partial(lambda b, kk: (b, kk), kk=k))
    return pl.pallas_call(
        body, name="conv_bwd", grid=(n_ex,),
        in_specs=[col(0), col(1), col(2), pl.BlockSpec((3, C), lambda b: (0, 0)),
                  pl.BlockSpec((S, C), lambda b: (b, 0))],
        out_specs=[pl.BlockSpec((S, C3), lambda b: (b, 0)), pl.BlockSpec((3, C), lambda b: (0, 0))],
        out_shape=[jax.ShapeDtypeStruct((T, total_cols), BF16), jax.ShapeDtypeStruct((3, C), F32)],
        compiler_params=_cp("arbitrary"),
    )(proj_a, proj_a, proj_a, conv_w, dy)


QW = N_Q_HEADS * HEAD_DIM
KW = N_KV_HEADS * HEAD_DIM
QP = N_Q_HEADS * LANES
KP = N_KV_HEADS * LANES


def _attn_consts(seq):
    rows = seq // GRID_W
    r_idx, c_idx = jnp.meshgrid(jnp.arange(rows), jnp.arange(GRID_W), indexing='ij')
    r_idx = r_idx.reshape(-1).astype(F32)
    c_idx = c_idx.reshape(-1).astype(F32)
    n_freq = HEAD_DIM // 4
    inv = ROPE_THETA ** (-jnp.arange(n_freq, dtype=F32) / n_freq)
    ang = jnp.concatenate([r_idx[:, None] * inv, c_idx[:, None] * inv], axis=-1)
    cos = jnp.repeat(jnp.cos(ang), 2, axis=1)
    sin = jnp.repeat(jnp.sin(ang), 2, axis=1)
    sgn = sin * jnp.tile(jnp.array([-1.0, 1.0], F32), HEAD_DIM // 2)
    cos = jnp.tile(cos, (1, N_Q_HEADS))
    sgn = jnp.tile(sgn, (1, N_Q_HEADS))
    lane = jnp.arange(QW)
    bd = jnp.where(lane[:, None] // HEAD_DIM == lane[None, :] // HEAD_DIM, 1.0 / HEAD_DIM, 0.0).astype(BF16)
    dst = (lane // HEAD_DIM) * LANES + lane % HEAD_DIM
    spread = (dst[:, None] == jnp.arange(QP)[None, :]).astype(BF16)
    return dict(cos=cos, sgn=sgn, bd=bd, spread=spread, gather=spread.T)


def qkv_prep_fwd(proj_b, qg, kg, cst, n_ex):
    T = proj_b.shape[0]
    S = T // n_ex
    tm = _tile(S, 512)
    nb = S // tm

    def body(p_ref, qg_ref, kg_ref, cos_ref, sgn_ref, bd_ref, sp_ref, q_ref, k_ref, v_ref):
        pv = p_ref[...]
        cos, sgn, bd, sp = cos_ref[...], sgn_ref[...], bd_ref[...], sp_ref[...]
        qr = _rope_norm(pv[:, :QW], qg_ref[...], cos, sgn, bd, HEAD_DIM ** -0.5)
        kr = _rope_norm(pv[:, QW:QW + KW], kg_ref[...], cos[:, :KW], sgn[:, :KW], bd[:KW, :KW], 1.0)
        q_ref[...] = _nn(qr.astype(BF16), sp).astype(BF16)
        k_ref[...] = _nn(kr.astype(BF16), sp[:KW, :KP]).astype(BF16)
        v_ref[...] = _nn(pv[:, QW + KW:].astype(BF16), sp[:KW, :KP]).astype(BF16)

    full = lambda a: pl.BlockSpec(a.shape, lambda i: (0,) * a.ndim)
    tab = pl.BlockSpec((tm, QW), lambda i: (i % nb, 0))
    return pl.pallas_call(
        body, name="qkv_prep_fwd", grid=(T // tm,),
        in_specs=[pl.BlockSpec((tm, QW + 2 * KW), lambda i: (i, 0)), full(qg), full(kg), tab, tab,
                  full(cst["bd"]), full(cst["spread"])],
        out_specs=[pl.BlockSpec((tm, QP), lambda i: (i, 0)), pl.BlockSpec((tm, KP), lambda i: (i, 0)),
                   pl.BlockSpec((tm, KP), lambda i: (i, 0))],
        out_shape=[jax.ShapeDtypeStruct((T, QP), BF16), jax.ShapeDtypeStruct((T, KP), BF16),
                   jax.ShapeDtypeStruct((T, KP), BF16)],
        compiler_params=_cp("parallel"),
    )(proj_b, qg, kg, cst["cos"], cst["sgn"], cst["bd"], cst["spread"])


def qkv_prep_bwd(proj_b, qg, kg, cst, n_ex, dq, dk_pad, dv_pad, d_proj):
    T = proj_b.shape[0]
    S = T // n_ex
    tm = _tile(S, 512)
    nb = S // tm

    def body(p_ref, qg_ref, kg_ref, cos_ref, sgn_ref, bd_ref, ga_ref, dq_ref, dk_ref, dv_ref, _kept,
             dp_ref, dqg_ref, dkg_ref):
        pv = p_ref[...]
        cos, sgn, bd, ga = cos_ref[...], sgn_ref[...], bd_ref[...], ga_ref[...]
        fq = lambda q, g: _rope_norm(q, g, cos, sgn, bd, HEAD_DIM ** -0.5)
        fk = lambda k, g: _rope_norm(k, g, cos[:, :KW], sgn[:, :KW], bd[:KW, :KW], 1.0)
        _, vq = jax.vjp(fq, pv[:, :QW], qg_ref[...])
        _, vk = jax.vjp(fk, pv[:, QW:QW + KW], kg_ref[...])
        dqp, dqg = vq(dq_ref[...])
        dkp, dkg = vk(_split_mm(dk_ref[...], ga[:KP, :KW]))
        dp_ref[:, :QW] = dqp.astype(BF16)
        dp_ref[:, QW:QW + KW] = dkp.astype(BF16)
        dp_ref[:, QW + KW:] = _split_mm(dv_ref[...], ga[:KP, :KW]).astype(BF16)

        @pl.when(pl.program_id(0) == 0)
        def _():
            dqg_ref[...] = jnp.zeros_like(dqg_ref)
            dkg_ref[...] = jnp.zeros_like(dkg_ref)

        dqg_ref[...] += dqg
        dkg_ref[...] += dkg

    full = lambda a: pl.BlockSpec(a.shape, lambda i: (0,) * a.ndim)
    tab = pl.BlockSpec((tm, QW), lambda i: (i % nb, 0))
    row = lambda n: pl.BlockSpec((tm, n), lambda i: (i, 0))
    wb = QW + 2 * KW
    assert d_proj.shape[1] % wb == 0
    last = d_proj.shape[1] // wb - 1
    return pl.pallas_call(
        body, name="qkv_prep_bwd", grid=(T // tm,),
        in_specs=[row(wb), full(qg), full(kg), tab, tab, full(cst["bd"]), full(cst["gather"]),
                  row(QW), row(KP), row(KP), ANY],
        out_specs=[pl.BlockSpec((tm, wb), lambda i: (i, last)), pl.BlockSpec((1, QW), lambda i: (0, 0)),
                   pl.BlockSpec((1, KW), lambda i: (0, 0))],
        out_shape=[jax.ShapeDtypeStruct(d_proj.shape, BF16), jax.ShapeDtypeStruct((1, QW), F32),
                   jax.ShapeDtypeStruct((1, KW), F32)],
        input_output_aliases={10: 0}, compiler_params=_cp("arbitrary"),
    )(proj_b, qg, kg, cst["cos"], cst["sgn"], cst["bd"], cst["gather"], dq, dk_pad, dv_pad, d_proj)


ATT_TQ = 256
ATT_TQ_FWD = 512


def attn_fwd(qp, kp, vp, gather, n_ex, mix, job=None):
    T = qp.shape[0]
    S = T // n_ex
    tq = _tile(S, ATT_TQ_FWD)
    nq = S // tq

    def body(q_ref, k_ref, v_ref, ga_ref, _kept, o_ref, op_ref, lse_ref):
        lane = lax.broadcasted_iota(jnp.int32, (tq, LANES), 1)
        lse_all = jnp.zeros((tq, LANES), F32)
        for h in range(N_Q_HEADS):
            kv = h // Q_PER_KV
            qh = q_ref[:, h * LANES:(h + 1) * LANES]
            s = _nt(qh, k_ref[:, kv * LANES:(kv + 1) * LANES])
            m = jnp.max(s, axis=-1, keepdims=True)
            p = jnp.exp(s - m)
            lsum = jnp.sum(p, axis=-1, keepdims=True)
            o = _nn(p.astype(BF16), v_ref[:, kv * LANES:(kv + 1) * LANES]) / lsum
            op_ref[:, h * LANES:(h + 1) * LANES] = o.astype(BF16)
            lse_all = jnp.where(lane == h, m + jnp.log(lsum), lse_all)
        lse_ref[...] = lse_all
        o_ref[...] = _nn(op_ref[...], ga_ref[...]).astype(BF16)

    blk = lambda n: pl.BlockSpec((tq, n), lambda b, i: (b * nq + i, 0))
    kvs = pl.BlockSpec((S, KP), lambda b, i: (b, 0))
    return _call(
        body, "attn_fwd", (n_ex, nq),
        [blk(QP), kvs, kvs, pl.BlockSpec(gather.shape, lambda b, i: (0, 0)), ANY], [qp, kp, vp, gather, mix],
        [pl.BlockSpec((tq, QW), lambda b, i: (b * nq + i, 1)), blk(QP), blk(LANES)],
        [jax.ShapeDtypeStruct(mix.shape, BF16), jax.ShapeDtypeStruct((T, QP), BF16),
         jax.ShapeDtypeStruct((T, LANES), F32)], (), ("parallel", "parallel"), job, {4: 0})


def attn_bwd(qp, kp, vp, op, lse, do, cst, n_ex, job=None):
    T = qp.shape[0]
    S = T // n_ex
    tq = _tile(S, ATT_TQ)
    nq = S // tq

    def body(q_ref, k_ref, v_ref, op_ref, lse_ref, do_ref, sp_ref, ga_ref, dq_ref, dk_ref, dv_ref, dqp_s):
        @pl.when(pl.program_id(1) == 0)
        def _():
            dk_ref[...] = jnp.zeros_like(dk_ref)
            dv_ref[...] = jnp.zeros_like(dv_ref)

        lane = lax.broadcasted_iota(jnp.int32, (tq, LANES), 1)
        dop = _nn(do_ref[...], sp_ref[...]).astype(BF16)
        lse_all = lse_ref[...]
        for h in range(N_Q_HEADS):
            kv = h // Q_PER_KV
            hs = slice(h * LANES, (h + 1) * LANES)
            ks = slice(kv * LANES, (kv + 1) * LANES)
            qh, kk, vv = q_ref[:, hs], k_ref[:, ks], v_ref[:, ks]
            doh = dop[:, hs]
            lse_h = jnp.sum(jnp.where(lane == h, lse_all, 0.0), axis=-1, keepdims=True)
            p = jnp.exp(_nt(qh, kk) - lse_h)
            dp = _nt(doh, vv)
            delta = jnp.sum(doh.astype(F32) * op_ref[:, hs].astype(F32), axis=-1, keepdims=True)
            ds = (p * (dp - delta)).astype(BF16)
            dqp_s[:, hs] = _nn(ds, kk)
            dk_ref[:, ks] += _tn(ds, qh)
            dv_ref[:, ks] += _tn(p.astype(BF16), doh)
        dq_ref[...] = _split_mm(dqp_s[...], ga_ref[...])

    blk = lambda n: pl.BlockSpec((tq, n), lambda b, i: (b * nq + i, 0))
    kvs = pl.BlockSpec((S, KP), lambda b, i: (b, 0))
    full = lambda a: pl.BlockSpec(a.shape, lambda b, i: (0, 0))
    return _call(
        body, "attn_bwd", (n_ex, nq),
        [blk(QP), kvs, kvs, blk(QP), blk(LANES), pl.BlockSpec((tq, QW), lambda b, i: (b * nq + i, 1)),
         full(cst["spread"]), full(cst["gather"])],
        [qp, kp, vp, op, lse, do, cst["spread"], cst["gather"]],
        [blk(QW), kvs, kvs],
        [jax.ShapeDtypeStruct((T, QW), F32), jax.ShapeDtypeStruct((T, KP), F32), jax.ShapeDtypeStruct((T, KP), F32)],
        [pltpu.VMEM((tq, QP), F32)], ("arbitrary", "arbitrary"), job)


def pool_fwd(p, pool_w, scale, n_ex):
    T, W = p.shape
    S = T // n_ex

    def body(p_ref, w_ref, s_ref, o_ref):
        o_ref[...] = _pool_mix(p_ref[...], w_ref[...], s_ref[...]).astype(BF16)

    return pl.pallas_call(
        body, name="pool_fwd", grid=(n_ex,),
        in_specs=[pl.BlockSpec((S, W), lambda b: (b, 0)),
                  pl.BlockSpec(pool_w.shape, lambda b: (0, 0, 0)),
                  pl.BlockSpec((1, W), lambda b: (0, 0))],
        out_specs=pl.BlockSpec((S, W), lambda b: (b, 0)),
        out_shape=jax.ShapeDtypeStruct((T, 2 * W), BF16), compiler_params=_cp("parallel"),
    )(p, pool_w, scale)


def pool_bwd(p, pool_w, scale, n_ex, dy, d_proj):
    T, W = p.shape
    S = T // n_ex
    last = d_proj.shape[1] // W - 1

    def body(p_ref, w_ref, s_ref, dy_ref, _kept, dp_ref, dw_ref, ds_ref):
        _, vjp = jax.vjp(_pool_mix, p_ref[...], w_ref[...], s_ref[...])
        dp, dw, ds = vjp(dy_ref[...].astype(F32))
        dp_ref[...] = dp.astype(BF16)

        @pl.when(pl.program_id(0) == 0)
        def _():
            dw_ref[...] = jnp.zeros_like(dw_ref)
            ds_ref[...] = jnp.zeros_like(ds_ref)

        dw_ref[...] += dw
        ds_ref[...] += ds

    wshape = pool_w.shape
    return pl.pallas_call(
        body, name="pool_bwd", grid=(n_ex,),
        in_specs=[pl.BlockSpec((S, W), lambda b: (b, 0)),
                  pl.BlockSpec(wshape, lambda b: (0, 0, 0)),
                  pl.BlockSpec((1, W), lambda b: (0, 0)), pl.BlockSpec((S, W), lambda b: (b, 0)), ANY],
        out_specs=[pl.BlockSpec((S, W), lambda b: (b, last)), pl.BlockSpec(wshape, lambda b: (0, 0, 0)),
                   pl.BlockSpec((1, W), lambda b: (0, 0))],
        out_shape=[jax.ShapeDtypeStruct(d_proj.shape, BF16), jax.ShapeDtypeStruct(wshape, F32),
                   jax.ShapeDtypeStruct((1, W), F32)],
        input_output_aliases={4: 0}, compiler_params=_cp("arbitrary"),
    )(p, pool_w, scale, dy, d_proj)


SGU_TS = 512


def sgu_fwd(u, v, norm_g, w_s, b_full, mix):
    T, W = u.shape
    ts = _tile(T, SGU_TS)

    def body(u_ref, v_ref, g_ref, w_ref, b_ref, _kept, o_ref):
        o_ref[...] = _sgu(u_ref[...], v_ref[...], g_ref[...], w_ref[...], b_ref[...]).astype(BF16)

    row = pl.BlockSpec((ts, W), lambda i: (i, 0))
    wsp = pl.BlockSpec(w_s.shape, lambda i: (0, 0, 0))
    return pl.pallas_call(
        body, name="sgu_fwd", grid=(T // ts,),
        in_specs=[row, row, pl.BlockSpec((1, W), lambda i: (0, 0)), wsp, wsp, ANY],
        out_specs=pl.BlockSpec((ts, W), lambda i: (i, 1)), out_shape=jax.ShapeDtypeStruct(mix.shape, BF16),
        input_output_aliases={5: 0}, compiler_params=_cp("parallel"),
    )(u, v, norm_g, w_s, b_full, mix)


def sgu_bwd(u, v, norm_g, w_s, b_full, dy):
    T, W = u.shape
    ts = _tile(T, SGU_TS)
    wshape = w_s.shape

    def body(u_ref, v_ref, g_ref, w_ref, b_ref, dy_ref, duv_ref, dg_ref, dw_ref, db_ref):
        _, vjp = jax.vjp(_sgu, u_ref[...], v_ref[...], g_ref[...], w_ref[...], b_ref[...])
        du, dv, dg, dw, db = vjp(dy_ref[...].astype(F32))
        duv_ref[:, :W] = du.astype(BF16)
        duv_ref[:, W:] = dv.astype(BF16)

        @pl.when(pl.program_id(0) == 0)
        def _():
            dg_ref[...] = jnp.zeros_like(dg_ref)
            dw_ref[...] = jnp.zeros_like(dw_ref)
            db_ref[...] = jnp.zeros_like(db_ref)

        dg_ref[...] += dg
        dw_ref[...] += dw
        db_ref[...] += db

    row = pl.BlockSpec((ts, W), lambda i: (i, 0))
    wsp = pl.BlockSpec(wshape, lambda i: (0, 0, 0))
    wout = pl.BlockSpec(wshape, lambda i: (0, 0, 0))
    vec = pl.BlockSpec((1, W), lambda i: (0, 0))
    return pl.pallas_call(
        body, name="sgu_bwd", grid=(T // ts,),
        in_specs=[row, row, vec, wsp, wsp, pl.BlockSpec((ts, W), lambda i: (i, 1))],
        out_specs=[pl.BlockSpec((ts, 2 * W), lambda i: (i, 0)), vec, wout, wout],
        out_shape=[jax.ShapeDtypeStruct((T, 3 * W), BF16),
                   jax.ShapeDtypeStruct((1, W), F32), jax.ShapeDtypeStruct(wshape, F32),
                   jax.ShapeDtypeStruct(wshape, F32)],
        compiler_params=_cp("arbitrary"),
    )(u, v, norm_g, w_s, b_full, dy)


def loss_head(x, gain, target):
    T, D = x.shape
    tm = _tile(T, 512)

    def body(x_ref, g_ref, t_ref, loss_ref, dx_ref, dg_ref):
        xv, g = x_ref[...], g_ref[...]
        r = lax.rsqrt(jnp.mean(xv * xv, axis=-1, keepdims=True) + EPS)
        err = xv * r * g - t_ref[...]
        part = 0.5 * jnp.sum(jnp.mean(err * err, axis=-1, keepdims=True), axis=0, keepdims=True)
        dx, dg = _rms_bwd_math(xv, g, err * (1.0 / D), jnp.zeros_like(xv))
        dx_ref[...] = dx

        @pl.when(pl.program_id(0) == 0)
        def _():
            loss_ref[...] = jnp.zeros_like(loss_ref)
            dg_ref[...] = jnp.zeros_like(dg_ref)

        loss_ref[...] += part
        dg_ref[...] += dg

    row = pl.BlockSpec((tm, D), lambda i: (i, 0))
    vec = pl.BlockSpec((1, D), lambda i: (0, 0))
    return pl.pallas_call(
        body, name="loss_head", grid=(T // tm,),
        in_specs=[row, vec, row], out_specs=[pl.BlockSpec((1, 1), lambda i: (0, 0)), row, vec],
        out_shape=[jax.ShapeDtypeStruct((1, 1), F32), jax.ShapeDtypeStruct((T, D), F32),
                   jax.ShapeDtypeStruct((1, D), F32)],
        compiler_params=_cp("arbitrary"),
    )(x, gain, target)


class MultiJob:
    def __init__(self, jobs):
        self.jobs = jobs
        self.args = [a for j in jobs for a in j.args]
        self.out_shape = [s for j in jobs for s in j.out_shape]
        self.scratch = [s for j in jobs for s in j.scratch]
        self.n_in, self.n_out = len(self.args), len(self.out_shape)

    def _each(self, ins, outs, sems):
        i = o = s = 0
        for j in self.jobs:
            yield j, ins[i:i + j.n_in], outs[o:o + j.n_out], sems[s:s + len(j.scratch)]
            i, o, s = i + j.n_in, o + j.n_out, s + len(j.scratch)

    def start(self, ins, outs, sems):
        for j, a, b, c in self._each(ins, outs, sems):
            j.start(a, b, c)

    def middle(self, ins, outs, sems):
        for j, a, b, c in self._each(ins, outs, sems):
            j.middle(a, b, c)

    def finish(self, ins, outs, sems):
        for j, a, b, c in self._each(ins, outs, sems):
            j.finish(a, b, c)

    def split(self, results):
        o = 0
        for j in self.jobs:
            yield results[o:o + j.n_out]
            o += j.n_out


class Plan:
    def __init__(self, shard, gathers, scatters, small_carrier=None, pack_small=None):
        self.shard, self.gathers, self.scatters = shard, gathers, scatters
        self.small_carrier, self.pack_small = small_carrier, pack_small
        self.weights, self.grads, self.parts = {}, {}, {}
        self.small_src = self.small_parts = None

    def weight(self, kind, l):
        return self.weights[(kind, l)]

    def grad(self, kind, l, g):
        self.grads[(kind, l)] = g

    def small_ready(self, small, d_final):
        if self.pack_small is not None:
            self.small_src = self.pack_small(small, d_final)

    def _jobs(self, key):
        jobs = []
        if key in self.gathers:
            ks = self.gathers[key]
            jobs.append((GatherJob([self.shard(*k) for k in ks]), self.weights, ks))
        if key in self.scatters:
            ks = self.scatters[key]
            jobs.append((ScatterJob([self.grads[k] for k in ks]), self.parts, ks))
        if key == self.small_carrier and self.small_src is not None:
            jobs.append((GatherJob([(self.small_src, None)]), None, None))
        return jobs

    def _deliver(self, jobs, results):
        multi = MultiJob([j for j, _, _ in jobs])
        for (_, store, ks), res in zip(jobs, multi.split(results)):
            if store is None:
                self.small_parts = res[0]
            else:
                store.update(zip(ks, res))

    def run(self, key, fn, *args, **kw):
        jobs = self._jobs(key)
        if not jobs:
            out = fn(*args, **kw)
            return out if fn is mm_tn else out[0]
        res, jres = fn(*args, job=MultiJob([j for j, _, _ in jobs]), **kw)
        self._deliver(jobs, jres)
        return res

    def alone(self, key, name):
        jobs = self._jobs(key)
        if jobs:
            self._deliver(jobs, run_job(MultiJob([j for j, _, _ in jobs]), name))


def _local_step(x, target, layers, final_norm, n_ex, plan):
    T, D = x.shape
    L = len(layers)
    cst = _attn_consts(T // n_ex)
    ident = lambda j: j
    EV_A, EV_B = 3 * (D // 2), QW + 2 * KW
    OD_W = D // 2
    wt = plan.weight

    saved = []
    for l, W in enumerate(layers):
        s = dict(x0=x)
        x1, *s["gu1"] = plan.run(("ffn1_fwd", l), ffn_fwd, x, W["n1"], wt("f1_in_t", l), wt("f1_out", l))
        if l % 2 == 0:
            pa, pb, h = mm_nt(x1, wt("mx_in_t", l), [(0, EV_A), (EV_A, EV_B)], F32, True, W["nm"])
            qg = jnp.tile(W["q_norm"], N_Q_HEADS)[None]
            kg = jnp.tile(W["k_norm"], N_KV_HEADS)[None]
            mix = conv_fwd(pa, W["conv_w"], n_ex)
            qp, kp, vp = qkv_prep_fwd(pb, qg, kg, cst, n_ex)
            mix, op, lse = plan.run(("attn_fwd", l), attn_fwd, qp, kp, vp, cst["gather"], n_ex, mix)
            s.update(pa=pa, pb=pb, qg=qg, kg=kg, qp=qp, kp=kp, vp=vp, op=op, lse=lse)
        else:
            p, u, v, h = mm_nt(x1, wt("mx_in_t", l), [(0, OD_W), (OD_W, OD_W), (2 * OD_W, OD_W)], F32, True, W["nm"])
            scale = W["pool_scale"][None]
            sn = W["sgu_norm"][None]
            b_full = jnp.broadcast_to(W["sgu_b"][..., None], W["sgu_w"].shape)
            mix = sgu_fwd(u, v, sn, W["sgu_w"], b_full, pool_fwd(p, W["pool_w"], scale, n_ex))
            s.update(p=p, u=u, v=v, scale=scale, sn=sn, b_full=b_full)
        x2 = mm_nn([(mix, 0, D, 0)], wt("mx_out", l), residual=x1)
        x3, *s["gu2"] = plan.run(("ffn2_fwd", l), ffn_fwd, x2, W["n2"], wt("f2_in_t", l), wt("f2_out", l))
        s.update(x1=x1, x2=x2, h=h, mix=mix)
        saved.append(s)
        x = x3

    loss, dx, d_final = loss_head(x, final_norm, target)

    small = [None] * L

    def ffn_back(which, l, dout, xin, gain, gu_xn, sm, sm_key):
        w_in, w_out = wt(which + "_in_t", l), wt(which + "_out", l)
        F = w_out.shape[0]
        nc = F // FFN_TN
        gu, xn = gu_xn
        dxi, sm[sm_key], a, dgu, dob = plan.run((which + "_bwd", l), ffn_bwd_x, dout, xin, gain, gu, w_in, w_out)
        if which == "f1" and l == 0:
            plan.small_ready(small, d_final)
        plan.grad(which + "_in_t", l, plan.run(
            (which + "_in_grad", l), mm_tn, dgu, xn, 2 * F, lambda k, c: k * nc + c, grid=(2, nc),
            col_block=lambda k, c: 2 * c + k))
        plan.grad(which + "_out", l, plan.run((which + "_out_grad", l), mm_tn, a, dob, F, ident))
        return dxi

    for l in reversed(range(L)):
        s, W = saved[l], layers[l]
        sm = small[l] = {}
        dx = ffn_back("f2", l, dx, s["x2"], W["n2"], s["gu2"], sm, "n2")
        dmix, dxb = mm_nt(dx, wt("mx_out", l), [(0, D)], BF16, emit_a_bf16=True)
        plan.grad("mx_out", l, mm_tn(s["mix"], dxb, D, ident))
        if l % 2 == 0:
            d_proj, sm["conv_w"] = conv_bwd(s["pa"], W["conv_w"], n_ex, dmix, EV_A + EV_B)
            dq, dkp, dvp = plan.run(("attn_bwd", l), attn_bwd, s["qp"], s["kp"], s["vp"], s["op"], s["lse"], dmix, cst, n_ex)
            d_proj, dqg, dkg = qkv_prep_bwd(s["pb"], s["qg"], s["kg"], cst, n_ex, dq, dkp, dvp, d_proj)
            d_pieces = [(d_proj, 0, EV_A + EV_B, 0)]
            plan.grad("mx_in_t", l, mm_tn(d_proj, s["h"], EV_A + EV_B, ident))
            sm["q_norm"] = dqg.reshape(N_Q_HEADS, HEAD_DIM).sum(0)
            sm["k_norm"] = dkg.reshape(N_KV_HEADS, HEAD_DIM).sum(0)
        else:
            d_proj, d_sn, sm["sgu_w"], d_sb = sgu_bwd(s["u"], s["v"], s["sn"], W["sgu_w"], s["b_full"], dmix)
            d_proj, sm["pool_w"], d_ps = pool_bwd(s["p"], W["pool_w"], s["scale"], n_ex, dmix, d_proj)
            d_pieces = [(d_proj, 0, OD_W, 1), (d_proj, 1, OD_W, 2), (d_proj, 2, OD_W, 0)]
            nb = OD_W // MM_TC
            plan.grad("mx_in_t", l, mm_tn(d_proj, s["h"], 3 * OD_W,
                                          lambda jj: jnp.where(jj < 2 * nb, jj + nb, jj - 2 * nb)))
            sm["pool_scale"], sm["sgu_norm"], sm["sgu_b"] = d_ps[0], d_sn[0], d_sb.sum(-1)
        dx, sm["nm"] = mm_nn(d_pieces, wt("mx_in_t", l), norm_bwd=(s["x1"], W["nm"], dx))
        dx = ffn_back("f1", l, dx, s["x0"], W["n1"], s["gu1"], sm, "n1")
    return loss, dx


def all_gather(srcs):
    return run_job(GatherJob(srcs), "all_gather")


def cast_shards(w):
    L, A, B = w.shape

    def body(w_ref, o_ref):
        o_ref[...] = w_ref[...].astype(BF16)

    return pl.pallas_call(
        body, name="cast_shards", grid=(L,),
        in_specs=[pl.BlockSpec((None, A, B), lambda l: (l, 0, 0))],
        out_specs=pl.BlockSpec((None, A, B), lambda l: (l, 0, 0)),
        out_shape=jax.ShapeDtypeStruct((L, A, B), BF16), compiler_params=_cp("parallel"),
    )(w)


ADAM_TC = 256


def adamw(parts, w, m, v, l, prev=None):
    P, R, C = parts.shape
    tc = _tile(C, ADAM_TC)
    c1, c2 = 1.0 - ADAM_B1 ** ADAM_STEP, 1.0 - ADAM_B2 ** ADAM_STEP

    def body(p_ref, w_ref, m_ref, v_ref, g_ref, d_ref, mo_ref, vo_ref):
        g = p_ref[0].astype(F32)
        for s in range(1, P):
            g = g + p_ref[s].astype(F32)
        m1 = ADAM_B1 * m_ref[...] + (1.0 - ADAM_B1) * g
        v1 = ADAM_B2 * v_ref[...] + (1.0 - ADAM_B2) * (g * g)
        g_ref[...] = g
        mo_ref[...] = m1
        vo_ref[...] = v1
        d_ref[...] = -ADAM_LR * ((m1 / c1) / (jnp.sqrt(v1 / c2) + ADAM_EPS) + ADAM_WD * w_ref[...])

    wspec = pl.BlockSpec((None, R, tc), lambda i: (l, 0, i))
    prev = list(prev) if prev is not None else []
    return pl.pallas_call(
        lambda *refs: body(*refs[:4], *refs[4 + len(prev):]), name="adamw", grid=(C // tc,),
        in_specs=[pl.BlockSpec((P, R, tc), lambda i: (0, 0, i)), wspec, wspec, wspec] + [ANY] * len(prev),
        out_specs=[wspec] * 4, out_shape=[jax.ShapeDtypeStruct(w.shape, F32)] * 4,
        input_output_aliases={4 + i: i for i in range(len(prev))},
        compiler_params=_cp("parallel"),
    )(parts, w, m, v, *prev)


_WEIGHTS = ['ffn1_norm', 'ffn1_w_in', 'ffn1_w_out', 'mix_norm', 'ffn2_norm', 'ffn2_w_in', 'ffn2_w_out', 'ev_w_in',
            'ev_conv_w', 'ev_q_norm', 'ev_k_norm', 'ev_w_out', 'od_w_in', 'od_pool_w', 'od_pool_scale', 'od_sgu_norm',
            'od_sgu_w', 'od_sgu_b', 'od_w_out', 'final_norm']
_BIG = dict(ffn1_w_in=True, ffn1_w_out=False, ffn2_w_in=True, ffn2_w_out=False,
            ev_w_in=True, ev_w_out=False, od_w_in=True, od_w_out=False)
_SMALL_SHARDED = ['ev_conv_w', 'od_pool_scale', 'od_sgu_norm']
_SMALL = [n for n in _WEIGHTS if n not in _BIG]
_PACK_ROWS = 8 * LANES


_KINDS = ("f1_in_t", "f1_out", "mx_in_t", "mx_out", "f2_in_t", "f2_out")
_CARRIER_US = dict(ffn1_fwd=105, ffn2_fwd=105, attn_fwd=115, f2_bwd=135, f1_bwd=135, attn_bwd=205,
                   f2_out_grad=33, f1_out_grad=33, f2_in_grad=61, f1_in_grad=61)
_GATHER_US_PER_ROW, _SCATTER_US_PER_ROW, _SMALL_GATHER_US, _SLACK_US = 0.08, 0.176, 42, 10


def _schedule(L, rows):
    events = []
    for l in range(L):
        events += [("ffn1_fwd", l), ("mixer", l)] + ([("attn_fwd", l)] if l % 2 == 0 else []) + [("ffn2_fwd", l)]
    consumer = {"f1": "ffn1_fwd", "mx": "mixer", "f2": "ffn2_fwd"}
    queue = [(k, l) for l in range(L) for k in _KINDS]
    pos = {t: events.index((consumer[t[0][:2]], t[1])) for t in queue}
    gathers = {"first": [t for t in queue if pos[t] == 0]}
    queue = [t for t in queue if pos[t] > 0]
    carriers = [i for i, e in enumerate(events) if e[0] in _CARRIER_US]
    for i in carriers:
        budget, take = _CARRIER_US[events[i][0]], []
        later = [j for j in carriers if j > i]
        while queue:
            t = queue[0]
            cost = rows(*t) * _GATHER_US_PER_ROW
            forced = not any(j < pos[t] for j in later)
            if not forced and cost > budget:
                break
            take.append(queue.pop(0))
            budget -= cost
        if take:
            gathers[events[i]] = take
    assert not queue
    events = []
    for l in reversed(range(L)):
        events += [("f2_bwd", l), ("f2_in_grad", l), ("f2_out_grad", l), ("mx_out_ready", l)]
        events += [("attn_bwd", l)] if l % 2 == 0 else []
        events += [("mx_in_ready", l), ("f1_bwd", l), ("f1_in_grad", l), ("f1_out_grad", l)]
    made_by = {"f2_out": "f2_out_grad", "f2_in_t": "f2_in_grad", "mx_out": "mx_out_ready", "mx_in_t": "mx_in_ready",
               "f1_out": "f1_out_grad", "f1_in_t": "f1_in_grad"}
    small_carrier = ("f1_in_grad", 0)
    scatters, ready = {}, []
    for e in events:
        if e[0] in _CARRIER_US:
            budget, take = _CARRIER_US[e[0]] - (_SMALL_GATHER_US if e == small_carrier else 0), []
            if e == events[-1]:
                budget = float("inf")
            while True:
                fits = [t for t in ready if rows(*t) * _SCATTER_US_PER_ROW <= budget + _SLACK_US]
                if not fits:
                    break
                t = max(fits, key=lambda u: rows(*u))
                budget -= rows(*t) * _SCATTER_US_PER_ROW
                ready.remove(t)
                take.append(t)
            if take:
                scatters[e] = take
        ready += [(k, e[1]) for k in _KINDS if made_by[k] == e[0]]
    scatters["last"] = ready
    return gathers, scatters, small_carrier


def _pack(arrs):
    flat = jnp.concatenate([a.reshape(-1) for a in arrs])
    pad = (-flat.shape[0]) % _PACK_ROWS
    return jnp.pad(flat, (0, pad)).reshape(-1, LANES)


def _unpack(buf, shapes):
    flat, out, off = buf.reshape(-1), [], 0
    for s in shapes:
        n = math.prod(s)
        out.append(flat[off:off + n].reshape(s))
        off += n
    return out


def _unshard_last(g, lead):
    nd = len(lead)
    return jnp.moveaxis(g, 0, nd).reshape(*lead, -1)


def kernel(x, ffn1_norm, ffn1_w_in, ffn1_w_out, mix_norm, ffn2_norm, ffn2_w_in, ffn2_w_out, ev_w_in, ev_conv_w, ev_q_norm, ev_k_norm, ev_w_out, od_w_in, od_pool_w, od_pool_scale, od_sgu_norm, od_sgu_w, od_sgu_b, od_w_out, final_norm, loss_target, m_ffn1_norm, m_ffn1_w_in, m_ffn1_w_out, m_mix_norm, m_ffn2_norm, m_ffn2_w_in, m_ffn2_w_out, m_ev_w_in, m_ev_conv_w, m_ev_q_norm, m_ev_k_norm, m_ev_w_out, m_od_w_in, m_od_pool_w, m_od_pool_scale, m_od_sgu_norm, m_od_sgu_w, m_od_sgu_b, m_od_w_out, m_final_norm, v_ffn1_norm, v_ffn1_w_in, v_ffn1_w_out, v_mix_norm, v_ffn2_norm, v_ffn2_w_in, v_ffn2_w_out, v_ev_w_in, v_ev_conv_w, v_ev_q_norm, v_ev_k_norm, v_ev_w_out, v_od_w_in, v_od_pool_w, v_od_pool_scale, v_od_sgu_norm, v_od_sgu_w, v_od_sgu_b, v_od_w_out, v_final_norm):
    w = dict(zip(_WEIGHTS, (ffn1_norm, ffn1_w_in, ffn1_w_out, mix_norm, ffn2_norm, ffn2_w_in, ffn2_w_out, ev_w_in, ev_conv_w, ev_q_norm, ev_k_norm, ev_w_out, od_w_in, od_pool_w, od_pool_scale, od_sgu_norm, od_sgu_w, od_sgu_b, od_w_out, final_norm)))
    m = dict(zip(_WEIGHTS, (m_ffn1_norm, m_ffn1_w_in, m_ffn1_w_out, m_mix_norm, m_ffn2_norm, m_ffn2_w_in, m_ffn2_w_out, m_ev_w_in, m_ev_conv_w, m_ev_q_norm, m_ev_k_norm, m_ev_w_out, m_od_w_in, m_od_pool_w, m_od_pool_scale, m_od_sgu_norm, m_od_sgu_w, m_od_sgu_b, m_od_w_out, m_final_norm)))
    v = dict(zip(_WEIGHTS, (v_ffn1_norm, v_ffn1_w_in, v_ffn1_w_out, v_mix_norm, v_ffn2_norm, v_ffn2_w_in, v_ffn2_w_out, v_ev_w_in, v_ev_conv_w, v_ev_q_norm, v_ev_k_norm, v_ev_w_out, v_od_w_in, v_od_pool_w, v_od_pool_scale, v_od_sgu_norm, v_od_sgu_w, v_od_sgu_b, v_od_w_out, v_final_norm)))
    n_ex, seq, D = x.shape
    T = n_ex * seq
    L = ffn1_norm.shape[0]
    me = 4 * lax.axis_index("x") + 2 * lax.axis_index("y") + lax.axis_index("c")

    sh_small = [w[n] for n in _SMALL_SHARDED]
    packed = all_gather([(_pack(sh_small), None)])[0].reshape(N_DEV, -1)
    full_small = {}
    off = 0
    for n, a in zip(_SMALL_SHARDED, sh_small):
        cnt = math.prod(a.shape)
        full_small[n] = _unshard_last(packed[:, off:off + cnt].reshape((N_DEV,) + a.shape), a.shape[:-1])
        off += cnt

    tr = lambda a: jnp.swapaxes(a, 1, 2)
    wmv = {n: tuple(tr(d[n]) if t else d[n] for d in (w, m, v)) for n, t in _BIG.items()}
    shards = {n: cast_shards(wmv[n][0]) for n in _BIG}

    def name_of(kind, l):
        mx = "ev" if l % 2 == 0 else "od"
        return {"f1_in_t": "ffn1_w_in", "f1_out": "ffn1_w_out", "f2_in_t": "ffn2_w_in", "f2_out": "ffn2_w_out",
                "mx_in_t": mx + "_w_in", "mx_out": mx + "_w_out"}[kind], (l // 2 if kind.startswith("mx") else l)

    def shard(kind, l):
        name, idx = name_of(kind, l)
        return shards[name], idx

    g_shapes = {}

    def pack_small(small, d_final):
        ev = [sm for l, sm in enumerate(small) if l % 2 == 0]
        od = [sm for l, sm in enumerate(small) if l % 2 == 1]
        st = lambda sms, k: jnp.stack([sm[k] for sm in sms])
        g_full = dict(ffn1_norm=st(small, "n1")[:, 0], mix_norm=st(small, "nm")[:, 0], ffn2_norm=st(small, "n2")[:, 0],
                      ev_conv_w=st(ev, "conv_w"), ev_q_norm=st(ev, "q_norm"), ev_k_norm=st(ev, "k_norm"),
                      od_pool_w=st(od, "pool_w"), od_pool_scale=st(od, "pool_scale"), od_sgu_norm=st(od, "sgu_norm"),
                      od_sgu_w=st(od, "sgu_w"), od_sgu_b=st(od, "sgu_b"), final_norm=d_final[0])
        g_shapes.update({n: g_full[n].shape for n in _SMALL})
        return _pack([g_full[n] for n in _SMALL])

    gathers, scatters, small_carrier = _schedule(L, lambda kind, l: shards[name_of(kind, l)[0]].shape[1])
    plan = Plan(shard, gathers, scatters, small_carrier, pack_small)
    layers = []
    for l in range(L):
        j = l // 2
        W = dict(n1=ffn1_norm[l][None], nm=mix_norm[l][None], n2=ffn2_norm[l][None])
        if l % 2 == 0:
            W.update(conv_w=full_small["ev_conv_w"][j], q_norm=ev_q_norm[j], k_norm=ev_k_norm[j])
        else:
            W.update(pool_w=od_pool_w[j], pool_scale=full_small["od_pool_scale"][j], sgu_norm=full_small["od_sgu_norm"][j],
                     sgu_w=od_sgu_w[j], sgu_b=od_sgu_b[j])
        layers.append(W)

    plan.alone("first", "gather_first")
    loss, dx = _local_step(x.reshape(T, D), loss_target.reshape(T, D), layers, final_norm[None], n_ex, plan)
    plan.alone("last", "scatter_last")

    out = {n: None for n in _BIG}
    for (kind, l), parts in plan.parts.items():
        name, idx = name_of(kind, l)
        out[name] = adamw(parts, *wmv[name], idx, prev=out[name])
    out = {n: [tr(a) if _BIG[n] else a for a in res] for n, res in out.items()}

    g8 = plan.small_parts.reshape(N_DEV, -1)
    cols, off = [], 0
    for n in _SMALL:
        cnt = math.prod(g_shapes[n])
        g = g8[:, off:off + cnt].reshape((N_DEV,) + g_shapes[n])
        off += cnt
        if n in _SMALL_SHARDED:
            width = w[n].shape[-1]
            g = lax.dynamic_slice_in_dim(g, me * width, width, axis=g.ndim - 1)
        cols.append(g.reshape(N_DEV, -1))
    g8 = jnp.concatenate(cols, axis=1)
    g8 = jnp.pad(g8, ((0, 0), (0, (-g8.shape[1]) % _PACK_ROWS))).reshape(N_DEV, -1, LANES)
    pk = lambda d: _pack([d[n] for n in _SMALL])[None]
    small_out = adamw(g8, pk(w), pk(m), pk(v), 0)
    shapes = [w[n].shape for n in _SMALL]
    for i in range(4):
        for n, a in zip(_SMALL, _unpack(small_out[i], shapes)):
            out.setdefault(n, [None] * 4)[i] = a

    total = lax.psum(loss[0, 0], ("x", "y", "c"))
    return (total, dx.reshape(n_ex, seq, D), *[out[n][0] for n in _WEIGHTS], *[out[n][1] for n in _WEIGHTS],
            *[out[n][2] for n in _WEIGHTS], *[out[n][3] for n in _WEIGHTS])
```

```python
import functools
import math

import jax
import jax.numpy as jnp
from jax import lax
from jax.experimental import pallas as pl
from jax.experimental.pallas import tpu as pltpu

F32, BF16 = jnp.float32, jnp.bfloat16
EPS = 1e-6
N_DEV = 8
V7X_VMEM_BYTES = 64 * 1024 * 1024
VMEM_LIMIT = V7X_VMEM_BYTES - 8 * 1024 * 1024
LANES = 128
HEAD_DIM = 64
N_Q_HEADS = 8
N_KV_HEADS = 2
Q_PER_KV = N_Q_HEADS // N_KV_HEADS
GRID_W = 64
ROPE_THETA = 10000.0
POOL_RADII = (1, 2, 4, 8)
SGU_CHUNK = 128
GROUP = 128
ADAM_LR, ADAM_B1, ADAM_B2, ADAM_EPS, ADAM_WD, ADAM_STEP = 0.001, 0.9, 0.999, 1e-08, 0.01, 10
MESH_ID = pl.DeviceIdType.MESH


def _cp(*sem):
    return pltpu.CompilerParams(dimension_semantics=sem, vmem_limit_bytes=VMEM_LIMIT)


def _dot(a, b, ca, cb):
    return lax.dot_general(a, b, (((ca,), (cb,)), ((), ())), preferred_element_type=F32)


def _nn(a, b):
    return _dot(a, b, 1, 0)


def _nt(a, b):
    return _dot(a, b, 1, 1)


def _tn(a, b):
    return _dot(a, b, 0, 0)


def _split_mm(x, m):
    hi = x.astype(BF16)
    lo = (x - hi.astype(F32)).astype(BF16)
    return _nn(hi, m) + _nn(lo, m)


def _tile(n, pref):
    t = min(n, pref)
    assert n % t == 0, (n, pref)
    return t


ANY = pl.BlockSpec(memory_space=pl.ANY)
OTHER_CHIPS = (2, 4, 6)
JOB_MIDDLE = 0.55


def _my_place():
    x, y, c = lax.axis_index("x"), lax.axis_index("y"), lax.axis_index("c")
    return x, y, c, 4 * x + 2 * y + c


def _peer(x, y, c, k):
    px = 1 - x if k & 4 else x
    py = 1 - y if k & 2 else y
    pc = 1 - c if k & 1 else c
    return (px, py, pc), 4 * px + 2 * py + pc


def _remote(src, dst, send_sems, recv_sems, i, peer):
    return pltpu.make_async_remote_copy(src_ref=src, dst_ref=dst, send_sem=send_sems.at[i], recv_sem=recv_sems.at[i],
                                        device_id=peer, device_id_type=MESH_ID)


class GatherJob:
    def __init__(self, srcs):
        self.srcs = srcs
        self.args = [a for a, _ in srcs]
        self.dims = [a.shape[-2:] for a, _ in srcs]
        n = self.n_in = self.n_out = len(srcs)
        self.out_shape = [jax.ShapeDtypeStruct((N_DEV * r, cc), a.dtype) for (a, _), (r, cc) in zip(srcs, self.dims)]
        self.scratch = [pltpu.SemaphoreType.DMA((N_DEV * n,)), pltpu.SemaphoreType.DMA((N_DEV * n,)),
                        pltpu.SemaphoreType.DMA((n,))]

    def _rows(self, outs, t, idx):
        r = self.dims[t][0]
        return outs[t].at[pl.ds(pl.multiple_of(idx * r, 8), r), :]

    def _local(self, ins, outs, loc, t, me):
        src = ins[t] if self.srcs[t][1] is None else ins[t].at[self.srcs[t][1]]
        return src, pltpu.make_async_copy(src, self._rows(outs, t, me), loc.at[t])

    def start(self, ins, outs, sems):
        send, recv, loc = sems
        x, y, c, me = _my_place()
        for t in range(self.n_in):
            src, local = self._local(ins, outs, loc, t, me)
            local.start()
            for k in (2, 4, 1):
                _remote(src, self._rows(outs, t, me), send, recv, N_DEV * t + k, _peer(x, y, c, k)[0]).start()

    def _copy(self, outs, sems, t, origin, i, to):
        x, y, c, _ = _my_place()
        blk = self._rows(outs, t, _peer(x, y, c, origin)[1])
        return _remote(blk, blk, sems[0], sems[1], N_DEV * t + i, _peer(x, y, c, to)[0])

    def middle(self, ins, outs, sems):
        c = _my_place()[2]

        def relay(t, got, to):
            self._copy(outs, sems, t, got, got, got).wait_recv()
            self._copy(outs, sems, t, got, 6, to).start()
            self._copy(outs, sems, t, to, to, to).wait_recv()

        for t in range(self.n_in):
            pl.when(c == 1)(functools.partial(relay, t, 2, 4))
            pl.when(c == 0)(functools.partial(relay, t, 4, 2))
            for k in (2, 4):
                self._copy(outs, sems, t, k, k + 1, 1).start()

    def finish(self, ins, outs, sems):
        send, recv, loc = sems
        x, y, c, me = _my_place()
        for t in range(self.n_in):
            self._copy(outs, sems, t, 6, 6, 6).wait_recv()
            self._copy(outs, sems, t, 6, 7, 1).start()
        for t in range(self.n_in):
            for k in range(1, N_DEV):
                peer, pidx = _peer(x, y, c, k)
                blk = self._rows(outs, t, pidx)
                if k % 2 == 1:
                    _remote(blk, blk, send, recv, N_DEV * t + k, peer).wait_recv()
                _remote(blk, blk, send, recv, N_DEV * t + k, peer).wait_send()
            self._local(ins, outs, loc, t, me)[1].wait()


class ScatterJob:
    def __init__(self, grads):
        self.args = list(grads)
        self.dims = [(g.shape[0] // N_DEV, g.shape[1]) for g in grads]
        n = self.n_in = self.n_out = len(grads)
        self.out_shape = [jax.ShapeDtypeStruct((N_DEV, r, cc), g.dtype) for g, (r, cc) in zip(grads, self.dims)]
        self.scratch = [pltpu.SemaphoreType.DMA((N_DEV * n,)), pltpu.SemaphoreType.DMA((N_DEV * n,)),
                        pltpu.SemaphoreType.DMA((n,))]

    def _rows(self, ins, t, idx):
        r = self.dims[t][0]
        return ins[t].at[pl.ds(pl.multiple_of(idx * r, 8), r), :]

    def start(self, ins, outs, sems):
        send, recv, loc = sems
        x, y, c, me = _my_place()
        for t in range(self.n_in):
            pltpu.make_async_copy(self._rows(ins, t, me), outs[t].at[me], loc.at[t]).start()
            for k in OTHER_CHIPS + (1, 3, 5, 7):
                peer, pidx = _peer(x, y, c, k)
                _remote(self._rows(ins, t, pidx), outs[t].at[me], send, recv, N_DEV * t + k, peer).start()

    def middle(self, ins, outs, sems):
        pass

    def finish(self, ins, outs, sems):
        send, recv, loc = sems
        x, y, c, me = _my_place()
        for t in range(self.n_in):
            for k in range(1, N_DEV):
                peer, pidx = _peer(x, y, c, k)
                cp = _remote(self._rows(ins, t, pidx), outs[t].at[pidx], send, recv, N_DEV * t + k, peer)
                cp.wait_recv()
                cp.wait_send()
            pltpu.make_async_copy(self._rows(ins, t, me), outs[t].at[me], loc.at[t]).wait()


def _call(body, name, grid, in_specs, args, out_specs, out_shape, scratch=(), sem=(), job=None, aliases=None):
    in_specs, out_specs, out_shape, scratch = list(in_specs), list(out_specs), list(out_shape), list(scratch)
    n_in, n_out, n_scr = len(args), len(out_shape), len(scratch)
    if job is None:
        res = pl.pallas_call(body, name=name, grid=grid, in_specs=in_specs, out_specs=out_specs, out_shape=out_shape,
                             scratch_shapes=scratch, input_output_aliases=aliases or {}, compiler_params=_cp(*sem))(*args)
        return res, None
    o0 = n_in + job.n_in
    s0 = o0 + n_out + job.n_out

    def carrier(*refs):
        jin, jout, jsem = refs[n_in:o0], refs[o0 + n_out:s0], refs[s0 + n_scr:]
        ids = [pl.program_id(a) for a in range(len(grid))]
        def at(step):
            idx = []
            for g in reversed(grid):
                idx.append(step % g)
                step //= g
            return functools.reduce(jnp.logical_and, [i == j for i, j in zip(ids, reversed(idx))])

        steps = math.prod(grid)
        if grid:
            pl.when(at(0))(lambda: job.start(jin, jout, jsem))
            pl.when(at(int(steps * JOB_MIDDLE)))(lambda: job.middle(jin, jout, jsem))
        else:
            job.start(jin, jout, jsem)
            job.middle(jin, jout, jsem)
        body(*refs[:n_in], *refs[o0:o0 + n_out], *refs[s0:s0 + n_scr])
        if grid:
            pl.when(at(steps - 1))(lambda: job.finish(jin, jout, jsem))
        else:
            job.finish(jin, jout, jsem)

    res = pl.pallas_call(
        carrier, name=name + "_comm", grid=grid, in_specs=in_specs + [ANY] * job.n_in,
        out_specs=out_specs + [ANY] * job.n_out, out_shape=out_shape + job.out_shape,
        scratch_shapes=scratch + job.scratch, input_output_aliases=aliases or {},
        compiler_params=_cp(*(["arbitrary"] * len(grid))))(*args, *job.args)
    return res[:n_out], res[n_out:]


def run_job(job, name):
    return _call(lambda: None, name, (), [], [], [], [], job=job)[1]


@jax.custom_vjp
def bmm(x, w):
    return _nn(x.astype(BF16), w.astype(BF16))


def _bmm_fwd(x, w):
    return bmm(x, w), (x, w)


def _bmm_bwd(res, g):
    x, w = res
    gb = g.astype(BF16)
    return _nt(gb, w.astype(BF16)), _tn(x.astype(BF16), gb)


bmm.defvjp(_bmm_fwd, _bmm_bwd)


def _shift_raw(x, d):
    n = x.shape[0]
    r = pltpu.roll(x, d % n, axis=0)
    row = lax.broadcasted_iota(jnp.int32, x.shape, 0)
    keep = (row >= d) if d > 0 else (row < n + d)
    return jnp.where(keep, r, 0.0)


def shift_rows(x, d):
    @jax.custom_vjp
    def f(v):
        return _shift_raw(v, d)

    f.defvjp(lambda v: (_shift_raw(v, d), None), lambda _, g: (_shift_raw(g, -d),))
    return f(x)


def _swap_raw(x):
    n = x.shape[1]
    nxt = pltpu.roll(x, n - 1, axis=1)
    prv = pltpu.roll(x, 1, axis=1)
    lane = lax.broadcasted_iota(jnp.int32, x.shape, 1)
    return jnp.where(lane % 2 == 0, nxt, prv)


@jax.custom_vjp
def swap_pairs(x):
    return _swap_raw(x)


swap_pairs.defvjp(lambda x: (_swap_raw(x), None), lambda _, g: (_swap_raw(g),))


@jax.custom_vjp
def group_mean(x, bd):
    return _split_mm(x, bd)


group_mean.defvjp(lambda x, bd: (_split_mm(x, bd), bd), lambda bd, g: (_split_mm(g, bd), jnp.zeros_like(bd)))


def _rope_norm(x, gain, cos, sgn, bd, scale):
    xn = x * lax.rsqrt(group_mean(x * x, bd) + EPS) * gain
    return (xn * cos + swap_pairs(xn) * sgn) * scale


def _conv_gate(gb, gc, hc, w):
    z = gc * hc
    c = shift_rows(z, 1) * w[0:1] + z * w[1:2] + shift_rows(z, -1) * w[2:3]
    return gb * c


def _window_sum(p, r):
    b = f = p
    k = 1
    while k < r:
        b = b + shift_rows(b, k)
        f = f + shift_rows(f, -k)
        k *= 2
    return b + f - p + shift_rows(p, r) + shift_rows(p, -r)


def _pool_mix(p, pool_w, scale):
    n = p.shape[0]
    t = lax.broadcasted_iota(jnp.int32, (n, 1), 0)
    outs = []
    for gi, r in enumerate(POOL_RADII):
        pg = p[:, gi * GROUP:(gi + 1) * GROUP]
        cnt = (jnp.minimum(t + r, n - 1) - jnp.maximum(t - r, 0) + 1).astype(F32)
        pooled = _window_sum(pg, r) / cnt - pg
        outs.append(bmm(pooled, pool_w[gi]))
    return jnp.concatenate(outs, axis=1) * scale


def _sgu(u, v, norm_g, w_s, b_full):
    ug = jax.nn.gelu(u)
    vg = jax.nn.gelu(v)
    vn = vg * lax.rsqrt(jnp.mean(vg * vg, axis=-1, keepdims=True) + EPS) * norm_g
    cols = []
    for g in range(w_s.shape[0]):
        rows = []
        for n in range(u.shape[0] // SGU_CHUNK):
            blk = vn[n * SGU_CHUNK:(n + 1) * SGU_CHUNK, g * GROUP:(g + 1) * GROUP]
            rows.append(bmm(w_s[g], blk) + b_full[g])
        cols.append(jnp.concatenate(rows, axis=0))
    return ug * jnp.concatenate(cols, axis=1)


def _rms_bwd_math(xv, gain, dy, dres):
    r = lax.rsqrt(jnp.mean(xv * xv, axis=-1, keepdims=True) + EPS)
    xh = xv * r
    dxh = dy * gain
    dx = dres + r * (dxh - xh * jnp.mean(dxh * xh, axis=-1, keepdims=True))
    return dx, jnp.sum(dy * xh, axis=0, keepdims=True)


FFN_TN = 256


def ffn_fwd(x, gain, wt_in, w_out, job=None):
    T, D = x.shape
    F = w_out.shape[0]
    tm, tn = _tile(T, 1024), FFN_TN
    nc = F // tn

    def body(x_ref, gn_ref, wg_ref, wu_ref, wo_ref, y_ref, gu_ref, xn_s, acc_s):
        c = pl.program_id(1)

        @pl.when(c == 0)
        def _():
            xv = x_ref[...]
            r = lax.rsqrt(jnp.mean(xv * xv, axis=-1, keepdims=True) + EPS)
            xn_s[...] = (xv * r * gn_ref[...]).astype(BF16)
            acc_s[...] = jnp.zeros_like(acc_s)

        xn = xn_s[...]
        g = _nt(xn, wg_ref[...])
        u = _nt(xn, wu_ref[...])
        gu_ref[:, :tn] = g.astype(BF16)
        gu_ref[:, tn:] = u.astype(BF16)
        a = (g * jax.nn.sigmoid(g) * u).astype(BF16)
        acc_s[...] += _nn(a, wo_ref[...])

        @pl.when(c == nc - 1)
        def _():
            y_ref[...] = x_ref[...] + 0.5 * acc_s[...]

    row = pl.BlockSpec((tm, D), lambda i, c: (i, 0))
    return _call(
        body, "ffn_fwd", (T // tm, nc),
        [row, pl.BlockSpec((1, D), lambda i, c: (0, 0)),
         pl.BlockSpec((tn, D), lambda i, c: (c, 0)),
         pl.BlockSpec((tn, D), lambda i, c: (c + nc, 0)),
         pl.BlockSpec((tn, D), lambda i, c: (c, 0))],
        [x, gain, wt_in, wt_in, w_out],
        [row, pl.BlockSpec((tm, 2 * tn), lambda i, c: (i, c)), row],
        [jax.ShapeDtypeStruct((T, D), F32), jax.ShapeDtypeStruct((T, 2 * F), BF16), jax.ShapeDtypeStruct((T, D), BF16)],
        [pltpu.VMEM((tm, D), F32)], ("parallel", "arbitrary"), job)


def ffn_bwd_x(dout, x, gain, gu, wt_in, w_out, job=None):
    T, D = x.shape
    F = w_out.shape[0]
    tm, tn = _tile(T, 1024), FFN_TN
    nc = F // tn

    def body(do_ref, x_ref, gn_ref, gu_ref, wg_ref, wu_ref, wo_ref,
             dx_ref, dgn_ref, a_ref, dgu_ref, dob_ref, acc_s):
        i, c = pl.program_id(0), pl.program_id(1)

        @pl.when(c == 0)
        def _():
            dob_ref[...] = (0.5 * do_ref[...]).astype(BF16)
            acc_s[...] = jnp.zeros_like(acc_s)

        da = jnp.concatenate([_nt(dob_ref[:tm // 2, :], wo_ref[...]), _nt(dob_ref[tm // 2:, :], wo_ref[...])], axis=0)
        g = gu_ref[:, :tn].astype(F32)
        u = gu_ref[:, tn:].astype(F32)
        sig = jax.nn.sigmoid(g)
        sl = g * sig
        a_ref[...] = (sl * u).astype(BF16)
        dg = (da * u * (sig * (1.0 + g * (1.0 - sig)))).astype(BF16)
        du = (da * sl).astype(BF16)
        dgu_ref[:, :tn] = dg
        dgu_ref[:, tn:] = du
        acc_s[...] += _nn(dg, wg_ref[...]) + _nn(du, wu_ref[...])

        @pl.when(c == nc - 1)
        def _():
            dx, dgn = _rms_bwd_math(x_ref[...], gn_ref[...], acc_s[...], do_ref[...])
            dx_ref[...] = dx

            @pl.when(i == 0)
            def _():
                dgn_ref[...] = jnp.zeros_like(dgn_ref)

            dgn_ref[...] += dgn

    row = pl.BlockSpec((tm, D), lambda i, c: (i, 0))
    return _call(
        body, "ffn_bwd_x", (T // tm, nc),
        [row, row, pl.BlockSpec((1, D), lambda i, c: (0, 0)),
         pl.BlockSpec((tm, 2 * tn), lambda i, c: (i, c)),
         pl.BlockSpec((tn, D), lambda i, c: (c, 0)),
         pl.BlockSpec((tn, D), lambda i, c: (c + nc, 0)),
         pl.BlockSpec((tn, D), lambda i, c: (c, 0))],
        [dout, x, gain, gu, wt_in, wt_in, w_out],
        [row, pl.BlockSpec((1, D), lambda i, c: (0, 0)),
         pl.BlockSpec((tm, tn), lambda i, c: (i, c)),
         pl.BlockSpec((tm, 2 * tn), lambda i, c: (i, c)), row],
        [jax.ShapeDtypeStruct((T, D), F32), jax.ShapeDtypeStruct((1, D), F32),
         jax.ShapeDtypeStruct((T, F), BF16), jax.ShapeDtypeStruct((T, 2 * F), BF16),
         jax.ShapeDtypeStruct((T, D), BF16)],
        [pltpu.VMEM((tm, D), F32)], ("arbitrary", "arbitrary"), job)


MM_TM = 512
MM_TC = 256


def mm_nt(a, wt, pieces, out_dtype, emit_a_bf16=False, norm_gain=None):
    T, K = a.shape
    tm = _tile(T, MM_TM)
    npc = len(pieces)
    n_lead = 1 if norm_gain is None else 2

    def body(*refs):
        a_ref, w_refs, o_refs = refs[0], refs[n_lead:n_lead + npc], refs[n_lead + npc:]
        av = a_ref[...]
        if norm_gain is not None:
            av = av * lax.rsqrt(jnp.mean(av * av, axis=-1, keepdims=True) + EPS) * refs[1][...]
        ab = av.astype(BF16)
        for w_ref, o_ref in zip(w_refs, o_refs[:npc]):
            o_ref[...] = _nt(ab, w_ref[...]).astype(o_ref.dtype)
        if emit_a_bf16:
            o_refs[npc][...] = ab

    in_specs = [pl.BlockSpec((tm, K), lambda i: (i, 0))]
    if norm_gain is not None:
        in_specs.append(pl.BlockSpec((1, K), lambda i: (0, 0)))
    out_specs, out_shape = [], []
    for r0, n in pieces:
        assert r0 % n == 0
        in_specs.append(pl.BlockSpec((n, K), functools.partial(lambda i, b: (b, 0), b=r0 // n)))
        out_specs.append(pl.BlockSpec((tm, n), lambda i: (i, 0)))
        out_shape.append(jax.ShapeDtypeStruct((T, n), out_dtype))
    if emit_a_bf16:
        out_specs.append(pl.BlockSpec((tm, K), lambda i: (i, 0)))
        out_shape.append(jax.ShapeDtypeStruct((T, K), BF16))
    return pl.pallas_call(
        body, name="mm_nt", grid=(T // tm,), in_specs=in_specs, out_specs=out_specs, out_shape=out_shape,
        compiler_params=_cp("parallel"),
    )(a, *([] if norm_gain is None else [norm_gain]), *([wt] * npc))


def mm_nn(pieces, w, residual=None, norm_bwd=None):
    T = pieces[0][0].shape[0]
    N = w.shape[1]
    tm = _tile(T, MM_TM)
    na = len(pieces)

    def body(*refs):
        a_refs, w_refs = refs[:na], refs[na:2 * na]
        acc = refs[2 * na][...] if residual is not None else None
        for a_ref, w_ref in zip(a_refs, w_refs):
            t = _nn(a_ref[...].astype(BF16), w_ref[...])
            acc = t if acc is None else acc + t
        if norm_bwd is None:
            refs[-1][...] = acc
            return
        x_ref, g_ref, dr_ref, dx_ref, dg_ref = refs[-5:]
        dx, dg = _rms_bwd_math(x_ref[...], g_ref[...], acc, dr_ref[...])
        dx_ref[...] = dx

        @pl.when(pl.program_id(0) == 0)
        def _():
            dg_ref[...] = jnp.zeros_like(dg_ref)

        dg_ref[...] += dg

    in_specs, w_specs = [], []
    for a, cb, k, rb in pieces:
        in_specs.append(pl.BlockSpec((tm, k), functools.partial(lambda i, b: (i, b), b=cb)))
        w_specs.append(pl.BlockSpec((k, N), functools.partial(lambda i, b: (b, 0), b=rb)))
    assert sum(k for _, _, k, _ in pieces) == w.shape[0]
    args = [a for a, _, _, _ in pieces] + [w] * na
    in_specs = in_specs + w_specs
    row = pl.BlockSpec((tm, N), lambda i: (i, 0))
    if residual is not None:
        in_specs.append(row)
        args.append(residual)
    if norm_bwd is None:
        return pl.pallas_call(
            body, name="mm_nn", grid=(T // tm,), in_specs=in_specs, out_specs=row,
            out_shape=jax.ShapeDtypeStruct((T, N), F32), compiler_params=_cp("parallel"),
        )(*args)
    vec = pl.BlockSpec((1, N), lambda i: (0, 0))
    return pl.pallas_call(
        body, name="mm_nn_norm_bwd", grid=(T // tm,), in_specs=in_specs + [row, vec, row], out_specs=[row, vec],
        out_shape=[jax.ShapeDtypeStruct((T, N), F32), jax.ShapeDtypeStruct((1, N), F32)],
        compiler_params=_cp("arbitrary"),
    )(*args, *norm_bwd)


def mm_tn(a, b, n_rows, row_block, prev=None, grid=None, col_block=None, job=None):
    T, M = a.shape
    N = b.shape[1]
    tc = MM_TC
    assert M % tc == 0 and n_rows % tc == 0
    if grid is None:
        grid, col_block = (M // tc,), (lambda j: j)

    def body(*refs):
        a_ref, b_ref, o_ref = refs[0], refs[1], refs[-1]
        o_ref[...] = _tn(a_ref[...], b_ref[...]).astype(BF16)

    in_specs = [pl.BlockSpec((T, tc), lambda *g: (0, col_block(*g))), pl.BlockSpec((T, N), lambda *g: (0, 0))]
    args = [a, b]
    aliases = {}
    if prev is not None:
        in_specs.append(pl.BlockSpec(memory_space=pl.ANY))
        args.append(prev)
        aliases = {2: 0}
    res, jres = _call(body, "mm_tn", grid, in_specs, args, [pl.BlockSpec((tc, N), lambda *g: (row_block(*g), 0))],
                      [jax.ShapeDtypeStruct((n_rows, N), BF16)], (), ["parallel"] * len(grid), job, aliases)
    return res[0] if job is None else (res[0], jres)


def conv_fwd(proj_a, conv_w, n_ex):
    T, C3 = proj_a.shape
    C = C3 // 3
    S = T // n_ex

    def body(gb_ref, gc_ref, hc_ref, w_ref, o_ref):
        o_ref[...] = _conv_gate(gb_ref[...], gc_ref[...], hc_ref[...], w_ref[...]).astype(BF16)

    col = lambda k: pl.BlockSpec((S, C), functools.partial(lambda b, kk: (b, kk), kk=k))
    return pl.pallas_call(
        body, name="conv_fwd", grid=(n_ex,),
        in_specs=[col(0), col(1), col(2), pl.BlockSpec((3, C), lambda b: (0, 0))],
        out_specs=pl.BlockSpec((S, C), lambda b: (b, 0)),
        out_shape=jax.ShapeDtypeStruct((T, 2 * C), BF16), compiler_params=_cp("parallel"),
    )(proj_a, proj_a, proj_a, conv_w)


def conv_bwd(proj_a, conv_w, n_ex, dy, total_cols):
    T, C3 = proj_a.shape
    C = C3 // 3
    S = T // n_ex

    def body(gb_ref, gc_ref, hc_ref, w_ref, dy_ref, dp_ref, dw_ref):
        _, vjp = jax.vjp(_conv_gate, gb_ref[...], gc_ref[...], hc_ref[...], w_ref[...])
        dgb, dgc, dhc, dw = vjp(dy_ref[...].astype(F32))
        dp_ref[:, 0:C] = dgb.astype(BF16)
        dp_ref[:, C:2 * C] = dgc.astype(BF16)
        dp_ref[:, 2 * C:] = dhc.astype(BF16)

        @pl.when(pl.program_id(0) == 0)
        def _():
            dw_ref[...] = jnp.zeros_like(dw_ref)

        dw_ref[...] += dw

    col = lambda k: pl.BlockSpec((S, C), functools.partial(lambda b, kk: (b, kk), kk=k))
    return pl.pallas_call(
        body, name="conv_bwd", grid=(n_ex,),
        in_specs=[col(0), col(1), col(2), pl.BlockSpec((3, C), lambda b: (0, 0)),
                  pl.BlockSpec((S, C), lambda b: (b, 0))],
        out_specs=[pl.BlockSpec((S, C3), lambda b: (b, 0)), pl.BlockSpec((3, C), lambda b: (0, 0))],
        out_shape=[jax.ShapeDtypeStruct((T, total_cols), BF16), jax.ShapeDtypeStruct((3, C), F32)],
        compiler_params=_cp("arbitrary"),
    )(proj_a, proj_a, proj_a, conv_w, dy)


QW = N_Q_HEADS * HEAD_DIM
KW = N_KV_HEADS * HEAD_DIM
QP = N_Q_HEADS * LANES
KP = N_KV_HEADS * LANES


def _attn_consts(seq):
    rows = seq // GRID_W
    r_idx, c_idx = jnp.meshgrid(jnp.arange(rows), jnp.arange(GRID_W), indexing='ij')
    r_idx = r_idx.reshape(-1).astype(F32)
    c_idx = c_idx.reshape(-1).astype(F32)
    n_freq = HEAD_DIM // 4
    inv = ROPE_THETA ** (-jnp.arange(n_freq, dtype=F32) / n_freq)
    ang = jnp.concatenate([r_idx[:, None] * inv, c_idx[:, None] * inv], axis=-1)
    cos = jnp.repeat(jnp.cos(ang), 2, axis=1)
    sin = jnp.repeat(jnp.sin(ang), 2, axis=1)
    sgn = sin * jnp.tile(jnp.array([-1.0, 1.0], F32), HEAD_DIM // 2)
    cos = jnp.tile(cos, (1, N_Q_HEADS))
    sgn = jnp.tile(sgn, (1, N_Q_HEADS))
    lane = jnp.arange(QW)
    bd = jnp.where(lane[:, None] // HEAD_DIM == lane[None, :] // HEAD_DIM, 1.0 / HEAD_DIM, 0.0).astype(BF16)
    dst = (lane // HEAD_DIM) * LANES + lane % HEAD_DIM
    spread = (dst[:, None] == jnp.arange(QP)[None, :]).astype(BF16)
    return dict(cos=cos, sgn=sgn, bd=bd, spread=spread, gather=spread.T)


def qkv_prep_fwd(proj_b, qg, kg, cst, n_ex):
    T = proj_b.shape[0]
    S = T // n_ex
    tm = _tile(S, 512)
    nb = S // tm

    def body(p_ref, qg_ref, kg_ref, cos_ref, sgn_ref, bd_ref, sp_ref, q_ref, k_ref, v_ref):
        pv = p_ref[...]
        cos, sgn, bd, sp = cos_ref[...], sgn_ref[...], bd_ref[...], sp_ref[...]
        qr = _rope_norm(pv[:, :QW], qg_ref[...], cos, sgn, bd, HEAD_DIM ** -0.5)
        kr = _rope_norm(pv[:, QW:QW + KW], kg_ref[...], cos[:, :KW], sgn[:, :KW], bd[:KW, :KW], 1.0)
        q_ref[...] = _nn(qr.astype(BF16), sp).astype(BF16)
        k_ref[...] = _nn(kr.astype(BF16), sp[:KW, :KP]).astype(BF16)
        v_ref[...] = _nn(pv[:, QW + KW:].astype(BF16), sp[:KW, :KP]).astype(BF16)

    full = lambda a: pl.BlockSpec(a.shape, lambda i: (0,) * a.ndim)
    tab = pl.BlockSpec((tm, QW), lambda i: (i % nb, 0))
    return pl.pallas_call(
        body, name="qkv_prep_fwd", grid=(T // tm,),
        in_specs=[pl.BlockSpec((tm, QW + 2 * KW), lambda i: (i, 0)), full(qg), full(kg), tab, tab,
                  full(cst["bd"]), full(cst["spread"])],
        out_specs=[pl.BlockSpec((tm, QP), lambda i: (i, 0)), pl.BlockSpec((tm, KP), lambda i: (i, 0)),
                   pl.BlockSpec((tm, KP), lambda i: (i, 0))],
        out_shape=[jax.ShapeDtypeStruct((T, QP), BF16), jax.ShapeDtypeStruct((T, KP), BF16),
                   jax.ShapeDtypeStruct((T, KP), BF16)],
        compiler_params=_cp("parallel"),
    )(proj_b, qg, kg, cst["cos"], cst["sgn"], cst["bd"], cst["spread"])


def qkv_prep_bwd(proj_b, qg, kg, cst, n_ex, dq, dk_pad, dv_pad, d_proj):
    T = proj_b.shape[0]
    S = T // n_ex
    tm = _tile(S, 512)
    nb = S // tm

    def body(p_ref, qg_ref, kg_ref, cos_ref, sgn_ref, bd_ref, ga_ref, dq_ref, dk_ref, dv_ref, _kept,
             dp_ref, dqg_ref, dkg_ref):
        pv = p_ref[...]
        cos, sgn, bd, ga = cos_ref[...], sgn_ref[...], bd_ref[...], ga_ref[...]
        fq = lambda q, g: _rope_norm(q, g, cos, sgn, bd, HEAD_DIM ** -0.5)
        fk = lambda k, g: _rope_norm(k, g, cos[:, :KW], sgn[:, :KW], bd[:KW, :KW], 1.0)
        _, vq = jax.vjp(fq, pv[:, :QW], qg_ref[...])
        _, vk = jax.vjp(fk, pv[:, QW:QW + KW], kg_ref[...])
        dqp, dqg = vq(dq_ref[...])
        dkp, dkg = vk(_split_mm(dk_ref[...], ga[:KP, :KW]))
        dp_ref[:, :QW] = dqp.astype(BF16)
        dp_ref[:, QW:QW + KW] = dkp.astype(BF16)
        dp_ref[:, QW + KW:] = _split_mm(dv_ref[...], ga[:KP, :KW]).astype(BF16)

        @pl.when(pl.program_id(0) == 0)
        def _():
            dqg_ref[...] = jnp.zeros_like(dqg_ref)
            dkg_ref[...] = jnp.zeros_like(dkg_ref)

        dqg_ref[...] += dqg
        dkg_ref[...] += dkg

    full = lambda a: pl.BlockSpec(a.shape, lambda i: (0,) * a.ndim)
    tab = pl.BlockSpec((tm, QW), lambda i: (i % nb, 0))
    row = lambda n: pl.BlockSpec((tm, n), lambda i: (i, 0))
    wb = QW + 2 * KW
    assert d_proj.shape[1] % wb == 0
    last = d_proj.shape[1] // wb - 1
    return pl.pallas_call(
        body, name="qkv_prep_bwd", grid=(T // tm,),
        in_specs=[row(wb), full(qg), full(kg), tab, tab, full(cst["bd"]), full(cst["gather"]),
                  row(QW), row(KP), row(KP), ANY],
        out_specs=[pl.BlockSpec((tm, wb), lambda i: (i, last)), pl.BlockSpec((1, QW), lambda i: (0, 0)),
                   pl.BlockSpec((1, KW), lambda i: (0, 0))],
        out_shape=[jax.ShapeDtypeStruct(d_proj.shape, BF16), jax.ShapeDtypeStruct((1, QW), F32),
                   jax.ShapeDtypeStruct((1, KW), F32)],
        input_output_aliases={10: 0}, compiler_params=_cp("arbitrary"),
    )(proj_b, qg, kg, cst["cos"], cst["sgn"], cst["bd"], cst["gather"], dq, dk_pad, dv_pad, d_proj)


ATT_TQ = 256
ATT_TQ_FWD = 512


def attn_fwd(qp, kp, vp, gather, n_ex, mix, job=None):
    T = qp.shape[0]
    S = T // n_ex
    tq = _tile(S, ATT_TQ_FWD)
    nq = S // tq

    def body(q_ref, k_ref, v_ref, ga_ref, _kept, o_ref, op_ref, lse_ref):
        lane = lax.broadcasted_iota(jnp.int32, (tq, LANES), 1)
        lse_all = jnp.zeros((tq, LANES), F32)
        for h in range(N_Q_HEADS):
            kv = h // Q_PER_KV
            qh = q_ref[:, h * LANES:(h + 1) * LANES]
            s = _nt(qh, k_ref[:, kv * LANES:(kv + 1) * LANES])
            m = jnp.max(s, axis=-1, keepdims=True)
            p = jnp.exp(s - m)
            lsum = jnp.sum(p, axis=-1, keepdims=True)
            o = _nn(p.astype(BF16), v_ref[:, kv * LANES:(kv + 1) * LANES]) / lsum
            op_ref[:, h * LANES:(h + 1) * LANES] = o.astype(BF16)
            lse_all = jnp.where(lane == h, m + jnp.log(lsum), lse_all)
        lse_ref[...] = lse_all
        o_ref[...] = _nn(op_ref[...], ga_ref[...]).astype(BF16)

    blk = lambda n: pl.BlockSpec((tq, n), lambda b, i: (b * nq + i, 0))
    kvs = pl.BlockSpec((S, KP), lambda b, i: (b, 0))
    return _call(
        body, "attn_fwd", (n_ex, nq),
        [blk(QP), kvs, kvs, pl.BlockSpec(gather.shape, lambda b, i: (0, 0)), ANY], [qp, kp, vp, gather, mix],
        [pl.BlockSpec((tq, QW), lambda b, i: (b * nq + i, 1)), blk(QP), blk(LANES)],
        [jax.ShapeDtypeStruct(mix.shape, BF16), jax.ShapeDtypeStruct((T, QP), BF16),
         jax.ShapeDtypeStruct((T, LANES), F32)], (), ("parallel", "parallel"), job, {4: 0})


def attn_bwd(qp, kp, vp, op, lse, do, cst, n_ex, job=None):
    T = qp.shape[0]
    S = T // n_ex
    tq = _tile(S, ATT_TQ)
    nq = S // tq

    def body(q_ref, k_ref, v_ref, op_ref, lse_ref, do_ref, sp_ref, ga_ref, dq_ref, dk_ref, dv_ref, dqp_s):
        @pl.when(pl.program_id(1) == 0)
        def _():
            dk_ref[...] = jnp.zeros_like(dk_ref)
            dv_ref[...] = jnp.zeros_like(dv_ref)

        lane = lax.broadcasted_iota(jnp.int32, (tq, LANES), 1)
        dop = _nn(do_ref[...], sp_ref[...]).astype(BF16)
        lse_all = lse_ref[...]
        for h in range(N_Q_HEADS):
            kv = h // Q_PER_KV
            hs = slice(h * LANES, (h + 1) * LANES)
            ks = slice(kv * LANES, (kv + 1) * LANES)
            qh, kk, vv = q_ref[:, hs], k_ref[:, ks], v_ref[:, ks]
            doh = dop[:, hs]
            lse_h = jnp.sum(jnp.where(lane == h, lse_all, 0.0), axis=-1, keepdims=True)
            p = jnp.exp(_nt(qh, kk) - lse_h)
            dp = _nt(doh, vv)
            delta = jnp.sum(doh.astype(F32) * op_ref[:, hs].astype(F32), axis=-1, keepdims=True)
            ds = (p * (dp - delta)).astype(BF16)
            dqp_s[:, hs] = _nn(ds, kk)
            dk_ref[:, ks] += _tn(ds, qh)
            dv_ref[:, ks] += _tn(p.astype(BF16), doh)
        dq_ref[...] = _split_mm(dqp_s[...], ga_ref[...])

    blk = lambda n: pl.BlockSpec((tq, n), lambda b, i: (b * nq + i, 0))
    kvs = pl.BlockSpec((S, KP), lambda b, i: (b, 0))
    full = lambda a: pl.BlockSpec(a.shape, lambda b, i: (0, 0))
    return _call(
        body, "attn_bwd", (n_ex, nq),
        [blk(QP), kvs, kvs, blk(QP), blk(LANES), pl.BlockSpec((tq, QW), lambda b, i: (b * nq + i, 1)),
         full(cst["spread"]), full(cst["gather"])],
        [qp, kp, vp, op, lse, do, cst["spread"], cst["gather"]],
        [blk(QW), kvs, kvs],
        [jax.ShapeDtypeStruct((T, QW), F32), jax.ShapeDtypeStruct((T, KP), F32), jax.ShapeDtypeStruct((T, KP), F32)],
        [pltpu.VMEM((tq, QP), F32)], ("arbitrary", "arbitrary"), job)


def pool_fwd(p, pool_w, scale, n_ex):
    T, W = p.shape
    S = T // n_ex

    def body(p_ref, w_ref, s_ref, o_ref):
        o_ref[...] = _pool_mix(p_ref[...], w_ref[...], s_ref[...]).astype(BF16)

    return pl.pallas_call(
        body, name="pool_fwd", grid=(n_ex,),
        in_specs=[pl.BlockSpec((S, W), lambda b: (b, 0)),
                  pl.BlockSpec(pool_w.shape, lambda b: (0, 0, 0)),
                  pl.BlockSpec((1, W), lambda b: (0, 0))],
        out_specs=pl.BlockSpec((S, W), lambda b: (b, 0)),
        out_shape=jax.ShapeDtypeStruct((T, 2 * W), BF16), compiler_params=_cp("parallel"),
    )(p, pool_w, scale)


def pool_bwd(p, pool_w, scale, n_ex, dy, d_proj):
    T, W = p.shape
    S = T // n_ex
    last = d_proj.shape[1] // W - 1

    def body(p_ref, w_ref, s_ref, dy_ref, _kept, dp_ref, dw_ref, ds_ref):
        _, vjp = jax.vjp(_pool_mix, p_ref[...], w_ref[...], s_ref[...])
        dp, dw, ds = vjp(dy_ref[...].astype(F32))
        dp_ref[...] = dp.astype(BF16)

        @pl.when(pl.program_id(0) == 0)
        def _():
            dw_ref[...] = jnp.zeros_like(dw_ref)
            ds_ref[...] = jnp.zeros_like(ds_ref)

        dw_ref[...] += dw
        ds_ref[...] += ds

    wshape = pool_w.shape
    return pl.pallas_call(
        body, name="pool_bwd", grid=(n_ex,),
        in_specs=[pl.BlockSpec((S, W), lambda b: (b, 0)),
                  pl.BlockSpec(wshape, lambda b: (0, 0, 0)),
                  pl.BlockSpec((1, W), lambda b: (0, 0)), pl.BlockSpec((S, W), lambda b: (b, 0)), ANY],
        out_specs=[pl.BlockSpec((S, W), lambda b: (b, last)), pl.BlockSpec(wshape, lambda b: (0, 0, 0)),
                   pl.BlockSpec((1, W), lambda b: (0, 0))],
        out_shape=[jax.ShapeDtypeStruct(d_proj.shape, BF16), jax.ShapeDtypeStruct(wshape, F32),
                   jax.ShapeDtypeStruct((1, W), F32)],
        input_output_aliases={4: 0}, compiler_params=_cp("arbitrary"),
    )(p, pool_w, scale, dy, d_proj)


SGU_TS = 512


def sgu_fwd(u, v, norm_g, w_s, b_full, mix):
    T, W = u.shape
    ts = _tile(T, SGU_TS)

    def body(u_ref, v_ref, g_ref, w_ref, b_ref, _kept, o_ref):
        o_ref[...] = _sgu(u_ref[...], v_ref[...], g_ref[...], w_ref[...], b_ref[...]).astype(BF16)

    row = pl.BlockSpec((ts, W), lambda i: (i, 0))
    wsp = pl.BlockSpec(w_s.shape, lambda i: (0, 0, 0))
    return pl.pallas_call(
        body, name="sgu_fwd", grid=(T // ts,),
        in_specs=[row, row, pl.BlockSpec((1, W), lambda i: (0, 0)), wsp, wsp, ANY],
        out_specs=pl.BlockSpec((ts, W), lambda i: (i, 1)), out_shape=jax.ShapeDtypeStruct(mix.shape, BF16),
        input_output_aliases={5: 0}, compiler_params=_cp("parallel"),
    )(u, v, norm_g, w_s, b_full, mix)


def sgu_bwd(u, v, norm_g, w_s, b_full, dy):
    T, W = u.shape
    ts = _tile(T, SGU_TS)
    wshape = w_s.shape

    def body(u_ref, v_ref, g_ref, w_ref, b_ref, dy_ref, duv_ref, dg_ref, dw_ref, db_ref):
        _, vjp = jax.vjp(_sgu, u_ref[...], v_ref[...], g_ref[...], w_ref[...], b_ref[...])
        du, dv, dg, dw, db = vjp(dy_ref[...].astype(F32))
        duv_ref[:, :W] = du.astype(BF16)
        duv_ref[:, W:] = dv.astype(BF16)

        @pl.when(pl.program_id(0) == 0)
        def _():
            dg_ref[...] = jnp.zeros_like(dg_ref)
            dw_ref[...] = jnp.zeros_like(dw_ref)
            db_ref[...] = jnp.zeros_like(db_ref)

        dg_ref[...] += dg
        dw_ref[...] += dw
        db_ref[...] += db

    row = pl.BlockSpec((ts, W), lambda i: (i, 0))
    wsp = pl.BlockSpec(wshape, lambda i: (0, 0, 0))
    wout = pl.BlockSpec(wshape, lambda i: (0, 0, 0))
    vec = pl.BlockSpec((1, W), lambda i: (0, 0))
    return pl.pallas_call(
        body, name="sgu_bwd", grid=(T // ts,),
        in_specs=[row, row, vec, wsp, wsp, pl.BlockSpec((ts, W), lambda i: (i, 1))],
        out_specs=[pl.BlockSpec((ts, 2 * W), lambda i: (i, 0)), vec, wout, wout],
        out_shape=[jax.ShapeDtypeStruct((T, 3 * W), BF16),
                   jax.ShapeDtypeStruct((1, W), F32), jax.ShapeDtypeStruct(wshape, F32),
                   jax.ShapeDtypeStruct(wshape, F32)],
        compiler_params=_cp("arbitrary"),
    )(u, v, norm_g, w_s, b_full, dy)


def loss_head(x, gain, target):
    T, D = x.shape
    tm = _tile(T, 512)

    def body(x_ref, g_ref, t_ref, loss_ref, dx_ref, dg_ref):
        xv, g = x_ref[...], g_ref[...]
        r = lax.rsqrt(jnp.mean(xv * xv, axis=-1, keepdims=True) + EPS)
        err = xv * r * g - t_ref[...]
        part = 0.5 * jnp.sum(jnp.mean(err * err, axis=-1, keepdims=True), axis=0, keepdims=True)
        dx, dg = _rms_bwd_math(xv, g, err * (1.0 / D), jnp.zeros_like(xv))
        dx_ref[...] = dx

        @pl.when(pl.program_id(0) == 0)
        def _():
            loss_ref[...] = jnp.zeros_like(loss_ref)
            dg_ref[...] = jnp.zeros_like(dg_ref)

        loss_ref[...] += part
        dg_ref[...] += dg

    row = pl.BlockSpec((tm, D), lambda i: (i, 0))
    vec = pl.BlockSpec((1, D), lambda i: (0, 0))
    return pl.pallas_call(
        body, name="loss_head", grid=(T // tm,),
        in_specs=[row, vec, row], out_specs=[pl.BlockSpec((1, 1), lambda i: (0, 0)), row, vec],
        out_shape=[jax.ShapeDtypeStruct((1, 1), F32), jax.ShapeDtypeStruct((T, D), F32),
                   jax.ShapeDtypeStruct((1, D), F32)],
        compiler_params=_cp("arbitrary"),
    )(x, gain, target)


class MultiJob:
    def __init__(self, jobs):
        self.jobs = jobs
        self.args = [a for j in jobs for a in j.args]
        self.out_shape = [s for j in jobs for s in j.out_shape]
        self.scratch = [s for j in jobs for s in j.scratch]
        self.n_in, self.n_out = len(self.args), len(self.out_shape)

    def _each(self, ins, outs, sems):
        i = o = s = 0
        for j in self.jobs:
            yield j, ins[i:i + j.n_in], outs[o:o + j.n_out], sems[s:s + len(j.scratch)]
            i, o, s = i + j.n_in, o + j.n_out, s + len(j.scratch)

    def start(self, ins, outs, sems):
        for j, a, b, c in self._each(ins, outs, sems):
            j.start(a, b, c)

    def middle(self, ins, outs, sems):
        for j, a, b, c in self._each(ins, outs, sems):
            j.middle(a, b, c)

    def finish(self, ins, outs, sems):
        for j, a, b, c in self._each(ins, outs, sems):
            j.finish(a, b, c)

    def split(self, results):
        o = 0
        for j in self.jobs:
            yield results[o:o + j.n_out]
            o += j.n_out


class Plan:
    def __init__(self, shard, gathers, scatters, small_carrier=None, pack_small=None):
        self.shard, self.gathers, self.scatters = shard, gathers, scatters
        self.small_carrier, self.pack_small = small_carrier, pack_small
        self.weights, self.grads, self.parts = {}, {}, {}
        self.small_src = self.small_parts = None

    def weight(self, kind, l):
        return self.weights[(kind, l)]

    def grad(self, kind, l, g):
        self.grads[(kind, l)] = g

    def small_ready(self, small, d_final):
        if self.pack_small is not None:
            self.small_src = self.pack_small(small, d_final)

    def _jobs(self, key):
        jobs = []
        if key in self.gathers:
            ks = self.gathers[key]
            jobs.append((GatherJob([self.shard(*k) for k in ks]), self.weights, ks))
        if key in self.scatters:
            ks = self.scatters[key]
            jobs.append((ScatterJob([self.grads[k] for k in ks]), self.parts, ks))
        if key == self.small_carrier and self.small_src is not None:
            jobs.append((GatherJob([(self.small_src, None)]), None, None))
        return jobs

    def _deliver(self, jobs, results):
        multi = MultiJob([j for j, _, _ in jobs])
        for (_, store, ks), res in zip(jobs, multi.split(results)):
            if store is None:
                self.small_parts = res[0]
            else:
                store.update(zip(ks, res))

    def run(self, key, fn, *args, **kw):
        jobs = self._jobs(key)
        if not jobs:
            out = fn(*args, **kw)
            return out if fn is mm_tn else out[0]
        res, jres = fn(*args, job=MultiJob([j for j, _, _ in jobs]), **kw)
        self._deliver(jobs, jres)
        return res

    def alone(self, key, name):
        jobs = self._jobs(key)
        if jobs:
            self._deliver(jobs, run_job(MultiJob([j for j, _, _ in jobs]), name))


def _local_step(x, target, layers, final_norm, n_ex, plan):
    T, D = x.shape
    L = len(layers)
    cst = _attn_consts(T // n_ex)
    ident = lambda j: j
    EV_A, EV_B = 3 * (D // 2), QW + 2 * KW
    OD_W = D // 2
    wt = plan.weight

    saved = []
    for l, W in enumerate(layers):
        s = dict(x0=x)
        x1, *s["gu1"] = plan.run(("ffn1_fwd", l), ffn_fwd, x, W["n1"], wt("f1_in_t", l), wt("f1_out", l))
        if l % 2 == 0:
            pa, pb, h = mm_nt(x1, wt("mx_in_t", l), [(0, EV_A), (EV_A, EV_B)], F32, True, W["nm"])
            qg = jnp.tile(W["q_norm"], N_Q_HEADS)[None]
            kg = jnp.tile(W["k_norm"], N_KV_HEADS)[None]
            mix = conv_fwd(pa, W["conv_w"], n_ex)
            qp, kp, vp = qkv_prep_fwd(pb, qg, kg, cst, n_ex)
            mix, op, lse = plan.run(("attn_fwd", l), attn_fwd, qp, kp, vp, cst["gather"], n_ex, mix)
            s.update(pa=pa, pb=pb, qg=qg, kg=kg, qp=qp, kp=kp, vp=vp, op=op, lse=lse)
        else:
            p, u, v, h = mm_nt(x1, wt("mx_in_t", l), [(0, OD_W), (OD_W, OD_W), (2 * OD_W, OD_W)], F32, True, W["nm"])
            scale = W["pool_scale"][None]
            sn = W["sgu_norm"][None]
            b_full = jnp.broadcast_to(W["sgu_b"][..., None], W["sgu_w"].shape)
            mix = sgu_fwd(u, v, sn, W["sgu_w"], b_full, pool_fwd(p, W["pool_w"], scale, n_ex))
            s.update(p=p, u=u, v=v, scale=scale, sn=sn, b_full=b_full)
        x2 = mm_nn([(mix, 0, D, 0)], wt("mx_out", l), residual=x1)
        x3, *s["gu2"] = plan.run(("ffn2_fwd", l), ffn_fwd, x2, W["n2"], wt("f2_in_t", l), wt("f2_out", l))
        s.update(x1=x1, x2=x2, h=h, mix=mix)
        saved.append(s)
        x = x3

    loss, dx, d_final = loss_head(x, final_norm, target)

    small = [None] * L

    def ffn_back(which, l, dout, xin, gain, gu_xn, sm, sm_key):
        w_in, w_out = wt(which + "_in_t", l), wt(which + "_out", l)
        F = w_out.shape[0]
        nc = F // FFN_TN
        gu, xn = gu_xn
        dxi, sm[sm_key], a, dgu, dob = plan.run((which + "_bwd", l), ffn_bwd_x, dout, xin, gain, gu, w_in, w_out)
        if which == "f1" and l == 0:
            plan.small_ready(small, d_final)
        plan.grad(which + "_in_t", l, plan.run(
            (which + "_in_grad", l), mm_tn, dgu, xn, 2 * F, lambda k, c: k * nc + c, grid=(2, nc),
            col_block=lambda k, c: 2 * c + k))
        plan.grad(which + "_out", l, plan.run((which + "_out_grad", l), mm_tn, a, dob, F, ident))
        return dxi

    for l in reversed(range(L)):
        s, W = saved[l], layers[l]
        sm = small[l] = {}
        dx = ffn_back("f2", l, dx, s["x2"], W["n2"], s["gu2"], sm, "n2")
        dmix, dxb = mm_nt(dx, wt("mx_out", l), [(0, D)], BF16, emit_a_bf16=True)
        plan.grad("mx_out", l, mm_tn(s["mix"], dxb, D, ident))
        if l % 2 == 0:
            d_proj, sm["conv_w"] = conv_bwd(s["pa"], W["conv_w"], n_ex, dmix, EV_A + EV_B)
            dq, dkp, dvp = plan.run(("attn_bwd", l), attn_bwd, s["qp"], s["kp"], s["vp"], s["op"], s["lse"], dmix, cst, n_ex)
            d_proj, dqg, dkg = qkv_prep_bwd(s["pb"], s["qg"], s["kg"], cst, n_ex, dq, dkp, dvp, d_proj)
            d_pieces = [(d_proj, 0, EV_A + EV_B, 0)]
            plan.grad("mx_in_t", l, mm_tn(d_proj, s["h"], EV_A + EV_B, ident))
            sm["q_norm"] = dqg.reshape(N_Q_HEADS, HEAD_DIM).sum(0)
            sm["k_norm"] = dkg.reshape(N_KV_HEADS, HEAD_DIM).sum(0)
        else:
            d_proj, d_sn, sm["sgu_w"], d_sb = sgu_bwd(s["u"], s["v"], s["sn"], W["sgu_w"], s["b_full"], dmix)
            d_proj, sm["pool_w"], d_ps = pool_bwd(s["p"], W["pool_w"], s["scale"], n_ex, dmix, d_proj)
            d_pieces = [(d_proj, 0, OD_W, 1), (d_proj, 1, OD_W, 2), (d_proj, 2, OD_W, 0)]
            nb = OD_W // MM_TC
            plan.grad("mx_in_t", l, mm_tn(d_proj, s["h"], 3 * OD_W,
                                          lambda jj: jnp.where(jj < 2 * nb, jj + nb, jj - 2 * nb)))
            sm["pool_scale"], sm["sgu_norm"], sm["sgu_b"] = d_ps[0], d_sn[0], d_sb.sum(-1)
        dx, sm["nm"] = mm_nn(d_pieces, wt("mx_in_t", l), norm_bwd=(s["x1"], W["nm"], dx))
        dx = ffn_back("f1", l, dx, s["x0"], W["n1"], s["gu1"], sm, "n1")
    return loss, dx


def all_gather(srcs):
    return run_job(GatherJob(srcs), "all_gather")


HBM_SPEC = pl.BlockSpec(memory_space=pltpu.HBM)
SEM_SPEC = pl.BlockSpec(memory_space=pltpu.SEMAPHORE)
SPLIT_COPY = pltpu.CompilerParams(has_side_effects=pltpu.SideEffectType.DATAFLOW_SIDE_EFFECTING)


def _scatter_copies(srcs, lands, send, recv, dims):
    x, y, c, me = _my_place()
    for t, (r, _) in enumerate(dims):
        for k in range(1, N_DEV):
            peer, pidx = _peer(x, y, c, k)
            rows = srcs[t].at[pl.ds(pl.multiple_of(pidx * r, 8), r), :]
            yield (_remote(rows, lands[t].at[me], send, recv, N_DEV * t + k, peer),
                   _remote(rows, lands[t].at[pidx], send, recv, N_DEV * t + k, peer))


def scatter_start(grads):
    n = len(grads)
    dims = [(g.shape[0] // N_DEV, g.shape[1]) for g in grads]

    def body(*refs):
        srcs, lands, send, recv, token = refs[:n], refs[n:2 * n], refs[2 * n], refs[2 * n + 1], refs[-1]
        for mine, _ in _scatter_copies(srcs, lands, send, recv, dims):
            mine.start()
        token[...] = jnp.zeros_like(token)

    zones = [lax.empty((N_DEV, r, cc), g.dtype) for g, (r, cc) in zip(grads, dims)]
    res = pl.pallas_call(
        body, name="scatter_start",
        out_shape=(pltpu.SemaphoreType.DMA((N_DEV * n,)), pltpu.SemaphoreType.DMA((N_DEV * n,)),
                   *[pltpu.HBM(g.shape, g.dtype) for g in grads], *[pltpu.HBM(z.shape, z.dtype) for z in zones],
                   jax.ShapeDtypeStruct((8, LANES), F32)),
        in_specs=[HBM_SPEC] * (2 * n),
        out_specs=(SEM_SPEC, SEM_SPEC, *([HBM_SPEC] * (2 * n)), pl.BlockSpec(memory_space=pltpu.VMEM)),
        input_output_aliases={i: 2 + i for i in range(2 * n)}, compiler_params=SPLIT_COPY,
    )(*[pltpu.with_memory_space_constraint(a, pltpu.HBM) for a in list(grads) + zones])
    return res[0], res[1], res[2:2 + n], res[2 + n:2 + 2 * n], res[-1]


def scatter_wait(send, recv, grads, zones, after):
    n = len(grads)
    dims = [(g.shape[0] // N_DEV, g.shape[1]) for g in grads]

    def body(*refs):
        srcs, lands, send_ref, recv_ref = refs[:n], refs[n:2 * n], refs[2 * n], refs[2 * n + 1]
        for mine, theirs in _scatter_copies(srcs, lands, send_ref, recv_ref, dims):
            mine.wait_send()
            theirs.wait_recv()

    res = pl.pallas_call(
        body, name="scatter_wait",
        out_shape=(*[pltpu.HBM(g.shape, g.dtype) for g in grads], *[pltpu.HBM(z.shape, z.dtype) for z in zones]),
        in_specs=[HBM_SPEC] * (2 * n) + [SEM_SPEC, SEM_SPEC, ANY], out_specs=[HBM_SPEC] * (2 * n),
        input_output_aliases={i: i for i in range(2 * n)}, compiler_params=SPLIT_COPY,
    )(*grads, *zones, send, recv, after)
    return res[:n], res[n:]


def cast_shards(w):
    L, A, B = w.shape

    def body(w_ref, o_ref):
        o_ref[...] = w_ref[...].astype(BF16)

    return pl.pallas_call(
        body, name="cast_shards", grid=(L,),
        in_specs=[pl.BlockSpec((None, A, B), lambda l: (l, 0, 0))],
        out_specs=pl.BlockSpec((None, A, B), lambda l: (l, 0, 0)),
        out_shape=jax.ShapeDtypeStruct((L, A, B), BF16), compiler_params=_cp("parallel"),
    )(w)


ADAM_TC = 256


def adamw(parts, w, m, v, l, prev=None, own=None, after=None):
    P, R, C = parts.shape
    tc = _tile(C, ADAM_TC)
    c1, c2 = 1.0 - ADAM_B1 ** ADAM_STEP, 1.0 - ADAM_B2 ** ADAM_STEP
    prev = list(prev) if prev is not None else []
    n_own = 0 if own is None else 1

    def body(*refs):
        p_ref, w_ref, m_ref, v_ref = refs[:4]
        g_ref, d_ref, mo_ref, vo_ref = refs[-4:]
        if own is None:
            term = lambda s: p_ref[s].astype(F32)
        else:
            me = _my_place()[3]
            term = lambda s: jnp.where(me == s, refs[4][s], p_ref[s]).astype(F32)
        g = term(0)
        for s in range(1, P):
            g = g + term(s)
        m1 = ADAM_B1 * m_ref[...] + (1.0 - ADAM_B1) * g
        v1 = ADAM_B2 * v_ref[...] + (1.0 - ADAM_B2) * (g * g)
        g_ref[...] = g
        mo_ref[...] = m1
        vo_ref[...] = v1
        d_ref[...] = -ADAM_LR * ((m1 / c1) / (jnp.sqrt(v1 / c2) + ADAM_EPS) + ADAM_WD * w_ref[...])

    wspec = pl.BlockSpec((None, R, tc), lambda i: (l, 0, i))
    pspec = pl.BlockSpec((P, R, tc), lambda i: (0, 0, i))
    extra = prev + ([] if after is None else [after])
    return pl.pallas_call(
        body, name="adamw", grid=(C // tc,),
        in_specs=[pspec, wspec, wspec, wspec] + [pspec] * n_own + [ANY] * len(extra),
        out_specs=[wspec] * 4, out_shape=[jax.ShapeDtypeStruct(w.shape, F32)] * 4,
        input_output_aliases={4 + n_own + i: i for i in range(len(prev))},
        compiler_params=_cp("parallel"),
    )(parts, w, m, v, *([] if own is None else [own]), *extra)


_WEIGHTS = ['ffn1_norm', 'ffn1_w_in', 'ffn1_w_out', 'mix_norm', 'ffn2_norm', 'ffn2_w_in', 'ffn2_w_out', 'ev_w_in',
            'ev_conv_w', 'ev_q_norm', 'ev_k_norm', 'ev_w_out', 'od_w_in', 'od_pool_w', 'od_pool_scale', 'od_sgu_norm',
            'od_sgu_w', 'od_sgu_b', 'od_w_out', 'final_norm']
_BIG = dict(ffn1_w_in=True, ffn1_w_out=False, ffn2_w_in=True, ffn2_w_out=False,
            ev_w_in=True, ev_w_out=False, od_w_in=True, od_w_out=False)
_SMALL_SHARDED = ['ev_conv_w', 'od_pool_scale', 'od_sgu_norm']
_SMALL = [n for n in _WEIGHTS if n not in _BIG]
_PACK_ROWS = 8 * LANES


_KINDS = ("f1_in_t", "f1_out", "mx_in_t", "mx_out", "f2_in_t", "f2_out")
_CARRIER_US = dict(ffn1_fwd=105, ffn2_fwd=105, attn_fwd=115, f2_bwd=135, f1_bwd=135, attn_bwd=205,
                   f2_out_grad=33, f1_out_grad=33, f2_in_grad=61, f1_in_grad=61)
_GATHER_US_PER_ROW, _SCATTER_US_PER_ROW, _SMALL_GATHER_US, _SLACK_US = 0.08, 0.176, 42, 10


def _schedule(L, rows):
    events = []
    for l in range(L):
        events += [("ffn1_fwd", l), ("mixer", l)] + ([("attn_fwd", l)] if l % 2 == 0 else []) + [("ffn2_fwd", l)]
    consumer = {"f1": "ffn1_fwd", "mx": "mixer", "f2": "ffn2_fwd"}
    queue = [(k, l) for l in range(L) for k in _KINDS]
    pos = {t: events.index((consumer[t[0][:2]], t[1])) for t in queue}
    gathers = {"first": [t for t in queue if pos[t] == 0]}
    queue = [t for t in queue if pos[t] > 0]
    carriers = [i for i, e in enumerate(events) if e[0] in _CARRIER_US]
    for i in carriers:
        budget, take = _CARRIER_US[events[i][0]], []
        later = [j for j in carriers if j > i]
        while queue:
            t = queue[0]
            cost = rows(*t) * _GATHER_US_PER_ROW
            forced = not any(j < pos[t] for j in later)
            if not forced and cost > budget:
                break
            take.append(queue.pop(0))
            budget -= cost
        if take:
            gathers[events[i]] = take
    assert not queue
    events = []
    for l in reversed(range(L)):
        events += [("f2_bwd", l), ("f2_in_grad", l), ("f2_out_grad", l), ("mx_out_ready", l)]
        events += [("attn_bwd", l)] if l % 2 == 0 else []
        events += [("mx_in_ready", l), ("f1_bwd", l), ("f1_in_grad", l), ("f1_out_grad", l)]
    made_by = {"f2_out": "f2_out_grad", "f2_in_t": "f2_in_grad", "mx_out": "mx_out_ready", "mx_in_t": "mx_in_ready",
               "f1_out": "f1_out_grad", "f1_in_t": "f1_in_grad"}
    small_carrier = ("f1_in_grad", 0)
    scatters, ready = {}, []
    for e in events:
        if e[0] in _CARRIER_US:
            budget, take = _CARRIER_US[e[0]] - (_SMALL_GATHER_US if e == small_carrier else 0), []
            while True:
                fits = [t for t in ready if rows(*t) * _SCATTER_US_PER_ROW <= budget + _SLACK_US]
                if not fits:
                    break
                t = max(fits, key=lambda u: rows(*u))
                budget -= rows(*t) * _SCATTER_US_PER_ROW
                ready.remove(t)
                take.append(t)
            if take:
                scatters[e] = take
        ready += [(k, e[1]) for k in _KINDS if made_by[k] == e[0]]
    scatters["last"] = ready
    return gathers, scatters, small_carrier


def _pack(arrs):
    flat = jnp.concatenate([a.reshape(-1) for a in arrs])
    pad = (-flat.shape[0]) % _PACK_ROWS
    return jnp.pad(flat, (0, pad)).reshape(-1, LANES)


def _unpack(buf, shapes):
    flat, out, off = buf.reshape(-1), [], 0
    for s in shapes:
        n = math.prod(s)
        out.append(flat[off:off + n].reshape(s))
        off += n
    return out


def _unshard_last(g, lead):
    nd = len(lead)
    return jnp.moveaxis(g, 0, nd).reshape(*lead, -1)


def kernel(x, ffn1_norm, ffn1_w_in, ffn1_w_out, mix_norm, ffn2_norm, ffn2_w_in, ffn2_w_out, ev_w_in, ev_conv_w, ev_q_norm, ev_k_norm, ev_w_out, od_w_in, od_pool_w, od_pool_scale, od_sgu_norm, od_sgu_w, od_sgu_b, od_w_out, final_norm, loss_target, m_ffn1_norm, m_ffn1_w_in, m_ffn1_w_out, m_mix_norm, m_ffn2_norm, m_ffn2_w_in, m_ffn2_w_out, m_ev_w_in, m_ev_conv_w, m_ev_q_norm, m_ev_k_norm, m_ev_w_out, m_od_w_in, m_od_pool_w, m_od_pool_scale, m_od_sgu_norm, m_od_sgu_w, m_od_sgu_b, m_od_w_out, m_final_norm, v_ffn1_norm, v_ffn1_w_in, v_ffn1_w_out, v_mix_norm, v_ffn2_norm, v_ffn2_w_in, v_ffn2_w_out, v_ev_w_in, v_ev_conv_w, v_ev_q_norm, v_ev_k_norm, v_ev_w_out, v_od_w_in, v_od_pool_w, v_od_pool_scale, v_od_sgu_norm, v_od_sgu_w, v_od_sgu_b, v_od_w_out, v_final_norm):
    w = dict(zip(_WEIGHTS, (ffn1_norm, ffn1_w_in, ffn1_w_out, mix_norm, ffn2_norm, ffn2_w_in, ffn2_w_out, ev_w_in, ev_conv_w, ev_q_norm, ev_k_norm, ev_w_out, od_w_in, od_pool_w, od_pool_scale, od_sgu_norm, od_sgu_w, od_sgu_b, od_w_out, final_norm)))
    m = dict(zip(_WEIGHTS, (m_ffn1_norm, m_ffn1_w_in, m_ffn1_w_out, m_mix_norm, m_ffn2_norm, m_ffn2_w_in, m_ffn2_w_out, m_ev_w_in, m_ev_conv_w, m_ev_q_norm, m_ev_k_norm, m_ev_w_out, m_od_w_in, m_od_pool_w, m_od_pool_scale, m_od_sgu_norm, m_od_sgu_w, m_od_sgu_b, m_od_w_out, m_final_norm)))
    v = dict(zip(_WEIGHTS, (v_ffn1_norm, v_ffn1_w_in, v_ffn1_w_out, v_mix_norm, v_ffn2_norm, v_ffn2_w_in, v_ffn2_w_out, v_ev_w_in, v_ev_conv_w, v_ev_q_norm, v_ev_k_norm, v_ev_w_out, v_od_w_in, v_od_pool_w, v_od_pool_scale, v_od_sgu_norm, v_od_sgu_w, v_od_sgu_b, v_od_w_out, v_final_norm)))
    n_ex, seq, D = x.shape
    T = n_ex * seq
    L = ffn1_norm.shape[0]
    me = 4 * lax.axis_index("x") + 2 * lax.axis_index("y") + lax.axis_index("c")

    sh_small = [w[n] for n in _SMALL_SHARDED]
    packed = all_gather([(_pack(sh_small), None)])[0].reshape(N_DEV, -1)
    full_small = {}
    off = 0
    for n, a in zip(_SMALL_SHARDED, sh_small):
        cnt = math.prod(a.shape)
        full_small[n] = _unshard_last(packed[:, off:off + cnt].reshape((N_DEV,) + a.shape), a.shape[:-1])
        off += cnt

    tr = lambda a: jnp.swapaxes(a, 1, 2)
    wmv = {n: tuple(tr(d[n]) if t else d[n] for d in (w, m, v)) for n, t in _BIG.items()}
    shards = {n: cast_shards(wmv[n][0]) for n in _BIG}

    def name_of(kind, l):
        mx = "ev" if l % 2 == 0 else "od"
        return {"f1_in_t": "ffn1_w_in", "f1_out": "ffn1_w_out", "f2_in_t": "ffn2_w_in", "f2_out": "ffn2_w_out",
                "mx_in_t": mx + "_w_in", "mx_out": mx + "_w_out"}[kind], (l // 2 if kind.startswith("mx") else l)

    def shard(kind, l):
        name, idx = name_of(kind, l)
        return shards[name], idx

    g_shapes = {}

    def pack_small(small, d_final):
        ev = [sm for l, sm in enumerate(small) if l % 2 == 0]
        od = [sm for l, sm in enumerate(small) if l % 2 == 1]
        st = lambda sms, k: jnp.stack([sm[k] for sm in sms])
        g_full = dict(ffn1_norm=st(small, "n1")[:, 0], mix_norm=st(small, "nm")[:, 0], ffn2_norm=st(small, "n2")[:, 0],
                      ev_conv_w=st(ev, "conv_w"), ev_q_norm=st(ev, "q_norm"), ev_k_norm=st(ev, "k_norm"),
                      od_pool_w=st(od, "pool_w"), od_pool_scale=st(od, "pool_scale"), od_sgu_norm=st(od, "sgu_norm"),
                      od_sgu_w=st(od, "sgu_w"), od_sgu_b=st(od, "sgu_b"), final_norm=d_final[0])
        g_shapes.update({n: g_full[n].shape for n in _SMALL})
        return _pack([g_full[n] for n in _SMALL])

    gathers, scatters, small_carrier = _schedule(L, lambda kind, l: shards[name_of(kind, l)[0]].shape[1])
    tail = scatters.pop("last")
    plan = Plan(shard, gathers, scatters, small_carrier, pack_small)
    layers = []
    for l in range(L):
        j = l // 2
        W = dict(n1=ffn1_norm[l][None], nm=mix_norm[l][None], n2=ffn2_norm[l][None])
        if l % 2 == 0:
            W.update(conv_w=full_small["ev_conv_w"][j], q_norm=ev_q_norm[j], k_norm=ev_k_norm[j])
        else:
            W.update(pool_w=od_pool_w[j], pool_scale=full_small["od_pool_scale"][j], sgu_norm=full_small["od_sgu_norm"][j],
                     sgu_w=od_sgu_w[j], sgu_b=od_sgu_b[j])
        layers.append(W)

    plan.alone("first", "gather_first")
    loss, dx = _local_step(x.reshape(T, D), loss_target.reshape(T, D), layers, final_norm[None], n_ex, plan)
    send_sems, recv_sems, tail_grads, tail_zones, token = scatter_start([plan.grads[k] for k in tail])

    out = {n: None for n in _BIG}
    for (kind, l), parts in plan.parts.items():
        name, idx = name_of(kind, l)
        out[name] = adamw(parts, *wmv[name], idx, prev=out[name], after=token)
        token = None

    g8 = plan.small_parts.reshape(N_DEV, -1)
    cols, off = [], 0
    for n in _SMALL:
        cnt = math.prod(g_shapes[n])
        g = g8[:, off:off + cnt].reshape((N_DEV,) + g_shapes[n])
        off += cnt
        if n in _SMALL_SHARDED:
            width = w[n].shape[-1]
            g = lax.dynamic_slice_in_dim(g, me * width, width, axis=g.ndim - 1)
        cols.append(g.reshape(N_DEV, -1))
    g8 = jnp.concatenate(cols, axis=1)
    g8 = jnp.pad(g8, ((0, 0), (0, (-g8.shape[1]) % _PACK_ROWS))).reshape(N_DEV, -1, LANES)
    pk = lambda d: _pack([d[n] for n in _SMALL])[None]
    small_out = adamw(g8, pk(w), pk(m), pk(v), 0)

    tail_grads, tail_zones = scatter_wait(send_sems, recv_sems, tail_grads, tail_zones, small_out[0])
    for (kind, l), own, parts in zip(tail, tail_grads, tail_zones):
        name, idx = name_of(kind, l)
        out[name] = adamw(parts, *wmv[name], idx, prev=out[name], own=own.reshape(parts.shape))
    out = {n: [tr(a) if _BIG[n] else a for a in res] for n, res in out.items()}

    shapes = [w[n].shape for n in _SMALL]
    for i in range(4):
        for n, a in zip(_SMALL, _unpack(small_out[i], shapes)):
            out.setdefault(n, [None] * 4)[i] = a

    total = lax.psum(loss[0, 0], ("x", "y", "c"))
    return (total, dx.reshape(n_ex, seq, D), *[out[n][0] for n in _WEIGHTS], *[out[n][1] for n in _WEIGHTS],
            *[out[n][2] for n in _WEIGHTS], *[out[n][3] for n in _WEIGHTS])
```

```python
import functools
import math

import jax
import jax.numpy as jnp
from jax import lax
from jax.experimental import pallas as pl
from jax.experimental.pallas import tpu as pltpu

F32, BF16 = jnp.float32, jnp.bfloat16
EPS = 1e-6
N_DEV = 8
V7X_VMEM_BYTES = 64 * 1024 * 1024
VMEM_LIMIT = V7X_VMEM_BYTES - 8 * 1024 * 1024
LANES = 128
HEAD_DIM = 64
N_Q_HEADS = 8
N_KV_HEADS = 2
Q_PER_KV = N_Q_HEADS // N_KV_HEADS
GRID_W = 64
ROPE_THETA = 10000.0
POOL_RADII = (1, 2, 4, 8)
SGU_CHUNK = 128
GROUP = 128
ADAM_LR, ADAM_B1, ADAM_B2, ADAM_EPS, ADAM_WD, ADAM_STEP = 0.001, 0.9, 0.999, 1e-08, 0.01, 10
MESH_ID = pl.DeviceIdType.MESH


def _cp(*sem):
    return pltpu.CompilerParams(dimension_semantics=sem, vmem_limit_bytes=VMEM_LIMIT)


def _dot(a, b, ca, cb):
    return lax.dot_general(a, b, (((ca,), (cb,)), ((), ())), preferred_element_type=F32)


def _nn(a, b):
    return _dot(a, b, 1, 0)


def _nt(a, b):
    return _dot(a, b, 1, 1)


def _tn(a, b):
    return _dot(a, b, 0, 0)


def _split_mm(x, m):
    hi = x.astype(BF16)
    lo = (x - hi.astype(F32)).astype(BF16)
    return _nn(hi, m) + _nn(lo, m)


def _tile(n, pref):
    t = min(n, pref)
    assert n % t == 0, (n, pref)
    return t


ANY = pl.BlockSpec(memory_space=pl.ANY)
JOB_MIDDLE = 0.55


def _my_place():
    x, y, c = lax.axis_index("x"), lax.axis_index("y"), lax.axis_index("c")
    return x, y, c, 4 * x + 2 * y + c


def _peer(x, y, c, k):
    px = 1 - x if k & 4 else x
    py = 1 - y if k & 2 else y
    pc = 1 - c if k & 1 else c
    return (px, py, pc), 4 * px + 2 * py + pc


def _remote(src, dst, send_sems, recv_sems, i, peer):
    return pltpu.make_async_remote_copy(src_ref=src, dst_ref=dst, send_sem=send_sems.at[i], recv_sem=recv_sems.at[i],
                                        device_id=peer, device_id_type=MESH_ID)


class GatherJob:
    def __init__(self, srcs):
        self.srcs = srcs
        self.args = [a for a, _ in srcs]
        self.dims = [a.shape[-2:] for a, _ in srcs]
        n = self.n_in = self.n_out = len(srcs)
        self.out_shape = [jax.ShapeDtypeStruct((N_DEV * r, cc), a.dtype) for (a, _), (r, cc) in zip(srcs, self.dims)]
        self.scratch = [pltpu.SemaphoreType.DMA((N_DEV * n,)), pltpu.SemaphoreType.DMA((N_DEV * n,)),
                        pltpu.SemaphoreType.DMA((n,))]

    def _rows(self, outs, t, idx):
        r = self.dims[t][0]
        return outs[t].at[pl.ds(pl.multiple_of(idx * r, 8), r), :]

    def _local(self, ins, outs, loc, t, me):
        src = ins[t] if self.srcs[t][1] is None else ins[t].at[self.srcs[t][1]]
        return src, pltpu.make_async_copy(src, self._rows(outs, t, me), loc.at[t])

    def start(self, ins, outs, sems):
        send, recv, loc = sems
        x, y, c, me = _my_place()
        for t in range(self.n_in):
            src, local = self._local(ins, outs, loc, t, me)
            local.start()
            for k in (2, 4, 1):
                _remote(src, self._rows(outs, t, me), send, recv, N_DEV * t + k, _peer(x, y, c, k)[0]).start()

    def _copy(self, outs, sems, t, origin, i, to):
        x, y, c, _ = _my_place()
        blk = self._rows(outs, t, _peer(x, y, c, origin)[1])
        return _remote(blk, blk, sems[0], sems[1], N_DEV * t + i, _peer(x, y, c, to)[0])

    def middle(self, ins, outs, sems):
        c = _my_place()[2]

        def relay(t, got, to):
            self._copy(outs, sems, t, got, got, got).wait_recv()
            self._copy(outs, sems, t, got, 6, to).start()
            self._copy(outs, sems, t, to, to, to).wait_recv()

        for t in range(self.n_in):
            pl.when(c == 1)(functools.partial(relay, t, 2, 4))
            pl.when(c == 0)(functools.partial(relay, t, 4, 2))
            for k in (2, 4):
                self._copy(outs, sems, t, k, k + 1, 1).start()

    def finish(self, ins, outs, sems):
        send, recv, loc = sems
        x, y, c, me = _my_place()
        for t in range(self.n_in):
            self._copy(outs, sems, t, 6, 6, 6).wait_recv()
            self._copy(outs, sems, t, 6, 7, 1).start()
        for t in range(self.n_in):
            for k in range(1, N_DEV):
                peer, pidx = _peer(x, y, c, k)
                blk = self._rows(outs, t, pidx)
                if k % 2 == 1:
                    _remote(blk, blk, send, recv, N_DEV * t + k, peer).wait_recv()
                _remote(blk, blk, send, recv, N_DEV * t + k, peer).wait_send()
            self._local(ins, outs, loc, t, me)[1].wait()


def _call(body, name, grid, in_specs, args, out_specs, out_shape, scratch=(), sem=(), job=None, aliases=None):
    in_specs, out_specs, out_shape, scratch = list(in_specs), list(out_specs), list(out_shape), list(scratch)
    n_in, n_out, n_scr = len(args), len(out_shape), len(scratch)
    if job is None:
        res = pl.pallas_call(body, name=name, grid=grid, in_specs=in_specs, out_specs=out_specs, out_shape=out_shape,
                             scratch_shapes=scratch, input_output_aliases=aliases or {}, compiler_params=_cp(*sem))(*args)
        return res, None
    o0 = n_in + job.n_in
    s0 = o0 + n_out + job.n_out

    def carrier(*refs):
        jin, jout, jsem = refs[n_in:o0], refs[o0 + n_out:s0], refs[s0 + n_scr:]
        ids = [pl.program_id(a) for a in range(len(grid))]
        def at(step):
            idx = []
            for g in reversed(grid):
                idx.append(step % g)
                step //= g
            return functools.reduce(jnp.logical_and, [i == j for i, j in zip(ids, reversed(idx))])

        steps = math.prod(grid)
        if grid:
            pl.when(at(0))(lambda: job.start(jin, jout, jsem))
            pl.when(at(int(steps * JOB_MIDDLE)))(lambda: job.middle(jin, jout, jsem))
        else:
            job.start(jin, jout, jsem)
            job.middle(jin, jout, jsem)
        body(*refs[:n_in], *refs[o0:o0 + n_out], *refs[s0:s0 + n_scr])
        if grid:
            pl.when(at(steps - 1))(lambda: job.finish(jin, jout, jsem))
        else:
            job.finish(jin, jout, jsem)

    res = pl.pallas_call(
        carrier, name=name + "_comm", grid=grid, in_specs=in_specs + [ANY] * job.n_in,
        out_specs=out_specs + [ANY] * job.n_out, out_shape=out_shape + job.out_shape,
        scratch_shapes=scratch + job.scratch, input_output_aliases=aliases or {},
        compiler_params=_cp(*(["arbitrary"] * len(grid))))(*args, *job.args)
    return res[:n_out], res[n_out:]


def run_job(job, name):
    return _call(lambda: None, name, (), [], [], [], [], job=job)[1]


@jax.custom_vjp
def bmm(x, w):
    return _nn(x.astype(BF16), w.astype(BF16))


def _bmm_fwd(x, w):
    return bmm(x, w), (x, w)


def _bmm_bwd(res, g):
    x, w = res
    gb = g.astype(BF16)
    return _nt(gb, w.astype(BF16)), _tn(x.astype(BF16), gb)


bmm.defvjp(_bmm_fwd, _bmm_bwd)


def _shift_raw(x, d):
    n = x.shape[0]
    r = pltpu.roll(x, d % n, axis=0)
    row = lax.broadcasted_iota(jnp.int32, x.shape, 0)
    keep = (row >= d) if d > 0 else (row < n + d)
    return jnp.where(keep, r, 0.0)


def shift_rows(x, d):
    @jax.custom_vjp
    def f(v):
        return _shift_raw(v, d)

    f.defvjp(lambda v: (_shift_raw(v, d), None), lambda _, g: (_shift_raw(g, -d),))
    return f(x)


def _swap_raw(x):
    n = x.shape[1]
    nxt = pltpu.roll(x, n - 1, axis=1)
    prv = pltpu.roll(x, 1, axis=1)
    lane = lax.broadcasted_iota(jnp.int32, x.shape, 1)
    return jnp.where(lane % 2 == 0, nxt, prv)


@jax.custom_vjp
def swap_pairs(x):
    return _swap_raw(x)


swap_pairs.defvjp(lambda x: (_swap_raw(x), None), lambda _, g: (_swap_raw(g),))


@jax.custom_vjp
def group_mean(x, bd):
    return _split_mm(x, bd)


group_mean.defvjp(lambda x, bd: (_split_mm(x, bd), bd), lambda bd, g: (_split_mm(g, bd), jnp.zeros_like(bd)))


def _rope_norm(x, gain, cos, sgn, bd, scale):
    xn = x * lax.rsqrt(group_mean(x * x, bd) + EPS) * gain
    return (xn * cos + swap_pairs(xn) * sgn) * scale


def _conv_gate(gb, gc, hc, w):
    z = gc * hc
    c = shift_rows(z, 1) * w[0:1] + z * w[1:2] + shift_rows(z, -1) * w[2:3]
    return gb * c


def _window_sum(p, r):
    b = f = p
    k = 1
    while k < r:
        b = b + shift_rows(b, k)
        f = f + shift_rows(f, -k)
        k *= 2
    return b + f - p + shift_rows(p, r) + shift_rows(p, -r)


def _pool_mix(p, pool_w, scale):
    n = p.shape[0]
    t = lax.broadcasted_iota(jnp.int32, (n, 1), 0)
    outs = []
    for gi, r in enumerate(POOL_RADII):
        pg = p[:, gi * GROUP:(gi + 1) * GROUP]
        cnt = (jnp.minimum(t + r, n - 1) - jnp.maximum(t - r, 0) + 1).astype(F32)
        pooled = _window_sum(pg, r) / cnt - pg
        outs.append(bmm(pooled, pool_w[gi]))
    return jnp.concatenate(outs, axis=1) * scale


def _sgu(u, v, norm_g, w_s, b_full):
    ug = jax.nn.gelu(u)
    vg = jax.nn.gelu(v)
    vn = vg * lax.rsqrt(jnp.mean(vg * vg, axis=-1, keepdims=True) + EPS) * norm_g
    cols = []
    for g in range(w_s.shape[0]):
        rows = []
        for n in range(u.shape[0] // SGU_CHUNK):
            blk = vn[n * SGU_CHUNK:(n + 1) * SGU_CHUNK, g * GROUP:(g + 1) * GROUP]
            rows.append(bmm(w_s[g], blk) + b_full[g])
        cols.append(jnp.concatenate(rows, axis=0))
    return ug * jnp.concatenate(cols, axis=1)


def _rms_bwd_math(xv, gain, dy, dres):
    r = lax.rsqrt(jnp.mean(xv * xv, axis=-1, keepdims=True) + EPS)
    xh = xv * r
    dxh = dy * gain
    dx = dres + r * (dxh - xh * jnp.mean(dxh * xh, axis=-1, keepdims=True))
    return dx, jnp.sum(dy * xh, axis=0, keepdims=True)


FFN_TN = 256


def ffn_fwd(x, gain, wt_in, w_out, job=None):
    T, D = x.shape
    F = w_out.shape[0]
    tm, tn = _tile(T, 1024), FFN_TN
    nc = F // tn

    def body(x_ref, gn_ref, wg_ref, wu_ref, wo_ref, y_ref, gu_ref, xn_s, acc_s):
        c = pl.program_id(1)

        @pl.when(c == 0)
        def _():
            xv = x_ref[...]
            r = lax.rsqrt(jnp.mean(xv * xv, axis=-1, keepdims=True) + EPS)
            xn_s[...] = (xv * r * gn_ref[...]).astype(BF16)
            acc_s[...] = jnp.zeros_like(acc_s)

        xn = xn_s[...]
        g = _nt(xn, wg_ref[...])
        u = _nt(xn, wu_ref[...])
        gu_ref[:, :tn] = g.astype(BF16)
        gu_ref[:, tn:] = u.astype(BF16)
        a = (g * jax.nn.sigmoid(g) * u).astype(BF16)
        acc_s[...] += _nn(a, wo_ref[...])

        @pl.when(c == nc - 1)
        def _():
            y_ref[...] = x_ref[...] + 0.5 * acc_s[...]

    row = pl.BlockSpec((tm, D), lambda i, c: (i, 0))
    return _call(
        body, "ffn_fwd", (T // tm, nc),
        [row, pl.BlockSpec((1, D), lambda i, c: (0, 0)),
         pl.BlockSpec((tn, D), lambda i, c: (c, 0)),
         pl.BlockSpec((tn, D), lambda i, c: (c + nc, 0)),
         pl.BlockSpec((tn, D), lambda i, c: (c, 0))],
        [x, gain, wt_in, wt_in, w_out],
        [row, pl.BlockSpec((tm, 2 * tn), lambda i, c: (i, c)), row],
        [jax.ShapeDtypeStruct((T, D), F32), jax.ShapeDtypeStruct((T, 2 * F), BF16), jax.ShapeDtypeStruct((T, D), BF16)],
        [pltpu.VMEM((tm, D), F32)], ("parallel", "arbitrary"), job)


def ffn_bwd_x(dout, x, gain, gu, wt_in, w_out, job=None):
    T, D = x.shape
    F = w_out.shape[0]
    tm, tn = _tile(T, 1024), FFN_TN
    nc = F // tn

    def body(do_ref, x_ref, gn_ref, gu_ref, wg_ref, wu_ref, wo_ref,
             dx_ref, dgn_ref, a_ref, dgu_ref, dob_ref, acc_s):
        i, c = pl.program_id(0), pl.program_id(1)

        @pl.when(c == 0)
        def _():
            dob_ref[...] = (0.5 * do_ref[...]).astype(BF16)
            acc_s[...] = jnp.zeros_like(acc_s)

        da = jnp.concatenate([_nt(dob_ref[:tm // 2, :], wo_ref[...]), _nt(dob_ref[tm // 2:, :], wo_ref[...])], axis=0)
        g = gu_ref[:, :tn].astype(F32)
        u = gu_ref[:, tn:].astype(F32)
        sig = jax.nn.sigmoid(g)
        sl = g * sig
        a_ref[...] = (sl * u).astype(BF16)
        dg = (da * u * (sig * (1.0 + g * (1.0 - sig)))).astype(BF16)
        du = (da * sl).astype(BF16)
        dgu_ref[:, :tn] = dg
        dgu_ref[:, tn:] = du
        acc_s[...] += _nn(dg, wg_ref[...]) + _nn(du, wu_ref[...])

        @pl.when(c == nc - 1)
        def _():
            dx, dgn = _rms_bwd_math(x_ref[...], gn_ref[...], acc_s[...], do_ref[...])
            dx_ref[...] = dx

            @pl.when(i == 0)
            def _():
                dgn_ref[...] = jnp.zeros_like(dgn_ref)

            dgn_ref[...] += dgn

    row = pl.BlockSpec((tm, D), lambda i, c: (i, 0))
    return _call(
        body, "ffn_bwd_x", (T // tm, nc),
        [row, row, pl.BlockSpec((1, D), lambda i, c: (0, 0)),
         pl.BlockSpec((tm, 2 * tn), lambda i, c: (i, c)),
         pl.BlockSpec((tn, D), lambda i, c: (c, 0)),
         pl.BlockSpec((tn, D), lambda i, c: (c + nc, 0)),
         pl.BlockSpec((tn, D), lambda i, c: (c, 0))],
        [dout, x, gain, gu, wt_in, wt_in, w_out],
        [row, pl.BlockSpec((1, D), lambda i, c: (0, 0)),
         pl.BlockSpec((tm, tn), lambda i, c: (i, c)),
         pl.BlockSpec((tm, 2 * tn), lambda i, c: (i, c)), row],
        [jax.ShapeDtypeStruct((T, D), F32), jax.ShapeDtypeStruct((1, D), F32),
         jax.ShapeDtypeStruct((T, F), BF16), jax.ShapeDtypeStruct((T, 2 * F), BF16),
         jax.ShapeDtypeStruct((T, D), BF16)],
        [pltpu.VMEM((tm, D), F32)], ("arbitrary", "arbitrary"), job)


MM_TM = 512
MM_TC = 256


def mm_nt(a, wt, pieces, out_dtype, emit_a_bf16=False, norm_gain=None):
    T, K = a.shape
    tm = _tile(T, MM_TM)
    npc = len(pieces)
    n_lead = 1 if norm_gain is None else 2

    def body(*refs):
        a_ref, w_refs, o_refs = refs[0], refs[n_lead:n_lead + npc], refs[n_lead + npc:]
        av = a_ref[...]
        if norm_gain is not None:
            av = av * lax.rsqrt(jnp.mean(av * av, axis=-1, keepdims=True) + EPS) * refs[1][...]
        ab = av.astype(BF16)
        for w_ref, o_ref in zip(w_refs, o_refs[:npc]):
            o_ref[...] = _nt(ab, w_ref[...]).astype(o_ref.dtype)
        if emit_a_bf16:
            o_refs[npc][...] = ab

    in_specs = [pl.BlockSpec((tm, K), lambda i: (i, 0))]
    if norm_gain is not None:
        in_specs.append(pl.BlockSpec((1, K), lambda i: (0, 0)))
    out_specs, out_shape = [], []
    for r0, n in pieces:
        assert r0 % n == 0
        in_specs.append(pl.BlockSpec((n, K), functools.partial(lambda i, b: (b, 0), b=r0 // n)))
        out_specs.append(pl.BlockSpec((tm, n), lambda i: (i, 0)))
        out_shape.append(jax.ShapeDtypeStruct((T, n), out_dtype))
    if emit_a_bf16:
        out_specs.append(pl.BlockSpec((tm, K), lambda i: (i, 0)))
        out_shape.append(jax.ShapeDtypeStruct((T, K), BF16))
    return pl.pallas_call(
        body, name="mm_nt", grid=(T // tm,), in_specs=in_specs, out_specs=out_specs, out_shape=out_shape,
        compiler_params=_cp("parallel"),
    )(a, *([] if norm_gain is None else [norm_gain]), *([wt] * npc))


def mm_nn(pieces, w, residual=None, norm_bwd=None):
    T = pieces[0][0].shape[0]
    N = w.shape[1]
    tm = _tile(T, MM_TM)
    na = len(pieces)

    def body(*refs):
        a_refs, w_refs = refs[:na], refs[na:2 * na]
        acc = refs[2 * na][...] if residual is not None else None
        for a_ref, w_ref in zip(a_refs, w_refs):
            t = _nn(a_ref[...].astype(BF16), w_ref[...])
            acc = t if acc is None else acc + t
        if norm_bwd is None:
            refs[-1][...] = acc
            return
        x_ref, g_ref, dr_ref, dx_ref, dg_ref = refs[-5:]
        dx, dg = _rms_bwd_math(x_ref[...], g_ref[...], acc, dr_ref[...])
        dx_ref[...] = dx

        @pl.when(pl.program_id(0) == 0)
        def _():
            dg_ref[...] = jnp.zeros_like(dg_ref)

        dg_ref[...] += dg

    in_specs, w_specs = [], []
    for a, cb, k, rb in pieces:
        in_specs.append(pl.BlockSpec((tm, k), functools.partial(lambda i, b: (i, b), b=cb)))
        w_specs.append(pl.BlockSpec((k, N), functools.partial(lambda i, b: (b, 0), b=rb)))
    assert sum(k for _, _, k, _ in pieces) == w.shape[0]
    args = [a for a, _, _, _ in pieces] + [w] * na
    in_specs = in_specs + w_specs
    row = pl.BlockSpec((tm, N), lambda i: (i, 0))
    if residual is not None:
        in_specs.append(row)
        args.append(residual)
    if norm_bwd is None:
        return pl.pallas_call(
            body, name="mm_nn", grid=(T // tm,), in_specs=in_specs, out_specs=row,
            out_shape=jax.ShapeDtypeStruct((T, N), F32), compiler_params=_cp("parallel"),
        )(*args)
    vec = pl.BlockSpec((1, N), lambda i: (0, 0))
    return pl.pallas_call(
        body, name="mm_nn_norm_bwd", grid=(T // tm,), in_specs=in_specs + [row, vec, row], out_specs=[row, vec],
        out_shape=[jax.ShapeDtypeStruct((T, N), F32), jax.ShapeDtypeStruct((1, N), F32)],
        compiler_params=_cp("arbitrary"),
    )(*args, *norm_bwd)


def mm_tn(a, b, n_rows, row_block, prev=None, grid=None, col_block=None, job=None):
    T, M = a.shape
    N = b.shape[1]
    tc = MM_TC
    assert M % tc == 0 and n_rows % tc == 0
    if grid is None:
        grid, col_block = (M // tc,), (lambda j: j)

    def body(*refs):
        a_ref, b_ref, o_ref = refs[0], refs[1], refs[-1]
        o_ref[...] = _tn(a_ref[...], b_ref[...]).astype(BF16)

    in_specs = [pl.BlockSpec((T, tc), lambda *g: (0, col_block(*g))), pl.BlockSpec((T, N), lambda *g: (0, 0))]
    args = [a, b]
    aliases = {}
    if prev is not None:
        in_specs.append(pl.BlockSpec(memory_space=pl.ANY))
        args.append(prev)
        aliases = {2: 0}
    res, jres = _call(body, "mm_tn", grid, in_specs, args, [pl.BlockSpec((tc, N), lambda *g: (row_block(*g), 0))],
                      [jax.ShapeDtypeStruct((n_rows, N), BF16)], (), ["parallel"] * len(grid), job, aliases)
    return res[0] if job is None else (res[0], jres)


def conv_fwd(proj_a, conv_w, n_ex):
    T, C3 = proj_a.shape
    C = C3 // 3
    S = T // n_ex

    def body(gb_ref, gc_ref, hc_ref, w_ref, o_ref):
        o_ref[...] = _conv_gate(gb_ref[...], gc_ref[...], hc_ref[...], w_ref[...]).astype(BF16)

    col = lambda k: pl.BlockSpec((S, C), functools.partial(lambda b, kk: (b, kk), kk=k))
    return pl.pallas_call(
        body, name="conv_fwd", grid=(n_ex,),
        in_specs=[col(0), col(1), col(2), pl.BlockSpec((3, C), lambda b: (0, 0))],
        out_specs=pl.BlockSpec((S, C), lambda b: (b, 0)),
        out_shape=jax.ShapeDtypeStruct((T, 2 * C), BF16), compiler_params=_cp("parallel"),
    )(proj_a, proj_a, proj_a, conv_w)


def conv_bwd(proj_a, conv_w, n_ex, dy, total_cols):
    T, C3 = proj_a.shape
    C = C3 // 3
    S = T // n_ex

    def body(gb_ref, gc_ref, hc_ref, w_ref, dy_ref, dp_ref, dw_ref):
        _, vjp = jax.vjp(_conv_gate, gb_ref[...], gc_ref[...], hc_ref[...], w_ref[...])
        dgb, dgc, dhc, dw = vjp(dy_ref[...].astype(F32))
        dp_ref[:, 0:C] = dgb.astype(BF16)
        dp_ref[:, C:2 * C] = dgc.astype(BF16)
        dp_ref[:, 2 * C:] = dhc.astype(BF16)

        @pl.when(pl.program_id(0) == 0)
        def _():
            dw_ref[...] = jnp.zeros_like(dw_ref)

        dw_ref[...] += dw

    col = lambda k: pl.BlockSpec((S, C), functools.partial(lambda b, kk: (b, kk), kk=k))
    return pl.pallas_call(
        body, name="conv_bwd", grid=(n_ex,),
        in_specs=[col(0), col(1), col(2), pl.BlockSpec((3, C), lambda b: (0, 0)),
                  pl.BlockSpec((S, C), lambda b: (b, 0))],
        out_specs=[pl.BlockSpec((S, C3), lambda b: (b, 0)), pl.BlockSpec((3, C), lambda b: (0, 0))],
        out_shape=[jax.ShapeDtypeStruct((T, total_cols), BF16), jax.ShapeDtypeStruct((3, C), F32)],
        compiler_params=_cp("arbitrary"),
    )(proj_a, proj_a, proj_a, conv_w, dy)


QW = N_Q_HEADS * HEAD_DIM
KW = N_KV_HEADS * HEAD_DIM
QP = N_Q_HEADS * LANES
KP = N_KV_HEADS * LANES


def _attn_consts(seq):
    rows = seq // GRID_W
    r_idx, c_idx = jnp.meshgrid(jnp.arange(rows), jnp.arange(GRID_W), indexing='ij')
    r_idx = r_idx.reshape(-1).astype(F32)
    c_idx = c_idx.reshape(-1).astype(F32)
    n_freq = HEAD_DIM // 4
    inv = ROPE_THETA ** (-jnp.arange(n_freq, dtype=F32) / n_freq)
    ang = jnp.concatenate([r_idx[:, None] * inv, c_idx[:, None] * inv], axis=-1)
    cos = jnp.repeat(jnp.cos(ang), 2, axis=1)
    sin = jnp.repeat(jnp.sin(ang), 2, axis=1)
    sgn = sin * jnp.tile(jnp.array([-1.0, 1.0], F32), HEAD_DIM // 2)
    cos = jnp.tile(cos, (1, N_Q_HEADS))
    sgn = jnp.tile(sgn, (1, N_Q_HEADS))
    lane = jnp.arange(QW)
    bd = jnp.where(lane[:, None] // HEAD_DIM == lane[None, :] // HEAD_DIM, 1.0 / HEAD_DIM, 0.0).astype(BF16)
    dst = (lane // HEAD_DIM) * LANES + lane % HEAD_DIM
    spread = (dst[:, None] == jnp.arange(QP)[None, :]).astype(BF16)
    return dict(cos=cos, sgn=sgn, bd=bd, spread=spread, gather=spread.T)


def qkv_prep_fwd(proj_b, qg, kg, cst, n_ex):
    T = proj_b.shape[0]
    S = T // n_ex
    tm = _tile(S, 512)
    nb = S // tm

    def body(p_ref, qg_ref, kg_ref, cos_ref, sgn_ref, bd_ref, sp_ref, q_ref, k_ref, v_ref):
        pv = p_ref[...]
        cos, sgn, bd, sp = cos_ref[...], sgn_ref[...], bd_ref[...], sp_ref[...]
        qr = _rope_norm(pv[:, :QW], qg_ref[...], cos, sgn, bd, HEAD_DIM ** -0.5)
        kr = _rope_norm(pv[:, QW:QW + KW], kg_ref[...], cos[:, :KW], sgn[:, :KW], bd[:KW, :KW], 1.0)
        q_ref[...] = _nn(qr.astype(BF16), sp).astype(BF16)
        k_ref[...] = _nn(kr.astype(BF16), sp[:KW, :KP]).astype(BF16)
        v_ref[...] = _nn(pv[:, QW + KW:].astype(BF16), sp[:KW, :KP]).astype(BF16)

    full = lambda a: pl.BlockSpec(a.shape, lambda i: (0,) * a.ndim)
    tab = pl.BlockSpec((tm, QW), lambda i: (i % nb, 0))
    return pl.pallas_call(
        body, name="qkv_prep_fwd", grid=(T // tm,),
        in_specs=[pl.BlockSpec((tm, QW + 2 * KW), lambda i: (i, 0)), full(qg), full(kg), tab, tab,
                  full(cst["bd"]), full(cst["spread"])],
        out_specs=[pl.BlockSpec((tm, QP), lambda i: (i, 0)), pl.BlockSpec((tm, KP), lambda i: (i, 0)),
                   pl.BlockSpec((tm, KP), lambda i: (i, 0))],
        out_shape=[jax.ShapeDtypeStruct((T, QP), BF16), jax.ShapeDtypeStruct((T, KP), BF16),
                   jax.ShapeDtypeStruct((T, KP), BF16)],
        compiler_params=_cp("parallel"),
    )(proj_b, qg, kg, cst["cos"], cst["sgn"], cst["bd"], cst["spread"])


def qkv_prep_bwd(proj_b, qg, kg, cst, n_ex, dq, dk_pad, dv_pad, d_proj):
    T = proj_b.shape[0]
    S = T // n_ex
    tm = _tile(S, 512)
    nb = S // tm

    def body(p_ref, qg_ref, kg_ref, cos_ref, sgn_ref, bd_ref, ga_ref, dq_ref, dk_ref, dv_ref, _kept,
             dp_ref, dqg_ref, dkg_ref):
        pv = p_ref[...]
        cos, sgn, bd, ga = cos_ref[...], sgn_ref[...], bd_ref[...], ga_ref[...]
        fq = lambda q, g: _rope_norm(q, g, cos, sgn, bd, HEAD_DIM ** -0.5)
        fk = lambda k, g: _rope_norm(k, g, cos[:, :KW], sgn[:, :KW], bd[:KW, :KW], 1.0)
        _, vq = jax.vjp(fq, pv[:, :QW], qg_ref[...])
        _, vk = jax.vjp(fk, pv[:, QW:QW + KW], kg_ref[...])
        dqp, dqg = vq(dq_ref[...])
        dkp, dkg = vk(_split_mm(dk_ref[...], ga[:KP, :KW]))
        dp_ref[:, :QW] = dqp.astype(BF16)
        dp_ref[:, QW:QW + KW] = dkp.astype(BF16)
        dp_ref[:, QW + KW:] = _split_mm(dv_ref[...], ga[:KP, :KW]).astype(BF16)

        @pl.when(pl.program_id(0) == 0)
        def _():
            dqg_ref[...] = jnp.zeros_like(dqg_ref)
            dkg_ref[...] = jnp.zeros_like(dkg_ref)

        dqg_ref[...] += dqg
        dkg_ref[...] += dkg

    full = lambda a: pl.BlockSpec(a.shape, lambda i: (0,) * a.ndim)
    tab = pl.BlockSpec((tm, QW), lambda i: (i % nb, 0))
    row = lambda n: pl.BlockSpec((tm, n), lambda i: (i, 0))
    wb = QW + 2 * KW
    assert d_proj.shape[1] % wb == 0
    last = d_proj.shape[1] // wb - 1
    return pl.pallas_call(
        body, name="qkv_prep_bwd", grid=(T // tm,),
        in_specs=[row(wb), full(qg), full(kg), tab, tab, full(cst["bd"]), full(cst["gather"]),
                  row(QW), row(KP), row(KP), ANY],
        out_specs=[pl.BlockSpec((tm, wb), lambda i: (i, last)), pl.BlockSpec((1, QW), lambda i: (0, 0)),
                   pl.BlockSpec((1, KW), lambda i: (0, 0))],
        out_shape=[jax.ShapeDtypeStruct(d_proj.shape, BF16), jax.ShapeDtypeStruct((1, QW), F32),
                   jax.ShapeDtypeStruct((1, KW), F32)],
        input_output_aliases={10: 0}, compiler_params=_cp("arbitrary"),
    )(proj_b, qg, kg, cst["cos"], cst["sgn"], cst["bd"], cst["gather"], dq, dk_pad, dv_pad, d_proj)


ATT_TQ = 256
ATT_TQ_FWD = 512


def attn_fwd(qp, kp, vp, gather, n_ex, mix, job=None):
    T = qp.shape[0]
    S = T // n_ex
    tq = _tile(S, ATT_TQ_FWD)
    nq = S // tq

    def body(q_ref, k_ref, v_ref, ga_ref, _kept, o_ref, op_ref, lse_ref):
        lane = lax.broadcasted_iota(jnp.int32, (tq, LANES), 1)
        lse_all = jnp.zeros((tq, LANES), F32)
        for h in range(N_Q_HEADS):
            kv = h // Q_PER_KV
            qh = q_ref[:, h * LANES:(h + 1) * LANES]
            s = _nt(qh, k_ref[:, kv * LANES:(kv + 1) * LANES])
            m = jnp.max(s, axis=-1, keepdims=True)
            p = jnp.exp(s - m)
            lsum = jnp.sum(p, axis=-1, keepdims=True)
            o = _nn(p.astype(BF16), v_ref[:, kv * LANES:(kv + 1) * LANES]) / lsum
            op_ref[:, h * LANES:(h + 1) * LANES] = o.astype(BF16)
            lse_all = jnp.where(lane == h, m + jnp.log(lsum), lse_all)
        lse_ref[...] = lse_all
        o_ref[...] = _nn(op_ref[...], ga_ref[...]).astype(BF16)

    blk = lambda n: pl.BlockSpec((tq, n), lambda b, i: (b * nq + i, 0))
    kvs = pl.BlockSpec((S, KP), lambda b, i: (b, 0))
    return _call(
        body, "attn_fwd", (n_ex, nq),
        [blk(QP), kvs, kvs, pl.BlockSpec(gather.shape, lambda b, i: (0, 0)), ANY], [qp, kp, vp, gather, mix],
        [pl.BlockSpec((tq, QW), lambda b, i: (b * nq + i, 1)), blk(QP), blk(LANES)],
        [jax.ShapeDtypeStruct(mix.shape, BF16), jax.ShapeDtypeStruct((T, QP), BF16),
         jax.ShapeDtypeStruct((T, LANES), F32)], (), ("parallel", "parallel"), job, {4: 0})


def attn_bwd(qp, kp, vp, op, lse, do, cst, n_ex, job=None):
    T = qp.shape[0]
    S = T // n_ex
    tq = _tile(S, ATT_TQ)
    nq = S // tq

    def body(q_ref, k_ref, v_ref, op_ref, lse_ref, do_ref, sp_ref, ga_ref, dq_ref, dk_ref, dv_ref, dqp_s):
        @pl.when(pl.program_id(1) == 0)
        def _():
            dk_ref[...] = jnp.zeros_like(dk_ref)
            dv_ref[...] = jnp.zeros_like(dv_ref)

        lane = lax.broadcasted_iota(jnp.int32, (tq, LANES), 1)
        dop = _nn(do_ref[...], sp_ref[...]).astype(BF16)
        lse_all = lse_ref[...]
        for h in range(N_Q_HEADS):
            kv = h // Q_PER_KV
            hs = slice(h * LANES, (h + 1) * LANES)
            ks = slice(kv * LANES, (kv + 1) * LANES)
            qh, kk, vv = q_ref[:, hs], k_ref[:, ks], v_ref[:, ks]
            doh = dop[:, hs]
            lse_h = jnp.sum(jnp.where(lane == h, lse_all, 0.0), axis=-1, keepdims=True)
            p = jnp.exp(_nt(qh, kk) - lse_h)
            dp = _nt(doh, vv)
            delta = jnp.sum(doh.astype(F32) * op_ref[:, hs].astype(F32), axis=-1, keepdims=True)
            ds = (p * (dp - delta)).astype(BF16)
            dqp_s[:, hs] = _nn(ds, kk)
            dk_ref[:, ks] += _tn(ds, qh)
            dv_ref[:, ks] += _tn(p.astype(BF16), doh)
        dq_ref[...] = _split_mm(dqp_s[...], ga_ref[...])

    blk = lambda n: pl.BlockSpec((tq, n), lambda b, i: (b * nq + i, 0))
    kvs = pl.BlockSpec((S, KP), lambda b, i: (b, 0))
    full = lambda a: pl.BlockSpec(a.shape, lambda b, i: (0, 0))
    return _call(
        body, "attn_bwd", (n_ex, nq),
        [blk(QP), kvs, kvs, blk(QP), blk(LANES), pl.BlockSpec((tq, QW), lambda b, i: (b * nq + i, 1)),
         full(cst["spread"]), full(cst["gather"])],
        [qp, kp, vp, op, lse, do, cst["spread"], cst["gather"]],
        [blk(QW), kvs, kvs],
        [jax.ShapeDtypeStruct((T, QW), F32), jax.ShapeDtypeStruct((T, KP), F32), jax.ShapeDtypeStruct((T, KP), F32)],
        [pltpu.VMEM((tq, QP), F32)], ("arbitrary", "arbitrary"), job)


def pool_fwd(p, pool_w, scale, n_ex):
    T, W = p.shape
    S = T // n_ex

    def body(p_ref, w_ref, s_ref, o_ref):
        o_ref[...] = _pool_mix(p_ref[...], w_ref[...], s_ref[...]).astype(BF16)

    return pl.pallas_call(
        body, name="pool_fwd", grid=(n_ex,),
        in_specs=[pl.BlockSpec((S, W), lambda b: (b, 0)),
                  pl.BlockSpec(pool_w.shape, lambda b: (0, 0, 0)),
                  pl.BlockSpec((1, W), lambda b: (0, 0))],
        out_specs=pl.BlockSpec((S, W), lambda b: (b, 0)),
        out_shape=jax.ShapeDtypeStruct((T, 2 * W), BF16), compiler_params=_cp("parallel"),
    )(p, pool_w, scale)


def pool_bwd(p, pool_w, scale, n_ex, dy, d_proj):
    T, W = p.shape
    S = T // n_ex
    last = d_proj.shape[1] // W - 1

    def body(p_ref, w_ref, s_ref, dy_ref, _kept, dp_ref, dw_ref, ds_ref):
        _, vjp = jax.vjp(_pool_mix, p_ref[...], w_ref[...], s_ref[...])
        dp, dw, ds = vjp(dy_ref[...].astype(F32))
        dp_ref[...] = dp.astype(BF16)

        @pl.when(pl.program_id(0) == 0)
        def _():
            dw_ref[...] = jnp.zeros_like(dw_ref)
            ds_ref[...] = jnp.zeros_like(ds_ref)

        dw_ref[...] += dw
        ds_ref[...] += ds

    wshape = pool_w.shape
    return pl.pallas_call(
        body, name="pool_bwd", grid=(n_ex,),
        in_specs=[pl.BlockSpec((S, W), lambda b: (b, 0)),
                  pl.BlockSpec(wshape, lambda b: (0, 0, 0)),
                  pl.BlockSpec((1, W), lambda b: (0, 0)), pl.BlockSpec((S, W), lambda b: (b, 0)), ANY],
        out_specs=[pl.BlockSpec((S, W), lambda b: (b, last)), pl.BlockSpec(wshape, lambda b: (0, 0, 0)),
                   pl.BlockSpec((1, W), lambda b: (0, 0))],
        out_shape=[jax.ShapeDtypeStruct(d_proj.shape, BF16), jax.ShapeDtypeStruct(wshape, F32),
                   jax.ShapeDtypeStruct((1, W), F32)],
        input_output_aliases={4: 0}, compiler_params=_cp("arbitrary"),
    )(p, pool_w, scale, dy, d_proj)


SGU_TS = 512


def sgu_fwd(u, v, norm_g, w_s, b_full, mix):
    T, W = u.shape
    ts = _tile(T, SGU_TS)

    def body(u_ref, v_ref, g_ref, w_ref, b_ref, _kept, o_ref):
        o_ref[...] = _sgu(u_ref[...], v_ref[...], g_ref[...], w_ref[...], b_ref[...]).astype(BF16)

    row = pl.BlockSpec((ts, W), lambda i: (i, 0))
    wsp = pl.BlockSpec(w_s.shape, lambda i: (0, 0, 0))
    return pl.pallas_call(
        body, name="sgu_fwd", grid=(T // ts,),
        in_specs=[row, row, pl.BlockSpec((1, W), lambda i: (0, 0)), wsp, wsp, ANY],
        out_specs=pl.BlockSpec((ts, W), lambda i: (i, 1)), out_shape=jax.ShapeDtypeStruct(mix.shape, BF16),
        input_output_aliases={5: 0}, compiler_params=_cp("parallel"),
    )(u, v, norm_g, w_s, b_full, mix)


def sgu_bwd(u, v, norm_g, w_s, b_full, dy):
    T, W = u.shape
    ts = _tile(T, SGU_TS)
    wshape = w_s.shape

    def body(u_ref, v_ref, g_ref, w_ref, b_ref, dy_ref, duv_ref, dg_ref, dw_ref, db_ref):
        _, vjp = jax.vjp(_sgu, u_ref[...], v_ref[...], g_ref[...], w_ref[...], b_ref[...])
        du, dv, dg, dw, db = vjp(dy_ref[...].astype(F32))
        duv_ref[:, :W] = du.astype(BF16)
        duv_ref[:, W:] = dv.astype(BF16)

        @pl.when(pl.program_id(0) == 0)
        def _():
            dg_ref[...] = jnp.zeros_like(dg_ref)
            dw_ref[...] = jnp.zeros_like(dw_ref)
            db_ref[...] = jnp.zeros_like(db_ref)

        dg_ref[...] += dg
        dw_ref[...] += dw
        db_ref[...] += db

    row = pl.BlockSpec((ts, W), lambda i: (i, 0))
    wsp = pl.BlockSpec(wshape, lambda i: (0, 0, 0))
    wout = pl.BlockSpec(wshape, lambda i: (0, 0, 0))
    vec = pl.BlockSpec((1, W), lambda i: (0, 0))
    return pl.pallas_call(
        body, name="sgu_bwd", grid=(T // ts,),
        in_specs=[row, row, vec, wsp, wsp, pl.BlockSpec((ts, W), lambda i: (i, 1))],
        out_specs=[pl.BlockSpec((ts, 2 * W), lambda i: (i, 0)), vec, wout, wout],
        out_shape=[jax.ShapeDtypeStruct((T, 3 * W), BF16),
                   jax.ShapeDtypeStruct((1, W), F32), jax.ShapeDtypeStruct(wshape, F32),
                   jax.ShapeDtypeStruct(wshape, F32)],
        compiler_params=_cp("arbitrary"),
    )(u, v, norm_g, w_s, b_full, dy)


def loss_head(x, gain, target):
    T, D = x.shape
    tm = _tile(T, 512)

    def body(x_ref, g_ref, t_ref, loss_ref, dx_ref, dg_ref):
        xv, g = x_ref[...], g_ref[...]
        r = lax.rsqrt(jnp.mean(xv * xv, axis=-1, keepdims=True) + EPS)
        err = xv * r * g - t_ref[...]
        part = 0.5 * jnp.sum(jnp.mean(err * err, axis=-1, keepdims=True), axis=0, keepdims=True)
        dx, dg = _rms_bwd_math(xv, g, err * (1.0 / D), jnp.zeros_like(xv))
        dx_ref[...] = dx

        @pl.when(pl.program_id(0) == 0)
        def _():
            loss_ref[...] = jnp.zeros_like(loss_ref)
            dg_ref[...] = jnp.zeros_like(dg_ref)

        loss_ref[...] += part
        dg_ref[...] += dg

    row = pl.BlockSpec((tm, D), lambda i: (i, 0))
    vec = pl.BlockSpec((1, D), lambda i: (0, 0))
    return pl.pallas_call(
        body, name="loss_head", grid=(T // tm,),
        in_specs=[row, vec, row], out_specs=[pl.BlockSpec((1, 1), lambda i: (0, 0)), row, vec],
        out_shape=[jax.ShapeDtypeStruct((1, 1), F32), jax.ShapeDtypeStruct((T, D), F32),
                   jax.ShapeDtypeStruct((1, D), F32)],
        compiler_params=_cp("arbitrary"),
    )(x, gain, target)


class MultiJob:
    def __init__(self, jobs):
        self.jobs = jobs
        self.args = [a for j in jobs for a in j.args]
        self.out_shape = [s for j in jobs for s in j.out_shape]
        self.scratch = [s for j in jobs for s in j.scratch]
        self.n_in, self.n_out = len(self.args), len(self.out_shape)

    def _each(self, ins, outs, sems):
        i = o = s = 0
        for j in self.jobs:
            yield j, ins[i:i + j.n_in], outs[o:o + j.n_out], sems[s:s + len(j.scratch)]
            i, o, s = i + j.n_in, o + j.n_out, s + len(j.scratch)

    def start(self, ins, outs, sems):
        for j, a, b, c in self._each(ins, outs, sems):
            j.start(a, b, c)

    def middle(self, ins, outs, sems):
        for j, a, b, c in self._each(ins, outs, sems):
            j.middle(a, b, c)

    def finish(self, ins, outs, sems):
        for j, a, b, c in self._each(ins, outs, sems):
            j.finish(a, b, c)

    def split(self, results):
        o = 0
        for j in self.jobs:
            yield results[o:o + j.n_out]
            o += j.n_out


class Plan:
    def __init__(self, shard, gathers, small_carrier=None, pack_small=None, exchange=True):
        self.shard, self.gathers, self.exchange = shard, gathers, exchange
        self.small_carrier, self.pack_small = small_carrier, pack_small
        self.weights, self.grads, self.started = {}, {}, []
        self.small_src = self.small_parts = None

    def scatter(self, keys):
        if self.exchange:
            tag = "_".join(f"{kind}{l}" for kind, l in keys)
            self.started.append((keys,) + tuple(scatter_start([self.grads[k] for k in keys], tag)) + (tag,))

    def collect(self, groups, after):
        got = {}
        for keys, send, recv, grads, zones, _, tag in groups:
            grads, zones = scatter_wait(send, recv, grads, zones, after, tag)
            got.update({k: (g, z) for k, g, z in zip(keys, grads, zones)})
        return got

    def weight(self, kind, l):
        return self.weights[(kind, l)]

    def grad(self, kind, l, g):
        self.grads[(kind, l)] = g

    def small_ready(self, small, d_final):
        if self.pack_small is not None:
            self.small_src = self.pack_small(small, d_final)

    def _jobs(self, key):
        jobs = []
        if key in self.gathers:
            ks = self.gathers[key]
            jobs.append((GatherJob([self.shard(*k) for k in ks]), self.weights, ks))
        if key == self.small_carrier and self.small_src is not None:
            jobs.append((GatherJob([(self.small_src, None)]), None, None))
        return jobs

    def _deliver(self, jobs, results):
        multi = MultiJob([j for j, _, _ in jobs])
        for (_, store, ks), res in zip(jobs, multi.split(results)):
            if store is None:
                self.small_parts = res[0]
            else:
                store.update(zip(ks, res))

    def run(self, key, fn, *args, **kw):
        jobs = self._jobs(key)
        if not jobs:
            out = fn(*args, **kw)
            return out if fn is mm_tn else out[0]
        res, jres = fn(*args, job=MultiJob([j for j, _, _ in jobs]), **kw)
        self._deliver(jobs, jres)
        return res

    def alone(self, key, name):
        jobs = self._jobs(key)
        if jobs:
            self._deliver(jobs, run_job(MultiJob([j for j, _, _ in jobs]), name))


def _local_step(x, target, layers, final_norm, n_ex, plan):
    T, D = x.shape
    L = len(layers)
    cst = _attn_consts(T // n_ex)
    ident = lambda j: j
    EV_A, EV_B = 3 * (D // 2), QW + 2 * KW
    OD_W = D // 2
    wt = plan.weight

    saved = []
    for l, W in enumerate(layers):
        s = dict(x0=x)
        x1, *s["gu1"] = plan.run(("ffn1_fwd", l), ffn_fwd, x, W["n1"], wt("f1_in_t", l), wt("f1_out", l))
        if l % 2 == 0:
            pa, pb, h = mm_nt(x1, wt("mx_in_t", l), [(0, EV_A), (EV_A, EV_B)], F32, True, W["nm"])
            qg = jnp.tile(W["q_norm"], N_Q_HEADS)[None]
            kg = jnp.tile(W["k_norm"], N_KV_HEADS)[None]
            mix = conv_fwd(pa, W["conv_w"], n_ex)
            qp, kp, vp = qkv_prep_fwd(pb, qg, kg, cst, n_ex)
            mix, op, lse = plan.run(("attn_fwd", l), attn_fwd, qp, kp, vp, cst["gather"], n_ex, mix)
            s.update(pa=pa, pb=pb, qg=qg, kg=kg, qp=qp, kp=kp, vp=vp, op=op, lse=lse)
        else:
            p, u, v, h = mm_nt(x1, wt("mx_in_t", l), [(0, OD_W), (OD_W, OD_W), (2 * OD_W, OD_W)], F32, True, W["nm"])
            scale = W["pool_scale"][None]
            sn = W["sgu_norm"][None]
            b_full = jnp.broadcast_to(W["sgu_b"][..., None], W["sgu_w"].shape)
            mix = sgu_fwd(u, v, sn, W["sgu_w"], b_full, pool_fwd(p, W["pool_w"], scale, n_ex))
            s.update(p=p, u=u, v=v, scale=scale, sn=sn, b_full=b_full)
        x2 = mm_nn([(mix, 0, D, 0)], wt("mx_out", l), residual=x1)
        x3, *s["gu2"] = plan.run(("ffn2_fwd", l), ffn_fwd, x2, W["n2"], wt("f2_in_t", l), wt("f2_out", l))
        s.update(x1=x1, x2=x2, h=h, mix=mix)
        saved.append(s)
        x = x3

    loss, dx, d_final = loss_head(x, final_norm, target)

    small = [None] * L

    def ffn_back(which, l, dout, xin, gain, gu_xn, sm, sm_key):
        w_in, w_out = wt(which + "_in_t", l), wt(which + "_out", l)
        F = w_out.shape[0]
        nc = F // FFN_TN
        gu, xn = gu_xn
        dxi, sm[sm_key], a, dgu, dob = plan.run((which + "_bwd", l), ffn_bwd_x, dout, xin, gain, gu, w_in, w_out)
        if which == "f1" and l == 0:
            plan.small_ready(small, d_final)
        plan.grad(which + "_in_t", l, plan.run(
            (which + "_in_grad", l), mm_tn, dgu, xn, 2 * F, lambda k, c: k * nc + c, grid=(2, nc),
            col_block=lambda k, c: 2 * c + k))
        plan.grad(which + "_out", l, plan.run((which + "_out_grad", l), mm_tn, a, dob, F, ident))
        plan.scatter([(which + "_in_t", l), (which + "_out", l)])
        return dxi

    for l in reversed(range(L)):
        s, W = saved[l], layers[l]
        sm = small[l] = {}
        dx = ffn_back("f2", l, dx, s["x2"], W["n2"], s["gu2"], sm, "n2")
        dmix, dxb = mm_nt(dx, wt("mx_out", l), [(0, D)], BF16, emit_a_bf16=True)
        plan.grad("mx_out", l, mm_tn(s["mix"], dxb, D, ident))
        if l % 2 == 0:
            d_proj, sm["conv_w"] = conv_bwd(s["pa"], W["conv_w"], n_ex, dmix, EV_A + EV_B)
            dq, dkp, dvp = plan.run(("attn_bwd", l), attn_bwd, s["qp"], s["kp"], s["vp"], s["op"], s["lse"], dmix, cst, n_ex)
            d_proj, dqg, dkg = qkv_prep_bwd(s["pb"], s["qg"], s["kg"], cst, n_ex, dq, dkp, dvp, d_proj)
            d_pieces = [(d_proj, 0, EV_A + EV_B, 0)]
            plan.grad("mx_in_t", l, mm_tn(d_proj, s["h"], EV_A + EV_B, ident))
            sm["q_norm"] = dqg.reshape(N_Q_HEADS, HEAD_DIM).sum(0)
            sm["k_norm"] = dkg.reshape(N_KV_HEADS, HEAD_DIM).sum(0)
        else:
            d_proj, d_sn, sm["sgu_w"], d_sb = sgu_bwd(s["u"], s["v"], s["sn"], W["sgu_w"], s["b_full"], dmix)
            d_proj, sm["pool_w"], d_ps = pool_bwd(s["p"], W["pool_w"], s["scale"], n_ex, dmix, d_proj)
            d_pieces = [(d_proj, 0, OD_W, 1), (d_proj, 1, OD_W, 2), (d_proj, 2, OD_W, 0)]
            nb = OD_W // MM_TC
            plan.grad("mx_in_t", l, mm_tn(d_proj, s["h"], 3 * OD_W,
                                          lambda jj: jnp.where(jj < 2 * nb, jj + nb, jj - 2 * nb)))
            sm["pool_scale"], sm["sgu_norm"], sm["sgu_b"] = d_ps[0], d_sn[0], d_sb.sum(-1)
        plan.scatter([("mx_out", l), ("mx_in_t", l)])
        dx, sm["nm"] = mm_nn(d_pieces, wt("mx_in_t", l), norm_bwd=(s["x1"], W["nm"], dx))
        dx = ffn_back("f1", l, dx, s["x0"], W["n1"], s["gu1"], sm, "n1")
    return loss, dx


def all_gather(srcs):
    return run_job(GatherJob(srcs), "all_gather")


HBM_SPEC = pl.BlockSpec(memory_space=pltpu.HBM)
SEM_SPEC = pl.BlockSpec(memory_space=pltpu.SEMAPHORE)
SPLIT_COPY = pltpu.CompilerParams(has_side_effects=pltpu.SideEffectType.DATAFLOW_SIDE_EFFECTING)


def _scatter_copies(srcs, lands, send, recv, dims):
    x, y, c, me = _my_place()
    for t, (r, _) in enumerate(dims):
        for k in range(1, N_DEV):
            peer, pidx = _peer(x, y, c, k)
            rows = srcs[t].at[pl.ds(pl.multiple_of(pidx * r, 8), r), :]
            yield (_remote(rows, lands[t].at[me], send, recv, N_DEV * t + k, peer),
                   _remote(rows, lands[t].at[pidx], send, recv, N_DEV * t + k, peer))


def scatter_start(grads, tag):
    n = len(grads)
    dims = [(g.shape[0] // N_DEV, g.shape[1]) for g in grads]

    def body(*refs):
        srcs, lands, send, recv, token = refs[:n], refs[n:2 * n], refs[2 * n], refs[2 * n + 1], refs[-1]
        for mine, _ in _scatter_copies(srcs, lands, send, recv, dims):
            mine.start()
        token[...] = jnp.zeros_like(token)

    zones = [lax.empty((N_DEV, r, cc), g.dtype) for g, (r, cc) in zip(grads, dims)]
    res = pl.pallas_call(
        body, name="scatter_start_" + tag,
        out_shape=(pltpu.SemaphoreType.DMA((N_DEV * n,)), pltpu.SemaphoreType.DMA((N_DEV * n,)),
                   *[pltpu.HBM(g.shape, g.dtype) for g in grads], *[pltpu.HBM(z.shape, z.dtype) for z in zones],
                   jax.ShapeDtypeStruct((8, LANES), F32)),
        in_specs=[HBM_SPEC] * (2 * n),
        out_specs=(SEM_SPEC, SEM_SPEC, *([HBM_SPEC] * (2 * n)), pl.BlockSpec(memory_space=pltpu.VMEM)),
        input_output_aliases={i: 2 + i for i in range(2 * n)}, compiler_params=SPLIT_COPY,
    )(*[pltpu.with_memory_space_constraint(a, pltpu.HBM) for a in list(grads) + zones])
    return res[0], res[1], res[2:2 + n], res[2 + n:2 + 2 * n], res[-1]


def scatter_wait(send, recv, grads, zones, after, tag):
    n = len(grads)
    dims = [(g.shape[0] // N_DEV, g.shape[1]) for g in grads]

    def body(*refs):
        srcs, lands, send_ref, recv_ref = refs[:n], refs[n:2 * n], refs[2 * n], refs[2 * n + 1]
        for mine, theirs in _scatter_copies(srcs, lands, send_ref, recv_ref, dims):
            mine.wait_send()
            theirs.wait_recv()

    res = pl.pallas_call(
        body, name="scatter_wait_" + tag,
        out_shape=(*[pltpu.HBM(g.shape, g.dtype) for g in grads], *[pltpu.HBM(z.shape, z.dtype) for z in zones]),
        in_specs=[HBM_SPEC] * (2 * n) + [SEM_SPEC, SEM_SPEC, ANY], out_specs=[HBM_SPEC] * (2 * n),
        input_output_aliases={i: i for i in range(2 * n)}, compiler_params=SPLIT_COPY,
    )(*grads, *zones, send, recv, after)
    return res[:n], res[n:]


def cast_shards(w):
    L, A, B = w.shape

    def body(w_ref, o_ref):
        o_ref[...] = w_ref[...].astype(BF16)

    return pl.pallas_call(
        body, name="cast_shards", grid=(L,),
        in_specs=[pl.BlockSpec((None, A, B), lambda l: (l, 0, 0))],
        out_specs=pl.BlockSpec((None, A, B), lambda l: (l, 0, 0)),
        out_shape=jax.ShapeDtypeStruct((L, A, B), BF16), compiler_params=_cp("parallel"),
    )(w)


ADAM_TC = 256


def adamw(parts, w, m, v, l, prev=None, own=None, after=None):
    P, R, C = parts.shape
    tc = _tile(C, ADAM_TC)
    c1, c2 = 1.0 - ADAM_B1 ** ADAM_STEP, 1.0 - ADAM_B2 ** ADAM_STEP
    prev = list(prev) if prev is not None else []

    def body(*refs):
        me_ref, refs = (refs[0], refs[1:]) if own is not None else (None, refs)
        p_ref, w_ref, m_ref, v_ref = refs[:4]
        g_ref, d_ref, mo_ref, vo_ref = refs[-4:]
        if own is None:
            term = lambda s: p_ref[s].astype(F32)
        else:
            term = lambda s: jnp.where(me_ref[0] == s, refs[4][...], p_ref[s]).astype(F32)
        g = term(0)
        for s in range(1, P):
            g = g + term(s)
        m1 = ADAM_B1 * m_ref[...] + (1.0 - ADAM_B1) * g
        v1 = ADAM_B2 * v_ref[...] + (1.0 - ADAM_B2) * (g * g)
        g_ref[...] = g
        mo_ref[...] = m1
        vo_ref[...] = v1
        d_ref[...] = -ADAM_LR * ((m1 / c1) / (jnp.sqrt(v1 / c2) + ADAM_EPS) + ADAM_WD * w_ref[...])

    wspec = pl.BlockSpec((None, R, tc), lambda i, *_: (l, 0, i))
    pspec = pl.BlockSpec((P, R, tc), lambda i, *_: (0, 0, i))
    extra = prev + ([] if after is None else [after])
    out_shape = [jax.ShapeDtypeStruct(w.shape, F32)] * 4
    if own is None:
        return pl.pallas_call(
            body, name="adamw", grid=(C // tc,), in_specs=[pspec, wspec, wspec, wspec] + [ANY] * len(extra),
            out_specs=[wspec] * 4, out_shape=out_shape,
            input_output_aliases={4 + i: i for i in range(len(prev))}, compiler_params=_cp("parallel"),
        )(parts, w, m, v, *extra)
    own_sums, me = own
    ospec = pl.BlockSpec((None, R, tc), lambda i, me_ref: (me_ref[0], 0, i))
    return pl.pallas_call(
        body, name="adamw_own", out_shape=out_shape,
        grid_spec=pltpu.PrefetchScalarGridSpec(
            num_scalar_prefetch=1, grid=(C // tc,),
            in_specs=[pspec, wspec, wspec, wspec, ospec] + [ANY] * len(extra), out_specs=[wspec] * 4),
        input_output_aliases={6 + i: i for i in range(len(prev))}, compiler_params=_cp("parallel"),
    )(me, parts, w, m, v, own_sums, *extra)


_WEIGHTS = ['ffn1_norm', 'ffn1_w_in', 'ffn1_w_out', 'mix_norm', 'ffn2_norm', 'ffn2_w_in', 'ffn2_w_out', 'ev_w_in',
            'ev_conv_w', 'ev_q_norm', 'ev_k_norm', 'ev_w_out', 'od_w_in', 'od_pool_w', 'od_pool_scale', 'od_sgu_norm',
            'od_sgu_w', 'od_sgu_b', 'od_w_out', 'final_norm']
_BIG = dict(ffn1_w_in=True, ffn1_w_out=False, ffn2_w_in=True, ffn2_w_out=False,
            ev_w_in=True, ev_w_out=False, od_w_in=True, od_w_out=False)
_SMALL_SHARDED = ['ev_conv_w', 'od_pool_scale', 'od_sgu_norm']
_SMALL = [n for n in _WEIGHTS if n not in _BIG]
_PACK_ROWS = 8 * LANES


_KINDS = ("f1_in_t", "f1_out", "mx_in_t", "mx_out", "f2_in_t", "f2_out")
_CARRIER_US = dict(ffn1_fwd=105, ffn2_fwd=105, attn_fwd=115)
_GATHER_US_PER_ROW = 0.08
SMALL_CARRIER = ("f1_in_grad", 0)


def _schedule(L, rows):
    events = []
    for l in range(L):
        events += [("ffn1_fwd", l), ("mixer", l)] + ([("attn_fwd", l)] if l % 2 == 0 else []) + [("ffn2_fwd", l)]
    consumer = {"f1": "ffn1_fwd", "mx": "mixer", "f2": "ffn2_fwd"}
    queue = [(k, l) for l in range(L) for k in _KINDS]
    pos = {t: events.index((consumer[t[0][:2]], t[1])) for t in queue}
    gathers = {"first": [t for t in queue if pos[t] == 0]}
    queue = [t for t in queue if pos[t] > 0]
    carriers = [i for i, e in enumerate(events) if e[0] in _CARRIER_US]
    for i in carriers:
        budget, take = _CARRIER_US[events[i][0]], []
        later = [j for j in carriers if j > i]
        while queue:
            t = queue[0]
            cost = rows(*t) * _GATHER_US_PER_ROW
            forced = not any(j < pos[t] for j in later)
            if not forced and cost > budget:
                break
            take.append(queue.pop(0))
            budget -= cost
        if take:
            gathers[events[i]] = take
    assert not queue
    return gathers


def _pack(arrs):
    flat = jnp.concatenate([a.reshape(-1) for a in arrs])
    pad = (-flat.shape[0]) % _PACK_ROWS
    return jnp.pad(flat, (0, pad)).reshape(-1, LANES)


def _unpack(buf, shapes):
    flat, out, off = buf.reshape(-1), [], 0
    for s in shapes:
        n = math.prod(s)
        out.append(flat[off:off + n].reshape(s))
        off += n
    return out


def _unshard_last(g, lead):
    nd = len(lead)
    return jnp.moveaxis(g, 0, nd).reshape(*lead, -1)


def kernel(x, ffn1_norm, ffn1_w_in, ffn1_w_out, mix_norm, ffn2_norm, ffn2_w_in, ffn2_w_out, ev_w_in, ev_conv_w, ev_q_norm, ev_k_norm, ev_w_out, od_w_in, od_pool_w, od_pool_scale, od_sgu_norm, od_sgu_w, od_sgu_b, od_w_out, final_norm, loss_target, m_ffn1_norm, m_ffn1_w_in, m_ffn1_w_out, m_mix_norm, m_ffn2_norm, m_ffn2_w_in, m_ffn2_w_out, m_ev_w_in, m_ev_conv_w, m_ev_q_norm, m_ev_k_norm, m_ev_w_out, m_od_w_in, m_od_pool_w, m_od_pool_scale, m_od_sgu_norm, m_od_sgu_w, m_od_sgu_b, m_od_w_out, m_final_norm, v_ffn1_norm, v_ffn1_w_in, v_ffn1_w_out, v_mix_norm, v_ffn2_norm, v_ffn2_w_in, v_ffn2_w_out, v_ev_w_in, v_ev_conv_w, v_ev_q_norm, v_ev_k_norm, v_ev_w_out, v_od_w_in, v_od_pool_w, v_od_pool_scale, v_od_sgu_norm, v_od_sgu_w, v_od_sgu_b, v_od_w_out, v_final_norm):
    w = dict(zip(_WEIGHTS, (ffn1_norm, ffn1_w_in, ffn1_w_out, mix_norm, ffn2_norm, ffn2_w_in, ffn2_w_out, ev_w_in, ev_conv_w, ev_q_norm, ev_k_norm, ev_w_out, od_w_in, od_pool_w, od_pool_scale, od_sgu_norm, od_sgu_w, od_sgu_b, od_w_out, final_norm)))
    m = dict(zip(_WEIGHTS, (m_ffn1_norm, m_ffn1_w_in, m_ffn1_w_out, m_mix_norm, m_ffn2_norm, m_ffn2_w_in, m_ffn2_w_out, m_ev_w_in, m_ev_conv_w, m_ev_q_norm, m_ev_k_norm, m_ev_w_out, m_od_w_in, m_od_pool_w, m_od_pool_scale, m_od_sgu_norm, m_od_sgu_w, m_od_sgu_b, m_od_w_out, m_final_norm)))
    v = dict(zip(_WEIGHTS, (v_ffn1_norm, v_ffn1_w_in, v_ffn1_w_out, v_mix_norm, v_ffn2_norm, v_ffn2_w_in, v_ffn2_w_out, v_ev_w_in, v_ev_conv_w, v_ev_q_norm, v_ev_k_norm, v_ev_w_out, v_od_w_in, v_od_pool_w, v_od_pool_scale, v_od_sgu_norm, v_od_sgu_w, v_od_sgu_b, v_od_w_out, v_final_norm)))
    n_ex, seq, D = x.shape
    T = n_ex * seq
    L = ffn1_norm.shape[0]
    me = 4 * lax.axis_index("x") + 2 * lax.axis_index("y") + lax.axis_index("c")

    sh_small = [w[n] for n in _SMALL_SHARDED]
    packed = all_gather([(_pack(sh_small), None)])[0].reshape(N_DEV, -1)
    full_small = {}
    off = 0
    for n, a in zip(_SMALL_SHARDED, sh_small):
        cnt = math.prod(a.shape)
        full_small[n] = _unshard_last(packed[:, off:off + cnt].reshape((N_DEV,) + a.shape), a.shape[:-1])
        off += cnt

    tr = lambda a: jnp.swapaxes(a, 1, 2)
    wmv = {n: tuple(tr(d[n]) if t else d[n] for d in (w, m, v)) for n, t in _BIG.items()}
    shards = {n: cast_shards(wmv[n][0]) for n in _BIG}

    def name_of(kind, l):
        mx = "ev" if l % 2 == 0 else "od"
        return {"f1_in_t": "ffn1_w_in", "f1_out": "ffn1_w_out", "f2_in_t": "ffn2_w_in", "f2_out": "ffn2_w_out",
                "mx_in_t": mx + "_w_in", "mx_out": mx + "_w_out"}[kind], (l // 2 if kind.startswith("mx") else l)

    def shard(kind, l):
        name, idx = name_of(kind, l)
        return shards[name], idx

    g_shapes = {}

    def pack_small(small, d_final):
        ev = [sm for l, sm in enumerate(small) if l % 2 == 0]
        od = [sm for l, sm in enumerate(small) if l % 2 == 1]
        st = lambda sms, k: jnp.stack([sm[k] for sm in sms])
        g_full = dict(ffn1_norm=st(small, "n1")[:, 0], mix_norm=st(small, "nm")[:, 0], ffn2_norm=st(small, "n2")[:, 0],
                      ev_conv_w=st(ev, "conv_w"), ev_q_norm=st(ev, "q_norm"), ev_k_norm=st(ev, "k_norm"),
                      od_pool_w=st(od, "pool_w"), od_pool_scale=st(od, "pool_scale"), od_sgu_norm=st(od, "sgu_norm"),
                      od_sgu_w=st(od, "sgu_w"), od_sgu_b=st(od, "sgu_b"), final_norm=d_final[0])
        g_shapes.update({n: g_full[n].shape for n in _SMALL})
        return _pack([g_full[n] for n in _SMALL])

    gathers = _schedule(L, lambda kind, l: shards[name_of(kind, l)[0]].shape[1])
    plan = Plan(shard, gathers, SMALL_CARRIER, pack_small)
    layers = []
    for l in range(L):
        j = l // 2
        W = dict(n1=ffn1_norm[l][None], nm=mix_norm[l][None], n2=ffn2_norm[l][None])
        if l % 2 == 0:
            W.update(conv_w=full_small["ev_conv_w"][j], q_norm=ev_q_norm[j], k_norm=ev_k_norm[j])
        else:
            W.update(pool_w=od_pool_w[j], pool_scale=full_small["od_pool_scale"][j], sgu_norm=full_small["od_sgu_norm"][j],
                     sgu_w=od_sgu_w[j], sgu_b=od_sgu_b[j])
        layers.append(W)

    plan.alone("first", "gather_first")
    loss, dx = _local_step(x.reshape(T, D), loss_target.reshape(T, D), layers, final_norm[None], n_ex, plan)

    out = {n: None for n in _BIG}
    me1 = me.astype(jnp.int32).reshape(1)

    def update(arrived):
        for (kind, l), (own, parts) in arrived.items():
            name, idx = name_of(kind, l)
            out[name] = adamw(parts, *wmv[name], idx, prev=out[name], own=(own.reshape(parts.shape), me1))

    *earlier, last = plan.started
    update(plan.collect(earlier, last[5]))

    g8 = plan.small_parts.reshape(N_DEV, -1)
    cols, off = [], 0
    for n in _SMALL:
        cnt = math.prod(g_shapes[n])
        g = g8[:, off:off + cnt].reshape((N_DEV,) + g_shapes[n])
        off += cnt
        if n in _SMALL_SHARDED:
            width = w[n].shape[-1]
            g = lax.dynamic_slice_in_dim(g, me * width, width, axis=g.ndim - 1)
        cols.append(g.reshape(N_DEV, -1))
    g8 = jnp.concatenate(cols, axis=1)
    g8 = jnp.pad(g8, ((0, 0), (0, (-g8.shape[1]) % _PACK_ROWS))).reshape(N_DEV, -1, LANES)
    pk = lambda d: _pack([d[n] for n in _SMALL])[None]
    small_out = adamw(g8, pk(w), pk(m), pk(v), 0)

    update(plan.collect([last], small_out[0]))
    out = {n: [tr(a) if _BIG[n] else a for a in res] for n, res in out.items()}

    shapes = [w[n].shape for n in _SMALL]
    for i in range(4):
        for n, a in zip(_SMALL, _unpack(small_out[i], shapes)):
            out.setdefault(n, [None] * 4)[i] = a

    total = lax.psum(loss[0, 0], ("x", "y", "c"))
    return (total, dx.reshape(n_ex, seq, D), *[out[n][0] for n in _WEIGHTS], *[out[n][1] for n in _WEIGHTS],
            *[out[n][2] for n in _WEIGHTS], *[out[n][3] for n in _WEIGHTS])
```

```python
import functools
import math

import jax
import jax.numpy as jnp
from jax import lax
from jax.experimental import pallas as pl
from jax.experimental.pallas import tpu as pltpu

F32, BF16 = jnp.float32, jnp.bfloat16
EPS = 1e-6
N_DEV = 8
V7X_VMEM_BYTES = 64 * 1024 * 1024
VMEM_LIMIT = V7X_VMEM_BYTES - 8 * 1024 * 1024
LANES = 128
HEAD_DIM = 64
N_Q_HEADS = 8
N_KV_HEADS = 2
Q_PER_KV = N_Q_HEADS // N_KV_HEADS
GRID_W = 64
ROPE_THETA = 10000.0
POOL_RADII = (1, 2, 4, 8)
SGU_CHUNK = 128
GROUP = 128
ADAM_LR, ADAM_B1, ADAM_B2, ADAM_EPS, ADAM_WD, ADAM_STEP = 0.001, 0.9, 0.999, 1e-08, 0.01, 10
MESH_ID = pl.DeviceIdType.MESH


def _cp(*sem):
    return pltpu.CompilerParams(dimension_semantics=sem, vmem_limit_bytes=VMEM_LIMIT)


def _dot(a, b, ca, cb):
    return lax.dot_general(a, b, (((ca,), (cb,)), ((), ())), preferred_element_type=F32)


def _nn(a, b):
    return _dot(a, b, 1, 0)


def _nt(a, b):
    return _dot(a, b, 1, 1)


def _tn(a, b):
    return _dot(a, b, 0, 0)


def _split_mm(x, m):
    hi = x.astype(BF16)
    lo = (x - hi.astype(F32)).astype(BF16)
    return _nn(hi, m) + _nn(lo, m)


def _tile(n, pref):
    t = min(n, pref)
    assert n % t == 0, (n, pref)
    return t


ANY = pl.BlockSpec(memory_space=pl.ANY)
JOB_MIDDLE = 0.55


def _my_place():
    x, y, c = lax.axis_index("x"), lax.axis_index("y"), lax.axis_index("c")
    return x, y, c, 4 * x + 2 * y + c


def _peer(x, y, c, k):
    px = 1 - x if k & 4 else x
    py = 1 - y if k & 2 else y
    pc = 1 - c if k & 1 else c
    return (px, py, pc), 4 * px + 2 * py + pc


def _remote(src, dst, send_sems, recv_sems, i, peer):
    return pltpu.make_async_remote_copy(src_ref=src, dst_ref=dst, send_sem=send_sems.at[i], recv_sem=recv_sems.at[i],
                                        device_id=peer, device_id_type=MESH_ID)


class GatherJob:
    def __init__(self, srcs):
        self.srcs = srcs
        self.args = [a for a, _ in srcs]
        self.dims = [a.shape[-2:] for a, _ in srcs]
        n = self.n_in = self.n_out = len(srcs)
        self.out_shape = [jax.ShapeDtypeStruct((N_DEV * r, cc), a.dtype) for (a, _), (r, cc) in zip(srcs, self.dims)]
        self.scratch = [pltpu.SemaphoreType.DMA((N_DEV * n,)), pltpu.SemaphoreType.DMA((N_DEV * n,)),
                        pltpu.SemaphoreType.DMA((n,))]

    def _rows(self, outs, t, idx):
        r = self.dims[t][0]
        return outs[t].at[pl.ds(pl.multiple_of(idx * r, 8), r), :]

    def _local(self, ins, outs, loc, t, me):
        src = ins[t] if self.srcs[t][1] is None else ins[t].at[self.srcs[t][1]]
        return src, pltpu.make_async_copy(src, self._rows(outs, t, me), loc.at[t])

    def start(self, ins, outs, sems):
        send, recv, loc = sems
        x, y, c, me = _my_place()
        for t in range(self.n_in):
            src, local = self._local(ins, outs, loc, t, me)
            local.start()
            for k in (2, 4, 1):
                _remote(src, self._rows(outs, t, me), send, recv, N_DEV * t + k, _peer(x, y, c, k)[0]).start()

    def _copy(self, outs, sems, t, origin, i, to):
        x, y, c, _ = _my_place()
        blk = self._rows(outs, t, _peer(x, y, c, origin)[1])
        return _remote(blk, blk, sems[0], sems[1], N_DEV * t + i, _peer(x, y, c, to)[0])

    def middle(self, ins, outs, sems):
        c = _my_place()[2]

        def relay(t, got, to):
            self._copy(outs, sems, t, got, got, got).wait_recv()
            self._copy(outs, sems, t, got, 6, to).start()
            self._copy(outs, sems, t, to, to, to).wait_recv()

        for t in range(self.n_in):
            pl.when(c == 1)(functools.partial(relay, t, 2, 4))
            pl.when(c == 0)(functools.partial(relay, t, 4, 2))
            for k in (2, 4):
                self._copy(outs, sems, t, k, k + 1, 1).start()

    def finish(self, ins, outs, sems):
        send, recv, loc = sems
        x, y, c, me = _my_place()
        for t in range(self.n_in):
            self._copy(outs, sems, t, 6, 6, 6).wait_recv()
            self._copy(outs, sems, t, 6, 7, 1).start()
        for t in range(self.n_in):
            for k in range(1, N_DEV):
                peer, pidx = _peer(x, y, c, k)
                blk = self._rows(outs, t, pidx)
                if k % 2 == 1:
                    _remote(blk, blk, send, recv, N_DEV * t + k, peer).wait_recv()
                _remote(blk, blk, send, recv, N_DEV * t + k, peer).wait_send()
            self._local(ins, outs, loc, t, me)[1].wait()


def _call(body, name, grid, in_specs, args, out_specs, out_shape, scratch=(), sem=(), job=None, aliases=None):
    in_specs, out_specs, out_shape, scratch = list(in_specs), list(out_specs), list(out_shape), list(scratch)
    n_in, n_out, n_scr = len(args), len(out_shape), len(scratch)
    if job is None:
        res = pl.pallas_call(body, name=name, grid=grid, in_specs=in_specs, out_specs=out_specs, out_shape=out_shape,
                             scratch_shapes=scratch, input_output_aliases=aliases or {}, compiler_params=_cp(*sem))(*args)
        return res, None
    o0 = n_in + job.n_in
    s0 = o0 + n_out + job.n_out

    def carrier(*refs):
        jin, jout, jsem = refs[n_in:o0], refs[o0 + n_out:s0], refs[s0 + n_scr:]
        ids = [pl.program_id(a) for a in range(len(grid))]
        def at(step):
            idx = []
            for g in reversed(grid):
                idx.append(step % g)
                step //= g
            return functools.reduce(jnp.logical_and, [i == j for i, j in zip(ids, reversed(idx))])

        steps = math.prod(grid)
        if grid:
            pl.when(at(0))(lambda: job.start(jin, jout, jsem))
            pl.when(at(int(steps * JOB_MIDDLE)))(lambda: job.middle(jin, jout, jsem))
        else:
            job.start(jin, jout, jsem)
            job.middle(jin, jout, jsem)
        body(*refs[:n_in], *refs[o0:o0 + n_out], *refs[s0:s0 + n_scr])
        if grid:
            pl.when(at(steps - 1))(lambda: job.finish(jin, jout, jsem))
        else:
            job.finish(jin, jout, jsem)

    res = pl.pallas_call(
        carrier, name=name + "_comm", grid=grid, in_specs=in_specs + [ANY] * job.n_in,
        out_specs=out_specs + [ANY] * job.n_out, out_shape=out_shape + job.out_shape,
        scratch_shapes=scratch + job.scratch, input_output_aliases=aliases or {},
        compiler_params=_cp(*(["arbitrary"] * len(grid))))(*args, *job.args)
    return res[:n_out], res[n_out:]


def run_job(job, name):
    return _call(lambda: None, name, (), [], [], [], [], job=job)[1]


@jax.custom_vjp
def bmm(x, w):
    return _nn(x.astype(BF16), w.astype(BF16))


def _bmm_fwd(x, w):
    return bmm(x, w), (x, w)


def _bmm_bwd(res, g):
    x, w = res
    gb = g.astype(BF16)
    return _nt(gb, w.astype(BF16)), _tn(x.astype(BF16), gb)


bmm.defvjp(_bmm_fwd, _bmm_bwd)


def _shift_raw(x, d):
    n = x.shape[0]
    r = pltpu.roll(x, d % n, axis=0)
    row = lax.broadcasted_iota(jnp.int32, x.shape, 0)
    keep = (row >= d) if d > 0 else (row < n + d)
    return jnp.where(keep, r, 0.0)


def shift_rows(x, d):
    @jax.custom_vjp
    def f(v):
        return _shift_raw(v, d)

    f.defvjp(lambda v: (_shift_raw(v, d), None), lambda _, g: (_shift_raw(g, -d),))
    return f(x)


def _swap_raw(x):
    n = x.shape[1]
    nxt = pltpu.roll(x, n - 1, axis=1)
    prv = pltpu.roll(x, 1, axis=1)
    lane = lax.broadcasted_iota(jnp.int32, x.shape, 1)
    return jnp.where(lane % 2 == 0, nxt, prv)


@jax.custom_vjp
def swap_pairs(x):
    return _swap_raw(x)


swap_pairs.defvjp(lambda x: (_swap_raw(x), None), lambda _, g: (_swap_raw(g),))


@jax.custom_vjp
def group_mean(x, bd):
    return _split_mm(x, bd)


group_mean.defvjp(lambda x, bd: (_split_mm(x, bd), bd), lambda bd, g: (_split_mm(g, bd), jnp.zeros_like(bd)))


def _rope_norm(x, gain, cos, sgn, bd, scale):
    xn = x * lax.rsqrt(group_mean(x * x, bd) + EPS) * gain
    return (xn * cos + swap_pairs(xn) * sgn) * scale


def _conv_gate(gb, gc, hc, w):
    z = gc * hc
    c = shift_rows(z, 1) * w[0:1] + z * w[1:2] + shift_rows(z, -1) * w[2:3]
    return gb * c


def _window_sum(p, r):
    b = f = p
    k = 1
    while k < r:
        b = b + shift_rows(b, k)
        f = f + shift_rows(f, -k)
        k *= 2
    return b + f - p + shift_rows(p, r) + shift_rows(p, -r)


def _pool_mix(p, pool_w, scale):
    n = p.shape[0]
    t = lax.broadcasted_iota(jnp.int32, (n, 1), 0)
    outs = []
    for gi, r in enumerate(POOL_RADII):
        pg = p[:, gi * GROUP:(gi + 1) * GROUP]
        cnt = (jnp.minimum(t + r, n - 1) - jnp.maximum(t - r, 0) + 1).astype(F32)
        pooled = _window_sum(pg, r) / cnt - pg
        outs.append(bmm(pooled, pool_w[gi]))
    return jnp.concatenate(outs, axis=1) * scale


def _sgu(u, v, norm_g, w_s, b_full):
    ug = jax.nn.gelu(u)
    vg = jax.nn.gelu(v)
    vn = vg * lax.rsqrt(jnp.mean(vg * vg, axis=-1, keepdims=True) + EPS) * norm_g
    cols = []
    for g in range(w_s.shape[0]):
        rows = []
        for n in range(u.shape[0] // SGU_CHUNK):
            blk = vn[n * SGU_CHUNK:(n + 1) * SGU_CHUNK, g * GROUP:(g + 1) * GROUP]
            rows.append(bmm(w_s[g], blk) + b_full[g])
        cols.append(jnp.concatenate(rows, axis=0))
    return ug * jnp.concatenate(cols, axis=1)


def _rms_bwd_math(xv, gain, dy, dres):
    r = lax.rsqrt(jnp.mean(xv * xv, axis=-1, keepdims=True) + EPS)
    xh = xv * r
    dxh = dy * gain
    dx = dres + r * (dxh - xh * jnp.mean(dxh * xh, axis=-1, keepdims=True))
    return dx, jnp.sum(dy * xh, axis=0, keepdims=True)


FFN_TN = 256


def ffn_fwd(x, gain, wt_in, w_out, job=None):
    T, D = x.shape
    F = w_out.shape[0]
    tm, tn = _tile(T, 1024), FFN_TN
    nc = F // tn

    def body(x_ref, gn_ref, wg_ref, wu_ref, wo_ref, y_ref, gu_ref, xn_s, acc_s):
        c = pl.program_id(1)

        @pl.when(c == 0)
        def _():
            xv = x_ref[...]
            r = lax.rsqrt(jnp.mean(xv * xv, axis=-1, keepdims=True) + EPS)
            xn_s[...] = (xv * r * gn_ref[...]).astype(BF16)
            acc_s[...] = jnp.zeros_like(acc_s)

        xn = xn_s[...]
        g = _nt(xn, wg_ref[...])
        u = _nt(xn, wu_ref[...])
        gu_ref[:, :tn] = g.astype(BF16)
        gu_ref[:, tn:] = u.astype(BF16)
        a = (g * jax.nn.sigmoid(g) * u).astype(BF16)
        acc_s[...] += _nn(a, wo_ref[...])

        @pl.when(c == nc - 1)
        def _():
            y_ref[...] = x_ref[...] + 0.5 * acc_s[...]

    row = pl.BlockSpec((tm, D), lambda i, c: (i, 0))
    return _call(
        body, "ffn_fwd", (T // tm, nc),
        [row, pl.BlockSpec((1, D), lambda i, c: (0, 0)),
         pl.BlockSpec((tn, D), lambda i, c: (c, 0)),
         pl.BlockSpec((tn, D), lambda i, c: (c + nc, 0)),
         pl.BlockSpec((tn, D), lambda i, c: (c, 0))],
        [x, gain, wt_in, wt_in, w_out],
        [row, pl.BlockSpec((tm, 2 * tn), lambda i, c: (i, c)), row],
        [jax.ShapeDtypeStruct((T, D), F32), jax.ShapeDtypeStruct((T, 2 * F), BF16), jax.ShapeDtypeStruct((T, D), BF16)],
        [pltpu.VMEM((tm, D), F32)], ("parallel", "arbitrary"), job)


def ffn_bwd_x(dout, x, gain, gu, wt_in, w_out, job=None):
    T, D = x.shape
    F = w_out.shape[0]
    tm, tn = _tile(T, 1024), FFN_TN
    nc = F // tn

    def body(do_ref, x_ref, gn_ref, gu_ref, wg_ref, wu_ref, wo_ref,
             dx_ref, dgn_ref, a_ref, dgu_ref, dob_ref, acc_s):
        i, c = pl.program_id(0), pl.program_id(1)

        @pl.when(c == 0)
        def _():
            dob_ref[...] = (0.5 * do_ref[...]).astype(BF16)
            acc_s[...] = jnp.zeros_like(acc_s)

        da = jnp.concatenate([_nt(dob_ref[:tm // 2, :], wo_ref[...]), _nt(dob_ref[tm // 2:, :], wo_ref[...])], axis=0)
        g = gu_ref[:, :tn].astype(F32)
        u = gu_ref[:, tn:].astype(F32)
        sig = jax.nn.sigmoid(g)
        sl = g * sig
        a_ref[...] = (sl * u).astype(BF16)
        dg = (da * u * (sig * (1.0 + g * (1.0 - sig)))).astype(BF16)
        du = (da * sl).astype(BF16)
        dgu_ref[:, :tn] = dg
        dgu_ref[:, tn:] = du
        acc_s[...] += _nn(dg, wg_ref[...]) + _nn(du, wu_ref[...])

        @pl.when(c == nc - 1)
        def _():
            dx, dgn = _rms_bwd_math(x_ref[...], gn_ref[...], acc_s[...], do_ref[...])
            dx_ref[...] = dx

            @pl.when(i == 0)
            def _():
                dgn_ref[...] = jnp.zeros_like(dgn_ref)

            dgn_ref[...] += dgn

    row = pl.BlockSpec((tm, D), lambda i, c: (i, 0))
    return _call(
        body, "ffn_bwd_x", (T // tm, nc),
        [row, row, pl.BlockSpec((1, D), lambda i, c: (0, 0)),
         pl.BlockSpec((tm, 2 * tn), lambda i, c: (i, c)),
         pl.BlockSpec((tn, D), lambda i, c: (c, 0)),
         pl.BlockSpec((tn, D), lambda i, c: (c + nc, 0)),
         pl.BlockSpec((tn, D), lambda i, c: (c, 0))],
        [dout, x, gain, gu, wt_in, wt_in, w_out],
        [row, pl.BlockSpec((1, D), lambda i, c: (0, 0)),
         pl.BlockSpec((tm, tn), lambda i, c: (i, c)),
         pl.BlockSpec((tm, 2 * tn), lambda i, c: (i, c)), row],
        [jax.ShapeDtypeStruct((T, D), F32), jax.ShapeDtypeStruct((1, D), F32),
         jax.ShapeDtypeStruct((T, F), BF16), jax.ShapeDtypeStruct((T, 2 * F), BF16),
         jax.ShapeDtypeStruct((T, D), BF16)],
        [pltpu.VMEM((tm, D), F32)], ("arbitrary", "arbitrary"), job)


MM_TM = 512
MM_TC = 256


def mm_nt(a, wt, pieces, out_dtype, emit_a_bf16=False, norm_gain=None):
    T, K = a.shape
    tm = _tile(T, MM_TM)
    npc = len(pieces)
    n_lead = 1 if norm_gain is None else 2

    def body(*refs):
        a_ref, w_refs, o_refs = refs[0], refs[n_lead:n_lead + npc], refs[n_lead + npc:]
        av = a_ref[...]
        if norm_gain is not None:
            av = av * lax.rsqrt(jnp.mean(av * av, axis=-1, keepdims=True) + EPS) * refs[1][...]
        ab = av.astype(BF16)
        for w_ref, o_ref in zip(w_refs, o_refs[:npc]):
            o_ref[...] = _nt(ab, w_ref[...]).astype(o_ref.dtype)
        if emit_a_bf16:
            o_refs[npc][...] = ab

    in_specs = [pl.BlockSpec((tm, K), lambda i: (i, 0))]
    if norm_gain is not None:
        in_specs.append(pl.BlockSpec((1, K), lambda i: (0, 0)))
    out_specs, out_shape = [], []
    for r0, n in pieces:
        assert r0 % n == 0
        in_specs.append(pl.BlockSpec((n, K), functools.partial(lambda i, b: (b, 0), b=r0 // n)))
        out_specs.append(pl.BlockSpec((tm, n), lambda i: (i, 0)))
        out_shape.append(jax.ShapeDtypeStruct((T, n), out_dtype))
    if emit_a_bf16:
        out_specs.append(pl.BlockSpec((tm, K), lambda i: (i, 0)))
        out_shape.append(jax.ShapeDtypeStruct((T, K), BF16))
    return pl.pallas_call(
        body, name="mm_nt", grid=(T // tm,), in_specs=in_specs, out_specs=out_specs, out_shape=out_shape,
        compiler_params=_cp("parallel"),
    )(a, *([] if norm_gain is None else [norm_gain]), *([wt] * npc))


def mm_nn(pieces, w, residual=None, norm_bwd=None):
    T = pieces[0][0].shape[0]
    N = w.shape[1]
    tm = _tile(T, MM_TM)
    na = len(pieces)

    def body(*refs):
        a_refs, w_refs = refs[:na], refs[na:2 * na]
        acc = refs[2 * na][...] if residual is not None else None
        for a_ref, w_ref in zip(a_refs, w_refs):
            t = _nn(a_ref[...].astype(BF16), w_ref[...])
            acc = t if acc is None else acc + t
        if norm_bwd is None:
            refs[-1][...] = acc
            return
        x_ref, g_ref, dr_ref, dx_ref, dg_ref = refs[-5:]
        dx, dg = _rms_bwd_math(x_ref[...], g_ref[...], acc, dr_ref[...])
        dx_ref[...] = dx

        @pl.when(pl.program_id(0) == 0)
        def _():
            dg_ref[...] = jnp.zeros_like(dg_ref)

        dg_ref[...] += dg

    in_specs, w_specs = [], []
    for a, cb, k, rb in pieces:
        in_specs.append(pl.BlockSpec((tm, k), functools.partial(lambda i, b: (i, b), b=cb)))
        w_specs.append(pl.BlockSpec((k, N), functools.partial(lambda i, b: (b, 0), b=rb)))
    assert sum(k for _, _, k, _ in pieces) == w.shape[0]
    args = [a for a, _, _, _ in pieces] + [w] * na
    in_specs = in_specs + w_specs
    row = pl.BlockSpec((tm, N), lambda i: (i, 0))
    if residual is not None:
        in_specs.append(row)
        args.append(residual)
    if norm_bwd is None:
        return pl.pallas_call(
            body, name="mm_nn", grid=(T // tm,), in_specs=in_specs, out_specs=row,
            out_shape=jax.ShapeDtypeStruct((T, N), F32), compiler_params=_cp("parallel"),
        )(*args)
    vec = pl.BlockSpec((1, N), lambda i: (0, 0))
    return pl.pallas_call(
        body, name="mm_nn_norm_bwd", grid=(T // tm,), in_specs=in_specs + [row, vec, row], out_specs=[row, vec],
        out_shape=[jax.ShapeDtypeStruct((T, N), F32), jax.ShapeDtypeStruct((1, N), F32)],
        compiler_params=_cp("arbitrary"),
    )(*args, *norm_bwd)


def mm_tn(a, b, n_rows, row_block, prev=None, grid=None, col_block=None, job=None):
    T, M = a.shape
    N = b.shape[1]
    tc = MM_TC
    assert M % tc == 0 and n_rows % tc == 0
    if grid is None:
        grid, col_block = (M // tc,), (lambda j: j)

    def body(*refs):
        a_ref, b_ref, o_ref = refs[0], refs[1], refs[-1]
        o_ref[...] = _tn(a_ref[...], b_ref[...]).astype(BF16)

    in_specs = [pl.BlockSpec((T, tc), lambda *g: (0, col_block(*g))), pl.BlockSpec((T, N), lambda *g: (0, 0))]
    args = [a, b]
    aliases = {}
    if prev is not None:
        in_specs.append(pl.BlockSpec(memory_space=pl.ANY))
        args.append(prev)
        aliases = {2: 0}
    res, jres = _call(body, "mm_tn", grid, in_specs, args, [pl.BlockSpec((tc, N), lambda *g: (row_block(*g), 0))],
                      [jax.ShapeDtypeStruct((n_rows, N), BF16)], (), ["parallel"] * len(grid), job, aliases)
    return res[0] if job is None else (res[0], jres)


def conv_fwd(proj_a, conv_w, n_ex):
    T, C3 = proj_a.shape
    C = C3 // 3
    S = T // n_ex

    def body(gb_ref, gc_ref, hc_ref, w_ref, o_ref):
        o_ref[...] = _conv_gate(gb_ref[...], gc_ref[...], hc_ref[...], w_ref[...]).astype(BF16)

    col = lambda k: pl.BlockSpec((S, C), functools.partial(lambda b, kk: (b, kk), kk=k))
    return pl.pallas_call(
        body, name="conv_fwd", grid=(n_ex,),
        in_specs=[col(0), col(1), col(2), pl.BlockSpec((3, C), lambda b: (0, 0))],
        out_specs=pl.BlockSpec((S, C), lambda b: (b, 0)),
        out_shape=jax.ShapeDtypeStruct((T, 2 * C), BF16), compiler_params=_cp("parallel"),
    )(proj_a, proj_a, proj_a, conv_w)


def conv_bwd(proj_a, conv_w, n_ex, dy, total_cols):
    T, C3 = proj_a.shape
    C = C3 // 3
    S = T // n_ex

    def body(gb_ref, gc_ref, hc_ref, w_ref, dy_ref, dp_ref, dw_ref):
        _, vjp = jax.vjp(_conv_gate, gb_ref[...], gc_ref[...], hc_ref[...], w_ref[...])
        dgb, dgc, dhc, dw = vjp(dy_ref[...].astype(F32))
        dp_ref[:, 0:C] = dgb.astype(BF16)
        dp_ref[:, C:2 * C] = dgc.astype(BF16)
        dp_ref[:, 2 * C:] = dhc.astype(BF16)

        @pl.when(pl.program_id(0) == 0)
        def _():
            dw_ref[...] = jnp.zeros_like(dw_ref)

        dw_ref[...] += dw

    col = lambda k: pl.BlockSpec((S, C), functools.partial(lambda b, kk: (b, kk), kk=k))
    return pl.pallas_call(
        body, name="conv_bwd", grid=(n_ex,),
        in_specs=[col(0), col(1), col(2), pl.BlockSpec((3, C), lambda b: (0, 0)),
                  pl.BlockSpec((S, C), lambda b: (b, 0))],
        out_specs=[pl.BlockSpec((S, C3), lambda b: (b, 0)), pl.BlockSpec((3, C), lambda b: (0, 0))],
        out_shape=[jax.ShapeDtypeStruct((T, total_cols), BF16), jax.ShapeDtypeStruct((3, C), F32)],
        compiler_params=_cp("arbitrary"),
    )(proj_a, proj_a, proj_a, conv_w, dy)


QW = N_Q_HEADS * HEAD_DIM
KW = N_KV_HEADS * HEAD_DIM
QP = N_Q_HEADS * LANES
KP = N_KV_HEADS * LANES


def _attn_consts(seq):
    rows = seq // GRID_W
    r_idx, c_idx = jnp.meshgrid(jnp.arange(rows), jnp.arange(GRID_W), indexing='ij')
    r_idx = r_idx.reshape(-1).astype(F32)
    c_idx = c_idx.reshape(-1).astype(F32)
    n_freq = HEAD_DIM // 4
    inv = ROPE_THETA ** (-jnp.arange(n_freq, dtype=F32) / n_freq)
    ang = jnp.concatenate([r_idx[:, None] * inv, c_idx[:, None] * inv], axis=-1)
    cos = jnp.repeat(jnp.cos(ang), 2, axis=1)
    sin = jnp.repeat(jnp.sin(ang), 2, axis=1)
    sgn = sin * jnp.tile(jnp.array([-1.0, 1.0], F32), HEAD_DIM // 2)
    cos = jnp.tile(cos, (1, N_Q_HEADS))
    sgn = jnp.tile(sgn, (1, N_Q_HEADS))
    lane = jnp.arange(QW)
    bd = jnp.where(lane[:, None] // HEAD_DIM == lane[None, :] // HEAD_DIM, 1.0 / HEAD_DIM, 0.0).astype(BF16)
    dst = (lane // HEAD_DIM) * LANES + lane % HEAD_DIM
    spread = (dst[:, None] == jnp.arange(QP)[None, :]).astype(BF16)
    return dict(cos=cos, sgn=sgn, bd=bd, spread=spread, gather=spread.T)


def qkv_prep_fwd(proj_b, qg, kg, cst, n_ex):
    T = proj_b.shape[0]
    S = T // n_ex
    tm = _tile(S, 512)
    nb = S // tm

    def body(p_ref, qg_ref, kg_ref, cos_ref, sgn_ref, bd_ref, sp_ref, q_ref, k_ref, v_ref):
        pv = p_ref[...]
        cos, sgn, bd, sp = cos_ref[...], sgn_ref[...], bd_ref[...], sp_ref[...]
        qr = _rope_norm(pv[:, :QW], qg_ref[...], cos, sgn, bd, HEAD_DIM ** -0.5)
        kr = _rope_norm(pv[:, QW:QW + KW], kg_ref[...], cos[:, :KW], sgn[:, :KW], bd[:KW, :KW], 1.0)
        q_ref[...] = _nn(qr.astype(BF16), sp).astype(BF16)
        k_ref[...] = _nn(kr.astype(BF16), sp[:KW, :KP]).astype(BF16)
        v_ref[...] = _nn(pv[:, QW + KW:].astype(BF16), sp[:KW, :KP]).astype(BF16)

    full = lambda a: pl.BlockSpec(a.shape, lambda i: (0,) * a.ndim)
    tab = pl.BlockSpec((tm, QW), lambda i: (i % nb, 0))
    return pl.pallas_call(
        body, name="qkv_prep_fwd", grid=(T // tm,),
        in_specs=[pl.BlockSpec((tm, QW + 2 * KW), lambda i: (i, 0)), full(qg), full(kg), tab, tab,
                  full(cst["bd"]), full(cst["spread"])],
        out_specs=[pl.BlockSpec((tm, QP), lambda i: (i, 0)), pl.BlockSpec((tm, KP), lambda i: (i, 0)),
                   pl.BlockSpec((tm, KP), lambda i: (i, 0))],
        out_shape=[jax.ShapeDtypeStruct((T, QP), BF16), jax.ShapeDtypeStruct((T, KP), BF16),
                   jax.ShapeDtypeStruct((T, KP), BF16)],
        compiler_params=_cp("parallel"),
    )(proj_b, qg, kg, cst["cos"], cst["sgn"], cst["bd"], cst["spread"])


def qkv_prep_bwd(proj_b, qg, kg, cst, n_ex, dq, dk_pad, dv_pad, d_proj):
    T = proj_b.shape[0]
    S = T // n_ex
    tm = _tile(S, 512)
    nb = S // tm

    def body(p_ref, qg_ref, kg_ref, cos_ref, sgn_ref, bd_ref, ga_ref, dq_ref, dk_ref, dv_ref, _kept,
             dp_ref, dqg_ref, dkg_ref):
        pv = p_ref[...]
        cos, sgn, bd, ga = cos_ref[...], sgn_ref[...], bd_ref[...], ga_ref[...]
        fq = lambda q, g: _rope_norm(q, g, cos, sgn, bd, HEAD_DIM ** -0.5)
        fk = lambda k, g: _rope_norm(k, g, cos[:, :KW], sgn[:, :KW], bd[:KW, :KW], 1.0)
        _, vq = jax.vjp(fq, pv[:, :QW], qg_ref[...])
        _, vk = jax.vjp(fk, pv[:, QW:QW + KW], kg_ref[...])
        dqp, dqg = vq(dq_ref[...])
        dkp, dkg = vk(_split_mm(dk_ref[...], ga[:KP, :KW]))
        dp_ref[:, :QW] = dqp.astype(BF16)
        dp_ref[:, QW:QW + KW] = dkp.astype(BF16)
        dp_ref[:, QW + KW:] = _split_mm(dv_ref[...], ga[:KP, :KW]).astype(BF16)

        @pl.when(pl.program_id(0) == 0)
        def _():
            dqg_ref[...] = jnp.zeros_like(dqg_ref)
            dkg_ref[...] = jnp.zeros_like(dkg_ref)

        dqg_ref[...] += dqg
        dkg_ref[...] += dkg

    full = lambda a: pl.BlockSpec(a.shape, lambda i: (0,) * a.ndim)
    tab = pl.BlockSpec((tm, QW), lambda i: (i % nb, 0))
    row = lambda n: pl.BlockSpec((tm, n), lambda i: (i, 0))
    wb = QW + 2 * KW
    assert d_proj.shape[1] % wb == 0
    last = d_proj.shape[1] // wb - 1
    return pl.pallas_call(
        body, name="qkv_prep_bwd", grid=(T // tm,),
        in_specs=[row(wb), full(qg), full(kg), tab, tab, full(cst["bd"]), full(cst["gather"]),
                  row(QW), row(KP), row(KP), ANY],
        out_specs=[pl.BlockSpec((tm, wb), lambda i: (i, last)), pl.BlockSpec((1, QW), lambda i: (0, 0)),
                   pl.BlockSpec((1, KW), lambda i: (0, 0))],
        out_shape=[jax.ShapeDtypeStruct(d_proj.shape, BF16), jax.ShapeDtypeStruct((1, QW), F32),
                   jax.ShapeDtypeStruct((1, KW), F32)],
        input_output_aliases={10: 0}, compiler_params=_cp("arbitrary"),
    )(proj_b, qg, kg, cst["cos"], cst["sgn"], cst["bd"], cst["gather"], dq, dk_pad, dv_pad, d_proj)


ATT_TQ = 256
ATT_TQ_FWD = 512


def attn_fwd(qp, kp, vp, gather, n_ex, mix, job=None):
    T = qp.shape[0]
    S = T // n_ex
    tq = _tile(S, ATT_TQ_FWD)
    nq = S // tq

    def body(q_ref, k_ref, v_ref, ga_ref, _kept, o_ref, op_ref, lse_ref):
        lane = lax.broadcasted_iota(jnp.int32, (tq, LANES), 1)
        lse_all = jnp.zeros((tq, LANES), F32)
        for h in range(N_Q_HEADS):
            kv = h // Q_PER_KV
            qh = q_ref[:, h * LANES:(h + 1) * LANES]
            s = _nt(qh, k_ref[:, kv * LANES:(kv + 1) * LANES])
            m = jnp.max(s, axis=-1, keepdims=True)
            p = jnp.exp(s - m)
            lsum = jnp.sum(p, axis=-1, keepdims=True)
            o = _nn(p.astype(BF16), v_ref[:, kv * LANES:(kv + 1) * LANES]) / lsum
            op_ref[:, h * LANES:(h + 1) * LANES] = o.astype(BF16)
            lse_all = jnp.where(lane == h, m + jnp.log(lsum), lse_all)
        lse_ref[...] = lse_all
        o_ref[...] = _nn(op_ref[...], ga_ref[...]).astype(BF16)

    blk = lambda n: pl.BlockSpec((tq, n), lambda b, i: (b * nq + i, 0))
    kvs = pl.BlockSpec((S, KP), lambda b, i: (b, 0))
    return _call(
        body, "attn_fwd", (n_ex, nq),
        [blk(QP), kvs, kvs, pl.BlockSpec(gather.shape, lambda b, i: (0, 0)), ANY], [qp, kp, vp, gather, mix],
        [pl.BlockSpec((tq, QW), lambda b, i: (b * nq + i, 1)), blk(QP), blk(LANES)],
        [jax.ShapeDtypeStruct(mix.shape, BF16), jax.ShapeDtypeStruct((T, QP), BF16),
         jax.ShapeDtypeStruct((T, LANES), F32)], (), ("parallel", "parallel"), job, {4: 0})


def attn_bwd(qp, kp, vp, op, lse, do, cst, n_ex, job=None):
    T = qp.shape[0]
    S = T // n_ex
    tq = _tile(S, ATT_TQ)
    nq = S // tq

    def body(q_ref, k_ref, v_ref, op_ref, lse_ref, do_ref, sp_ref, ga_ref, dq_ref, dk_ref, dv_ref, dqp_s):
        @pl.when(pl.program_id(1) == 0)
        def _():
            dk_ref[...] = jnp.zeros_like(dk_ref)
            dv_ref[...] = jnp.zeros_like(dv_ref)

        lane = lax.broadcasted_iota(jnp.int32, (tq, LANES), 1)
        dop = _nn(do_ref[...], sp_ref[...]).astype(BF16)
        lse_all = lse_ref[...]
        for h in range(N_Q_HEADS):
            kv = h // Q_PER_KV
            hs = slice(h * LANES, (h + 1) * LANES)
            ks = slice(kv * LANES, (kv + 1) * LANES)
            qh, kk, vv = q_ref[:, hs], k_ref[:, ks], v_ref[:, ks]
            doh = dop[:, hs]
            lse_h = jnp.sum(jnp.where(lane == h, lse_all, 0.0), axis=-1, keepdims=True)
            p = jnp.exp(_nt(qh, kk) - lse_h)
            dp = _nt(doh, vv)
            delta = jnp.sum(doh.astype(F32) * op_ref[:, hs].astype(F32), axis=-1, keepdims=True)
            ds = (p * (dp - delta)).astype(BF16)
            dqp_s[:, hs] = _nn(ds, kk)
            dk_ref[:, ks] += _tn(ds, qh)
            dv_ref[:, ks] += _tn(p.astype(BF16), doh)
        dq_ref[...] = _split_mm(dqp_s[...], ga_ref[...])

    blk = lambda n: pl.BlockSpec((tq, n), lambda b, i: (b * nq + i, 0))
    kvs = pl.BlockSpec((S, KP), lambda b, i: (b, 0))
    full = lambda a: pl.BlockSpec(a.shape, lambda b, i: (0, 0))
    return _call(
        body, "attn_bwd", (n_ex, nq),
        [blk(QP), kvs, kvs, blk(QP), blk(LANES), pl.BlockSpec((tq, QW), lambda b, i: (b * nq + i, 1)),
         full(cst["spread"]), full(cst["gather"])],
        [qp, kp, vp, op, lse, do, cst["spread"], cst["gather"]],
        [blk(QW), kvs, kvs],
        [jax.ShapeDtypeStruct((T, QW), F32), jax.ShapeDtypeStruct((T, KP), F32), jax.ShapeDtypeStruct((T, KP), F32)],
        [pltpu.VMEM((tq, QP), F32)], ("arbitrary", "arbitrary"), job)


def pool_fwd(p, pool_w, scale, n_ex):
    T, W = p.shape
    S = T // n_ex

    def body(p_ref, w_ref, s_ref, o_ref):
        o_ref[...] = _pool_mix(p_ref[...], w_ref[...], s_ref[...]).astype(BF16)

    return pl.pallas_call(
        body, name="pool_fwd", grid=(n_ex,),
        in_specs=[pl.BlockSpec((S, W), lambda b: (b, 0)),
                  pl.BlockSpec(pool_w.shape, lambda b: (0, 0, 0)),
                  pl.BlockSpec((1, W), lambda b: (0, 0))],
        out_specs=pl.BlockSpec((S, W), lambda b: (b, 0)),
        out_shape=jax.ShapeDtypeStruct((T, 2 * W), BF16), compiler_params=_cp("parallel"),
    )(p, pool_w, scale)


def pool_bwd(p, pool_w, scale, n_ex, dy, d_proj):
    T, W = p.shape
    S = T // n_ex
    last = d_proj.shape[1] // W - 1

    def body(p_ref, w_ref, s_ref, dy_ref, _kept, dp_ref, dw_ref, ds_ref):
        _, vjp = jax.vjp(_pool_mix, p_ref[...], w_ref[...], s_ref[...])
        dp, dw, ds = vjp(dy_ref[...].astype(F32))
        dp_ref[...] = dp.astype(BF16)

        @pl.when(pl.program_id(0) == 0)
        def _():
            dw_ref[...] = jnp.zeros_like(dw_ref)
            ds_ref[...] = jnp.zeros_like(ds_ref)

        dw_ref[...] += dw
        ds_ref[...] += ds

    wshape = pool_w.shape
    return pl.pallas_call(
        body, name="pool_bwd", grid=(n_ex,),
        in_specs=[pl.BlockSpec((S, W), lambda b: (b, 0)),
                  pl.BlockSpec(wshape, lambda b: (0, 0, 0)),
                  pl.BlockSpec((1, W), lambda b: (0, 0)), pl.BlockSpec((S, W), lambda b: (b, 0)), ANY],
        out_specs=[pl.BlockSpec((S, W), lambda b: (b, last)), pl.BlockSpec(wshape, lambda b: (0, 0, 0)),
                   pl.BlockSpec((1, W), lambda b: (0, 0))],
        out_shape=[jax.ShapeDtypeStruct(d_proj.shape, BF16), jax.ShapeDtypeStruct(wshape, F32),
                   jax.ShapeDtypeStruct((1, W), F32)],
        input_output_aliases={4: 0}, compiler_params=_cp("arbitrary"),
    )(p, pool_w, scale, dy, d_proj)


SGU_TS = 512


def sgu_fwd(u, v, norm_g, w_s, b_full, mix):
    T, W = u.shape
    ts = _tile(T, SGU_TS)

    def body(u_ref, v_ref, g_ref, w_ref, b_ref, _kept, o_ref):
        o_ref[...] = _sgu(u_ref[...], v_ref[...], g_ref[...], w_ref[...], b_ref[...]).astype(BF16)

    row = pl.BlockSpec((ts, W), lambda i: (i, 0))
    wsp = pl.BlockSpec(w_s.shape, lambda i: (0, 0, 0))
    return pl.pallas_call(
        body, name="sgu_fwd", grid=(T // ts,),
        in_specs=[row, row, pl.BlockSpec((1, W), lambda i: (0, 0)), wsp, wsp, ANY],
        out_specs=pl.BlockSpec((ts, W), lambda i: (i, 1)), out_shape=jax.ShapeDtypeStruct(mix.shape, BF16),
        input_output_aliases={5: 0}, compiler_params=_cp("parallel"),
    )(u, v, norm_g, w_s, b_full, mix)


def sgu_bwd(u, v, norm_g, w_s, b_full, dy):
    T, W = u.shape
    ts = _tile(T, SGU_TS)
    wshape = w_s.shape

    def body(u_ref, v_ref, g_ref, w_ref, b_ref, dy_ref, duv_ref, dg_ref, dw_ref, db_ref):
        _, vjp = jax.vjp(_sgu, u_ref[...], v_ref[...], g_ref[...], w_ref[...], b_ref[...])
        du, dv, dg, dw, db = vjp(dy_ref[...].astype(F32))
        duv_ref[:, :W] = du.astype(BF16)
        duv_ref[:, W:] = dv.astype(BF16)

        @pl.when(pl.program_id(0) == 0)
        def _():
            dg_ref[...] = jnp.zeros_like(dg_ref)
            dw_ref[...] = jnp.zeros_like(dw_ref)
            db_ref[...] = jnp.zeros_like(db_ref)

        dg_ref[...] += dg
        dw_ref[...] += dw
        db_ref[...] += db

    row = pl.BlockSpec((ts, W), lambda i: (i, 0))
    wsp = pl.BlockSpec(wshape, lambda i: (0, 0, 0))
    wout = pl.BlockSpec(wshape, lambda i: (0, 0, 0))
    vec = pl.BlockSpec((1, W), lambda i: (0, 0))
    return pl.pallas_call(
        body, name="sgu_bwd", grid=(T // ts,),
        in_specs=[row, row, vec, wsp, wsp, pl.BlockSpec((ts, W), lambda i: (i, 1))],
        out_specs=[pl.BlockSpec((ts, 2 * W), lambda i: (i, 0)), vec, wout, wout],
        out_shape=[jax.ShapeDtypeStruct((T, 3 * W), BF16),
                   jax.ShapeDtypeStruct((1, W), F32), jax.ShapeDtypeStruct(wshape, F32),
                   jax.ShapeDtypeStruct(wshape, F32)],
        compiler_params=_cp("arbitrary"),
    )(u, v, norm_g, w_s, b_full, dy)


def loss_head(x, gain, target):
    T, D = x.shape
    tm = _tile(T, 512)

    def body(x_ref, g_ref, t_ref, loss_ref, dx_ref, dg_ref):
        xv, g = x_ref[...], g_ref[...]
        r = lax.rsqrt(jnp.mean(xv * xv, axis=-1, keepdims=True) + EPS)
        err = xv * r * g - t_ref[...]
        part = 0.5 * jnp.sum(jnp.mean(err * err, axis=-1, keepdims=True), axis=0, keepdims=True)
        dx, dg = _rms_bwd_math(xv, g, err * (1.0 / D), jnp.zeros_like(xv))
        dx_ref[...] = dx

        @pl.when(pl.program_id(0) == 0)
        def _():
            loss_ref[...] = jnp.zeros_like(loss_ref)
            dg_ref[...] = jnp.zeros_like(dg_ref)

        loss_ref[...] += part
        dg_ref[...] += dg

    row = pl.BlockSpec((tm, D), lambda i: (i, 0))
    vec = pl.BlockSpec((1, D), lambda i: (0, 0))
    return pl.pallas_call(
        body, name="loss_head", grid=(T // tm,),
        in_specs=[row, vec, row], out_specs=[pl.BlockSpec((1, 1), lambda i: (0, 0)), row, vec],
        out_shape=[jax.ShapeDtypeStruct((1, 1), F32), jax.ShapeDtypeStruct((T, D), F32),
                   jax.ShapeDtypeStruct((1, D), F32)],
        compiler_params=_cp("arbitrary"),
    )(x, gain, target)


class MultiJob:
    def __init__(self, jobs):
        self.jobs = jobs
        self.args = [a for j in jobs for a in j.args]
        self.out_shape = [s for j in jobs for s in j.out_shape]
        self.scratch = [s for j in jobs for s in j.scratch]
        self.n_in, self.n_out = len(self.args), len(self.out_shape)

    def _each(self, ins, outs, sems):
        i = o = s = 0
        for j in self.jobs:
            yield j, ins[i:i + j.n_in], outs[o:o + j.n_out], sems[s:s + len(j.scratch)]
            i, o, s = i + j.n_in, o + j.n_out, s + len(j.scratch)

    def start(self, ins, outs, sems):
        for j, a, b, c in self._each(ins, outs, sems):
            j.start(a, b, c)

    def middle(self, ins, outs, sems):
        for j, a, b, c in self._each(ins, outs, sems):
            j.middle(a, b, c)

    def finish(self, ins, outs, sems):
        for j, a, b, c in self._each(ins, outs, sems):
            j.finish(a, b, c)

    def split(self, results):
        o = 0
        for j in self.jobs:
            yield results[o:o + j.n_out]
            o += j.n_out


class Plan:
    def __init__(self, shard, gathers, small_carrier=None, pack_small=None, exchange=True):
        self.shard, self.gathers, self.exchange = shard, gathers, exchange
        self.small_carrier, self.pack_small = small_carrier, pack_small
        self.weights, self.grads, self.started = {}, {}, []
        self.small_src = self.small_parts = None

    def scatter(self, keys, carry=None):
        if not self.exchange:
            return carry
        tag = "_".join(f"{kind}{l}" for kind, l in keys)
        send, recv, grads, zones, carry, token = scatter_start([self.grads[k] for k in keys], tag, carry)
        self.started.append((keys, send, recv, grads, zones, token, tag))
        return carry

    def collect(self, groups, after):
        got = {}
        for keys, send, recv, grads, zones, _, tag in groups:
            grads, zones = scatter_wait(send, recv, grads, zones, after, tag)
            got.update({k: (g, z) for k, g, z in zip(keys, grads, zones)})
        return got

    def weight(self, kind, l):
        return self.weights[(kind, l)]

    def grad(self, kind, l, g):
        self.grads[(kind, l)] = g

    def small_ready(self, small, d_final):
        if self.pack_small is not None:
            self.small_src = self.pack_small(small, d_final)

    def _jobs(self, key):
        jobs = []
        if key in self.gathers:
            ks = self.gathers[key]
            jobs.append((GatherJob([self.shard(*k) for k in ks]), self.weights, ks))
        if key == self.small_carrier and self.small_src is not None:
            jobs.append((GatherJob([(self.small_src, None)]), None, None))
        return jobs

    def _deliver(self, jobs, results):
        multi = MultiJob([j for j, _, _ in jobs])
        for (_, store, ks), res in zip(jobs, multi.split(results)):
            if store is None:
                self.small_parts = res[0]
            else:
                store.update(zip(ks, res))

    def run(self, key, fn, *args, **kw):
        jobs = self._jobs(key)
        if not jobs:
            out = fn(*args, **kw)
            return out if fn is mm_tn else out[0]
        res, jres = fn(*args, job=MultiJob([j for j, _, _ in jobs]), **kw)
        self._deliver(jobs, jres)
        return res

    def alone(self, key, name):
        jobs = self._jobs(key)
        if jobs:
            self._deliver(jobs, run_job(MultiJob([j for j, _, _ in jobs]), name))


def _local_step(x, target, layers, final_norm, n_ex, plan):
    T, D = x.shape
    L = len(layers)
    cst = _attn_consts(T // n_ex)
    ident = lambda j: j
    EV_A, EV_B = 3 * (D // 2), QW + 2 * KW
    OD_W = D // 2
    wt = plan.weight

    saved = []
    for l, W in enumerate(layers):
        s = dict(x0=x)
        x1, *s["gu1"] = plan.run(("ffn1_fwd", l), ffn_fwd, x, W["n1"], wt("f1_in_t", l), wt("f1_out", l))
        if l % 2 == 0:
            pa, pb, h = mm_nt(x1, wt("mx_in_t", l), [(0, EV_A), (EV_A, EV_B)], F32, True, W["nm"])
            qg = jnp.tile(W["q_norm"], N_Q_HEADS)[None]
            kg = jnp.tile(W["k_norm"], N_KV_HEADS)[None]
            mix = conv_fwd(pa, W["conv_w"], n_ex)
            qp, kp, vp = qkv_prep_fwd(pb, qg, kg, cst, n_ex)
            mix, op, lse = plan.run(("attn_fwd", l), attn_fwd, qp, kp, vp, cst["gather"], n_ex, mix)
            s.update(pa=pa, pb=pb, qg=qg, kg=kg, qp=qp, kp=kp, vp=vp, op=op, lse=lse)
        else:
            p, u, v, h = mm_nt(x1, wt("mx_in_t", l), [(0, OD_W), (OD_W, OD_W), (2 * OD_W, OD_W)], F32, True, W["nm"])
            scale = W["pool_scale"][None]
            sn = W["sgu_norm"][None]
            b_full = jnp.broadcast_to(W["sgu_b"][..., None], W["sgu_w"].shape)
            mix = sgu_fwd(u, v, sn, W["sgu_w"], b_full, pool_fwd(p, W["pool_w"], scale, n_ex))
            s.update(p=p, u=u, v=v, scale=scale, sn=sn, b_full=b_full)
        x2 = mm_nn([(mix, 0, D, 0)], wt("mx_out", l), residual=x1)
        x3, *s["gu2"] = plan.run(("ffn2_fwd", l), ffn_fwd, x2, W["n2"], wt("f2_in_t", l), wt("f2_out", l))
        s.update(x1=x1, x2=x2, h=h, mix=mix)
        saved.append(s)
        x = x3

    loss, dx, d_final = loss_head(x, final_norm, target)

    small = [None] * L

    def ffn_back(which, l, dout, xin, gain, gu_xn, sm, sm_key):
        w_in, w_out = wt(which + "_in_t", l), wt(which + "_out", l)
        F = w_out.shape[0]
        nc = F // FFN_TN
        gu, xn = gu_xn
        dxi, sm[sm_key], a, dgu, dob = plan.run((which + "_bwd", l), ffn_bwd_x, dout, xin, gain, gu, w_in, w_out)
        if which == "f1" and l == 0:
            plan.small_ready(small, d_final)
        plan.grad(which + "_in_t", l, plan.run(
            (which + "_in_grad", l), mm_tn, dgu, xn, 2 * F, lambda k, c: k * nc + c, grid=(2, nc),
            col_block=lambda k, c: 2 * c + k))
        plan.grad(which + "_out", l, plan.run((which + "_out_grad", l), mm_tn, a, dob, F, ident))
        keys = [(which + "_in_t", l), (which + "_out", l)]
        if which == "f1" and l == 0:
            plan.scatter(keys)
            return dxi
        return plan.scatter(keys, dxi)

    for l in reversed(range(L)):
        s, W = saved[l], layers[l]
        sm = small[l] = {}
        dx = ffn_back("f2", l, dx, s["x2"], W["n2"], s["gu2"], sm, "n2")
        dmix, dxb = mm_nt(dx, wt("mx_out", l), [(0, D)], BF16, emit_a_bf16=True)
        plan.grad("mx_out", l, mm_tn(s["mix"], dxb, D, ident))
        if l % 2 == 0:
            d_proj, sm["conv_w"] = conv_bwd(s["pa"], W["conv_w"], n_ex, dmix, EV_A + EV_B)
            dq, dkp, dvp = plan.run(("attn_bwd", l), attn_bwd, s["qp"], s["kp"], s["vp"], s["op"], s["lse"], dmix, cst, n_ex)
            d_proj, dqg, dkg = qkv_prep_bwd(s["pb"], s["qg"], s["kg"], cst, n_ex, dq, dkp, dvp, d_proj)
            d_pieces = [(d_proj, 0, EV_A + EV_B, 0)]
            plan.grad("mx_in_t", l, mm_tn(d_proj, s["h"], EV_A + EV_B, ident))
            sm["q_norm"] = dqg.reshape(N_Q_HEADS, HEAD_DIM).sum(0)
            sm["k_norm"] = dkg.reshape(N_KV_HEADS, HEAD_DIM).sum(0)
        else:
            d_proj, d_sn, sm["sgu_w"], d_sb = sgu_bwd(s["u"], s["v"], s["sn"], W["sgu_w"], s["b_full"], dmix)
            d_proj, sm["pool_w"], d_ps = pool_bwd(s["p"], W["pool_w"], s["scale"], n_ex, dmix, d_proj)
            d_pieces = [(d_proj, 0, OD_W, 1), (d_proj, 1, OD_W, 2), (d_proj, 2, OD_W, 0)]
            nb = OD_W // MM_TC
            plan.grad("mx_in_t", l, mm_tn(d_proj, s["h"], 3 * OD_W,
                                          lambda jj: jnp.where(jj < 2 * nb, jj + nb, jj - 2 * nb)))
            sm["pool_scale"], sm["sgu_norm"], sm["sgu_b"] = d_ps[0], d_sn[0], d_sb.sum(-1)
        dx = plan.scatter([("mx_out", l), ("mx_in_t", l)], dx)
        dx, sm["nm"] = mm_nn(d_pieces, wt("mx_in_t", l), norm_bwd=(s["x1"], W["nm"], dx))
        dx = ffn_back("f1", l, dx, s["x0"], W["n1"], s["gu1"], sm, "n1")
    return loss, dx


def all_gather(srcs):
    return run_job(GatherJob(srcs), "all_gather")


HBM_SPEC = pl.BlockSpec(memory_space=pltpu.HBM)
SEM_SPEC = pl.BlockSpec(memory_space=pltpu.SEMAPHORE)
SPLIT_COPY = pltpu.CompilerParams(has_side_effects=pltpu.SideEffectType.DATAFLOW_SIDE_EFFECTING)


def _scatter_copies(srcs, lands, send, recv, dims):
    x, y, c, me = _my_place()
    for t, (r, _) in enumerate(dims):
        for k in range(1, N_DEV):
            peer, pidx = _peer(x, y, c, k)
            rows = srcs[t].at[pl.ds(pl.multiple_of(pidx * r, 8), r), :]
            yield (_remote(rows, lands[t].at[me], send, recv, N_DEV * t + k, peer),
                   _remote(rows, lands[t].at[pidx], send, recv, N_DEV * t + k, peer))


def scatter_start(grads, tag, carry=None):
    n = len(grads)
    dims = [(g.shape[0] // N_DEV, g.shape[1]) for g in grads]
    passed = list(grads) + [lax.empty((N_DEV, r, cc), g.dtype) for g, (r, cc) in zip(grads, dims)]
    passed += [] if carry is None else [carry]
    m = len(passed)

    def body(*refs):
        srcs, lands, send, recv, token = refs[:n], refs[n:2 * n], refs[m], refs[m + 1], refs[-1]
        for mine, _ in _scatter_copies(srcs, lands, send, recv, dims):
            mine.start()
        token[...] = jnp.zeros_like(token)

    res = pl.pallas_call(
        body, name="scatter_start_" + tag,
        out_shape=(pltpu.SemaphoreType.DMA((N_DEV * n,)), pltpu.SemaphoreType.DMA((N_DEV * n,)),
                   *[pltpu.HBM(a.shape, a.dtype) for a in passed], jax.ShapeDtypeStruct((8, LANES), F32)),
        in_specs=[HBM_SPEC] * m,
        out_specs=(SEM_SPEC, SEM_SPEC, *([HBM_SPEC] * m), pl.BlockSpec(memory_space=pltpu.VMEM)),
        input_output_aliases={i: 2 + i for i in range(m)}, compiler_params=SPLIT_COPY,
    )(*[pltpu.with_memory_space_constraint(a, pltpu.HBM) for a in passed])
    return res[0], res[1], res[2:2 + n], res[2 + n:2 + 2 * n], (None if carry is None else res[2 + 2 * n]), res[-1]


def scatter_wait(send, recv, grads, zones, after, tag):
    n = len(grads)
    dims = [(g.shape[0] // N_DEV, g.shape[1]) for g in grads]

    def body(*refs):
        srcs, lands, send_ref, recv_ref = refs[:n], refs[n:2 * n], refs[2 * n], refs[2 * n + 1]
        for mine, theirs in _scatter_copies(srcs, lands, send_ref, recv_ref, dims):
            mine.wait_send()
            theirs.wait_recv()

    res = pl.pallas_call(
        body, name="scatter_wait_" + tag,
        out_shape=(*[pltpu.HBM(g.shape, g.dtype) for g in grads], *[pltpu.HBM(z.shape, z.dtype) for z in zones]),
        in_specs=[HBM_SPEC] * (2 * n) + [SEM_SPEC, SEM_SPEC, ANY], out_specs=[HBM_SPEC] * (2 * n),
        input_output_aliases={i: i for i in range(2 * n)}, compiler_params=SPLIT_COPY,
    )(*grads, *zones, send, recv, after)
    return res[:n], res[n:]


def cast_shards(w):
    L, A, B = w.shape

    def body(w_ref, o_ref):
        o_ref[...] = w_ref[...].astype(BF16)

    return pl.pallas_call(
        body, name="cast_shards", grid=(L,),
        in_specs=[pl.BlockSpec((None, A, B), lambda l: (l, 0, 0))],
        out_specs=pl.BlockSpec((None, A, B), lambda l: (l, 0, 0)),
        out_shape=jax.ShapeDtypeStruct((L, A, B), BF16), compiler_params=_cp("parallel"),
    )(w)


ADAM_TC = 256


def adamw(parts, w, m, v, l, prev=None, own=None, after=None):
    P, R, C = parts.shape
    tc = _tile(C, ADAM_TC)
    c1, c2 = 1.0 - ADAM_B1 ** ADAM_STEP, 1.0 - ADAM_B2 ** ADAM_STEP
    prev = list(prev) if prev is not None else []

    def body(*refs):
        me_ref, refs = (refs[0], refs[1:]) if own is not None else (None, refs)
        p_ref, w_ref, m_ref, v_ref = refs[:4]
        g_ref, d_ref, mo_ref, vo_ref = refs[-4:]
        if own is None:
            term = lambda s: p_ref[s].astype(F32)
        else:
            term = lambda s: jnp.where(me_ref[0] == s, refs[4][...], p_ref[s]).astype(F32)
        g = term(0)
        for s in range(1, P):
            g = g + term(s)
        m1 = ADAM_B1 * m_ref[...] + (1.0 - ADAM_B1) * g
        v1 = ADAM_B2 * v_ref[...] + (1.0 - ADAM_B2) * (g * g)
        g_ref[...] = g
        mo_ref[...] = m1
        vo_ref[...] = v1
        d_ref[...] = -ADAM_LR * ((m1 / c1) / (jnp.sqrt(v1 / c2) + ADAM_EPS) + ADAM_WD * w_ref[...])

    wspec = pl.BlockSpec((None, R, tc), lambda i, *_: (l, 0, i))
    pspec = pl.BlockSpec((P, R, tc), lambda i, *_: (0, 0, i))
    extra = prev + ([] if after is None else [after])
    out_shape = [jax.ShapeDtypeStruct(w.shape, F32)] * 4
    if own is None:
        return pl.pallas_call(
            body, name="adamw", grid=(C // tc,), in_specs=[pspec, wspec, wspec, wspec] + [ANY] * len(extra),
            out_specs=[wspec] * 4, out_shape=out_shape,
            input_output_aliases={4 + i: i for i in range(len(prev))}, compiler_params=_cp("parallel"),
        )(parts, w, m, v, *extra)
    own_sums, me = own
    ospec = pl.BlockSpec((None, R, tc), lambda i, me_ref: (me_ref[0], 0, i))
    return pl.pallas_call(
        body, name="adamw_own", out_shape=out_shape,
        grid_spec=pltpu.PrefetchScalarGridSpec(
            num_scalar_prefetch=1, grid=(C // tc,),
            in_specs=[pspec, wspec, wspec, wspec, ospec] + [ANY] * len(extra), out_specs=[wspec] * 4),
        input_output_aliases={6 + i: i for i in range(len(prev))}, compiler_params=_cp("parallel"),
    )(me, parts, w, m, v, own_sums, *extra)


_WEIGHTS = ['ffn1_norm', 'ffn1_w_in', 'ffn1_w_out', 'mix_norm', 'ffn2_norm', 'ffn2_w_in', 'ffn2_w_out', 'ev_w_in',
            'ev_conv_w', 'ev_q_norm', 'ev_k_norm', 'ev_w_out', 'od_w_in', 'od_pool_w', 'od_pool_scale', 'od_sgu_norm',
            'od_sgu_w', 'od_sgu_b', 'od_w_out', 'final_norm']
_BIG = dict(ffn1_w_in=True, ffn1_w_out=False, ffn2_w_in=True, ffn2_w_out=False,
            ev_w_in=True, ev_w_out=False, od_w_in=True, od_w_out=False)
_SMALL_SHARDED = ['ev_conv_w', 'od_pool_scale', 'od_sgu_norm']
_SMALL = [n for n in _WEIGHTS if n not in _BIG]
_PACK_ROWS = 8 * LANES


_KINDS = ("f1_in_t", "f1_out", "mx_in_t", "mx_out", "f2_in_t", "f2_out")
_CARRIER_US = dict(ffn1_fwd=105, ffn2_fwd=105, attn_fwd=115)
_GATHER_US_PER_ROW = 0.08
SMALL_CARRIER = ("f1_in_grad", 0)


def _schedule(L, rows):
    events = []
    for l in range(L):
        events += [("ffn1_fwd", l), ("mixer", l)] + ([("attn_fwd", l)] if l % 2 == 0 else []) + [("ffn2_fwd", l)]
    consumer = {"f1": "ffn1_fwd", "mx": "mixer", "f2": "ffn2_fwd"}
    queue = [(k, l) for l in range(L) for k in _KINDS]
    pos = {t: events.index((consumer[t[0][:2]], t[1])) for t in queue}
    gathers = {"first": [t for t in queue if pos[t] == 0]}
    queue = [t for t in queue if pos[t] > 0]
    carriers = [i for i, e in enumerate(events) if e[0] in _CARRIER_US]
    for i in carriers:
        budget, take = _CARRIER_US[events[i][0]], []
        later = [j for j in carriers if j > i]
        while queue:
            t = queue[0]
            cost = rows(*t) * _GATHER_US_PER_ROW
            forced = not any(j < pos[t] for j in later)
            if not forced and cost > budget:
                break
            take.append(queue.pop(0))
            budget -= cost
        if take:
            gathers[events[i]] = take
    assert not queue
    return gathers


def _pack(arrs):
    flat = jnp.concatenate([a.reshape(-1) for a in arrs])
    pad = (-flat.shape[0]) % _PACK_ROWS
    return jnp.pad(flat, (0, pad)).reshape(-1, LANES)


def _unpack(buf, shapes):
    flat, out, off = buf.reshape(-1), [], 0
    for s in shapes:
        n = math.prod(s)
        out.append(flat[off:off + n].reshape(s))
        off += n
    return out


def _unshard_last(g, lead):
    nd = len(lead)
    return jnp.moveaxis(g, 0, nd).reshape(*lead, -1)


def kernel(x, ffn1_norm, ffn1_w_in, ffn1_w_out, mix_norm, ffn2_norm, ffn2_w_in, ffn2_w_out, ev_w_in, ev_conv_w, ev_q_norm, ev_k_norm, ev_w_out, od_w_in, od_pool_w, od_pool_scale, od_sgu_norm, od_sgu_w, od_sgu_b, od_w_out, final_norm, loss_target, m_ffn1_norm, m_ffn1_w_in, m_ffn1_w_out, m_mix_norm, m_ffn2_norm, m_ffn2_w_in, m_ffn2_w_out, m_ev_w_in, m_ev_conv_w, m_ev_q_norm, m_ev_k_norm, m_ev_w_out, m_od_w_in, m_od_pool_w, m_od_pool_scale, m_od_sgu_norm, m_od_sgu_w, m_od_sgu_b, m_od_w_out, m_final_norm, v_ffn1_norm, v_ffn1_w_in, v_ffn1_w_out, v_mix_norm, v_ffn2_norm, v_ffn2_w_in, v_ffn2_w_out, v_ev_w_in, v_ev_conv_w, v_ev_q_norm, v_ev_k_norm, v_ev_w_out, v_od_w_in, v_od_pool_w, v_od_pool_scale, v_od_sgu_norm, v_od_sgu_w, v_od_sgu_b, v_od_w_out, v_final_norm):
    w = dict(zip(_WEIGHTS, (ffn1_norm, ffn1_w_in, ffn1_w_out, mix_norm, ffn2_norm, ffn2_w_in, ffn2_w_out, ev_w_in, ev_conv_w, ev_q_norm, ev_k_norm, ev_w_out, od_w_in, od_pool_w, od_pool_scale, od_sgu_norm, od_sgu_w, od_sgu_b, od_w_out, final_norm)))
    m = dict(zip(_WEIGHTS, (m_ffn1_norm, m_ffn1_w_in, m_ffn1_w_out, m_mix_norm, m_ffn2_norm, m_ffn2_w_in, m_ffn2_w_out, m_ev_w_in, m_ev_conv_w, m_ev_q_norm, m_ev_k_norm, m_ev_w_out, m_od_w_in, m_od_pool_w, m_od_pool_scale, m_od_sgu_norm, m_od_sgu_w, m_od_sgu_b, m_od_w_out, m_final_norm)))
    v = dict(zip(_WEIGHTS, (v_ffn1_norm, v_ffn1_w_in, v_ffn1_w_out, v_mix_norm, v_ffn2_norm, v_ffn2_w_in, v_ffn2_w_out, v_ev_w_in, v_ev_conv_w, v_ev_q_norm, v_ev_k_norm, v_ev_w_out, v_od_w_in, v_od_pool_w, v_od_pool_scale, v_od_sgu_norm, v_od_sgu_w, v_od_sgu_b, v_od_w_out, v_final_norm)))
    n_ex, seq, D = x.shape
    T = n_ex * seq
    L = ffn1_norm.shape[0]
    me = 4 * lax.axis_index("x") + 2 * lax.axis_index("y") + lax.axis_index("c")

    sh_small = [w[n] for n in _SMALL_SHARDED]
    packed = all_gather([(_pack(sh_small), None)])[0].reshape(N_DEV, -1)
    full_small = {}
    off = 0
    for n, a in zip(_SMALL_SHARDED, sh_small):
        cnt = math.prod(a.shape)
        full_small[n] = _unshard_last(packed[:, off:off + cnt].reshape((N_DEV,) + a.shape), a.shape[:-1])
        off += cnt

    tr = lambda a: jnp.swapaxes(a, 1, 2)
    wmv = {n: tuple(tr(d[n]) if t else d[n] for d in (w, m, v)) for n, t in _BIG.items()}
    shards = {n: cast_shards(wmv[n][0]) for n in _BIG}

    def name_of(kind, l):
        mx = "ev" if l % 2 == 0 else "od"
        return {"f1_in_t": "ffn1_w_in", "f1_out": "ffn1_w_out", "f2_in_t": "ffn2_w_in", "f2_out": "ffn2_w_out",
                "mx_in_t": mx + "_w_in", "mx_out": mx + "_w_out"}[kind], (l // 2 if kind.startswith("mx") else l)

    def shard(kind, l):
        name, idx = name_of(kind, l)
        return shards[name], idx

    g_shapes = {}

    def pack_small(small, d_final):
        ev = [sm for l, sm in enumerate(small) if l % 2 == 0]
        od = [sm for l, sm in enumerate(small) if l % 2 == 1]
        st = lambda sms, k: jnp.stack([sm[k] for sm in sms])
        g_full = dict(ffn1_norm=st(small, "n1")[:, 0], mix_norm=st(small, "nm")[:, 0], ffn2_norm=st(small, "n2")[:, 0],
                      ev_conv_w=st(ev, "conv_w"), ev_q_norm=st(ev, "q_norm"), ev_k_norm=st(ev, "k_norm"),
                      od_pool_w=st(od, "pool_w"), od_pool_scale=st(od, "pool_scale"), od_sgu_norm=st(od, "sgu_norm"),
                      od_sgu_w=st(od, "sgu_w"), od_sgu_b=st(od, "sgu_b"), final_norm=d_final[0])
        g_shapes.update({n: g_full[n].shape for n in _SMALL})
        return _pack([g_full[n] for n in _SMALL])

    gathers = _schedule(L, lambda kind, l: shards[name_of(kind, l)[0]].shape[1])
    plan = Plan(shard, gathers, SMALL_CARRIER, pack_small)
    layers = []
    for l in range(L):
        j = l // 2
        W = dict(n1=ffn1_norm[l][None], nm=mix_norm[l][None], n2=ffn2_norm[l][None])
        if l % 2 == 0:
            W.update(conv_w=full_small["ev_conv_w"][j], q_norm=ev_q_norm[j], k_norm=ev_k_norm[j])
        else:
            W.update(pool_w=od_pool_w[j], pool_scale=full_small["od_pool_scale"][j], sgu_norm=full_small["od_sgu_norm"][j],
                     sgu_w=od_sgu_w[j], sgu_b=od_sgu_b[j])
        layers.append(W)

    plan.alone("first", "gather_first")
    loss, dx = _local_step(x.reshape(T, D), loss_target.reshape(T, D), layers, final_norm[None], n_ex, plan)

    out = {n: None for n in _BIG}
    me1 = me.astype(jnp.int32).reshape(1)

    def update(arrived):
        for (kind, l), (own, parts) in arrived.items():
            name, idx = name_of(kind, l)
            out[name] = adamw(parts, *wmv[name], idx, prev=out[name], own=(own.reshape(parts.shape), me1))

    *earlier, last = plan.started
    update(plan.collect(earlier, last[5]))

    g8 = plan.small_parts.reshape(N_DEV, -1)
    cols, off = [], 0
    for n in _SMALL:
        cnt = math.prod(g_shapes[n])
        g = g8[:, off:off + cnt].reshape((N_DEV,) + g_shapes[n])
        off += cnt
        if n in _SMALL_SHARDED:
            width = w[n].shape[-1]
            g = lax.dynamic_slice_in_dim(g, me * width, width, axis=g.ndim - 1)
        cols.append(g.reshape(N_DEV, -1))
    g8 = jnp.concatenate(cols, axis=1)
    g8 = jnp.pad(g8, ((0, 0), (0, (-g8.shape[1]) % _PACK_ROWS))).reshape(N_DEV, -1, LANES)
    pk = lambda d: _pack([d[n] for n in _SMALL])[None]
    small_out = adamw(g8, pk(w), pk(m), pk(v), 0)

    update(plan.collect([last], small_out[0]))
    out = {n: [tr(a) if _BIG[n] else a for a in res] for n, res in out.items()}

    shapes = [w[n].shape for n in _SMALL]
    for i in range(4):
        for n, a in zip(_SMALL, _unpack(small_out[i], shapes)):
            out.setdefault(n, [None] * 4)[i] = a

    total = lax.psum(loss[0, 0], ("x", "y", "c"))
    return (total, dx.reshape(n_ex, seq, D), *[out[n][0] for n in _WEIGHTS], *[out[n][1] for n in _WEIGHTS],
            *[out[n][2] for n in _WEIGHTS], *[out[n][3] for n in _WEIGHTS])
```

```python
import functools
import math

import jax
import jax.numpy as jnp
from jax import lax
from jax.experimental import pallas as pl
from jax.experimental.pallas import tpu as pltpu

F32, BF16 = jnp.float32, jnp.bfloat16
EPS = 1e-6
N_DEV = 8
V7X_VMEM_BYTES = 64 * 1024 * 1024
VMEM_LIMIT = V7X_VMEM_BYTES - 8 * 1024 * 1024
LANES = 128
HEAD_DIM = 64
N_Q_HEADS = 8
N_KV_HEADS = 2
Q_PER_KV = N_Q_HEADS // N_KV_HEADS
GRID_W = 64
ROPE_THETA = 10000.0
POOL_RADII = (1, 2, 4, 8)
SGU_CHUNK = 128
GROUP = 128
ADAM_LR, ADAM_B1, ADAM_B2, ADAM_EPS, ADAM_WD, ADAM_STEP = 0.001, 0.9, 0.999, 1e-08, 0.01, 10
MESH_ID = pl.DeviceIdType.MESH


def _cp(*sem):
    return pltpu.CompilerParams(dimension_semantics=sem, vmem_limit_bytes=VMEM_LIMIT)


def _dot(a, b, ca, cb):
    return lax.dot_general(a, b, (((ca,), (cb,)), ((), ())), preferred_element_type=F32)


def _nn(a, b):
    return _dot(a, b, 1, 0)


def _nt(a, b):
    return _dot(a, b, 1, 1)


def _tn(a, b):
    return _dot(a, b, 0, 0)


def _split_mm(x, m):
    hi = x.astype(BF16)
    lo = (x - hi.astype(F32)).astype(BF16)
    return _nn(hi, m) + _nn(lo, m)


def _tile(n, pref):
    t = min(n, pref)
    assert n % t == 0, (n, pref)
    return t


ANY = pl.BlockSpec(memory_space=pl.ANY)
JOB_MIDDLE = 0.55


def _my_place():
    x, y, c = lax.axis_index("x"), lax.axis_index("y"), lax.axis_index("c")
    return x, y, c, 4 * x + 2 * y + c


def _peer(x, y, c, k):
    px = 1 - x if k & 4 else x
    py = 1 - y if k & 2 else y
    pc = 1 - c if k & 1 else c
    return (px, py, pc), 4 * px + 2 * py + pc


def _remote(src, dst, send_sems, recv_sems, i, peer):
    return pltpu.make_async_remote_copy(src_ref=src, dst_ref=dst, send_sem=send_sems.at[i], recv_sem=recv_sems.at[i],
                                        device_id=peer, device_id_type=MESH_ID)


class GatherJob:
    def __init__(self, srcs):
        self.srcs = srcs
        self.args = [a for a, _ in srcs]
        self.dims = [a.shape[-2:] for a, _ in srcs]
        n = self.n_in = self.n_out = len(srcs)
        self.out_shape = [jax.ShapeDtypeStruct((N_DEV * r, cc), a.dtype) for (a, _), (r, cc) in zip(srcs, self.dims)]
        self.scratch = [pltpu.SemaphoreType.DMA((N_DEV * n,)), pltpu.SemaphoreType.DMA((N_DEV * n,)),
                        pltpu.SemaphoreType.DMA((n,))]

    def _rows(self, outs, t, idx):
        r = self.dims[t][0]
        return outs[t].at[pl.ds(pl.multiple_of(idx * r, 8), r), :]

    def _local(self, ins, outs, loc, t, me):
        src = ins[t] if self.srcs[t][1] is None else ins[t].at[self.srcs[t][1]]
        return src, pltpu.make_async_copy(src, self._rows(outs, t, me), loc.at[t])

    def start(self, ins, outs, sems):
        send, recv, loc = sems
        x, y, c, me = _my_place()
        for t in range(self.n_in):
            src, local = self._local(ins, outs, loc, t, me)
            local.start()
            for k in (2, 4, 1):
                _remote(src, self._rows(outs, t, me), send, recv, N_DEV * t + k, _peer(x, y, c, k)[0]).start()

    def _copy(self, outs, sems, t, origin, i, to):
        x, y, c, _ = _my_place()
        blk = self._rows(outs, t, _peer(x, y, c, origin)[1])
        return _remote(blk, blk, sems[0], sems[1], N_DEV * t + i, _peer(x, y, c, to)[0])

    def middle(self, ins, outs, sems):
        c = _my_place()[2]

        def relay(t, got, to):
            self._copy(outs, sems, t, got, got, got).wait_recv()
            self._copy(outs, sems, t, got, 6, to).start()
            self._copy(outs, sems, t, to, to, to).wait_recv()

        for t in range(self.n_in):
            pl.when(c == 1)(functools.partial(relay, t, 2, 4))
            pl.when(c == 0)(functools.partial(relay, t, 4, 2))
            for k in (2, 4):
                self._copy(outs, sems, t, k, k + 1, 1).start()

    def finish(self, ins, outs, sems):
        send, recv, loc = sems
        x, y, c, me = _my_place()
        for t in range(self.n_in):
            self._copy(outs, sems, t, 6, 6, 6).wait_recv()
            self._copy(outs, sems, t, 6, 7, 1).start()
        for t in range(self.n_in):
            for k in range(1, N_DEV):
                peer, pidx = _peer(x, y, c, k)
                blk = self._rows(outs, t, pidx)
                if k % 2 == 1:
                    _remote(blk, blk, send, recv, N_DEV * t + k, peer).wait_recv()
                _remote(blk, blk, send, recv, N_DEV * t + k, peer).wait_send()
            self._local(ins, outs, loc, t, me)[1].wait()


def _call(body, name, grid, in_specs, args, out_specs, out_shape, scratch=(), sem=(), job=None, aliases=None):
    in_specs, out_specs, out_shape, scratch = list(in_specs), list(out_specs), list(out_shape), list(scratch)
    n_in, n_out, n_scr = len(args), len(out_shape), len(scratch)
    if job is None:
        res = pl.pallas_call(body, name=name, grid=grid, in_specs=in_specs, out_specs=out_specs, out_shape=out_shape,
                             scratch_shapes=scratch, input_output_aliases=aliases or {}, compiler_params=_cp(*sem))(*args)
        return res, None
    o0 = n_in + job.n_in
    s0 = o0 + n_out + job.n_out

    def carrier(*refs):
        jin, jout, jsem = refs[n_in:o0], refs[o0 + n_out:s0], refs[s0 + n_scr:]
        ids = [pl.program_id(a) for a in range(len(grid))]
        def at(step):
            idx = []
            for g in reversed(grid):
                idx.append(step % g)
                step //= g
            return functools.reduce(jnp.logical_and, [i == j for i, j in zip(ids, reversed(idx))])

        steps = math.prod(grid)
        if grid:
            pl.when(at(0))(lambda: job.start(jin, jout, jsem))
            pl.when(at(int(steps * JOB_MIDDLE)))(lambda: job.middle(jin, jout, jsem))
        else:
            job.start(jin, jout, jsem)
            job.middle(jin, jout, jsem)
        body(*refs[:n_in], *refs[o0:o0 + n_out], *refs[s0:s0 + n_scr])
        if grid:
            pl.when(at(steps - 1))(lambda: job.finish(jin, jout, jsem))
        else:
            job.finish(jin, jout, jsem)

    res = pl.pallas_call(
        carrier, name=name + "_comm", grid=grid, in_specs=in_specs + [ANY] * job.n_in,
        out_specs=out_specs + [ANY] * job.n_out, out_shape=out_shape + job.out_shape,
        scratch_shapes=scratch + job.scratch, input_output_aliases=aliases or {},
        compiler_params=_cp(*(["arbitrary"] * len(grid))))(*args, *job.args)
    return res[:n_out], res[n_out:]


def run_job(job, name):
    return _call(lambda: None, name, (), [], [], [], [], job=job)[1]


@jax.custom_vjp
def bmm(x, w):
    return _nn(x.astype(BF16), w.astype(BF16))


def _bmm_fwd(x, w):
    return bmm(x, w), (x, w)


def _bmm_bwd(res, g):
    x, w = res
    gb = g.astype(BF16)
    return _nt(gb, w.astype(BF16)), _tn(x.astype(BF16), gb)


bmm.defvjp(_bmm_fwd, _bmm_bwd)


def _shift_raw(x, d):
    n = x.shape[0]
    r = pltpu.roll(x, d % n, axis=0)
    row = lax.broadcasted_iota(jnp.int32, x.shape, 0)
    keep = (row >= d) if d > 0 else (row < n + d)
    return jnp.where(keep, r, 0.0)


def shift_rows(x, d):
    @jax.custom_vjp
    def f(v):
        return _shift_raw(v, d)

    f.defvjp(lambda v: (_shift_raw(v, d), None), lambda _, g: (_shift_raw(g, -d),))
    return f(x)


def _swap_raw(x):
    n = x.shape[1]
    nxt = pltpu.roll(x, n - 1, axis=1)
    prv = pltpu.roll(x, 1, axis=1)
    lane = lax.broadcasted_iota(jnp.int32, x.shape, 1)
    return jnp.where(lane % 2 == 0, nxt, prv)


@jax.custom_vjp
def swap_pairs(x):
    return _swap_raw(x)


swap_pairs.defvjp(lambda x: (_swap_raw(x), None), lambda _, g: (_swap_raw(g),))


@jax.custom_vjp
def group_mean(x, bd):
    return _split_mm(x, bd)


group_mean.defvjp(lambda x, bd: (_split_mm(x, bd), bd), lambda bd, g: (_split_mm(g, bd), jnp.zeros_like(bd)))


def _rope_norm(x, gain, cos, sgn, bd, scale):
    xn = x * lax.rsqrt(group_mean(x * x, bd) + EPS) * gain
    return (xn * cos + swap_pairs(xn) * sgn) * scale


def _conv_gate(gb, gc, hc, w):
    z = gc * hc
    c = shift_rows(z, 1) * w[0:1] + z * w[1:2] + shift_rows(z, -1) * w[2:3]
    return gb * c


def _window_sum(p, r):
    b = f = p
    k = 1
    while k < r:
        b = b + shift_rows(b, k)
        f = f + shift_rows(f, -k)
        k *= 2
    return b + f - p + shift_rows(p, r) + shift_rows(p, -r)


def _pool_mix(p, pool_w, scale):
    n = p.shape[0]
    t = lax.broadcasted_iota(jnp.int32, (n, 1), 0)
    outs = []
    for gi, r in enumerate(POOL_RADII):
        pg = p[:, gi * GROUP:(gi + 1) * GROUP]
        cnt = (jnp.minimum(t + r, n - 1) - jnp.maximum(t - r, 0) + 1).astype(F32)
        pooled = _window_sum(pg, r) / cnt - pg
        outs.append(bmm(pooled, pool_w[gi]))
    return jnp.concatenate(outs, axis=1) * scale


def _sgu(u, v, norm_g, w_s, b_full):
    ug = jax.nn.gelu(u)
    vg = jax.nn.gelu(v)
    vn = vg * lax.rsqrt(jnp.mean(vg * vg, axis=-1, keepdims=True) + EPS) * norm_g
    cols = []
    for g in range(w_s.shape[0]):
        rows = []
        for n in range(u.shape[0] // SGU_CHUNK):
            blk = vn[n * SGU_CHUNK:(n + 1) * SGU_CHUNK, g * GROUP:(g + 1) * GROUP]
            rows.append(bmm(w_s[g], blk) + b_full[g])
        cols.append(jnp.concatenate(rows, axis=0))
    return ug * jnp.concatenate(cols, axis=1)


def _rms_bwd_math(xv, gain, dy, dres):
    r = lax.rsqrt(jnp.mean(xv * xv, axis=-1, keepdims=True) + EPS)
    xh = xv * r
    dxh = dy * gain
    dx = dres + r * (dxh - xh * jnp.mean(dxh * xh, axis=-1, keepdims=True))
    return dx, jnp.sum(dy * xh, axis=0, keepdims=True)


FFN_TN = 256


def ffn_fwd(x, gain, wt_in, w_out, job=None):
    T, D = x.shape
    F = w_out.shape[0]
    tm, tn = _tile(T, 1024), FFN_TN
    nc = F // tn

    def body(x_ref, gn_ref, wg_ref, wu_ref, wo_ref, y_ref, gu_ref, xn_s, acc_s):
        c = pl.program_id(1)

        @pl.when(c == 0)
        def _():
            xv = x_ref[...]
            r = lax.rsqrt(jnp.mean(xv * xv, axis=-1, keepdims=True) + EPS)
            xn_s[...] = (xv * r * gn_ref[...]).astype(BF16)
            acc_s[...] = jnp.zeros_like(acc_s)

        xn = xn_s[...]
        g = _nt(xn, wg_ref[...])
        u = _nt(xn, wu_ref[...])
        gu_ref[:, :tn] = g.astype(BF16)
        gu_ref[:, tn:] = u.astype(BF16)
        a = (g * jax.nn.sigmoid(g) * u).astype(BF16)
        acc_s[...] += _nn(a, wo_ref[...])

        @pl.when(c == nc - 1)
        def _():
            y_ref[...] = x_ref[...] + 0.5 * acc_s[...]

    row = pl.BlockSpec((tm, D), lambda i, c: (i, 0))
    return _call(
        body, "ffn_fwd", (T // tm, nc),
        [row, pl.BlockSpec((1, D), lambda i, c: (0, 0)),
         pl.BlockSpec((tn, D), lambda i, c: (c, 0)),
         pl.BlockSpec((tn, D), lambda i, c: (c + nc, 0)),
         pl.BlockSpec((tn, D), lambda i, c: (c, 0))],
        [x, gain, wt_in, wt_in, w_out],
        [row, pl.BlockSpec((tm, 2 * tn), lambda i, c: (i, c)), row],
        [jax.ShapeDtypeStruct((T, D), F32), jax.ShapeDtypeStruct((T, 2 * F), BF16), jax.ShapeDtypeStruct((T, D), BF16)],
        [pltpu.VMEM((tm, D), F32)], ("parallel", "arbitrary"), job)


def ffn_bwd_x(dout, x, gain, gu, wt_in, w_out, job=None):
    T, D = x.shape
    F = w_out.shape[0]
    tm, tn = _tile(T, 1024), FFN_TN
    nc = F // tn

    def body(do_ref, x_ref, gn_ref, gu_ref, wg_ref, wu_ref, wo_ref,
             dx_ref, dgn_ref, a_ref, dgu_ref, dob_ref, acc_s):
        i, c = pl.program_id(0), pl.program_id(1)

        @pl.when(c == 0)
        def _():
            dob_ref[...] = (0.5 * do_ref[...]).astype(BF16)
            acc_s[...] = jnp.zeros_like(acc_s)

        da = jnp.concatenate([_nt(dob_ref[:tm // 2, :], wo_ref[...]), _nt(dob_ref[tm // 2:, :], wo_ref[...])], axis=0)
        g = gu_ref[:, :tn].astype(F32)
        u = gu_ref[:, tn:].astype(F32)
        sig = jax.nn.sigmoid(g)
        sl = g * sig
        a_ref[...] = (sl * u).astype(BF16)
        dg = (da * u * (sig * (1.0 + g * (1.0 - sig)))).astype(BF16)
        du = (da * sl).astype(BF16)
        dgu_ref[:, :tn] = dg
        dgu_ref[:, tn:] = du
        acc_s[...] += _nn(dg, wg_ref[...]) + _nn(du, wu_ref[...])

        @pl.when(c == nc - 1)
        def _():
            dx, dgn = _rms_bwd_math(x_ref[...], gn_ref[...], acc_s[...], do_ref[...])
            dx_ref[...] = dx

            @pl.when(i == 0)
            def _():
                dgn_ref[...] = jnp.zeros_like(dgn_ref)

            dgn_ref[...] += dgn

    row = pl.BlockSpec((tm, D), lambda i, c: (i, 0))
    return _call(
        body, "ffn_bwd_x", (T // tm, nc),
        [row, row, pl.BlockSpec((1, D), lambda i, c: (0, 0)),
         pl.BlockSpec((tm, 2 * tn), lambda i, c: (i, c)),
         pl.BlockSpec((tn, D), lambda i, c: (c, 0)),
         pl.BlockSpec((tn, D), lambda i, c: (c + nc, 0)),
         pl.BlockSpec((tn, D), lambda i, c: (c, 0))],
        [dout, x, gain, gu, wt_in, wt_in, w_out],
        [row, pl.BlockSpec((1, D), lambda i, c: (0, 0)),
         pl.BlockSpec((tm, tn), lambda i, c: (i, c)),
         pl.BlockSpec((tm, 2 * tn), lambda i, c: (i, c)), row],
        [jax.ShapeDtypeStruct((T, D), F32), jax.ShapeDtypeStruct((1, D), F32),
         jax.ShapeDtypeStruct((T, F), BF16), jax.ShapeDtypeStruct((T, 2 * F), BF16),
         jax.ShapeDtypeStruct((T, D), BF16)],
        [pltpu.VMEM((tm, D), F32)], ("arbitrary", "arbitrary"), job)


MM_TM = 512
MM_TC = 256


def mm_nt(a, wt, pieces, out_dtype, emit_a_bf16=False, norm_gain=None):
    T, K = a.shape
    tm = _tile(T, MM_TM)
    npc = len(pieces)
    n_lead = 1 if norm_gain is None else 2

    def body(*refs):
        a_ref, w_refs, o_refs = refs[0], refs[n_lead:n_lead + npc], refs[n_lead + npc:]
        av = a_ref[...]
        if norm_gain is not None:
            av = av * lax.rsqrt(jnp.mean(av * av, axis=-1, keepdims=True) + EPS) * refs[1][...]
        ab = av.astype(BF16)
        for w_ref, o_ref in zip(w_refs, o_refs[:npc]):
            o_ref[...] = _nt(ab, w_ref[...]).astype(o_ref.dtype)
        if emit_a_bf16:
            o_refs[npc][...] = ab

    in_specs = [pl.BlockSpec((tm, K), lambda i: (i, 0))]
    if norm_gain is not None:
        in_specs.append(pl.BlockSpec((1, K), lambda i: (0, 0)))
    out_specs, out_shape = [], []
    for r0, n in pieces:
        assert r0 % n == 0
        in_specs.append(pl.BlockSpec((n, K), functools.partial(lambda i, b: (b, 0), b=r0 // n)))
        out_specs.append(pl.BlockSpec((tm, n), lambda i: (i, 0)))
        out_shape.append(jax.ShapeDtypeStruct((T, n), out_dtype))
    if emit_a_bf16:
        out_specs.append(pl.BlockSpec((tm, K), lambda i: (i, 0)))
        out_shape.append(jax.ShapeDtypeStruct((T, K), BF16))
    return pl.pallas_call(
        body, name="mm_nt", grid=(T // tm,), in_specs=in_specs, out_specs=out_specs, out_shape=out_shape,
        compiler_params=_cp("parallel"),
    )(a, *([] if norm_gain is None else [norm_gain]), *([wt] * npc))


def mm_nn(pieces, w, residual=None, norm_bwd=None):
    T = pieces[0][0].shape[0]
    N = w.shape[1]
    tm = _tile(T, MM_TM)
    na = len(pieces)

    def body(*refs):
        a_refs, w_refs = refs[:na], refs[na:2 * na]
        acc = refs[2 * na][...] if residual is not None else None
        for a_ref, w_ref in zip(a_refs, w_refs):
            t = _nn(a_ref[...].astype(BF16), w_ref[...])
            acc = t if acc is None else acc + t
        if norm_bwd is None:
            refs[-1][...] = acc
            return
        x_ref, g_ref, dr_ref, dx_ref, dg_ref = refs[-5:]
        dx, dg = _rms_bwd_math(x_ref[...], g_ref[...], acc, dr_ref[...])
        dx_ref[...] = dx

        @pl.when(pl.program_id(0) == 0)
        def _():
            dg_ref[...] = jnp.zeros_like(dg_ref)

        dg_ref[...] += dg

    in_specs, w_specs = [], []
    for a, cb, k, rb in pieces:
        in_specs.append(pl.BlockSpec((tm, k), functools.partial(lambda i, b: (i, b), b=cb)))
        w_specs.append(pl.BlockSpec((k, N), functools.partial(lambda i, b: (b, 0), b=rb)))
    assert sum(k for _, _, k, _ in pieces) == w.shape[0]
    args = [a for a, _, _, _ in pieces] + [w] * na
    in_specs = in_specs + w_specs
    row = pl.BlockSpec((tm, N), lambda i: (i, 0))
    if residual is not None:
        in_specs.append(row)
        args.append(residual)
    if norm_bwd is None:
        return pl.pallas_call(
            body, name="mm_nn", grid=(T // tm,), in_specs=in_specs, out_specs=row,
            out_shape=jax.ShapeDtypeStruct((T, N), F32), compiler_params=_cp("parallel"),
        )(*args)
    vec = pl.BlockSpec((1, N), lambda i: (0, 0))
    return pl.pallas_call(
        body, name="mm_nn_norm_bwd", grid=(T // tm,), in_specs=in_specs + [row, vec, row], out_specs=[row, vec],
        out_shape=[jax.ShapeDtypeStruct((T, N), F32), jax.ShapeDtypeStruct((1, N), F32)],
        compiler_params=_cp("arbitrary"),
    )(*args, *norm_bwd)


def mm_tn(a, b, n_rows, row_block, prev=None, grid=None, col_block=None, job=None):
    T, M = a.shape
    N = b.shape[1]
    tc = MM_TC
    assert M % tc == 0 and n_rows % tc == 0
    if grid is None:
        grid, col_block = (M // tc,), (lambda j: j)

    def body(*refs):
        a_ref, b_ref, o_ref = refs[0], refs[1], refs[-1]
        o_ref[...] = _tn(a_ref[...], b_ref[...]).astype(BF16)

    in_specs = [pl.BlockSpec((T, tc), lambda *g: (0, col_block(*g))), pl.BlockSpec((T, N), lambda *g: (0, 0))]
    args = [a, b]
    aliases = {}
    if prev is not None:
        in_specs.append(pl.BlockSpec(memory_space=pl.ANY))
        args.append(prev)
        aliases = {2: 0}
    res, jres = _call(body, "mm_tn", grid, in_specs, args, [pl.BlockSpec((tc, N), lambda *g: (row_block(*g), 0))],
                      [jax.ShapeDtypeStruct((n_rows, N), BF16)], (), ["parallel"] * len(grid), job, aliases)
    return res[0] if job is None else (res[0], jres)


def conv_fwd(proj_a, conv_w, n_ex):
    T, C3 = proj_a.shape
    C = C3 // 3
    S = T // n_ex

    def body(gb_ref, gc_ref, hc_ref, w_ref, o_ref):
        o_ref[...] = _conv_gate(gb_ref[...], gc_ref[...], hc_ref[...], w_ref[...]).astype(BF16)

    col = lambda k: pl.BlockSpec((S, C), functools.partial(lambda b, kk: (b, kk), kk=k))
    return pl.pallas_call(
        body, name="conv_fwd", grid=(n_ex,),
        in_specs=[col(0), col(1), col(2), pl.BlockSpec((3, C), lambda b: (0, 0))],
        out_specs=pl.BlockSpec((S, C), lambda b: (b, 0)),
        out_shape=jax.ShapeDtypeStruct((T, 2 * C), BF16), compiler_params=_cp("parallel"),
    )(proj_a, proj_a, proj_a, conv_w)


def conv_bwd(proj_a, conv_w, n_ex, dy, total_cols):
    T, C3 = proj_a.shape
    C = C3 // 3
    S = T // n_ex

    def body(gb_ref, gc_ref, hc_ref, w_ref, dy_ref, dp_ref, dw_ref):
        _, vjp = jax.vjp(_conv_gate, gb_ref[...], gc_ref[...], hc_ref[...], w_ref[...])
        dgb, dgc, dhc, dw = vjp(dy_ref[...].astype(F32))
        dp_ref[:, 0:C] = dgb.astype(BF16)
        dp_ref[:, C:2 * C] = dgc.astype(BF16)
        dp_ref[:, 2 * C:] = dhc.astype(BF16)

        @pl.when(pl.program_id(0) == 0)
        def _():
            dw_ref[...] = jnp.zeros_like(dw_ref)

        dw_ref[...] += dw

    col = lambda k: pl.BlockSpec((S, C), functools.partial(lambda b, kk: (b, kk), kk=k))
    return pl.pallas_call(
        body, name="conv_bwd", grid=(n_ex,),
        in_specs=[col(0), col(1), col(2), pl.BlockSpec((3, C), lambda b: (0, 0)),
                  pl.BlockSpec((S, C), lambda b: (b, 0))],
        out_specs=[pl.BlockSpec((S, C3), lambda b: (b, 0)), pl.BlockSpec((3, C), lambda b: (0, 0))],
        out_shape=[jax.ShapeDtypeStruct((T, total_cols), BF16), jax.ShapeDtypeStruct((3, C), F32)],
        compiler_params=_cp("arbitrary"),
    )(proj_a, proj_a, proj_a, conv_w, dy)


QW = N_Q_HEADS * HEAD_DIM
KW = N_KV_HEADS * HEAD_DIM
QP = N_Q_HEADS * LANES
KP = N_KV_HEADS * LANES


def _attn_consts(seq):
    rows = seq // GRID_W
    r_idx, c_idx = jnp.meshgrid(jnp.arange(rows), jnp.arange(GRID_W), indexing='ij')
    r_idx = r_idx.reshape(-1).astype(F32)
    c_idx = c_idx.reshape(-1).astype(F32)
    n_freq = HEAD_DIM // 4
    inv = ROPE_THETA ** (-jnp.arange(n_freq, dtype=F32) / n_freq)
    ang = jnp.concatenate([r_idx[:, None] * inv, c_idx[:, None] * inv], axis=-1)
    cos = jnp.repeat(jnp.cos(ang), 2, axis=1)
    sin = jnp.repeat(jnp.sin(ang), 2, axis=1)
    sgn = sin * jnp.tile(jnp.array([-1.0, 1.0], F32), HEAD_DIM // 2)
    cos = jnp.tile(cos, (1, N_Q_HEADS))
    sgn = jnp.tile(sgn, (1, N_Q_HEADS))
    lane = jnp.arange(QW)
    bd = jnp.where(lane[:, None] // HEAD_DIM == lane[None, :] // HEAD_DIM, 1.0 / HEAD_DIM, 0.0).astype(BF16)
    dst = (lane // HEAD_DIM) * LANES + lane % HEAD_DIM
    spread = (dst[:, None] == jnp.arange(QP)[None, :]).astype(BF16)
    return dict(cos=cos, sgn=sgn, bd=bd, spread=spread, gather=spread.T)


def qkv_prep_fwd(proj_b, qg, kg, cst, n_ex):
    T = proj_b.shape[0]
    S = T // n_ex
    tm = _tile(S, 512)
    nb = S // tm

    def body(p_ref, qg_ref, kg_ref, cos_ref, sgn_ref, bd_ref, sp_ref, q_ref, k_ref, v_ref):
        pv = p_ref[...]
        cos, sgn, bd, sp = cos_ref[...], sgn_ref[...], bd_ref[...], sp_ref[...]
        qr = _rope_norm(pv[:, :QW], qg_ref[...], cos, sgn, bd, HEAD_DIM ** -0.5)
        kr = _rope_norm(pv[:, QW:QW + KW], kg_ref[...], cos[:, :KW], sgn[:, :KW], bd[:KW, :KW], 1.0)
        q_ref[...] = _nn(qr.astype(BF16), sp).astype(BF16)
        k_ref[...] = _nn(kr.astype(BF16), sp[:KW, :KP]).astype(BF16)
        v_ref[...] = _nn(pv[:, QW + KW:].astype(BF16), sp[:KW, :KP]).astype(BF16)

    full = lambda a: pl.BlockSpec(a.shape, lambda i: (0,) * a.ndim)
    tab = pl.BlockSpec((tm, QW), lambda i: (i % nb, 0))
    return pl.pallas_call(
        body, name="qkv_prep_fwd", grid=(T // tm,),
        in_specs=[pl.BlockSpec((tm, QW + 2 * KW), lambda i: (i, 0)), full(qg), full(kg), tab, tab,
                  full(cst["bd"]), full(cst["spread"])],
        out_specs=[pl.BlockSpec((tm, QP), lambda i: (i, 0)), pl.BlockSpec((tm, KP), lambda i: (i, 0)),
                   pl.BlockSpec((tm, KP), lambda i: (i, 0))],
        out_shape=[jax.ShapeDtypeStruct((T, QP), BF16), jax.ShapeDtypeStruct((T, KP), BF16),
                   jax.ShapeDtypeStruct((T, KP), BF16)],
        compiler_params=_cp("parallel"),
    )(proj_b, qg, kg, cst["cos"], cst["sgn"], cst["bd"], cst["spread"])


def qkv_prep_bwd(proj_b, qg, kg, cst, n_ex, dq, dk_pad, dv_pad, d_proj):
    T = proj_b.shape[0]
    S = T // n_ex
    tm = _tile(S, 512)
    nb = S // tm

    def body(p_ref, qg_ref, kg_ref, cos_ref, sgn_ref, bd_ref, ga_ref, dq_ref, dk_ref, dv_ref, _kept,
             dp_ref, dqg_ref, dkg_ref):
        pv = p_ref[...]
        cos, sgn, bd, ga = cos_ref[...], sgn_ref[...], bd_ref[...], ga_ref[...]
        fq = lambda q, g: _rope_norm(q, g, cos, sgn, bd, HEAD_DIM ** -0.5)
        fk = lambda k, g: _rope_norm(k, g, cos[:, :KW], sgn[:, :KW], bd[:KW, :KW], 1.0)
        _, vq = jax.vjp(fq, pv[:, :QW], qg_ref[...])
        _, vk = jax.vjp(fk, pv[:, QW:QW + KW], kg_ref[...])
        dqp, dqg = vq(dq_ref[...])
        dkp, dkg = vk(_split_mm(dk_ref[...], ga[:KP, :KW]))
        dp_ref[:, :QW] = dqp.astype(BF16)
        dp_ref[:, QW:QW + KW] = dkp.astype(BF16)
        dp_ref[:, QW + KW:] = _split_mm(dv_ref[...], ga[:KP, :KW]).astype(BF16)

        @pl.when(pl.program_id(0) == 0)
        def _():
            dqg_ref[...] = jnp.zeros_like(dqg_ref)
            dkg_ref[...] = jnp.zeros_like(dkg_ref)

        dqg_ref[...] += dqg
        dkg_ref[...] += dkg

    full = lambda a: pl.BlockSpec(a.shape, lambda i: (0,) * a.ndim)
    tab = pl.BlockSpec((tm, QW), lambda i: (i % nb, 0))
    row = lambda n: pl.BlockSpec((tm, n), lambda i: (i, 0))
    wb = QW + 2 * KW
    assert d_proj.shape[1] % wb == 0
    last = d_proj.shape[1] // wb - 1
    return pl.pallas_call(
        body, name="qkv_prep_bwd", grid=(T // tm,),
        in_specs=[row(wb), full(qg), full(kg), tab, tab, full(cst["bd"]), full(cst["gather"]),
                  row(QW), row(KP), row(KP), ANY],
        out_specs=[pl.BlockSpec((tm, wb), lambda i: (i, last)), pl.BlockSpec((1, QW), lambda i: (0, 0)),
                   pl.BlockSpec((1, KW), lambda i: (0, 0))],
        out_shape=[jax.ShapeDtypeStruct(d_proj.shape, BF16), jax.ShapeDtypeStruct((1, QW), F32),
                   jax.ShapeDtypeStruct((1, KW), F32)],
        input_output_aliases={10: 0}, compiler_params=_cp("arbitrary"),
    )(proj_b, qg, kg, cst["cos"], cst["sgn"], cst["bd"], cst["gather"], dq, dk_pad, dv_pad, d_proj)


ATT_TQ = 256
ATT_TQ_FWD = 512


def attn_fwd(qp, kp, vp, gather, n_ex, mix, job=None):
    T = qp.shape[0]
    S = T // n_ex
    tq = _tile(S, ATT_TQ_FWD)
    nq = S // tq

    def body(q_ref, k_ref, v_ref, ga_ref, _kept, o_ref, op_ref, lse_ref):
        lane = lax.broadcasted_iota(jnp.int32, (tq, LANES), 1)
        lse_all = jnp.zeros((tq, LANES), F32)
        for h in range(N_Q_HEADS):
            kv = h // Q_PER_KV
            qh = q_ref[:, h * LANES:(h + 1) * LANES]
            s = _nt(qh, k_ref[:, kv * LANES:(kv + 1) * LANES])
            m = jnp.max(s, axis=-1, keepdims=True)
            p = jnp.exp(s - m)
            lsum = jnp.sum(p, axis=-1, keepdims=True)
            o = _nn(p.astype(BF16), v_ref[:, kv * LANES:(kv + 1) * LANES]) / lsum
            op_ref[:, h * LANES:(h + 1) * LANES] = o.astype(BF16)
            lse_all = jnp.where(lane == h, m + jnp.log(lsum), lse_all)
        lse_ref[...] = lse_all
        o_ref[...] = _nn(op_ref[...], ga_ref[...]).astype(BF16)

    blk = lambda n: pl.BlockSpec((tq, n), lambda b, i: (b * nq + i, 0))
    kvs = pl.BlockSpec((S, KP), lambda b, i: (b, 0))
    return _call(
        body, "attn_fwd", (n_ex, nq),
        [blk(QP), kvs, kvs, pl.BlockSpec(gather.shape, lambda b, i: (0, 0)), ANY], [qp, kp, vp, gather, mix],
        [pl.BlockSpec((tq, QW), lambda b, i: (b * nq + i, 1)), blk(QP), blk(LANES)],
        [jax.ShapeDtypeStruct(mix.shape, BF16), jax.ShapeDtypeStruct((T, QP), BF16),
         jax.ShapeDtypeStruct((T, LANES), F32)], (), ("parallel", "parallel"), job, {4: 0})


def attn_bwd(qp, kp, vp, op, lse, do, cst, n_ex, job=None):
    T = qp.shape[0]
    S = T // n_ex
    tq = _tile(S, ATT_TQ)
    nq = S // tq

    def body(q_ref, k_ref, v_ref, op_ref, lse_ref, do_ref, sp_ref, ga_ref, dq_ref, dk_ref, dv_ref, dqp_s):
        @pl.when(pl.program_id(1) == 0)
        def _():
            dk_ref[...] = jnp.zeros_like(dk_ref)
            dv_ref[...] = jnp.zeros_like(dv_ref)

        lane = lax.broadcasted_iota(jnp.int32, (tq, LANES), 1)
        dop = _nn(do_ref[...], sp_ref[...]).astype(BF16)
        lse_all = lse_ref[...]
        for h in range(N_Q_HEADS):
            kv = h // Q_PER_KV
            hs = slice(h * LANES, (h + 1) * LANES)
            ks = slice(kv * LANES, (kv + 1) * LANES)
            qh, kk, vv = q_ref[:, hs], k_ref[:, ks], v_ref[:, ks]
            doh = dop[:, hs]
            lse_h = jnp.sum(jnp.where(lane == h, lse_all, 0.0), axis=-1, keepdims=True)
            p = jnp.exp(_nt(qh, kk) - lse_h)
            dp = _nt(doh, vv)
            delta = jnp.sum(doh.astype(F32) * op_ref[:, hs].astype(F32), axis=-1, keepdims=True)
            ds = (p * (dp - delta)).astype(BF16)
            dqp_s[:, hs] = _nn(ds, kk)
            dk_ref[:, ks] += _tn(ds, qh)
            dv_ref[:, ks] += _tn(p.astype(BF16), doh)
        dq_ref[...] = _split_mm(dqp_s[...], ga_ref[...])

    blk = lambda n: pl.BlockSpec((tq, n), lambda b, i: (b * nq + i, 0))
    kvs = pl.BlockSpec((S, KP), lambda b, i: (b, 0))
    full = lambda a: pl.BlockSpec(a.shape, lambda b, i: (0, 0))
    return _call(
        body, "attn_bwd", (n_ex, nq),
        [blk(QP), kvs, kvs, blk(QP), blk(LANES), pl.BlockSpec((tq, QW), lambda b, i: (b * nq + i, 1)),
         full(cst["spread"]), full(cst["gather"])],
        [qp, kp, vp, op, lse, do, cst["spread"], cst["gather"]],
        [blk(QW), kvs, kvs],
        [jax.ShapeDtypeStruct((T, QW), F32), jax.ShapeDtypeStruct((T, KP), F32), jax.ShapeDtypeStruct((T, KP), F32)],
        [pltpu.VMEM((tq, QP), F32)], ("arbitrary", "arbitrary"), job)


def pool_fwd(p, pool_w, scale, n_ex):
    T, W = p.shape
    S = T // n_ex

    def body(p_ref, w_ref, s_ref, o_ref):
        o_ref[...] = _pool_mix(p_ref[...], w_ref[...], s_ref[...]).astype(BF16)

    return pl.pallas_call(
        body, name="pool_fwd", grid=(n_ex,),
        in_specs=[pl.BlockSpec((S, W), lambda b: (b, 0)),
                  pl.BlockSpec(pool_w.shape, lambda b: (0, 0, 0)),
                  pl.BlockSpec((1, W), lambda b: (0, 0))],
        out_specs=pl.BlockSpec((S, W), lambda b: (b, 0)),
        out_shape=jax.ShapeDtypeStruct((T, 2 * W), BF16), compiler_params=_cp("parallel"),
    )(p, pool_w, scale)


def pool_bwd(p, pool_w, scale, n_ex, dy, d_proj):
    T, W = p.shape
    S = T // n_ex
    last = d_proj.shape[1] // W - 1

    def body(p_ref, w_ref, s_ref, dy_ref, _kept, dp_ref, dw_ref, ds_ref):
        _, vjp = jax.vjp(_pool_mix, p_ref[...], w_ref[...], s_ref[...])
        dp, dw, ds = vjp(dy_ref[...].astype(F32))
        dp_ref[...] = dp.astype(BF16)

        @pl.when(pl.program_id(0) == 0)
        def _():
            dw_ref[...] = jnp.zeros_like(dw_ref)
            ds_ref[...] = jnp.zeros_like(ds_ref)

        dw_ref[...] += dw
        ds_ref[...] += ds

    wshape = pool_w.shape
    return pl.pallas_call(
        body, name="pool_bwd", grid=(n_ex,),
        in_specs=[pl.BlockSpec((S, W), lambda b: (b, 0)),
                  pl.BlockSpec(wshape, lambda b: (0, 0, 0)),
                  pl.BlockSpec((1, W), lambda b: (0, 0)), pl.BlockSpec((S, W), lambda b: (b, 0)), ANY],
        out_specs=[pl.BlockSpec((S, W), lambda b: (b, last)), pl.BlockSpec(wshape, lambda b: (0, 0, 0)),
                   pl.BlockSpec((1, W), lambda b: (0, 0))],
        out_shape=[jax.ShapeDtypeStruct(d_proj.shape, BF16), jax.ShapeDtypeStruct(wshape, F32),
                   jax.ShapeDtypeStruct((1, W), F32)],
        input_output_aliases={4: 0}, compiler_params=_cp("arbitrary"),
    )(p, pool_w, scale, dy, d_proj)


SGU_TS = 512


def sgu_fwd(u, v, norm_g, w_s, b_full, mix):
    T, W = u.shape
    ts = _tile(T, SGU_TS)

    def body(u_ref, v_ref, g_ref, w_ref, b_ref, _kept, o_ref):
        o_ref[...] = _sgu(u_ref[...], v_ref[...], g_ref[...], w_ref[...], b_ref[...]).astype(BF16)

    row = pl.BlockSpec((ts, W), lambda i: (i, 0))
    wsp = pl.BlockSpec(w_s.shape, lambda i: (0, 0, 0))
    return pl.pallas_call(
        body, name="sgu_fwd", grid=(T // ts,),
        in_specs=[row, row, pl.BlockSpec((1, W), lambda i: (0, 0)), wsp, wsp, ANY],
        out_specs=pl.BlockSpec((ts, W), lambda i: (i, 1)), out_shape=jax.ShapeDtypeStruct(mix.shape, BF16),
        input_output_aliases={5: 0}, compiler_params=_cp("parallel"),
    )(u, v, norm_g, w_s, b_full, mix)


def sgu_bwd(u, v, norm_g, w_s, b_full, dy):
    T, W = u.shape
    ts = _tile(T, SGU_TS)
    wshape = w_s.shape

    def body(u_ref, v_ref, g_ref, w_ref, b_ref, dy_ref, duv_ref, dg_ref, dw_ref, db_ref):
        _, vjp = jax.vjp(_sgu, u_ref[...], v_ref[...], g_ref[...], w_ref[...], b_ref[...])
        du, dv, dg, dw, db = vjp(dy_ref[...].astype(F32))
        duv_ref[:, :W] = du.astype(BF16)
        duv_ref[:, W:] = dv.astype(BF16)

        @pl.when(pl.program_id(0) == 0)
        def _():
            dg_ref[...] = jnp.zeros_like(dg_ref)
            dw_ref[...] = jnp.zeros_like(dw_ref)
            db_ref[...] = jnp.zeros_like(db_ref)

        dg_ref[...] += dg
        dw_ref[...] += dw
        db_ref[...] += db

    row = pl.BlockSpec((ts, W), lambda i: (i, 0))
    wsp = pl.BlockSpec(wshape, lambda i: (0, 0, 0))
    wout = pl.BlockSpec(wshape, lambda i: (0, 0, 0))
    vec = pl.BlockSpec((1, W), lambda i: (0, 0))
    return pl.pallas_call(
        body, name="sgu_bwd", grid=(T // ts,),
        in_specs=[row, row, vec, wsp, wsp, pl.BlockSpec((ts, W), lambda i: (i, 1))],
        out_specs=[pl.BlockSpec((ts, 2 * W), lambda i: (i, 0)), vec, wout, wout],
        out_shape=[jax.ShapeDtypeStruct((T, 3 * W), BF16),
                   jax.ShapeDtypeStruct((1, W), F32), jax.ShapeDtypeStruct(wshape, F32),
                   jax.ShapeDtypeStruct(wshape, F32)],
        compiler_params=_cp("arbitrary"),
    )(u, v, norm_g, w_s, b_full, dy)


def loss_head(x, gain, target):
    T, D = x.shape
    tm = _tile(T, 512)

    def body(x_ref, g_ref, t_ref, loss_ref, dx_ref, dg_ref):
        xv, g = x_ref[...], g_ref[...]
        r = lax.rsqrt(jnp.mean(xv * xv, axis=-1, keepdims=True) + EPS)
        err = xv * r * g - t_ref[...]
        part = 0.5 * jnp.sum(jnp.mean(err * err, axis=-1, keepdims=True), axis=0, keepdims=True)
        dx, dg = _rms_bwd_math(xv, g, err * (1.0 / D), jnp.zeros_like(xv))
        dx_ref[...] = dx

        @pl.when(pl.program_id(0) == 0)
        def _():
            loss_ref[...] = jnp.zeros_like(loss_ref)
            dg_ref[...] = jnp.zeros_like(dg_ref)

        loss_ref[...] += part
        dg_ref[...] += dg

    row = pl.BlockSpec((tm, D), lambda i: (i, 0))
    vec = pl.BlockSpec((1, D), lambda i: (0, 0))
    return pl.pallas_call(
        body, name="loss_head", grid=(T // tm,),
        in_specs=[row, vec, row], out_specs=[pl.BlockSpec((1, 1), lambda i: (0, 0)), row, vec],
        out_shape=[jax.ShapeDtypeStruct((1, 1), F32), jax.ShapeDtypeStruct((T, D), F32),
                   jax.ShapeDtypeStruct((1, D), F32)],
        compiler_params=_cp("arbitrary"),
    )(x, gain, target)


class MultiJob:
    def __init__(self, jobs):
        self.jobs = jobs
        self.args = [a for j in jobs for a in j.args]
        self.out_shape = [s for j in jobs for s in j.out_shape]
        self.scratch = [s for j in jobs for s in j.scratch]
        self.n_in, self.n_out = len(self.args), len(self.out_shape)

    def _each(self, ins, outs, sems):
        i = o = s = 0
        for j in self.jobs:
            yield j, ins[i:i + j.n_in], outs[o:o + j.n_out], sems[s:s + len(j.scratch)]
            i, o, s = i + j.n_in, o + j.n_out, s + len(j.scratch)

    def start(self, ins, outs, sems):
        for j, a, b, c in self._each(ins, outs, sems):
            j.start(a, b, c)

    def middle(self, ins, outs, sems):
        for j, a, b, c in self._each(ins, outs, sems):
            j.middle(a, b, c)

    def finish(self, ins, outs, sems):
        for j, a, b, c in self._each(ins, outs, sems):
            j.finish(a, b, c)

    def split(self, results):
        o = 0
        for j in self.jobs:
            yield results[o:o + j.n_out]
            o += j.n_out


class Plan:
    def __init__(self, shard, gathers, small_carrier=None, pack_small=None, exchange=True):
        self.shard, self.gathers, self.exchange = shard, gathers, exchange
        self.small_carrier, self.pack_small = small_carrier, pack_small
        self.weights, self.grads, self.started = {}, {}, []
        self.small_src = self.small_parts = None

    def scatter(self, keys, carry=None):
        if not self.exchange:
            return carry
        tag = "_".join(f"{kind}{l}" for kind, l in keys)
        send, recv, grads, zones, carry, token = scatter_start([self.grads[k] for k in keys], tag, carry)
        self.started.append((keys, send, recv, grads, zones, token, tag))
        return carry

    def collect(self, groups, after):
        got = {}
        for keys, send, recv, grads, zones, _, tag in groups:
            grads, zones = scatter_wait(send, recv, grads, zones, after, tag)
            got.update({k: (g, z) for k, g, z in zip(keys, grads, zones)})
        return got

    def weight(self, kind, l):
        return self.weights[(kind, l)]

    def grad(self, kind, l, g):
        self.grads[(kind, l)] = g

    def small_ready(self, small, d_final):
        if self.pack_small is not None:
            self.small_src = self.pack_small(small, d_final)

    def _jobs(self, key):
        jobs = []
        if key in self.gathers:
            ks = self.gathers[key]
            jobs.append((GatherJob([self.shard(*k) for k in ks]), self.weights, ks))
        if key == self.small_carrier and self.small_src is not None:
            jobs.append((GatherJob([(self.small_src, None)]), None, None))
        return jobs

    def _deliver(self, jobs, results):
        multi = MultiJob([j for j, _, _ in jobs])
        for (_, store, ks), res in zip(jobs, multi.split(results)):
            if store is None:
                self.small_parts = res[0]
            else:
                store.update(zip(ks, res))

    def run(self, key, fn, *args, **kw):
        jobs = self._jobs(key)
        if not jobs:
            out = fn(*args, **kw)
            return out if fn is mm_tn else out[0]
        res, jres = fn(*args, job=MultiJob([j for j, _, _ in jobs]), **kw)
        self._deliver(jobs, jres)
        return res

    def alone(self, key, name):
        jobs = self._jobs(key)
        if jobs:
            self._deliver(jobs, run_job(MultiJob([j for j, _, _ in jobs]), name))


def _local_step(x, target, layers, final_norm, n_ex, plan):
    T, D = x.shape
    L = len(layers)
    cst = _attn_consts(T // n_ex)
    ident = lambda j: j
    EV_A, EV_B = 3 * (D // 2), QW + 2 * KW
    OD_W = D // 2
    wt = plan.weight

    saved = []
    for l, W in enumerate(layers):
        s = dict(x0=x)
        x1, *s["gu1"] = plan.run(("ffn1_fwd", l), ffn_fwd, x, W["n1"], wt("f1_in_t", l), wt("f1_out", l))
        if l % 2 == 0:
            pa, pb, h = mm_nt(x1, wt("mx_in_t", l), [(0, EV_A), (EV_A, EV_B)], F32, True, W["nm"])
            qg = jnp.tile(W["q_norm"], N_Q_HEADS)[None]
            kg = jnp.tile(W["k_norm"], N_KV_HEADS)[None]
            mix = conv_fwd(pa, W["conv_w"], n_ex)
            qp, kp, vp = qkv_prep_fwd(pb, qg, kg, cst, n_ex)
            mix, op, lse = plan.run(("attn_fwd", l), attn_fwd, qp, kp, vp, cst["gather"], n_ex, mix)
            s.update(pa=pa, pb=pb, qg=qg, kg=kg, qp=qp, kp=kp, vp=vp, op=op, lse=lse)
        else:
            p, u, v, h = mm_nt(x1, wt("mx_in_t", l), [(0, OD_W), (OD_W, OD_W), (2 * OD_W, OD_W)], F32, True, W["nm"])
            scale = W["pool_scale"][None]
            sn = W["sgu_norm"][None]
            b_full = jnp.broadcast_to(W["sgu_b"][..., None], W["sgu_w"].shape)
            mix = sgu_fwd(u, v, sn, W["sgu_w"], b_full, pool_fwd(p, W["pool_w"], scale, n_ex))
            s.update(p=p, u=u, v=v, scale=scale, sn=sn, b_full=b_full)
        x2 = mm_nn([(mix, 0, D, 0)], wt("mx_out", l), residual=x1)
        x3, *s["gu2"] = plan.run(("ffn2_fwd", l), ffn_fwd, x2, W["n2"], wt("f2_in_t", l), wt("f2_out", l))
        s.update(x1=x1, x2=x2, h=h, mix=mix)
        saved.append(s)
        x = x3

    loss, dx, d_final = loss_head(x, final_norm, target)

    small = [None] * L

    def ffn_back(which, l, dout, xin, gain, gu_xn, sm, sm_key):
        w_in, w_out = wt(which + "_in_t", l), wt(which + "_out", l)
        F = w_out.shape[0]
        nc = F // FFN_TN
        gu, xn = gu_xn
        dxi, sm[sm_key], a, dgu, dob = plan.run((which + "_bwd", l), ffn_bwd_x, dout, xin, gain, gu, w_in, w_out)
        if which == "f1" and l == 0:
            plan.small_ready(small, d_final)
        plan.grad(which + "_in_t", l, plan.run(
            (which + "_in_grad", l), mm_tn, dgu, xn, 2 * F, lambda k, c: k * nc + c, grid=(2, nc),
            col_block=lambda k, c: 2 * c + k))
        plan.grad(which + "_out", l, plan.run((which + "_out_grad", l), mm_tn, a, dob, F, ident))
        keys = [(which + "_in_t", l), (which + "_out", l)]
        if which == "f1" and l == 0:
            plan.scatter(keys)
            return dxi
        return plan.scatter(keys, dxi)

    for l in reversed(range(L)):
        s, W = saved[l], layers[l]
        sm = small[l] = {}
        dx = ffn_back("f2", l, dx, s["x2"], W["n2"], s["gu2"], sm, "n2")
        dmix, dxb = mm_nt(dx, wt("mx_out", l), [(0, D)], BF16, emit_a_bf16=True)
        plan.grad("mx_out", l, mm_tn(s["mix"], dxb, D, ident))
        if l % 2 == 0:
            d_proj, sm["conv_w"] = conv_bwd(s["pa"], W["conv_w"], n_ex, dmix, EV_A + EV_B)
            dq, dkp, dvp = plan.run(("attn_bwd", l), attn_bwd, s["qp"], s["kp"], s["vp"], s["op"], s["lse"], dmix, cst, n_ex)
            d_proj, dqg, dkg = qkv_prep_bwd(s["pb"], s["qg"], s["kg"], cst, n_ex, dq, dkp, dvp, d_proj)
            d_pieces = [(d_proj, 0, EV_A + EV_B, 0)]
            plan.grad("mx_in_t", l, mm_tn(d_proj, s["h"], EV_A + EV_B, ident))
            sm["q_norm"] = dqg.reshape(N_Q_HEADS, HEAD_DIM).sum(0)
            sm["k_norm"] = dkg.reshape(N_KV_HEADS, HEAD_DIM).sum(0)
        else:
            d_proj, d_sn, sm["sgu_w"], d_sb = sgu_bwd(s["u"], s["v"], s["sn"], W["sgu_w"], s["b_full"], dmix)
            d_proj, sm["pool_w"], d_ps = pool_bwd(s["p"], W["pool_w"], s["scale"], n_ex, dmix, d_proj)
            d_pieces = [(d_proj, 0, OD_W, 1), (d_proj, 1, OD_W, 2), (d_proj, 2, OD_W, 0)]
            nb = OD_W // MM_TC
            plan.grad("mx_in_t", l, mm_tn(d_proj, s["h"], 3 * OD_W,
                                          lambda jj: jnp.where(jj < 2 * nb, jj + nb, jj - 2 * nb)))
            sm["pool_scale"], sm["sgu_norm"], sm["sgu_b"] = d_ps[0], d_sn[0], d_sb.sum(-1)
        dx = plan.scatter([("mx_out", l), ("mx_in_t", l)], dx)
        dx, sm["nm"] = mm_nn(d_pieces, wt("mx_in_t", l), norm_bwd=(s["x1"], W["nm"], dx))
        dx = ffn_back("f1", l, dx, s["x0"], W["n1"], s["gu1"], sm, "n1")
    return loss, dx


def all_gather(srcs):
    return run_job(GatherJob(srcs), "all_gather")


HBM_SPEC = pl.BlockSpec(memory_space=pltpu.HBM)
SEM_SPEC = pl.BlockSpec(memory_space=pltpu.SEMAPHORE)
SPLIT_COPY = pltpu.CompilerParams(has_side_effects=pltpu.SideEffectType.DATAFLOW_SIDE_EFFECTING)


def _scatter_copies(srcs, lands, send, recv, dims):
    x, y, c, me = _my_place()
    for t, (r, _) in enumerate(dims):
        for k in range(1, N_DEV):
            peer, pidx = _peer(x, y, c, k)
            rows = srcs[t].at[pl.ds(pl.multiple_of(pidx * r, 8), r), :]
            yield (_remote(rows, lands[t].at[me], send, recv, N_DEV * t + k, peer),
                   _remote(rows, lands[t].at[pidx], send, recv, N_DEV * t + k, peer))


def scatter_start(grads, tag, carry=None):
    n = len(grads)
    dims = [(g.shape[0] // N_DEV, g.shape[1]) for g in grads]
    passed = list(grads) + [lax.empty((N_DEV, r, cc), g.dtype) for g, (r, cc) in zip(grads, dims)]
    passed += [] if carry is None else [carry]
    m = len(passed)

    def body(*refs):
        srcs, lands, send, recv, token = refs[:n], refs[n:2 * n], refs[m], refs[m + 1], refs[-1]
        for mine, _ in _scatter_copies(srcs, lands, send, recv, dims):
            mine.start()
        token[...] = jnp.zeros_like(token)

    res = pl.pallas_call(
        body, name="scatter_start_" + tag,
        out_shape=(pltpu.SemaphoreType.DMA((N_DEV * n,)), pltpu.SemaphoreType.DMA((N_DEV * n,)),
                   *[pltpu.HBM(a.shape, a.dtype) for a in passed], jax.ShapeDtypeStruct((8, LANES), F32)),
        in_specs=[HBM_SPEC] * m,
        out_specs=(SEM_SPEC, SEM_SPEC, *([HBM_SPEC] * m), pl.BlockSpec(memory_space=pltpu.VMEM)),
        input_output_aliases={i: 2 + i for i in range(m)}, compiler_params=SPLIT_COPY,
    )(*[pltpu.with_memory_space_constraint(a, pltpu.HBM) for a in passed])
    return res[0], res[1], res[2:2 + n], res[2 + n:2 + 2 * n], (None if carry is None else res[2 + 2 * n]), res[-1]


def scatter_wait(send, recv, grads, zones, after, tag):
    n = len(grads)
    dims = [(g.shape[0] // N_DEV, g.shape[1]) for g in grads]

    def body(*refs):
        srcs, lands, send_ref, recv_ref = refs[:n], refs[n:2 * n], refs[2 * n], refs[2 * n + 1]
        for mine, theirs in _scatter_copies(srcs, lands, send_ref, recv_ref, dims):
            mine.wait_send()
            theirs.wait_recv()

    res = pl.pallas_call(
        body, name="scatter_wait_" + tag,
        out_shape=(*[pltpu.HBM(g.shape, g.dtype) for g in grads], *[pltpu.HBM(z.shape, z.dtype) for z in zones]),
        in_specs=[HBM_SPEC] * (2 * n) + [SEM_SPEC, SEM_SPEC, ANY], out_specs=[HBM_SPEC] * (2 * n),
        input_output_aliases={i: i for i in range(2 * n)}, compiler_params=SPLIT_COPY,
    )(*grads, *zones, send, recv, after)
    return res[:n], res[n:]


def cast_shards(w):
    L, A, B = w.shape

    def body(w_ref, o_ref):
        o_ref[...] = w_ref[...].astype(BF16)

    return pl.pallas_call(
        body, name="cast_shards", grid=(L,),
        in_specs=[pl.BlockSpec((None, A, B), lambda l: (l, 0, 0))],
        out_specs=pl.BlockSpec((None, A, B), lambda l: (l, 0, 0)),
        out_shape=jax.ShapeDtypeStruct((L, A, B), BF16), compiler_params=_cp("parallel"),
    )(w)


ADAM_TC = 128


def adamw(parts, w, m, v, l, prev=None, own=None, after=None):
    P, R, C = parts.shape
    tc = _tile(C, ADAM_TC)
    c1, c2 = 1.0 - ADAM_B1 ** ADAM_STEP, 1.0 - ADAM_B2 ** ADAM_STEP
    prev = list(prev) if prev is not None else []

    def body(*refs):
        me_ref, refs = (refs[0], refs[1:]) if own is not None else (None, refs)
        p_ref, w_ref, m_ref, v_ref = refs[:4]
        g_ref, d_ref, mo_ref, vo_ref = refs[-4:]
        if own is None:
            term = lambda s: p_ref[s].astype(F32)
        else:
            term = lambda s: jnp.where(me_ref[0] == s, refs[4][...], p_ref[s]).astype(F32)
        g = term(0)
        for s in range(1, P):
            g = g + term(s)
        m1 = ADAM_B1 * m_ref[...] + (1.0 - ADAM_B1) * g
        v1 = ADAM_B2 * v_ref[...] + (1.0 - ADAM_B2) * (g * g)
        g_ref[...] = g
        mo_ref[...] = m1
        vo_ref[...] = v1
        d_ref[...] = -ADAM_LR * ((m1 / c1) / (jnp.sqrt(v1 / c2) + ADAM_EPS) + ADAM_WD * w_ref[...])

    wspec = pl.BlockSpec((None, R, tc), lambda i, *_: (l, 0, i))
    pspec = pl.BlockSpec((P, R, tc), lambda i, *_: (0, 0, i))
    extra = prev + ([] if after is None else [after])
    out_shape = [jax.ShapeDtypeStruct(w.shape, F32)] * 4
    if own is None:
        return pl.pallas_call(
            body, name="adamw", grid=(C // tc,), in_specs=[pspec, wspec, wspec, wspec] + [ANY] * len(extra),
            out_specs=[wspec] * 4, out_shape=out_shape,
            input_output_aliases={4 + i: i for i in range(len(prev))}, compiler_params=_cp("parallel"),
        )(parts, w, m, v, *extra)
    own_sums, me = own
    ospec = pl.BlockSpec((None, R, tc), lambda i, me_ref: (me_ref[0], 0, i))
    return pl.pallas_call(
        body, name="adamw_own", out_shape=out_shape,
        grid_spec=pltpu.PrefetchScalarGridSpec(
            num_scalar_prefetch=1, grid=(C // tc,),
            in_specs=[pspec, wspec, wspec, wspec, ospec] + [ANY] * len(extra), out_specs=[wspec] * 4),
        input_output_aliases={6 + i: i for i in range(len(prev))}, compiler_params=_cp("parallel"),
    )(me, parts, w, m, v, own_sums, *extra)


_WEIGHTS = ['ffn1_norm', 'ffn1_w_in', 'ffn1_w_out', 'mix_norm', 'ffn2_norm', 'ffn2_w_in', 'ffn2_w_out', 'ev_w_in',
            'ev_conv_w', 'ev_q_norm', 'ev_k_norm', 'ev_w_out', 'od_w_in', 'od_pool_w', 'od_pool_scale', 'od_sgu_norm',
            'od_sgu_w', 'od_sgu_b', 'od_w_out', 'final_norm']
_BIG = dict(ffn1_w_in=True, ffn1_w_out=False, ffn2_w_in=True, ffn2_w_out=False,
            ev_w_in=True, ev_w_out=False, od_w_in=True, od_w_out=False)
_SMALL_SHARDED = ['ev_conv_w', 'od_pool_scale', 'od_sgu_norm']
_SMALL = [n for n in _WEIGHTS if n not in _BIG]
_PACK_ROWS = 8 * LANES


_KINDS = ("f1_in_t", "f1_out", "mx_in_t", "mx_out", "f2_in_t", "f2_out")
_CARRIER_US = dict(ffn1_fwd=105, ffn2_fwd=105, attn_fwd=115)
_GATHER_US_PER_ROW = 0.08
SMALL_CARRIER = ("f1_in_grad", 0)


def _schedule(L, rows):
    events = []
    for l in range(L):
        events += [("ffn1_fwd", l), ("mixer", l)] + ([("attn_fwd", l)] if l % 2 == 0 else []) + [("ffn2_fwd", l)]
    consumer = {"f1": "ffn1_fwd", "mx": "mixer", "f2": "ffn2_fwd"}
    queue = [(k, l) for l in range(L) for k in _KINDS]
    pos = {t: events.index((consumer[t[0][:2]], t[1])) for t in queue}
    gathers = {"first": [t for t in queue if pos[t] == 0]}
    queue = [t for t in queue if pos[t] > 0]
    carriers = [i for i, e in enumerate(events) if e[0] in _CARRIER_US]
    for i in carriers:
        budget, take = _CARRIER_US[events[i][0]], []
        later = [j for j in carriers if j > i]
        while queue:
            t = queue[0]
            cost = rows(*t) * _GATHER_US_PER_ROW
            forced = not any(j < pos[t] for j in later)
            if not forced and cost > budget:
                break
            take.append(queue.pop(0))
            budget -= cost
        if take:
            gathers[events[i]] = take
    assert not queue
    return gathers


def _pack(arrs):
    flat = jnp.concatenate([a.reshape(-1) for a in arrs])
    pad = (-flat.shape[0]) % _PACK_ROWS
    return jnp.pad(flat, (0, pad)).reshape(-1, LANES)


def _unpack(buf, shapes):
    flat, out, off = buf.reshape(-1), [], 0
    for s in shapes:
        n = math.prod(s)
        out.append(flat[off:off + n].reshape(s))
        off += n
    return out


def _unshard_last(g, lead):
    nd = len(lead)
    return jnp.moveaxis(g, 0, nd).reshape(*lead, -1)


def kernel(x, ffn1_norm, ffn1_w_in, ffn1_w_out, mix_norm, ffn2_norm, ffn2_w_in, ffn2_w_out, ev_w_in, ev_conv_w, ev_q_norm, ev_k_norm, ev_w_out, od_w_in, od_pool_w, od_pool_scale, od_sgu_norm, od_sgu_w, od_sgu_b, od_w_out, final_norm, loss_target, m_ffn1_norm, m_ffn1_w_in, m_ffn1_w_out, m_mix_norm, m_ffn2_norm, m_ffn2_w_in, m_ffn2_w_out, m_ev_w_in, m_ev_conv_w, m_ev_q_norm, m_ev_k_norm, m_ev_w_out, m_od_w_in, m_od_pool_w, m_od_pool_scale, m_od_sgu_norm, m_od_sgu_w, m_od_sgu_b, m_od_w_out, m_final_norm, v_ffn1_norm, v_ffn1_w_in, v_ffn1_w_out, v_mix_norm, v_ffn2_norm, v_ffn2_w_in, v_ffn2_w_out, v_ev_w_in, v_ev_conv_w, v_ev_q_norm, v_ev_k_norm, v_ev_w_out, v_od_w_in, v_od_pool_w, v_od_pool_scale, v_od_sgu_norm, v_od_sgu_w, v_od_sgu_b, v_od_w_out, v_final_norm):
    w = dict(zip(_WEIGHTS, (ffn1_norm, ffn1_w_in, ffn1_w_out, mix_norm, ffn2_norm, ffn2_w_in, ffn2_w_out, ev_w_in, ev_conv_w, ev_q_norm, ev_k_norm, ev_w_out, od_w_in, od_pool_w, od_pool_scale, od_sgu_norm, od_sgu_w, od_sgu_b, od_w_out, final_norm)))
    m = dict(zip(_WEIGHTS, (m_ffn1_norm, m_ffn1_w_in, m_ffn1_w_out, m_mix_norm, m_ffn2_norm, m_ffn2_w_in, m_ffn2_w_out, m_ev_w_in, m_ev_conv_w, m_ev_q_norm, m_ev_k_norm, m_ev_w_out, m_od_w_in, m_od_pool_w, m_od_pool_scale, m_od_sgu_norm, m_od_sgu_w, m_od_sgu_b, m_od_w_out, m_final_norm)))
    v = dict(zip(_WEIGHTS, (v_ffn1_norm, v_ffn1_w_in, v_ffn1_w_out, v_mix_norm, v_ffn2_norm, v_ffn2_w_in, v_ffn2_w_out, v_ev_w_in, v_ev_conv_w, v_ev_q_norm, v_ev_k_norm, v_ev_w_out, v_od_w_in, v_od_pool_w, v_od_pool_scale, v_od_sgu_norm, v_od_sgu_w, v_od_sgu_b, v_od_w_out, v_final_norm)))
    n_ex, seq, D = x.shape
    T = n_ex * seq
    L = ffn1_norm.shape[0]
    me = 4 * lax.axis_index("x") + 2 * lax.axis_index("y") + lax.axis_index("c")

    sh_small = [w[n] for n in _SMALL_SHARDED]
    packed = all_gather([(_pack(sh_small), None)])[0].reshape(N_DEV, -1)
    full_small = {}
    off = 0
    for n, a in zip(_SMALL_SHARDED, sh_small):
        cnt = math.prod(a.shape)
        full_small[n] = _unshard_last(packed[:, off:off + cnt].reshape((N_DEV,) + a.shape), a.shape[:-1])
        off += cnt

    tr = lambda a: jnp.swapaxes(a, 1, 2)
    wmv = {n: tuple(tr(d[n]) if t else d[n] for d in (w, m, v)) for n, t in _BIG.items()}
    shards = {n: cast_shards(wmv[n][0]) for n in _BIG}

    def name_of(kind, l):
        mx = "ev" if l % 2 == 0 else "od"
        return {"f1_in_t": "ffn1_w_in", "f1_out": "ffn1_w_out", "f2_in_t": "ffn2_w_in", "f2_out": "ffn2_w_out",
                "mx_in_t": mx + "_w_in", "mx_out": mx + "_w_out"}[kind], (l // 2 if kind.startswith("mx") else l)

    def shard(kind, l):
        name, idx = name_of(kind, l)
        return shards[name], idx

    g_shapes = {}

    def pack_small(small, d_final):
        ev = [sm for l, sm in enumerate(small) if l % 2 == 0]
        od = [sm for l, sm in enumerate(small) if l % 2 == 1]
        st = lambda sms, k: jnp.stack([sm[k] for sm in sms])
        g_full = dict(ffn1_norm=st(small, "n1")[:, 0], mix_norm=st(small, "nm")[:, 0], ffn2_norm=st(small, "n2")[:, 0],
                      ev_conv_w=st(ev, "conv_w"), ev_q_norm=st(ev, "q_norm"), ev_k_norm=st(ev, "k_norm"),
                      od_pool_w=st(od, "pool_w"), od_pool_scale=st(od, "pool_scale"), od_sgu_norm=st(od, "sgu_norm"),
                      od_sgu_w=st(od, "sgu_w"), od_sgu_b=st(od, "sgu_b"), final_norm=d_final[0])
        g_shapes.update({n: g_full[n].shape for n in _SMALL})
        return _pack([g_full[n] for n in _SMALL])

    gathers = _schedule(L, lambda kind, l: shards[name_of(kind, l)[0]].shape[1])
    plan = Plan(shard, gathers, SMALL_CARRIER, pack_small)
    layers = []
    for l in range(L):
        j = l // 2
        W = dict(n1=ffn1_norm[l][None], nm=mix_norm[l][None], n2=ffn2_norm[l][None])
        if l % 2 == 0:
            W.update(conv_w=full_small["ev_conv_w"][j], q_norm=ev_q_norm[j], k_norm=ev_k_norm[j])
        else:
            W.update(pool_w=od_pool_w[j], pool_scale=full_small["od_pool_scale"][j], sgu_norm=full_small["od_sgu_norm"][j],
                     sgu_w=od_sgu_w[j], sgu_b=od_sgu_b[j])
        layers.append(W)

    plan.alone("first", "gather_first")
    loss, dx = _local_step(x.reshape(T, D), loss_target.reshape(T, D), layers, final_norm[None], n_ex, plan)

    out = {n: None for n in _BIG}
    me1 = me.astype(jnp.int32).reshape(1)

    def update(arrived):
        for (kind, l), (own, parts) in arrived.items():
            name, idx = name_of(kind, l)
            out[name] = adamw(parts, *wmv[name], idx, prev=out[name], own=(own.reshape(parts.shape), me1))

    *earlier, last = plan.started
    update(plan.collect(earlier, last[5]))

    g8 = plan.small_parts.reshape(N_DEV, -1)
    cols, off = [], 0
    for n in _SMALL:
        cnt = math.prod(g_shapes[n])
        g = g8[:, off:off + cnt].reshape((N_DEV,) + g_shapes[n])
        off += cnt
        if n in _SMALL_SHARDED:
            width = w[n].shape[-1]
            g = lax.dynamic_slice_in_dim(g, me * width, width, axis=g.ndim - 1)
        cols.append(g.reshape(N_DEV, -1))
    g8 = jnp.concatenate(cols, axis=1)
    g8 = jnp.pad(g8, ((0, 0), (0, (-g8.shape[1]) % _PACK_ROWS))).reshape(N_DEV, -1, LANES)
    pk = lambda d: _pack([d[n] for n in _SMALL])[None]
    small_out = adamw(g8, pk(w), pk(m), pk(v), 0)

    update(plan.collect([last], small_out[0]))
    out = {n: [tr(a) if _BIG[n] else a for a in res] for n, res in out.items()}

    shapes = [w[n].shape for n in _SMALL]
    for i in range(4):
        for n, a in zip(_SMALL, _unpack(small_out[i], shapes)):
            out.setdefault(n, [None] * 4)[i] = a

    total = lax.psum(loss[0, 0], ("x", "y", "c"))
    return (total, dx.reshape(n_ex, seq, D), *[out[n][0] for n in _WEIGHTS], *[out[n][1] for n in _WEIGHTS],
            *[out[n][2] for n in _WEIGHTS], *[out[n][3] for n in _WEIGHTS])
```

```python
import functools
import math

import jax
import jax.numpy as jnp
from jax import lax
from jax.experimental import pallas as pl
from jax.experimental.pallas import tpu as pltpu

F32, BF16 = jnp.float32, jnp.bfloat16
EPS = 1e-6
N_DEV = 8
V7X_VMEM_BYTES = 64 * 1024 * 1024
VMEM_LIMIT = V7X_VMEM_BYTES - 8 * 1024 * 1024
LANES = 128
HEAD_DIM = 64
N_Q_HEADS = 8
N_KV_HEADS = 2
Q_PER_KV = N_Q_HEADS // N_KV_HEADS
GRID_W = 64
ROPE_THETA = 10000.0
POOL_RADII = (1, 2, 4, 8)
SGU_CHUNK = 128
GROUP = 128
ADAM_LR, ADAM_B1, ADAM_B2, ADAM_EPS, ADAM_WD, ADAM_STEP = 0.001, 0.9, 0.999, 1e-08, 0.01, 10
MESH_ID = pl.DeviceIdType.MESH


def _cp(*sem):
    return pltpu.CompilerParams(dimension_semantics=sem, vmem_limit_bytes=VMEM_LIMIT)


def _dot(a, b, ca, cb):
    return lax.dot_general(a, b, (((ca,), (cb,)), ((), ())), preferred_element_type=F32)


def _nn(a, b):
    return _dot(a, b, 1, 0)


def _nt(a, b):
    return _dot(a, b, 1, 1)


def _tn(a, b):
    return _dot(a, b, 0, 0)


def _split_mm(x, m):
    hi = x.astype(BF16)
    lo = (x - hi.astype(F32)).astype(BF16)
    return _nn(hi, m) + _nn(lo, m)


def _tile(n, pref):
    t = min(n, pref)
    assert n % t == 0, (n, pref)
    return t


ANY = pl.BlockSpec(memory_space=pl.ANY)
JOB_MIDDLE, JOB_LATE = 0.55, 0.85


def _my_place():
    x, y, c = lax.axis_index("x"), lax.axis_index("y"), lax.axis_index("c")
    return x, y, c, 4 * x + 2 * y + c


def _peer(x, y, c, k):
    px = 1 - x if k & 4 else x
    py = 1 - y if k & 2 else y
    pc = 1 - c if k & 1 else c
    return (px, py, pc), 4 * px + 2 * py + pc


def _remote(src, dst, send_sems, recv_sems, i, peer):
    return pltpu.make_async_remote_copy(src_ref=src, dst_ref=dst, send_sem=send_sems.at[i], recv_sem=recv_sems.at[i],
                                        device_id=peer, device_id_type=MESH_ID)


class GatherJob:
    def __init__(self, srcs):
        self.srcs = srcs
        self.args = [a for a, _ in srcs]
        self.dims = [a.shape[-2:] for a, _ in srcs]
        n = self.n_in = self.n_out = len(srcs)
        self.out_shape = [jax.ShapeDtypeStruct((N_DEV * r, cc), a.dtype) for (a, _), (r, cc) in zip(srcs, self.dims)]
        self.scratch = [pltpu.SemaphoreType.DMA((N_DEV * n,)), pltpu.SemaphoreType.DMA((N_DEV * n,)),
                        pltpu.SemaphoreType.DMA((n,))]

    def _rows(self, outs, t, idx):
        r = self.dims[t][0]
        return outs[t].at[pl.ds(pl.multiple_of(idx * r, 8), r), :]

    def _local(self, ins, outs, loc, t, me):
        src = ins[t] if self.srcs[t][1] is None else ins[t].at[self.srcs[t][1]]
        return src, pltpu.make_async_copy(src, self._rows(outs, t, me), loc.at[t])

    def start(self, ins, outs, sems):
        send, recv, loc = sems
        x, y, c, me = _my_place()
        for t in range(self.n_in):
            src, local = self._local(ins, outs, loc, t, me)
            local.start()
            for k in (2, 4, 1):
                _remote(src, self._rows(outs, t, me), send, recv, N_DEV * t + k, _peer(x, y, c, k)[0]).start()

    def _copy(self, outs, sems, t, origin, i, to):
        x, y, c, _ = _my_place()
        blk = self._rows(outs, t, _peer(x, y, c, origin)[1])
        return _remote(blk, blk, sems[0], sems[1], N_DEV * t + i, _peer(x, y, c, to)[0])

    def middle(self, ins, outs, sems):
        c = _my_place()[2]

        def relay(t, got, to):
            self._copy(outs, sems, t, got, got, got).wait_recv()
            self._copy(outs, sems, t, got, 6, to).start()
            self._copy(outs, sems, t, to, to, to).wait_recv()

        for t in range(self.n_in):
            pl.when(c == 1)(functools.partial(relay, t, 2, 4))
            pl.when(c == 0)(functools.partial(relay, t, 4, 2))
            for k in (2, 4):
                self._copy(outs, sems, t, k, k + 1, 1).start()

    def late(self, ins, outs, sems):
        for t in range(self.n_in):
            self._copy(outs, sems, t, 6, 6, 6).wait_recv()
            self._copy(outs, sems, t, 6, 7, 1).start()

    def finish(self, ins, outs, sems):
        send, recv, loc = sems
        x, y, c, me = _my_place()
        for t in range(self.n_in):
            for k in range(1, N_DEV):
                peer, pidx = _peer(x, y, c, k)
                blk = self._rows(outs, t, pidx)
                if k % 2 == 1:
                    _remote(blk, blk, send, recv, N_DEV * t + k, peer).wait_recv()
                _remote(blk, blk, send, recv, N_DEV * t + k, peer).wait_send()
            self._local(ins, outs, loc, t, me)[1].wait()


def _call(body, name, grid, in_specs, args, out_specs, out_shape, scratch=(), sem=(), job=None, aliases=None):
    in_specs, out_specs, out_shape, scratch = list(in_specs), list(out_specs), list(out_shape), list(scratch)
    n_in, n_out, n_scr = len(args), len(out_shape), len(scratch)
    if job is None:
        res = pl.pallas_call(body, name=name, grid=grid, in_specs=in_specs, out_specs=out_specs, out_shape=out_shape,
                             scratch_shapes=scratch, input_output_aliases=aliases or {}, compiler_params=_cp(*sem))(*args)
        return res, None
    o0 = n_in + job.n_in
    s0 = o0 + n_out + job.n_out

    def carrier(*refs):
        jin, jout, jsem = refs[n_in:o0], refs[o0 + n_out:s0], refs[s0 + n_scr:]
        ids = [pl.program_id(a) for a in range(len(grid))]
        def at(step):
            idx = []
            for g in reversed(grid):
                idx.append(step % g)
                step //= g
            return functools.reduce(jnp.logical_and, [i == j for i, j in zip(ids, reversed(idx))])

        steps = math.prod(grid)
        if grid:
            pl.when(at(0))(lambda: job.start(jin, jout, jsem))
            pl.when(at(int(steps * JOB_MIDDLE)))(lambda: job.middle(jin, jout, jsem))
            pl.when(at(int(steps * JOB_LATE)))(lambda: job.late(jin, jout, jsem))
        else:
            job.start(jin, jout, jsem)
            job.middle(jin, jout, jsem)
            job.late(jin, jout, jsem)
        body(*refs[:n_in], *refs[o0:o0 + n_out], *refs[s0:s0 + n_scr])
        if grid:
            pl.when(at(steps - 1))(lambda: job.finish(jin, jout, jsem))
        else:
            job.finish(jin, jout, jsem)

    res = pl.pallas_call(
        carrier, name=name + "_comm", grid=grid, in_specs=in_specs + [ANY] * job.n_in,
        out_specs=out_specs + [ANY] * job.n_out, out_shape=out_shape + job.out_shape,
        scratch_shapes=scratch + job.scratch, input_output_aliases=aliases or {},
        compiler_params=_cp(*(["arbitrary"] * len(grid))))(*args, *job.args)
    return res[:n_out], res[n_out:]


def run_job(job, name):
    return _call(lambda: None, name, (), [], [], [], [], job=job)[1]


@jax.custom_vjp
def bmm(x, w):
    return _nn(x.astype(BF16), w.astype(BF16))


def _bmm_fwd(x, w):
    return bmm(x, w), (x, w)


def _bmm_bwd(res, g):
    x, w = res
    gb = g.astype(BF16)
    return _nt(gb, w.astype(BF16)), _tn(x.astype(BF16), gb)


bmm.defvjp(_bmm_fwd, _bmm_bwd)


def _shift_raw(x, d):
    n = x.shape[0]
    r = pltpu.roll(x, d % n, axis=0)
    row = lax.broadcasted_iota(jnp.int32, x.shape, 0)
    keep = (row >= d) if d > 0 else (row < n + d)
    return jnp.where(keep, r, 0.0)


def shift_rows(x, d):
    @jax.custom_vjp
    def f(v):
        return _shift_raw(v, d)

    f.defvjp(lambda v: (_shift_raw(v, d), None), lambda _, g: (_shift_raw(g, -d),))
    return f(x)


def _swap_raw(x):
    n = x.shape[1]
    nxt = pltpu.roll(x, n - 1, axis=1)
    prv = pltpu.roll(x, 1, axis=1)
    lane = lax.broadcasted_iota(jnp.int32, x.shape, 1)
    return jnp.where(lane % 2 == 0, nxt, prv)


@jax.custom_vjp
def swap_pairs(x):
    return _swap_raw(x)


swap_pairs.defvjp(lambda x: (_swap_raw(x), None), lambda _, g: (_swap_raw(g),))


@jax.custom_vjp
def group_mean(x, bd):
    return _split_mm(x, bd)


group_mean.defvjp(lambda x, bd: (_split_mm(x, bd), bd), lambda bd, g: (_split_mm(g, bd), jnp.zeros_like(bd)))


def _rope_norm(x, gain, cos, sgn, bd, scale):
    xn = x * lax.rsqrt(group_mean(x * x, bd) + EPS) * gain
    return (xn * cos + swap_pairs(xn) * sgn) * scale


def _conv_gate(gb, gc, hc, w):
    z = gc * hc
    c = shift_rows(z, 1) * w[0:1] + z * w[1:2] + shift_rows(z, -1) * w[2:3]
    return gb * c


def _window_sum(p, r):
    b = f = p
    k = 1
    while k < r:
        b = b + shift_rows(b, k)
        f = f + shift_rows(f, -k)
        k *= 2
    return b + f - p + shift_rows(p, r) + shift_rows(p, -r)


def _pool_mix(p, pool_w, scale):
    n = p.shape[0]
    t = lax.broadcasted_iota(jnp.int32, (n, 1), 0)
    outs = []
    for gi, r in enumerate(POOL_RADII):
        pg = p[:, gi * GROUP:(gi + 1) * GROUP]
        cnt = (jnp.minimum(t + r, n - 1) - jnp.maximum(t - r, 0) + 1).astype(F32)
        pooled = _window_sum(pg, r) / cnt - pg
        outs.append(bmm(pooled, pool_w[gi]))
    return jnp.concatenate(outs, axis=1) * scale


def _sgu(u, v, norm_g, w_s, b_full):
    ug = jax.nn.gelu(u)
    vg = jax.nn.gelu(v)
    vn = vg * lax.rsqrt(jnp.mean(vg * vg, axis=-1, keepdims=True) + EPS) * norm_g
    cols = []
    for g in range(w_s.shape[0]):
        rows = []
        for n in range(u.shape[0] // SGU_CHUNK):
            blk = vn[n * SGU_CHUNK:(n + 1) * SGU_CHUNK, g * GROUP:(g + 1) * GROUP]
            rows.append(bmm(w_s[g], blk) + b_full[g])
        cols.append(jnp.concatenate(rows, axis=0))
    return ug * jnp.concatenate(cols, axis=1)


def _rms_bwd_math(xv, gain, dy, dres):
    r = lax.rsqrt(jnp.mean(xv * xv, axis=-1, keepdims=True) + EPS)
    xh = xv * r
    dxh = dy * gain
    dx = dres + r * (dxh - xh * jnp.mean(dxh * xh, axis=-1, keepdims=True))
    return dx, jnp.sum(dy * xh, axis=0, keepdims=True)


FFN_TN = 256


def ffn_fwd(x, gain, wt_in, w_out, job=None):
    T, D = x.shape
    F = w_out.shape[0]
    tm, tn = _tile(T, 1024), FFN_TN
    nc = F // tn

    def body(x_ref, gn_ref, wg_ref, wu_ref, wo_ref, y_ref, gu_ref, xn_s, acc_s):
        c = pl.program_id(1)

        @pl.when(c == 0)
        def _():
            xv = x_ref[...]
            r = lax.rsqrt(jnp.mean(xv * xv, axis=-1, keepdims=True) + EPS)
            xn_s[...] = (xv * r * gn_ref[...]).astype(BF16)
            acc_s[...] = jnp.zeros_like(acc_s)

        xn = xn_s[...]
        g = _nt(xn, wg_ref[...])
        u = _nt(xn, wu_ref[...])
        gu_ref[:, :tn] = g.astype(BF16)
        gu_ref[:, tn:] = u.astype(BF16)
        a = (g * jax.nn.sigmoid(g) * u).astype(BF16)
        acc_s[...] += _nn(a, wo_ref[...])

        @pl.when(c == nc - 1)
        def _():
            y_ref[...] = x_ref[...] + 0.5 * acc_s[...]

    row = pl.BlockSpec((tm, D), lambda i, c: (i, 0))
    return _call(
        body, "ffn_fwd", (T // tm, nc),
        [row, pl.BlockSpec((1, D), lambda i, c: (0, 0)),
         pl.BlockSpec((tn, D), lambda i, c: (c, 0)),
         pl.BlockSpec((tn, D), lambda i, c: (c + nc, 0)),
         pl.BlockSpec((tn, D), lambda i, c: (c, 0))],
        [x, gain, wt_in, wt_in, w_out],
        [row, pl.BlockSpec((tm, 2 * tn), lambda i, c: (i, c)), row],
        [jax.ShapeDtypeStruct((T, D), F32), jax.ShapeDtypeStruct((T, 2 * F), BF16), jax.ShapeDtypeStruct((T, D), BF16)],
        [pltpu.VMEM((tm, D), F32)], ("parallel", "arbitrary"), job)


def ffn_bwd_x(dout, x, gain, gu, wt_in, w_out, job=None):
    T, D = x.shape
    F = w_out.shape[0]
    tm, tn = _tile(T, 1024), FFN_TN
    nc = F // tn

    def body(do_ref, x_ref, gn_ref, gu_ref, wg_ref, wu_ref, wo_ref,
             dx_ref, dgn_ref, a_ref, dgu_ref, dob_ref, acc_s):
        i, c = pl.program_id(0), pl.program_id(1)

        @pl.when(c == 0)
        def _():
            dob_ref[...] = (0.5 * do_ref[...]).astype(BF16)
            acc_s[...] = jnp.zeros_like(acc_s)

        da = jnp.concatenate([_nt(dob_ref[:tm // 2, :], wo_ref[...]), _nt(dob_ref[tm // 2:, :], wo_ref[...])], axis=0)
        g = gu_ref[:, :tn].astype(F32)
        u = gu_ref[:, tn:].astype(F32)
        sig = jax.nn.sigmoid(g)
        sl = g * sig
        a_ref[...] = (sl * u).astype(BF16)
        dg = (da * u * (sig * (1.0 + g * (1.0 - sig)))).astype(BF16)
        du = (da * sl).astype(BF16)
        dgu_ref[:, :tn] = dg
        dgu_ref[:, tn:] = du
        acc_s[...] += _nn(dg, wg_ref[...]) + _nn(du, wu_ref[...])

        @pl.when(c == nc - 1)
        def _():
            dx, dgn = _rms_bwd_math(x_ref[...], gn_ref[...], acc_s[...], do_ref[...])
            dx_ref[...] = dx

            @pl.when(i == 0)
            def _():
                dgn_ref[...] = jnp.zeros_like(dgn_ref)

            dgn_ref[...] += dgn

    row = pl.BlockSpec((tm, D), lambda i, c: (i, 0))
    return _call(
        body, "ffn_bwd_x", (T // tm, nc),
        [row, row, pl.BlockSpec((1, D), lambda i, c: (0, 0)),
         pl.BlockSpec((tm, 2 * tn), lambda i, c: (i, c)),
         pl.BlockSpec((tn, D), lambda i, c: (c, 0)),
         pl.BlockSpec((tn, D), lambda i, c: (c + nc, 0)),
         pl.BlockSpec((tn, D), lambda i, c: (c, 0))],
        [dout, x, gain, gu, wt_in, wt_in, w_out],
        [row, pl.BlockSpec((1, D), lambda i, c: (0, 0)),
         pl.BlockSpec((tm, tn), lambda i, c: (i, c)),
         pl.BlockSpec((tm, 2 * tn), lambda i, c: (i, c)), row],
        [jax.ShapeDtypeStruct((T, D), F32), jax.ShapeDtypeStruct((1, D), F32),
         jax.ShapeDtypeStruct((T, F), BF16), jax.ShapeDtypeStruct((T, 2 * F), BF16),
         jax.ShapeDtypeStruct((T, D), BF16)],
        [pltpu.VMEM((tm, D), F32)], ("arbitrary", "arbitrary"), job)


MM_TM = 512
MM_TC = 256


def mm_nt(a, wt, pieces, out_dtype, emit_a_bf16=False, norm_gain=None):
    T, K = a.shape
    tm = _tile(T, MM_TM)
    npc = len(pieces)
    n_lead = 1 if norm_gain is None else 2

    def body(*refs):
        a_ref, w_refs, o_refs = refs[0], refs[n_lead:n_lead + npc], refs[n_lead + npc:]
        av = a_ref[...]
        if norm_gain is not None:
            av = av * lax.rsqrt(jnp.mean(av * av, axis=-1, keepdims=True) + EPS) * refs[1][...]
        ab = av.astype(BF16)
        for w_ref, o_ref in zip(w_refs, o_refs[:npc]):
            o_ref[...] = _nt(ab, w_ref[...]).astype(o_ref.dtype)
        if emit_a_bf16:
            o_refs[npc][...] = ab

    in_specs = [pl.BlockSpec((tm, K), lambda i: (i, 0))]
    if norm_gain is not None:
        in_specs.append(pl.BlockSpec((1, K), lambda i: (0, 0)))
    out_specs, out_shape = [], []
    for r0, n in pieces:
        assert r0 % n == 0
        in_specs.append(pl.BlockSpec((n, K), functools.partial(lambda i, b: (b, 0), b=r0 // n)))
        out_specs.append(pl.BlockSpec((tm, n), lambda i: (i, 0)))
        out_shape.append(jax.ShapeDtypeStruct((T, n), out_dtype))
    if emit_a_bf16:
        out_specs.append(pl.BlockSpec((tm, K), lambda i: (i, 0)))
        out_shape.append(jax.ShapeDtypeStruct((T, K), BF16))
    return pl.pallas_call(
        body, name="mm_nt", grid=(T // tm,), in_specs=in_specs, out_specs=out_specs, out_shape=out_shape,
        compiler_params=_cp("parallel"),
    )(a, *([] if norm_gain is None else [norm_gain]), *([wt] * npc))


def mm_nn(pieces, w, residual=None, norm_bwd=None):
    T = pieces[0][0].shape[0]
    N = w.shape[1]
    tm = _tile(T, MM_TM)
    na = len(pieces)

    def body(*refs):
        a_refs, w_refs = refs[:na], refs[na:2 * na]
        acc = refs[2 * na][...] if residual is not None else None
        for a_ref, w_ref in zip(a_refs, w_refs):
            t = _nn(a_ref[...].astype(BF16), w_ref[...])
            acc = t if acc is None else acc + t
        if norm_bwd is None:
            refs[-1][...] = acc
            return
        x_ref, g_ref, dr_ref, dx_ref, dg_ref = refs[-5:]
        dx, dg = _rms_bwd_math(x_ref[...], g_ref[...], acc, dr_ref[...])
        dx_ref[...] = dx

        @pl.when(pl.program_id(0) == 0)
        def _():
            dg_ref[...] = jnp.zeros_like(dg_ref)

        dg_ref[...] += dg

    in_specs, w_specs = [], []
    for a, cb, k, rb in pieces:
        in_specs.append(pl.BlockSpec((tm, k), functools.partial(lambda i, b: (i, b), b=cb)))
        w_specs.append(pl.BlockSpec((k, N), functools.partial(lambda i, b: (b, 0), b=rb)))
    assert sum(k for _, _, k, _ in pieces) == w.shape[0]
    args = [a for a, _, _, _ in pieces] + [w] * na
    in_specs = in_specs + w_specs
    row = pl.BlockSpec((tm, N), lambda i: (i, 0))
    if residual is not None:
        in_specs.append(row)
        args.append(residual)
    if norm_bwd is None:
        return pl.pallas_call(
            body, name="mm_nn", grid=(T // tm,), in_specs=in_specs, out_specs=row,
            out_shape=jax.ShapeDtypeStruct((T, N), F32), compiler_params=_cp("parallel"),
        )(*args)
    vec = pl.BlockSpec((1, N), lambda i: (0, 0))
    return pl.pallas_call(
        body, name="mm_nn_norm_bwd", grid=(T // tm,), in_specs=in_specs + [row, vec, row], out_specs=[row, vec],
        out_shape=[jax.ShapeDtypeStruct((T, N), F32), jax.ShapeDtypeStruct((1, N), F32)],
        compiler_params=_cp("arbitrary"),
    )(*args, *norm_bwd)


def mm_tn(a, b, n_rows, row_block, prev=None, grid=None, col_block=None, job=None):
    T, M = a.shape
    N = b.shape[1]
    tc = MM_TC
    assert M % tc == 0 and n_rows % tc == 0
    if grid is None:
        grid, col_block = (M // tc,), (lambda j: j)

    def body(*refs):
        a_ref, b_ref, o_ref = refs[0], refs[1], refs[-1]
        o_ref[...] = _tn(a_ref[...], b_ref[...]).astype(BF16)

    in_specs = [pl.BlockSpec((T, tc), lambda *g: (0, col_block(*g))), pl.BlockSpec((T, N), lambda *g: (0, 0))]
    args = [a, b]
    aliases = {}
    if prev is not None:
        in_specs.append(pl.BlockSpec(memory_space=pl.ANY))
        args.append(prev)
        aliases = {2: 0}
    res, jres = _call(body, "mm_tn", grid, in_specs, args, [pl.BlockSpec((tc, N), lambda *g: (row_block(*g), 0))],
                      [jax.ShapeDtypeStruct((n_rows, N), BF16)], (), ["parallel"] * len(grid), job, aliases)
    return res[0] if job is None else (res[0], jres)


def conv_fwd(proj_a, conv_w, n_ex):
    T, C3 = proj_a.shape
    C = C3 // 3
    S = T // n_ex

    def body(gb_ref, gc_ref, hc_ref, w_ref, o_ref):
        o_ref[...] = _conv_gate(gb_ref[...], gc_ref[...], hc_ref[...], w_ref[...]).astype(BF16)

    col = lambda k: pl.BlockSpec((S, C), functools.partial(lambda b, kk: (b, kk), kk=k))
    return pl.pallas_call(
        body, name="conv_fwd", grid=(n_ex,),
        in_specs=[col(0), col(1), col(2), pl.BlockSpec((3, C), lambda b: (0, 0))],
        out_specs=pl.BlockSpec((S, C), lambda b: (b, 0)),
        out_shape=jax.ShapeDtypeStruct((T, 2 * C), BF16), compiler_params=_cp("parallel"),
    )(proj_a, proj_a, proj_a, conv_w)


def conv_bwd(proj_a, conv_w, n_ex, dy, total_cols):
    T, C3 = proj_a.shape
    C = C3 // 3
    S = T // n_ex

    def body(gb_ref, gc_ref, hc_ref, w_ref, dy_ref, dp_ref, dw_ref):
        _, vjp = jax.vjp(_conv_gate, gb_ref[...], gc_ref[...], hc_ref[...], w_ref[...])
        dgb, dgc, dhc, dw = vjp(dy_ref[...].astype(F32))
        dp_ref[:, 0:C] = dgb.astype(BF16)
        dp_ref[:, C:2 * C] = dgc.astype(BF16)
        dp_ref[:, 2 * C:] = dhc.astype(BF16)

        @pl.when(pl.program_id(0) == 0)
        def _():
            dw_ref[...] = jnp.zeros_like(dw_ref)

        dw_ref[...] += dw

    col = lambda k: pl.BlockSpec((S, C), functools.partial(lambda b, kk: (b, kk), kk=k))
    return pl.pallas_call(
        body, name="conv_bwd", grid=(n_ex,),
        in_specs=[col(0), col(1), col(2), pl.BlockSpec((3, C), lambda b: (0, 0)),
                  pl.BlockSpec((S, C), lambda b: (b, 0))],
        out_specs=[pl.BlockSpec((S, C3), lambda b: (b, 0)), pl.BlockSpec((3, C), lambda b: (0, 0))],
        out_shape=[jax.ShapeDtypeStruct((T, total_cols), BF16), jax.ShapeDtypeStruct((3, C), F32)],
        compiler_params=_cp("arbitrary"),
    )(proj_a, proj_a, proj_a, conv_w, dy)


QW = N_Q_HEADS * HEAD_DIM
KW = N_KV_HEADS * HEAD_DIM
QP = N_Q_HEADS * LANES
KP = N_KV_HEADS * LANES


def _attn_consts(seq):
    rows = seq // GRID_W
    r_idx, c_idx = jnp.meshgrid(jnp.arange(rows), jnp.arange(GRID_W), indexing='ij')
    r_idx = r_idx.reshape(-1).astype(F32)
    c_idx = c_idx.reshape(-1).astype(F32)
    n_freq = HEAD_DIM // 4
    inv = ROPE_THETA ** (-jnp.arange(n_freq, dtype=F32) / n_freq)
    ang = jnp.concatenate([r_idx[:, None] * inv, c_idx[:, None] * inv], axis=-1)
    cos = jnp.repeat(jnp.cos(ang), 2, axis=1)
    sin = jnp.repeat(jnp.sin(ang), 2, axis=1)
    sgn = sin * jnp.tile(jnp.array([-1.0, 1.0], F32), HEAD_DIM // 2)
    cos = jnp.tile(cos, (1, N_Q_HEADS))
    sgn = jnp.tile(sgn, (1, N_Q_HEADS))
    lane = jnp.arange(QW)
    bd = jnp.where(lane[:, None] // HEAD_DIM == lane[None, :] // HEAD_DIM, 1.0 / HEAD_DIM, 0.0).astype(BF16)
    dst = (lane // HEAD_DIM) * LANES + lane % HEAD_DIM
    spread = (dst[:, None] == jnp.arange(QP)[None, :]).astype(BF16)
    return dict(cos=cos, sgn=sgn, bd=bd, spread=spread, gather=spread.T)


def qkv_prep_fwd(proj_b, qg, kg, cst, n_ex):
    T = proj_b.shape[0]
    S = T // n_ex
    tm = _tile(S, 512)
    nb = S // tm

    def body(p_ref, qg_ref, kg_ref, cos_ref, sgn_ref, bd_ref, sp_ref, q_ref, k_ref, v_ref):
        pv = p_ref[...]
        cos, sgn, bd, sp = cos_ref[...], sgn_ref[...], bd_ref[...], sp_ref[...]
        qr = _rope_norm(pv[:, :QW], qg_ref[...], cos, sgn, bd, HEAD_DIM ** -0.5)
        kr = _rope_norm(pv[:, QW:QW + KW], kg_ref[...], cos[:, :KW], sgn[:, :KW], bd[:KW, :KW], 1.0)
        q_ref[...] = _nn(qr.astype(BF16), sp).astype(BF16)
        k_ref[...] = _nn(kr.astype(BF16), sp[:KW, :KP]).astype(BF16)
        v_ref[...] = _nn(pv[:, QW + KW:].astype(BF16), sp[:KW, :KP]).astype(BF16)

    full = lambda a: pl.BlockSpec(a.shape, lambda i: (0,) * a.ndim)
    tab = pl.BlockSpec((tm, QW), lambda i: (i % nb, 0))
    return pl.pallas_call(
        body, name="qkv_prep_fwd", grid=(T // tm,),
        in_specs=[pl.BlockSpec((tm, QW + 2 * KW), lambda i: (i, 0)), full(qg), full(kg), tab, tab,
                  full(cst["bd"]), full(cst["spread"])],
        out_specs=[pl.BlockSpec((tm, QP), lambda i: (i, 0)), pl.BlockSpec((tm, KP), lambda i: (i, 0)),
                   pl.BlockSpec((tm, KP), lambda i: (i, 0))],
        out_shape=[jax.ShapeDtypeStruct((T, QP), BF16), jax.ShapeDtypeStruct((T, KP), BF16),
                   jax.ShapeDtypeStruct((T, KP), BF16)],
        compiler_params=_cp("parallel"),
    )(proj_b, qg, kg, cst["cos"], cst["sgn"], cst["bd"], cst["spread"])


def qkv_prep_bwd(proj_b, qg, kg, cst, n_ex, dq, dk_pad, dv_pad, d_proj):
    T = proj_b.shape[0]
    S = T // n_ex
    tm = _tile(S, 512)
    nb = S // tm

    def body(p_ref, qg_ref, kg_ref, cos_ref, sgn_ref, bd_ref, ga_ref, dq_ref, dk_ref, dv_ref, _kept,
             dp_ref, dqg_ref, dkg_ref):
        pv = p_ref[...]
        cos, sgn, bd, ga = cos_ref[...], sgn_ref[...], bd_ref[...], ga_ref[...]
        fq = lambda q, g: _rope_norm(q, g, cos, sgn, bd, HEAD_DIM ** -0.5)
        fk = lambda k, g: _rope_norm(k, g, cos[:, :KW], sgn[:, :KW], bd[:KW, :KW], 1.0)
        _, vq = jax.vjp(fq, pv[:, :QW], qg_ref[...])
        _, vk = jax.vjp(fk, pv[:, QW:QW + KW], kg_ref[...])
        dqp, dqg = vq(dq_ref[...])
        dkp, dkg = vk(_split_mm(dk_ref[...], ga[:KP, :KW]))
        dp_ref[:, :QW] = dqp.astype(BF16)
        dp_ref[:, QW:QW + KW] = dkp.astype(BF16)
        dp_ref[:, QW + KW:] = _split_mm(dv_ref[...], ga[:KP, :KW]).astype(BF16)

        @pl.when(pl.program_id(0) == 0)
        def _():
            dqg_ref[...] = jnp.zeros_like(dqg_ref)
            dkg_ref[...] = jnp.zeros_like(dkg_ref)

        dqg_ref[...] += dqg
        dkg_ref[...] += dkg

    full = lambda a: pl.BlockSpec(a.shape, lambda i: (0,) * a.ndim)
    tab = pl.BlockSpec((tm, QW), lambda i: (i % nb, 0))
    row = lambda n: pl.BlockSpec((tm, n), lambda i: (i, 0))
    wb = QW + 2 * KW
    assert d_proj.shape[1] % wb == 0
    last = d_proj.shape[1] // wb - 1
    return pl.pallas_call(
        body, name="qkv_prep_bwd", grid=(T // tm,),
        in_specs=[row(wb), full(qg), full(kg), tab, tab, full(cst["bd"]), full(cst["gather"]),
                  row(QW), row(KP), row(KP), ANY],
        out_specs=[pl.BlockSpec((tm, wb), lambda i: (i, last)), pl.BlockSpec((1, QW), lambda i: (0, 0)),
                   pl.BlockSpec((1, KW), lambda i: (0, 0))],
        out_shape=[jax.ShapeDtypeStruct(d_proj.shape, BF16), jax.ShapeDtypeStruct((1, QW), F32),
                   jax.ShapeDtypeStruct((1, KW), F32)],
        input_output_aliases={10: 0}, compiler_params=_cp("arbitrary"),
    )(proj_b, qg, kg, cst["cos"], cst["sgn"], cst["bd"], cst["gather"], dq, dk_pad, dv_pad, d_proj)


ATT_TQ = 256
ATT_TQ_FWD = 512


def attn_fwd(qp, kp, vp, gather, n_ex, mix, job=None):
    T = qp.shape[0]
    S = T // n_ex
    tq = _tile(S, ATT_TQ_FWD)
    nq = S // tq

    def body(q_ref, k_ref, v_ref, ga_ref, _kept, o_ref, op_ref, lse_ref):
        lane = lax.broadcasted_iota(jnp.int32, (tq, LANES), 1)
        lse_all = jnp.zeros((tq, LANES), F32)
        for h in range(N_Q_HEADS):
            kv = h // Q_PER_KV
            qh = q_ref[:, h * LANES:(h + 1) * LANES]
            s = _nt(qh, k_ref[:, kv * LANES:(kv + 1) * LANES])
            m = jnp.max(s, axis=-1, keepdims=True)
            p = jnp.exp(s - m)
            lsum = jnp.sum(p, axis=-1, keepdims=True)
            o = _nn(p.astype(BF16), v_ref[:, kv * LANES:(kv + 1) * LANES]) / lsum
            op_ref[:, h * LANES:(h + 1) * LANES] = o.astype(BF16)
            lse_all = jnp.where(lane == h, m + jnp.log(lsum), lse_all)
        lse_ref[...] = lse_all
        o_ref[...] = _nn(op_ref[...], ga_ref[...]).astype(BF16)

    blk = lambda n: pl.BlockSpec((tq, n), lambda b, i: (b * nq + i, 0))
    kvs = pl.BlockSpec((S, KP), lambda b, i: (b, 0))
    return _call(
        body, "attn_fwd", (n_ex, nq),
        [blk(QP), kvs, kvs, pl.BlockSpec(gather.shape, lambda b, i: (0, 0)), ANY], [qp, kp, vp, gather, mix],
        [pl.BlockSpec((tq, QW), lambda b, i: (b * nq + i, 1)), blk(QP), blk(LANES)],
        [jax.ShapeDtypeStruct(mix.shape, BF16), jax.ShapeDtypeStruct((T, QP), BF16),
         jax.ShapeDtypeStruct((T, LANES), F32)], (), ("parallel", "parallel"), job, {4: 0})


def attn_bwd(qp, kp, vp, op, lse, do, cst, n_ex, job=None):
    T = qp.shape[0]
    S = T // n_ex
    tq = _tile(S, ATT_TQ)
    nq = S // tq

    def body(q_ref, k_ref, v_ref, op_ref, lse_ref, do_ref, sp_ref, ga_ref, dq_ref, dk_ref, dv_ref, dqp_s):
        @pl.when(pl.program_id(1) == 0)
        def _():
            dk_ref[...] = jnp.zeros_like(dk_ref)
            dv_ref[...] = jnp.zeros_like(dv_ref)

        lane = lax.broadcasted_iota(jnp.int32, (tq, LANES), 1)
        dop = _nn(do_ref[...], sp_ref[...]).astype(BF16)
        lse_all = lse_ref[...]
        for h in range(N_Q_HEADS):
            kv = h // Q_PER_KV
            hs = slice(h * LANES, (h + 1) * LANES)
            ks = slice(kv * LANES, (kv + 1) * LANES)
            qh, kk, vv = q_ref[:, hs], k_ref[:, ks], v_ref[:, ks]
            doh = dop[:, hs]
            lse_h = jnp.sum(jnp.where(lane == h, lse_all, 0.0), axis=-1, keepdims=True)
            p = jnp.exp(_nt(qh, kk) - lse_h)
            dp = _nt(doh, vv)
            delta = jnp.sum(doh.astype(F32) * op_ref[:, hs].astype(F32), axis=-1, keepdims=True)
            ds = (p * (dp - delta)).astype(BF16)
            dqp_s[:, hs] = _nn(ds, kk)
            dk_ref[:, ks] += _tn(ds, qh)
            dv_ref[:, ks] += _tn(p.astype(BF16), doh)
        dq_ref[...] = _split_mm(dqp_s[...], ga_ref[...])

    blk = lambda n: pl.BlockSpec((tq, n), lambda b, i: (b * nq + i, 0))
    kvs = pl.BlockSpec((S, KP), lambda b, i: (b, 0))
    full = lambda a: pl.BlockSpec(a.shape, lambda b, i: (0, 0))
    return _call(
        body, "attn_bwd", (n_ex, nq),
        [blk(QP), kvs, kvs, blk(QP), blk(LANES), pl.BlockSpec((tq, QW), lambda b, i: (b * nq + i, 1)),
         full(cst["spread"]), full(cst["gather"])],
        [qp, kp, vp, op, lse, do, cst["spread"], cst["gather"]],
        [blk(QW), kvs, kvs],
        [jax.ShapeDtypeStruct((T, QW), F32), jax.ShapeDtypeStruct((T, KP), F32), jax.ShapeDtypeStruct((T, KP), F32)],
        [pltpu.VMEM((tq, QP), F32)], ("arbitrary", "arbitrary"), job)


def pool_fwd(p, pool_w, scale, n_ex):
    T, W = p.shape
    S = T // n_ex

    def body(p_ref, w_ref, s_ref, o_ref):
        o_ref[...] = _pool_mix(p_ref[...], w_ref[...], s_ref[...]).astype(BF16)

    return pl.pallas_call(
        body, name="pool_fwd", grid=(n_ex,),
        in_specs=[pl.BlockSpec((S, W), lambda b: (b, 0)),
                  pl.BlockSpec(pool_w.shape, lambda b: (0, 0, 0)),
                  pl.BlockSpec((1, W), lambda b: (0, 0))],
        out_specs=pl.BlockSpec((S, W), lambda b: (b, 0)),
        out_shape=jax.ShapeDtypeStruct((T, 2 * W), BF16), compiler_params=_cp("parallel"),
    )(p, pool_w, scale)


def pool_bwd(p, pool_w, scale, n_ex, dy, d_proj):
    T, W = p.shape
    S = T // n_ex
    last = d_proj.shape[1] // W - 1

    def body(p_ref, w_ref, s_ref, dy_ref, _kept, dp_ref, dw_ref, ds_ref):
        _, vjp = jax.vjp(_pool_mix, p_ref[...], w_ref[...], s_ref[...])
        dp, dw, ds = vjp(dy_ref[...].astype(F32))
        dp_ref[...] = dp.astype(BF16)

        @pl.when(pl.program_id(0) == 0)
        def _():
            dw_ref[...] = jnp.zeros_like(dw_ref)
            ds_ref[...] = jnp.zeros_like(ds_ref)

        dw_ref[...] += dw
        ds_ref[...] += ds

    wshape = pool_w.shape
    return pl.pallas_call(
        body, name="pool_bwd", grid=(n_ex,),
        in_specs=[pl.BlockSpec((S, W), lambda b: (b, 0)),
                  pl.BlockSpec(wshape, lambda b: (0, 0, 0)),
                  pl.BlockSpec((1, W), lambda b: (0, 0)), pl.BlockSpec((S, W), lambda b: (b, 0)), ANY],
        out_specs=[pl.BlockSpec((S, W), lambda b: (b, last)), pl.BlockSpec(wshape, lambda b: (0, 0, 0)),
                   pl.BlockSpec((1, W), lambda b: (0, 0))],
        out_shape=[jax.ShapeDtypeStruct(d_proj.shape, BF16), jax.ShapeDtypeStruct(wshape, F32),
                   jax.ShapeDtypeStruct((1, W), F32)],
        input_output_aliases={4: 0}, compiler_params=_cp("arbitrary"),
    )(p, pool_w, scale, dy, d_proj)


SGU_TS = 512


def sgu_fwd(u, v, norm_g, w_s, b_full, mix):
    T, W = u.shape
    ts = _tile(T, SGU_TS)

    def body(u_ref, v_ref, g_ref, w_ref, b_ref, _kept, o_ref):
        o_ref[...] = _sgu(u_ref[...], v_ref[...], g_ref[...], w_ref[...], b_ref[...]).astype(BF16)

    row = pl.BlockSpec((ts, W), lambda i: (i, 0))
    wsp = pl.BlockSpec(w_s.shape, lambda i: (0, 0, 0))
    return pl.pallas_call(
        body, name="sgu_fwd", grid=(T // ts,),
        in_specs=[row, row, pl.BlockSpec((1, W), lambda i: (0, 0)), wsp, wsp, ANY],
        out_specs=pl.BlockSpec((ts, W), lambda i: (i, 1)), out_shape=jax.ShapeDtypeStruct(mix.shape, BF16),
        input_output_aliases={5: 0}, compiler_params=_cp("parallel"),
    )(u, v, norm_g, w_s, b_full, mix)


def sgu_bwd(u, v, norm_g, w_s, b_full, dy):
    T, W = u.shape
    ts = _tile(T, SGU_TS)
    wshape = w_s.shape

    def body(u_ref, v_ref, g_ref, w_ref, b_ref, dy_ref, duv_ref, dg_ref, dw_ref, db_ref):
        _, vjp = jax.vjp(_sgu, u_ref[...], v_ref[...], g_ref[...], w_ref[...], b_ref[...])
        du, dv, dg, dw, db = vjp(dy_ref[...].astype(F32))
        duv_ref[:, :W] = du.astype(BF16)
        duv_ref[:, W:] = dv.astype(BF16)

        @pl.when(pl.program_id(0) == 0)
        def _():
            dg_ref[...] = jnp.zeros_like(dg_ref)
            dw_ref[...] = jnp.zeros_like(dw_ref)
            db_ref[...] = jnp.zeros_like(db_ref)

        dg_ref[...] += dg
        dw_ref[...] += dw
        db_ref[...] += db

    row = pl.BlockSpec((ts, W), lambda i: (i, 0))
    wsp = pl.BlockSpec(wshape, lambda i: (0, 0, 0))
    wout = pl.BlockSpec(wshape, lambda i: (0, 0, 0))
    vec = pl.BlockSpec((1, W), lambda i: (0, 0))
    return pl.pallas_call(
        body, name="sgu_bwd", grid=(T // ts,),
        in_specs=[row, row, vec, wsp, wsp, pl.BlockSpec((ts, W), lambda i: (i, 1))],
        out_specs=[pl.BlockSpec((ts, 2 * W), lambda i: (i, 0)), vec, wout, wout],
        out_shape=[jax.ShapeDtypeStruct((T, 3 * W), BF16),
                   jax.ShapeDtypeStruct((1, W), F32), jax.ShapeDtypeStruct(wshape, F32),
                   jax.ShapeDtypeStruct(wshape, F32)],
        compiler_params=_cp("arbitrary"),
    )(u, v, norm_g, w_s, b_full, dy)


def loss_head(x, gain, target):
    T, D = x.shape
    tm = _tile(T, 512)

    def body(x_ref, g_ref, t_ref, loss_ref, dx_ref, dg_ref):
        xv, g = x_ref[...], g_ref[...]
        r = lax.rsqrt(jnp.mean(xv * xv, axis=-1, keepdims=True) + EPS)
        err = xv * r * g - t_ref[...]
        part = 0.5 * jnp.sum(jnp.mean(err * err, axis=-1, keepdims=True), axis=0, keepdims=True)
        dx, dg = _rms_bwd_math(xv, g, err * (1.0 / D), jnp.zeros_like(xv))
        dx_ref[...] = dx

        @pl.when(pl.program_id(0) == 0)
        def _():
            loss_ref[...] = jnp.zeros_like(loss_ref)
            dg_ref[...] = jnp.zeros_like(dg_ref)

        loss_ref[...] += part
        dg_ref[...] += dg

    row = pl.BlockSpec((tm, D), lambda i: (i, 0))
    vec = pl.BlockSpec((1, D), lambda i: (0, 0))
    return pl.pallas_call(
        body, name="loss_head", grid=(T // tm,),
        in_specs=[row, vec, row], out_specs=[pl.BlockSpec((1, 1), lambda i: (0, 0)), row, vec],
        out_shape=[jax.ShapeDtypeStruct((1, 1), F32), jax.ShapeDtypeStruct((T, D), F32),
                   jax.ShapeDtypeStruct((1, D), F32)],
        compiler_params=_cp("arbitrary"),
    )(x, gain, target)


class MultiJob:
    def __init__(self, jobs):
        self.jobs = jobs
        self.args = [a for j in jobs for a in j.args]
        self.out_shape = [s for j in jobs for s in j.out_shape]
        self.scratch = [s for j in jobs for s in j.scratch]
        self.n_in, self.n_out = len(self.args), len(self.out_shape)

    def _each(self, ins, outs, sems):
        i = o = s = 0
        for j in self.jobs:
            yield j, ins[i:i + j.n_in], outs[o:o + j.n_out], sems[s:s + len(j.scratch)]
            i, o, s = i + j.n_in, o + j.n_out, s + len(j.scratch)

    def start(self, ins, outs, sems):
        for j, a, b, c in self._each(ins, outs, sems):
            j.start(a, b, c)

    def middle(self, ins, outs, sems):
        for j, a, b, c in self._each(ins, outs, sems):
            j.middle(a, b, c)

    def late(self, ins, outs, sems):
        for j, a, b, c in self._each(ins, outs, sems):
            j.late(a, b, c)

    def finish(self, ins, outs, sems):
        for j, a, b, c in self._each(ins, outs, sems):
            j.finish(a, b, c)

    def split(self, results):
        o = 0
        for j in self.jobs:
            yield results[o:o + j.n_out]
            o += j.n_out


class Plan:
    def __init__(self, shard, gathers, small_carrier=None, pack_small=None, exchange=True):
        self.shard, self.gathers, self.exchange = shard, gathers, exchange
        self.small_carrier, self.pack_small = small_carrier, pack_small
        self.weights, self.grads, self.started = {}, {}, []
        self.small_src = self.small_parts = None

    def scatter(self, keys, carry=None):
        if not self.exchange:
            return carry
        tag = "_".join(f"{kind}{l}" for kind, l in keys)
        send, recv, grads, zones, carry, token = scatter_start([self.grads[k] for k in keys], tag, carry)
        self.started.append((keys, send, recv, grads, zones, token, tag))
        return carry

    def collect(self, groups, after):
        got = {}
        for keys, send, recv, grads, zones, _, tag in groups:
            grads, zones = scatter_wait(send, recv, grads, zones, after, tag)
            got.update({k: (g, z) for k, g, z in zip(keys, grads, zones)})
        return got

    def weight(self, kind, l):
        return self.weights[(kind, l)]

    def grad(self, kind, l, g):
        self.grads[(kind, l)] = g

    def small_ready(self, small, d_final):
        if self.pack_small is not None:
            self.small_src = self.pack_small(small, d_final)

    def _jobs(self, key):
        jobs = []
        if key in self.gathers:
            ks = self.gathers[key]
            jobs.append((GatherJob([self.shard(*k) for k in ks]), self.weights, ks))
        if key == self.small_carrier and self.small_src is not None:
            jobs.append((GatherJob([(self.small_src, None)]), None, None))
        return jobs

    def _deliver(self, jobs, results):
        multi = MultiJob([j for j, _, _ in jobs])
        for (_, store, ks), res in zip(jobs, multi.split(results)):
            if store is None:
                self.small_parts = res[0]
            else:
                store.update(zip(ks, res))

    def run(self, key, fn, *args, **kw):
        jobs = self._jobs(key)
        if not jobs:
            out = fn(*args, **kw)
            return out if fn is mm_tn else out[0]
        res, jres = fn(*args, job=MultiJob([j for j, _, _ in jobs]), **kw)
        self._deliver(jobs, jres)
        return res

    def alone(self, key, name):
        jobs = self._jobs(key)
        if jobs:
            self._deliver(jobs, run_job(MultiJob([j for j, _, _ in jobs]), name))


def _local_step(x, target, layers, final_norm, n_ex, plan):
    T, D = x.shape
    L = len(layers)
    cst = _attn_consts(T // n_ex)
    ident = lambda j: j
    EV_A, EV_B = 3 * (D // 2), QW + 2 * KW
    OD_W = D // 2
    wt = plan.weight

    saved = []
    for l, W in enumerate(layers):
        s = dict(x0=x)
        x1, *s["gu1"] = plan.run(("ffn1_fwd", l), ffn_fwd, x, W["n1"], wt("f1_in_t", l), wt("f1_out", l))
        if l % 2 == 0:
            pa, pb, h = mm_nt(x1, wt("mx_in_t", l), [(0, EV_A), (EV_A, EV_B)], F32, True, W["nm"])
            qg = jnp.tile(W["q_norm"], N_Q_HEADS)[None]
            kg = jnp.tile(W["k_norm"], N_KV_HEADS)[None]
            mix = conv_fwd(pa, W["conv_w"], n_ex)
            qp, kp, vp = qkv_prep_fwd(pb, qg, kg, cst, n_ex)
            mix, op, lse = plan.run(("attn_fwd", l), attn_fwd, qp, kp, vp, cst["gather"], n_ex, mix)
            s.update(pa=pa, pb=pb, qg=qg, kg=kg, qp=qp, kp=kp, vp=vp, op=op, lse=lse)
        else:
            p, u, v, h = mm_nt(x1, wt("mx_in_t", l), [(0, OD_W), (OD_W, OD_W), (2 * OD_W, OD_W)], F32, True, W["nm"])
            scale = W["pool_scale"][None]
            sn = W["sgu_norm"][None]
            b_full = jnp.broadcast_to(W["sgu_b"][..., None], W["sgu_w"].shape)
            mix = sgu_fwd(u, v, sn, W["sgu_w"], b_full, pool_fwd(p, W["pool_w"], scale, n_ex))
            s.update(p=p, u=u, v=v, scale=scale, sn=sn, b_full=b_full)
        x2 = mm_nn([(mix, 0, D, 0)], wt("mx_out", l), residual=x1)
        x3, *s["gu2"] = plan.run(("ffn2_fwd", l), ffn_fwd, x2, W["n2"], wt("f2_in_t", l), wt("f2_out", l))
        s.update(x1=x1, x2=x2, h=h, mix=mix)
        saved.append(s)
        x = x3

    loss, dx, d_final = loss_head(x, final_norm, target)

    small = [None] * L

    def ffn_back(which, l, dout, xin, gain, gu_xn, sm, sm_key):
        w_in, w_out = wt(which + "_in_t", l), wt(which + "_out", l)
        F = w_out.shape[0]
        nc = F // FFN_TN
        gu, xn = gu_xn
        dxi, sm[sm_key], a, dgu, dob = plan.run((which + "_bwd", l), ffn_bwd_x, dout, xin, gain, gu, w_in, w_out)
        if which == "f1" and l == 0:
            plan.small_ready(small, d_final)
        plan.grad(which + "_in_t", l, plan.run(
            (which + "_in_grad", l), mm_tn, dgu, xn, 2 * F, lambda k, c: k * nc + c, grid=(2, nc),
            col_block=lambda k, c: 2 * c + k))
        plan.grad(which + "_out", l, plan.run((which + "_out_grad", l), mm_tn, a, dob, F, ident))
        keys = [(which + "_in_t", l), (which + "_out", l)]
        if which == "f1" and l == 0:
            plan.scatter(keys)
            return dxi
        return plan.scatter(keys, dxi)

    for l in reversed(range(L)):
        s, W = saved[l], layers[l]
        sm = small[l] = {}
        dx = ffn_back("f2", l, dx, s["x2"], W["n2"], s["gu2"], sm, "n2")
        dmix, dxb = mm_nt(dx, wt("mx_out", l), [(0, D)], BF16, emit_a_bf16=True)
        plan.grad("mx_out", l, mm_tn(s["mix"], dxb, D, ident))
        if l % 2 == 0:
            d_proj, sm["conv_w"] = conv_bwd(s["pa"], W["conv_w"], n_ex, dmix, EV_A + EV_B)
            dq, dkp, dvp = plan.run(("attn_bwd", l), attn_bwd, s["qp"], s["kp"], s["vp"], s["op"], s["lse"], dmix, cst, n_ex)
            d_proj, dqg, dkg = qkv_prep_bwd(s["pb"], s["qg"], s["kg"], cst, n_ex, dq, dkp, dvp, d_proj)
            d_pieces = [(d_proj, 0, EV_A + EV_B, 0)]
            plan.grad("mx_in_t", l, mm_tn(d_proj, s["h"], EV_A + EV_B, ident))
            sm["q_norm"] = dqg.reshape(N_Q_HEADS, HEAD_DIM).sum(0)
            sm["k_norm"] = dkg.reshape(N_KV_HEADS, HEAD_DIM).sum(0)
        else:
            d_proj, d_sn, sm["sgu_w"], d_sb = sgu_bwd(s["u"], s["v"], s["sn"], W["sgu_w"], s["b_full"], dmix)
            d_proj, sm["pool_w"], d_ps = pool_bwd(s["p"], W["pool_w"], s["scale"], n_ex, dmix, d_proj)
            d_pieces = [(d_proj, 0, OD_W, 1), (d_proj, 1, OD_W, 2), (d_proj, 2, OD_W, 0)]
            nb = OD_W // MM_TC
            plan.grad("mx_in_t", l, mm_tn(d_proj, s["h"], 3 * OD_W,
                                          lambda jj: jnp.where(jj < 2 * nb, jj + nb, jj - 2 * nb)))
            sm["pool_scale"], sm["sgu_norm"], sm["sgu_b"] = d_ps[0], d_sn[0], d_sb.sum(-1)
        dx = plan.scatter([("mx_out", l), ("mx_in_t", l)], dx)
        dx, sm["nm"] = mm_nn(d_pieces, wt("mx_in_t", l), norm_bwd=(s["x1"], W["nm"], dx))
        dx = ffn_back("f1", l, dx, s["x0"], W["n1"], s["gu1"], sm, "n1")
    return loss, dx


def all_gather(srcs):
    return run_job(GatherJob(srcs), "all_gather")


HBM_SPEC = pl.BlockSpec(memory_space=pltpu.HBM)
SEM_SPEC = pl.BlockSpec(memory_space=pltpu.SEMAPHORE)
SPLIT_COPY = pltpu.CompilerParams(has_side_effects=pltpu.SideEffectType.DATAFLOW_SIDE_EFFECTING)


def _scatter_copies(srcs, lands, send, recv, dims):
    x, y, c, me = _my_place()
    for t, (r, _) in enumerate(dims):
        for k in range(1, N_DEV):
            peer, pidx = _peer(x, y, c, k)
            rows = srcs[t].at[pl.ds(pl.multiple_of(pidx * r, 8), r), :]
            yield (_remote(rows, lands[t].at[me], send, recv, N_DEV * t + k, peer),
                   _remote(rows, lands[t].at[pidx], send, recv, N_DEV * t + k, peer))


def scatter_start(grads, tag, carry=None):
    n = len(grads)
    dims = [(g.shape[0] // N_DEV, g.shape[1]) for g in grads]
    passed = list(grads) + [lax.empty((N_DEV, r, cc), g.dtype) for g, (r, cc) in zip(grads, dims)]
    passed += [] if carry is None else [carry]
    m = len(passed)

    def body(*refs):
        srcs, lands, send, recv, token = refs[:n], refs[n:2 * n], refs[m], refs[m + 1], refs[-1]
        for mine, _ in _scatter_copies(srcs, lands, send, recv, dims):
            mine.start()
        token[...] = jnp.zeros_like(token)

    res = pl.pallas_call(
        body, name="scatter_start_" + tag,
        out_shape=(pltpu.SemaphoreType.DMA((N_DEV * n,)), pltpu.SemaphoreType.DMA((N_DEV * n,)),
                   *[pltpu.HBM(a.shape, a.dtype) for a in passed], jax.ShapeDtypeStruct((8, LANES), F32)),
        in_specs=[HBM_SPEC] * m,
        out_specs=(SEM_SPEC, SEM_SPEC, *([HBM_SPEC] * m), pl.BlockSpec(memory_space=pltpu.VMEM)),
        input_output_aliases={i: 2 + i for i in range(m)}, compiler_params=SPLIT_COPY,
    )(*[pltpu.with_memory_space_constraint(a, pltpu.HBM) for a in passed])
    return res[0], res[1], res[2:2 + n], res[2 + n:2 + 2 * n], (None if carry is None else res[2 + 2 * n]), res[-1]


def scatter_wait(send, recv, grads, zones, after, tag):
    n = len(grads)
    dims = [(g.shape[0] // N_DEV, g.shape[1]) for g in grads]

    def body(*refs):
        srcs, lands, send_ref, recv_ref = refs[:n], refs[n:2 * n], refs[2 * n], refs[2 * n + 1]
        for mine, theirs in _scatter_copies(srcs, lands, send_ref, recv_ref, dims):
            mine.wait_send()
            theirs.wait_recv()

    res = pl.pallas_call(
        body, name="scatter_wait_" + tag,
        out_shape=(*[pltpu.HBM(g.shape, g.dtype) for g in grads], *[pltpu.HBM(z.shape, z.dtype) for z in zones]),
        in_specs=[HBM_SPEC] * (2 * n) + [SEM_SPEC, SEM_SPEC, ANY], out_specs=[HBM_SPEC] * (2 * n),
        input_output_aliases={i: i for i in range(2 * n)}, compiler_params=SPLIT_COPY,
    )(*grads, *zones, send, recv, after)
    return res[:n], res[n:]


def cast_shards(w):
    L, A, B = w.shape

    def body(w_ref, o_ref):
        o_ref[...] = w_ref[...].astype(BF16)

    return pl.pallas_call(
        body, name="cast_shards", grid=(L,),
        in_specs=[pl.BlockSpec((None, A, B), lambda l: (l, 0, 0))],
        out_specs=pl.BlockSpec((None, A, B), lambda l: (l, 0, 0)),
        out_shape=jax.ShapeDtypeStruct((L, A, B), BF16), compiler_params=_cp("parallel"),
    )(w)


ADAM_TC = 256


def adamw(parts, w, m, v, l, prev=None, own=None, after=None):
    P, R, C = parts.shape
    tc = _tile(C, ADAM_TC)
    c1, c2 = 1.0 - ADAM_B1 ** ADAM_STEP, 1.0 - ADAM_B2 ** ADAM_STEP
    prev = list(prev) if prev is not None else []

    def body(*refs):
        me_ref, refs = (refs[0], refs[1:]) if own is not None else (None, refs)
        p_ref, w_ref, m_ref, v_ref = refs[:4]
        g_ref, d_ref, mo_ref, vo_ref = refs[-4:]
        if own is None:
            term = lambda s: p_ref[s].astype(F32)
        else:
            term = lambda s: jnp.where(me_ref[0] == s, refs[4][...], p_ref[s]).astype(F32)
        g = term(0)
        for s in range(1, P):
            g = g + term(s)
        m1 = ADAM_B1 * m_ref[...] + (1.0 - ADAM_B1) * g
        v1 = ADAM_B2 * v_ref[...] + (1.0 - ADAM_B2) * (g * g)
        g_ref[...] = g
        mo_ref[...] = m1
        vo_ref[...] = v1
        d_ref[...] = -ADAM_LR * ((m1 / c1) / (jnp.sqrt(v1 / c2) + ADAM_EPS) + ADAM_WD * w_ref[...])

    wspec = pl.BlockSpec((None, R, tc), lambda i, *_: (l, 0, i))
    pspec = pl.BlockSpec((P, R, tc), lambda i, *_: (0, 0, i))
    extra = prev + ([] if after is None else [after])
    out_shape = [jax.ShapeDtypeStruct(w.shape, F32)] * 4
    if own is None:
        return pl.pallas_call(
            body, name="adamw", grid=(C // tc,), in_specs=[pspec, wspec, wspec, wspec] + [ANY] * len(extra),
            out_specs=[wspec] * 4, out_shape=out_shape,
            input_output_aliases={4 + i: i for i in range(len(prev))}, compiler_params=_cp("parallel"),
        )(parts, w, m, v, *extra)
    own_sums, me = own
    ospec = pl.BlockSpec((None, R, tc), lambda i, me_ref: (me_ref[0], 0, i))
    return pl.pallas_call(
        body, name="adamw_own", out_shape=out_shape,
        grid_spec=pltpu.PrefetchScalarGridSpec(
            num_scalar_prefetch=1, grid=(C // tc,),
            in_specs=[pspec, wspec, wspec, wspec, ospec] + [ANY] * len(extra), out_specs=[wspec] * 4),
        input_output_aliases={6 + i: i for i in range(len(prev))}, compiler_params=_cp("parallel"),
    )(me, parts, w, m, v, own_sums, *extra)


_WEIGHTS = ['ffn1_norm', 'ffn1_w_in', 'ffn1_w_out', 'mix_norm', 'ffn2_norm', 'ffn2_w_in', 'ffn2_w_out', 'ev_w_in',
            'ev_conv_w', 'ev_q_norm', 'ev_k_norm', 'ev_w_out', 'od_w_in', 'od_pool_w', 'od_pool_scale', 'od_sgu_norm',
            'od_sgu_w', 'od_sgu_b', 'od_w_out', 'final_norm']
_BIG = dict(ffn1_w_in=True, ffn1_w_out=False, ffn2_w_in=True, ffn2_w_out=False,
            ev_w_in=True, ev_w_out=False, od_w_in=True, od_w_out=False)
_SMALL_SHARDED = ['ev_conv_w', 'od_pool_scale', 'od_sgu_norm']
_SMALL = [n for n in _WEIGHTS if n not in _BIG]
_PACK_ROWS = 8 * LANES


_KINDS = ("f1_in_t", "f1_out", "mx_in_t", "mx_out", "f2_in_t", "f2_out")
_CARRIER_US = dict(ffn1_fwd=105, ffn2_fwd=105, attn_fwd=115)
_GATHER_US_PER_ROW = 0.08
SMALL_CARRIER = ("f1_in_grad", 0)


def _schedule(L, rows):
    events = []
    for l in range(L):
        events += [("ffn1_fwd", l), ("mixer", l)] + ([("attn_fwd", l)] if l % 2 == 0 else []) + [("ffn2_fwd", l)]
    consumer = {"f1": "ffn1_fwd", "mx": "mixer", "f2": "ffn2_fwd"}
    queue = [(k, l) for l in range(L) for k in _KINDS]
    pos = {t: events.index((consumer[t[0][:2]], t[1])) for t in queue}
    gathers = {"first": [t for t in queue if pos[t] == 0]}
    queue = [t for t in queue if pos[t] > 0]
    carriers = [i for i, e in enumerate(events) if e[0] in _CARRIER_US]
    for i in carriers:
        budget, take = _CARRIER_US[events[i][0]], []
        later = [j for j in carriers if j > i]
        while queue:
            t = queue[0]
            cost = rows(*t) * _GATHER_US_PER_ROW
            forced = not any(j < pos[t] for j in later)
            if not forced and cost > budget:
                break
            take.append(queue.pop(0))
            budget -= cost
        if take:
            gathers[events[i]] = take
    assert not queue
    return gathers


def _pack(arrs):
    flat = jnp.concatenate([a.reshape(-1) for a in arrs])
    pad = (-flat.shape[0]) % _PACK_ROWS
    return jnp.pad(flat, (0, pad)).reshape(-1, LANES)


def _unpack(buf, shapes):
    flat, out, off = buf.reshape(-1), [], 0
    for s in shapes:
        n = math.prod(s)
        out.append(flat[off:off + n].reshape(s))
        off += n
    return out


def _unshard_last(g, lead):
    nd = len(lead)
    return jnp.moveaxis(g, 0, nd).reshape(*lead, -1)


def kernel(x, ffn1_norm, ffn1_w_in, ffn1_w_out, mix_norm, ffn2_norm, ffn2_w_in, ffn2_w_out, ev_w_in, ev_conv_w, ev_q_norm, ev_k_norm, ev_w_out, od_w_in, od_pool_w, od_pool_scale, od_sgu_norm, od_sgu_w, od_sgu_b, od_w_out, final_norm, loss_target, m_ffn1_norm, m_ffn1_w_in, m_ffn1_w_out, m_mix_norm, m_ffn2_norm, m_ffn2_w_in, m_ffn2_w_out, m_ev_w_in, m_ev_conv_w, m_ev_q_norm, m_ev_k_norm, m_ev_w_out, m_od_w_in, m_od_pool_w, m_od_pool_scale, m_od_sgu_norm, m_od_sgu_w, m_od_sgu_b, m_od_w_out, m_final_norm, v_ffn1_norm, v_ffn1_w_in, v_ffn1_w_out, v_mix_norm, v_ffn2_norm, v_ffn2_w_in, v_ffn2_w_out, v_ev_w_in, v_ev_conv_w, v_ev_q_norm, v_ev_k_norm, v_ev_w_out, v_od_w_in, v_od_pool_w, v_od_pool_scale, v_od_sgu_norm, v_od_sgu_w, v_od_sgu_b, v_od_w_out, v_final_norm):
    w = dict(zip(_WEIGHTS, (ffn1_norm, ffn1_w_in, ffn1_w_out, mix_norm, ffn2_norm, ffn2_w_in, ffn2_w_out, ev_w_in, ev_conv_w, ev_q_norm, ev_k_norm, ev_w_out, od_w_in, od_pool_w, od_pool_scale, od_sgu_norm, od_sgu_w, od_sgu_b, od_w_out, final_norm)))
    m = dict(zip(_WEIGHTS, (m_ffn1_norm, m_ffn1_w_in, m_ffn1_w_out, m_mix_norm, m_ffn2_norm, m_ffn2_w_in, m_ffn2_w_out, m_ev_w_in, m_ev_conv_w, m_ev_q_norm, m_ev_k_norm, m_ev_w_out, m_od_w_in, m_od_pool_w, m_od_pool_scale, m_od_sgu_norm, m_od_sgu_w, m_od_sgu_b, m_od_w_out, m_final_norm)))
    v = dict(zip(_WEIGHTS, (v_ffn1_norm, v_ffn1_w_in, v_ffn1_w_out, v_mix_norm, v_ffn2_norm, v_ffn2_w_in, v_ffn2_w_out, v_ev_w_in, v_ev_conv_w, v_ev_q_norm, v_ev_k_norm, v_ev_w_out, v_od_w_in, v_od_pool_w, v_od_pool_scale, v_od_sgu_norm, v_od_sgu_w, v_od_sgu_b, v_od_w_out, v_final_norm)))
    n_ex, seq, D = x.shape
    T = n_ex * seq
    L = ffn1_norm.shape[0]
    me = 4 * lax.axis_index("x") + 2 * lax.axis_index("y") + lax.axis_index("c")

    sh_small = [w[n] for n in _SMALL_SHARDED]
    packed = all_gather([(_pack(sh_small), None)])[0].reshape(N_DEV, -1)
    full_small = {}
    off = 0
    for n, a in zip(_SMALL_SHARDED, sh_small):
        cnt = math.prod(a.shape)
        full_small[n] = _unshard_last(packed[:, off:off + cnt].reshape((N_DEV,) + a.shape), a.shape[:-1])
        off += cnt

    tr = lambda a: jnp.swapaxes(a, 1, 2)
    wmv = {n: tuple(tr(d[n]) if t else d[n] for d in (w, m, v)) for n, t in _BIG.items()}
    shards = {n: cast_shards(wmv[n][0]) for n in _BIG}

    def name_of(kind, l):
        mx = "ev" if l % 2 == 0 else "od"
        return {"f1_in_t": "ffn1_w_in", "f1_out": "ffn1_w_out", "f2_in_t": "ffn2_w_in", "f2_out": "ffn2_w_out",
                "mx_in_t": mx + "_w_in", "mx_out": mx + "_w_out"}[kind], (l // 2 if kind.startswith("mx") else l)

    def shard(kind, l):
        name, idx = name_of(kind, l)
        return shards[name], idx

    g_shapes = {}

    def pack_small(small, d_final):
        ev = [sm for l, sm in enumerate(small) if l % 2 == 0]
        od = [sm for l, sm in enumerate(small) if l % 2 == 1]
        st = lambda sms, k: jnp.stack([sm[k] for sm in sms])
        g_full = dict(ffn1_norm=st(small, "n1")[:, 0], mix_norm=st(small, "nm")[:, 0], ffn2_norm=st(small, "n2")[:, 0],
                      ev_conv_w=st(ev, "conv_w"), ev_q_norm=st(ev, "q_norm"), ev_k_norm=st(ev, "k_norm"),
                      od_pool_w=st(od, "pool_w"), od_pool_scale=st(od, "pool_scale"), od_sgu_norm=st(od, "sgu_norm"),
                      od_sgu_w=st(od, "sgu_w"), od_sgu_b=st(od, "sgu_b"), final_norm=d_final[0])
        g_shapes.update({n: g_full[n].shape for n in _SMALL})
        return _pack([g_full[n] for n in _SMALL])

    gathers = _schedule(L, lambda kind, l: shards[name_of(kind, l)[0]].shape[1])
    plan = Plan(shard, gathers, SMALL_CARRIER, pack_small)
    layers = []
    for l in range(L):
        j = l // 2
        W = dict(n1=ffn1_norm[l][None], nm=mix_norm[l][None], n2=ffn2_norm[l][None])
        if l % 2 == 0:
            W.update(conv_w=full_small["ev_conv_w"][j], q_norm=ev_q_norm[j], k_norm=ev_k_norm[j])
        else:
            W.update(pool_w=od_pool_w[j], pool_scale=full_small["od_pool_scale"][j], sgu_norm=full_small["od_sgu_norm"][j],
                     sgu_w=od_sgu_w[j], sgu_b=od_sgu_b[j])
        layers.append(W)

    plan.alone("first", "gather_first")
    loss, dx = _local_step(x.reshape(T, D), loss_target.reshape(T, D), layers, final_norm[None], n_ex, plan)

    out = {n: None for n in _BIG}
    me1 = me.astype(jnp.int32).reshape(1)

    def update(arrived):
        for (kind, l), (own, parts) in arrived.items():
            name, idx = name_of(kind, l)
            out[name] = adamw(parts, *wmv[name], idx, prev=out[name], own=(own.reshape(parts.shape), me1))

    *earlier, last = plan.started
    update(plan.collect(earlier, last[5]))

    g8 = plan.small_parts.reshape(N_DEV, -1)
    cols, off = [], 0
    for n in _SMALL:
        cnt = math.prod(g_shapes[n])
        g = g8[:, off:off + cnt].reshape((N_DEV,) + g_shapes[n])
        off += cnt
        if n in _SMALL_SHARDED:
            width = w[n].shape[-1]
            g = lax.dynamic_slice_in_dim(g, me * width, width, axis=g.ndim - 1)
        cols.append(g.reshape(N_DEV, -1))
    g8 = jnp.concatenate(cols, axis=1)
    g8 = jnp.pad(g8, ((0, 0), (0, (-g8.shape[1]) % _PACK_ROWS))).reshape(N_DEV, -1, LANES)
    pk = lambda d: _pack([d[n] for n in _SMALL])[None]
    small_out = adamw(g8, pk(w), pk(m), pk(v), 0)

    update(plan.collect([last], small_out[0]))
    out = {n: [tr(a) if _BIG[n] else a for a in res] for n, res in out.items()}

    shapes = [w[n].shape for n in _SMALL]
    for i in range(4):
        for n, a in zip(_SMALL, _unpack(small_out[i], shapes)):
            out.setdefault(n, [None] * 4)[i] = a

    total = lax.psum(loss[0, 0], ("x", "y", "c"))
    return (total, dx.reshape(n_ex, seq, D), *[out[n][0] for n in _WEIGHTS], *[out[n][1] for n in _WEIGHTS],
            *[out[n][2] for n in _WEIGHTS], *[out[n][3] for n in _WEIGHTS])
```

```python
import functools
import math

import jax
import jax.numpy as jnp
from jax import lax
from jax.experimental import pallas as pl
from jax.experimental.pallas import tpu as pltpu

F32, BF16 = jnp.float32, jnp.bfloat16
EPS = 1e-6
N_DEV = 8
V7X_VMEM_BYTES = 64 * 1024 * 1024
VMEM_LIMIT = V7X_VMEM_BYTES - 8 * 1024 * 1024
LANES = 128
HEAD_DIM = 64
N_Q_HEADS = 8
N_KV_HEADS = 2
Q_PER_KV = N_Q_HEADS // N_KV_HEADS
GRID_W = 64
ROPE_THETA = 10000.0
POOL_RADII = (1, 2, 4, 8)
SGU_CHUNK = 128
GROUP = 128
ADAM_LR, ADAM_B1, ADAM_B2, ADAM_EPS, ADAM_WD, ADAM_STEP = 0.001, 0.9, 0.999, 1e-08, 0.01, 10
MESH_ID = pl.DeviceIdType.MESH


def _cp(*sem):
    return pltpu.CompilerParams(dimension_semantics=sem, vmem_limit_bytes=VMEM_LIMIT)


def _dot(a, b, ca, cb):
    return lax.dot_general(a, b, (((ca,), (cb,)), ((), ())), preferred_element_type=F32)


def _nn(a, b):
    return _dot(a, b, 1, 0)


def _nt(a, b):
    return _dot(a, b, 1, 1)


def _tn(a, b):
    return _dot(a, b, 0, 0)


def _split_mm(x, m):
    hi = x.astype(BF16)
    lo = (x - hi.astype(F32)).astype(BF16)
    return _nn(hi, m) + _nn(lo, m)


def _tile(n, pref):
    t = min(n, pref)
    assert n % t == 0, (n, pref)
    return t


ANY = pl.BlockSpec(memory_space=pl.ANY)
JOB_MIDDLE, JOB_LATE = 0.55, 0.85


def _my_place():
    x, y, c = lax.axis_index("x"), lax.axis_index("y"), lax.axis_index("c")
    return x, y, c, 4 * x + 2 * y + c


def _peer(x, y, c, k):
    px = 1 - x if k & 4 else x
    py = 1 - y if k & 2 else y
    pc = 1 - c if k & 1 else c
    return (px, py, pc), 4 * px + 2 * py + pc


def _remote(src, dst, send_sems, recv_sems, i, peer):
    return pltpu.make_async_remote_copy(src_ref=src, dst_ref=dst, send_sem=send_sems.at[i], recv_sem=recv_sems.at[i],
                                        device_id=peer, device_id_type=MESH_ID)


class GatherJob:
    def __init__(self, srcs):
        self.srcs = srcs
        self.args = [a for a, _ in srcs]
        self.dims = [a.shape[-2:] for a, _ in srcs]
        n = self.n_in = self.n_out = len(srcs)
        self.out_shape = [jax.ShapeDtypeStruct((N_DEV * r, cc), a.dtype) for (a, _), (r, cc) in zip(srcs, self.dims)]
        self.scratch = [pltpu.SemaphoreType.DMA((N_DEV * n,)), pltpu.SemaphoreType.DMA((N_DEV * n,)),
                        pltpu.SemaphoreType.DMA((n,))]

    def _rows(self, outs, t, idx):
        r = self.dims[t][0]
        return outs[t].at[pl.ds(pl.multiple_of(idx * r, 8), r), :]

    def _local(self, ins, outs, loc, t, me):
        src = ins[t] if self.srcs[t][1] is None else ins[t].at[self.srcs[t][1]]
        return src, pltpu.make_async_copy(src, self._rows(outs, t, me), loc.at[t])

    def start(self, ins, outs, sems):
        send, recv, loc = sems
        x, y, c, me = _my_place()
        for t in range(self.n_in):
            src, local = self._local(ins, outs, loc, t, me)
            local.start()
            for k in (2, 4, 1):
                _remote(src, self._rows(outs, t, me), send, recv, N_DEV * t + k, _peer(x, y, c, k)[0]).start()

    def _copy(self, outs, sems, t, origin, i, to):
        x, y, c, _ = _my_place()
        blk = self._rows(outs, t, _peer(x, y, c, origin)[1])
        return _remote(blk, blk, sems[0], sems[1], N_DEV * t + i, _peer(x, y, c, to)[0])

    def middle(self, ins, outs, sems):
        c = _my_place()[2]

        def relay(t, got, to):
            self._copy(outs, sems, t, got, got, got).wait_recv()
            self._copy(outs, sems, t, got, 6, to).start()
            self._copy(outs, sems, t, to, to, to).wait_recv()

        for t in range(self.n_in):
            pl.when(c == 1)(functools.partial(relay, t, 2, 4))
            pl.when(c == 0)(functools.partial(relay, t, 4, 2))
            for k in (2, 4):
                self._copy(outs, sems, t, k, k + 1, 1).start()

    def late(self, ins, outs, sems):
        for t in range(self.n_in):
            self._copy(outs, sems, t, 6, 6, 6).wait_recv()
            self._copy(outs, sems, t, 6, 7, 1).start()

    def finish(self, ins, outs, sems):
        send, recv, loc = sems
        x, y, c, me = _my_place()
        for t in range(self.n_in):
            for k in range(1, N_DEV):
                peer, pidx = _peer(x, y, c, k)
                blk = self._rows(outs, t, pidx)
                if k % 2 == 1:
                    _remote(blk, blk, send, recv, N_DEV * t + k, peer).wait_recv()
                _remote(blk, blk, send, recv, N_DEV * t + k, peer).wait_send()
            self._local(ins, outs, loc, t, me)[1].wait()


def _call(body, name, grid, in_specs, args, out_specs, out_shape, scratch=(), sem=(), job=None, aliases=None):
    in_specs, out_specs, out_shape, scratch = list(in_specs), list(out_specs), list(out_shape), list(scratch)
    n_in, n_out, n_scr = len(args), len(out_shape), len(scratch)
    if job is None:
        res = pl.pallas_call(body, name=name, grid=grid, in_specs=in_specs, out_specs=out_specs, out_shape=out_shape,
                             scratch_shapes=scratch, input_output_aliases=aliases or {}, compiler_params=_cp(*sem))(*args)
        return res, None
    o0 = n_in + job.n_in
    s0 = o0 + n_out + job.n_out

    def carrier(*refs):
        jin, jout, jsem = refs[n_in:o0], refs[o0 + n_out:s0], refs[s0 + n_scr:]
        ids = [pl.program_id(a) for a in range(len(grid))]
        def at(step):
            idx = []
            for g in reversed(grid):
                idx.append(step % g)
                step //= g
            return functools.reduce(jnp.logical_and, [i == j for i, j in zip(ids, reversed(idx))])

        steps = math.prod(grid)
        if grid:
            pl.when(at(0))(lambda: job.start(jin, jout, jsem))
            pl.when(at(int(steps * JOB_MIDDLE)))(lambda: job.middle(jin, jout, jsem))
            pl.when(at(int(steps * JOB_LATE)))(lambda: job.late(jin, jout, jsem))
        else:
            job.start(jin, jout, jsem)
            job.middle(jin, jout, jsem)
            job.late(jin, jout, jsem)
        body(*refs[:n_in], *refs[o0:o0 + n_out], *refs[s0:s0 + n_scr])
        if grid:
            pl.when(at(steps - 1))(lambda: job.finish(jin, jout, jsem))
        else:
            job.finish(jin, jout, jsem)

    res = pl.pallas_call(
        carrier, name=name + "_comm", grid=grid, in_specs=in_specs + [ANY] * job.n_in,
        out_specs=out_specs + [ANY] * job.n_out, out_shape=out_shape + job.out_shape,
        scratch_shapes=scratch + job.scratch, input_output_aliases=aliases or {},
        compiler_params=_cp(*(["arbitrary"] * len(grid))))(*args, *job.args)
    return res[:n_out], res[n_out:]


def run_job(job, name):
    return _call(lambda: None, name, (), [], [], [], [], job=job)[1]


@jax.custom_vjp
def bmm(x, w):
    return _nn(x.astype(BF16), w.astype(BF16))


def _bmm_fwd(x, w):
    return bmm(x, w), (x, w)


def _bmm_bwd(res, g):
    x, w = res
    gb = g.astype(BF16)
    return _nt(gb, w.astype(BF16)), _tn(x.astype(BF16), gb)


bmm.defvjp(_bmm_fwd, _bmm_bwd)


def _shift_raw(x, d):
    n = x.shape[0]
    r = pltpu.roll(x, d % n, axis=0)
    row = lax.broadcasted_iota(jnp.int32, x.shape, 0)
    keep = (row >= d) if d > 0 else (row < n + d)
    return jnp.where(keep, r, 0.0)


def shift_rows(x, d):
    @jax.custom_vjp
    def f(v):
        return _shift_raw(v, d)

    f.defvjp(lambda v: (_shift_raw(v, d), None), lambda _, g: (_shift_raw(g, -d),))
    return f(x)


def _swap_raw(x):
    n = x.shape[1]
    nxt = pltpu.roll(x, n - 1, axis=1)
    prv = pltpu.roll(x, 1, axis=1)
    lane = lax.broadcasted_iota(jnp.int32, x.shape, 1)
    return jnp.where(lane % 2 == 0, nxt, prv)


@jax.custom_vjp
def swap_pairs(x):
    return _swap_raw(x)


swap_pairs.defvjp(lambda x: (_swap_raw(x), None), lambda _, g: (_swap_raw(g),))


@jax.custom_vjp
def group_mean(x, bd):
    return _split_mm(x, bd)


group_mean.defvjp(lambda x, bd: (_split_mm(x, bd), bd), lambda bd, g: (_split_mm(g, bd), jnp.zeros_like(bd)))


def _rope_norm(x, gain, cos, sgn, bd, scale):
    xn = x * lax.rsqrt(group_mean(x * x, bd) + EPS) * gain
    return (xn * cos + swap_pairs(xn) * sgn) * scale


def _conv_gate(gb, gc, hc, w):
    z = gc * hc
    c = shift_rows(z, 1) * w[0:1] + z * w[1:2] + shift_rows(z, -1) * w[2:3]
    return gb * c


def _window_sum(p, r):
    b = f = p
    k = 1
    while k < r:
        b = b + shift_rows(b, k)
        f = f + shift_rows(f, -k)
        k *= 2
    return b + f - p + shift_rows(p, r) + shift_rows(p, -r)


def _pool_mix(p, pool_w, scale):
    n = p.shape[0]
    t = lax.broadcasted_iota(jnp.int32, (n, 1), 0)
    outs = []
    for gi, r in enumerate(POOL_RADII):
        pg = p[:, gi * GROUP:(gi + 1) * GROUP]
        cnt = (jnp.minimum(t + r, n - 1) - jnp.maximum(t - r, 0) + 1).astype(F32)
        pooled = _window_sum(pg, r) / cnt - pg
        outs.append(bmm(pooled, pool_w[gi]))
    return jnp.concatenate(outs, axis=1) * scale


def _sgu(u, v, norm_g, w_s, b_full):
    ug = jax.nn.gelu(u)
    vg = jax.nn.gelu(v)
    vn = vg * lax.rsqrt(jnp.mean(vg * vg, axis=-1, keepdims=True) + EPS) * norm_g
    cols = []
    for g in range(w_s.shape[0]):
        rows = []
        for n in range(u.shape[0] // SGU_CHUNK):
            blk = vn[n * SGU_CHUNK:(n + 1) * SGU_CHUNK, g * GROUP:(g + 1) * GROUP]
            rows.append(bmm(w_s[g], blk) + b_full[g])
        cols.append(jnp.concatenate(rows, axis=0))
    return ug * jnp.concatenate(cols, axis=1)


def _rms_bwd_math(xv, gain, dy, dres):
    r = lax.rsqrt(jnp.mean(xv * xv, axis=-1, keepdims=True) + EPS)
    xh = xv * r
    dxh = dy * gain
    dx = dres + r * (dxh - xh * jnp.mean(dxh * xh, axis=-1, keepdims=True))
    return dx, jnp.sum(dy * xh, axis=0, keepdims=True)


FFN_TN = 256


def ffn_fwd(x, gain, wt_in, w_out, job=None):
    T, D = x.shape
    F = w_out.shape[0]
    tm, tn = _tile(T, 1024), FFN_TN
    nc = F // tn

    def body(x_ref, gn_ref, wg_ref, wu_ref, wo_ref, y_ref, gu_ref, xn_s, acc_s):
        c = pl.program_id(1)

        @pl.when(c == 0)
        def _():
            xv = x_ref[...]
            r = lax.rsqrt(jnp.mean(xv * xv, axis=-1, keepdims=True) + EPS)
            xn_s[...] = (xv * r * gn_ref[...]).astype(BF16)
            acc_s[...] = jnp.zeros_like(acc_s)

        xn = xn_s[...]
        g = _nt(xn, wg_ref[...])
        u = _nt(xn, wu_ref[...])
        gu_ref[:, :tn] = g.astype(BF16)
        gu_ref[:, tn:] = u.astype(BF16)
        a = (g * jax.nn.sigmoid(g) * u).astype(BF16)
        acc_s[...] += _nn(a, wo_ref[...])

        @pl.when(c == nc - 1)
        def _():
            y_ref[...] = x_ref[...] + 0.5 * acc_s[...]

    row = pl.BlockSpec((tm, D), lambda i, c: (i, 0))
    return _call(
        body, "ffn_fwd", (T // tm, nc),
        [row, pl.BlockSpec((1, D), lambda i, c: (0, 0)),
         pl.BlockSpec((tn, D), lambda i, c: (c, 0)),
         pl.BlockSpec((tn, D), lambda i, c: (c + nc, 0)),
         pl.BlockSpec((tn, D), lambda i, c: (c, 0))],
        [x, gain, wt_in, wt_in, w_out],
        [row, pl.BlockSpec((tm, 2 * tn), lambda i, c: (i, c)), row],
        [jax.ShapeDtypeStruct((T, D), F32), jax.ShapeDtypeStruct((T, 2 * F), BF16), jax.ShapeDtypeStruct((T, D), BF16)],
        [pltpu.VMEM((tm, D), F32)], ("parallel", "arbitrary"), job)


def ffn_bwd_x(dout, x, gain, gu, wt_in, w_out, job=None):
    T, D = x.shape
    F = w_out.shape[0]
    tm, tn = _tile(T, 1024), FFN_TN
    nc = F // tn

    def body(do_ref, x_ref, gn_ref, gu_ref, wg_ref, wu_ref, wo_ref,
             dx_ref, dgn_ref, a_ref, dgu_ref, dob_ref, acc_s):
        i, c = pl.program_id(0), pl.program_id(1)

        @pl.when(c == 0)
        def _():
            dob_ref[...] = (0.5 * do_ref[...]).astype(BF16)
            acc_s[...] = jnp.zeros_like(acc_s)

        da = jnp.concatenate([_nt(dob_ref[:tm // 2, :], wo_ref[...]), _nt(dob_ref[tm // 2:, :], wo_ref[...])], axis=0)
        g = gu_ref[:, :tn].astype(F32)
        u = gu_ref[:, tn:].astype(F32)
        sig = jax.nn.sigmoid(g)
        sl = g * sig
        a_ref[...] = (sl * u).astype(BF16)
        dg = (da * u * (sig * (1.0 + g * (1.0 - sig)))).astype(BF16)
        du = (da * sl).astype(BF16)
        dgu_ref[:, :tn] = dg
        dgu_ref[:, tn:] = du
        acc_s[...] += _nn(dg, wg_ref[...]) + _nn(du, wu_ref[...])

        @pl.when(c == nc - 1)
        def _():
            dx, dgn = _rms_bwd_math(x_ref[...], gn_ref[...], acc_s[...], do_ref[...])
            dx_ref[...] = dx

            @pl.when(i == 0)
            def _():
                dgn_ref[...] = jnp.zeros_like(dgn_ref)

            dgn_ref[...] += dgn

    row = pl.BlockSpec((tm, D), lambda i, c: (i, 0))
    return _call(
        body, "ffn_bwd_x", (T // tm, nc),
        [row, row, pl.BlockSpec((1, D), lambda i, c: (0, 0)),
         pl.BlockSpec((tm, 2 * tn), lambda i, c: (i, c)),
         pl.BlockSpec((tn, D), lambda i, c: (c, 0)),
         pl.BlockSpec((tn, D), lambda i, c: (c + nc, 0)),
         pl.BlockSpec((tn, D), lambda i, c: (c, 0))],
        [dout, x, gain, gu, wt_in, wt_in, w_out],
        [row, pl.BlockSpec((1, D), lambda i, c: (0, 0)),
         pl.BlockSpec((tm, tn), lambda i, c: (i, c)),
         pl.BlockSpec((tm, 2 * tn), lambda i, c: (i, c)), row],
        [jax.ShapeDtypeStruct((T, D), F32), jax.ShapeDtypeStruct((1, D), F32),
         jax.ShapeDtypeStruct((T, F), BF16), jax.ShapeDtypeStruct((T, 2 * F), BF16),
         jax.ShapeDtypeStruct((T, D), BF16)],
        [pltpu.VMEM((tm, D), F32)], ("arbitrary", "arbitrary"), job)


MM_TM = 512
MM_TC = 256


def mm_nt(a, wt, pieces, out_dtype, emit_a_bf16=False, norm_gain=None):
    T, K = a.shape
    tm = _tile(T, MM_TM)
    npc = len(pieces)
    n_lead = 1 if norm_gain is None else 2

    def body(*refs):
        a_ref, w_refs, o_refs = refs[0], refs[n_lead:n_lead + npc], refs[n_lead + npc:]
        av = a_ref[...]
        if norm_gain is not None:
            av = av * lax.rsqrt(jnp.mean(av * av, axis=-1, keepdims=True) + EPS) * refs[1][...]
        ab = av.astype(BF16)
        for w_ref, o_ref in zip(w_refs, o_refs[:npc]):
            o_ref[...] = _nt(ab, w_ref[...]).astype(o_ref.dtype)
        if emit_a_bf16:
            o_refs[npc][...] = ab

    in_specs = [pl.BlockSpec((tm, K), lambda i: (i, 0))]
    if norm_gain is not None:
        in_specs.append(pl.BlockSpec((1, K), lambda i: (0, 0)))
    out_specs, out_shape = [], []
    for r0, n in pieces:
        assert r0 % n == 0
        in_specs.append(pl.BlockSpec((n, K), functools.partial(lambda i, b: (b, 0), b=r0 // n)))
        out_specs.append(pl.BlockSpec((tm, n), lambda i: (i, 0)))
        out_shape.append(jax.ShapeDtypeStruct((T, n), out_dtype))
    if emit_a_bf16:
        out_specs.append(pl.BlockSpec((tm, K), lambda i: (i, 0)))
        out_shape.append(jax.ShapeDtypeStruct((T, K), BF16))
    return pl.pallas_call(
        body, name="mm_nt", grid=(T // tm,), in_specs=in_specs, out_specs=out_specs, out_shape=out_shape,
        compiler_params=_cp("parallel"),
    )(a, *([] if norm_gain is None else [norm_gain]), *([wt] * npc))


def mm_nn(pieces, w, residual=None, norm_bwd=None):
    T = pieces[0][0].shape[0]
    N = w.shape[1]
    tm = _tile(T, MM_TM)
    na = len(pieces)

    def body(*refs):
        a_refs, w_refs = refs[:na], refs[na:2 * na]
        acc = refs[2 * na][...] if residual is not None else None
        for a_ref, w_ref in zip(a_refs, w_refs):
            t = _nn(a_ref[...].astype(BF16), w_ref[...])
            acc = t if acc is None else acc + t
        if norm_bwd is None:
            refs[-1][...] = acc
            return
        x_ref, g_ref, dr_ref, dx_ref, dg_ref = refs[-5:]
        dx, dg = _rms_bwd_math(x_ref[...], g_ref[...], acc, dr_ref[...])
        dx_ref[...] = dx

        @pl.when(pl.program_id(0) == 0)
        def _():
            dg_ref[...] = jnp.zeros_like(dg_ref)

        dg_ref[...] += dg

    in_specs, w_specs = [], []
    for a, cb, k, rb in pieces:
        in_specs.append(pl.BlockSpec((tm, k), functools.partial(lambda i, b: (i, b), b=cb)))
        w_specs.append(pl.BlockSpec((k, N), functools.partial(lambda i, b: (b, 0), b=rb)))
    assert sum(k for _, _, k, _ in pieces) == w.shape[0]
    args = [a for a, _, _, _ in pieces] + [w] * na
    in_specs = in_specs + w_specs
    row = pl.BlockSpec((tm, N), lambda i: (i, 0))
    if residual is not None:
        in_specs.append(row)
        args.append(residual)
    if norm_bwd is None:
        return pl.pallas_call(
            body, name="mm_nn", grid=(T // tm,), in_specs=in_specs, out_specs=row,
            out_shape=jax.ShapeDtypeStruct((T, N), F32), compiler_params=_cp("parallel"),
        )(*args)
    vec = pl.BlockSpec((1, N), lambda i: (0, 0))
    return pl.pallas_call(
        body, name="mm_nn_norm_bwd", grid=(T // tm,), in_specs=in_specs + [row, vec, row], out_specs=[row, vec],
        out_shape=[jax.ShapeDtypeStruct((T, N), F32), jax.ShapeDtypeStruct((1, N), F32)],
        compiler_params=_cp("arbitrary"),
    )(*args, *norm_bwd)


def mm_tn(a, b, n_rows, row_block, prev=None, grid=None, col_block=None, job=None):
    T, M = a.shape
    N = b.shape[1]
    tc = MM_TC
    assert M % tc == 0 and n_rows % tc == 0
    if grid is None:
        grid, col_block = (M // tc,), (lambda j: j)

    def body(*refs):
        a_ref, b_ref, o_ref = refs[0], refs[1], refs[-1]
        o_ref[...] = _tn(a_ref[...], b_ref[...]).astype(BF16)

    in_specs = [pl.BlockSpec((T, tc), lambda *g: (0, col_block(*g))), pl.BlockSpec((T, N), lambda *g: (0, 0))]
    args = [a, b]
    aliases = {}
    if prev is not None:
        in_specs.append(pl.BlockSpec(memory_space=pl.ANY))
        args.append(prev)
        aliases = {2: 0}
    res, jres = _call(body, "mm_tn", grid, in_specs, args, [pl.BlockSpec((tc, N), lambda *g: (row_block(*g), 0))],
                      [jax.ShapeDtypeStruct((n_rows, N), BF16)], (), ["parallel"] * len(grid), job, aliases)
    return res[0] if job is None else (res[0], jres)


def conv_fwd(proj_a, conv_w, n_ex):
    T, C3 = proj_a.shape
    C = C3 // 3
    S = T // n_ex

    def body(gb_ref, gc_ref, hc_ref, w_ref, o_ref):
        o_ref[...] = _conv_gate(gb_ref[...], gc_ref[...], hc_ref[...], w_ref[...]).astype(BF16)

    col = lambda k: pl.BlockSpec((S, C), functools.partial(lambda b, kk: (b, kk), kk=k))
    return pl.pallas_call(
        body, name="conv_fwd", grid=(n_ex,),
        in_specs=[col(0), col(1), col(2), pl.BlockSpec((3, C), lambda b: (0, 0))],
        out_specs=pl.BlockSpec((S, C), lambda b: (b, 0)),
        out_shape=jax.ShapeDtypeStruct((T, 2 * C), BF16), compiler_params=_cp("parallel"),
    )(proj_a, proj_a, proj_a, conv_w)


def conv_bwd(proj_a, conv_w, n_ex, dy, total_cols):
    T, C3 = proj_a.shape
    C = C3 // 3
    S = T // n_ex

    def body(gb_ref, gc_ref, hc_ref, w_ref, dy_ref, dp_ref, dw_ref):
        _, vjp = jax.vjp(_conv_gate, gb_ref[...], gc_ref[...], hc_ref[...], w_ref[...])
        dgb, dgc, dhc, dw = vjp(dy_ref[...].astype(F32))
        dp_ref[:, 0:C] = dgb.astype(BF16)
        dp_ref[:, C:2 * C] = dgc.astype(BF16)
        dp_ref[:, 2 * C:] = dhc.astype(BF16)

        @pl.when(pl.program_id(0) == 0)
        def _():
            dw_ref[...] = jnp.zeros_like(dw_ref)

        dw_ref[...] += dw

    col = lambda k: pl.BlockSpec((S, C), functools.partial(lambda b, kk: (b, kk), kk=k))
    return pl.pallas_call(
        body, name="conv_bwd", grid=(n_ex,),
        in_specs=[col(0), col(1), col(2), pl.BlockSpec((3, C), lambda b: (0, 0)),
                  pl.BlockSpec((S, C), lambda b: (b, 0))],
        out_specs=[pl.BlockSpec((S, C3), lambda b: (b, 0)), pl.BlockSpec((3, C), lambda b: (0, 0))],
        out_shape=[jax.ShapeDtypeStruct((T, total_cols), BF16), jax.ShapeDtypeStruct((3, C), F32)],
        compiler_params=_cp("arbitrary"),
    )(proj_a, proj_a, proj_a, conv_w, dy)


QW = N_Q_HEADS * HEAD_DIM
KW = N_KV_HEADS * HEAD_DIM
QP = N_Q_HEADS * LANES
KP = N_KV_HEADS * LANES


def _attn_consts(seq):
    rows = seq // GRID_W
    r_idx, c_idx = jnp.meshgrid(jnp.arange(rows), jnp.arange(GRID_W), indexing='ij')
    r_idx = r_idx.reshape(-1).astype(F32)
    c_idx = c_idx.reshape(-1).astype(F32)
    n_freq = HEAD_DIM // 4
    inv = ROPE_THETA ** (-jnp.arange(n_freq, dtype=F32) / n_freq)
    ang = jnp.concatenate([r_idx[:, None] * inv, c_idx[:, None] * inv], axis=-1)
    cos = jnp.repeat(jnp.cos(ang), 2, axis=1)
    sin = jnp.repeat(jnp.sin(ang), 2, axis=1)
    sgn = sin * jnp.tile(jnp.array([-1.0, 1.0], F32), HEAD_DIM // 2)
    cos = jnp.tile(cos, (1, N_Q_HEADS))
    sgn = jnp.tile(sgn, (1, N_Q_HEADS))
    lane = jnp.arange(QW)
    bd = jnp.where(lane[:, None] // HEAD_DIM == lane[None, :] // HEAD_DIM, 1.0 / HEAD_DIM, 0.0).astype(BF16)
    dst = (lane // HEAD_DIM) * LANES + lane % HEAD_DIM
    spread = (dst[:, None] == jnp.arange(QP)[None, :]).astype(BF16)
    return dict(cos=cos, sgn=sgn, bd=bd, spread=spread, gather=spread.T)


def qkv_prep_fwd(proj_b, qg, kg, cst, n_ex):
    T = proj_b.shape[0]
    S = T // n_ex
    tm = _tile(S, 512)
    nb = S // tm

    def body(p_ref, qg_ref, kg_ref, cos_ref, sgn_ref, bd_ref, sp_ref, q_ref, k_ref, v_ref):
        pv = p_ref[...]
        cos, sgn, bd, sp = cos_ref[...], sgn_ref[...], bd_ref[...], sp_ref[...]
        qr = _rope_norm(pv[:, :QW], qg_ref[...], cos, sgn, bd, HEAD_DIM ** -0.5)
        kr = _rope_norm(pv[:, QW:QW + KW], kg_ref[...], cos[:, :KW], sgn[:, :KW], bd[:KW, :KW], 1.0)
        q_ref[...] = _nn(qr.astype(BF16), sp).astype(BF16)
        k_ref[...] = _nn(kr.astype(BF16), sp[:KW, :KP]).astype(BF16)
        v_ref[...] = _nn(pv[:, QW + KW:].astype(BF16), sp[:KW, :KP]).astype(BF16)

    full = lambda a: pl.BlockSpec(a.shape, lambda i: (0,) * a.ndim)
    tab = pl.BlockSpec((tm, QW), lambda i: (i % nb, 0))
    return pl.pallas_call(
        body, name="qkv_prep_fwd", grid=(T // tm,),
        in_specs=[pl.BlockSpec((tm, QW + 2 * KW), lambda i: (i, 0)), full(qg), full(kg), tab, tab,
                  full(cst["bd"]), full(cst["spread"])],
        out_specs=[pl.BlockSpec((tm, QP), lambda i: (i, 0)), pl.BlockSpec((tm, KP), lambda i: (i, 0)),
                   pl.BlockSpec((tm, KP), lambda i: (i, 0))],
        out_shape=[jax.ShapeDtypeStruct((T, QP), BF16), jax.ShapeDtypeStruct((T, KP), BF16),
                   jax.ShapeDtypeStruct((T, KP), BF16)],
        compiler_params=_cp("parallel"),
    )(proj_b, qg, kg, cst["cos"], cst["sgn"], cst["bd"], cst["spread"])


def qkv_prep_bwd(proj_b, qg, kg, cst, n_ex, dq, dk_pad, dv_pad, d_proj):
    T = proj_b.shape[0]
    S = T // n_ex
    tm = _tile(S, 512)
    nb = S // tm

    def body(p_ref, qg_ref, kg_ref, cos_ref, sgn_ref, bd_ref, ga_ref, dq_ref, dk_ref, dv_ref, _kept,
             dp_ref, dqg_ref, dkg_ref):
        pv = p_ref[...]
        cos, sgn, bd, ga = cos_ref[...], sgn_ref[...], bd_ref[...], ga_ref[...]
        fq = lambda q, g: _rope_norm(q, g, cos, sgn, bd, HEAD_DIM ** -0.5)
        fk = lambda k, g: _rope_norm(k, g, cos[:, :KW], sgn[:, :KW], bd[:KW, :KW], 1.0)
        _, vq = jax.vjp(fq, pv[:, :QW], qg_ref[...])
        _, vk = jax.vjp(fk, pv[:, QW:QW + KW], kg_ref[...])
        dqp, dqg = vq(dq_ref[...])
        dkp, dkg = vk(_split_mm(dk_ref[...], ga[:KP, :KW]))
        dp_ref[:, :QW] = dqp.astype(BF16)
        dp_ref[:, QW:QW + KW] = dkp.astype(BF16)
        dp_ref[:, QW + KW:] = _split_mm(dv_ref[...], ga[:KP, :KW]).astype(BF16)

        @pl.when(pl.program_id(0) == 0)
        def _():
            dqg_ref[...] = jnp.zeros_like(dqg_ref)
            dkg_ref[...] = jnp.zeros_like(dkg_ref)

        dqg_ref[...] += dqg
        dkg_ref[...] += dkg

    full = lambda a: pl.BlockSpec(a.shape, lambda i: (0,) * a.ndim)
    tab = pl.BlockSpec((tm, QW), lambda i: (i % nb, 0))
    row = lambda n: pl.BlockSpec((tm, n), lambda i: (i, 0))
    wb = QW + 2 * KW
    assert d_proj.shape[1] % wb == 0
    last = d_proj.shape[1] // wb - 1
    return pl.pallas_call(
        body, name="qkv_prep_bwd", grid=(T // tm,),
        in_specs=[row(wb), full(qg), full(kg), tab, tab, full(cst["bd"]), full(cst["gather"]),
                  row(QW), row(KP), row(KP), ANY],
        out_specs=[pl.BlockSpec((tm, wb), lambda i: (i, last)), pl.BlockSpec((1, QW), lambda i: (0, 0)),
                   pl.BlockSpec((1, KW), lambda i: (0, 0))],
        out_shape=[jax.ShapeDtypeStruct(d_proj.shape, BF16), jax.ShapeDtypeStruct((1, QW), F32),
                   jax.ShapeDtypeStruct((1, KW), F32)],
        input_output_aliases={10: 0}, compiler_params=_cp("arbitrary"),
    )(proj_b, qg, kg, cst["cos"], cst["sgn"], cst["bd"], cst["gather"], dq, dk_pad, dv_pad, d_proj)


ATT_TQ = 256
ATT_TQ_FWD = 512


def attn_fwd(qp, kp, vp, gather, n_ex, mix, job=None):
    T = qp.shape[0]
    S = T // n_ex
    tq = _tile(S, ATT_TQ_FWD)
    nq = S // tq

    def body(q_ref, k_ref, v_ref, ga_ref, _kept, o_ref, op_ref, lse_ref):
        lane = lax.broadcasted_iota(jnp.int32, (tq, LANES), 1)
        lse_all = jnp.zeros((tq, LANES), F32)
        for h in range(N_Q_HEADS):
            kv = h // Q_PER_KV
            qh = q_ref[:, h * LANES:(h + 1) * LANES]
            s = _nt(qh, k_ref[:, kv * LANES:(kv + 1) * LANES])
            m = jnp.max(s, axis=-1, keepdims=True)
            p = jnp.exp(s - m)
            lsum = jnp.sum(p, axis=-1, keepdims=True)
            o = _nn(p.astype(BF16), v_ref[:, kv * LANES:(kv + 1) * LANES]) / lsum
            op_ref[:, h * LANES:(h + 1) * LANES] = o.astype(BF16)
            lse_all = jnp.where(lane == h, m + jnp.log(lsum), lse_all)
        lse_ref[...] = lse_all
        o_ref[...] = _nn(op_ref[...], ga_ref[...]).astype(BF16)

    blk = lambda n: pl.BlockSpec((tq, n), lambda b, i: (b * nq + i, 0))
    kvs = pl.BlockSpec((S, KP), lambda b, i: (b, 0))
    return _call(
        body, "attn_fwd", (n_ex, nq),
        [blk(QP), kvs, kvs, pl.BlockSpec(gather.shape, lambda b, i: (0, 0)), ANY], [qp, kp, vp, gather, mix],
        [pl.BlockSpec((tq, QW), lambda b, i: (b * nq + i, 1)), blk(QP), blk(LANES)],
        [jax.ShapeDtypeStruct(mix.shape, BF16), jax.ShapeDtypeStruct((T, QP), BF16),
         jax.ShapeDtypeStruct((T, LANES), F32)], (), ("parallel", "parallel"), job, {4: 0})


def attn_bwd(qp, kp, vp, op, lse, do, cst, n_ex, job=None):
    T = qp.shape[0]
    S = T // n_ex
    tq = _tile(S, ATT_TQ)
    nq = S // tq

    def body(q_ref, k_ref, v_ref, op_ref, lse_ref, do_ref, sp_ref, ga_ref, dq_ref, dk_ref, dv_ref, dqp_s):
        @pl.when(pl.program_id(1) == 0)
        def _():
            dk_ref[...] = jnp.zeros_like(dk_ref)
            dv_ref[...] = jnp.zeros_like(dv_ref)

        lane = lax.broadcasted_iota(jnp.int32, (tq, LANES), 1)
        dop = _nn(do_ref[...], sp_ref[...]).astype(BF16)
        lse_all = lse_ref[...]
        for h in range(N_Q_HEADS):
            kv = h // Q_PER_KV
            hs = slice(h * LANES, (h + 1) * LANES)
            ks = slice(kv * LANES, (kv + 1) * LANES)
            qh, kk, vv = q_ref[:, hs], k_ref[:, ks], v_ref[:, ks]
            doh = dop[:, hs]
            lse_h = jnp.sum(jnp.where(lane == h, lse_all, 0.0), axis=-1, keepdims=True)
            p = jnp.exp(_nt(qh, kk) - lse_h)
            dp = _nt(doh, vv)
            delta = jnp.sum(doh.astype(F32) * op_ref[:, hs].astype(F32), axis=-1, keepdims=True)
            ds = (p * (dp - delta)).astype(BF16)
            dqp_s[:, hs] = _nn(ds, kk)
            dk_ref[:, ks] += _tn(ds, qh)
            dv_ref[:, ks] += _tn(p.astype(BF16), doh)
        dq_ref[...] = _split_mm(dqp_s[...], ga_ref[...])

    blk = lambda n: pl.BlockSpec((tq, n), lambda b, i: (b * nq + i, 0))
    kvs = pl.BlockSpec((S, KP), lambda b, i: (b, 0))
    full = lambda a: pl.BlockSpec(a.shape, lambda b, i: (0, 0))
    return _call(
        body, "attn_bwd", (n_ex, nq),
        [blk(QP), kvs, kvs, blk(QP), blk(LANES), pl.BlockSpec((tq, QW), lambda b, i: (b * nq + i, 1)),
         full(cst["spread"]), full(cst["gather"])],
        [qp, kp, vp, op, lse, do, cst["spread"], cst["gather"]],
        [blk(QW), kvs, kvs],
        [jax.ShapeDtypeStruct((T, QW), F32), jax.ShapeDtypeStruct((T, KP), F32), jax.ShapeDtypeStruct((T, KP), F32)],
        [pltpu.VMEM((tq, QP), F32)], ("arbitrary", "arbitrary"), job)


def pool_fwd(p, pool_w, scale, n_ex):
    T, W = p.shape
    S = T // n_ex

    def body(p_ref, w_ref, s_ref, o_ref):
        o_ref[...] = _pool_mix(p_ref[...], w_ref[...], s_ref[...]).astype(BF16)

    return pl.pallas_call(
        body, name="pool_fwd", grid=(n_ex,),
        in_specs=[pl.BlockSpec((S, W), lambda b: (b, 0)),
                  pl.BlockSpec(pool_w.shape, lambda b: (0, 0, 0)),
                  pl.BlockSpec((1, W), lambda b: (0, 0))],
        out_specs=pl.BlockSpec((S, W), lambda b: (b, 0)),
        out_shape=jax.ShapeDtypeStruct((T, 2 * W), BF16), compiler_params=_cp("parallel"),
    )(p, pool_w, scale)


def pool_bwd(p, pool_w, scale, n_ex, dy, d_proj):
    T, W = p.shape
    S = T // n_ex
    last = d_proj.shape[1] // W - 1

    def body(p_ref, w_ref, s_ref, dy_ref, _kept, dp_ref, dw_ref, ds_ref):
        _, vjp = jax.vjp(_pool_mix, p_ref[...], w_ref[...], s_ref[...])
        dp, dw, ds = vjp(dy_ref[...].astype(F32))
        dp_ref[...] = dp.astype(BF16)

        @pl.when(pl.program_id(0) == 0)
        def _():
            dw_ref[...] = jnp.zeros_like(dw_ref)
            ds_ref[...] = jnp.zeros_like(ds_ref)

        dw_ref[...] += dw
        ds_ref[...] += ds

    wshape = pool_w.shape
    return pl.pallas_call(
        body, name="pool_bwd", grid=(n_ex,),
        in_specs=[pl.BlockSpec((S, W), lambda b: (b, 0)),
                  pl.BlockSpec(wshape, lambda b: (0, 0, 0)),
                  pl.BlockSpec((1, W), lambda b: (0, 0)), pl.BlockSpec((S, W), lambda b: (b, 0)), ANY],
        out_specs=[pl.BlockSpec((S, W), lambda b: (b, last)), pl.BlockSpec(wshape, lambda b: (0, 0, 0)),
                   pl.BlockSpec((1, W), lambda b: (0, 0))],
        out_shape=[jax.ShapeDtypeStruct(d_proj.shape, BF16), jax.ShapeDtypeStruct(wshape, F32),
                   jax.ShapeDtypeStruct((1, W), F32)],
        input_output_aliases={4: 0}, compiler_params=_cp("arbitrary"),
    )(p, pool_w, scale, dy, d_proj)


SGU_TS = 512


def sgu_fwd(u, v, norm_g, w_s, b_full, mix):
    T, W = u.shape
    ts = _tile(T, SGU_TS)

    def body(u_ref, v_ref, g_ref, w_ref, b_ref, _kept, o_ref):
        o_ref[...] = _sgu(u_ref[...], v_ref[...], g_ref[...], w_ref[...], b_ref[...]).astype(BF16)

    row = pl.BlockSpec((ts, W), lambda i: (i, 0))
    wsp = pl.BlockSpec(w_s.shape, lambda i: (0, 0, 0))
    return pl.pallas_call(
        body, name="sgu_fwd", grid=(T // ts,),
        in_specs=[row, row, pl.BlockSpec((1, W), lambda i: (0, 0)), wsp, wsp, ANY],
        out_specs=pl.BlockSpec((ts, W), lambda i: (i, 1)), out_shape=jax.ShapeDtypeStruct(mix.shape, BF16),
        input_output_aliases={5: 0}, compiler_params=_cp("parallel"),
    )(u, v, norm_g, w_s, b_full, mix)


def sgu_bwd(u, v, norm_g, w_s, b_full, dy):
    T, W = u.shape
    ts = _tile(T, SGU_TS)
    wshape = w_s.shape

    def body(u_ref, v_ref, g_ref, w_ref, b_ref, dy_ref, duv_ref, dg_ref, dw_ref, db_ref):
        _, vjp = jax.vjp(_sgu, u_ref[...], v_ref[...], g_ref[...], w_ref[...], b_ref[...])
        du, dv, dg, dw, db = vjp(dy_ref[...].astype(F32))
        duv_ref[:, :W] = du.astype(BF16)
        duv_ref[:, W:] = dv.astype(BF16)

        @pl.when(pl.program_id(0) == 0)
        def _():
            dg_ref[...] = jnp.zeros_like(dg_ref)
            dw_ref[...] = jnp.zeros_like(dw_ref)
            db_ref[...] = jnp.zeros_like(db_ref)

        dg_ref[...] += dg
        dw_ref[...] += dw
        db_ref[...] += db

    row = pl.BlockSpec((ts, W), lambda i: (i, 0))
    wsp = pl.BlockSpec(wshape, lambda i: (0, 0, 0))
    wout = pl.BlockSpec(wshape, lambda i: (0, 0, 0))
    vec = pl.BlockSpec((1, W), lambda i: (0, 0))
    return pl.pallas_call(
        body, name="sgu_bwd", grid=(T // ts,),
        in_specs=[row, row, vec, wsp, wsp, pl.BlockSpec((ts, W), lambda i: (i, 1))],
        out_specs=[pl.BlockSpec((ts, 2 * W), lambda i: (i, 0)), vec, wout, wout],
        out_shape=[jax.ShapeDtypeStruct((T, 3 * W), BF16),
                   jax.ShapeDtypeStruct((1, W), F32), jax.ShapeDtypeStruct(wshape, F32),
                   jax.ShapeDtypeStruct(wshape, F32)],
        compiler_params=_cp("arbitrary"),
    )(u, v, norm_g, w_s, b_full, dy)


def loss_head(x, gain, target):
    T, D = x.shape
    tm = _tile(T, 512)

    def body(x_ref, g_ref, t_ref, loss_ref, dx_ref, dg_ref):
        xv, g = x_ref[...], g_ref[...]
        r = lax.rsqrt(jnp.mean(xv * xv, axis=-1, keepdims=True) + EPS)
        err = xv * r * g - t_ref[...]
        part = 0.5 * jnp.sum(jnp.mean(err * err, axis=-1, keepdims=True), axis=0, keepdims=True)
        dx, dg = _rms_bwd_math(xv, g, err * (1.0 / D), jnp.zeros_like(xv))
        dx_ref[...] = dx

        @pl.when(pl.program_id(0) == 0)
        def _():
            loss_ref[...] = jnp.zeros_like(loss_ref)
            dg_ref[...] = jnp.zeros_like(dg_ref)

        loss_ref[...] += part
        dg_ref[...] += dg

    row = pl.BlockSpec((tm, D), lambda i: (i, 0))
    vec = pl.BlockSpec((1, D), lambda i: (0, 0))
    return pl.pallas_call(
        body, name="loss_head", grid=(T // tm,),
        in_specs=[row, vec, row], out_specs=[pl.BlockSpec((1, 1), lambda i: (0, 0)), row, vec],
        out_shape=[jax.ShapeDtypeStruct((1, 1), F32), jax.ShapeDtypeStruct((T, D), F32),
                   jax.ShapeDtypeStruct((1, D), F32)],
        compiler_params=_cp("arbitrary"),
    )(x, gain, target)


class MultiJob:
    def __init__(self, jobs):
        self.jobs = jobs
        self.args = [a for j in jobs for a in j.args]
        self.out_shape = [s for j in jobs for s in j.out_shape]
        self.scratch = [s for j in jobs for s in j.scratch]
        self.n_in, self.n_out = len(self.args), len(self.out_shape)

    def _each(self, ins, outs, sems):
        i = o = s = 0
        for j in self.jobs:
            yield j, ins[i:i + j.n_in], outs[o:o + j.n_out], sems[s:s + len(j.scratch)]
            i, o, s = i + j.n_in, o + j.n_out, s + len(j.scratch)

    def start(self, ins, outs, sems):
        for j, a, b, c in self._each(ins, outs, sems):
            j.start(a, b, c)

    def middle(self, ins, outs, sems):
        for j, a, b, c in self._each(ins, outs, sems):
            j.middle(a, b, c)

    def late(self, ins, outs, sems):
        for j, a, b, c in self._each(ins, outs, sems):
            j.late(a, b, c)

    def finish(self, ins, outs, sems):
        for j, a, b, c in self._each(ins, outs, sems):
            j.finish(a, b, c)

    def split(self, results):
        o = 0
        for j in self.jobs:
            yield results[o:o + j.n_out]
            o += j.n_out


class Plan:
    def __init__(self, shard, gathers, small_carrier=None, pack_small=None, exchange=True):
        self.shard, self.gathers, self.exchange = shard, gathers, exchange
        self.small_carrier, self.pack_small = small_carrier, pack_small
        self.weights, self.grads, self.started = {}, {}, []
        self.small_src = self.small_parts = None

    def scatter(self, keys, carry=None):
        if not self.exchange:
            return carry
        tag = "_".join(f"{kind}{l}" for kind, l in keys)
        send, recv, grads, zones, carry, token = scatter_start([self.grads[k] for k in keys], tag, carry)
        self.started.append((keys, send, recv, grads, zones, token, tag))
        return carry

    def collect(self, groups, after):
        got = {}
        for keys, send, recv, grads, zones, _, tag in groups:
            grads, zones = scatter_wait(send, recv, grads, zones, after, tag)
            got.update({k: (g, z) for k, g, z in zip(keys, grads, zones)})
        return got

    def weight(self, kind, l):
        return self.weights[(kind, l)]

    def grad(self, kind, l, g):
        self.grads[(kind, l)] = g

    def small_ready(self, small, d_final):
        if self.pack_small is not None:
            self.small_src = self.pack_small(small, d_final)

    def _jobs(self, key):
        jobs = []
        if key in self.gathers:
            ks = self.gathers[key]
            jobs.append((GatherJob([self.shard(*k) for k in ks]), self.weights, ks))
        if key == self.small_carrier and self.small_src is not None:
            jobs.append((GatherJob([(self.small_src, None)]), None, None))
        return jobs

    def _deliver(self, jobs, results):
        multi = MultiJob([j for j, _, _ in jobs])
        for (_, store, ks), res in zip(jobs, multi.split(results)):
            if store is None:
                self.small_parts = res[0]
            else:
                store.update(zip(ks, res))

    def run(self, key, fn, *args, **kw):
        jobs = self._jobs(key)
        if not jobs:
            out = fn(*args, **kw)
            return out if fn is mm_tn else out[0]
        res, jres = fn(*args, job=MultiJob([j for j, _, _ in jobs]), **kw)
        self._deliver(jobs, jres)
        return res

    def alone(self, key, name):
        jobs = self._jobs(key)
        if jobs:
            self._deliver(jobs, run_job(MultiJob([j for j, _, _ in jobs]), name))


def _local_step(x, target, layers, final_norm, n_ex, plan):
    T, D = x.shape
    L = len(layers)
    cst = _attn_consts(T // n_ex)
    ident = lambda j: j
    EV_A, EV_B = 3 * (D // 2), QW + 2 * KW
    OD_W = D // 2
    wt = plan.weight

    saved = []
    for l, W in enumerate(layers):
        s = dict(x0=x)
        x1, *s["gu1"] = plan.run(("ffn1_fwd", l), ffn_fwd, x, W["n1"], wt("f1_in_t", l), wt("f1_out", l))
        if l % 2 == 0:
            pa, pb, h = mm_nt(x1, wt("mx_in_t", l), [(0, EV_A), (EV_A, EV_B)], F32, True, W["nm"])
            qg = jnp.tile(W["q_norm"], N_Q_HEADS)[None]
            kg = jnp.tile(W["k_norm"], N_KV_HEADS)[None]
            mix = conv_fwd(pa, W["conv_w"], n_ex)
            qp, kp, vp = qkv_prep_fwd(pb, qg, kg, cst, n_ex)
            mix, op, lse = plan.run(("attn_fwd", l), attn_fwd, qp, kp, vp, cst["gather"], n_ex, mix)
            s.update(pa=pa, pb=pb, qg=qg, kg=kg, qp=qp, kp=kp, vp=vp, op=op, lse=lse)
        else:
            p, u, v, h = mm_nt(x1, wt("mx_in_t", l), [(0, OD_W), (OD_W, OD_W), (2 * OD_W, OD_W)], F32, True, W["nm"])
            scale = W["pool_scale"][None]
            sn = W["sgu_norm"][None]
            b_full = jnp.broadcast_to(W["sgu_b"][..., None], W["sgu_w"].shape)
            mix = sgu_fwd(u, v, sn, W["sgu_w"], b_full, pool_fwd(p, W["pool_w"], scale, n_ex))
            s.update(p=p, u=u, v=v, scale=scale, sn=sn, b_full=b_full)
        x2 = mm_nn([(mix, 0, D, 0)], wt("mx_out", l), residual=x1)
        x3, *s["gu2"] = plan.run(("ffn2_fwd", l), ffn_fwd, x2, W["n2"], wt("f2_in_t", l), wt("f2_out", l))
        s.update(x1=x1, x2=x2, h=h, mix=mix)
        saved.append(s)
        x = x3

    loss, dx, d_final = loss_head(x, final_norm, target)

    small = [None] * L

    def ffn_back(which, l, dout, xin, gain, gu_xn, sm, sm_key):
        w_in, w_out = wt(which + "_in_t", l), wt(which + "_out", l)
        F = w_out.shape[0]
        nc = F // FFN_TN
        gu, xn = gu_xn
        dxi, sm[sm_key], a, dgu, dob = plan.run((which + "_bwd", l), ffn_bwd_x, dout, xin, gain, gu, w_in, w_out)
        if which == "f1" and l == 0:
            plan.small_ready(small, d_final)
        plan.grad(which + "_in_t", l, plan.run(
            (which + "_in_grad", l), mm_tn, dgu, xn, 2 * F, lambda k, c: k * nc + c, grid=(2, nc),
            col_block=lambda k, c: 2 * c + k))
        plan.grad(which + "_out", l, plan.run((which + "_out_grad", l), mm_tn, a, dob, F, ident))
        keys = [(which + "_in_t", l), (which + "_out", l)]
        if which == "f1" and l == 0:
            plan.scatter(keys)
            return dxi
        if which == "f1":
            keys += [("mx_out", l), ("mx_in_t", l)]
        return plan.scatter(keys, dxi)

    for l in reversed(range(L)):
        s, W = saved[l], layers[l]
        sm = small[l] = {}
        dx = ffn_back("f2", l, dx, s["x2"], W["n2"], s["gu2"], sm, "n2")
        dmix, dxb = mm_nt(dx, wt("mx_out", l), [(0, D)], BF16, emit_a_bf16=True)
        plan.grad("mx_out", l, mm_tn(s["mix"], dxb, D, ident))
        if l % 2 == 0:
            d_proj, sm["conv_w"] = conv_bwd(s["pa"], W["conv_w"], n_ex, dmix, EV_A + EV_B)
            dq, dkp, dvp = plan.run(("attn_bwd", l), attn_bwd, s["qp"], s["kp"], s["vp"], s["op"], s["lse"], dmix, cst, n_ex)
            d_proj, dqg, dkg = qkv_prep_bwd(s["pb"], s["qg"], s["kg"], cst, n_ex, dq, dkp, dvp, d_proj)
            d_pieces = [(d_proj, 0, EV_A + EV_B, 0)]
            plan.grad("mx_in_t", l, mm_tn(d_proj, s["h"], EV_A + EV_B, ident))
            sm["q_norm"] = dqg.reshape(N_Q_HEADS, HEAD_DIM).sum(0)
            sm["k_norm"] = dkg.reshape(N_KV_HEADS, HEAD_DIM).sum(0)
        else:
            d_proj, d_sn, sm["sgu_w"], d_sb = sgu_bwd(s["u"], s["v"], s["sn"], W["sgu_w"], s["b_full"], dmix)
            d_proj, sm["pool_w"], d_ps = pool_bwd(s["p"], W["pool_w"], s["scale"], n_ex, dmix, d_proj)
            d_pieces = [(d_proj, 0, OD_W, 1), (d_proj, 1, OD_W, 2), (d_proj, 2, OD_W, 0)]
            nb = OD_W // MM_TC
            plan.grad("mx_in_t", l, mm_tn(d_proj, s["h"], 3 * OD_W,
                                          lambda jj: jnp.where(jj < 2 * nb, jj + nb, jj - 2 * nb)))
            sm["pool_scale"], sm["sgu_norm"], sm["sgu_b"] = d_ps[0], d_sn[0], d_sb.sum(-1)
        if l == 0:
            dx = plan.scatter([("mx_out", l), ("mx_in_t", l)], dx)
        dx, sm["nm"] = mm_nn(d_pieces, wt("mx_in_t", l), norm_bwd=(s["x1"], W["nm"], dx))
        dx = ffn_back("f1", l, dx, s["x0"], W["n1"], s["gu1"], sm, "n1")
    return loss, dx


def all_gather(srcs):
    return run_job(GatherJob(srcs), "all_gather")


HBM_SPEC = pl.BlockSpec(memory_space=pltpu.HBM)
SEM_SPEC = pl.BlockSpec(memory_space=pltpu.SEMAPHORE)
SPLIT_COPY = pltpu.CompilerParams(has_side_effects=pltpu.SideEffectType.DATAFLOW_SIDE_EFFECTING)


def _scatter_copies(srcs, lands, send, recv, dims):
    x, y, c, me = _my_place()
    for t, (r, _) in enumerate(dims):
        for k in range(1, N_DEV):
            peer, pidx = _peer(x, y, c, k)
            rows = srcs[t].at[pl.ds(pl.multiple_of(pidx * r, 8), r), :]
            yield (_remote(rows, lands[t].at[me], send, recv, N_DEV * t + k, peer),
                   _remote(rows, lands[t].at[pidx], send, recv, N_DEV * t + k, peer))


def scatter_start(grads, tag, carry=None):
    n = len(grads)
    dims = [(g.shape[0] // N_DEV, g.shape[1]) for g in grads]
    passed = list(grads) + [lax.empty((N_DEV, r, cc), g.dtype) for g, (r, cc) in zip(grads, dims)]
    passed += [] if carry is None else [carry]
    m = len(passed)

    def body(*refs):
        srcs, lands, send, recv, token = refs[:n], refs[n:2 * n], refs[m], refs[m + 1], refs[-1]
        for mine, _ in _scatter_copies(srcs, lands, send, recv, dims):
            mine.start()
        token[...] = jnp.zeros_like(token)

    res = pl.pallas_call(
        body, name="scatter_start_" + tag,
        out_shape=(pltpu.SemaphoreType.DMA((N_DEV * n,)), pltpu.SemaphoreType.DMA((N_DEV * n,)),
                   *[pltpu.HBM(a.shape, a.dtype) for a in passed], jax.ShapeDtypeStruct((8, LANES), F32)),
        in_specs=[HBM_SPEC] * m,
        out_specs=(SEM_SPEC, SEM_SPEC, *([HBM_SPEC] * m), pl.BlockSpec(memory_space=pltpu.VMEM)),
        input_output_aliases={i: 2 + i for i in range(m)}, compiler_params=SPLIT_COPY,
    )(*[pltpu.with_memory_space_constraint(a, pltpu.HBM) for a in passed])
    return res[0], res[1], res[2:2 + n], res[2 + n:2 + 2 * n], (None if carry is None else res[2 + 2 * n]), res[-1]


def scatter_wait(send, recv, grads, zones, after, tag):
    n = len(grads)
    dims = [(g.shape[0] // N_DEV, g.shape[1]) for g in grads]

    def body(*refs):
        srcs, lands, send_ref, recv_ref = refs[:n], refs[n:2 * n], refs[2 * n], refs[2 * n + 1]
        for mine, theirs in _scatter_copies(srcs, lands, send_ref, recv_ref, dims):
            mine.wait_send()
            theirs.wait_recv()

    res = pl.pallas_call(
        body, name="scatter_wait_" + tag,
        out_shape=(*[pltpu.HBM(g.shape, g.dtype) for g in grads], *[pltpu.HBM(z.shape, z.dtype) for z in zones]),
        in_specs=[HBM_SPEC] * (2 * n) + [SEM_SPEC, SEM_SPEC, ANY], out_specs=[HBM_SPEC] * (2 * n),
        input_output_aliases={i: i for i in range(2 * n)}, compiler_params=SPLIT_COPY,
    )(*grads, *zones, send, recv, after)
    return res[:n], res[n:]


def cast_shards(w):
    L, A, B = w.shape

    def body(w_ref, o_ref):
        o_ref[...] = w_ref[...].astype(BF16)

    return pl.pallas_call(
        body, name="cast_shards", grid=(L,),
        in_specs=[pl.BlockSpec((None, A, B), lambda l: (l, 0, 0))],
        out_specs=pl.BlockSpec((None, A, B), lambda l: (l, 0, 0)),
        out_shape=jax.ShapeDtypeStruct((L, A, B), BF16), compiler_params=_cp("parallel"),
    )(w)


ADAM_TC = 256


def adamw(parts, w, m, v, l, prev=None, own=None, after=None):
    P, R, C = parts.shape
    tc = _tile(C, ADAM_TC)
    c1, c2 = 1.0 - ADAM_B1 ** ADAM_STEP, 1.0 - ADAM_B2 ** ADAM_STEP
    prev = list(prev) if prev is not None else []

    def body(*refs):
        me_ref, refs = (refs[0], refs[1:]) if own is not None else (None, refs)
        p_ref, w_ref, m_ref, v_ref = refs[:4]
        g_ref, d_ref, mo_ref, vo_ref = refs[-4:]
        if own is None:
            term = lambda s: p_ref[s].astype(F32)
        else:
            term = lambda s: jnp.where(me_ref[0] == s, refs[4][...], p_ref[s]).astype(F32)
        g = term(0)
        for s in range(1, P):
            g = g + term(s)
        m1 = ADAM_B1 * m_ref[...] + (1.0 - ADAM_B1) * g
        v1 = ADAM_B2 * v_ref[...] + (1.0 - ADAM_B2) * (g * g)
        g_ref[...] = g
        mo_ref[...] = m1
        vo_ref[...] = v1
        d_ref[...] = -ADAM_LR * ((m1 / c1) / (jnp.sqrt(v1 / c2) + ADAM_EPS) + ADAM_WD * w_ref[...])

    wspec = pl.BlockSpec((None, R, tc), lambda i, *_: (l, 0, i))
    pspec = pl.BlockSpec((P, R, tc), lambda i, *_: (0, 0, i))
    extra = prev + ([] if after is None else [after])
    out_shape = [jax.ShapeDtypeStruct(w.shape, F32)] * 4
    if own is None:
        return pl.pallas_call(
            body, name="adamw", grid=(C // tc,), in_specs=[pspec, wspec, wspec, wspec] + [ANY] * len(extra),
            out_specs=[wspec] * 4, out_shape=out_shape,
            input_output_aliases={4 + i: i for i in range(len(prev))}, compiler_params=_cp("parallel"),
        )(parts, w, m, v, *extra)
    own_sums, me = own
    ospec = pl.BlockSpec((None, R, tc), lambda i, me_ref: (me_ref[0], 0, i))
    return pl.pallas_call(
        body, name="adamw_own", out_shape=out_shape,
        grid_spec=pltpu.PrefetchScalarGridSpec(
            num_scalar_prefetch=1, grid=(C // tc,),
            in_specs=[pspec, wspec, wspec, wspec, ospec] + [ANY] * len(extra), out_specs=[wspec] * 4),
        input_output_aliases={6 + i: i for i in range(len(prev))}, compiler_params=_cp("parallel"),
    )(me, parts, w, m, v, own_sums, *extra)


_WEIGHTS = ['ffn1_norm', 'ffn1_w_in', 'ffn1_w_out', 'mix_norm', 'ffn2_norm', 'ffn2_w_in', 'ffn2_w_out', 'ev_w_in',
            'ev_conv_w', 'ev_q_norm', 'ev_k_norm', 'ev_w_out', 'od_w_in', 'od_pool_w', 'od_pool_scale', 'od_sgu_norm',
            'od_sgu_w', 'od_sgu_b', 'od_w_out', 'final_norm']
_BIG = dict(ffn1_w_in=True, ffn1_w_out=False, ffn2_w_in=True, ffn2_w_out=False,
            ev_w_in=True, ev_w_out=False, od_w_in=True, od_w_out=False)
_SMALL_SHARDED = ['ev_conv_w', 'od_pool_scale', 'od_sgu_norm']
_SMALL = [n for n in _WEIGHTS if n not in _BIG]
_PACK_ROWS = 8 * LANES


_KINDS = ("f1_in_t", "f1_out", "mx_in_t", "mx_out", "f2_in_t", "f2_out")
_CARRIER_US = dict(ffn1_fwd=105, ffn2_fwd=105, attn_fwd=115)
_GATHER_US_PER_ROW = 0.08
SMALL_CARRIER = ("f1_in_grad", 0)


def _schedule(L, rows):
    events = []
    for l in range(L):
        events += [("ffn1_fwd", l), ("mixer", l)] + ([("attn_fwd", l)] if l % 2 == 0 else []) + [("ffn2_fwd", l)]
    consumer = {"f1": "ffn1_fwd", "mx": "mixer", "f2": "ffn2_fwd"}
    queue = [(k, l) for l in range(L) for k in _KINDS]
    pos = {t: events.index((consumer[t[0][:2]], t[1])) for t in queue}
    gathers = {"first": [t for t in queue if pos[t] == 0]}
    queue = [t for t in queue if pos[t] > 0]
    carriers = [i for i, e in enumerate(events) if e[0] in _CARRIER_US]
    for i in carriers:
        budget, take = _CARRIER_US[events[i][0]], []
        later = [j for j in carriers if j > i]
        while queue:
            t = queue[0]
            cost = rows(*t) * _GATHER_US_PER_ROW
            forced = not any(j < pos[t] for j in later)
            if not forced and cost > budget:
                break
            take.append(queue.pop(0))
            budget -= cost
        if take:
            gathers[events[i]] = take
    assert not queue
    return gathers


def _pack(arrs):
    flat = jnp.concatenate([a.reshape(-1) for a in arrs])
    pad = (-flat.shape[0]) % _PACK_ROWS
    return jnp.pad(flat, (0, pad)).reshape(-1, LANES)


def _unpack(buf, shapes):
    flat, out, off = buf.reshape(-1), [], 0
    for s in shapes:
        n = math.prod(s)
        out.append(flat[off:off + n].reshape(s))
        off += n
    return out


def _unshard_last(g, lead):
    nd = len(lead)
    return jnp.moveaxis(g, 0, nd).reshape(*lead, -1)


def kernel(x, ffn1_norm, ffn1_w_in, ffn1_w_out, mix_norm, ffn2_norm, ffn2_w_in, ffn2_w_out, ev_w_in, ev_conv_w, ev_q_norm, ev_k_norm, ev_w_out, od_w_in, od_pool_w, od_pool_scale, od_sgu_norm, od_sgu_w, od_sgu_b, od_w_out, final_norm, loss_target, m_ffn1_norm, m_ffn1_w_in, m_ffn1_w_out, m_mix_norm, m_ffn2_norm, m_ffn2_w_in, m_ffn2_w_out, m_ev_w_in, m_ev_conv_w, m_ev_q_norm, m_ev_k_norm, m_ev_w_out, m_od_w_in, m_od_pool_w, m_od_pool_scale, m_od_sgu_norm, m_od_sgu_w, m_od_sgu_b, m_od_w_out, m_final_norm, v_ffn1_norm, v_ffn1_w_in, v_ffn1_w_out, v_mix_norm, v_ffn2_norm, v_ffn2_w_in, v_ffn2_w_out, v_ev_w_in, v_ev_conv_w, v_ev_q_norm, v_ev_k_norm, v_ev_w_out, v_od_w_in, v_od_pool_w, v_od_pool_scale, v_od_sgu_norm, v_od_sgu_w, v_od_sgu_b, v_od_w_out, v_final_norm):
    w = dict(zip(_WEIGHTS, (ffn1_norm, ffn1_w_in, ffn1_w_out, mix_norm, ffn2_norm, ffn2_w_in, ffn2_w_out, ev_w_in, ev_conv_w, ev_q_norm, ev_k_norm, ev_w_out, od_w_in, od_pool_w, od_pool_scale, od_sgu_norm, od_sgu_w, od_sgu_b, od_w_out, final_norm)))
    m = dict(zip(_WEIGHTS, (m_ffn1_norm, m_ffn1_w_in, m_ffn1_w_out, m_mix_norm, m_ffn2_norm, m_ffn2_w_in, m_ffn2_w_out, m_ev_w_in, m_ev_conv_w, m_ev_q_norm, m_ev_k_norm, m_ev_w_out, m_od_w_in, m_od_pool_w, m_od_pool_scale, m_od_sgu_norm, m_od_sgu_w, m_od_sgu_b, m_od_w_out, m_final_norm)))
    v = dict(zip(_WEIGHTS, (v_ffn1_norm, v_ffn1_w_in, v_ffn1_w_out, v_mix_norm, v_ffn2_norm, v_ffn2_w_in, v_ffn2_w_out, v_ev_w_in, v_ev_conv_w, v_ev_q_norm, v_ev_k_norm, v_ev_w_out, v_od_w_in, v_od_pool_w, v_od_pool_scale, v_od_sgu_norm, v_od_sgu_w, v_od_sgu_b, v_od_w_out, v_final_norm)))
    n_ex, seq, D = x.shape
    T = n_ex * seq
    L = ffn1_norm.shape[0]
    me = 4 * lax.axis_index("x") + 2 * lax.axis_index("y") + lax.axis_index("c")

    sh_small = [w[n] for n in _SMALL_SHARDED]
    packed = all_gather([(_pack(sh_small), None)])[0].reshape(N_DEV, -1)
    full_small = {}
    off = 0
    for n, a in zip(_SMALL_SHARDED, sh_small):
        cnt = math.prod(a.shape)
        full_small[n] = _unshard_last(packed[:, off:off + cnt].reshape((N_DEV,) + a.shape), a.shape[:-1])
        off += cnt

    tr = lambda a: jnp.swapaxes(a, 1, 2)
    wmv = {n: tuple(tr(d[n]) if t else d[n] for d in (w, m, v)) for n, t in _BIG.items()}
    shards = {n: cast_shards(wmv[n][0]) for n in _BIG}

    def name_of(kind, l):
        mx = "ev" if l % 2 == 0 else "od"
        return {"f1_in_t": "ffn1_w_in", "f1_out": "ffn1_w_out", "f2_in_t": "ffn2_w_in", "f2_out": "ffn2_w_out",
                "mx_in_t": mx + "_w_in", "mx_out": mx + "_w_out"}[kind], (l // 2 if kind.startswith("mx") else l)

    def shard(kind, l):
        name, idx = name_of(kind, l)
        return shards[name], idx

    g_shapes = {}

    def pack_small(small, d_final):
        ev = [sm for l, sm in enumerate(small) if l % 2 == 0]
        od = [sm for l, sm in enumerate(small) if l % 2 == 1]
        st = lambda sms, k: jnp.stack([sm[k] for sm in sms])
        g_full = dict(ffn1_norm=st(small, "n1")[:, 0], mix_norm=st(small, "nm")[:, 0], ffn2_norm=st(small, "n2")[:, 0],
                      ev_conv_w=st(ev, "conv_w"), ev_q_norm=st(ev, "q_norm"), ev_k_norm=st(ev, "k_norm"),
                      od_pool_w=st(od, "pool_w"), od_pool_scale=st(od, "pool_scale"), od_sgu_norm=st(od, "sgu_norm"),
                      od_sgu_w=st(od, "sgu_w"), od_sgu_b=st(od, "sgu_b"), final_norm=d_final[0])
        g_shapes.update({n: g_full[n].shape for n in _SMALL})
        return _pack([g_full[n] for n in _SMALL])

    gathers = _schedule(L, lambda kind, l: shards[name_of(kind, l)[0]].shape[1])
    plan = Plan(shard, gathers, SMALL_CARRIER, pack_small)
    layers = []
    for l in range(L):
        j = l // 2
        W = dict(n1=ffn1_norm[l][None], nm=mix_norm[l][None], n2=ffn2_norm[l][None])
        if l % 2 == 0:
            W.update(conv_w=full_small["ev_conv_w"][j], q_norm=ev_q_norm[j], k_norm=ev_k_norm[j])
        else:
            W.update(pool_w=od_pool_w[j], pool_scale=full_small["od_pool_scale"][j], sgu_norm=full_small["od_sgu_norm"][j],
                     sgu_w=od_sgu_w[j], sgu_b=od_sgu_b[j])
        layers.append(W)

    plan.alone("first", "gather_first")
    loss, dx = _local_step(x.reshape(T, D), loss_target.reshape(T, D), layers, final_norm[None], n_ex, plan)

    out = {n: None for n in _BIG}
    me1 = me.astype(jnp.int32).reshape(1)

    def update(arrived):
        for (kind, l), (own, parts) in arrived.items():
            name, idx = name_of(kind, l)
            out[name] = adamw(parts, *wmv[name], idx, prev=out[name], own=(own.reshape(parts.shape), me1))

    *earlier, last = plan.started
    update(plan.collect(earlier, last[5]))

    g8 = plan.small_parts.reshape(N_DEV, -1)
    cols, off = [], 0
    for n in _SMALL:
        cnt = math.prod(g_shapes[n])
        g = g8[:, off:off + cnt].reshape((N_DEV,) + g_shapes[n])
        off += cnt
        if n in _SMALL_SHARDED:
            width = w[n].shape[-1]
            g = lax.dynamic_slice_in_dim(g, me * width, width, axis=g.ndim - 1)
        cols.append(g.reshape(N_DEV, -1))
    g8 = jnp.concatenate(cols, axis=1)
    g8 = jnp.pad(g8, ((0, 0), (0, (-g8.shape[1]) % _PACK_ROWS))).reshape(N_DEV, -1, LANES)
    pk = lambda d: _pack([d[n] for n in _SMALL])[None]
    small_out = adamw(g8, pk(w), pk(m), pk(v), 0)

    update(plan.collect([last], small_out[0]))
    out = {n: [tr(a) if _BIG[n] else a for a in res] for n, res in out.items()}

    shapes = [w[n].shape for n in _SMALL]
    for i in range(4):
        for n, a in zip(_SMALL, _unpack(small_out[i], shapes)):
            out.setdefault(n, [None] * 4)[i] = a

    total = lax.psum(loss[0, 0], ("x", "y", "c"))
    return (total, dx.reshape(n_ex, seq, D), *[out[n][0] for n in _WEIGHTS], *[out[n][1] for n in _WEIGHTS],
            *[out[n][2] for n in _WEIGHTS], *[out[n][3] for n in _WEIGHTS])
```

```python
import functools
import math

import jax
import jax.numpy as jnp
from jax import lax
from jax.experimental import pallas as pl
from jax.experimental.pallas import tpu as pltpu

F32, BF16 = jnp.float32, jnp.bfloat16
EPS = 1e-6
N_DEV = 8
V7X_VMEM_BYTES = 64 * 1024 * 1024
VMEM_LIMIT = V7X_VMEM_BYTES - 8 * 1024 * 1024
LANES = 128
HEAD_DIM = 64
N_Q_HEADS = 8
N_KV_HEADS = 2
Q_PER_KV = N_Q_HEADS // N_KV_HEADS
GRID_W = 64
ROPE_THETA = 10000.0
POOL_RADII = (1, 2, 4, 8)
SGU_CHUNK = 128
GROUP = 128
ADAM_LR, ADAM_B1, ADAM_B2, ADAM_EPS, ADAM_WD, ADAM_STEP = 0.001, 0.9, 0.999, 1e-08, 0.01, 10
MESH_ID = pl.DeviceIdType.MESH


def _cp(*sem):
    return pltpu.CompilerParams(dimension_semantics=sem, vmem_limit_bytes=VMEM_LIMIT)


def _dot(a, b, ca, cb):
    return lax.dot_general(a, b, (((ca,), (cb,)), ((), ())), preferred_element_type=F32)


def _nn(a, b):
    return _dot(a, b, 1, 0)


def _nt(a, b):
    return _dot(a, b, 1, 1)


def _tn(a, b):
    return _dot(a, b, 0, 0)


def _split_mm(x, m):
    hi = x.astype(BF16)
    lo = (x - hi.astype(F32)).astype(BF16)
    return _nn(hi, m) + _nn(lo, m)


def _tile(n, pref):
    t = min(n, pref)
    assert n % t == 0, (n, pref)
    return t


ANY = pl.BlockSpec(memory_space=pl.ANY)
JOB_MIDDLE, JOB_LATE = 0.55, 0.85


def _my_place():
    x, y, c = lax.axis_index("x"), lax.axis_index("y"), lax.axis_index("c")
    return x, y, c, 4 * x + 2 * y + c


def _peer(x, y, c, k):
    px = 1 - x if k & 4 else x
    py = 1 - y if k & 2 else y
    pc = 1 - c if k & 1 else c
    return (px, py, pc), 4 * px + 2 * py + pc


def _remote(src, dst, send_sems, recv_sems, i, peer):
    return pltpu.make_async_remote_copy(src_ref=src, dst_ref=dst, send_sem=send_sems.at[i], recv_sem=recv_sems.at[i],
                                        device_id=peer, device_id_type=MESH_ID)


class GatherJob:
    def __init__(self, srcs):
        self.srcs = srcs
        self.args = [a for a, _ in srcs]
        self.dims = [a.shape[-2:] for a, _ in srcs]
        n = self.n_in = self.n_out = len(srcs)
        self.out_shape = [jax.ShapeDtypeStruct((N_DEV * r, cc), a.dtype) for (a, _), (r, cc) in zip(srcs, self.dims)]
        self.scratch = [pltpu.SemaphoreType.DMA((N_DEV * n,)), pltpu.SemaphoreType.DMA((N_DEV * n,)),
                        pltpu.SemaphoreType.DMA((n,))]

    def _rows(self, outs, t, idx):
        r = self.dims[t][0]
        return outs[t].at[pl.ds(pl.multiple_of(idx * r, 8), r), :]

    def _local(self, ins, outs, loc, t, me):
        src = ins[t] if self.srcs[t][1] is None else ins[t].at[self.srcs[t][1]]
        return src, pltpu.make_async_copy(src, self._rows(outs, t, me), loc.at[t])

    def start(self, ins, outs, sems):
        send, recv, loc = sems
        x, y, c, me = _my_place()
        for t in range(self.n_in):
            src, local = self._local(ins, outs, loc, t, me)
            local.start()
            for k in (2, 4, 1):
                _remote(src, self._rows(outs, t, me), send, recv, N_DEV * t + k, _peer(x, y, c, k)[0]).start()

    def _copy(self, outs, sems, t, origin, i, to):
        x, y, c, _ = _my_place()
        blk = self._rows(outs, t, _peer(x, y, c, origin)[1])
        return _remote(blk, blk, sems[0], sems[1], N_DEV * t + i, _peer(x, y, c, to)[0])

    def middle(self, ins, outs, sems):
        c = _my_place()[2]

        def relay(t, got, to):
            self._copy(outs, sems, t, got, got, got).wait_recv()
            self._copy(outs, sems, t, got, 6, to).start()
            self._copy(outs, sems, t, to, to, to).wait_recv()

        for t in range(self.n_in):
            pl.when(c == 1)(functools.partial(relay, t, 2, 4))
            pl.when(c == 0)(functools.partial(relay, t, 4, 2))
            for k in (2, 4):
                self._copy(outs, sems, t, k, k + 1, 1).start()

    def late(self, ins, outs, sems):
        for t in range(self.n_in):
            self._copy(outs, sems, t, 6, 6, 6).wait_recv()
            self._copy(outs, sems, t, 6, 7, 1).start()

    def finish(self, ins, outs, sems):
        send, recv, loc = sems
        x, y, c, me = _my_place()
        for t in range(self.n_in):
            for k in range(1, N_DEV):
                peer, pidx = _peer(x, y, c, k)
                blk = self._rows(outs, t, pidx)
                if k % 2 == 1:
                    _remote(blk, blk, send, recv, N_DEV * t + k, peer).wait_recv()
                _remote(blk, blk, send, recv, N_DEV * t + k, peer).wait_send()
            self._local(ins, outs, loc, t, me)[1].wait()


def _call(body, name, grid, in_specs, args, out_specs, out_shape, scratch=(), sem=(), job=None, aliases=None):
    in_specs, out_specs, out_shape, scratch = list(in_specs), list(out_specs), list(out_shape), list(scratch)
    n_in, n_out, n_scr = len(args), len(out_shape), len(scratch)
    if job is None:
        res = pl.pallas_call(body, name=name, grid=grid, in_specs=in_specs, out_specs=out_specs, out_shape=out_shape,
                             scratch_shapes=scratch, input_output_aliases=aliases or {}, compiler_params=_cp(*sem))(*args)
        return res, None
    o0 = n_in + job.n_in
    s0 = o0 + n_out + job.n_out

    def carrier(*refs):
        jin, jout, jsem = refs[n_in:o0], refs[o0 + n_out:s0], refs[s0 + n_scr:]
        ids = [pl.program_id(a) for a in range(len(grid))]
        def at(step):
            idx = []
            for g in reversed(grid):
                idx.append(step % g)
                step //= g
            return functools.reduce(jnp.logical_and, [i == j for i, j in zip(ids, reversed(idx))])

        steps = math.prod(grid)
        if grid:
            pl.when(at(0))(lambda: job.start(jin, jout, jsem))
            pl.when(at(int(steps * JOB_MIDDLE)))(lambda: job.middle(jin, jout, jsem))
            pl.when(at(int(steps * JOB_LATE)))(lambda: job.late(jin, jout, jsem))
        else:
            job.start(jin, jout, jsem)
            job.middle(jin, jout, jsem)
            job.late(jin, jout, jsem)
        body(*refs[:n_in], *refs[o0:o0 + n_out], *refs[s0:s0 + n_scr])
        if grid:
            pl.when(at(steps - 1))(lambda: job.finish(jin, jout, jsem))
        else:
            job.finish(jin, jout, jsem)

    res = pl.pallas_call(
        carrier, name=name + "_comm", grid=grid, in_specs=in_specs + [ANY] * job.n_in,
        out_specs=out_specs + [ANY] * job.n_out, out_shape=out_shape + job.out_shape,
        scratch_shapes=scratch + job.scratch, input_output_aliases=aliases or {},
        compiler_params=_cp(*(["arbitrary"] * len(grid))))(*args, *job.args)
    return res[:n_out], res[n_out:]


def run_job(job, name):
    return _call(lambda: None, name, (), [], [], [], [], job=job)[1]


@jax.custom_vjp
def bmm(x, w):
    return _nn(x.astype(BF16), w.astype(BF16))


def _bmm_fwd(x, w):
    return bmm(x, w), (x, w)


def _bmm_bwd(res, g):
    x, w = res
    gb = g.astype(BF16)
    return _nt(gb, w.astype(BF16)), _tn(x.astype(BF16), gb)


bmm.defvjp(_bmm_fwd, _bmm_bwd)


def _shift_raw(x, d):
    n = x.shape[0]
    r = pltpu.roll(x, d % n, axis=0)
    row = lax.broadcasted_iota(jnp.int32, x.shape, 0)
    keep = (row >= d) if d > 0 else (row < n + d)
    return jnp.where(keep, r, 0.0)


def shift_rows(x, d):
    @jax.custom_vjp
    def f(v):
        return _shift_raw(v, d)

    f.defvjp(lambda v: (_shift_raw(v, d), None), lambda _, g: (_shift_raw(g, -d),))
    return f(x)


def _swap_raw(x):
    n = x.shape[1]
    nxt = pltpu.roll(x, n - 1, axis=1)
    prv = pltpu.roll(x, 1, axis=1)
    lane = lax.broadcasted_iota(jnp.int32, x.shape, 1)
    return jnp.where(lane % 2 == 0, nxt, prv)


@jax.custom_vjp
def swap_pairs(x):
    return _swap_raw(x)


swap_pairs.defvjp(lambda x: (_swap_raw(x), None), lambda _, g: (_swap_raw(g),))


@jax.custom_vjp
def group_mean(x, bd):
    return _split_mm(x, bd)


group_mean.defvjp(lambda x, bd: (_split_mm(x, bd), bd), lambda bd, g: (_split_mm(g, bd), jnp.zeros_like(bd)))


def _rope_norm(x, gain, cos, sgn, bd, scale):
    xn = x * lax.rsqrt(group_mean(x * x, bd) + EPS) * gain
    return (xn * cos + swap_pairs(xn) * sgn) * scale


def _conv_gate(gb, gc, hc, w):
    z = gc * hc
    c = shift_rows(z, 1) * w[0:1] + z * w[1:2] + shift_rows(z, -1) * w[2:3]
    return gb * c


def _window_sum(p, r):
    b = f = p
    k = 1
    while k < r:
        b = b + shift_rows(b, k)
        f = f + shift_rows(f, -k)
        k *= 2
    return b + f - p + shift_rows(p, r) + shift_rows(p, -r)


def _pool_mix(p, pool_w, scale):
    n = p.shape[0]
    t = lax.broadcasted_iota(jnp.int32, (n, 1), 0)
    outs = []
    for gi, r in enumerate(POOL_RADII):
        pg = p[:, gi * GROUP:(gi + 1) * GROUP]
        cnt = (jnp.minimum(t + r, n - 1) - jnp.maximum(t - r, 0) + 1).astype(F32)
        pooled = _window_sum(pg, r) / cnt - pg
        outs.append(bmm(pooled, pool_w[gi]))
    return jnp.concatenate(outs, axis=1) * scale


def _sgu(u, v, norm_g, w_s, b_full):
    ug = jax.nn.gelu(u)
    vg = jax.nn.gelu(v)
    vn = vg * lax.rsqrt(jnp.mean(vg * vg, axis=-1, keepdims=True) + EPS) * norm_g
    cols = []
    for g in range(w_s.shape[0]):
        rows = []
        for n in range(u.shape[0] // SGU_CHUNK):
            blk = vn[n * SGU_CHUNK:(n + 1) * SGU_CHUNK, g * GROUP:(g + 1) * GROUP]
            rows.append(bmm(w_s[g], blk) + b_full[g])
        cols.append(jnp.concatenate(rows, axis=0))
    return ug * jnp.concatenate(cols, axis=1)


def _rms_bwd_math(xv, gain, dy, dres):
    r = lax.rsqrt(jnp.mean(xv * xv, axis=-1, keepdims=True) + EPS)
    xh = xv * r
    dxh = dy * gain
    dx = dres + r * (dxh - xh * jnp.mean(dxh * xh, axis=-1, keepdims=True))
    return dx, jnp.sum(dy * xh, axis=0, keepdims=True)


FFN_TN = 256


def ffn_fwd(x, gain, wt_in, w_out, job=None):
    T, D = x.shape
    F = w_out.shape[0]
    tm, tn = _tile(T, 1024), FFN_TN
    nc = F // tn

    def body(x_ref, gn_ref, wg_ref, wu_ref, wo_ref, y_ref, gu_ref, xn_s, acc_s):
        c = pl.program_id(1)

        @pl.when(c == 0)
        def _():
            xv = x_ref[...]
            r = lax.rsqrt(jnp.mean(xv * xv, axis=-1, keepdims=True) + EPS)
            xn_s[...] = (xv * r * gn_ref[...]).astype(BF16)
            acc_s[...] = jnp.zeros_like(acc_s)

        xn = xn_s[...]
        g = _nt(xn, wg_ref[...])
        u = _nt(xn, wu_ref[...])
        gu_ref[:, :tn] = g.astype(BF16)
        gu_ref[:, tn:] = u.astype(BF16)
        a = (g * jax.nn.sigmoid(g) * u).astype(BF16)
        acc_s[...] += _nn(a, wo_ref[...])

        @pl.when(c == nc - 1)
        def _():
            y_ref[...] = x_ref[...] + 0.5 * acc_s[...]

    row = pl.BlockSpec((tm, D), lambda i, c: (i, 0))
    return _call(
        body, "ffn_fwd", (T // tm, nc),
        [row, pl.BlockSpec((1, D), lambda i, c: (0, 0)),
         pl.BlockSpec((tn, D), lambda i, c: (c, 0)),
         pl.BlockSpec((tn, D), lambda i, c: (c + nc, 0)),
         pl.BlockSpec((tn, D), lambda i, c: (c, 0))],
        [x, gain, wt_in, wt_in, w_out],
        [row, pl.BlockSpec((tm, 2 * tn), lambda i, c: (i, c)), row],
        [jax.ShapeDtypeStruct((T, D), F32), jax.ShapeDtypeStruct((T, 2 * F), BF16), jax.ShapeDtypeStruct((T, D), BF16)],
        [pltpu.VMEM((tm, D), F32)], ("parallel", "arbitrary"), job)


def ffn_bwd_x(dout, x, gain, gu, wt_in, w_out, job=None):
    T, D = x.shape
    F = w_out.shape[0]
    tm, tn = _tile(T, 1024), FFN_TN
    nc = F // tn

    def body(do_ref, x_ref, gn_ref, gu_ref, wg_ref, wu_ref, wo_ref,
             dx_ref, dgn_ref, a_ref, dgu_ref, dob_ref, acc_s):
        i, c = pl.program_id(0), pl.program_id(1)

        @pl.when(c == 0)
        def _():
            dob_ref[...] = (0.5 * do_ref[...]).astype(BF16)
            acc_s[...] = jnp.zeros_like(acc_s)

        da = jnp.concatenate([_nt(dob_ref[:tm // 2, :], wo_ref[...]), _nt(dob_ref[tm // 2:, :], wo_ref[...])], axis=0)
        g = gu_ref[:, :tn].astype(F32)
        u = gu_ref[:, tn:].astype(F32)
        sig = jax.nn.sigmoid(g)
        sl = g * sig
        a_ref[...] = (sl * u).astype(BF16)
        dg = (da * u * (sig * (1.0 + g * (1.0 - sig)))).astype(BF16)
        du = (da * sl).astype(BF16)
        dgu_ref[:, :tn] = dg
        dgu_ref[:, tn:] = du
        acc_s[...] += _nn(dg, wg_ref[...]) + _nn(du, wu_ref[...])

        @pl.when(c == nc - 1)
        def _():
            dx, dgn = _rms_bwd_math(x_ref[...], gn_ref[...], acc_s[...], do_ref[...])
            dx_ref[...] = dx

            @pl.when(i == 0)
            def _():
                dgn_ref[...] = jnp.zeros_like(dgn_ref)

            dgn_ref[...] += dgn

    row = pl.BlockSpec((tm, D), lambda i, c: (i, 0))
    return _call(
        body, "ffn_bwd_x", (T // tm, nc),
        [row, row, pl.BlockSpec((1, D), lambda i, c: (0, 0)),
         pl.BlockSpec((tm, 2 * tn), lambda i, c: (i, c)),
         pl.BlockSpec((tn, D), lambda i, c: (c, 0)),
         pl.BlockSpec((tn, D), lambda i, c: (c + nc, 0)),
         pl.BlockSpec((tn, D), lambda i, c: (c, 0))],
        [dout, x, gain, gu, wt_in, wt_in, w_out],
        [row, pl.BlockSpec((1, D), lambda i, c: (0, 0)),
         pl.BlockSpec((tm, tn), lambda i, c: (i, c)),
         pl.BlockSpec((tm, 2 * tn), lambda i, c: (i, c)), row],
        [jax.ShapeDtypeStruct((T, D), F32), jax.ShapeDtypeStruct((1, D), F32),
         jax.ShapeDtypeStruct((T, F), BF16), jax.ShapeDtypeStruct((T, 2 * F), BF16),
         jax.ShapeDtypeStruct((T, D), BF16)],
        [pltpu.VMEM((tm, D), F32)], ("arbitrary", "arbitrary"), job)


MM_TM = 512
MM_TC = 256


def mm_nt(a, wt, pieces, out_dtype, emit_a_bf16=False, norm_gain=None):
    T, K = a.shape
    tm = _tile(T, MM_TM)
    npc = len(pieces)
    n_lead = 1 if norm_gain is None else 2

    def body(*refs):
        a_ref, w_refs, o_refs = refs[0], refs[n_lead:n_lead + npc], refs[n_lead + npc:]
        av = a_ref[...]
        if norm_gain is not None:
            av = av * lax.rsqrt(jnp.mean(av * av, axis=-1, keepdims=True) + EPS) * refs[1][...]
        ab = av.astype(BF16)
        for w_ref, o_ref in zip(w_refs, o_refs[:npc]):
            o_ref[...] = _nt(ab, w_ref[...]).astype(o_ref.dtype)
        if emit_a_bf16:
            o_refs[npc][...] = ab

    in_specs = [pl.BlockSpec((tm, K), lambda i: (i, 0))]
    if norm_gain is not None:
        in_specs.append(pl.BlockSpec((1, K), lambda i: (0, 0)))
    out_specs, out_shape = [], []
    for r0, n in pieces:
        assert r0 % n == 0
        in_specs.append(pl.BlockSpec((n, K), functools.partial(lambda i, b: (b, 0), b=r0 // n)))
        out_specs.append(pl.BlockSpec((tm, n), lambda i: (i, 0)))
        out_shape.append(jax.ShapeDtypeStruct((T, n), out_dtype))
    if emit_a_bf16:
        out_specs.append(pl.BlockSpec((tm, K), lambda i: (i, 0)))
        out_shape.append(jax.ShapeDtypeStruct((T, K), BF16))
    return pl.pallas_call(
        body, name="mm_nt", grid=(T // tm,), in_specs=in_specs, out_specs=out_specs, out_shape=out_shape,
        compiler_params=_cp("parallel"),
    )(a, *([] if norm_gain is None else [norm_gain]), *([wt] * npc))


def mm_nn(pieces, w, residual=None, norm_bwd=None):
    T = pieces[0][0].shape[0]
    N = w.shape[1]
    tm = _tile(T, MM_TM)
    na = len(pieces)

    def body(*refs):
        a_refs, w_refs = refs[:na], refs[na:2 * na]
        acc = refs[2 * na][...] if residual is not None else None
        for a_ref, w_ref in zip(a_refs, w_refs):
            t = _nn(a_ref[...].astype(BF16), w_ref[...])
            acc = t if acc is None else acc + t
        if norm_bwd is None:
            refs[-1][...] = acc
            return
        x_ref, g_ref, dr_ref, dx_ref, dg_ref = refs[-5:]
        dx, dg = _rms_bwd_math(x_ref[...], g_ref[...], acc, dr_ref[...])
        dx_ref[...] = dx

        @pl.when(pl.program_id(0) == 0)
        def _():
            dg_ref[...] = jnp.zeros_like(dg_ref)

        dg_ref[...] += dg

    in_specs, w_specs = [], []
    for a, cb, k, rb in pieces:
        in_specs.append(pl.BlockSpec((tm, k), functools.partial(lambda i, b: (i, b), b=cb)))
        w_specs.append(pl.BlockSpec((k, N), functools.partial(lambda i, b: (b, 0), b=rb)))
    assert sum(k for _, _, k, _ in pieces) == w.shape[0]
    args = [a for a, _, _, _ in pieces] + [w] * na
    in_specs = in_specs + w_specs
    row = pl.BlockSpec((tm, N), lambda i: (i, 0))
    if residual is not None:
        in_specs.append(row)
        args.append(residual)
    if norm_bwd is None:
        return pl.pallas_call(
            body, name="mm_nn", grid=(T // tm,), in_specs=in_specs, out_specs=row,
            out_shape=jax.ShapeDtypeStruct((T, N), F32), compiler_params=_cp("parallel"),
        )(*args)
    vec = pl.BlockSpec((1, N), lambda i: (0, 0))
    return pl.pallas_call(
        body, name="mm_nn_norm_bwd", grid=(T // tm,), in_specs=in_specs + [row, vec, row], out_specs=[row, vec],
        out_shape=[jax.ShapeDtypeStruct((T, N), F32), jax.ShapeDtypeStruct((1, N), F32)],
        compiler_params=_cp("arbitrary"),
    )(*args, *norm_bwd)


def mm_tn(a, b, n_rows, row_block, prev=None, grid=None, col_block=None, job=None):
    T, M = a.shape
    N = b.shape[1]
    tc = MM_TC
    assert M % tc == 0 and n_rows % tc == 0
    if grid is None:
        grid, col_block = (M // tc,), (lambda j: j)

    def body(*refs):
        a_ref, b_ref, o_ref = refs[0], refs[1], refs[-1]
        o_ref[...] = _tn(a_ref[...], b_ref[...]).astype(BF16)

    in_specs = [pl.BlockSpec((T, tc), lambda *g: (0, col_block(*g))), pl.BlockSpec((T, N), lambda *g: (0, 0))]
    args = [a, b]
    aliases = {}
    if prev is not None:
        in_specs.append(pl.BlockSpec(memory_space=pl.ANY))
        args.append(prev)
        aliases = {2: 0}
    res, jres = _call(body, "mm_tn", grid, in_specs, args, [pl.BlockSpec((tc, N), lambda *g: (row_block(*g), 0))],
                      [jax.ShapeDtypeStruct((n_rows, N), BF16)], (), ["parallel"] * len(grid), job, aliases)
    return res[0] if job is None else (res[0], jres)


def conv_fwd(proj_a, conv_w, n_ex):
    T, C3 = proj_a.shape
    C = C3 // 3
    S = T // n_ex

    def body(gb_ref, gc_ref, hc_ref, w_ref, o_ref):
        o_ref[...] = _conv_gate(gb_ref[...], gc_ref[...], hc_ref[...], w_ref[...]).astype(BF16)

    col = lambda k: pl.BlockSpec((S, C), functools.partial(lambda b, kk: (b, kk), kk=k))
    return pl.pallas_call(
        body, name="conv_fwd", grid=(n_ex,),
        in_specs=[col(0), col(1), col(2), pl.BlockSpec((3, C), lambda b: (0, 0))],
        out_specs=pl.BlockSpec((S, C), lambda b: (b, 0)),
        out_shape=jax.ShapeDtypeStruct((T, 2 * C), BF16), compiler_params=_cp("parallel"),
    )(proj_a, proj_a, proj_a, conv_w)


def conv_bwd(proj_a, conv_w, n_ex, dy, total_cols):
    T, C3 = proj_a.shape
    C = C3 // 3
    S = T // n_ex

    def body(gb_ref, gc_ref, hc_ref, w_ref, dy_ref, dp_ref, dw_ref):
        _, vjp = jax.vjp(_conv_gate, gb_ref[...], gc_ref[...], hc_ref[...], w_ref[...])
        dgb, dgc, dhc, dw = vjp(dy_ref[...].astype(F32))
        dp_ref[:, 0:C] = dgb.astype(BF16)
        dp_ref[:, C:2 * C] = dgc.astype(BF16)
        dp_ref[:, 2 * C:] = dhc.astype(BF16)

        @pl.when(pl.program_id(0) == 0)
        def _():
            dw_ref[...] = jnp.zeros_like(dw_ref)

        dw_ref[...] += dw

    col = lambda k: pl.BlockSpec((S, C), functools.partial(lambda b, kk: (b, kk), kk=k))
    return pl.pallas_call(
        body, name="conv_bwd", grid=(n_ex,),
        in_specs=[col(0), col(1), col(2), pl.BlockSpec((3, C), lambda b: (0, 0)),
                  pl.BlockSpec((S, C), lambda b: (b, 0))],
        out_specs=[pl.BlockSpec((S, C3), lambda b: (b, 0)), pl.BlockSpec((3, C), lambda b: (0, 0))],
        out_shape=[jax.ShapeDtypeStruct((T, total_cols), BF16), jax.ShapeDtypeStruct((3, C), F32)],
        compiler_params=_cp("arbitrary"),
    )(proj_a, proj_a, proj_a, conv_w, dy)


QW = N_Q_HEADS * HEAD_DIM
KW = N_KV_HEADS * HEAD_DIM
QP = N_Q_HEADS * LANES
KP = N_KV_HEADS * LANES


def _attn_consts(seq):
    rows = seq // GRID_W
    r_idx, c_idx = jnp.meshgrid(jnp.arange(rows), jnp.arange(GRID_W), indexing='ij')
    r_idx = r_idx.reshape(-1).astype(F32)
    c_idx = c_idx.reshape(-1).astype(F32)
    n_freq = HEAD_DIM // 4
    inv = ROPE_THETA ** (-jnp.arange(n_freq, dtype=F32) / n_freq)
    ang = jnp.concatenate([r_idx[:, None] * inv, c_idx[:, None] * inv], axis=-1)
    cos = jnp.repeat(jnp.cos(ang), 2, axis=1)
    sin = jnp.repeat(jnp.sin(ang), 2, axis=1)
    sgn = sin * jnp.tile(jnp.array([-1.0, 1.0], F32), HEAD_DIM // 2)
    cos = jnp.tile(cos, (1, N_Q_HEADS))
    sgn = jnp.tile(sgn, (1, N_Q_HEADS))
    lane = jnp.arange(QW)
    bd = jnp.where(lane[:, None] // HEAD_DIM == lane[None, :] // HEAD_DIM, 1.0 / HEAD_DIM, 0.0).astype(BF16)
    dst = (lane // HEAD_DIM) * LANES + lane % HEAD_DIM
    spread = (dst[:, None] == jnp.arange(QP)[None, :]).astype(BF16)
    return dict(cos=cos, sgn=sgn, bd=bd, spread=spread, gather=spread.T)


def qkv_prep_fwd(proj_b, qg, kg, cst, n_ex):
    T = proj_b.shape[0]
    S = T // n_ex
    tm = _tile(S, 512)
    nb = S // tm

    def body(p_ref, qg_ref, kg_ref, cos_ref, sgn_ref, bd_ref, sp_ref, q_ref, k_ref, v_ref):
        pv = p_ref[...]
        cos, sgn, bd, sp = cos_ref[...], sgn_ref[...], bd_ref[...], sp_ref[...]
        qr = _rope_norm(pv[:, :QW], qg_ref[...], cos, sgn, bd, HEAD_DIM ** -0.5)
        kr = _rope_norm(pv[:, QW:QW + KW], kg_ref[...], cos[:, :KW], sgn[:, :KW], bd[:KW, :KW], 1.0)
        q_ref[...] = _nn(qr.astype(BF16), sp).astype(BF16)
        k_ref[...] = _nn(kr.astype(BF16), sp[:KW, :KP]).astype(BF16)
        v_ref[...] = _nn(pv[:, QW + KW:].astype(BF16), sp[:KW, :KP]).astype(BF16)

    full = lambda a: pl.BlockSpec(a.shape, lambda i: (0,) * a.ndim)
    tab = pl.BlockSpec((tm, QW), lambda i: (i % nb, 0))
    return pl.pallas_call(
        body, name="qkv_prep_fwd", grid=(T // tm,),
        in_specs=[pl.BlockSpec((tm, QW + 2 * KW), lambda i: (i, 0)), full(qg), full(kg), tab, tab,
                  full(cst["bd"]), full(cst["spread"])],
        out_specs=[pl.BlockSpec((tm, QP), lambda i: (i, 0)), pl.BlockSpec((tm, KP), lambda i: (i, 0)),
                   pl.BlockSpec((tm, KP), lambda i: (i, 0))],
        out_shape=[jax.ShapeDtypeStruct((T, QP), BF16), jax.ShapeDtypeStruct((T, KP), BF16),
                   jax.ShapeDtypeStruct((T, KP), BF16)],
        compiler_params=_cp("parallel"),
    )(proj_b, qg, kg, cst["cos"], cst["sgn"], cst["bd"], cst["spread"])


def qkv_prep_bwd(proj_b, qg, kg, cst, n_ex, dq, dk_pad, dv_pad, d_proj):
    T = proj_b.shape[0]
    S = T // n_ex
    tm = _tile(S, 512)
    nb = S // tm

    def body(p_ref, qg_ref, kg_ref, cos_ref, sgn_ref, bd_ref, ga_ref, dq_ref, dk_ref, dv_ref, _kept,
             dp_ref, dqg_ref, dkg_ref):
        pv = p_ref[...]
        cos, sgn, bd, ga = cos_ref[...], sgn_ref[...], bd_ref[...], ga_ref[...]
        fq = lambda q, g: _rope_norm(q, g, cos, sgn, bd, HEAD_DIM ** -0.5)
        fk = lambda k, g: _rope_norm(k, g, cos[:, :KW], sgn[:, :KW], bd[:KW, :KW], 1.0)
        _, vq = jax.vjp(fq, pv[:, :QW], qg_ref[...])
        _, vk = jax.vjp(fk, pv[:, QW:QW + KW], kg_ref[...])
        dqp, dqg = vq(dq_ref[...])
        dkp, dkg = vk(_split_mm(dk_ref[...], ga[:KP, :KW]))
        dp_ref[:, :QW] = dqp.astype(BF16)
        dp_ref[:, QW:QW + KW] = dkp.astype(BF16)
        dp_ref[:, QW + KW:] = _split_mm(dv_ref[...], ga[:KP, :KW]).astype(BF16)

        @pl.when(pl.program_id(0) == 0)
        def _():
            dqg_ref[...] = jnp.zeros_like(dqg_ref)
            dkg_ref[...] = jnp.zeros_like(dkg_ref)

        dqg_ref[...] += dqg
        dkg_ref[...] += dkg

    full = lambda a: pl.BlockSpec(a.shape, lambda i: (0,) * a.ndim)
    tab = pl.BlockSpec((tm, QW), lambda i: (i % nb, 0))
    row = lambda n: pl.BlockSpec((tm, n), lambda i: (i, 0))
    wb = QW + 2 * KW
    assert d_proj.shape[1] % wb == 0
    last = d_proj.shape[1] // wb - 1
    return pl.pallas_call(
        body, name="qkv_prep_bwd", grid=(T // tm,),
        in_specs=[row(wb), full(qg), full(kg), tab, tab, full(cst["bd"]), full(cst["gather"]),
                  row(QW), row(KP), row(KP), ANY],
        out_specs=[pl.BlockSpec((tm, wb), lambda i: (i, last)), pl.BlockSpec((1, QW), lambda i: (0, 0)),
                   pl.BlockSpec((1, KW), lambda i: (0, 0))],
        out_shape=[jax.ShapeDtypeStruct(d_proj.shape, BF16), jax.ShapeDtypeStruct((1, QW), F32),
                   jax.ShapeDtypeStruct((1, KW), F32)],
        input_output_aliases={10: 0}, compiler_params=_cp("arbitrary"),
    )(proj_b, qg, kg, cst["cos"], cst["sgn"], cst["bd"], cst["gather"], dq, dk_pad, dv_pad, d_proj)


ATT_TQ = 256
ATT_TQ_FWD = 512


def attn_fwd(qp, kp, vp, gather, n_ex, mix, job=None):
    T = qp.shape[0]
    S = T // n_ex
    tq = _tile(S, ATT_TQ_FWD)
    nq = S // tq

    def body(q_ref, k_ref, v_ref, ga_ref, _kept, o_ref, op_ref, lse_ref):
        lane = lax.broadcasted_iota(jnp.int32, (tq, LANES), 1)
        lse_all = jnp.zeros((tq, LANES), F32)
        for h in range(N_Q_HEADS):
            kv = h // Q_PER_KV
            qh = q_ref[:, h * LANES:(h + 1) * LANES]
            s = _nt(qh, k_ref[:, kv * LANES:(kv + 1) * LANES])
            m = jnp.max(s, axis=-1, keepdims=True)
            p = jnp.exp(s - m)
            lsum = jnp.sum(p, axis=-1, keepdims=True)
            o = _nn(p.astype(BF16), v_ref[:, kv * LANES:(kv + 1) * LANES]) / lsum
            op_ref[:, h * LANES:(h + 1) * LANES] = o.astype(BF16)
            lse_all = jnp.where(lane == h, m + jnp.log(lsum), lse_all)
        lse_ref[...] = lse_all
        o_ref[...] = _nn(op_ref[...], ga_ref[...]).astype(BF16)

    blk = lambda n: pl.BlockSpec((tq, n), lambda b, i: (b * nq + i, 0))
    kvs = pl.BlockSpec((S, KP), lambda b, i: (b, 0))
    return _call(
        body, "attn_fwd", (n_ex, nq),
        [blk(QP), kvs, kvs, pl.BlockSpec(gather.shape, lambda b, i: (0, 0)), ANY], [qp, kp, vp, gather, mix],
        [pl.BlockSpec((tq, QW), lambda b, i: (b * nq + i, 1)), blk(QP), blk(LANES)],
        [jax.ShapeDtypeStruct(mix.shape, BF16), jax.ShapeDtypeStruct((T, QP), BF16),
         jax.ShapeDtypeStruct((T, LANES), F32)], (), ("parallel", "parallel"), job, {4: 0})


def attn_bwd(qp, kp, vp, op, lse, do, cst, n_ex, job=None):
    T = qp.shape[0]
    S = T // n_ex
    tq = _tile(S, ATT_TQ)
    nq = S // tq

    def body(q_ref, k_ref, v_ref, op_ref, lse_ref, do_ref, sp_ref, ga_ref, dq_ref, dk_ref, dv_ref, dqp_s):
        @pl.when(pl.program_id(1) == 0)
        def _():
            dk_ref[...] = jnp.zeros_like(dk_ref)
            dv_ref[...] = jnp.zeros_like(dv_ref)

        lane = lax.broadcasted_iota(jnp.int32, (tq, LANES), 1)
        dop = _nn(do_ref[...], sp_ref[...]).astype(BF16)
        lse_all = lse_ref[...]
        for h in range(N_Q_HEADS):
            kv = h // Q_PER_KV
            hs = slice(h * LANES, (h + 1) * LANES)
            ks = slice(kv * LANES, (kv + 1) * LANES)
            qh, kk, vv = q_ref[:, hs], k_ref[:, ks], v_ref[:, ks]
            doh = dop[:, hs]
            lse_h = jnp.sum(jnp.where(lane == h, lse_all, 0.0), axis=-1, keepdims=True)
            p = jnp.exp(_nt(qh, kk) - lse_h)
            dp = _nt(doh, vv)
            delta = jnp.sum(doh.astype(F32) * op_ref[:, hs].astype(F32), axis=-1, keepdims=True)
            ds = (p * (dp - delta)).astype(BF16)
            dqp_s[:, hs] = _nn(ds, kk)
            dk_ref[:, ks] += _tn(ds, qh)
            dv_ref[:, ks] += _tn(p.astype(BF16), doh)
        dq_ref[...] = _split_mm(dqp_s[...], ga_ref[...])

    blk = lambda n: pl.BlockSpec((tq, n), lambda b, i: (b * nq + i, 0))
    kvs = pl.BlockSpec((S, KP), lambda b, i: (b, 0))
    full = lambda a: pl.BlockSpec(a.shape, lambda b, i: (0, 0))
    return _call(
        body, "attn_bwd", (n_ex, nq),
        [blk(QP), kvs, kvs, blk(QP), blk(LANES), pl.BlockSpec((tq, QW), lambda b, i: (b * nq + i, 1)),
         full(cst["spread"]), full(cst["gather"])],
        [qp, kp, vp, op, lse, do, cst["spread"], cst["gather"]],
        [blk(QW), kvs, kvs],
        [jax.ShapeDtypeStruct((T, QW), F32), jax.ShapeDtypeStruct((T, KP), F32), jax.ShapeDtypeStruct((T, KP), F32)],
        [pltpu.VMEM((tq, QP), F32)], ("arbitrary", "arbitrary"), job)


def pool_fwd(p, pool_w, scale, n_ex):
    T, W = p.shape
    S = T // n_ex

    def body(p_ref, w_ref, s_ref, o_ref):
        o_ref[...] = _pool_mix(p_ref[...], w_ref[...], s_ref[...]).astype(BF16)

    return pl.pallas_call(
        body, name="pool_fwd", grid=(n_ex,),
        in_specs=[pl.BlockSpec((S, W), lambda b: (b, 0)),
                  pl.BlockSpec(pool_w.shape, lambda b: (0, 0, 0)),
                  pl.BlockSpec((1, W), lambda b: (0, 0))],
        out_specs=pl.BlockSpec((S, W), lambda b: (b, 0)),
        out_shape=jax.ShapeDtypeStruct((T, 2 * W), BF16), compiler_params=_cp("parallel"),
    )(p, pool_w, scale)


def pool_bwd(p, pool_w, scale, n_ex, dy, d_proj):
    T, W = p.shape
    S = T // n_ex
    last = d_proj.shape[1] // W - 1

    def body(p_ref, w_ref, s_ref, dy_ref, _kept, dp_ref, dw_ref, ds_ref):
        _, vjp = jax.vjp(_pool_mix, p_ref[...], w_ref[...], s_ref[...])
        dp, dw, ds = vjp(dy_ref[...].astype(F32))
        dp_ref[...] = dp.astype(BF16)

        @pl.when(pl.program_id(0) == 0)
        def _():
            dw_ref[...] = jnp.zeros_like(dw_ref)
            ds_ref[...] = jnp.zeros_like(ds_ref)

        dw_ref[...] += dw
        ds_ref[...] += ds

    wshape = pool_w.shape
    return pl.pallas_call(
        body, name="pool_bwd", grid=(n_ex,),
        in_specs=[pl.BlockSpec((S, W), lambda b: (b, 0)),
                  pl.BlockSpec(wshape, lambda b: (0, 0, 0)),
                  pl.BlockSpec((1, W), lambda b: (0, 0)), pl.BlockSpec((S, W), lambda b: (b, 0)), ANY],
        out_specs=[pl.BlockSpec((S, W), lambda b: (b, last)), pl.BlockSpec(wshape, lambda b: (0, 0, 0)),
                   pl.BlockSpec((1, W), lambda b: (0, 0))],
        out_shape=[jax.ShapeDtypeStruct(d_proj.shape, BF16), jax.ShapeDtypeStruct(wshape, F32),
                   jax.ShapeDtypeStruct((1, W), F32)],
        input_output_aliases={4: 0}, compiler_params=_cp("arbitrary"),
    )(p, pool_w, scale, dy, d_proj)


SGU_TS = 512


def sgu_fwd(u, v, norm_g, w_s, b_full, mix):
    T, W = u.shape
    ts = _tile(T, SGU_TS)

    def body(u_ref, v_ref, g_ref, w_ref, b_ref, _kept, o_ref):
        o_ref[...] = _sgu(u_ref[...], v_ref[...], g_ref[...], w_ref[...], b_ref[...]).astype(BF16)

    row = pl.BlockSpec((ts, W), lambda i: (i, 0))
    wsp = pl.BlockSpec(w_s.shape, lambda i: (0, 0, 0))
    return pl.pallas_call(
        body, name="sgu_fwd", grid=(T // ts,),
        in_specs=[row, row, pl.BlockSpec((1, W), lambda i: (0, 0)), wsp, wsp, ANY],
        out_specs=pl.BlockSpec((ts, W), lambda i: (i, 1)), out_shape=jax.ShapeDtypeStruct(mix.shape, BF16),
        input_output_aliases={5: 0}, compiler_params=_cp("parallel"),
    )(u, v, norm_g, w_s, b_full, mix)


def sgu_bwd(u, v, norm_g, w_s, b_full, dy):
    T, W = u.shape
    ts = _tile(T, SGU_TS)
    wshape = w_s.shape

    def body(u_ref, v_ref, g_ref, w_ref, b_ref, dy_ref, duv_ref, dg_ref, dw_ref, db_ref):
        _, vjp = jax.vjp(_sgu, u_ref[...], v_ref[...], g_ref[...], w_ref[...], b_ref[...])
        du, dv, dg, dw, db = vjp(dy_ref[...].astype(F32))
        duv_ref[:, :W] = du.astype(BF16)
        duv_ref[:, W:] = dv.astype(BF16)

        @pl.when(pl.program_id(0) == 0)
        def _():
            dg_ref[...] = jnp.zeros_like(dg_ref)
            dw_ref[...] = jnp.zeros_like(dw_ref)
            db_ref[...] = jnp.zeros_like(db_ref)

        dg_ref[...] += dg
        dw_ref[...] += dw
        db_ref[...] += db

    row = pl.BlockSpec((ts, W), lambda i: (i, 0))
    wsp = pl.BlockSpec(wshape, lambda i: (0, 0, 0))
    wout = pl.BlockSpec(wshape, lambda i: (0, 0, 0))
    vec = pl.BlockSpec((1, W), lambda i: (0, 0))
    return pl.pallas_call(
        body, name="sgu_bwd", grid=(T // ts,),
        in_specs=[row, row, vec, wsp, wsp, pl.BlockSpec((ts, W), lambda i: (i, 1))],
        out_specs=[pl.BlockSpec((ts, 2 * W), lambda i: (i, 0)), vec, wout, wout],
        out_shape=[jax.ShapeDtypeStruct((T, 3 * W), BF16),
                   jax.ShapeDtypeStruct((1, W), F32), jax.ShapeDtypeStruct(wshape, F32),
                   jax.ShapeDtypeStruct(wshape, F32)],
        compiler_params=_cp("arbitrary"),
    )(u, v, norm_g, w_s, b_full, dy)


def loss_head(x, gain, target):
    T, D = x.shape
    tm = _tile(T, 512)

    def body(x_ref, g_ref, t_ref, loss_ref, dx_ref, dg_ref):
        xv, g = x_ref[...], g_ref[...]
        r = lax.rsqrt(jnp.mean(xv * xv, axis=-1, keepdims=True) + EPS)
        err = xv * r * g - t_ref[...]
        part = 0.5 * jnp.sum(jnp.mean(err * err, axis=-1, keepdims=True), axis=0, keepdims=True)
        dx, dg = _rms_bwd_math(xv, g, err * (1.0 / D), jnp.zeros_like(xv))
        dx_ref[...] = dx

        @pl.when(pl.program_id(0) == 0)
        def _():
            loss_ref[...] = jnp.zeros_like(loss_ref)
            dg_ref[...] = jnp.zeros_like(dg_ref)

        loss_ref[...] += part
        dg_ref[...] += dg

    row = pl.BlockSpec((tm, D), lambda i: (i, 0))
    vec = pl.BlockSpec((1, D), lambda i: (0, 0))
    return pl.pallas_call(
        body, name="loss_head", grid=(T // tm,),
        in_specs=[row, vec, row], out_specs=[pl.BlockSpec((1, 1), lambda i: (0, 0)), row, vec],
        out_shape=[jax.ShapeDtypeStruct((1, 1), F32), jax.ShapeDtypeStruct((T, D), F32),
                   jax.ShapeDtypeStruct((1, D), F32)],
        compiler_params=_cp("arbitrary"),
    )(x, gain, target)


class MultiJob:
    def __init__(self, jobs):
        self.jobs = jobs
        self.args = [a for j in jobs for a in j.args]
        self.out_shape = [s for j in jobs for s in j.out_shape]
        self.scratch = [s for j in jobs for s in j.scratch]
        self.n_in, self.n_out = len(self.args), len(self.out_shape)

    def _each(self, ins, outs, sems):
        i = o = s = 0
        for j in self.jobs:
            yield j, ins[i:i + j.n_in], outs[o:o + j.n_out], sems[s:s + len(j.scratch)]
            i, o, s = i + j.n_in, o + j.n_out, s + len(j.scratch)

    def start(self, ins, outs, sems):
        for j, a, b, c in self._each(ins, outs, sems):
            j.start(a, b, c)

    def middle(self, ins, outs, sems):
        for j, a, b, c in self._each(ins, outs, sems):
            j.middle(a, b, c)

    def late(self, ins, outs, sems):
        for j, a, b, c in self._each(ins, outs, sems):
            j.late(a, b, c)

    def finish(self, ins, outs, sems):
        for j, a, b, c in self._each(ins, outs, sems):
            j.finish(a, b, c)

    def split(self, results):
        o = 0
        for j in self.jobs:
            yield results[o:o + j.n_out]
            o += j.n_out


class Plan:
    def __init__(self, shard, gathers, small_carrier=None, pack_small=None, exchange=True):
        self.shard, self.gathers, self.exchange = shard, gathers, exchange
        self.small_carrier, self.pack_small = small_carrier, pack_small
        self.weights, self.grads, self.started = {}, {}, []
        self.small_src = self.small_parts = None

    def scatter(self, keys, carry=None):
        if not self.exchange:
            return carry
        tag = "_".join(f"{kind}{l}" for kind, l in keys)
        send, recv, grads, zones, carry, token = scatter_start([self.grads[k] for k in keys], tag, carry)
        self.started.append((keys, send, recv, grads, zones, token, tag))
        return carry

    def collect(self, groups, after):
        got = {}
        for keys, send, recv, grads, zones, _, tag in groups:
            grads, zones = scatter_wait(send, recv, grads, zones, after, tag)
            got.update({k: (g, z) for k, g, z in zip(keys, grads, zones)})
        return got

    def weight(self, kind, l):
        return self.weights[(kind, l)]

    def grad(self, kind, l, g):
        self.grads[(kind, l)] = g

    def small_ready(self, small, d_final):
        if self.pack_small is not None:
            self.small_src = self.pack_small(small, d_final)

    def _jobs(self, key):
        jobs = []
        if key in self.gathers:
            ks = self.gathers[key]
            jobs.append((GatherJob([self.shard(*k) for k in ks]), self.weights, ks))
        if key == self.small_carrier and self.small_src is not None:
            jobs.append((GatherJob([(self.small_src, None)]), None, None))
        return jobs

    def _deliver(self, jobs, results):
        multi = MultiJob([j for j, _, _ in jobs])
        for (_, store, ks), res in zip(jobs, multi.split(results)):
            if store is None:
                self.small_parts = res[0]
            else:
                store.update(zip(ks, res))

    def run(self, key, fn, *args, **kw):
        jobs = self._jobs(key)
        if not jobs:
            out = fn(*args, **kw)
            return out if fn is mm_tn else out[0]
        res, jres = fn(*args, job=MultiJob([j for j, _, _ in jobs]), **kw)
        self._deliver(jobs, jres)
        return res

    def alone(self, key, name):
        jobs = self._jobs(key)
        if jobs:
            self._deliver(jobs, run_job(MultiJob([j for j, _, _ in jobs]), name))


def _local_step(x, target, layers, final_norm, n_ex, plan):
    T, D = x.shape
    L = len(layers)
    cst = _attn_consts(T // n_ex)
    ident = lambda j: j
    EV_A, EV_B = 3 * (D // 2), QW + 2 * KW
    OD_W = D // 2
    wt = plan.weight

    saved = []
    for l, W in enumerate(layers):
        s = dict(x0=x)
        x1, *s["gu1"] = plan.run(("ffn1_fwd", l), ffn_fwd, x, W["n1"], wt("f1_in_t", l), wt("f1_out", l))
        if l % 2 == 0:
            pa, pb, h = mm_nt(x1, wt("mx_in_t", l), [(0, EV_A), (EV_A, EV_B)], F32, True, W["nm"])
            qg = jnp.tile(W["q_norm"], N_Q_HEADS)[None]
            kg = jnp.tile(W["k_norm"], N_KV_HEADS)[None]
            mix = conv_fwd(pa, W["conv_w"], n_ex)
            qp, kp, vp = qkv_prep_fwd(pb, qg, kg, cst, n_ex)
            mix, op, lse = plan.run(("attn_fwd", l), attn_fwd, qp, kp, vp, cst["gather"], n_ex, mix)
            s.update(pa=pa, pb=pb, qg=qg, kg=kg, qp=qp, kp=kp, vp=vp, op=op, lse=lse)
        else:
            p, u, v, h = mm_nt(x1, wt("mx_in_t", l), [(0, OD_W), (OD_W, OD_W), (2 * OD_W, OD_W)], F32, True, W["nm"])
            scale = W["pool_scale"][None]
            sn = W["sgu_norm"][None]
            b_full = jnp.broadcast_to(W["sgu_b"][..., None], W["sgu_w"].shape)
            mix = sgu_fwd(u, v, sn, W["sgu_w"], b_full, pool_fwd(p, W["pool_w"], scale, n_ex))
            s.update(p=p, u=u, v=v, scale=scale, sn=sn, b_full=b_full)
        x2 = mm_nn([(mix, 0, D, 0)], wt("mx_out", l), residual=x1)
        x3, *s["gu2"] = plan.run(("ffn2_fwd", l), ffn_fwd, x2, W["n2"], wt("f2_in_t", l), wt("f2_out", l))
        s.update(x1=x1, x2=x2, h=h, mix=mix)
        saved.append(s)
        x = x3

    loss, dx, d_final = loss_head(x, final_norm, target)

    small = [None] * L

    def ffn_back(which, l, dout, xin, gain, gu_xn, sm, sm_key):
        w_in, w_out = wt(which + "_in_t", l), wt(which + "_out", l)
        F = w_out.shape[0]
        nc = F // FFN_TN
        gu, xn = gu_xn
        dxi, sm[sm_key], a, dgu, dob = plan.run((which + "_bwd", l), ffn_bwd_x, dout, xin, gain, gu, w_in, w_out)
        if which == "f1" and l == 0:
            plan.small_ready(small, d_final)
        plan.grad(which + "_in_t", l, plan.run(
            (which + "_in_grad", l), mm_tn, dgu, xn, 2 * F, lambda k, c: k * nc + c, grid=(2, nc),
            col_block=lambda k, c: 2 * c + k))
        plan.grad(which + "_out", l, plan.run((which + "_out_grad", l), mm_tn, a, dob, F, ident))
        keys = [(which + "_in_t", l), (which + "_out", l)]
        if which == "f1" and l == 0:
            plan.scatter(keys)
            return dxi
        if which == "f1":
            keys += [("mx_out", l), ("mx_in_t", l)]
        return plan.scatter(keys, dxi)

    for l in reversed(range(L)):
        s, W = saved[l], layers[l]
        sm = small[l] = {}
        dx = ffn_back("f2", l, dx, s["x2"], W["n2"], s["gu2"], sm, "n2")
        dmix, dxb = mm_nt(dx, wt("mx_out", l), [(0, D)], BF16, emit_a_bf16=True)
        plan.grad("mx_out", l, mm_tn(s["mix"], dxb, D, ident))
        if l % 2 == 0:
            d_proj, sm["conv_w"] = conv_bwd(s["pa"], W["conv_w"], n_ex, dmix, EV_A + EV_B)
            dq, dkp, dvp = plan.run(("attn_bwd", l), attn_bwd, s["qp"], s["kp"], s["vp"], s["op"], s["lse"], dmix, cst, n_ex)
            d_proj, dqg, dkg = qkv_prep_bwd(s["pb"], s["qg"], s["kg"], cst, n_ex, dq, dkp, dvp, d_proj)
            d_pieces = [(d_proj, 0, EV_A + EV_B, 0)]
            plan.grad("mx_in_t", l, mm_tn(d_proj, s["h"], EV_A + EV_B, ident))
            sm["q_norm"] = dqg.reshape(N_Q_HEADS, HEAD_DIM).sum(0)
            sm["k_norm"] = dkg.reshape(N_KV_HEADS, HEAD_DIM).sum(0)
        else:
            d_proj, d_sn, sm["sgu_w"], d_sb = sgu_bwd(s["u"], s["v"], s["sn"], W["sgu_w"], s["b_full"], dmix)
            d_proj, sm["pool_w"], d_ps = pool_bwd(s["p"], W["pool_w"], s["scale"], n_ex, dmix, d_proj)
            d_pieces = [(d_proj, 0, OD_W, 1), (d_proj, 1, OD_W, 2), (d_proj, 2, OD_W, 0)]
            nb = OD_W // MM_TC
            plan.grad("mx_in_t", l, mm_tn(d_proj, s["h"], 3 * OD_W,
                                          lambda jj: jnp.where(jj < 2 * nb, jj + nb, jj - 2 * nb)))
            sm["pool_scale"], sm["sgu_norm"], sm["sgu_b"] = d_ps[0], d_sn[0], d_sb.sum(-1)
        if l == 0:
            dx = plan.scatter([("mx_out", l), ("mx_in_t", l)], dx)
        dx, sm["nm"] = mm_nn(d_pieces, wt("mx_in_t", l), norm_bwd=(s["x1"], W["nm"], dx))
        dx = ffn_back("f1", l, dx, s["x0"], W["n1"], s["gu1"], sm, "n1")
    return loss, dx


def all_gather(srcs):
    return run_job(GatherJob(srcs), "all_gather")


HBM_SPEC = pl.BlockSpec(memory_space=pltpu.HBM)
SEM_SPEC = pl.BlockSpec(memory_space=pltpu.SEMAPHORE)
SPLIT_COPY = pltpu.CompilerParams(has_side_effects=pltpu.SideEffectType.DATAFLOW_SIDE_EFFECTING)


def _scatter_copies(srcs, lands, send, recv, dims):
    x, y, c, me = _my_place()
    for t, (r, _) in enumerate(dims):
        for k in range(1, N_DEV):
            peer, pidx = _peer(x, y, c, k)
            rows = srcs[t].at[pl.ds(pl.multiple_of(pidx * r, 8), r), :]
            yield (_remote(rows, lands[t].at[me], send, recv, N_DEV * t + k, peer),
                   _remote(rows, lands[t].at[pidx], send, recv, N_DEV * t + k, peer))


def scatter_start(grads, tag, carry=None):
    n = len(grads)
    dims = [(g.shape[0] // N_DEV, g.shape[1]) for g in grads]
    passed = list(grads) + [lax.empty((N_DEV, r, cc), g.dtype) for g, (r, cc) in zip(grads, dims)]
    passed += [] if carry is None else [carry]
    m = len(passed)

    def body(*refs):
        srcs, lands, send, recv, token = refs[:n], refs[n:2 * n], refs[m], refs[m + 1], refs[-1]
        for mine, _ in _scatter_copies(srcs, lands, send, recv, dims):
            mine.start()
        token[...] = jnp.zeros_like(token)

    res = pl.pallas_call(
        body, name="scatter_start_" + tag,
        out_shape=(pltpu.SemaphoreType.DMA((N_DEV * n,)), pltpu.SemaphoreType.DMA((N_DEV * n,)),
                   *[pltpu.HBM(a.shape, a.dtype) for a in passed], jax.ShapeDtypeStruct((8, LANES), F32)),
        in_specs=[HBM_SPEC] * m,
        out_specs=(SEM_SPEC, SEM_SPEC, *([HBM_SPEC] * m), pl.BlockSpec(memory_space=pltpu.VMEM)),
        input_output_aliases={i: 2 + i for i in range(m)}, compiler_params=SPLIT_COPY,
    )(*[pltpu.with_memory_space_constraint(a, pltpu.HBM) for a in passed])
    return res[0], res[1], res[2:2 + n], res[2 + n:2 + 2 * n], (None if carry is None else res[2 + 2 * n]), res[-1]


def scatter_wait(send, recv, grads, zones, after, tag):
    n = len(grads)
    dims = [(g.shape[0] // N_DEV, g.shape[1]) for g in grads]

    def body(*refs):
        srcs, lands, send_ref, recv_ref = refs[:n], refs[n:2 * n], refs[2 * n], refs[2 * n + 1]
        for mine, theirs in _scatter_copies(srcs, lands, send_ref, recv_ref, dims):
            mine.wait_send()
            theirs.wait_recv()

    res = pl.pallas_call(
        body, name="scatter_wait_" + tag,
        out_shape=(*[pltpu.HBM(g.shape, g.dtype) for g in grads], *[pltpu.HBM(z.shape, z.dtype) for z in zones]),
        in_specs=[HBM_SPEC] * (2 * n) + [SEM_SPEC, SEM_SPEC, ANY], out_specs=[HBM_SPEC] * (2 * n),
        input_output_aliases={i: i for i in range(2 * n)}, compiler_params=SPLIT_COPY,
    )(*grads, *zones, send, recv, after)
    return res[:n], res[n:]


def cast_shards(w):
    L, A, B = w.shape

    def body(w_ref, o_ref):
        o_ref[...] = w_ref[...].astype(BF16)

    return pl.pallas_call(
        body, name="cast_shards", grid=(L,),
        in_specs=[pl.BlockSpec((None, A, B), lambda l: (l, 0, 0))],
        out_specs=pl.BlockSpec((None, A, B), lambda l: (l, 0, 0)),
        out_shape=jax.ShapeDtypeStruct((L, A, B), BF16), compiler_params=_cp("parallel"),
    )(w)


ADAM_TC = 256


def adamw(parts, w, m, v, l, prev=None, own=None, after=None):
    P, R, C = parts.shape
    tc = _tile(C, ADAM_TC)
    c1, c2 = 1.0 - ADAM_B1 ** ADAM_STEP, 1.0 - ADAM_B2 ** ADAM_STEP
    prev = list(prev) if prev is not None else []

    def body(*refs):
        me_ref, refs = (refs[0], refs[1:]) if own is not None else (None, refs)
        p_ref, w_ref, m_ref, v_ref = refs[:4]
        g_ref, d_ref, mo_ref, vo_ref = refs[-4:]
        if own is None:
            term = lambda s: p_ref[s].astype(F32)
        else:
            term = lambda s: jnp.where(me_ref[0] == s, refs[4][...], p_ref[s]).astype(F32)
        g = term(0)
        for s in range(1, P):
            g = g + term(s)
        m1 = ADAM_B1 * m_ref[...] + (1.0 - ADAM_B1) * g
        v1 = ADAM_B2 * v_ref[...] + (1.0 - ADAM_B2) * (g * g)
        g_ref[...] = g
        mo_ref[...] = m1
        vo_ref[...] = v1
        d_ref[...] = -ADAM_LR * ((m1 / c1) / (jnp.sqrt(v1 / c2) + ADAM_EPS) + ADAM_WD * w_ref[...])

    wspec = pl.BlockSpec((None, R, tc), lambda i, *_: (l, 0, i))
    pspec = pl.BlockSpec((P, R, tc), lambda i, *_: (0, 0, i))
    extra = prev + ([] if after is None else [after])
    out_shape = [jax.ShapeDtypeStruct(w.shape, F32)] * 4
    if own is None:
        return pl.pallas_call(
            body, name="adamw", grid=(C // tc,), in_specs=[pspec, wspec, wspec, wspec] + [ANY] * len(extra),
            out_specs=[wspec] * 4, out_shape=out_shape,
            input_output_aliases={4 + i: i for i in range(len(prev))}, compiler_params=_cp("parallel"),
        )(parts, w, m, v, *extra)
    own_sums, me = own
    ospec = pl.BlockSpec((None, R, tc), lambda i, me_ref: (me_ref[0], 0, i))
    return pl.pallas_call(
        body, name="adamw_own", out_shape=out_shape,
        grid_spec=pltpu.PrefetchScalarGridSpec(
            num_scalar_prefetch=1, grid=(C // tc,),
            in_specs=[pspec, wspec, wspec, wspec, ospec] + [ANY] * len(extra), out_specs=[wspec] * 4),
        input_output_aliases={6 + i: i for i in range(len(prev))}, compiler_params=_cp("parallel"),
    )(me, parts, w, m, v, own_sums, *extra)


_WEIGHTS = ['ffn1_norm', 'ffn1_w_in', 'ffn1_w_out', 'mix_norm', 'ffn2_norm', 'ffn2_w_in', 'ffn2_w_out', 'ev_w_in',
            'ev_conv_w', 'ev_q_norm', 'ev_k_norm', 'ev_w_out', 'od_w_in', 'od_pool_w', 'od_pool_scale', 'od_sgu_norm',
            'od_sgu_w', 'od_sgu_b', 'od_w_out', 'final_norm']
_BIG = dict(ffn1_w_in=True, ffn1_w_out=False, ffn2_w_in=True, ffn2_w_out=False,
            ev_w_in=True, ev_w_out=False, od_w_in=True, od_w_out=False)
_SMALL_SHARDED = ['ev_conv_w', 'od_pool_scale', 'od_sgu_norm']
_SMALL = [n for n in _WEIGHTS if n not in _BIG]
_PACK_ROWS = 8 * LANES


_KINDS = ("f1_in_t", "f1_out", "mx_in_t", "mx_out", "f2_in_t", "f2_out")
_CARRIER_US = dict(ffn1_fwd=105, ffn2_fwd=105, attn_fwd=105)
_GATHER_US_PER_ROW = 0.09
SMALL_CARRIER = ("f1_in_grad", 0)


def _schedule(L, rows):
    events = []
    for l in range(L):
        events += [("ffn1_fwd", l), ("mixer", l)] + ([("attn_fwd", l)] if l % 2 == 0 else []) + [("ffn2_fwd", l)]
    consumer = {"f1": "ffn1_fwd", "mx": "mixer", "f2": "ffn2_fwd"}
    queue = [(k, l) for l in range(L) for k in _KINDS]
    pos = {t: events.index((consumer[t[0][:2]], t[1])) for t in queue}
    gathers = {"first": [t for t in queue if pos[t] == 0]}
    queue = [t for t in queue if pos[t] > 0]
    carriers = [i for i, e in enumerate(events) if e[0] in _CARRIER_US]
    for i in carriers:
        budget, take = _CARRIER_US[events[i][0]], []
        later = [j for j in carriers if j > i]
        while queue:
            t = queue[0]
            cost = rows(*t) * _GATHER_US_PER_ROW
            forced = not any(j < pos[t] for j in later)
            if not forced and cost > budget:
                break
            take.append(queue.pop(0))
            budget -= cost
        if take:
            gathers[events[i]] = take
    assert not queue
    return gathers


def _pack(arrs):
    flat = jnp.concatenate([a.reshape(-1) for a in arrs])
    pad = (-flat.shape[0]) % _PACK_ROWS
    return jnp.pad(flat, (0, pad)).reshape(-1, LANES)


def _unpack(buf, shapes):
    flat, out, off = buf.reshape(-1), [], 0
    for s in shapes:
        n = math.prod(s)
        out.append(flat[off:off + n].reshape(s))
        off += n
    return out


def _unshard_last(g, lead):
    nd = len(lead)
    return jnp.moveaxis(g, 0, nd).reshape(*lead, -1)


def kernel(x, ffn1_norm, ffn1_w_in, ffn1_w_out, mix_norm, ffn2_norm, ffn2_w_in, ffn2_w_out, ev_w_in, ev_conv_w, ev_q_norm, ev_k_norm, ev_w_out, od_w_in, od_pool_w, od_pool_scale, od_sgu_norm, od_sgu_w, od_sgu_b, od_w_out, final_norm, loss_target, m_ffn1_norm, m_ffn1_w_in, m_ffn1_w_out, m_mix_norm, m_ffn2_norm, m_ffn2_w_in, m_ffn2_w_out, m_ev_w_in, m_ev_conv_w, m_ev_q_norm, m_ev_k_norm, m_ev_w_out, m_od_w_in, m_od_pool_w, m_od_pool_scale, m_od_sgu_norm, m_od_sgu_w, m_od_sgu_b, m_od_w_out, m_final_norm, v_ffn1_norm, v_ffn1_w_in, v_ffn1_w_out, v_mix_norm, v_ffn2_norm, v_ffn2_w_in, v_ffn2_w_out, v_ev_w_in, v_ev_conv_w, v_ev_q_norm, v_ev_k_norm, v_ev_w_out, v_od_w_in, v_od_pool_w, v_od_pool_scale, v_od_sgu_norm, v_od_sgu_w, v_od_sgu_b, v_od_w_out, v_final_norm):
    w = dict(zip(_WEIGHTS, (ffn1_norm, ffn1_w_in, ffn1_w_out, mix_norm, ffn2_norm, ffn2_w_in, ffn2_w_out, ev_w_in, ev_conv_w, ev_q_norm, ev_k_norm, ev_w_out, od_w_in, od_pool_w, od_pool_scale, od_sgu_norm, od_sgu_w, od_sgu_b, od_w_out, final_norm)))
    m = dict(zip(_WEIGHTS, (m_ffn1_norm, m_ffn1_w_in, m_ffn1_w_out, m_mix_norm, m_ffn2_norm, m_ffn2_w_in, m_ffn2_w_out, m_ev_w_in, m_ev_conv_w, m_ev_q_norm, m_ev_k_norm, m_ev_w_out, m_od_w_in, m_od_pool_w, m_od_pool_scale, m_od_sgu_norm, m_od_sgu_w, m_od_sgu_b, m_od_w_out, m_final_norm)))
    v = dict(zip(_WEIGHTS, (v_ffn1_norm, v_ffn1_w_in, v_ffn1_w_out, v_mix_norm, v_ffn2_norm, v_ffn2_w_in, v_ffn2_w_out, v_ev_w_in, v_ev_conv_w, v_ev_q_norm, v_ev_k_norm, v_ev_w_out, v_od_w_in, v_od_pool_w, v_od_pool_scale, v_od_sgu_norm, v_od_sgu_w, v_od_sgu_b, v_od_w_out, v_final_norm)))
    n_ex, seq, D = x.shape
    T = n_ex * seq
    L = ffn1_norm.shape[0]
    me = 4 * lax.axis_index("x") + 2 * lax.axis_index("y") + lax.axis_index("c")

    sh_small = [w[n] for n in _SMALL_SHARDED]
    packed = all_gather([(_pack(sh_small), None)])[0].reshape(N_DEV, -1)
    full_small = {}
    off = 0
    for n, a in zip(_SMALL_SHARDED, sh_small):
        cnt = math.prod(a.shape)
        full_small[n] = _unshard_last(packed[:, off:off + cnt].reshape((N_DEV,) + a.shape), a.shape[:-1])
        off += cnt

    tr = lambda a: jnp.swapaxes(a, 1, 2)
    wmv = {n: tuple(tr(d[n]) if t else d[n] for d in (w, m, v)) for n, t in _BIG.items()}
    shards = {n: cast_shards(wmv[n][0]) for n in _BIG}

    def name_of(kind, l):
        mx = "ev" if l % 2 == 0 else "od"
        return {"f1_in_t": "ffn1_w_in", "f1_out": "ffn1_w_out", "f2_in_t": "ffn2_w_in", "f2_out": "ffn2_w_out",
                "mx_in_t": mx + "_w_in", "mx_out": mx + "_w_out"}[kind], (l // 2 if kind.startswith("mx") else l)

    def shard(kind, l):
        name, idx = name_of(kind, l)
        return shards[name], idx

    g_shapes = {}

    def pack_small(small, d_final):
        ev = [sm for l, sm in enumerate(small) if l % 2 == 0]
        od = [sm for l, sm in enumerate(small) if l % 2 == 1]
        st = lambda sms, k: jnp.stack([sm[k] for sm in sms])
        g_full = dict(ffn1_norm=st(small, "n1")[:, 0], mix_norm=st(small, "nm")[:, 0], ffn2_norm=st(small, "n2")[:, 0],
                      ev_conv_w=st(ev, "conv_w"), ev_q_norm=st(ev, "q_norm"), ev_k_norm=st(ev, "k_norm"),
                      od_pool_w=st(od, "pool_w"), od_pool_scale=st(od, "pool_scale"), od_sgu_norm=st(od, "sgu_norm"),
                      od_sgu_w=st(od, "sgu_w"), od_sgu_b=st(od, "sgu_b"), final_norm=d_final[0])
        g_shapes.update({n: g_full[n].shape for n in _SMALL})
        return _pack([g_full[n] for n in _SMALL])

    gathers = _schedule(L, lambda kind, l: shards[name_of(kind, l)[0]].shape[1])
    plan = Plan(shard, gathers, SMALL_CARRIER, pack_small)
    layers = []
    for l in range(L):
        j = l // 2
        W = dict(n1=ffn1_norm[l][None], nm=mix_norm[l][None], n2=ffn2_norm[l][None])
        if l % 2 == 0:
            W.update(conv_w=full_small["ev_conv_w"][j], q_norm=ev_q_norm[j], k_norm=ev_k_norm[j])
        else:
            W.update(pool_w=od_pool_w[j], pool_scale=full_small["od_pool_scale"][j], sgu_norm=full_small["od_sgu_norm"][j],
                     sgu_w=od_sgu_w[j], sgu_b=od_sgu_b[j])
        layers.append(W)

    plan.alone("first", "gather_first")
    loss, dx = _local_step(x.reshape(T, D), loss_target.reshape(T, D), layers, final_norm[None], n_ex, plan)

    out = {n: None for n in _BIG}
    me1 = me.astype(jnp.int32).reshape(1)

    def update(arrived):
        for (kind, l), (own, parts) in arrived.items():
            name, idx = name_of(kind, l)
            out[name] = adamw(parts, *wmv[name], idx, prev=out[name], own=(own.reshape(parts.shape), me1))

    *earlier, last = plan.started
    update(plan.collect(earlier, last[5]))

    g8 = plan.small_parts.reshape(N_DEV, -1)
    cols, off = [], 0
    for n in _SMALL:
        cnt = math.prod(g_shapes[n])
        g = g8[:, off:off + cnt].reshape((N_DEV,) + g_shapes[n])
        off += cnt
        if n in _SMALL_SHARDED:
            width = w[n].shape[-1]
            g = lax.dynamic_slice_in_dim(g, me * width, width, axis=g.ndim - 1)
        cols.append(g.reshape(N_DEV, -1))
    g8 = jnp.concatenate(cols, axis=1)
    g8 = jnp.pad(g8, ((0, 0), (0, (-g8.shape[1]) % _PACK_ROWS))).reshape(N_DEV, -1, LANES)
    pk = lambda d: _pack([d[n] for n in _SMALL])[None]
    small_out = adamw(g8, pk(w), pk(m), pk(v), 0)

    update(plan.collect([last], small_out[0]))
    out = {n: [tr(a) if _BIG[n] else a for a in res] for n, res in out.items()}

    shapes = [w[n].shape for n in _SMALL]
    for i in range(4):
        for n, a in zip(_SMALL, _unpack(small_out[i], shapes)):
            out.setdefault(n, [None] * 4)[i] = a

    total = lax.psum(loss[0, 0], ("x", "y", "c"))
    return (total, dx.reshape(n_ex, seq, D), *[out[n][0] for n in _WEIGHTS], *[out[n][1] for n in _WEIGHTS],
            *[out[n][2] for n in _WEIGHTS], *[out[n][3] for n in _WEIGHTS])
```

```python
import functools
import math

import jax
import jax.numpy as jnp
from jax import lax
from jax.experimental import pallas as pl
from jax.experimental.pallas import tpu as pltpu

F32, BF16 = jnp.float32, jnp.bfloat16
EPS = 1e-6
N_DEV = 8
V7X_VMEM_BYTES = 64 * 1024 * 1024
VMEM_LIMIT = V7X_VMEM_BYTES - 8 * 1024 * 1024
LANES = 128
HEAD_DIM = 64
N_Q_HEADS = 8
N_KV_HEADS = 2
Q_PER_KV = N_Q_HEADS // N_KV_HEADS
GRID_W = 64
ROPE_THETA = 10000.0
POOL_RADII = (1, 2, 4, 8)
SGU_CHUNK = 128
GROUP = 128
ADAM_LR, ADAM_B1, ADAM_B2, ADAM_EPS, ADAM_WD, ADAM_STEP = 0.001, 0.9, 0.999, 1e-08, 0.01, 10
MESH_ID = pl.DeviceIdType.MESH


def _cp(*sem):
    return pltpu.CompilerParams(dimension_semantics=sem, vmem_limit_bytes=VMEM_LIMIT)


def _dot(a, b, ca, cb):
    return lax.dot_general(a, b, (((ca,), (cb,)), ((), ())), preferred_element_type=F32)


def _nn(a, b):
    return _dot(a, b, 1, 0)


def _nt(a, b):
    return _dot(a, b, 1, 1)


def _tn(a, b):
    return _dot(a, b, 0, 0)


def _split_mm(x, m):
    hi = x.astype(BF16)
    lo = (x - hi.astype(F32)).astype(BF16)
    return _nn(hi, m) + _nn(lo, m)


def _tile(n, pref):
    t = min(n, pref)
    assert n % t == 0, (n, pref)
    return t


ANY = pl.BlockSpec(memory_space=pl.ANY)
JOB_MIDDLE, JOB_LATE = 0.5, 0.85


def _my_place():
    x, y, c = lax.axis_index("x"), lax.axis_index("y"), lax.axis_index("c")
    return x, y, c, 4 * x + 2 * y + c


def _peer(x, y, c, k):
    px = 1 - x if k & 4 else x
    py = 1 - y if k & 2 else y
    pc = 1 - c if k & 1 else c
    return (px, py, pc), 4 * px + 2 * py + pc


def _remote(src, dst, send_sems, recv_sems, i, peer):
    return pltpu.make_async_remote_copy(src_ref=src, dst_ref=dst, send_sem=send_sems.at[i], recv_sem=recv_sems.at[i],
                                        device_id=peer, device_id_type=MESH_ID)


class GatherJob:
    def __init__(self, srcs):
        self.srcs = srcs
        self.args = [a for a, _ in srcs]
        self.dims = [a.shape[-2:] for a, _ in srcs]
        n = self.n_in = self.n_out = len(srcs)
        self.out_shape = [jax.ShapeDtypeStruct((N_DEV * r, cc), a.dtype) for (a, _), (r, cc) in zip(srcs, self.dims)]
        self.scratch = [pltpu.SemaphoreType.DMA((N_DEV * n,)), pltpu.SemaphoreType.DMA((N_DEV * n,)),
                        pltpu.SemaphoreType.DMA((n,))]

    def _rows(self, outs, t, idx):
        r = self.dims[t][0]
        return outs[t].at[pl.ds(pl.multiple_of(idx * r, 8), r), :]

    def _local(self, ins, outs, loc, t, me):
        src = ins[t] if self.srcs[t][1] is None else ins[t].at[self.srcs[t][1]]
        return src, pltpu.make_async_copy(src, self._rows(outs, t, me), loc.at[t])

    def start(self, ins, outs, sems):
        send, recv, loc = sems
        x, y, c, me = _my_place()
        for t in range(self.n_in):
            src, local = self._local(ins, outs, loc, t, me)
            local.start()
            for k in (2, 4, 1):
                _remote(src, self._rows(outs, t, me), send, recv, N_DEV * t + k, _peer(x, y, c, k)[0]).start()

    def _copy(self, outs, sems, t, origin, i, to):
        x, y, c, _ = _my_place()
        blk = self._rows(outs, t, _peer(x, y, c, origin)[1])
        return _remote(blk, blk, sems[0], sems[1], N_DEV * t + i, _peer(x, y, c, to)[0])

    def middle(self, ins, outs, sems):
        c = _my_place()[2]

        def relay(t, got, to):
            self._copy(outs, sems, t, got, got, got).wait_recv()
            self._copy(outs, sems, t, got, 6, to).start()
            self._copy(outs, sems, t, to, to, to).wait_recv()

        for t in range(self.n_in):
            pl.when(c == 1)(functools.partial(relay, t, 2, 4))
            pl.when(c == 0)(functools.partial(relay, t, 4, 2))
            for k in (2, 4):
                self._copy(outs, sems, t, k, k + 1, 1).start()

    def late(self, ins, outs, sems):
        for t in range(self.n_in):
            self._copy(outs, sems, t, 6, 6, 6).wait_recv()
            self._copy(outs, sems, t, 6, 7, 1).start()

    def finish(self, ins, outs, sems):
        send, recv, loc = sems
        x, y, c, me = _my_place()
        for t in range(self.n_in):
            for k in range(1, N_DEV):
                peer, pidx = _peer(x, y, c, k)
                blk = self._rows(outs, t, pidx)
                if k % 2 == 1:
                    _remote(blk, blk, send, recv, N_DEV * t + k, peer).wait_recv()
                _remote(blk, blk, send, recv, N_DEV * t + k, peer).wait_send()
            self._local(ins, outs, loc, t, me)[1].wait()


def _call(body, name, grid, in_specs, args, out_specs, out_shape, scratch=(), sem=(), job=None, aliases=None):
    in_specs, out_specs, out_shape, scratch = list(in_specs), list(out_specs), list(out_shape), list(scratch)
    n_in, n_out, n_scr = len(args), len(out_shape), len(scratch)
    if job is None:
        res = pl.pallas_call(body, name=name, grid=grid, in_specs=in_specs, out_specs=out_specs, out_shape=out_shape,
                             scratch_shapes=scratch, input_output_aliases=aliases or {}, compiler_params=_cp(*sem))(*args)
        return res, None
    o0 = n_in + job.n_in
    s0 = o0 + n_out + job.n_out

    def carrier(*refs):
        jin, jout, jsem = refs[n_in:o0], refs[o0 + n_out:s0], refs[s0 + n_scr:]
        ids = [pl.program_id(a) for a in range(len(grid))]
        def at(step):
            idx = []
            for g in reversed(grid):
                idx.append(step % g)
                step //= g
            return functools.reduce(jnp.logical_and, [i == j for i, j in zip(ids, reversed(idx))])

        steps = math.prod(grid)
        if grid:
            pl.when(at(0))(lambda: job.start(jin, jout, jsem))
            pl.when(at(int(steps * JOB_MIDDLE)))(lambda: job.middle(jin, jout, jsem))
            pl.when(at(int(steps * JOB_LATE)))(lambda: job.late(jin, jout, jsem))
        else:
            job.start(jin, jout, jsem)
            job.middle(jin, jout, jsem)
            job.late(jin, jout, jsem)
        body(*refs[:n_in], *refs[o0:o0 + n_out], *refs[s0:s0 + n_scr])
        if grid:
            pl.when(at(steps - 1))(lambda: job.finish(jin, jout, jsem))
        else:
            job.finish(jin, jout, jsem)

    res = pl.pallas_call(
        carrier, name=name + "_comm", grid=grid, in_specs=in_specs + [ANY] * job.n_in,
        out_specs=out_specs + [ANY] * job.n_out, out_shape=out_shape + job.out_shape,
        scratch_shapes=scratch + job.scratch, input_output_aliases=aliases or {},
        compiler_params=_cp(*(["arbitrary"] * len(grid))))(*args, *job.args)
    return res[:n_out], res[n_out:]


def run_job(job, name):
    return _call(lambda: None, name, (), [], [], [], [], job=job)[1]


@jax.custom_vjp
def bmm(x, w):
    return _nn(x.astype(BF16), w.astype(BF16))


def _bmm_fwd(x, w):
    return bmm(x, w), (x, w)


def _bmm_bwd(res, g):
    x, w = res
    gb = g.astype(BF16)
    return _nt(gb, w.astype(BF16)), _tn(x.astype(BF16), gb)


bmm.defvjp(_bmm_fwd, _bmm_bwd)


def _shift_raw(x, d):
    n = x.shape[0]
    r = pltpu.roll(x, d % n, axis=0)
    row = lax.broadcasted_iota(jnp.int32, x.shape, 0)
    keep = (row >= d) if d > 0 else (row < n + d)
    return jnp.where(keep, r, 0.0)


def shift_rows(x, d):
    @jax.custom_vjp
    def f(v):
        return _shift_raw(v, d)

    f.defvjp(lambda v: (_shift_raw(v, d), None), lambda _, g: (_shift_raw(g, -d),))
    return f(x)


def _swap_raw(x):
    n = x.shape[1]
    nxt = pltpu.roll(x, n - 1, axis=1)
    prv = pltpu.roll(x, 1, axis=1)
    lane = lax.broadcasted_iota(jnp.int32, x.shape, 1)
    return jnp.where(lane % 2 == 0, nxt, prv)


@jax.custom_vjp
def swap_pairs(x):
    return _swap_raw(x)


swap_pairs.defvjp(lambda x: (_swap_raw(x), None), lambda _, g: (_swap_raw(g),))


@jax.custom_vjp
def group_mean(x, bd):
    return _split_mm(x, bd)


group_mean.defvjp(lambda x, bd: (_split_mm(x, bd), bd), lambda bd, g: (_split_mm(g, bd), jnp.zeros_like(bd)))


def _rope_norm(x, gain, cos, sgn, bd, scale):
    xn = x * lax.rsqrt(group_mean(x * x, bd) + EPS) * gain
    return (xn * cos + swap_pairs(xn) * sgn) * scale


def _conv_gate(gb, gc, hc, w):
    z = gc * hc
    c = shift_rows(z, 1) * w[0:1] + z * w[1:2] + shift_rows(z, -1) * w[2:3]
    return gb * c


def _window_sum(p, r):
    b = f = p
    k = 1
    while k < r:
        b = b + shift_rows(b, k)
        f = f + shift_rows(f, -k)
        k *= 2
    return b + f - p + shift_rows(p, r) + shift_rows(p, -r)


def _pool_mix(p, pool_w, scale):
    n = p.shape[0]
    t = lax.broadcasted_iota(jnp.int32, (n, 1), 0)
    outs = []
    for gi, r in enumerate(POOL_RADII):
        pg = p[:, gi * GROUP:(gi + 1) * GROUP]
        cnt = (jnp.minimum(t + r, n - 1) - jnp.maximum(t - r, 0) + 1).astype(F32)
        pooled = _window_sum(pg, r) / cnt - pg
        outs.append(bmm(pooled, pool_w[gi]))
    return jnp.concatenate(outs, axis=1) * scale


def _sgu(u, v, norm_g, w_s, b_full):
    ug = jax.nn.gelu(u)
    vg = jax.nn.gelu(v)
    vn = vg * lax.rsqrt(jnp.mean(vg * vg, axis=-1, keepdims=True) + EPS) * norm_g
    cols = []
    for g in range(w_s.shape[0]):
        rows = []
        for n in range(u.shape[0] // SGU_CHUNK):
            blk = vn[n * SGU_CHUNK:(n + 1) * SGU_CHUNK, g * GROUP:(g + 1) * GROUP]
            rows.append(bmm(w_s[g], blk) + b_full[g])
        cols.append(jnp.concatenate(rows, axis=0))
    return ug * jnp.concatenate(cols, axis=1)


def _rms_bwd_math(xv, gain, dy, dres):
    r = lax.rsqrt(jnp.mean(xv * xv, axis=-1, keepdims=True) + EPS)
    xh = xv * r
    dxh = dy * gain
    dx = dres + r * (dxh - xh * jnp.mean(dxh * xh, axis=-1, keepdims=True))
    return dx, jnp.sum(dy * xh, axis=0, keepdims=True)


FFN_TN = 256


def ffn_fwd(x, gain, wt_in, w_out, job=None):
    T, D = x.shape
    F = w_out.shape[0]
    tm, tn = _tile(T, 1024), FFN_TN
    nc = F // tn

    def body(x_ref, gn_ref, wg_ref, wu_ref, wo_ref, y_ref, gu_ref, xn_s, acc_s):
        c = pl.program_id(1)

        @pl.when(c == 0)
        def _():
            xv = x_ref[...]
            r = lax.rsqrt(jnp.mean(xv * xv, axis=-1, keepdims=True) + EPS)
            xn_s[...] = (xv * r * gn_ref[...]).astype(BF16)
            acc_s[...] = jnp.zeros_like(acc_s)

        xn = xn_s[...]
        g = _nt(xn, wg_ref[...])
        u = _nt(xn, wu_ref[...])
        gu_ref[:, :tn] = g.astype(BF16)
        gu_ref[:, tn:] = u.astype(BF16)
        a = (g * jax.nn.sigmoid(g) * u).astype(BF16)
        acc_s[...] += _nn(a, wo_ref[...])

        @pl.when(c == nc - 1)
        def _():
            y_ref[...] = x_ref[...] + 0.5 * acc_s[...]

    row = pl.BlockSpec((tm, D), lambda i, c: (i, 0))
    return _call(
        body, "ffn_fwd", (T // tm, nc),
        [row, pl.BlockSpec((1, D), lambda i, c: (0, 0)),
         pl.BlockSpec((tn, D), lambda i, c: (c, 0)),
         pl.BlockSpec((tn, D), lambda i, c: (c + nc, 0)),
         pl.BlockSpec((tn, D), lambda i, c: (c, 0))],
        [x, gain, wt_in, wt_in, w_out],
        [row, pl.BlockSpec((tm, 2 * tn), lambda i, c: (i, c)), row],
        [jax.ShapeDtypeStruct((T, D), F32), jax.ShapeDtypeStruct((T, 2 * F), BF16), jax.ShapeDtypeStruct((T, D), BF16)],
        [pltpu.VMEM((tm, D), F32)], ("parallel", "arbitrary"), job)


def ffn_bwd_x(dout, x, gain, gu, wt_in, w_out, job=None):
    T, D = x.shape
    F = w_out.shape[0]
    tm, tn = _tile(T, 1024), FFN_TN
    nc = F // tn

    def body(do_ref, x_ref, gn_ref, gu_ref, wg_ref, wu_ref, wo_ref,
             dx_ref, dgn_ref, a_ref, dgu_ref, dob_ref, acc_s):
        i, c = pl.program_id(0), pl.program_id(1)

        @pl.when(c == 0)
        def _():
            dob_ref[...] = (0.5 * do_ref[...]).astype(BF16)
            acc_s[...] = jnp.zeros_like(acc_s)

        da = jnp.concatenate([_nt(dob_ref[:tm // 2, :], wo_ref[...]), _nt(dob_ref[tm // 2:, :], wo_ref[...])], axis=0)
        g = gu_ref[:, :tn].astype(F32)
        u = gu_ref[:, tn:].astype(F32)
        sig = jax.nn.sigmoid(g)
        sl = g * sig
        a_ref[...] = (sl * u).astype(BF16)
        dg = (da * u * (sig * (1.0 + g * (1.0 - sig)))).astype(BF16)
        du = (da * sl).astype(BF16)
        dgu_ref[:, :tn] = dg
        dgu_ref[:, tn:] = du
        acc_s[...] += _nn(dg, wg_ref[...]) + _nn(du, wu_ref[...])

        @pl.when(c == nc - 1)
        def _():
            dx, dgn = _rms_bwd_math(x_ref[...], gn_ref[...], acc_s[...], do_ref[...])
            dx_ref[...] = dx

            @pl.when(i == 0)
            def _():
                dgn_ref[...] = jnp.zeros_like(dgn_ref)

            dgn_ref[...] += dgn

    row = pl.BlockSpec((tm, D), lambda i, c: (i, 0))
    return _call(
        body, "ffn_bwd_x", (T // tm, nc),
        [row, row, pl.BlockSpec((1, D), lambda i, c: (0, 0)),
         pl.BlockSpec((tm, 2 * tn), lambda i, c: (i, c)),
         pl.BlockSpec((tn, D), lambda i, c: (c, 0)),
         pl.BlockSpec((tn, D), lambda i, c: (c + nc, 0)),
         pl.BlockSpec((tn, D), lambda i, c: (c, 0))],
        [dout, x, gain, gu, wt_in, wt_in, w_out],
        [row, pl.BlockSpec((1, D), lambda i, c: (0, 0)),
         pl.BlockSpec((tm, tn), lambda i, c: (i, c)),
         pl.BlockSpec((tm, 2 * tn), lambda i, c: (i, c)), row],
        [jax.ShapeDtypeStruct((T, D), F32), jax.ShapeDtypeStruct((1, D), F32),
         jax.ShapeDtypeStruct((T, F), BF16), jax.ShapeDtypeStruct((T, 2 * F), BF16),
         jax.ShapeDtypeStruct((T, D), BF16)],
        [pltpu.VMEM((tm, D), F32)], ("arbitrary", "arbitrary"), job)


MM_TM = 512
MM_TC = 256


def mm_nt(a, wt, pieces, out_dtype, emit_a_bf16=False, norm_gain=None):
    T, K = a.shape
    tm = _tile(T, MM_TM)
    npc = len(pieces)
    n_lead = 1 if norm_gain is None else 2

    def body(*refs):
        a_ref, w_refs, o_refs = refs[0], refs[n_lead:n_lead + npc], refs[n_lead + npc:]
        av = a_ref[...]
        if norm_gain is not None:
            av = av * lax.rsqrt(jnp.mean(av * av, axis=-1, keepdims=True) + EPS) * refs[1][...]
        ab = av.astype(BF16)
        for w_ref, o_ref in zip(w_refs, o_refs[:npc]):
            o_ref[...] = _nt(ab, w_ref[...]).astype(o_ref.dtype)
        if emit_a_bf16:
            o_refs[npc][...] = ab

    in_specs = [pl.BlockSpec((tm, K), lambda i: (i, 0))]
    if norm_gain is not None:
        in_specs.append(pl.BlockSpec((1, K), lambda i: (0, 0)))
    out_specs, out_shape = [], []
    for r0, n in pieces:
        assert r0 % n == 0
        in_specs.append(pl.BlockSpec((n, K), functools.partial(lambda i, b: (b, 0), b=r0 // n)))
        out_specs.append(pl.BlockSpec((tm, n), lambda i: (i, 0)))
        out_shape.append(jax.ShapeDtypeStruct((T, n), out_dtype))
    if emit_a_bf16:
        out_specs.append(pl.BlockSpec((tm, K), lambda i: (i, 0)))
        out_shape.append(jax.ShapeDtypeStruct((T, K), BF16))
    return pl.pallas_call(
        body, name="mm_nt", grid=(T // tm,), in_specs=in_specs, out_specs=out_specs, out_shape=out_shape,
        compiler_params=_cp("parallel"),
    )(a, *([] if norm_gain is None else [norm_gain]), *([wt] * npc))


def mm_nn(pieces, w, residual=None, norm_bwd=None):
    T = pieces[0][0].shape[0]
    N = w.shape[1]
    tm = _tile(T, MM_TM)
    na = len(pieces)

    def body(*refs):
        a_refs, w_refs = refs[:na], refs[na:2 * na]
        acc = refs[2 * na][...] if residual is not None else None
        for a_ref, w_ref in zip(a_refs, w_refs):
            t = _nn(a_ref[...].astype(BF16), w_ref[...])
            acc = t if acc is None else acc + t
        if norm_bwd is None:
            refs[-1][...] = acc
            return
        x_ref, g_ref, dr_ref, dx_ref, dg_ref = refs[-5:]
        dx, dg = _rms_bwd_math(x_ref[...], g_ref[...], acc, dr_ref[...])
        dx_ref[...] = dx

        @pl.when(pl.program_id(0) == 0)
        def _():
            dg_ref[...] = jnp.zeros_like(dg_ref)

        dg_ref[...] += dg

    in_specs, w_specs = [], []
    for a, cb, k, rb in pieces:
        in_specs.append(pl.BlockSpec((tm, k), functools.partial(lambda i, b: (i, b), b=cb)))
        w_specs.append(pl.BlockSpec((k, N), functools.partial(lambda i, b: (b, 0), b=rb)))
    assert sum(k for _, _, k, _ in pieces) == w.shape[0]
    args = [a for a, _, _, _ in pieces] + [w] * na
    in_specs = in_specs + w_specs
    row = pl.BlockSpec((tm, N), lambda i: (i, 0))
    if residual is not None:
        in_specs.append(row)
        args.append(residual)
    if norm_bwd is None:
        return pl.pallas_call(
            body, name="mm_nn", grid=(T // tm,), in_specs=in_specs, out_specs=row,
            out_shape=jax.ShapeDtypeStruct((T, N), F32), compiler_params=_cp("parallel"),
        )(*args)
    vec = pl.BlockSpec((1, N), lambda i: (0, 0))
    return pl.pallas_call(
        body, name="mm_nn_norm_bwd", grid=(T // tm,), in_specs=in_specs + [row, vec, row], out_specs=[row, vec],
        out_shape=[jax.ShapeDtypeStruct((T, N), F32), jax.ShapeDtypeStruct((1, N), F32)],
        compiler_params=_cp("arbitrary"),
    )(*args, *norm_bwd)


def mm_tn(a, b, n_rows, row_block, prev=None, grid=None, col_block=None, job=None):
    T, M = a.shape
    N = b.shape[1]
    tc = MM_TC
    assert M % tc == 0 and n_rows % tc == 0
    if grid is None:
        grid, col_block = (M // tc,), (lambda j: j)

    def body(*refs):
        a_ref, b_ref, o_ref = refs[0], refs[1], refs[-1]
        o_ref[...] = _tn(a_ref[...], b_ref[...]).astype(BF16)

    in_specs = [pl.BlockSpec((T, tc), lambda *g: (0, col_block(*g))), pl.BlockSpec((T, N), lambda *g: (0, 0))]
    args = [a, b]
    aliases = {}
    if prev is not None:
        in_specs.append(pl.BlockSpec(memory_space=pl.ANY))
        args.append(prev)
        aliases = {2: 0}
    res, jres = _call(body, "mm_tn", grid, in_specs, args, [pl.BlockSpec((tc, N), lambda *g: (row_block(*g), 0))],
                      [jax.ShapeDtypeStruct((n_rows, N), BF16)], (), ["parallel"] * len(grid), job, aliases)
    return res[0] if job is None else (res[0], jres)


def conv_fwd(proj_a, conv_w, n_ex):
    T, C3 = proj_a.shape
    C = C3 // 3
    S = T // n_ex

    def body(gb_ref, gc_ref, hc_ref, w_ref, o_ref):
        o_ref[...] = _conv_gate(gb_ref[...], gc_ref[...], hc_ref[...], w_ref[...]).astype(BF16)

    col = lambda k: pl.BlockSpec((S, C), functools.partial(lambda b, kk: (b, kk), kk=k))
    return pl.pallas_call(
        body, name="conv_fwd", grid=(n_ex,),
        in_specs=[col(0), col(1), col(2), pl.BlockSpec((3, C), lambda b: (0, 0))],
        out_specs=pl.BlockSpec((S, C), lambda b: (b, 0)),
        out_shape=jax.ShapeDtypeStruct((T, 2 * C), BF16), compiler_params=_cp("parallel"),
    )(proj_a, proj_a, proj_a, conv_w)


def conv_bwd(proj_a, conv_w, n_ex, dy, total_cols):
    T, C3 = proj_a.shape
    C = C3 // 3
    S = T // n_ex

    def body(gb_ref, gc_ref, hc_ref, w_ref, dy_ref, dp_ref, dw_ref):
        _, vjp = jax.vjp(_conv_gate, gb_ref[...], gc_ref[...], hc_ref[...], w_ref[...])
        dgb, dgc, dhc, dw = vjp(dy_ref[...].astype(F32))
        dp_ref[:, 0:C] = dgb.astype(BF16)
        dp_ref[:, C:2 * C] = dgc.astype(BF16)
        dp_ref[:, 2 * C:] = dhc.astype(BF16)

        @pl.when(pl.program_id(0) == 0)
        def _():
            dw_ref[...] = jnp.zeros_like(dw_ref)

        dw_ref[...] += dw

    col = lambda k: pl.BlockSpec((S, C), functools.partial(lambda b, kk: (b, kk), kk=k))
    return pl.pallas_call(
        body, name="conv_bwd", grid=(n_ex,),
        in_specs=[col(0), col(1), col(2), pl.BlockSpec((3, C), lambda b: (0, 0)),
                  pl.BlockSpec((S, C), lambda b: (b, 0))],
        out_specs=[pl.BlockSpec((S, C3), lambda b: (b, 0)), pl.BlockSpec((3, C), lambda b: (0, 0))],
        out_shape=[jax.ShapeDtypeStruct((T, total_cols), BF16), jax.ShapeDtypeStruct((3, C), F32)],
        compiler_params=_cp("arbitrary"),
    )(proj_a, proj_a, proj_a, conv_w, dy)


QW = N_Q_HEADS * HEAD_DIM
KW = N_KV_HEADS * HEAD_DIM
QP = N_Q_HEADS * LANES
KP = N_KV_HEADS * LANES


def _attn_consts(seq):
    rows = seq // GRID_W
    r_idx, c_idx = jnp.meshgrid(jnp.arange(rows), jnp.arange(GRID_W), indexing='ij')
    r_idx = r_idx.reshape(-1).astype(F32)
    c_idx = c_idx.reshape(-1).astype(F32)
    n_freq = HEAD_DIM // 4
    inv = ROPE_THETA ** (-jnp.arange(n_freq, dtype=F32) / n_freq)
    ang = jnp.concatenate([r_idx[:, None] * inv, c_idx[:, None] * inv], axis=-1)
    cos = jnp.repeat(jnp.cos(ang), 2, axis=1)
    sin = jnp.repeat(jnp.sin(ang), 2, axis=1)
    sgn = sin * jnp.tile(jnp.array([-1.0, 1.0], F32), HEAD_DIM // 2)
    cos = jnp.tile(cos, (1, N_Q_HEADS))
    sgn = jnp.tile(sgn, (1, N_Q_HEADS))
    lane = jnp.arange(QW)
    bd = jnp.where(lane[:, None] // HEAD_DIM == lane[None, :] // HEAD_DIM, 1.0 / HEAD_DIM, 0.0).astype(BF16)
    dst = (lane // HEAD_DIM) * LANES + lane % HEAD_DIM
    spread = (dst[:, None] == jnp.arange(QP)[None, :]).astype(BF16)
    return dict(cos=cos, sgn=sgn, bd=bd, spread=spread, gather=spread.T)


def qkv_prep_fwd(proj_b, qg, kg, cst, n_ex):
    T = proj_b.shape[0]
    S = T // n_ex
    tm = _tile(S, 512)
    nb = S // tm

    def body(p_ref, qg_ref, kg_ref, cos_ref, sgn_ref, bd_ref, sp_ref, q_ref, k_ref, v_ref):
        pv = p_ref[...]
        cos, sgn, bd, sp = cos_ref[...], sgn_ref[...], bd_ref[...], sp_ref[...]
        qr = _rope_norm(pv[:, :QW], qg_ref[...], cos, sgn, bd, HEAD_DIM ** -0.5)
        kr = _rope_norm(pv[:, QW:QW + KW], kg_ref[...], cos[:, :KW], sgn[:, :KW], bd[:KW, :KW], 1.0)
        q_ref[...] = _nn(qr.astype(BF16), sp).astype(BF16)
        k_ref[...] = _nn(kr.astype(BF16), sp[:KW, :KP]).astype(BF16)
        v_ref[...] = _nn(pv[:, QW + KW:].astype(BF16), sp[:KW, :KP]).astype(BF16)

    full = lambda a: pl.BlockSpec(a.shape, lambda i: (0,) * a.ndim)
    tab = pl.BlockSpec((tm, QW), lambda i: (i % nb, 0))
    return pl.pallas_call(
        body, name="qkv_prep_fwd", grid=(T // tm,),
        in_specs=[pl.BlockSpec((tm, QW + 2 * KW), lambda i: (i, 0)), full(qg), full(kg), tab, tab,
                  full(cst["bd"]), full(cst["spread"])],
        out_specs=[pl.BlockSpec((tm, QP), lambda i: (i, 0)), pl.BlockSpec((tm, KP), lambda i: (i, 0)),
                   pl.BlockSpec((tm, KP), lambda i: (i, 0))],
        out_shape=[jax.ShapeDtypeStruct((T, QP), BF16), jax.ShapeDtypeStruct((T, KP), BF16),
                   jax.ShapeDtypeStruct((T, KP), BF16)],
        compiler_params=_cp("parallel"),
    )(proj_b, qg, kg, cst["cos"], cst["sgn"], cst["bd"], cst["spread"])


def qkv_prep_bwd(proj_b, qg, kg, cst, n_ex, dq, dk_pad, dv_pad, d_proj):
    T = proj_b.shape[0]
    S = T // n_ex
    tm = _tile(S, 512)
    nb = S // tm

    def body(p_ref, qg_ref, kg_ref, cos_ref, sgn_ref, bd_ref, ga_ref, dq_ref, dk_ref, dv_ref, _kept,
             dp_ref, dqg_ref, dkg_ref):
        pv = p_ref[...]
        cos, sgn, bd, ga = cos_ref[...], sgn_ref[...], bd_ref[...], ga_ref[...]
        fq = lambda q, g: _rope_norm(q, g, cos, sgn, bd, HEAD_DIM ** -0.5)
        fk = lambda k, g: _rope_norm(k, g, cos[:, :KW], sgn[:, :KW], bd[:KW, :KW], 1.0)
        _, vq = jax.vjp(fq, pv[:, :QW], qg_ref[...])
        _, vk = jax.vjp(fk, pv[:, QW:QW + KW], kg_ref[...])
        dqp, dqg = vq(dq_ref[...])
        dkp, dkg = vk(_split_mm(dk_ref[...], ga[:KP, :KW]))
        dp_ref[:, :QW] = dqp.astype(BF16)
        dp_ref[:, QW:QW + KW] = dkp.astype(BF16)
        dp_ref[:, QW + KW:] = _split_mm(dv_ref[...], ga[:KP, :KW]).astype(BF16)

        @pl.when(pl.program_id(0) == 0)
        def _():
            dqg_ref[...] = jnp.zeros_like(dqg_ref)
            dkg_ref[...] = jnp.zeros_like(dkg_ref)

        dqg_ref[...] += dqg
        dkg_ref[...] += dkg

    full = lambda a: pl.BlockSpec(a.shape, lambda i: (0,) * a.ndim)
    tab = pl.BlockSpec((tm, QW), lambda i: (i % nb, 0))
    row = lambda n: pl.BlockSpec((tm, n), lambda i: (i, 0))
    wb = QW + 2 * KW
    assert d_proj.shape[1] % wb == 0
    last = d_proj.shape[1] // wb - 1
    return pl.pallas_call(
        body, name="qkv_prep_bwd", grid=(T // tm,),
        in_specs=[row(wb), full(qg), full(kg), tab, tab, full(cst["bd"]), full(cst["gather"]),
                  row(QW), row(KP), row(KP), ANY],
        out_specs=[pl.BlockSpec((tm, wb), lambda i: (i, last)), pl.BlockSpec((1, QW), lambda i: (0, 0)),
                   pl.BlockSpec((1, KW), lambda i: (0, 0))],
        out_shape=[jax.ShapeDtypeStruct(d_proj.shape, BF16), jax.ShapeDtypeStruct((1, QW), F32),
                   jax.ShapeDtypeStruct((1, KW), F32)],
        input_output_aliases={10: 0}, compiler_params=_cp("arbitrary"),
    )(proj_b, qg, kg, cst["cos"], cst["sgn"], cst["bd"], cst["gather"], dq, dk_pad, dv_pad, d_proj)


ATT_TQ = 256
ATT_TQ_FWD = 512


def attn_fwd(qp, kp, vp, gather, n_ex, mix, job=None):
    T = qp.shape[0]
    S = T // n_ex
    tq = _tile(S, ATT_TQ_FWD)
    nq = S // tq

    def body(q_ref, k_ref, v_ref, ga_ref, _kept, o_ref, op_ref, lse_ref):
        lane = lax.broadcasted_iota(jnp.int32, (tq, LANES), 1)
        lse_all = jnp.zeros((tq, LANES), F32)
        for h in range(N_Q_HEADS):
            kv = h // Q_PER_KV
            qh = q_ref[:, h * LANES:(h + 1) * LANES]
            s = _nt(qh, k_ref[:, kv * LANES:(kv + 1) * LANES])
            m = jnp.max(s, axis=-1, keepdims=True)
            p = jnp.exp(s - m)
            lsum = jnp.sum(p, axis=-1, keepdims=True)
            o = _nn(p.astype(BF16), v_ref[:, kv * LANES:(kv + 1) * LANES]) / lsum
            op_ref[:, h * LANES:(h + 1) * LANES] = o.astype(BF16)
            lse_all = jnp.where(lane == h, m + jnp.log(lsum), lse_all)
        lse_ref[...] = lse_all
        o_ref[...] = _nn(op_ref[...], ga_ref[...]).astype(BF16)

    blk = lambda n: pl.BlockSpec((tq, n), lambda b, i: (b * nq + i, 0))
    kvs = pl.BlockSpec((S, KP), lambda b, i: (b, 0))
    return _call(
        body, "attn_fwd", (n_ex, nq),
        [blk(QP), kvs, kvs, pl.BlockSpec(gather.shape, lambda b, i: (0, 0)), ANY], [qp, kp, vp, gather, mix],
        [pl.BlockSpec((tq, QW), lambda b, i: (b * nq + i, 1)), blk(QP), blk(LANES)],
        [jax.ShapeDtypeStruct(mix.shape, BF16), jax.ShapeDtypeStruct((T, QP), BF16),
         jax.ShapeDtypeStruct((T, LANES), F32)], (), ("parallel", "parallel"), job, {4: 0})


def attn_bwd(qp, kp, vp, op, lse, do, cst, n_ex, job=None):
    T = qp.shape[0]
    S = T // n_ex
    tq = _tile(S, ATT_TQ)
    nq = S // tq

    def body(q_ref, k_ref, v_ref, op_ref, lse_ref, do_ref, sp_ref, ga_ref, dq_ref, dk_ref, dv_ref, dqp_s):
        @pl.when(pl.program_id(1) == 0)
        def _():
            dk_ref[...] = jnp.zeros_like(dk_ref)
            dv_ref[...] = jnp.zeros_like(dv_ref)

        lane = lax.broadcasted_iota(jnp.int32, (tq, LANES), 1)
        dop = _nn(do_ref[...], sp_ref[...]).astype(BF16)
        lse_all = lse_ref[...]
        for h in range(N_Q_HEADS):
            kv = h // Q_PER_KV
            hs = slice(h * LANES, (h + 1) * LANES)
            ks = slice(kv * LANES, (kv + 1) * LANES)
            qh, kk, vv = q_ref[:, hs], k_ref[:, ks], v_ref[:, ks]
            doh = dop[:, hs]
            lse_h = jnp.sum(jnp.where(lane == h, lse_all, 0.0), axis=-1, keepdims=True)
            p = jnp.exp(_nt(qh, kk) - lse_h)
            dp = _nt(doh, vv)
            delta = jnp.sum(doh.astype(F32) * op_ref[:, hs].astype(F32), axis=-1, keepdims=True)
            ds = (p * (dp - delta)).astype(BF16)
            dqp_s[:, hs] = _nn(ds, kk)
            dk_ref[:, ks] += _tn(ds, qh)
            dv_ref[:, ks] += _tn(p.astype(BF16), doh)
        dq_ref[...] = _split_mm(dqp_s[...], ga_ref[...])

    blk = lambda n: pl.BlockSpec((tq, n), lambda b, i: (b * nq + i, 0))
    kvs = pl.BlockSpec((S, KP), lambda b, i: (b, 0))
    full = lambda a: pl.BlockSpec(a.shape, lambda b, i: (0, 0))
    return _call(
        body, "attn_bwd", (n_ex, nq),
        [blk(QP), kvs, kvs, blk(QP), blk(LANES), pl.BlockSpec((tq, QW), lambda b, i: (b * nq + i, 1)),
         full(cst["spread"]), full(cst["gather"])],
        [qp, kp, vp, op, lse, do, cst["spread"], cst["gather"]],
        [blk(QW), kvs, kvs],
        [jax.ShapeDtypeStruct((T, QW), F32), jax.ShapeDtypeStruct((T, KP), F32), jax.ShapeDtypeStruct((T, KP), F32)],
        [pltpu.VMEM((tq, QP), F32)], ("arbitrary", "arbitrary"), job)


def pool_fwd(p, pool_w, scale, n_ex):
    T, W = p.shape
    S = T // n_ex

    def body(p_ref, w_ref, s_ref, o_ref):
        o_ref[...] = _pool_mix(p_ref[...], w_ref[...], s_ref[...]).astype(BF16)

    return pl.pallas_call(
        body, name="pool_fwd", grid=(n_ex,),
        in_specs=[pl.BlockSpec((S, W), lambda b: (b, 0)),
                  pl.BlockSpec(pool_w.shape, lambda b: (0, 0, 0)),
                  pl.BlockSpec((1, W), lambda b: (0, 0))],
        out_specs=pl.BlockSpec((S, W), lambda b: (b, 0)),
        out_shape=jax.ShapeDtypeStruct((T, 2 * W), BF16), compiler_params=_cp("parallel"),
    )(p, pool_w, scale)


def pool_bwd(p, pool_w, scale, n_ex, dy, d_proj):
    T, W = p.shape
    S = T // n_ex
    last = d_proj.shape[1] // W - 1

    def body(p_ref, w_ref, s_ref, dy_ref, _kept, dp_ref, dw_ref, ds_ref):
        _, vjp = jax.vjp(_pool_mix, p_ref[...], w_ref[...], s_ref[...])
        dp, dw, ds = vjp(dy_ref[...].astype(F32))
        dp_ref[...] = dp.astype(BF16)

        @pl.when(pl.program_id(0) == 0)
        def _():
            dw_ref[...] = jnp.zeros_like(dw_ref)
            ds_ref[...] = jnp.zeros_like(ds_ref)

        dw_ref[...] += dw
        ds_ref[...] += ds

    wshape = pool_w.shape
    return pl.pallas_call(
        body, name="pool_bwd", grid=(n_ex,),
        in_specs=[pl.BlockSpec((S, W), lambda b: (b, 0)),
                  pl.BlockSpec(wshape, lambda b: (0, 0, 0)),
                  pl.BlockSpec((1, W), lambda b: (0, 0)), pl.BlockSpec((S, W), lambda b: (b, 0)), ANY],
        out_specs=[pl.BlockSpec((S, W), lambda b: (b, last)), pl.BlockSpec(wshape, lambda b: (0, 0, 0)),
                   pl.BlockSpec((1, W), lambda b: (0, 0))],
        out_shape=[jax.ShapeDtypeStruct(d_proj.shape, BF16), jax.ShapeDtypeStruct(wshape, F32),
                   jax.ShapeDtypeStruct((1, W), F32)],
        input_output_aliases={4: 0}, compiler_params=_cp("arbitrary"),
    )(p, pool_w, scale, dy, d_proj)


SGU_TS = 512


def sgu_fwd(u, v, norm_g, w_s, b_full, mix):
    T, W = u.shape
    ts = _tile(T, SGU_TS)

    def body(u_ref, v_ref, g_ref, w_ref, b_ref, _kept, o_ref):
        o_ref[...] = _sgu(u_ref[...], v_ref[...], g_ref[...], w_ref[...], b_ref[...]).astype(BF16)

    row = pl.BlockSpec((ts, W), lambda i: (i, 0))
    wsp = pl.BlockSpec(w_s.shape, lambda i: (0, 0, 0))
    return pl.pallas_call(
        body, name="sgu_fwd", grid=(T // ts,),
        in_specs=[row, row, pl.BlockSpec((1, W), lambda i: (0, 0)), wsp, wsp, ANY],
        out_specs=pl.BlockSpec((ts, W), lambda i: (i, 1)), out_shape=jax.ShapeDtypeStruct(mix.shape, BF16),
        input_output_aliases={5: 0}, compiler_params=_cp("parallel"),
    )(u, v, norm_g, w_s, b_full, mix)


def sgu_bwd(u, v, norm_g, w_s, b_full, dy):
    T, W = u.shape
    ts = _tile(T, SGU_TS)
    wshape = w_s.shape

    def body(u_ref, v_ref, g_ref, w_ref, b_ref, dy_ref, duv_ref, dg_ref, dw_ref, db_ref):
        _, vjp = jax.vjp(_sgu, u_ref[...], v_ref[...], g_ref[...], w_ref[...], b_ref[...])
        du, dv, dg, dw, db = vjp(dy_ref[...].astype(F32))
        duv_ref[:, :W] = du.astype(BF16)
        duv_ref[:, W:] = dv.astype(BF16)

        @pl.when(pl.program_id(0) == 0)
        def _():
            dg_ref[...] = jnp.zeros_like(dg_ref)
            dw_ref[...] = jnp.zeros_like(dw_ref)
            db_ref[...] = jnp.zeros_like(db_ref)

        dg_ref[...] += dg
        dw_ref[...] += dw
        db_ref[...] += db

    row = pl.BlockSpec((ts, W), lambda i: (i, 0))
    wsp = pl.BlockSpec(wshape, lambda i: (0, 0, 0))
    wout = pl.BlockSpec(wshape, lambda i: (0, 0, 0))
    vec = pl.BlockSpec((1, W), lambda i: (0, 0))
    return pl.pallas_call(
        body, name="sgu_bwd", grid=(T // ts,),
        in_specs=[row, row, vec, wsp, wsp, pl.BlockSpec((ts, W), lambda i: (i, 1))],
        out_specs=[pl.BlockSpec((ts, 2 * W), lambda i: (i, 0)), vec, wout, wout],
        out_shape=[jax.ShapeDtypeStruct((T, 3 * W), BF16),
                   jax.ShapeDtypeStruct((1, W), F32), jax.ShapeDtypeStruct(wshape, F32),
                   jax.ShapeDtypeStruct(wshape, F32)],
        compiler_params=_cp("arbitrary"),
    )(u, v, norm_g, w_s, b_full, dy)


def loss_head(x, gain, target):
    T, D = x.shape
    tm = _tile(T, 512)

    def body(x_ref, g_ref, t_ref, loss_ref, dx_ref, dg_ref):
        xv, g = x_ref[...], g_ref[...]
        r = lax.rsqrt(jnp.mean(xv * xv, axis=-1, keepdims=True) + EPS)
        err = xv * r * g - t_ref[...]
        part = 0.5 * jnp.sum(jnp.mean(err * err, axis=-1, keepdims=True), axis=0, keepdims=True)
        dx, dg = _rms_bwd_math(xv, g, err * (1.0 / D), jnp.zeros_like(xv))
        dx_ref[...] = dx

        @pl.when(pl.program_id(0) == 0)
        def _():
            loss_ref[...] = jnp.zeros_like(loss_ref)
            dg_ref[...] = jnp.zeros_like(dg_ref)

        loss_ref[...] += part
        dg_ref[...] += dg

    row = pl.BlockSpec((tm, D), lambda i: (i, 0))
    vec = pl.BlockSpec((1, D), lambda i: (0, 0))
    return pl.pallas_call(
        body, name="loss_head", grid=(T // tm,),
        in_specs=[row, vec, row], out_specs=[pl.BlockSpec((1, 1), lambda i: (0, 0)), row, vec],
        out_shape=[jax.ShapeDtypeStruct((1, 1), F32), jax.ShapeDtypeStruct((T, D), F32),
                   jax.ShapeDtypeStruct((1, D), F32)],
        compiler_params=_cp("arbitrary"),
    )(x, gain, target)


class MultiJob:
    def __init__(self, jobs):
        self.jobs = jobs
        self.args = [a for j in jobs for a in j.args]
        self.out_shape = [s for j in jobs for s in j.out_shape]
        self.scratch = [s for j in jobs for s in j.scratch]
        self.n_in, self.n_out = len(self.args), len(self.out_shape)

    def _each(self, ins, outs, sems):
        i = o = s = 0
        for j in self.jobs:
            yield j, ins[i:i + j.n_in], outs[o:o + j.n_out], sems[s:s + len(j.scratch)]
            i, o, s = i + j.n_in, o + j.n_out, s + len(j.scratch)

    def start(self, ins, outs, sems):
        for j, a, b, c in self._each(ins, outs, sems):
            j.start(a, b, c)

    def middle(self, ins, outs, sems):
        for j, a, b, c in self._each(ins, outs, sems):
            j.middle(a, b, c)

    def late(self, ins, outs, sems):
        for j, a, b, c in self._each(ins, outs, sems):
            j.late(a, b, c)

    def finish(self, ins, outs, sems):
        for j, a, b, c in self._each(ins, outs, sems):
            j.finish(a, b, c)

    def split(self, results):
        o = 0
        for j in self.jobs:
            yield results[o:o + j.n_out]
            o += j.n_out


class Plan:
    def __init__(self, shard, gathers, small_carrier=None, pack_small=None, exchange=True):
        self.shard, self.gathers, self.exchange = shard, gathers, exchange
        self.small_carrier, self.pack_small = small_carrier, pack_small
        self.weights, self.grads, self.started = {}, {}, []
        self.small_src = self.small_parts = None

    def scatter(self, keys, carry=None):
        if not self.exchange:
            return carry
        tag = "_".join(f"{kind}{l}" for kind, l in keys)
        send, recv, grads, zones, carry, token = scatter_start([self.grads[k] for k in keys], tag, carry)
        self.started.append((keys, send, recv, grads, zones, token, tag))
        return carry

    def collect(self, groups, after):
        got = {}
        for keys, send, recv, grads, zones, _, tag in groups:
            grads, zones = scatter_wait(send, recv, grads, zones, after, tag)
            got.update({k: (g, z) for k, g, z in zip(keys, grads, zones)})
        return got

    def weight(self, kind, l):
        return self.weights[(kind, l)]

    def grad(self, kind, l, g):
        self.grads[(kind, l)] = g

    def small_ready(self, small, d_final):
        if self.pack_small is not None:
            self.small_src = self.pack_small(small, d_final)

    def _jobs(self, key):
        jobs = []
        if key in self.gathers:
            ks = self.gathers[key]
            jobs.append((GatherJob([self.shard(*k) for k in ks]), self.weights, ks))
        if key == self.small_carrier and self.small_src is not None:
            jobs.append((GatherJob([(self.small_src, None)]), None, None))
        return jobs

    def _deliver(self, jobs, results):
        multi = MultiJob([j for j, _, _ in jobs])
        for (_, store, ks), res in zip(jobs, multi.split(results)):
            if store is None:
                self.small_parts = res[0]
            else:
                store.update(zip(ks, res))

    def run(self, key, fn, *args, **kw):
        jobs = self._jobs(key)
        if not jobs:
            out = fn(*args, **kw)
            return out if fn is mm_tn else out[0]
        res, jres = fn(*args, job=MultiJob([j for j, _, _ in jobs]), **kw)
        self._deliver(jobs, jres)
        return res

    def alone(self, key, name):
        jobs = self._jobs(key)
        if jobs:
            self._deliver(jobs, run_job(MultiJob([j for j, _, _ in jobs]), name))


def _local_step(x, target, layers, final_norm, n_ex, plan):
    T, D = x.shape
    L = len(layers)
    cst = _attn_consts(T // n_ex)
    ident = lambda j: j
    EV_A, EV_B = 3 * (D // 2), QW + 2 * KW
    OD_W = D // 2
    wt = plan.weight

    saved = []
    for l, W in enumerate(layers):
        s = dict(x0=x)
        x1, *s["gu1"] = plan.run(("ffn1_fwd", l), ffn_fwd, x, W["n1"], wt("f1_in_t", l), wt("f1_out", l))
        if l % 2 == 0:
            pa, pb, h = mm_nt(x1, wt("mx_in_t", l), [(0, EV_A), (EV_A, EV_B)], F32, True, W["nm"])
            qg = jnp.tile(W["q_norm"], N_Q_HEADS)[None]
            kg = jnp.tile(W["k_norm"], N_KV_HEADS)[None]
            mix = conv_fwd(pa, W["conv_w"], n_ex)
            qp, kp, vp = qkv_prep_fwd(pb, qg, kg, cst, n_ex)
            mix, op, lse = plan.run(("attn_fwd", l), attn_fwd, qp, kp, vp, cst["gather"], n_ex, mix)
            s.update(pa=pa, pb=pb, qg=qg, kg=kg, qp=qp, kp=kp, vp=vp, op=op, lse=lse)
        else:
            p, u, v, h = mm_nt(x1, wt("mx_in_t", l), [(0, OD_W), (OD_W, OD_W), (2 * OD_W, OD_W)], F32, True, W["nm"])
            scale = W["pool_scale"][None]
            sn = W["sgu_norm"][None]
            b_full = jnp.broadcast_to(W["sgu_b"][..., None], W["sgu_w"].shape)
            mix = sgu_fwd(u, v, sn, W["sgu_w"], b_full, pool_fwd(p, W["pool_w"], scale, n_ex))
            s.update(p=p, u=u, v=v, scale=scale, sn=sn, b_full=b_full)
        x2 = mm_nn([(mix, 0, D, 0)], wt("mx_out", l), residual=x1)
        x3, *s["gu2"] = plan.run(("ffn2_fwd", l), ffn_fwd, x2, W["n2"], wt("f2_in_t", l), wt("f2_out", l))
        s.update(x1=x1, x2=x2, h=h, mix=mix)
        saved.append(s)
        x = x3

    loss, dx, d_final = loss_head(x, final_norm, target)

    small = [None] * L

    def ffn_back(which, l, dout, xin, gain, gu_xn, sm, sm_key):
        w_in, w_out = wt(which + "_in_t", l), wt(which + "_out", l)
        F = w_out.shape[0]
        nc = F // FFN_TN
        gu, xn = gu_xn
        dxi, sm[sm_key], a, dgu, dob = plan.run((which + "_bwd", l), ffn_bwd_x, dout, xin, gain, gu, w_in, w_out)
        if which == "f1" and l == 0:
            plan.small_ready(small, d_final)
        plan.grad(which + "_in_t", l, plan.run(
            (which + "_in_grad", l), mm_tn, dgu, xn, 2 * F, lambda k, c: k * nc + c, grid=(2, nc),
            col_block=lambda k, c: 2 * c + k))
        plan.grad(which + "_out", l, plan.run((which + "_out_grad", l), mm_tn, a, dob, F, ident))
        keys = [(which + "_in_t", l), (which + "_out", l)]
        if which == "f1" and l == 0:
            plan.scatter(keys)
            return dxi
        if which == "f1":
            keys += [("mx_out", l), ("mx_in_t", l)]
        return plan.scatter(keys, dxi)

    for l in reversed(range(L)):
        s, W = saved[l], layers[l]
        sm = small[l] = {}
        dx = ffn_back("f2", l, dx, s["x2"], W["n2"], s["gu2"], sm, "n2")
        dmix, dxb = mm_nt(dx, wt("mx_out", l), [(0, D)], BF16, emit_a_bf16=True)
        plan.grad("mx_out", l, mm_tn(s["mix"], dxb, D, ident))
        if l % 2 == 0:
            d_proj, sm["conv_w"] = conv_bwd(s["pa"], W["conv_w"], n_ex, dmix, EV_A + EV_B)
            dq, dkp, dvp = plan.run(("attn_bwd", l), attn_bwd, s["qp"], s["kp"], s["vp"], s["op"], s["lse"], dmix, cst, n_ex)
            d_proj, dqg, dkg = qkv_prep_bwd(s["pb"], s["qg"], s["kg"], cst, n_ex, dq, dkp, dvp, d_proj)
            d_pieces = [(d_proj, 0, EV_A + EV_B, 0)]
            plan.grad("mx_in_t", l, mm_tn(d_proj, s["h"], EV_A + EV_B, ident))
            sm["q_norm"] = dqg.reshape(N_Q_HEADS, HEAD_DIM).sum(0)
            sm["k_norm"] = dkg.reshape(N_KV_HEADS, HEAD_DIM).sum(0)
        else:
            d_proj, d_sn, sm["sgu_w"], d_sb = sgu_bwd(s["u"], s["v"], s["sn"], W["sgu_w"], s["b_full"], dmix)
            d_proj, sm["pool_w"], d_ps = pool_bwd(s["p"], W["pool_w"], s["scale"], n_ex, dmix, d_proj)
            d_pieces = [(d_proj, 0, OD_W, 1), (d_proj, 1, OD_W, 2), (d_proj, 2, OD_W, 0)]
            nb = OD_W // MM_TC
            plan.grad("mx_in_t", l, mm_tn(d_proj, s["h"], 3 * OD_W,
                                          lambda jj: jnp.where(jj < 2 * nb, jj + nb, jj - 2 * nb)))
            sm["pool_scale"], sm["sgu_norm"], sm["sgu_b"] = d_ps[0], d_sn[0], d_sb.sum(-1)
        if l == 0:
            dx = plan.scatter([("mx_out", l), ("mx_in_t", l)], dx)
        dx, sm["nm"] = mm_nn(d_pieces, wt("mx_in_t", l), norm_bwd=(s["x1"], W["nm"], dx))
        dx = ffn_back("f1", l, dx, s["x0"], W["n1"], s["gu1"], sm, "n1")
    return loss, dx


def all_gather(srcs):
    return run_job(GatherJob(srcs), "all_gather")


HBM_SPEC = pl.BlockSpec(memory_space=pltpu.HBM)
SEM_SPEC = pl.BlockSpec(memory_space=pltpu.SEMAPHORE)
SPLIT_COPY = pltpu.CompilerParams(has_side_effects=pltpu.SideEffectType.DATAFLOW_SIDE_EFFECTING)


def _scatter_copies(srcs, lands, send, recv, dims):
    x, y, c, me = _my_place()
    for t, (r, _) in enumerate(dims):
        for k in range(1, N_DEV):
            peer, pidx = _peer(x, y, c, k)
            rows = srcs[t].at[pl.ds(pl.multiple_of(pidx * r, 8), r), :]
            yield (_remote(rows, lands[t].at[me], send, recv, N_DEV * t + k, peer),
                   _remote(rows, lands[t].at[pidx], send, recv, N_DEV * t + k, peer))


def scatter_start(grads, tag, carry=None):
    n = len(grads)
    dims = [(g.shape[0] // N_DEV, g.shape[1]) for g in grads]
    passed = list(grads) + [lax.empty((N_DEV, r, cc), g.dtype) for g, (r, cc) in zip(grads, dims)]
    passed += [] if carry is None else [carry]
    m = len(passed)

    def body(*refs):
        srcs, lands, send, recv, token = refs[:n], refs[n:2 * n], refs[m], refs[m + 1], refs[-1]
        for mine, _ in _scatter_copies(srcs, lands, send, recv, dims):
            mine.start()
        token[...] = jnp.zeros_like(token)

    res = pl.pallas_call(
        body, name="scatter_start_" + tag,
        out_shape=(pltpu.SemaphoreType.DMA((N_DEV * n,)), pltpu.SemaphoreType.DMA((N_DEV * n,)),
                   *[pltpu.HBM(a.shape, a.dtype) for a in passed], jax.ShapeDtypeStruct((8, LANES), F32)),
        in_specs=[HBM_SPEC] * m,
        out_specs=(SEM_SPEC, SEM_SPEC, *([HBM_SPEC] * m), pl.BlockSpec(memory_space=pltpu.VMEM)),
        input_output_aliases={i: 2 + i for i in range(m)}, compiler_params=SPLIT_COPY,
    )(*[pltpu.with_memory_space_constraint(a, pltpu.HBM) for a in passed])
    return res[0], res[1], res[2:2 + n], res[2 + n:2 + 2 * n], (None if carry is None else res[2 + 2 * n]), res[-1]


def scatter_wait(send, recv, grads, zones, after, tag):
    n = len(grads)
    dims = [(g.shape[0] // N_DEV, g.shape[1]) for g in grads]

    def body(*refs):
        srcs, lands, send_ref, recv_ref = refs[:n], refs[n:2 * n], refs[2 * n], refs[2 * n + 1]
        for mine, theirs in _scatter_copies(srcs, lands, send_ref, recv_ref, dims):
            mine.wait_send()
            theirs.wait_recv()

    res = pl.pallas_call(
        body, name="scatter_wait_" + tag,
        out_shape=(*[pltpu.HBM(g.shape, g.dtype) for g in grads], *[pltpu.HBM(z.shape, z.dtype) for z in zones]),
        in_specs=[HBM_SPEC] * (2 * n) + [SEM_SPEC, SEM_SPEC, ANY], out_specs=[HBM_SPEC] * (2 * n),
        input_output_aliases={i: i for i in range(2 * n)}, compiler_params=SPLIT_COPY,
    )(*grads, *zones, send, recv, after)
    return res[:n], res[n:]


def cast_shards(w):
    L, A, B = w.shape

    def body(w_ref, o_ref):
        o_ref[...] = w_ref[...].astype(BF16)

    return pl.pallas_call(
        body, name="cast_shards", grid=(L,),
        in_specs=[pl.BlockSpec((None, A, B), lambda l: (l, 0, 0))],
        out_specs=pl.BlockSpec((None, A, B), lambda l: (l, 0, 0)),
        out_shape=jax.ShapeDtypeStruct((L, A, B), BF16), compiler_params=_cp("parallel"),
    )(w)


ADAM_TC = 256


def adamw(parts, w, m, v, l, prev=None, own=None, after=None):
    P, R, C = parts.shape
    tc = _tile(C, ADAM_TC)
    c1, c2 = 1.0 - ADAM_B1 ** ADAM_STEP, 1.0 - ADAM_B2 ** ADAM_STEP
    prev = list(prev) if prev is not None else []

    def body(*refs):
        me_ref, refs = (refs[0], refs[1:]) if own is not None else (None, refs)
        p_ref, w_ref, m_ref, v_ref = refs[:4]
        g_ref, d_ref, mo_ref, vo_ref = refs[-4:]
        if own is None:
            term = lambda s: p_ref[s].astype(F32)
        else:
            term = lambda s: jnp.where(me_ref[0] == s, refs[4][...], p_ref[s]).astype(F32)
        g = term(0)
        for s in range(1, P):
            g = g + term(s)
        m1 = ADAM_B1 * m_ref[...] + (1.0 - ADAM_B1) * g
        v1 = ADAM_B2 * v_ref[...] + (1.0 - ADAM_B2) * (g * g)
        g_ref[...] = g
        mo_ref[...] = m1
        vo_ref[...] = v1
        d_ref[...] = -ADAM_LR * ((m1 / c1) / (jnp.sqrt(v1 / c2) + ADAM_EPS) + ADAM_WD * w_ref[...])

    wspec = pl.BlockSpec((None, R, tc), lambda i, *_: (l, 0, i))
    pspec = pl.BlockSpec((P, R, tc), lambda i, *_: (0, 0, i))
    extra = prev + ([] if after is None else [after])
    out_shape = [jax.ShapeDtypeStruct(w.shape, F32)] * 4
    if own is None:
        return pl.pallas_call(
            body, name="adamw", grid=(C // tc,), in_specs=[pspec, wspec, wspec, wspec] + [ANY] * len(extra),
            out_specs=[wspec] * 4, out_shape=out_shape,
            input_output_aliases={4 + i: i for i in range(len(prev))}, compiler_params=_cp("parallel"),
        )(parts, w, m, v, *extra)
    own_sums, me = own
    ospec = pl.BlockSpec((None, R, tc), lambda i, me_ref: (me_ref[0], 0, i))
    return pl.pallas_call(
        body, name="adamw_own", out_shape=out_shape,
        grid_spec=pltpu.PrefetchScalarGridSpec(
            num_scalar_prefetch=1, grid=(C // tc,),
            in_specs=[pspec, wspec, wspec, wspec, ospec] + [ANY] * len(extra), out_specs=[wspec] * 4),
        input_output_aliases={6 + i: i for i in range(len(prev))}, compiler_params=_cp("parallel"),
    )(me, parts, w, m, v, own_sums, *extra)


_WEIGHTS = ['ffn1_norm', 'ffn1_w_in', 'ffn1_w_out', 'mix_norm', 'ffn2_norm', 'ffn2_w_in', 'ffn2_w_out', 'ev_w_in',
            'ev_conv_w', 'ev_q_norm', 'ev_k_norm', 'ev_w_out', 'od_w_in', 'od_pool_w', 'od_pool_scale', 'od_sgu_norm',
            'od_sgu_w', 'od_sgu_b', 'od_w_out', 'final_norm']
_BIG = dict(ffn1_w_in=True, ffn1_w_out=False, ffn2_w_in=True, ffn2_w_out=False,
            ev_w_in=True, ev_w_out=False, od_w_in=True, od_w_out=False)
_SMALL_SHARDED = ['ev_conv_w', 'od_pool_scale', 'od_sgu_norm']
_SMALL = [n for n in _WEIGHTS if n not in _BIG]
_PACK_ROWS = 8 * LANES


_KINDS = ("f1_in_t", "f1_out", "mx_in_t", "mx_out", "f2_in_t", "f2_out")
_CARRIER_US = dict(ffn1_fwd=105, ffn2_fwd=105, attn_fwd=105)
_GATHER_US_PER_ROW = 0.09
SMALL_CARRIER = ("f1_in_grad", 0)


def _schedule(L, rows):
    events = []
    for l in range(L):
        events += [("ffn1_fwd", l), ("mixer", l)] + ([("attn_fwd", l)] if l % 2 == 0 else []) + [("ffn2_fwd", l)]
    consumer = {"f1": "ffn1_fwd", "mx": "mixer", "f2": "ffn2_fwd"}
    queue = [(k, l) for l in range(L) for k in _KINDS]
    pos = {t: events.index((consumer[t[0][:2]], t[1])) for t in queue}
    gathers = {"first": [t for t in queue if pos[t] == 0]}
    queue = [t for t in queue if pos[t] > 0]
    carriers = [i for i, e in enumerate(events) if e[0] in _CARRIER_US]
    for i in carriers:
        budget, take = _CARRIER_US[events[i][0]], []
        later = [j for j in carriers if j > i]
        while queue:
            t = queue[0]
            cost = rows(*t) * _GATHER_US_PER_ROW
            forced = not any(j < pos[t] for j in later)
            if not forced and cost > budget:
                break
            take.append(queue.pop(0))
            budget -= cost
        if take:
            gathers[events[i]] = take
    assert not queue
    return gathers


def _pack(arrs):
    flat = jnp.concatenate([a.reshape(-1) for a in arrs])
    pad = (-flat.shape[0]) % _PACK_ROWS
    return jnp.pad(flat, (0, pad)).reshape(-1, LANES)


def _unpack(buf, shapes):
    flat, out, off = buf.reshape(-1), [], 0
    for s in shapes:
        n = math.prod(s)
        out.append(flat[off:off + n].reshape(s))
        off += n
    return out


def _unshard_last(g, lead):
    nd = len(lead)
    return jnp.moveaxis(g, 0, nd).reshape(*lead, -1)


def kernel(x, ffn1_norm, ffn1_w_in, ffn1_w_out, mix_norm, ffn2_norm, ffn2_w_in, ffn2_w_out, ev_w_in, ev_conv_w, ev_q_norm, ev_k_norm, ev_w_out, od_w_in, od_pool_w, od_pool_scale, od_sgu_norm, od_sgu_w, od_sgu_b, od_w_out, final_norm, loss_target, m_ffn1_norm, m_ffn1_w_in, m_ffn1_w_out, m_mix_norm, m_ffn2_norm, m_ffn2_w_in, m_ffn2_w_out, m_ev_w_in, m_ev_conv_w, m_ev_q_norm, m_ev_k_norm, m_ev_w_out, m_od_w_in, m_od_pool_w, m_od_pool_scale, m_od_sgu_norm, m_od_sgu_w, m_od_sgu_b, m_od_w_out, m_final_norm, v_ffn1_norm, v_ffn1_w_in, v_ffn1_w_out, v_mix_norm, v_ffn2_norm, v_ffn2_w_in, v_ffn2_w_out, v_ev_w_in, v_ev_conv_w, v_ev_q_norm, v_ev_k_norm, v_ev_w_out, v_od_w_in, v_od_pool_w, v_od_pool_scale, v_od_sgu_norm, v_od_sgu_w, v_od_sgu_b, v_od_w_out, v_final_norm):
    w = dict(zip(_WEIGHTS, (ffn1_norm, ffn1_w_in, ffn1_w_out, mix_norm, ffn2_norm, ffn2_w_in, ffn2_w_out, ev_w_in, ev_conv_w, ev_q_norm, ev_k_norm, ev_w_out, od_w_in, od_pool_w, od_pool_scale, od_sgu_norm, od_sgu_w, od_sgu_b, od_w_out, final_norm)))
    m = dict(zip(_WEIGHTS, (m_ffn1_norm, m_ffn1_w_in, m_ffn1_w_out, m_mix_norm, m_ffn2_norm, m_ffn2_w_in, m_ffn2_w_out, m_ev_w_in, m_ev_conv_w, m_ev_q_norm, m_ev_k_norm, m_ev_w_out, m_od_w_in, m_od_pool_w, m_od_pool_scale, m_od_sgu_norm, m_od_sgu_w, m_od_sgu_b, m_od_w_out, m_final_norm)))
    v = dict(zip(_WEIGHTS, (v_ffn1_norm, v_ffn1_w_in, v_ffn1_w_out, v_mix_norm, v_ffn2_norm, v_ffn2_w_in, v_ffn2_w_out, v_ev_w_in, v_ev_conv_w, v_ev_q_norm, v_ev_k_norm, v_ev_w_out, v_od_w_in, v_od_pool_w, v_od_pool_scale, v_od_sgu_norm, v_od_sgu_w, v_od_sgu_b, v_od_w_out, v_final_norm)))
    n_ex, seq, D = x.shape
    T = n_ex * seq
    L = ffn1_norm.shape[0]
    me = 4 * lax.axis_index("x") + 2 * lax.axis_index("y") + lax.axis_index("c")

    sh_small = [w[n] for n in _SMALL_SHARDED]
    packed = all_gather([(_pack(sh_small), None)])[0].reshape(N_DEV, -1)
    full_small = {}
    off = 0
    for n, a in zip(_SMALL_SHARDED, sh_small):
        cnt = math.prod(a.shape)
        full_small[n] = _unshard_last(packed[:, off:off + cnt].reshape((N_DEV,) + a.shape), a.shape[:-1])
        off += cnt

    tr = lambda a: jnp.swapaxes(a, 1, 2)
    wmv = {n: tuple(tr(d[n]) if t else d[n] for d in (w, m, v)) for n, t in _BIG.items()}
    shards = {n: cast_shards(wmv[n][0]) for n in _BIG}

    def name_of(kind, l):
        mx = "ev" if l % 2 == 0 else "od"
        return {"f1_in_t": "ffn1_w_in", "f1_out": "ffn1_w_out", "f2_in_t": "ffn2_w_in", "f2_out": "ffn2_w_out",
                "mx_in_t": mx + "_w_in", "mx_out": mx + "_w_out"}[kind], (l // 2 if kind.startswith("mx") else l)

    def shard(kind, l):
        name, idx = name_of(kind, l)
        return shards[name], idx

    g_shapes = {}

    def pack_small(small, d_final):
        ev = [sm for l, sm in enumerate(small) if l % 2 == 0]
        od = [sm for l, sm in enumerate(small) if l % 2 == 1]
        st = lambda sms, k: jnp.stack([sm[k] for sm in sms])
        g_full = dict(ffn1_norm=st(small, "n1")[:, 0], mix_norm=st(small, "nm")[:, 0], ffn2_norm=st(small, "n2")[:, 0],
                      ev_conv_w=st(ev, "conv_w"), ev_q_norm=st(ev, "q_norm"), ev_k_norm=st(ev, "k_norm"),
                      od_pool_w=st(od, "pool_w"), od_pool_scale=st(od, "pool_scale"), od_sgu_norm=st(od, "sgu_norm"),
                      od_sgu_w=st(od, "sgu_w"), od_sgu_b=st(od, "sgu_b"), final_norm=d_final[0])
        g_shapes.update({n: g_full[n].shape for n in _SMALL})
        return _pack([g_full[n] for n in _SMALL])

    gathers = _schedule(L, lambda kind, l: shards[name_of(kind, l)[0]].shape[1])
    plan = Plan(shard, gathers, SMALL_CARRIER, pack_small)
    layers = []
    for l in range(L):
        j = l // 2
        W = dict(n1=ffn1_norm[l][None], nm=mix_norm[l][None], n2=ffn2_norm[l][None])
        if l % 2 == 0:
            W.update(conv_w=full_small["ev_conv_w"][j], q_norm=ev_q_norm[j], k_norm=ev_k_norm[j])
        else:
            W.update(pool_w=od_pool_w[j], pool_scale=full_small["od_pool_scale"][j], sgu_norm=full_small["od_sgu_norm"][j],
                     sgu_w=od_sgu_w[j], sgu_b=od_sgu_b[j])
        layers.append(W)

    plan.alone("first", "gather_first")
    loss, dx = _local_step(x.reshape(T, D), loss_target.reshape(T, D), layers, final_norm[None], n_ex, plan)

    out = {n: None for n in _BIG}
    me1 = me.astype(jnp.int32).reshape(1)

    def update(arrived):
        for (kind, l), (own, parts) in arrived.items():
            name, idx = name_of(kind, l)
            out[name] = adamw(parts, *wmv[name], idx, prev=out[name], own=(own.reshape(parts.shape), me1))

    *earlier, last = plan.started
    update(plan.collect(earlier, last[5]))

    g8 = plan.small_parts.reshape(N_DEV, -1)
    cols, off = [], 0
    for n in _SMALL:
        cnt = math.prod(g_shapes[n])
        g = g8[:, off:off + cnt].reshape((N_DEV,) + g_shapes[n])
        off += cnt
        if n in _SMALL_SHARDED:
            width = w[n].shape[-1]
            g = lax.dynamic_slice_in_dim(g, me * width, width, axis=g.ndim - 1)
        cols.append(g.reshape(N_DEV, -1))
    g8 = jnp.concatenate(cols, axis=1)
    g8 = jnp.pad(g8, ((0, 0), (0, (-g8.shape[1]) % _PACK_ROWS))).reshape(N_DEV, -1, LANES)
    pk = lambda d: _pack([d[n] for n in _SMALL])[None]
    small_out = adamw(g8, pk(w), pk(m), pk(v), 0)

    update(plan.collect([last], small_out[0]))
    out = {n: [tr(a) if _BIG[n] else a for a in res] for n, res in out.items()}

    shapes = [w[n].shape for n in _SMALL]
    for i in range(4):
        for n, a in zip(_SMALL, _unpack(small_out[i], shapes)):
            out.setdefault(n, [None] * 4)[i] = a

    total = lax.psum(loss[0, 0], ("x", "y", "c"))
    return (total, dx.reshape(n_ex, seq, D), *[out[n][0] for n in _WEIGHTS], *[out[n][1] for n in _WEIGHTS],
            *[out[n][2] for n in _WEIGHTS], *[out[n][3] for n in _WEIGHTS])
```

```python
import functools
import math

import jax
import jax.numpy as jnp
from jax import lax
from jax.experimental import pallas as pl
from jax.experimental.pallas import tpu as pltpu

F32, BF16 = jnp.float32, jnp.bfloat16
EPS = 1e-6
N_DEV = 8
V7X_VMEM_BYTES = 64 * 1024 * 1024
VMEM_LIMIT = V7X_VMEM_BYTES - 8 * 1024 * 1024
LANES = 128
HEAD_DIM = 64
N_Q_HEADS = 8
N_KV_HEADS = 2
Q_PER_KV = N_Q_HEADS // N_KV_HEADS
GRID_W = 64
ROPE_THETA = 10000.0
POOL_RADII = (1, 2, 4, 8)
SGU_CHUNK = 128
GROUP = 128
ADAM_LR, ADAM_B1, ADAM_B2, ADAM_EPS, ADAM_WD, ADAM_STEP = 0.001, 0.9, 0.999, 1e-08, 0.01, 10
MESH_ID = pl.DeviceIdType.MESH


def _cp(*sem):
    return pltpu.CompilerParams(dimension_semantics=sem, vmem_limit_bytes=VMEM_LIMIT)


def _dot(a, b, ca, cb):
    return lax.dot_general(a, b, (((ca,), (cb,)), ((), ())), preferred_element_type=F32)


def _nn(a, b):
    return _dot(a, b, 1, 0)


def _nt(a, b):
    return _dot(a, b, 1, 1)


def _tn(a, b):
    return _dot(a, b, 0, 0)


def _split_mm(x, m):
    hi = x.astype(BF16)
    lo = (x - hi.astype(F32)).astype(BF16)
    return _nn(hi, m) + _nn(lo, m)


def _tile(n, pref):
    t = min(n, pref)
    assert n % t == 0, (n, pref)
    return t


ANY = pl.BlockSpec(memory_space=pl.ANY)
JOB_MIDDLE, JOB_LATE = 0.55, 0.85


def _my_place():
    x, y, c = lax.axis_index("x"), lax.axis_index("y"), lax.axis_index("c")
    return x, y, c, 4 * x + 2 * y + c


def _peer(x, y, c, k):
    px = 1 - x if k & 4 else x
    py = 1 - y if k & 2 else y
    pc = 1 - c if k & 1 else c
    return (px, py, pc), 4 * px + 2 * py + pc


def _remote(src, dst, send_sems, recv_sems, i, peer):
    return pltpu.make_async_remote_copy(src_ref=src, dst_ref=dst, send_sem=send_sems.at[i], recv_sem=recv_sems.at[i],
                                        device_id=peer, device_id_type=MESH_ID)


class GatherJob:
    def __init__(self, srcs):
        self.srcs = srcs
        self.args = [a for a, _ in srcs]
        self.dims = [a.shape[-2:] for a, _ in srcs]
        n = self.n_in = self.n_out = len(srcs)
        self.out_shape = [jax.ShapeDtypeStruct((N_DEV * r, cc), a.dtype) for (a, _), (r, cc) in zip(srcs, self.dims)]
        self.scratch = [pltpu.SemaphoreType.DMA((N_DEV * n,)), pltpu.SemaphoreType.DMA((N_DEV * n,)),
                        pltpu.SemaphoreType.DMA((n,))]

    def _rows(self, outs, t, idx):
        r = self.dims[t][0]
        return outs[t].at[pl.ds(pl.multiple_of(idx * r, 8), r), :]

    def _local(self, ins, outs, loc, t, me):
        src = ins[t] if self.srcs[t][1] is None else ins[t].at[self.srcs[t][1]]
        return src, pltpu.make_async_copy(src, self._rows(outs, t, me), loc.at[t])

    def start(self, ins, outs, sems):
        send, recv, loc = sems
        x, y, c, me = _my_place()
        for t in range(self.n_in):
            src, local = self._local(ins, outs, loc, t, me)
            local.start()
            for k in (2, 4, 1):
                _remote(src, self._rows(outs, t, me), send, recv, N_DEV * t + k, _peer(x, y, c, k)[0]).start()

    def _copy(self, outs, sems, t, origin, i, to):
        x, y, c, _ = _my_place()
        blk = self._rows(outs, t, _peer(x, y, c, origin)[1])
        return _remote(blk, blk, sems[0], sems[1], N_DEV * t + i, _peer(x, y, c, to)[0])

    def middle(self, ins, outs, sems):
        c = _my_place()[2]

        def relay(t, got, to):
            self._copy(outs, sems, t, got, got, got).wait_recv()
            self._copy(outs, sems, t, got, 6, to).start()
            self._copy(outs, sems, t, to, to, to).wait_recv()

        for t in range(self.n_in):
            pl.when(c == 1)(functools.partial(relay, t, 2, 4))
            pl.when(c == 0)(functools.partial(relay, t, 4, 2))
            for k in (2, 4):
                self._copy(outs, sems, t, k, k + 1, 1).start()

    def late(self, ins, outs, sems):
        for t in range(self.n_in):
            self._copy(outs, sems, t, 6, 6, 6).wait_recv()
            self._copy(outs, sems, t, 6, 7, 1).start()

    def finish(self, ins, outs, sems):
        send, recv, loc = sems
        x, y, c, me = _my_place()
        for t in range(self.n_in):
            for k in range(1, N_DEV):
                peer, pidx = _peer(x, y, c, k)
                blk = self._rows(outs, t, pidx)
                if k % 2 == 1:
                    _remote(blk, blk, send, recv, N_DEV * t + k, peer).wait_recv()
                _remote(blk, blk, send, recv, N_DEV * t + k, peer).wait_send()
            self._local(ins, outs, loc, t, me)[1].wait()


def _call(body, name, grid, in_specs, args, out_specs, out_shape, scratch=(), sem=(), job=None, aliases=None):
    in_specs, out_specs, out_shape, scratch = list(in_specs), list(out_specs), list(out_shape), list(scratch)
    n_in, n_out, n_scr = len(args), len(out_shape), len(scratch)
    if job is None:
        res = pl.pallas_call(body, name=name, grid=grid, in_specs=in_specs, out_specs=out_specs, out_shape=out_shape,
                             scratch_shapes=scratch, input_output_aliases=aliases or {}, compiler_params=_cp(*sem))(*args)
        return res, None
    o0 = n_in + job.n_in
    s0 = o0 + n_out + job.n_out

    def carrier(*refs):
        jin, jout, jsem = refs[n_in:o0], refs[o0 + n_out:s0], refs[s0 + n_scr:]
        ids = [pl.program_id(a) for a in range(len(grid))]
        def at(step):
            idx = []
            for g in reversed(grid):
                idx.append(step % g)
                step //= g
            return functools.reduce(jnp.logical_and, [i == j for i, j in zip(ids, reversed(idx))])

        steps = math.prod(grid)
        if grid:
            pl.when(at(0))(lambda: job.start(jin, jout, jsem))
            pl.when(at(int(steps * JOB_MIDDLE)))(lambda: job.middle(jin, jout, jsem))
            pl.when(at(int(steps * JOB_LATE)))(lambda: job.late(jin, jout, jsem))
        else:
            job.start(jin, jout, jsem)
            job.middle(jin, jout, jsem)
            job.late(jin, jout, jsem)
        body(*refs[:n_in], *refs[o0:o0 + n_out], *refs[s0:s0 + n_scr])
        if grid:
            pl.when(at(steps - 1))(lambda: job.finish(jin, jout, jsem))
        else:
            job.finish(jin, jout, jsem)

    res = pl.pallas_call(
        carrier, name=name + "_comm", grid=grid, in_specs=in_specs + [ANY] * job.n_in,
        out_specs=out_specs + [ANY] * job.n_out, out_shape=out_shape + job.out_shape,
        scratch_shapes=scratch + job.scratch, input_output_aliases=aliases or {},
        compiler_params=_cp(*(["arbitrary"] * len(grid))))(*args, *job.args)
    return res[:n_out], res[n_out:]


def run_job(job, name):
    return _call(lambda: None, name, (), [], [], [], [], job=job)[1]


@jax.custom_vjp
def bmm(x, w):
    return _nn(x.astype(BF16), w.astype(BF16))


def _bmm_fwd(x, w):
    return bmm(x, w), (x, w)


def _bmm_bwd(res, g):
    x, w = res
    gb = g.astype(BF16)
    return _nt(gb, w.astype(BF16)), _tn(x.astype(BF16), gb)


bmm.defvjp(_bmm_fwd, _bmm_bwd)


def _shift_raw(x, d):
    n = x.shape[0]
    r = pltpu.roll(x, d % n, axis=0)
    row = lax.broadcasted_iota(jnp.int32, x.shape, 0)
    keep = (row >= d) if d > 0 else (row < n + d)
    return jnp.where(keep, r, 0.0)


def shift_rows(x, d):
    @jax.custom_vjp
    def f(v):
        return _shift_raw(v, d)

    f.defvjp(lambda v: (_shift_raw(v, d), None), lambda _, g: (_shift_raw(g, -d),))
    return f(x)


def _swap_raw(x):
    n = x.shape[1]
    nxt = pltpu.roll(x, n - 1, axis=1)
    prv = pltpu.roll(x, 1, axis=1)
    lane = lax.broadcasted_iota(jnp.int32, x.shape, 1)
    return jnp.where(lane % 2 == 0, nxt, prv)


@jax.custom_vjp
def swap_pairs(x):
    return _swap_raw(x)


swap_pairs.defvjp(lambda x: (_swap_raw(x), None), lambda _, g: (_swap_raw(g),))


@jax.custom_vjp
def group_mean(x, bd):
    return _split_mm(x, bd)


group_mean.defvjp(lambda x, bd: (_split_mm(x, bd), bd), lambda bd, g: (_split_mm(g, bd), jnp.zeros_like(bd)))


def _rope_norm(x, gain, cos, sgn, bd, scale):
    xn = x * lax.rsqrt(group_mean(x * x, bd) + EPS) * gain
    return (xn * cos + swap_pairs(xn) * sgn) * scale


def _conv_gate(gb, gc, hc, w):
    z = gc * hc
    c = shift_rows(z, 1) * w[0:1] + z * w[1:2] + shift_rows(z, -1) * w[2:3]
    return gb * c


def _window_sum(p, r):
    b = f = p
    k = 1
    while k < r:
        b = b + shift_rows(b, k)
        f = f + shift_rows(f, -k)
        k *= 2
    return b + f - p + shift_rows(p, r) + shift_rows(p, -r)


def _pool_mix(p, pool_w, scale):
    n = p.shape[0]
    t = lax.broadcasted_iota(jnp.int32, (n, 1), 0)
    outs = []
    for gi, r in enumerate(POOL_RADII):
        pg = p[:, gi * GROUP:(gi + 1) * GROUP]
        cnt = (jnp.minimum(t + r, n - 1) - jnp.maximum(t - r, 0) + 1).astype(F32)
        pooled = _window_sum(pg, r) / cnt - pg
        outs.append(bmm(pooled, pool_w[gi]))
    return jnp.concatenate(outs, axis=1) * scale


def _sgu(u, v, norm_g, w_s, b_full):
    ug = jax.nn.gelu(u)
    vg = jax.nn.gelu(v)
    vn = vg * lax.rsqrt(jnp.mean(vg * vg, axis=-1, keepdims=True) + EPS) * norm_g
    cols = []
    for g in range(w_s.shape[0]):
        rows = []
        for n in range(u.shape[0] // SGU_CHUNK):
            blk = vn[n * SGU_CHUNK:(n + 1) * SGU_CHUNK, g * GROUP:(g + 1) * GROUP]
            rows.append(bmm(w_s[g], blk) + b_full[g])
        cols.append(jnp.concatenate(rows, axis=0))
    return ug * jnp.concatenate(cols, axis=1)


def _rms_bwd_math(xv, gain, dy, dres):
    r = lax.rsqrt(jnp.mean(xv * xv, axis=-1, keepdims=True) + EPS)
    xh = xv * r
    dxh = dy * gain
    dx = dres + r * (dxh - xh * jnp.mean(dxh * xh, axis=-1, keepdims=True))
    return dx, jnp.sum(dy * xh, axis=0, keepdims=True)


FFN_TN = 256


def ffn_fwd(x, gain, wt_in, w_out, job=None):
    T, D = x.shape
    F = w_out.shape[0]
    tm, tn = _tile(T, 1024), FFN_TN
    nc = F // tn

    def body(x_ref, gn_ref, wg_ref, wu_ref, wo_ref, y_ref, gu_ref, xn_s, acc_s):
        c = pl.program_id(1)

        @pl.when(c == 0)
        def _():
            xv = x_ref[...]
            r = lax.rsqrt(jnp.mean(xv * xv, axis=-1, keepdims=True) + EPS)
            xn_s[...] = (xv * r * gn_ref[...]).astype(BF16)
            acc_s[...] = jnp.zeros_like(acc_s)

        xn = xn_s[...]
        g = _nt(xn, wg_ref[...])
        u = _nt(xn, wu_ref[...])
        gu_ref[:, :tn] = g.astype(BF16)
        gu_ref[:, tn:] = u.astype(BF16)
        a = (g * jax.nn.sigmoid(g) * u).astype(BF16)
        acc_s[...] += _nn(a, wo_ref[...])

        @pl.when(c == nc - 1)
        def _():
            y_ref[...] = x_ref[...] + 0.5 * acc_s[...]

    row = pl.BlockSpec((tm, D), lambda i, c: (i, 0))
    return _call(
        body, "ffn_fwd", (T // tm, nc),
        [row, pl.BlockSpec((1, D), lambda i, c: (0, 0)),
         pl.BlockSpec((tn, D), lambda i, c: (c, 0)),
         pl.BlockSpec((tn, D), lambda i, c: (c + nc, 0)),
         pl.BlockSpec((tn, D), lambda i, c: (c, 0))],
        [x, gain, wt_in, wt_in, w_out],
        [row, pl.BlockSpec((tm, 2 * tn), lambda i, c: (i, c)), row],
        [jax.ShapeDtypeStruct((T, D), F32), jax.ShapeDtypeStruct((T, 2 * F), BF16), jax.ShapeDtypeStruct((T, D), BF16)],
        [pltpu.VMEM((tm, D), F32)], ("parallel", "arbitrary"), job)


def ffn_bwd_x(dout, x, gain, gu, wt_in, w_out, job=None):
    T, D = x.shape
    F = w_out.shape[0]
    tm, tn = _tile(T, 1024), FFN_TN
    nc = F // tn

    def body(do_ref, x_ref, gn_ref, gu_ref, wg_ref, wu_ref, wo_ref,
             dx_ref, dgn_ref, a_ref, dgu_ref, dob_ref, acc_s):
        i, c = pl.program_id(0), pl.program_id(1)

        @pl.when(c == 0)
        def _():
            dob_ref[...] = (0.5 * do_ref[...]).astype(BF16)
            acc_s[...] = jnp.zeros_like(acc_s)

        da = jnp.concatenate([_nt(dob_ref[:tm // 2, :], wo_ref[...]), _nt(dob_ref[tm // 2:, :], wo_ref[...])], axis=0)
        g = gu_ref[:, :tn].astype(F32)
        u = gu_ref[:, tn:].astype(F32)
        sig = jax.nn.sigmoid(g)
        sl = g * sig
        a_ref[...] = (sl * u).astype(BF16)
        dg = (da * u * (sig * (1.0 + g * (1.0 - sig)))).astype(BF16)
        du = (da * sl).astype(BF16)
        dgu_ref[:, :tn] = dg
        dgu_ref[:, tn:] = du
        acc_s[...] += _nn(dg, wg_ref[...]) + _nn(du, wu_ref[...])

        @pl.when(c == nc - 1)
        def _():
            dx, dgn = _rms_bwd_math(x_ref[...], gn_ref[...], acc_s[...], do_ref[...])
            dx_ref[...] = dx

            @pl.when(i == 0)
            def _():
                dgn_ref[...] = jnp.zeros_like(dgn_ref)

            dgn_ref[...] += dgn

    row = pl.BlockSpec((tm, D), lambda i, c: (i, 0))
    return _call(
        body, "ffn_bwd_x", (T // tm, nc),
        [row, row, pl.BlockSpec((1, D), lambda i, c: (0, 0)),
         pl.BlockSpec((tm, 2 * tn), lambda i, c: (i, c)),
         pl.BlockSpec((tn, D), lambda i, c: (c, 0)),
         pl.BlockSpec((tn, D), lambda i, c: (c + nc, 0)),
         pl.BlockSpec((tn, D), lambda i, c: (c, 0))],
        [dout, x, gain, gu, wt_in, wt_in, w_out],
        [row, pl.BlockSpec((1, D), lambda i, c: (0, 0)),
         pl.BlockSpec((tm, tn), lambda i, c: (i, c)),
         pl.BlockSpec((tm, 2 * tn), lambda i, c: (i, c)), row],
        [jax.ShapeDtypeStruct((T, D), F32), jax.ShapeDtypeStruct((1, D), F32),
         jax.ShapeDtypeStruct((T, F), BF16), jax.ShapeDtypeStruct((T, 2 * F), BF16),
         jax.ShapeDtypeStruct((T, D), BF16)],
        [pltpu.VMEM((tm, D), F32)], ("arbitrary", "arbitrary"), job)


MM_TM = 512
MM_TC = 256


def mm_nt(a, wt, pieces, out_dtype, emit_a_bf16=False, norm_gain=None):
    T, K = a.shape
    tm = _tile(T, MM_TM)
    npc = len(pieces)
    n_lead = 1 if norm_gain is None else 2

    def body(*refs):
        a_ref, w_refs, o_refs = refs[0], refs[n_lead:n_lead + npc], refs[n_lead + npc:]
        av = a_ref[...]
        if norm_gain is not None:
            av = av * lax.rsqrt(jnp.mean(av * av, axis=-1, keepdims=True) + EPS) * refs[1][...]
        ab = av.astype(BF16)
        for w_ref, o_ref in zip(w_refs, o_refs[:npc]):
            o_ref[...] = _nt(ab, w_ref[...]).astype(o_ref.dtype)
        if emit_a_bf16:
            o_refs[npc][...] = ab

    in_specs = [pl.BlockSpec((tm, K), lambda i: (i, 0))]
    if norm_gain is not None:
        in_specs.append(pl.BlockSpec((1, K), lambda i: (0, 0)))
    out_specs, out_shape = [], []
    for r0, n in pieces:
        assert r0 % n == 0
        in_specs.append(pl.BlockSpec((n, K), functools.partial(lambda i, b: (b, 0), b=r0 // n)))
        out_specs.append(pl.BlockSpec((tm, n), lambda i: (i, 0)))
        out_shape.append(jax.ShapeDtypeStruct((T, n), out_dtype))
    if emit_a_bf16:
        out_specs.append(pl.BlockSpec((tm, K), lambda i: (i, 0)))
        out_shape.append(jax.ShapeDtypeStruct((T, K), BF16))
    return pl.pallas_call(
        body, name="mm_nt", grid=(T // tm,), in_specs=in_specs, out_specs=out_specs, out_shape=out_shape,
        compiler_params=_cp("parallel"),
    )(a, *([] if norm_gain is None else [norm_gain]), *([wt] * npc))


def mm_nn(pieces, w, residual=None, norm_bwd=None):
    T = pieces[0][0].shape[0]
    N = w.shape[1]
    tm = _tile(T, MM_TM)
    na = len(pieces)

    def body(*refs):
        a_refs, w_refs = refs[:na], refs[na:2 * na]
        acc = refs[2 * na][...] if residual is not None else None
        for a_ref, w_ref in zip(a_refs, w_refs):
            t = _nn(a_ref[...].astype(BF16), w_ref[...])
            acc = t if acc is None else acc + t
        if norm_bwd is None:
            refs[-1][...] = acc
            return
        x_ref, g_ref, dr_ref, dx_ref, dg_ref = refs[-5:]
        dx, dg = _rms_bwd_math(x_ref[...], g_ref[...], acc, dr_ref[...])
        dx_ref[...] = dx

        @pl.when(pl.program_id(0) == 0)
        def _():
            dg_ref[...] = jnp.zeros_like(dg_ref)

        dg_ref[...] += dg

    in_specs, w_specs = [], []
    for a, cb, k, rb in pieces:
        in_specs.append(pl.BlockSpec((tm, k), functools.partial(lambda i, b: (i, b), b=cb)))
        w_specs.append(pl.BlockSpec((k, N), functools.partial(lambda i, b: (b, 0), b=rb)))
    assert sum(k for _, _, k, _ in pieces) == w.shape[0]
    args = [a for a, _, _, _ in pieces] + [w] * na
    in_specs = in_specs + w_specs
    row = pl.BlockSpec((tm, N), lambda i: (i, 0))
    if residual is not None:
        in_specs.append(row)
        args.append(residual)
    if norm_bwd is None:
        return pl.pallas_call(
            body, name="mm_nn", grid=(T // tm,), in_specs=in_specs, out_specs=row,
            out_shape=jax.ShapeDtypeStruct((T, N), F32), compiler_params=_cp("parallel"),
        )(*args)
    vec = pl.BlockSpec((1, N), lambda i: (0, 0))
    return pl.pallas_call(
        body, name="mm_nn_norm_bwd", grid=(T // tm,), in_specs=in_specs + [row, vec, row], out_specs=[row, vec],
        out_shape=[jax.ShapeDtypeStruct((T, N), F32), jax.ShapeDtypeStruct((1, N), F32)],
        compiler_params=_cp("arbitrary"),
    )(*args, *norm_bwd)


def mm_tn(a, b, n_rows, row_block, prev=None, grid=None, col_block=None, job=None, tc=MM_TC):
    T, M = a.shape
    N = b.shape[1]
    assert M % tc == 0 and n_rows % tc == 0
    if grid is None:
        grid, col_block = (M // tc,), (lambda j: j)

    def body(*refs):
        a_ref, b_ref, o_ref = refs[0], refs[1], refs[-1]
        o_ref[...] = _tn(a_ref[...], b_ref[...]).astype(BF16)

    in_specs = [pl.BlockSpec((T, tc), lambda *g: (0, col_block(*g))), pl.BlockSpec((T, N), lambda *g: (0, 0))]
    args = [a, b]
    aliases = {}
    if prev is not None:
        in_specs.append(pl.BlockSpec(memory_space=pl.ANY))
        args.append(prev)
        aliases = {2: 0}
    res, jres = _call(body, "mm_tn", grid, in_specs, args, [pl.BlockSpec((tc, N), lambda *g: (row_block(*g), 0))],
                      [jax.ShapeDtypeStruct((n_rows, N), BF16)], (), ["parallel"] * len(grid), job, aliases)
    return res[0] if job is None else (res[0], jres)


def conv_fwd(proj_a, conv_w, n_ex):
    T, C3 = proj_a.shape
    C = C3 // 3
    S = T // n_ex

    def body(gb_ref, gc_ref, hc_ref, w_ref, o_ref):
        o_ref[...] = _conv_gate(gb_ref[...], gc_ref[...], hc_ref[...], w_ref[...]).astype(BF16)

    col = lambda k: pl.BlockSpec((S, C), functools.partial(lambda b, kk: (b, kk), kk=k))
    return pl.pallas_call(
        body, name="conv_fwd", grid=(n_ex,),
        in_specs=[col(0), col(1), col(2), pl.BlockSpec((3, C), lambda b: (0, 0))],
        out_specs=pl.BlockSpec((S, C), lambda b: (b, 0)),
        out_shape=jax.ShapeDtypeStruct((T, 2 * C), BF16), compiler_params=_cp("parallel"),
    )(proj_a, proj_a, proj_a, conv_w)


def conv_bwd(proj_a, conv_w, n_ex, dy, total_cols):
    T, C3 = proj_a.shape
    C = C3 // 3
    S = T // n_ex

    def body(gb_ref, gc_ref, hc_ref, w_ref, dy_ref, dp_ref, dw_ref):
        _, vjp = jax.vjp(_conv_gate, gb_ref[...], gc_ref[...], hc_ref[...], w_ref[...])
        dgb, dgc, dhc, dw = vjp(dy_ref[...].astype(F32))
        dp_ref[:, 0:C] = dgb.astype(BF16)
        dp_ref[:, C:2 * C] = dgc.astype(BF16)
        dp_ref[:, 2 * C:] = dhc.astype(BF16)

        @pl.when(pl.program_id(0) == 0)
        def _():
            dw_ref[...] = jnp.zeros_like(dw_ref)

        dw_ref[...] += dw

    col = lambda k: pl.BlockSpec((S, C), functools.partial(lambda b, kk: (b, kk), kk=k))
    return pl.pallas_call(
        body, name="conv_bwd", grid=(n_ex,),
        in_specs=[col(0), col(1), col(2), pl.BlockSpec((3, C), lambda b: (0, 0)),
                  pl.BlockSpec((S, C), lambda b: (b, 0))],
        out_specs=[pl.BlockSpec((S, C3), lambda b: (b, 0)), pl.BlockSpec((3, C), lambda b: (0, 0))],
        out_shape=[jax.ShapeDtypeStruct((T, total_cols), BF16), jax.ShapeDtypeStruct((3, C), F32)],
        compiler_params=_cp("arbitrary"),
    )(proj_a, proj_a, proj_a, conv_w, dy)


QW = N_Q_HEADS * HEAD_DIM
KW = N_KV_HEADS * HEAD_DIM
QP = N_Q_HEADS * LANES
KP = N_KV_HEADS * LANES


def _attn_consts(seq):
    rows = seq // GRID_W
    r_idx, c_idx = jnp.meshgrid(jnp.arange(rows), jnp.arange(GRID_W), indexing='ij')
    r_idx = r_idx.reshape(-1).astype(F32)
    c_idx = c_idx.reshape(-1).astype(F32)
    n_freq = HEAD_DIM // 4
    inv = ROPE_THETA ** (-jnp.arange(n_freq, dtype=F32) / n_freq)
    ang = jnp.concatenate([r_idx[:, None] * inv, c_idx[:, None] * inv], axis=-1)
    cos = jnp.repeat(jnp.cos(ang), 2, axis=1)
    sin = jnp.repeat(jnp.sin(ang), 2, axis=1)
    sgn = sin * jnp.tile(jnp.array([-1.0, 1.0], F32), HEAD_DIM // 2)
    cos = jnp.tile(cos, (1, N_Q_HEADS))
    sgn = jnp.tile(sgn, (1, N_Q_HEADS))
    lane = jnp.arange(QW)
    bd = jnp.where(lane[:, None] // HEAD_DIM == lane[None, :] // HEAD_DIM, 1.0 / HEAD_DIM, 0.0).astype(BF16)
    dst = (lane // HEAD_DIM) * LANES + lane % HEAD_DIM
    spread = (dst[:, None] == jnp.arange(QP)[None, :]).astype(BF16)
    return dict(cos=cos, sgn=sgn, bd=bd, spread=spread, gather=spread.T)


def qkv_prep_fwd(proj_b, qg, kg, cst, n_ex):
    T = proj_b.shape[0]
    S = T // n_ex
    tm = _tile(S, 512)
    nb = S // tm

    def body(p_ref, qg_ref, kg_ref, cos_ref, sgn_ref, bd_ref, sp_ref, q_ref, k_ref, v_ref):
        pv = p_ref[...]
        cos, sgn, bd, sp = cos_ref[...], sgn_ref[...], bd_ref[...], sp_ref[...]
        qr = _rope_norm(pv[:, :QW], qg_ref[...], cos, sgn, bd, HEAD_DIM ** -0.5)
        kr = _rope_norm(pv[:, QW:QW + KW], kg_ref[...], cos[:, :KW], sgn[:, :KW], bd[:KW, :KW], 1.0)
        q_ref[...] = _nn(qr.astype(BF16), sp).astype(BF16)
        k_ref[...] = _nn(kr.astype(BF16), sp[:KW, :KP]).astype(BF16)
        v_ref[...] = _nn(pv[:, QW + KW:].astype(BF16), sp[:KW, :KP]).astype(BF16)

    full = lambda a: pl.BlockSpec(a.shape, lambda i: (0,) * a.ndim)
    tab = pl.BlockSpec((tm, QW), lambda i: (i % nb, 0))
    return pl.pallas_call(
        body, name="qkv_prep_fwd", grid=(T // tm,),
        in_specs=[pl.BlockSpec((tm, QW + 2 * KW), lambda i: (i, 0)), full(qg), full(kg), tab, tab,
                  full(cst["bd"]), full(cst["spread"])],
        out_specs=[pl.BlockSpec((tm, QP), lambda i: (i, 0)), pl.BlockSpec((tm, KP), lambda i: (i, 0)),
                   pl.BlockSpec((tm, KP), lambda i: (i, 0))],
        out_shape=[jax.ShapeDtypeStruct((T, QP), BF16), jax.ShapeDtypeStruct((T, KP), BF16),
                   jax.ShapeDtypeStruct((T, KP), BF16)],
        compiler_params=_cp("parallel"),
    )(proj_b, qg, kg, cst["cos"], cst["sgn"], cst["bd"], cst["spread"])


def qkv_prep_bwd(proj_b, qg, kg, cst, n_ex, dq, dk_pad, dv_pad, d_proj):
    T = proj_b.shape[0]
    S = T // n_ex
    tm = _tile(S, 512)
    nb = S // tm

    def body(p_ref, qg_ref, kg_ref, cos_ref, sgn_ref, bd_ref, ga_ref, dq_ref, dk_ref, dv_ref, _kept,
             dp_ref, dqg_ref, dkg_ref):
        pv = p_ref[...]
        cos, sgn, bd, ga = cos_ref[...], sgn_ref[...], bd_ref[...], ga_ref[...]
        fq = lambda q, g: _rope_norm(q, g, cos, sgn, bd, HEAD_DIM ** -0.5)
        fk = lambda k, g: _rope_norm(k, g, cos[:, :KW], sgn[:, :KW], bd[:KW, :KW], 1.0)
        _, vq = jax.vjp(fq, pv[:, :QW], qg_ref[...])
        _, vk = jax.vjp(fk, pv[:, QW:QW + KW], kg_ref[...])
        dqp, dqg = vq(dq_ref[...])
        dkp, dkg = vk(_split_mm(dk_ref[...], ga[:KP, :KW]))
        dp_ref[:, :QW] = dqp.astype(BF16)
        dp_ref[:, QW:QW + KW] = dkp.astype(BF16)
        dp_ref[:, QW + KW:] = _split_mm(dv_ref[...], ga[:KP, :KW]).astype(BF16)

        @pl.when(pl.program_id(0) == 0)
        def _():
            dqg_ref[...] = jnp.zeros_like(dqg_ref)
            dkg_ref[...] = jnp.zeros_like(dkg_ref)

        dqg_ref[...] += dqg
        dkg_ref[...] += dkg

    full = lambda a: pl.BlockSpec(a.shape, lambda i: (0,) * a.ndim)
    tab = pl.BlockSpec((tm, QW), lambda i: (i % nb, 0))
    row = lambda n: pl.BlockSpec((tm, n), lambda i: (i, 0))
    wb = QW + 2 * KW
    assert d_proj.shape[1] % wb == 0
    last = d_proj.shape[1] // wb - 1
    return pl.pallas_call(
        body, name="qkv_prep_bwd", grid=(T // tm,),
        in_specs=[row(wb), full(qg), full(kg), tab, tab, full(cst["bd"]), full(cst["gather"]),
                  row(QW), row(KP), row(KP), ANY],
        out_specs=[pl.BlockSpec((tm, wb), lambda i: (i, last)), pl.BlockSpec((1, QW), lambda i: (0, 0)),
                   pl.BlockSpec((1, KW), lambda i: (0, 0))],
        out_shape=[jax.ShapeDtypeStruct(d_proj.shape, BF16), jax.ShapeDtypeStruct((1, QW), F32),
                   jax.ShapeDtypeStruct((1, KW), F32)],
        input_output_aliases={10: 0}, compiler_params=_cp("arbitrary"),
    )(proj_b, qg, kg, cst["cos"], cst["sgn"], cst["bd"], cst["gather"], dq, dk_pad, dv_pad, d_proj)


ATT_TQ = 256
ATT_TQ_FWD = 512


def attn_fwd(qp, kp, vp, gather, n_ex, mix, job=None):
    T = qp.shape[0]
    S = T // n_ex
    tq = _tile(S, ATT_TQ_FWD)
    nq = S // tq

    def body(q_ref, k_ref, v_ref, ga_ref, _kept, o_ref, op_ref, lse_ref):
        lane = lax.broadcasted_iota(jnp.int32, (tq, LANES), 1)
        lse_all = jnp.zeros((tq, LANES), F32)
        for h in range(N_Q_HEADS):
            kv = h // Q_PER_KV
            qh = q_ref[:, h * LANES:(h + 1) * LANES]
            s = _nt(qh, k_ref[:, kv * LANES:(kv + 1) * LANES])
            m = jnp.max(s, axis=-1, keepdims=True)
            p = jnp.exp(s - m)
            lsum = jnp.sum(p, axis=-1, keepdims=True)
            o = _nn(p.astype(BF16), v_ref[:, kv * LANES:(kv + 1) * LANES]) / lsum
            op_ref[:, h * LANES:(h + 1) * LANES] = o.astype(BF16)
            lse_all = jnp.where(lane == h, m + jnp.log(lsum), lse_all)
        lse_ref[...] = lse_all
        o_ref[...] = _nn(op_ref[...], ga_ref[...]).astype(BF16)

    blk = lambda n: pl.BlockSpec((tq, n), lambda b, i: (b * nq + i, 0))
    kvs = pl.BlockSpec((S, KP), lambda b, i: (b, 0))
    return _call(
        body, "attn_fwd", (n_ex, nq),
        [blk(QP), kvs, kvs, pl.BlockSpec(gather.shape, lambda b, i: (0, 0)), ANY], [qp, kp, vp, gather, mix],
        [pl.BlockSpec((tq, QW), lambda b, i: (b * nq + i, 1)), blk(QP), blk(LANES)],
        [jax.ShapeDtypeStruct(mix.shape, BF16), jax.ShapeDtypeStruct((T, QP), BF16),
         jax.ShapeDtypeStruct((T, LANES), F32)], (), ("parallel", "parallel"), job, {4: 0})


def attn_bwd(qp, kp, vp, op, lse, do, cst, n_ex, job=None):
    T = qp.shape[0]
    S = T // n_ex
    tq = _tile(S, ATT_TQ)
    nq = S // tq

    def body(q_ref, k_ref, v_ref, op_ref, lse_ref, do_ref, sp_ref, ga_ref, dq_ref, dk_ref, dv_ref, dqp_s):
        @pl.when(pl.program_id(1) == 0)
        def _():
            dk_ref[...] = jnp.zeros_like(dk_ref)
            dv_ref[...] = jnp.zeros_like(dv_ref)

        lane = lax.broadcasted_iota(jnp.int32, (tq, LANES), 1)
        dop = _nn(do_ref[...], sp_ref[...]).astype(BF16)
        lse_all = lse_ref[...]
        for h in range(N_Q_HEADS):
            kv = h // Q_PER_KV
            hs = slice(h * LANES, (h + 1) * LANES)
            ks = slice(kv * LANES, (kv + 1) * LANES)
            qh, kk, vv = q_ref[:, hs], k_ref[:, ks], v_ref[:, ks]
            doh = dop[:, hs]
            lse_h = jnp.sum(jnp.where(lane == h, lse_all, 0.0), axis=-1, keepdims=True)
            p = jnp.exp(_nt(qh, kk) - lse_h)
            dp = _nt(doh, vv)
            delta = jnp.sum(doh.astype(F32) * op_ref[:, hs].astype(F32), axis=-1, keepdims=True)
            ds = (p * (dp - delta)).astype(BF16)
            dqp_s[:, hs] = _nn(ds, kk)
            dk_ref[:, ks] += _tn(ds, qh)
            dv_ref[:, ks] += _tn(p.astype(BF16), doh)
        dq_ref[...] = _split_mm(dqp_s[...], ga_ref[...])

    blk = lambda n: pl.BlockSpec((tq, n), lambda b, i: (b * nq + i, 0))
    kvs = pl.BlockSpec((S, KP), lambda b, i: (b, 0))
    full = lambda a: pl.BlockSpec(a.shape, lambda b, i: (0, 0))
    return _call(
        body, "attn_bwd", (n_ex, nq),
        [blk(QP), kvs, kvs, blk(QP), blk(LANES), pl.BlockSpec((tq, QW), lambda b, i: (b * nq + i, 1)),
         full(cst["spread"]), full(cst["gather"])],
        [qp, kp, vp, op, lse, do, cst["spread"], cst["gather"]],
        [blk(QW), kvs, kvs],
        [jax.ShapeDtypeStruct((T, QW), F32), jax.ShapeDtypeStruct((T, KP), F32), jax.ShapeDtypeStruct((T, KP), F32)],
        [pltpu.VMEM((tq, QP), F32)], ("arbitrary", "arbitrary"), job)


def pool_fwd(p, pool_w, scale, n_ex):
    T, W = p.shape
    S = T // n_ex

    def body(p_ref, w_ref, s_ref, o_ref):
        o_ref[...] = _pool_mix(p_ref[...], w_ref[...], s_ref[...]).astype(BF16)

    return pl.pallas_call(
        body, name="pool_fwd", grid=(n_ex,),
        in_specs=[pl.BlockSpec((S, W), lambda b: (b, 0)),
                  pl.BlockSpec(pool_w.shape, lambda b: (0, 0, 0)),
                  pl.BlockSpec((1, W), lambda b: (0, 0))],
        out_specs=pl.BlockSpec((S, W), lambda b: (b, 0)),
        out_shape=jax.ShapeDtypeStruct((T, 2 * W), BF16), compiler_params=_cp("parallel"),
    )(p, pool_w, scale)


def pool_bwd(p, pool_w, scale, n_ex, dy, d_proj):
    T, W = p.shape
    S = T // n_ex
    last = d_proj.shape[1] // W - 1

    def body(p_ref, w_ref, s_ref, dy_ref, _kept, dp_ref, dw_ref, ds_ref):
        _, vjp = jax.vjp(_pool_mix, p_ref[...], w_ref[...], s_ref[...])
        dp, dw, ds = vjp(dy_ref[...].astype(F32))
        dp_ref[...] = dp.astype(BF16)

        @pl.when(pl.program_id(0) == 0)
        def _():
            dw_ref[...] = jnp.zeros_like(dw_ref)
            ds_ref[...] = jnp.zeros_like(ds_ref)

        dw_ref[...] += dw
        ds_ref[...] += ds

    wshape = pool_w.shape
    return pl.pallas_call(
        body, name="pool_bwd", grid=(n_ex,),
        in_specs=[pl.BlockSpec((S, W), lambda b: (b, 0)),
                  pl.BlockSpec(wshape, lambda b: (0, 0, 0)),
                  pl.BlockSpec((1, W), lambda b: (0, 0)), pl.BlockSpec((S, W), lambda b: (b, 0)), ANY],
        out_specs=[pl.BlockSpec((S, W), lambda b: (b, last)), pl.BlockSpec(wshape, lambda b: (0, 0, 0)),
                   pl.BlockSpec((1, W), lambda b: (0, 0))],
        out_shape=[jax.ShapeDtypeStruct(d_proj.shape, BF16), jax.ShapeDtypeStruct(wshape, F32),
                   jax.ShapeDtypeStruct((1, W), F32)],
        input_output_aliases={4: 0}, compiler_params=_cp("arbitrary"),
    )(p, pool_w, scale, dy, d_proj)


SGU_TS = 512


def sgu_fwd(u, v, norm_g, w_s, b_full, mix):
    T, W = u.shape
    ts = _tile(T, SGU_TS)

    def body(u_ref, v_ref, g_ref, w_ref, b_ref, _kept, o_ref):
        o_ref[...] = _sgu(u_ref[...], v_ref[...], g_ref[...], w_ref[...], b_ref[...]).astype(BF16)

    row = pl.BlockSpec((ts, W), lambda i: (i, 0))
    wsp = pl.BlockSpec(w_s.shape, lambda i: (0, 0, 0))
    return pl.pallas_call(
        body, name="sgu_fwd", grid=(T // ts,),
        in_specs=[row, row, pl.BlockSpec((1, W), lambda i: (0, 0)), wsp, wsp, ANY],
        out_specs=pl.BlockSpec((ts, W), lambda i: (i, 1)), out_shape=jax.ShapeDtypeStruct(mix.shape, BF16),
        input_output_aliases={5: 0}, compiler_params=_cp("parallel"),
    )(u, v, norm_g, w_s, b_full, mix)


def sgu_bwd(u, v, norm_g, w_s, b_full, dy):
    T, W = u.shape
    ts = _tile(T, SGU_TS)
    wshape = w_s.shape

    def body(u_ref, v_ref, g_ref, w_ref, b_ref, dy_ref, duv_ref, dg_ref, dw_ref, db_ref):
        _, vjp = jax.vjp(_sgu, u_ref[...], v_ref[...], g_ref[...], w_ref[...], b_ref[...])
        du, dv, dg, dw, db = vjp(dy_ref[...].astype(F32))
        duv_ref[:, :W] = du.astype(BF16)
        duv_ref[:, W:] = dv.astype(BF16)

        @pl.when(pl.program_id(0) == 0)
        def _():
            dg_ref[...] = jnp.zeros_like(dg_ref)
            dw_ref[...] = jnp.zeros_like(dw_ref)
            db_ref[...] = jnp.zeros_like(db_ref)

        dg_ref[...] += dg
        dw_ref[...] += dw
        db_ref[...] += db

    row = pl.BlockSpec((ts, W), lambda i: (i, 0))
    wsp = pl.BlockSpec(wshape, lambda i: (0, 0, 0))
    wout = pl.BlockSpec(wshape, lambda i: (0, 0, 0))
    vec = pl.BlockSpec((1, W), lambda i: (0, 0))
    return pl.pallas_call(
        body, name="sgu_bwd", grid=(T // ts,),
        in_specs=[row, row, vec, wsp, wsp, pl.BlockSpec((ts, W), lambda i: (i, 1))],
        out_specs=[pl.BlockSpec((ts, 2 * W), lambda i: (i, 0)), vec, wout, wout],
        out_shape=[jax.ShapeDtypeStruct((T, 3 * W), BF16),
                   jax.ShapeDtypeStruct((1, W), F32), jax.ShapeDtypeStruct(wshape, F32),
                   jax.ShapeDtypeStruct(wshape, F32)],
        compiler_params=_cp("arbitrary"),
    )(u, v, norm_g, w_s, b_full, dy)


def loss_head(x, gain, target):
    T, D = x.shape
    tm = _tile(T, 512)

    def body(x_ref, g_ref, t_ref, loss_ref, dx_ref, dg_ref):
        xv, g = x_ref[...], g_ref[...]
        r = lax.rsqrt(jnp.mean(xv * xv, axis=-1, keepdims=True) + EPS)
        err = xv * r * g - t_ref[...]
        part = 0.5 * jnp.sum(jnp.mean(err * err, axis=-1, keepdims=True), axis=0, keepdims=True)
        dx, dg = _rms_bwd_math(xv, g, err * (1.0 / D), jnp.zeros_like(xv))
        dx_ref[...] = dx

        @pl.when(pl.program_id(0) == 0)
        def _():
            loss_ref[...] = jnp.zeros_like(loss_ref)
            dg_ref[...] = jnp.zeros_like(dg_ref)

        loss_ref[...] += part
        dg_ref[...] += dg

    row = pl.BlockSpec((tm, D), lambda i: (i, 0))
    vec = pl.BlockSpec((1, D), lambda i: (0, 0))
    return pl.pallas_call(
        body, name="loss_head", grid=(T // tm,),
        in_specs=[row, vec, row], out_specs=[pl.BlockSpec((1, 1), lambda i: (0, 0)), row, vec],
        out_shape=[jax.ShapeDtypeStruct((1, 1), F32), jax.ShapeDtypeStruct((T, D), F32),
                   jax.ShapeDtypeStruct((1, D), F32)],
        compiler_params=_cp("arbitrary"),
    )(x, gain, target)


class MultiJob:
    def __init__(self, jobs):
        self.jobs = jobs
        self.args = [a for j in jobs for a in j.args]
        self.out_shape = [s for j in jobs for s in j.out_shape]
        self.scratch = [s for j in jobs for s in j.scratch]
        self.n_in, self.n_out = len(self.args), len(self.out_shape)

    def _each(self, ins, outs, sems):
        i = o = s = 0
        for j in self.jobs:
            yield j, ins[i:i + j.n_in], outs[o:o + j.n_out], sems[s:s + len(j.scratch)]
            i, o, s = i + j.n_in, o + j.n_out, s + len(j.scratch)

    def start(self, ins, outs, sems):
        for j, a, b, c in self._each(ins, outs, sems):
            j.start(a, b, c)

    def middle(self, ins, outs, sems):
        for j, a, b, c in self._each(ins, outs, sems):
            j.middle(a, b, c)

    def late(self, ins, outs, sems):
        for j, a, b, c in self._each(ins, outs, sems):
            j.late(a, b, c)

    def finish(self, ins, outs, sems):
        for j, a, b, c in self._each(ins, outs, sems):
            j.finish(a, b, c)

    def split(self, results):
        o = 0
        for j in self.jobs:
            yield results[o:o + j.n_out]
            o += j.n_out


class Plan:
    def __init__(self, shard, gathers, small_carrier=None, pack_small=None, exchange=True):
        self.shard, self.gathers, self.exchange = shard, gathers, exchange
        self.small_carrier, self.pack_small = small_carrier, pack_small
        self.weights, self.grads, self.started = {}, {}, []
        self.small_src = self.small_parts = None

    def scatter(self, keys, carry=None):
        if not self.exchange:
            return carry
        tag = "_".join(f"{kind}{l}" for kind, l in keys)
        send, recv, grads, zones, carry, token = scatter_start([self.grads[k] for k in keys], tag, carry)
        self.started.append((keys, send, recv, grads, zones, token, tag))
        return carry

    def collect(self, groups, after):
        got = {}
        for keys, send, recv, grads, zones, _, tag in groups:
            grads, zones = scatter_wait(send, recv, grads, zones, after, tag)
            got.update({k: (g, z) for k, g, z in zip(keys, grads, zones)})
        return got

    def weight(self, kind, l):
        return self.weights[(kind, l)]

    def grad(self, kind, l, g):
        self.grads[(kind, l)] = g

    def small_ready(self, small, d_final):
        if self.pack_small is not None:
            self.small_src = self.pack_small(small, d_final)

    def _jobs(self, key):
        jobs = []
        if key in self.gathers:
            ks = self.gathers[key]
            jobs.append((GatherJob([self.shard(*k) for k in ks]), self.weights, ks))
        if key == self.small_carrier and self.small_src is not None:
            jobs.append((GatherJob([(self.small_src, None)]), None, None))
        return jobs

    def _deliver(self, jobs, results):
        multi = MultiJob([j for j, _, _ in jobs])
        for (_, store, ks), res in zip(jobs, multi.split(results)):
            if store is None:
                self.small_parts = res[0]
            else:
                store.update(zip(ks, res))

    def run(self, key, fn, *args, **kw):
        jobs = self._jobs(key)
        if not jobs:
            out = fn(*args, **kw)
            return out if fn is mm_tn else out[0]
        res, jres = fn(*args, job=MultiJob([j for j, _, _ in jobs]), **kw)
        self._deliver(jobs, jres)
        return res

    def alone(self, key, name):
        jobs = self._jobs(key)
        if jobs:
            self._deliver(jobs, run_job(MultiJob([j for j, _, _ in jobs]), name))


def _local_step(x, target, layers, final_norm, n_ex, plan):
    T, D = x.shape
    L = len(layers)
    cst = _attn_consts(T // n_ex)
    ident = lambda j: j
    EV_A, EV_B = 3 * (D // 2), QW + 2 * KW
    OD_W = D // 2
    wt = plan.weight

    saved = []
    for l, W in enumerate(layers):
        s = dict(x0=x)
        x1, *s["gu1"] = plan.run(("ffn1_fwd", l), ffn_fwd, x, W["n1"], wt("f1_in_t", l), wt("f1_out", l))
        if l % 2 == 0:
            pa, pb, h = mm_nt(x1, wt("mx_in_t", l), [(0, EV_A), (EV_A, EV_B)], F32, True, W["nm"])
            qg = jnp.tile(W["q_norm"], N_Q_HEADS)[None]
            kg = jnp.tile(W["k_norm"], N_KV_HEADS)[None]
            mix = conv_fwd(pa, W["conv_w"], n_ex)
            qp, kp, vp = qkv_prep_fwd(pb, qg, kg, cst, n_ex)
            mix, op, lse = plan.run(("attn_fwd", l), attn_fwd, qp, kp, vp, cst["gather"], n_ex, mix)
            s.update(pa=pa, pb=pb, qg=qg, kg=kg, qp=qp, kp=kp, vp=vp, op=op, lse=lse)
        else:
            p, u, v, h = mm_nt(x1, wt("mx_in_t", l), [(0, OD_W), (OD_W, OD_W), (2 * OD_W, OD_W)], F32, True, W["nm"])
            scale = W["pool_scale"][None]
            sn = W["sgu_norm"][None]
            b_full = jnp.broadcast_to(W["sgu_b"][..., None], W["sgu_w"].shape)
            mix = sgu_fwd(u, v, sn, W["sgu_w"], b_full, pool_fwd(p, W["pool_w"], scale, n_ex))
            s.update(p=p, u=u, v=v, scale=scale, sn=sn, b_full=b_full)
        x2 = mm_nn([(mix, 0, D, 0)], wt("mx_out", l), residual=x1)
        x3, *s["gu2"] = plan.run(("ffn2_fwd", l), ffn_fwd, x2, W["n2"], wt("f2_in_t", l), wt("f2_out", l))
        s.update(x1=x1, x2=x2, h=h, mix=mix)
        saved.append(s)
        x = x3

    loss, dx, d_final = loss_head(x, final_norm, target)

    small = [None] * L

    def ffn_back(which, l, dout, xin, gain, gu_xn, sm, sm_key):
        w_in, w_out = wt(which + "_in_t", l), wt(which + "_out", l)
        F = w_out.shape[0]
        nc = F // FFN_TN
        gu, xn = gu_xn
        dxi, sm[sm_key], a, dgu, dob = plan.run((which + "_bwd", l), ffn_bwd_x, dout, xin, gain, gu, w_in, w_out)
        if which == "f1" and l == 0:
            plan.small_ready(small, d_final)
        plan.grad(which + "_in_t", l, plan.run(
            (which + "_in_grad", l), mm_tn, dgu, xn, 2 * F, lambda k, c: k * nc + c, grid=(2, nc),
            col_block=lambda k, c: 2 * c + k))
        plan.grad(which + "_out", l, plan.run((which + "_out_grad", l), mm_tn, a, dob, F, ident, tc=F // 2))
        keys = [(which + "_in_t", l), (which + "_out", l)]
        if which == "f1" and l == 0:
            plan.scatter(keys)
            return dxi
        if which == "f1":
            keys += [("mx_out", l), ("mx_in_t", l)]
        return plan.scatter(keys, dxi)

    for l in reversed(range(L)):
        s, W = saved[l], layers[l]
        sm = small[l] = {}
        dx = ffn_back("f2", l, dx, s["x2"], W["n2"], s["gu2"], sm, "n2")
        dmix, dxb = mm_nt(dx, wt("mx_out", l), [(0, D)], BF16, emit_a_bf16=True)
        plan.grad("mx_out", l, mm_tn(s["mix"], dxb, D, ident, tc=D // 2))
        if l % 2 == 0:
            d_proj, sm["conv_w"] = conv_bwd(s["pa"], W["conv_w"], n_ex, dmix, EV_A + EV_B)
            dq, dkp, dvp = plan.run(("attn_bwd", l), attn_bwd, s["qp"], s["kp"], s["vp"], s["op"], s["lse"], dmix, cst, n_ex)
            d_proj, dqg, dkg = qkv_prep_bwd(s["pb"], s["qg"], s["kg"], cst, n_ex, dq, dkp, dvp, d_proj)
            d_pieces = [(d_proj, 0, EV_A + EV_B, 0)]
            plan.grad("mx_in_t", l, mm_tn(d_proj, s["h"], EV_A + EV_B, ident, tc=(EV_A + EV_B) // 3))
            sm["q_norm"] = dqg.reshape(N_Q_HEADS, HEAD_DIM).sum(0)
            sm["k_norm"] = dkg.reshape(N_KV_HEADS, HEAD_DIM).sum(0)
        else:
            d_proj, d_sn, sm["sgu_w"], d_sb = sgu_bwd(s["u"], s["v"], s["sn"], W["sgu_w"], s["b_full"], dmix)
            d_proj, sm["pool_w"], d_ps = pool_bwd(s["p"], W["pool_w"], s["scale"], n_ex, dmix, d_proj)
            d_pieces = [(d_proj, 0, OD_W, 1), (d_proj, 1, OD_W, 2), (d_proj, 2, OD_W, 0)]
            plan.grad("mx_in_t", l, mm_tn(d_proj, s["h"], 3 * OD_W, lambda jj: jnp.where(jj < 2, jj + 1, jj - 2), tc=OD_W))
            sm["pool_scale"], sm["sgu_norm"], sm["sgu_b"] = d_ps[0], d_sn[0], d_sb.sum(-1)
        if l == 0:
            dx = plan.scatter([("mx_out", l), ("mx_in_t", l)], dx)
        dx, sm["nm"] = mm_nn(d_pieces, wt("mx_in_t", l), norm_bwd=(s["x1"], W["nm"], dx))
        dx = ffn_back("f1", l, dx, s["x0"], W["n1"], s["gu1"], sm, "n1")
    return loss, dx


def all_gather(srcs):
    return run_job(GatherJob(srcs), "all_gather")


HBM_SPEC = pl.BlockSpec(memory_space=pltpu.HBM)
SEM_SPEC = pl.BlockSpec(memory_space=pltpu.SEMAPHORE)
SPLIT_COPY = pltpu.CompilerParams(has_side_effects=pltpu.SideEffectType.DATAFLOW_SIDE_EFFECTING)


def _scatter_copies(srcs, lands, send, recv, dims):
    x, y, c, me = _my_place()
    for t, (r, _) in enumerate(dims):
        for k in range(1, N_DEV):
            peer, pidx = _peer(x, y, c, k)
            rows = srcs[t].at[pl.ds(pl.multiple_of(pidx * r, 8), r), :]
            yield (_remote(rows, lands[t].at[me], send, recv, N_DEV * t + k, peer),
                   _remote(rows, lands[t].at[pidx], send, recv, N_DEV * t + k, peer))


def scatter_start(grads, tag, carry=None):
    n = len(grads)
    dims = [(g.shape[0] // N_DEV, g.shape[1]) for g in grads]
    passed = list(grads) + [lax.empty((N_DEV, r, cc), g.dtype) for g, (r, cc) in zip(grads, dims)]
    passed += [] if carry is None else [carry]
    m = len(passed)

    def body(*refs):
        srcs, lands, send, recv, token = refs[:n], refs[n:2 * n], refs[m], refs[m + 1], refs[-1]
        for mine, _ in _scatter_copies(srcs, lands, send, recv, dims):
            mine.start()
        token[...] = jnp.zeros_like(token)

    res = pl.pallas_call(
        body, name="scatter_start_" + tag,
        out_shape=(pltpu.SemaphoreType.DMA((N_DEV * n,)), pltpu.SemaphoreType.DMA((N_DEV * n,)),
                   *[pltpu.HBM(a.shape, a.dtype) for a in passed], jax.ShapeDtypeStruct((8, LANES), F32)),
        in_specs=[HBM_SPEC] * m,
        out_specs=(SEM_SPEC, SEM_SPEC, *([HBM_SPEC] * m), pl.BlockSpec(memory_space=pltpu.VMEM)),
        input_output_aliases={i: 2 + i for i in range(m)}, compiler_params=SPLIT_COPY,
    )(*[pltpu.with_memory_space_constraint(a, pltpu.HBM) for a in passed])
    return res[0], res[1], res[2:2 + n], res[2 + n:2 + 2 * n], (None if carry is None else res[2 + 2 * n]), res[-1]


def scatter_wait(send, recv, grads, zones, after, tag):
    n = len(grads)
    dims = [(g.shape[0] // N_DEV, g.shape[1]) for g in grads]

    def body(*refs):
        srcs, lands, send_ref, recv_ref = refs[:n], refs[n:2 * n], refs[2 * n], refs[2 * n + 1]
        for mine, theirs in _scatter_copies(srcs, lands, send_ref, recv_ref, dims):
            mine.wait_send()
            theirs.wait_recv()

    res = pl.pallas_call(
        body, name="scatter_wait_" + tag,
        out_shape=(*[pltpu.HBM(g.shape, g.dtype) for g in grads], *[pltpu.HBM(z.shape, z.dtype) for z in zones]),
        in_specs=[HBM_SPEC] * (2 * n) + [SEM_SPEC, SEM_SPEC, ANY], out_specs=[HBM_SPEC] * (2 * n),
        input_output_aliases={i: i for i in range(2 * n)}, compiler_params=SPLIT_COPY,
    )(*grads, *zones, send, recv, after)
    return res[:n], res[n:]


def cast_shards(w):
    L, A, B = w.shape

    def body(w_ref, o_ref):
        o_ref[...] = w_ref[...].astype(BF16)

    return pl.pallas_call(
        body, name="cast_shards", grid=(L,),
        in_specs=[pl.BlockSpec((None, A, B), lambda l: (l, 0, 0))],
        out_specs=pl.BlockSpec((None, A, B), lambda l: (l, 0, 0)),
        out_shape=jax.ShapeDtypeStruct((L, A, B), BF16), compiler_params=_cp("parallel"),
    )(w)


ADAM_TC = 256


def adamw(parts, w, m, v, l, prev=None, own=None, after=None):
    P, R, C = parts.shape
    tc = _tile(C, ADAM_TC)
    c1, c2 = 1.0 - ADAM_B1 ** ADAM_STEP, 1.0 - ADAM_B2 ** ADAM_STEP
    prev = list(prev) if prev is not None else []

    def body(*refs):
        me_ref, refs = (refs[0], refs[1:]) if own is not None else (None, refs)
        p_ref, w_ref, m_ref, v_ref = refs[:4]
        g_ref, d_ref, mo_ref, vo_ref = refs[-4:]
        if own is None:
            term = lambda s: p_ref[s].astype(F32)
        else:
            term = lambda s: jnp.where(me_ref[0] == s, refs[4][...], p_ref[s]).astype(F32)
        g = term(0)
        for s in range(1, P):
            g = g + term(s)
        m1 = ADAM_B1 * m_ref[...] + (1.0 - ADAM_B1) * g
        v1 = ADAM_B2 * v_ref[...] + (1.0 - ADAM_B2) * (g * g)
        g_ref[...] = g
        mo_ref[...] = m1
        vo_ref[...] = v1
        d_ref[...] = -ADAM_LR * ((m1 / c1) / (jnp.sqrt(v1 / c2) + ADAM_EPS) + ADAM_WD * w_ref[...])

    wspec = pl.BlockSpec((None, R, tc), lambda i, *_: (l, 0, i))
    pspec = pl.BlockSpec((P, R, tc), lambda i, *_: (0, 0, i))
    extra = prev + ([] if after is None else [after])
    out_shape = [jax.ShapeDtypeStruct(w.shape, F32)] * 4
    if own is None:
        return pl.pallas_call(
            body, name="adamw", grid=(C // tc,), in_specs=[pspec, wspec, wspec, wspec] + [ANY] * len(extra),
            out_specs=[wspec] * 4, out_shape=out_shape,
            input_output_aliases={4 + i: i for i in range(len(prev))}, compiler_params=_cp("parallel"),
        )(parts, w, m, v, *extra)
    own_sums, me = own
    ospec = pl.BlockSpec((None, R, tc), lambda i, me_ref: (me_ref[0], 0, i))
    return pl.pallas_call(
        body, name="adamw_own", out_shape=out_shape,
        grid_spec=pltpu.PrefetchScalarGridSpec(
            num_scalar_prefetch=1, grid=(C // tc,),
            in_specs=[pspec, wspec, wspec, wspec, ospec] + [ANY] * len(extra), out_specs=[wspec] * 4),
        input_output_aliases={6 + i: i for i in range(len(prev))}, compiler_params=_cp("parallel"),
    )(me, parts, w, m, v, own_sums, *extra)


_WEIGHTS = ['ffn1_norm', 'ffn1_w_in', 'ffn1_w_out', 'mix_norm', 'ffn2_norm', 'ffn2_w_in', 'ffn2_w_out', 'ev_w_in',
            'ev_conv_w', 'ev_q_norm', 'ev_k_norm', 'ev_w_out', 'od_w_in', 'od_pool_w', 'od_pool_scale', 'od_sgu_norm',
            'od_sgu_w', 'od_sgu_b', 'od_w_out', 'final_norm']
_BIG = dict(ffn1_w_in=True, ffn1_w_out=False, ffn2_w_in=True, ffn2_w_out=False,
            ev_w_in=True, ev_w_out=False, od_w_in=True, od_w_out=False)
_SMALL_SHARDED = ['ev_conv_w', 'od_pool_scale', 'od_sgu_norm']
_SMALL = [n for n in _WEIGHTS if n not in _BIG]
_PACK_ROWS = 8 * LANES


_KINDS = ("f1_in_t", "f1_out", "mx_in_t", "mx_out", "f2_in_t", "f2_out")
_CARRIER_US = dict(ffn1_fwd=105, ffn2_fwd=105, attn_fwd=105)
_GATHER_US_PER_ROW = 0.09
SMALL_CARRIER = ("f1_in_grad", 0)


def _schedule(L, rows):
    events = []
    for l in range(L):
        events += [("ffn1_fwd", l), ("mixer", l)] + ([("attn_fwd", l)] if l % 2 == 0 else []) + [("ffn2_fwd", l)]
    consumer = {"f1": "ffn1_fwd", "mx": "mixer", "f2": "ffn2_fwd"}
    queue = [(k, l) for l in range(L) for k in _KINDS]
    pos = {t: events.index((consumer[t[0][:2]], t[1])) for t in queue}
    gathers = {"first": [t for t in queue if pos[t] == 0]}
    queue = [t for t in queue if pos[t] > 0]
    carriers = [i for i, e in enumerate(events) if e[0] in _CARRIER_US]
    for i in carriers:
        budget, take = _CARRIER_US[events[i][0]], []
        later = [j for j in carriers if j > i]
        while queue:
            t = queue[0]
            cost = rows(*t) * _GATHER_US_PER_ROW
            forced = not any(j < pos[t] for j in later)
            if not forced and cost > budget:
                break
            take.append(queue.pop(0))
            budget -= cost
        if take:
            gathers[events[i]] = take
    assert not queue
    return gathers


def _pack(arrs):
    flat = jnp.concatenate([a.reshape(-1) for a in arrs])
    pad = (-flat.shape[0]) % _PACK_ROWS
    return jnp.pad(flat, (0, pad)).reshape(-1, LANES)


def _unpack(buf, shapes):
    flat, out, off = buf.reshape(-1), [], 0
    for s in shapes:
        n = math.prod(s)
        out.append(flat[off:off + n].reshape(s))
        off += n
    return out


def _unshard_last(g, lead):
    nd = len(lead)
    return jnp.moveaxis(g, 0, nd).reshape(*lead, -1)


def kernel(x, ffn1_norm, ffn1_w_in, ffn1_w_out, mix_norm, ffn2_norm, ffn2_w_in, ffn2_w_out, ev_w_in, ev_conv_w, ev_q_norm, ev_k_norm, ev_w_out, od_w_in, od_pool_w, od_pool_scale, od_sgu_norm, od_sgu_w, od_sgu_b, od_w_out, final_norm, loss_target, m_ffn1_norm, m_ffn1_w_in, m_ffn1_w_out, m_mix_norm, m_ffn2_norm, m_ffn2_w_in, m_ffn2_w_out, m_ev_w_in, m_ev_conv_w, m_ev_q_norm, m_ev_k_norm, m_ev_w_out, m_od_w_in, m_od_pool_w, m_od_pool_scale, m_od_sgu_norm, m_od_sgu_w, m_od_sgu_b, m_od_w_out, m_final_norm, v_ffn1_norm, v_ffn1_w_in, v_ffn1_w_out, v_mix_norm, v_ffn2_norm, v_ffn2_w_in, v_ffn2_w_out, v_ev_w_in, v_ev_conv_w, v_ev_q_norm, v_ev_k_norm, v_ev_w_out, v_od_w_in, v_od_pool_w, v_od_pool_scale, v_od_sgu_norm, v_od_sgu_w, v_od_sgu_b, v_od_w_out, v_final_norm):
    w = dict(zip(_WEIGHTS, (ffn1_norm, ffn1_w_in, ffn1_w_out, mix_norm, ffn2_norm, ffn2_w_in, ffn2_w_out, ev_w_in, ev_conv_w, ev_q_norm, ev_k_norm, ev_w_out, od_w_in, od_pool_w, od_pool_scale, od_sgu_norm, od_sgu_w, od_sgu_b, od_w_out, final_norm)))
    m = dict(zip(_WEIGHTS, (m_ffn1_norm, m_ffn1_w_in, m_ffn1_w_out, m_mix_norm, m_ffn2_norm, m_ffn2_w_in, m_ffn2_w_out, m_ev_w_in, m_ev_conv_w, m_ev_q_norm, m_ev_k_norm, m_ev_w_out, m_od_w_in, m_od_pool_w, m_od_pool_scale, m_od_sgu_norm, m_od_sgu_w, m_od_sgu_b, m_od_w_out, m_final_norm)))
    v = dict(zip(_WEIGHTS, (v_ffn1_norm, v_ffn1_w_in, v_ffn1_w_out, v_mix_norm, v_ffn2_norm, v_ffn2_w_in, v_ffn2_w_out, v_ev_w_in, v_ev_conv_w, v_ev_q_norm, v_ev_k_norm, v_ev_w_out, v_od_w_in, v_od_pool_w, v_od_pool_scale, v_od_sgu_norm, v_od_sgu_w, v_od_sgu_b, v_od_w_out, v_final_norm)))
    n_ex, seq, D = x.shape
    T = n_ex * seq
    L = ffn1_norm.shape[0]
    me = 4 * lax.axis_index("x") + 2 * lax.axis_index("y") + lax.axis_index("c")

    sh_small = [w[n] for n in _SMALL_SHARDED]
    packed = all_gather([(_pack(sh_small), None)])[0].reshape(N_DEV, -1)
    full_small = {}
    off = 0
    for n, a in zip(_SMALL_SHARDED, sh_small):
        cnt = math.prod(a.shape)
        full_small[n] = _unshard_last(packed[:, off:off + cnt].reshape((N_DEV,) + a.shape), a.shape[:-1])
        off += cnt

    tr = lambda a: jnp.swapaxes(a, 1, 2)
    wmv = {n: tuple(tr(d[n]) if t else d[n] for d in (w, m, v)) for n, t in _BIG.items()}
    shards = {n: cast_shards(wmv[n][0]) for n in _BIG}

    def name_of(kind, l):
        mx = "ev" if l % 2 == 0 else "od"
        return {"f1_in_t": "ffn1_w_in", "f1_out": "ffn1_w_out", "f2_in_t": "ffn2_w_in", "f2_out": "ffn2_w_out",
                "mx_in_t": mx + "_w_in", "mx_out": mx + "_w_out"}[kind], (l // 2 if kind.startswith("mx") else l)

    def shard(kind, l):
        name, idx = name_of(kind, l)
        return shards[name], idx

    g_shapes = {}

    def pack_small(small, d_final):
        ev = [sm for l, sm in enumerate(small) if l % 2 == 0]
        od = [sm for l, sm in enumerate(small) if l % 2 == 1]
        st = lambda sms, k: jnp.stack([sm[k] for sm in sms])
        g_full = dict(ffn1_norm=st(small, "n1")[:, 0], mix_norm=st(small, "nm")[:, 0], ffn2_norm=st(small, "n2")[:, 0],
                      ev_conv_w=st(ev, "conv_w"), ev_q_norm=st(ev, "q_norm"), ev_k_norm=st(ev, "k_norm"),
                      od_pool_w=st(od, "pool_w"), od_pool_scale=st(od, "pool_scale"), od_sgu_norm=st(od, "sgu_norm"),
                      od_sgu_w=st(od, "sgu_w"), od_sgu_b=st(od, "sgu_b"), final_norm=d_final[0])
        g_shapes.update({n: g_full[n].shape for n in _SMALL})
        return _pack([g_full[n] for n in _SMALL])

    gathers = _schedule(L, lambda kind, l: shards[name_of(kind, l)[0]].shape[1])
    plan = Plan(shard, gathers, SMALL_CARRIER, pack_small)
    layers = []
    for l in range(L):
        j = l // 2
        W = dict(n1=ffn1_norm[l][None], nm=mix_norm[l][None], n2=ffn2_norm[l][None])
        if l % 2 == 0:
            W.update(conv_w=full_small["ev_conv_w"][j], q_norm=ev_q_norm[j], k_norm=ev_k_norm[j])
        else:
            W.update(pool_w=od_pool_w[j], pool_scale=full_small["od_pool_scale"][j], sgu_norm=full_small["od_sgu_norm"][j],
                     sgu_w=od_sgu_w[j], sgu_b=od_sgu_b[j])
        layers.append(W)

    plan.alone("first", "gather_first")
    loss, dx = _local_step(x.reshape(T, D), loss_target.reshape(T, D), layers, final_norm[None], n_ex, plan)

    out = {n: None for n in _BIG}
    me1 = me.astype(jnp.int32).reshape(1)

    def update(arrived):
        for (kind, l), (own, parts) in arrived.items():
            name, idx = name_of(kind, l)
            out[name] = adamw(parts, *wmv[name], idx, prev=out[name], own=(own.reshape(parts.shape), me1))

    *earlier, last = plan.started
    update(plan.collect(earlier, last[5]))

    g8 = plan.small_parts.reshape(N_DEV, -1)
    cols, off = [], 0
    for n in _SMALL:
        cnt = math.prod(g_shapes[n])
        g = g8[:, off:off + cnt].reshape((N_DEV,) + g_shapes[n])
        off += cnt
        if n in _SMALL_SHARDED:
            width = w[n].shape[-1]
            g = lax.dynamic_slice_in_dim(g, me * width, width, axis=g.ndim - 1)
        cols.append(g.reshape(N_DEV, -1))
    g8 = jnp.concatenate(cols, axis=1)
    g8 = jnp.pad(g8, ((0, 0), (0, (-g8.shape[1]) % _PACK_ROWS))).reshape(N_DEV, -1, LANES)
    pk = lambda d: _pack([d[n] for n in _SMALL])[None]
    small_out = adamw(g8, pk(w), pk(m), pk(v), 0)

    update(plan.collect([last], small_out[0]))
    out = {n: [tr(a) if _BIG[n] else a for a in res] for n, res in out.items()}

    shapes = [w[n].shape for n in _SMALL]
    for i in range(4):
        for n, a in zip(_SMALL, _unpack(small_out[i], shapes)):
            out.setdefault(n, [None] * 4)[i] = a

    total = lax.psum(loss[0, 0], ("x", "y", "c"))
    return (total, dx.reshape(n_ex, seq, D), *[out[n][0] for n in _WEIGHTS], *[out[n][1] for n in _WEIGHTS],
            *[out[n][2] for n in _WEIGHTS], *[out[n][3] for n in _WEIGHTS])
```
